```python
import math
import jax, jax.numpy as jnp
from jax import lax
import numpy as np

D_MODEL = 1024
BATCH = 8
SEQ = 4096
DEPTH = 2

HEAD_DIM = 64
N_MEM = 256
MEM_HEADS = 4
MEM_WIDTH = MEM_HEADS * HEAD_DIM
RWKV_HEADS = 12
RWKV_WIDTH = RWKV_HEADS * HEAD_DIM
DECAY_LORA = 64
AAA_LORA = 64
GATE_LORA = 128
SHIFT_WIDTH = 3 * RWKV_WIDTH + DECAY_LORA + AAA_LORA + GATE_LORA
A_IN_WIDTH = SHIFT_WIDTH + MEM_WIDTH
A_OUT_WIDTH = RWKV_WIDTH + MEM_WIDTH
DIL_GROUPS = ((128, 1), (512, 4), (2048, 16))
DIL_HEADS_PER_GROUP = 4
DIL_HEADS = DIL_HEADS_PER_GROUP * len(DIL_GROUPS)
DIL_WIDTH = DIL_HEADS * HEAD_DIM
DIL_BLOCK = 128
B_IN_WIDTH = DIL_WIDTH + MEM_WIDTH
B_OUT_WIDTH = DIL_HEADS_PER_GROUP * HEAD_DIM + MEM_WIDTH
D_FF = 2816
CONV_WIDTH = 3
ROPE_THETA = 10000.0
RMS_EPS = 1e-6
LNX_EPS = 64e-5
NEG_INF = -1e30
N_A = DEPTH // 2
N_B = DEPTH - N_A

kernel_name = "yoco_rwkv7_dilated_memory_convffn"


def f32(a):
    return a.astype(jnp.float32)


def rmsnorm(x, g):
    xf = f32(x)
    return xf * lax.rsqrt(jnp.mean(xf * xf, axis=-1, keepdims=True) + RMS_EPS) * f32(g)


def rope_tables(T):
    inv = ROPE_THETA ** (-jnp.arange(0, HEAD_DIM, 2, dtype=jnp.float32) / HEAD_DIM)
    ang = jnp.arange(T, dtype=jnp.float32)[:, None] * inv[None, :]
    return jnp.cos(ang)[:, None, :], jnp.sin(ang)[:, None, :]


def apply_rope(z, cos, sin):
    half = HEAD_DIM // 2
    z1, z2 = z[..., :half], z[..., half:]
    return jnp.concatenate([z1 * cos - z2 * sin, z2 * cos + z1 * sin], axis=-1)


def rwkv7_time_mix(p, mu, w0, w2, a0, a2, g2, k_k, k_a, r_k, lnx_w, lnx_b):
    B, T, _ = p.shape
    prev = jnp.pad(p[:, :-1], ((0, 0), (1, 0), (0, 0)))
    xs = p + (prev - p) * f32(mu)
    cuts = np.cumsum([RWKV_WIDTH, RWKV_WIDTH, RWKV_WIDTH, DECAY_LORA, AAA_LORA]).tolist()
    r, k, v, wd, ad, gd = jnp.split(xs, cuts, axis=-1)
    w_log = -jax.nn.softplus(-(f32(w0) + jnp.tanh(wd) @ f32(w2))) - 0.5
    decay = jnp.exp(-jnp.exp(w_log))
    a = jax.nn.sigmoid(f32(a0) + ad @ f32(a2))
    g = jax.nn.sigmoid(gd) @ f32(g2)
    heads = lambda z: z.reshape(B, T, RWKV_HEADS, HEAD_DIM)
    kk = heads(k * f32(k_k))
    kk = kk / jnp.maximum(jnp.linalg.norm(kk, axis=-1, keepdims=True), 1e-12)
    k = k * (1.0 + (a - 1.0) * f32(k_a))
    r_h, k_h, v_h, w_h, a_h = heads(r), heads(k), heads(v), heads(decay), heads(a)

    def step(S, inp):
        r_t, w_t, k_t, v_t, kk_t, a_t = inp
        sa = jnp.einsum('bhvk,bhk->bhv', S, -kk_t)
        S = (S * w_t[:, :, None, :] + sa[..., None] * (kk_t * a_t)[:, :, None, :]
             + v_t[..., None] * k_t[:, :, None, :])
        return S, jnp.einsum('bhvk,bhk->bhv', S, r_t)

    seq = tuple(jnp.moveaxis(z, 1, 0) for z in (r_h, w_h, k_h, v_h, kk, a_h))
    S0 = jnp.zeros((B, RWKV_HEADS, HEAD_DIM, HEAD_DIM), jnp.float32)
    _, ys = lax.scan(step, S0, seq)
    y = jnp.moveaxis(ys, 0, 1)
    m = jnp.mean(y, axis=-1, keepdims=True)
    var = jnp.mean((y - m) ** 2, axis=-1, keepdims=True)
    y = ((y - m) * lax.rsqrt(var + LNX_EPS)).reshape(B, T, RWKV_WIDTH) * f32(lnx_w) + f32(lnx_b)
    bonus = (jnp.sum(r_h * k_h * f32(r_k), axis=-1, keepdims=True) * v_h).reshape(B, T, RWKV_WIDTH)
    return (y + bonus) * g


def dilated_group_attention(q, k, v, window, dilation):
    B, T, H, Dh = q.shape
    back = window // dilation
    span = dilation * DIL_BLOCK
    Tp = ((T + span - 1) // span) * span
    U = Tp // dilation
    nb = U // DIL_BLOCK

    def to_blocks(z):
        z = jnp.pad(z, ((0, 0), (0, Tp - T), (0, 0), (0, 0)))
        z = z.reshape(B, U, dilation, H, Dh).transpose(0, 3, 2, 1, 4)
        return z.reshape(B, H, dilation, nb, DIL_BLOCK, Dh)

    def with_prev(z):
        prev = jnp.pad(z, ((0, 0), (0, 0), (0, 0), (1, 0), (0, 0), (0, 0)))[:, :, :, :-1]
        return jnp.concatenate([prev, z], axis=4)

    qb = to_blocks(q)
    kw, vw = with_prev(to_blocks(k)), with_prev(to_blocks(v))
    s = jnp.einsum('bhrnid,bhrnjd->bhrnij', qb, kw) / math.sqrt(Dh)
    i = jnp.arange(DIL_BLOCK)[:, None]
    jj = jnp.arange(2 * DIL_BLOCK)[None, :]
    dist = DIL_BLOCK + i - jj
    band = (dist >= 0) & (dist <= back)
    valid = band[None] & ((jnp.arange(nb) > 0)[:, None, None] | (jj >= DIL_BLOCK)[None])
    s = jnp.where(valid, s, NEG_INF)
    mx = jnp.max(s, axis=-1, keepdims=True)
    pr = jnp.exp(s - mx)
    den = jnp.sum(pr, axis=-1, keepdims=True)
    o = jnp.einsum('bhrnij,bhrnjd->bhrnid', pr, vw) / den
    lse = (mx + jnp.log(den))[..., 0]
    o = o.reshape(B, H, dilation, U, Dh).transpose(0, 3, 2, 1, 4).reshape(B, Tp, H, Dh)[:, :T]
    lse = lse.reshape(B, H, dilation, U).transpose(0, 3, 2, 1).reshape(B, Tp, H)[:, :T]
    return o, lse


def dilated_attention(q, k, v):
    B, T = q.shape[:2]
    outs, lses = [], []
    for gi, (win, dil) in enumerate(DIL_GROUPS):
        hs = slice(gi * DIL_HEADS_PER_GROUP, (gi + 1) * DIL_HEADS_PER_GROUP)
        o, l = dilated_group_attention(q[:, :, hs], k[:, :, hs], v[:, :, hs], win, dil)
        outs.append(o)
        lses.append(l)
    wgt = jax.nn.softmax(jnp.stack(lses, axis=0), axis=0)
    o = jnp.sum(wgt[..., None] * jnp.stack(outs, axis=0), axis=0)
    return o.reshape(B, T, DIL_HEADS_PER_GROUP * HEAD_DIM)


def memory_attention(q, mem, norm_g, w_kv, q_norm, k_norm):
    B, T, _ = q.shape
    M = mem.shape[1]
    kv = f32(rmsnorm(mem, norm_g).astype(mem.dtype) @ w_kv)
    k = rmsnorm(kv[..., :MEM_WIDTH].reshape(B, M, MEM_HEADS, HEAD_DIM), k_norm)
    v = kv[..., MEM_WIDTH:].reshape(B, M, MEM_HEADS, HEAD_DIM)
    qh = rmsnorm(q.reshape(B, T, MEM_HEADS, HEAD_DIM), q_norm)
    s = jnp.einsum('bthd,bmhd->bhtm', qh, k) / math.sqrt(HEAD_DIM)
    p = jax.nn.softmax(s, axis=-1)
    return jnp.einsum('bhtm,bmhd->bthd', p, v).reshape(B, T, MEM_WIDTH)


def conv_ffn(x, g, w_up, conv_w, conv_b, w_down):
    T = x.shape[1]
    u = f32(rmsnorm(x, g).astype(x.dtype) @ w_up)
    up = jnp.pad(u, ((0, 0), (CONV_WIDTH - 1, 0), (0, 0)))
    c = f32(conv_b) + sum(f32(conv_w[j]) * up[:, j:j + T] for j in range(CONV_WIDTH))
    gate, val = jnp.split(c, 2, axis=-1)
    z = jax.nn.silu(gate) * val
    return z.astype(x.dtype) @ w_down


def _fwd_setup_inputs(seed: int = 0) -> dict:
    key = jax.random.key(seed)
    ks = iter(jax.random.split(key, 40))
    D = D_MODEL

    def nrm(shape, scale):
        return scale * jax.random.normal(next(ks), shape, jnp.float32)

    def unif(shape, lo, hi):
        return jax.random.uniform(next(ks), shape, jnp.float32, lo, hi)

    def gain(shape):
        return 1.0 + nrm(shape, 0.02)

    return {
        "x": nrm((BATCH, SEQ, D), 1.0),
        "mem": nrm((BATCH, N_MEM, D), 1.0),
        "attn_norm": gain((DEPTH, D)),
        "a_w_in": nrm((N_A, D, A_IN_WIDTH), D ** -0.5),
        "a_mu": unif((N_A, SHIFT_WIDTH), 0.0, 1.0),
        "a_w0": unif((N_A, RWKV_WIDTH), -5.0, 0.0),
        "a_w2": nrm((N_A, DECAY_LORA, RWKV_WIDTH), 0.5 * DECAY_LORA ** -0.5),
        "a_a0": nrm((N_A, RWKV_WIDTH), 0.5),
        "a_a2": nrm((N_A, AAA_LORA, RWKV_WIDTH), AAA_LORA ** -0.5),
        "a_g2": nrm((N_A, GATE_LORA, RWKV_WIDTH), GATE_LORA ** -0.5),
        "a_k_k": 0.85 + nrm((N_A, RWKV_WIDTH), 0.05),
        "a_k_a": 1.0 + nrm((N_A, RWKV_WIDTH), 0.05),
        "a_r_k": nrm((N_A, RWKV_HEADS, HEAD_DIM), 0.1),
        "a_lnx_w": gain((N_A, RWKV_WIDTH)),
        "a_lnx_b": nrm((N_A, RWKV_WIDTH), 0.02),
        "a_w_out": nrm((N_A, A_OUT_WIDTH, D), A_OUT_WIDTH ** -0.5),
        "kv_norm": gain((D,)),
        "kv_w": nrm((D, 2 * DIL_WIDTH), D ** -0.5),
        "kv_k_norm": gain((HEAD_DIM,)),
        "b_w_in": nrm((N_B, D, B_IN_WIDTH), D ** -0.5),
        "b_q_norm": gain((N_B, HEAD_DIM)),
        "b_w_out": nrm((N_B, B_OUT_WIDTH, D), B_OUT_WIDTH ** -0.5),
        "mem_norm": gain((DEPTH, D)),
        "mem_w_kv": nrm((DEPTH, D, 2 * MEM_WIDTH), D ** -0.5),
        "mem_q_norm": gain((DEPTH, HEAD_DIM)),
        "mem_k_norm": gain((DEPTH, HEAD_DIM)),
        "ffn_norm": gain((DEPTH, D)),
        "ffn_w_up": nrm((DEPTH, D, 2 * D_FF), D ** -0.5),
        "ffn_conv_w": nrm((DEPTH, CONV_WIDTH, 2 * D_FF), CONV_WIDTH ** -0.5),
        "ffn_conv_b": nrm((DEPTH, 2 * D_FF), 0.02),
        "ffn_w_down": nrm((DEPTH, D_FF, D), D_FF ** -0.5),
    }


def _fwd_reference(x, mem, attn_norm, a_w_in, a_mu, a_w0, a_w2, a_a0, a_a2, a_g2, a_k_k, a_k_a,
              a_r_k, a_lnx_w, a_lnx_b, a_w_out, kv_norm, kv_w, kv_k_norm, b_w_in, b_q_norm,
              b_w_out, mem_norm, mem_w_kv, mem_q_norm, mem_k_norm, ffn_norm, ffn_w_up,
              ffn_conv_w, ffn_conv_b, ffn_w_down):
    B, T, _ = x.shape
    cos, sin = rope_tables(T)
    k_sh = v_sh = None
    for i in range(DEPTH):
        h = rmsnorm(x, attn_norm[i]).astype(x.dtype)
        if i < N_A:
            j = i
            p = f32(h @ a_w_in[j])
            y_mix = rwkv7_time_mix(p[..., :SHIFT_WIDTH], a_mu[j], a_w0[j], a_w2[j], a_a0[j],
                                   a_a2[j], a_g2[j], a_k_k[j], a_k_a[j], a_r_k[j],
                                   a_lnx_w[j], a_lnx_b[j])
            q_mem = p[..., SHIFT_WIDTH:]
            w_out = a_w_out[j]
        else:
            j = i - N_A
            if j == 0:
                kvp = f32(rmsnorm(x, kv_norm).astype(x.dtype) @ kv_w)
                k_sh = apply_rope(rmsnorm(kvp[..., :DIL_WIDTH].reshape(B, T, DIL_HEADS, HEAD_DIM),
                                          kv_k_norm), cos, sin)
                v_sh = kvp[..., DIL_WIDTH:].reshape(B, T, DIL_HEADS, HEAD_DIM)
            p = f32(h @ b_w_in[j])
            q = apply_rope(rmsnorm(p[..., :DIL_WIDTH].reshape(B, T, DIL_HEADS, HEAD_DIM),
                                   b_q_norm[j]), cos, sin)
            y_mix = dilated_attention(q, k_sh, v_sh)
            q_mem = p[..., DIL_WIDTH:]
            w_out = b_w_out[j]
        y_mem = memory_attention(q_mem, mem, mem_norm[i], mem_w_kv[i], mem_q_norm[i], mem_k_norm[i])
        y = jnp.concatenate([y_mix, y_mem], axis=-1).astype(x.dtype)
        x = x + y @ w_out
        x = x + conv_ffn(x, ffn_norm[i], ffn_w_up[i], ffn_conv_w[i], ffn_conv_b[i], ffn_w_down[i])
    return x


import jax as _jax
import jax.numpy as _jnp

TWIN_FORMAT = 'train_step'
FWD_PARAMS = ['x', 'mem', 'attn_norm', 'a_w_in', 'a_mu', 'a_w0', 'a_w2', 'a_a0', 'a_a2', 'a_g2', 'a_k_k', 'a_k_a', 'a_r_k', 'a_lnx_w', 'a_lnx_b', 'a_w_out', 'kv_norm', 'kv_w', 'kv_k_norm', 'b_w_in', 'b_q_norm', 'b_w_out', 'mem_norm', 'mem_w_kv', 'mem_q_norm', 'mem_k_norm', 'ffn_norm', 'ffn_w_up', 'ffn_conv_w', 'ffn_conv_b', 'ffn_w_down']
TWIN_WEIGHTS = ['attn_norm', 'a_w_in', 'a_mu', 'a_w0', 'a_w2', 'a_a0', 'a_a2', 'a_g2', 'a_k_k', 'a_k_a', 'a_r_k', 'a_lnx_w', 'a_lnx_b', 'a_w_out', 'kv_norm', 'kv_w', 'kv_k_norm', 'b_w_in', 'b_q_norm', 'b_w_out', 'mem_norm', 'mem_w_kv', 'mem_q_norm', 'mem_k_norm', 'ffn_norm', 'ffn_w_up', 'ffn_conv_w', 'ffn_conv_b', 'ffn_w_down']
TWIN_DIFF_INPUT = 'x'
TWIN_INPUTS = ['x', 'mem', 'attn_norm', 'a_w_in', 'a_mu', 'a_w0', 'a_w2', 'a_a0', 'a_a2', 'a_g2', 'a_k_k', 'a_k_a', 'a_r_k', 'a_lnx_w', 'a_lnx_b', 'a_w_out', 'kv_norm', 'kv_w', 'kv_k_norm', 'b_w_in', 'b_q_norm', 'b_w_out', 'mem_norm', 'mem_w_kv', 'mem_q_norm', 'mem_k_norm', 'ffn_norm', 'ffn_w_up', 'ffn_conv_w', 'ffn_conv_b', 'ffn_w_down', 'loss_target', 'm_attn_norm', 'm_a_w_in', 'm_a_mu', 'm_a_w0', 'm_a_w2', 'm_a_a0', 'm_a_a2', 'm_a_g2', 'm_a_k_k', 'm_a_k_a', 'm_a_r_k', 'm_a_lnx_w', 'm_a_lnx_b', 'm_a_w_out', 'm_kv_norm', 'm_kv_w', 'm_kv_k_norm', 'm_b_w_in', 'm_b_q_norm', 'm_b_w_out', 'm_mem_norm', 'm_mem_w_kv', 'm_mem_q_norm', 'm_mem_k_norm', 'm_ffn_norm', 'm_ffn_w_up', 'm_ffn_conv_w', 'm_ffn_conv_b', 'm_ffn_w_down', 'v_attn_norm', 'v_a_w_in', 'v_a_mu', 'v_a_w0', 'v_a_w2', 'v_a_a0', 'v_a_a2', 'v_a_g2', 'v_a_k_k', 'v_a_k_a', 'v_a_r_k', 'v_a_lnx_w', 'v_a_lnx_b', 'v_a_w_out', 'v_kv_norm', 'v_kv_w', 'v_kv_k_norm', 'v_b_w_in', 'v_b_q_norm', 'v_b_w_out', 'v_mem_norm', 'v_mem_w_kv', 'v_mem_q_norm', 'v_mem_k_norm', 'v_ffn_norm', 'v_ffn_w_up', 'v_ffn_conv_w', 'v_ffn_conv_b', 'v_ffn_w_down']
TWIN_OUTPUTS = ['loss', 'grad_x', 'grad_attn_norm', 'grad_a_w_in', 'grad_a_mu', 'grad_a_w0', 'grad_a_w2', 'grad_a_a0', 'grad_a_a2', 'grad_a_g2', 'grad_a_k_k', 'grad_a_k_a', 'grad_a_r_k', 'grad_a_lnx_w', 'grad_a_lnx_b', 'grad_a_w_out', 'grad_kv_norm', 'grad_kv_w', 'grad_kv_k_norm', 'grad_b_w_in', 'grad_b_q_norm', 'grad_b_w_out', 'grad_mem_norm', 'grad_mem_w_kv', 'grad_mem_q_norm', 'grad_mem_k_norm', 'grad_ffn_norm', 'grad_ffn_w_up', 'grad_ffn_conv_w', 'grad_ffn_conv_b', 'grad_ffn_w_down', 'delta_attn_norm', 'delta_a_w_in', 'delta_a_mu', 'delta_a_w0', 'delta_a_w2', 'delta_a_a0', 'delta_a_a2', 'delta_a_g2', 'delta_a_k_k', 'delta_a_k_a', 'delta_a_r_k', 'delta_a_lnx_w', 'delta_a_lnx_b', 'delta_a_w_out', 'delta_kv_norm', 'delta_kv_w', 'delta_kv_k_norm', 'delta_b_w_in', 'delta_b_q_norm', 'delta_b_w_out', 'delta_mem_norm', 'delta_mem_w_kv', 'delta_mem_q_norm', 'delta_mem_k_norm', 'delta_ffn_norm', 'delta_ffn_w_up', 'delta_ffn_conv_w', 'delta_ffn_conv_b', 'delta_ffn_w_down', 'new_m_attn_norm', 'new_m_a_w_in', 'new_m_a_mu', 'new_m_a_w0', 'new_m_a_w2', 'new_m_a_a0', 'new_m_a_a2', 'new_m_a_g2', 'new_m_a_k_k', 'new_m_a_k_a', 'new_m_a_r_k', 'new_m_a_lnx_w', 'new_m_a_lnx_b', 'new_m_a_w_out', 'new_m_kv_norm', 'new_m_kv_w', 'new_m_kv_k_norm', 'new_m_b_w_in', 'new_m_b_q_norm', 'new_m_b_w_out', 'new_m_mem_norm', 'new_m_mem_w_kv', 'new_m_mem_q_norm', 'new_m_mem_k_norm', 'new_m_ffn_norm', 'new_m_ffn_w_up', 'new_m_ffn_conv_w', 'new_m_ffn_conv_b', 'new_m_ffn_w_down', 'new_v_attn_norm', 'new_v_a_w_in', 'new_v_a_mu', 'new_v_a_w0', 'new_v_a_w2', 'new_v_a_a0', 'new_v_a_a2', 'new_v_a_g2', 'new_v_a_k_k', 'new_v_a_k_a', 'new_v_a_r_k', 'new_v_a_lnx_w', 'new_v_a_lnx_b', 'new_v_a_w_out', 'new_v_kv_norm', 'new_v_kv_w', 'new_v_kv_k_norm', 'new_v_b_w_in', 'new_v_b_q_norm', 'new_v_b_w_out', 'new_v_mem_norm', 'new_v_mem_w_kv', 'new_v_mem_q_norm', 'new_v_mem_k_norm', 'new_v_ffn_norm', 'new_v_ffn_w_up', 'new_v_ffn_conv_w', 'new_v_ffn_conv_b', 'new_v_ffn_w_down']
TWIN_LEAF_KINDS = {'loss': 'loss', 'grad_x': 'grad_x', 'grad_attn_norm': 'grad_w', 'grad_a_w_in': 'grad_w', 'grad_a_mu': 'grad_w', 'grad_a_w0': 'grad_w', 'grad_a_w2': 'grad_w', 'grad_a_a0': 'grad_w', 'grad_a_a2': 'grad_w', 'grad_a_g2': 'grad_w', 'grad_a_k_k': 'grad_w', 'grad_a_k_a': 'grad_w', 'grad_a_r_k': 'grad_w', 'grad_a_lnx_w': 'grad_w', 'grad_a_lnx_b': 'grad_w', 'grad_a_w_out': 'grad_w', 'grad_kv_norm': 'grad_w', 'grad_kv_w': 'grad_w', 'grad_kv_k_norm': 'grad_w', 'grad_b_w_in': 'grad_w', 'grad_b_q_norm': 'grad_w', 'grad_b_w_out': 'grad_w', 'grad_mem_norm': 'grad_w', 'grad_mem_w_kv': 'grad_w', 'grad_mem_q_norm': 'grad_w', 'grad_mem_k_norm': 'grad_w', 'grad_ffn_norm': 'grad_w', 'grad_ffn_w_up': 'grad_w', 'grad_ffn_conv_w': 'grad_w', 'grad_ffn_conv_b': 'grad_w', 'grad_ffn_w_down': 'grad_w', 'delta_attn_norm': 'delta_w', 'delta_a_w_in': 'delta_w', 'delta_a_mu': 'delta_w', 'delta_a_w0': 'delta_w', 'delta_a_w2': 'delta_w', 'delta_a_a0': 'delta_w', 'delta_a_a2': 'delta_w', 'delta_a_g2': 'delta_w', 'delta_a_k_k': 'delta_w', 'delta_a_k_a': 'delta_w', 'delta_a_r_k': 'delta_w', 'delta_a_lnx_w': 'delta_w', 'delta_a_lnx_b': 'delta_w', 'delta_a_w_out': 'delta_w', 'delta_kv_norm': 'delta_w', 'delta_kv_w': 'delta_w', 'delta_kv_k_norm': 'delta_w', 'delta_b_w_in': 'delta_w', 'delta_b_q_norm': 'delta_w', 'delta_b_w_out': 'delta_w', 'delta_mem_norm': 'delta_w', 'delta_mem_w_kv': 'delta_w', 'delta_mem_q_norm': 'delta_w', 'delta_mem_k_norm': 'delta_w', 'delta_ffn_norm': 'delta_w', 'delta_ffn_w_up': 'delta_w', 'delta_ffn_conv_w': 'delta_w', 'delta_ffn_conv_b': 'delta_w', 'delta_ffn_w_down': 'delta_w', 'new_m_attn_norm': 'new_m', 'new_m_a_w_in': 'new_m', 'new_m_a_mu': 'new_m', 'new_m_a_w0': 'new_m', 'new_m_a_w2': 'new_m', 'new_m_a_a0': 'new_m', 'new_m_a_a2': 'new_m', 'new_m_a_g2': 'new_m', 'new_m_a_k_k': 'new_m', 'new_m_a_k_a': 'new_m', 'new_m_a_r_k': 'new_m', 'new_m_a_lnx_w': 'new_m', 'new_m_a_lnx_b': 'new_m', 'new_m_a_w_out': 'new_m', 'new_m_kv_norm': 'new_m', 'new_m_kv_w': 'new_m', 'new_m_kv_k_norm': 'new_m', 'new_m_b_w_in': 'new_m', 'new_m_b_q_norm': 'new_m', 'new_m_b_w_out': 'new_m', 'new_m_mem_norm': 'new_m', 'new_m_mem_w_kv': 'new_m', 'new_m_mem_q_norm': 'new_m', 'new_m_mem_k_norm': 'new_m', 'new_m_ffn_norm': 'new_m', 'new_m_ffn_w_up': 'new_m', 'new_m_ffn_conv_w': 'new_m', 'new_m_ffn_conv_b': 'new_m', 'new_m_ffn_w_down': 'new_m', 'new_v_attn_norm': 'new_v', 'new_v_a_w_in': 'new_v', 'new_v_a_mu': 'new_v', 'new_v_a_w0': 'new_v', 'new_v_a_w2': 'new_v', 'new_v_a_a0': 'new_v', 'new_v_a_a2': 'new_v', 'new_v_a_g2': 'new_v', 'new_v_a_k_k': 'new_v', 'new_v_a_k_a': 'new_v', 'new_v_a_r_k': 'new_v', 'new_v_a_lnx_w': 'new_v', 'new_v_a_lnx_b': 'new_v', 'new_v_a_w_out': 'new_v', 'new_v_kv_norm': 'new_v', 'new_v_kv_w': 'new_v', 'new_v_kv_k_norm': 'new_v', 'new_v_b_w_in': 'new_v', 'new_v_b_q_norm': 'new_v', 'new_v_b_w_out': 'new_v', 'new_v_mem_norm': 'new_v', 'new_v_mem_w_kv': 'new_v', 'new_v_mem_q_norm': 'new_v', 'new_v_mem_k_norm': 'new_v', 'new_v_ffn_norm': 'new_v', 'new_v_ffn_w_up': 'new_v', 'new_v_ffn_conv_w': 'new_v', 'new_v_ffn_conv_b': 'new_v', 'new_v_ffn_w_down': 'new_v'}


def _forward(args):
    return _fwd_reference(*[args[k] for k in FWD_PARAMS])


def _output_shape():
    out = _jax.eval_shape(lambda: _forward(_fwd_setup_inputs(0)))
    return out.shape, out.dtype

N_MICROBATCH = 1
ADAM_LR = 0.001
ADAM_B1 = 0.9
ADAM_B2 = 0.999
ADAM_EPS = 1e-08
ADAM_WD = 0.01
ADAM_STEP = 10
PER_EXAMPLE_BATCH_AXIS = {'x': 0, 'mem': 0, 'loss_target': 0}
SHARED_INPUTS = []
_WEIGHT_DTYPES = {'attn_norm': _jnp.float32, 'a_w_in': _jnp.float32, 'a_mu': _jnp.float32, 'a_w0': _jnp.float32, 'a_w2': _jnp.float32, 'a_a0': _jnp.float32, 'a_a2': _jnp.float32, 'a_g2': _jnp.float32, 'a_k_k': _jnp.float32, 'a_k_a': _jnp.float32, 'a_r_k': _jnp.float32, 'a_lnx_w': _jnp.float32, 'a_lnx_b': _jnp.float32, 'a_w_out': _jnp.float32, 'kv_norm': _jnp.float32, 'kv_w': _jnp.float32, 'kv_k_norm': _jnp.float32, 'b_w_in': _jnp.float32, 'b_q_norm': _jnp.float32, 'b_w_out': _jnp.float32, 'mem_norm': _jnp.float32, 'mem_w_kv': _jnp.float32, 'mem_q_norm': _jnp.float32, 'mem_k_norm': _jnp.float32, 'ffn_norm': _jnp.float32, 'ffn_w_up': _jnp.float32, 'ffn_conv_w': _jnp.float32, 'ffn_conv_b': _jnp.float32, 'ffn_w_down': _jnp.float32}
MOMENT_SCALE = {'attn_norm': 9.469260e-01, 'a_w_in': 3.698895e-01, 'a_mu': 3.699694e+00, 'a_w0': 2.047778e-01, 'a_w2': 2.233600e-02, 'a_a0': 3.877158e-01, 'a_a2': 1.221917e-01, 'a_g2': 8.392881e+00, 'a_k_k': 3.182796e-01, 'a_k_a': 8.008593e-01, 'a_r_k': 6.816591e+00, 'a_lnx_w': 1.437957e+01, 'a_lnx_b': 6.782889e+00, 'a_w_out': 4.316281e-01, 'kv_norm': 1.358069e-01, 'kv_w': 7.432727e-02, 'kv_k_norm': 1.037809e+00, 'b_w_in': 6.505719e-02, 'b_q_norm': 1.037846e+00, 'b_w_out': 1.029076e-01, 'mem_norm': 1.895925e-01, 'mem_w_kv': 2.209276e-01, 'mem_q_norm': 1.627335e+00, 'mem_k_norm': 1.618586e+00, 'ffn_norm': 2.604511e+01, 'ffn_w_up': 2.510547e-01, 'ffn_conv_w': 3.548421e+00, 'ffn_conv_b': 3.186608e+00, 'ffn_w_down': 3.500625e-01}


def _to_microbatches(a, axis):
    t = _jnp.moveaxis(a, axis, 0)
    t = t.reshape((N_MICROBATCH, t.shape[0] // N_MICROBATCH) + t.shape[1:])
    return _jnp.moveaxis(t, 1, axis + 1)


def setup_inputs(seed: int = 0) -> dict:
    inp = _fwd_setup_inputs(seed)
    key = _jax.random.fold_in(_jax.random.key(seed), 7919)
    shape, _ = _output_shape()
    out = dict(inp)
    out["loss_target"] = _jax.random.normal(_jax.random.fold_in(key, 0), shape, _jnp.float32)
    for i, name in enumerate(TWIN_WEIGHTS):
        w = inp[name].astype(_jnp.float32)
        if MOMENT_SCALE is None:
            s = _jnp.sqrt(_jnp.mean(_jnp.square(w)) + 1e-30)
        else:
            s = MOMENT_SCALE[name]
        km, kv = _jax.random.split(_jax.random.fold_in(key, i + 1))
        out[name] = w
        out["m_" + name] = s * _jax.random.normal(km, w.shape, _jnp.float32)
        out["v_" + name] = (s * s) * _jax.random.uniform(kv, w.shape, _jnp.float32, 0.5, 1.5)
    if N_MICROBATCH > 1:
        for name, axis in PER_EXAMPLE_BATCH_AXIS.items():
            out[name] = _to_microbatches(out[name], axis)
    return {'x': out['x'], 'mem': out['mem'], 'attn_norm': out['attn_norm'], 'a_w_in': out['a_w_in'], 'a_mu': out['a_mu'], 'a_w0': out['a_w0'], 'a_w2': out['a_w2'], 'a_a0': out['a_a0'], 'a_a2': out['a_a2'], 'a_g2': out['a_g2'], 'a_k_k': out['a_k_k'], 'a_k_a': out['a_k_a'], 'a_r_k': out['a_r_k'], 'a_lnx_w': out['a_lnx_w'], 'a_lnx_b': out['a_lnx_b'], 'a_w_out': out['a_w_out'], 'kv_norm': out['kv_norm'], 'kv_w': out['kv_w'], 'kv_k_norm': out['kv_k_norm'], 'b_w_in': out['b_w_in'], 'b_q_norm': out['b_q_norm'], 'b_w_out': out['b_w_out'], 'mem_norm': out['mem_norm'], 'mem_w_kv': out['mem_w_kv'], 'mem_q_norm': out['mem_q_norm'], 'mem_k_norm': out['mem_k_norm'], 'ffn_norm': out['ffn_norm'], 'ffn_w_up': out['ffn_w_up'], 'ffn_conv_w': out['ffn_conv_w'], 'ffn_conv_b': out['ffn_conv_b'], 'ffn_w_down': out['ffn_w_down'], 'loss_target': out['loss_target'], 'm_attn_norm': out['m_attn_norm'], 'm_a_w_in': out['m_a_w_in'], 'm_a_mu': out['m_a_mu'], 'm_a_w0': out['m_a_w0'], 'm_a_w2': out['m_a_w2'], 'm_a_a0': out['m_a_a0'], 'm_a_a2': out['m_a_a2'], 'm_a_g2': out['m_a_g2'], 'm_a_k_k': out['m_a_k_k'], 'm_a_k_a': out['m_a_k_a'], 'm_a_r_k': out['m_a_r_k'], 'm_a_lnx_w': out['m_a_lnx_w'], 'm_a_lnx_b': out['m_a_lnx_b'], 'm_a_w_out': out['m_a_w_out'], 'm_kv_norm': out['m_kv_norm'], 'm_kv_w': out['m_kv_w'], 'm_kv_k_norm': out['m_kv_k_norm'], 'm_b_w_in': out['m_b_w_in'], 'm_b_q_norm': out['m_b_q_norm'], 'm_b_w_out': out['m_b_w_out'], 'm_mem_norm': out['m_mem_norm'], 'm_mem_w_kv': out['m_mem_w_kv'], 'm_mem_q_norm': out['m_mem_q_norm'], 'm_mem_k_norm': out['m_mem_k_norm'], 'm_ffn_norm': out['m_ffn_norm'], 'm_ffn_w_up': out['m_ffn_w_up'], 'm_ffn_conv_w': out['m_ffn_conv_w'], 'm_ffn_conv_b': out['m_ffn_conv_b'], 'm_ffn_w_down': out['m_ffn_w_down'], 'v_attn_norm': out['v_attn_norm'], 'v_a_w_in': out['v_a_w_in'], 'v_a_mu': out['v_a_mu'], 'v_a_w0': out['v_a_w0'], 'v_a_w2': out['v_a_w2'], 'v_a_a0': out['v_a_a0'], 'v_a_a2': out['v_a_a2'], 'v_a_g2': out['v_a_g2'], 'v_a_k_k': out['v_a_k_k'], 'v_a_k_a': out['v_a_k_a'], 'v_a_r_k': out['v_a_r_k'], 'v_a_lnx_w': out['v_a_lnx_w'], 'v_a_lnx_b': out['v_a_lnx_b'], 'v_a_w_out': out['v_a_w_out'], 'v_kv_norm': out['v_kv_norm'], 'v_kv_w': out['v_kv_w'], 'v_kv_k_norm': out['v_kv_k_norm'], 'v_b_w_in': out['v_b_w_in'], 'v_b_q_norm': out['v_b_q_norm'], 'v_b_w_out': out['v_b_w_out'], 'v_mem_norm': out['v_mem_norm'], 'v_mem_w_kv': out['v_mem_w_kv'], 'v_mem_q_norm': out['v_mem_q_norm'], 'v_mem_k_norm': out['v_mem_k_norm'], 'v_ffn_norm': out['v_ffn_norm'], 'v_ffn_w_up': out['v_ffn_w_up'], 'v_ffn_conv_w': out['v_ffn_conv_w'], 'v_ffn_conv_b': out['v_ffn_conv_b'], 'v_ffn_w_down': out['v_ffn_w_down']}


def _loss(weights, diff, rest, loss_target):
    with _jax.named_scope("forward"):
        args = {**rest, TWIN_DIFF_INPUT: diff, **{k: w.astype(_WEIGHT_DTYPES[k]) for k, w in weights.items()}}
        y = _forward(args)
    with _jax.named_scope("loss_head"):
        err = _jnp.square(y.astype(_jnp.float32) - loss_target)
        return 0.5 * _jnp.sum(_jnp.mean(err, axis=-1)) if err.ndim else 0.5 * err


def _adamw(w, g, m, v):
    m = ADAM_B1 * m + (1.0 - ADAM_B1) * g
    v = ADAM_B2 * v + (1.0 - ADAM_B2) * _jnp.square(g)
    m_hat = m / (1.0 - ADAM_B1 ** ADAM_STEP)
    v_hat = v / (1.0 - ADAM_B2 ** ADAM_STEP)
    delta = -ADAM_LR * (m_hat / (_jnp.sqrt(v_hat) + ADAM_EPS) + ADAM_WD * w)
    return delta, m, v


def reference(x, mem, attn_norm, a_w_in, a_mu, a_w0, a_w2, a_a0, a_a2, a_g2, a_k_k, a_k_a, a_r_k, a_lnx_w, a_lnx_b, a_w_out, kv_norm, kv_w, kv_k_norm, b_w_in, b_q_norm, b_w_out, mem_norm, mem_w_kv, mem_q_norm, mem_k_norm, ffn_norm, ffn_w_up, ffn_conv_w, ffn_conv_b, ffn_w_down, loss_target, m_attn_norm, m_a_w_in, m_a_mu, m_a_w0, m_a_w2, m_a_a0, m_a_a2, m_a_g2, m_a_k_k, m_a_k_a, m_a_r_k, m_a_lnx_w, m_a_lnx_b, m_a_w_out, m_kv_norm, m_kv_w, m_kv_k_norm, m_b_w_in, m_b_q_norm, m_b_w_out, m_mem_norm, m_mem_w_kv, m_mem_q_norm, m_mem_k_norm, m_ffn_norm, m_ffn_w_up, m_ffn_conv_w, m_ffn_conv_b, m_ffn_w_down, v_attn_norm, v_a_w_in, v_a_mu, v_a_w0, v_a_w2, v_a_a0, v_a_a2, v_a_g2, v_a_k_k, v_a_k_a, v_a_r_k, v_a_lnx_w, v_a_lnx_b, v_a_w_out, v_kv_norm, v_kv_w, v_kv_k_norm, v_b_w_in, v_b_q_norm, v_b_w_out, v_mem_norm, v_mem_w_kv, v_mem_q_norm, v_mem_k_norm, v_ffn_norm, v_ffn_w_up, v_ffn_conv_w, v_ffn_conv_b, v_ffn_w_down):
    given = dict(x=x, mem=mem, attn_norm=attn_norm, a_w_in=a_w_in, a_mu=a_mu, a_w0=a_w0, a_w2=a_w2, a_a0=a_a0, a_a2=a_a2, a_g2=a_g2, a_k_k=a_k_k, a_k_a=a_k_a, a_r_k=a_r_k, a_lnx_w=a_lnx_w, a_lnx_b=a_lnx_b, a_w_out=a_w_out, kv_norm=kv_norm, kv_w=kv_w, kv_k_norm=kv_k_norm, b_w_in=b_w_in, b_q_norm=b_q_norm, b_w_out=b_w_out, mem_norm=mem_norm, mem_w_kv=mem_w_kv, mem_q_norm=mem_q_norm, mem_k_norm=mem_k_norm, ffn_norm=ffn_norm, ffn_w_up=ffn_w_up, ffn_conv_w=ffn_conv_w, ffn_conv_b=ffn_conv_b, ffn_w_down=ffn_w_down, loss_target=loss_target, m_attn_norm=m_attn_norm, m_a_w_in=m_a_w_in, m_a_mu=m_a_mu, m_a_w0=m_a_w0, m_a_w2=m_a_w2, m_a_a0=m_a_a0, m_a_a2=m_a_a2, m_a_g2=m_a_g2, m_a_k_k=m_a_k_k, m_a_k_a=m_a_k_a, m_a_r_k=m_a_r_k, m_a_lnx_w=m_a_lnx_w, m_a_lnx_b=m_a_lnx_b, m_a_w_out=m_a_w_out, m_kv_norm=m_kv_norm, m_kv_w=m_kv_w, m_kv_k_norm=m_kv_k_norm, m_b_w_in=m_b_w_in, m_b_q_norm=m_b_q_norm, m_b_w_out=m_b_w_out, m_mem_norm=m_mem_norm, m_mem_w_kv=m_mem_w_kv, m_mem_q_norm=m_mem_q_norm, m_mem_k_norm=m_mem_k_norm, m_ffn_norm=m_ffn_norm, m_ffn_w_up=m_ffn_w_up, m_ffn_conv_w=m_ffn_conv_w, m_ffn_conv_b=m_ffn_conv_b, m_ffn_w_down=m_ffn_w_down, v_attn_norm=v_attn_norm, v_a_w_in=v_a_w_in, v_a_mu=v_a_mu, v_a_w0=v_a_w0, v_a_w2=v_a_w2, v_a_a0=v_a_a0, v_a_a2=v_a_a2, v_a_g2=v_a_g2, v_a_k_k=v_a_k_k, v_a_k_a=v_a_k_a, v_a_r_k=v_a_r_k, v_a_lnx_w=v_a_lnx_w, v_a_lnx_b=v_a_lnx_b, v_a_w_out=v_a_w_out, v_kv_norm=v_kv_norm, v_kv_w=v_kv_w, v_kv_k_norm=v_kv_k_norm, v_b_w_in=v_b_w_in, v_b_q_norm=v_b_q_norm, v_b_w_out=v_b_w_out, v_mem_norm=v_mem_norm, v_mem_w_kv=v_mem_w_kv, v_mem_q_norm=v_mem_q_norm, v_mem_k_norm=v_mem_k_norm, v_ffn_norm=v_ffn_norm, v_ffn_w_up=v_ffn_w_up, v_ffn_conv_w=v_ffn_conv_w, v_ffn_conv_b=v_ffn_conv_b, v_ffn_w_down=v_ffn_w_down)
    weights = {n: given[n] for n in TWIN_WEIGHTS}
    shared = {n: given[n] for n in SHARED_INPUTS}
    per_example = {n: given[n] for n in ['x', 'mem']}
    grad_fn = _jax.value_and_grad(_loss, argnums=(0, 1))

    def one_microbatch(ex, loss_target):
        ex = dict(ex)
        diff = ex.pop(TWIN_DIFF_INPUT)
        return grad_fn(weights, diff, {**shared, **ex}, loss_target)

    if N_MICROBATCH == 1:
        loss, (grad_w, grad_x) = one_microbatch(per_example, given["loss_target"])
    else:
        def body(carry, xs):
            loss_sum, grad_sum = carry
            l_k, (gw_k, gx_k) = one_microbatch(xs[0], xs[1])
            with _jax.named_scope("update"):
                return (loss_sum + l_k, _jax.tree.map(_jnp.add, grad_sum, gw_k)), gx_k

        init = (_jnp.zeros((), _jnp.float32), _jax.tree.map(_jnp.zeros_like, weights))
        (loss, grad_w), grad_x = _jax.lax.scan(body, init, (per_example, given["loss_target"]))
    with _jax.named_scope("update"):
        delta_w, new_m, new_v = {}, {}, {}
        for n in TWIN_WEIGHTS:
            delta_w[n], new_m[n], new_v[n] = _adamw(weights[n], grad_w[n], given["m_" + n], given["v_" + n])
    return (loss, grad_x, *[grad_w[n] for n in TWIN_WEIGHTS], *[delta_w[n] for n in TWIN_WEIGHTS],
            *[new_m[n] for n in TWIN_WEIGHTS], *[new_v[n] for n in TWIN_WEIGHTS])
```

```python
import functools
import math

import numpy as np
import jax
import jax.numpy as jnp
from jax import lax
from jax.experimental import pallas as pl
from jax.experimental.pallas import tpu as pltpu

F32 = jnp.float32
BF16 = jnp.bfloat16
MMD = jnp.bfloat16

D_MODEL = 1024
HEAD_DIM = 64
N_MEM = 256
MEM_WIDTH = 256
RWKV_HEADS = 12
RWKV_WIDTH = 768
SHIFT_WIDTH = 2560
LORA_WIDTH = 256
DIL_WIDTH = 768
DIL_GROUPS = ((128, 1), (512, 4), (2048, 16))
DIL_BLOCK = 128
D_FF = 2816
ROPE_THETA = 10000.0
RMS_EPS = 1e-6
LNX_EPS = 64e-5
NEG_INF = -1e30
ADAM_LR = 0.001
ADAM_B1 = 0.9
ADAM_B2 = 0.999
ADAM_EPS = 1e-08
ADAM_WD = 0.01
ADAM_STEP = 10
N_CHIPS = 4
MESH = pl.DeviceIdType.MESH
VMEM_LIMIT_MB = 56
SCAN_CHUNK = 16

BIG = (("a_w_in", 2), ("a_w_out", 1), ("kv_w", 1), ("b_w_in", 1), ("b_w_out", 2), ("mem_w_kv", 1),
       ("ffn_w_up", 2), ("ffn_w_down", 1))
SMALL_SHARDED = (("a_mu", 1), ("a_w0", 1), ("a_w2", 2), ("a_a0", 1), ("a_a2", 2), ("a_g2", 2), ("a_k_k", 1),
                 ("a_k_a", 1), ("a_lnx_w", 1), ("a_lnx_b", 1), ("ffn_conv_w", 2))
SMALL_REPL = ("attn_norm", "a_r_k", "kv_norm", "kv_k_norm", "b_q_norm", "mem_norm", "mem_q_norm", "mem_k_norm",
              "ffn_norm", "ffn_conv_b")
WEIGHTS = ("attn_norm", "a_w_in", "a_mu", "a_w0", "a_w2", "a_a0", "a_a2", "a_g2", "a_k_k", "a_k_a", "a_r_k",
           "a_lnx_w", "a_lnx_b", "a_w_out", "kv_norm", "kv_w", "kv_k_norm", "b_w_in", "b_q_norm", "b_w_out",
           "mem_norm", "mem_w_kv", "mem_q_norm", "mem_k_norm", "ffn_norm", "ffn_w_up", "ffn_conv_w", "ffn_conv_b",
           "ffn_w_down")


def _cp(sem=None, **kw):
    return pltpu.CompilerParams(dimension_semantics=sem, vmem_limit_bytes=VMEM_LIMIT_MB << 20, **kw)


def _tile(n, cands=(512, 256, 128)):
    for c in cands:
        if n % c == 0:
            return c
    return n


def _mm(a, b, *, name, ta=False, tb=False, add=None, out_dtype=F32):
    K, M = a.shape if ta else a.shape[::-1]
    N = b.shape[0] if tb else b.shape[1]
    assert K == (b.shape[1] if tb else b.shape[0])
    tm, tn = _tile(M), _tile(N)
    a_spec = pl.BlockSpec((K, tm), lambda i, j: (0, i)) if ta else pl.BlockSpec((tm, K), lambda i, j: (i, 0))
    b_spec = pl.BlockSpec((tn, K), lambda i, j: (j, 0)) if tb else pl.BlockSpec((K, tn), lambda i, j: (0, j))
    o_spec = pl.BlockSpec((tm, tn), lambda i, j: (i, j))
    dn = (((0,) if ta else (1,), (1,) if tb else (0,)), ((), ()))
    has_add = add is not None

    def body(*refs):
        a_ref, b_ref = refs[0], refs[1]
        o_ref = refs[-1]
        acc = lax.dot_general(a_ref[...].astype(MMD), b_ref[...].astype(MMD), dn, preferred_element_type=F32)
        if has_add:
            acc = acc + refs[2][...]
        o_ref[...] = acc.astype(o_ref.dtype)

    ins = [a, b] + ([add] if has_add else [])
    specs = [a_spec, b_spec] + ([o_spec] if has_add else [])
    return pl.pallas_call(
        body, name=name, grid=(M // tm, N // tn), in_specs=specs, out_specs=o_spec,
        out_shape=jax.ShapeDtypeStruct((M, N), out_dtype), compiler_params=_cp(("parallel", "parallel")),
    )(*ins)


def _rowmap(fn, *, name, T, tb, ins, outs, accs=()):
    nblk = T // tb
    assert T % tb == 0 and tb % 8 == 0
    in_specs, args = [], []
    for spec in ins:
        kind, arr = spec[0], spec[1]
        w, cb = (spec[2], spec[3]) if len(spec) > 2 else (arr.shape[-1], 0)
        if kind == "row":
            in_specs.append(pl.BlockSpec((tb, w), lambda i, cb=cb: (i, cb)))
        elif kind == "prev":
            in_specs.append(pl.BlockSpec((8, w), lambda i, cb=cb: (jnp.maximum(i * (tb // 8) - 1, 0), cb)))
        elif kind == "next":
            in_specs.append(pl.BlockSpec((8, w), lambda i, cb=cb: (jnp.minimum((i + 1) * (tb // 8), T // 8 - 1), cb)))
        elif kind == "const":
            in_specs.append(pl.BlockSpec(arr.shape, lambda i, nd=arr.ndim: (0,) * nd))
        else:
            raise ValueError(kind)
        args.append(arr)
    out_shape, out_specs = [], []
    for kind, w, dt in outs:
        out_shape.append(jax.ShapeDtypeStruct((T, w), dt))
        out_specs.append(pl.BlockSpec((tb, w), lambda i: (i, 0)))
    for shp, dt in accs:
        out_shape.append(jax.ShapeDtypeStruct(shp, dt))
        out_specs.append(pl.BlockSpec(shp, lambda i, nd=len(shp): (0,) * nd))
    n_in, n_out = len(ins), len(outs)

    def body(*refs):
        i = pl.program_id(0)
        vals = [r[...] for r in refs[:n_in]]
        res = fn(i, nblk, *vals)
        if not isinstance(res, (tuple, list)):
            res = (res,)
        assert len(res) == n_out + len(accs), (name, len(res))
        for r, v in zip(refs[n_in:n_in + n_out], res[:n_out]):
            r[...] = v.astype(r.dtype)
        acc_refs = refs[n_in + n_out:]
        if acc_refs:
            @pl.when(i == 0)
            def _():
                for r in acc_refs:
                    r[...] = jnp.zeros(r.shape, r.dtype)

            for r, v in zip(acc_refs, res[n_out:]):
                r[...] += v

    res = pl.pallas_call(
        body, name=name, grid=(nblk,), in_specs=in_specs, out_specs=out_specs, out_shape=out_shape,
        compiler_params=_cp(("arbitrary",)),
    )(*args)
    return res


def _row_pick(halo, r):
    rid = lax.broadcasted_iota(jnp.int32, halo.shape, 0)
    return jnp.sum(jnp.where(rid == r, halo, 0.0), axis=0, keepdims=True)


def _shift_down(x, row_before, is_first):
    rid = lax.broadcasted_iota(jnp.int32, x.shape, 0)
    first = jnp.where(is_first, 0.0, 1.0) * row_before
    return jnp.where(rid == 0, first, pltpu.roll(x, 1, axis=0))


def _shift_up(x, row_after, is_last):
    n = x.shape[0]
    rid = lax.broadcasted_iota(jnp.int32, x.shape, 0)
    last = jnp.where(is_last, 0.0, 1.0) * row_after
    return jnp.where(rid == n - 1, last, pltpu.roll(x, n - 1, axis=0))


def _dot(a, b, dn=(((1,), (0,)), ((), ()))):
    return lax.dot_general(a.astype(MMD), b.astype(MMD), dn, preferred_element_type=F32)


def _dot_nt(a, b):
    return _dot(a, b, (((1,), (1,)), ((), ())))


def _dot_tn(a, b):
    return _dot(a, b, (((0,), (0,)), ((), ())))


def _dot_exact01(x, g01):
    hi = x.astype(BF16)
    lo = (x - hi.astype(F32)).astype(BF16)
    dn = (((1,), (0,)), ((), ()))
    return (lax.dot_general(hi, g01, dn, preferred_element_type=F32)
            + lax.dot_general(lo, g01, dn, preferred_element_type=F32))


def _fold_heads(v, fold):
    return _row_pick(_dot_exact01(jnp.broadcast_to(v, (8, v.shape[1])), fold), 0)


def _group_ones(width):
    idx = np.arange(width) // HEAD_DIM
    return jnp.asarray((idx[:, None] == idx[None, :]).astype(np.float32), BF16)


def _fold_ones(width):
    idx = np.arange(width) % HEAD_DIM
    return jnp.asarray((idx[:, None] == np.arange(HEAD_DIM)[None, :]).astype(np.float32), BF16)


def _head_masks(width):
    idx = np.arange(width) // HEAD_DIM
    return jnp.asarray((idx[None, :] == np.arange(width // HEAD_DIM)[:, None]).astype(np.float32)[:, None, :], F32)


def _rms_stats(x):
    r = lax.rsqrt(jnp.mean(x * x, axis=-1, keepdims=True) + RMS_EPS)
    return r, x * r


def rms_fwd(x, gains, *, name):
    T, D = x.shape

    def fn(i, nblk, xb, *gs):
        _, xh = _rms_stats(xb)
        return tuple(xh * g for g in gs)

    return _rowmap(fn, name=name, T=T, tb=512, ins=[("row", x)] + [("const", g) for g in gains],
                   outs=[("row", D, MMD)] * len(gains))


def rms_bwd(x, gains, dhs, dres, *, name):
    T, D = x.shape
    n = len(gains)

    def fn(i, nblk, xb, dr, *rest):
        gs, ds = rest[:n], rest[n:]
        r, xh = _rms_stats(xb)
        dx = dr
        dgs = []
        for g, dh in zip(gs, ds):
            dgs.append(jnp.sum(dh * xh, axis=0, keepdims=True))
            dxh = dh * g
            dx = dx + r * (dxh - xh * jnp.mean(dxh * xh, axis=-1, keepdims=True))
        return (dx, *dgs)

    return _rowmap(fn, name=name, T=T, tb=512,
                   ins=[("row", x), ("row", dres)] + [("const", g) for g in gains] + [("row", d) for d in dhs],
                   outs=[("row", D, F32)], accs=[((1, D), F32)] * n)


def _pre1_common(i, ps, halo, mu, w0, a0, w2p, a2p, g2p):
    prev = _shift_down(ps, _row_pick(halo, 7), i == 0)
    xs = ps + (prev - ps) * mu
    lo = xs[:, 3 * RWKV_WIDTH:]
    tl, sl = jnp.tanh(lo), jax.nn.sigmoid(lo)
    dec = w0 + _dot(tl, w2p)
    ain = a0 + _dot(lo, a2p)
    g = _dot(sl, g2p)
    wl = -jax.nn.softplus(-dec) - 0.5
    w = jnp.exp(-jnp.exp(wl))
    a = jax.nn.sigmoid(ain)
    return prev, xs, lo, tl, sl, dec, wl, w, a, g


def rwkv_pre_fwd(p, mu, w0, a0, w2p, a2p, g2p):
    T = p.shape[0]

    def fn(i, nblk, ps, halo, mu, w0, a0, w2p, a2p, g2p):
        _, xs, _, _, _, _, _, w, a, g = _pre1_common(i, ps, halo, mu, w0, a0, w2p, a2p, g2p)
        W = RWKV_WIDTH
        return xs[:, :W], xs[:, W:2 * W], xs[:, 2 * W:3 * W], w, a, g

    return _rowmap(fn, name="rwkv_pre_fwd", T=T, tb=256,
                   ins=[("row", p, SHIFT_WIDTH, 0), ("prev", p, SHIFT_WIDTH, 0)]
                   + [("const", c) for c in (mu, w0, a0, w2p, a2p, g2p)],
                   outs=[("row", RWKV_WIDTH, F32)] * 6)


def rwkv_pre_bwd(p, mu, w0, a0, w2p, a2p, g2p, dr, dk, dv, dw, da, dg):
    T = p.shape[0]

    def fn(i, nblk, ps, halo, mu, w0, a0, w2p, a2p, g2p, dr, dk, dv, dw, da, dg):
        prev, xs, lo, tl, sl, dec, wl, w, a, g = _pre1_common(i, ps, halo, mu, w0, a0, w2p, a2p, g2p)
        ddec = dw * (-w * jnp.exp(wl)) * jax.nn.sigmoid(-dec)
        dain = da * a * (1.0 - a)
        dlo = (_dot_nt(ddec, w2p) * (1.0 - tl * tl) + _dot_nt(dain, a2p) + _dot_nt(dg, g2p) * sl * (1.0 - sl))
        dxs = jnp.concatenate([dr, dk, dv, dlo], axis=1)
        dmu = jnp.sum(dxs * (prev - ps), axis=0, keepdims=True)
        return (dxs, dmu, jnp.sum(ddec, axis=0, keepdims=True), jnp.sum(dain, axis=0, keepdims=True),
                _dot_tn(tl, ddec), _dot_tn(lo, dain), _dot_tn(sl, dg))

    return _rowmap(fn, name="rwkv_pre_bwd", T=T, tb=256,
                   ins=[("row", p, SHIFT_WIDTH, 0), ("prev", p, SHIFT_WIDTH, 0)]
                   + [("const", c) for c in (mu, w0, a0, w2p, a2p, g2p)]
                   + [("row", c) for c in (dr, dk, dv, dw, da, dg)],
                   outs=[("row", SHIFT_WIDTH, F32)],
                   accs=[((1, SHIFT_WIDTH), F32), ((1, RWKV_WIDTH), F32), ((1, RWKV_WIDTH), F32)]
                   + [((LORA_WIDTH, RWKV_WIDTH), F32)] * 3)


def shift_bwd(dxs, mu, dq_mem):
    T = dxs.shape[0]

    def fn(i, nblk, d, halo, mu, dq):
        nxt = _shift_up(d, _row_pick(halo, 0), i == nblk - 1)
        return jnp.concatenate([d * (1.0 - mu) + nxt * mu, dq], axis=1)

    return _rowmap(fn, name="shift_bwd", T=T, tb=256,
                   ins=[("row", dxs), ("next", dxs), ("const", mu), ("row", dq_mem)],
                   outs=[("row", SHIFT_WIDTH + MEM_WIDTH, MMD)])[0]


def _scan_prep(k, a, k_k, k_a):
    kk = k * k_k
    kk = kk / jnp.maximum(jnp.sqrt(jnp.sum(kk * kk, axis=-1, keepdims=True)), 1e-12)
    k2 = k * (1.0 + (a - 1.0) * k_a)
    return kk, k2, kk * a


def _scan_post(y, r, k2, v, lnx_w, lnx_b, r_k):
    m = jnp.mean(y, axis=-1, keepdims=True)
    yc = y - m
    var = jnp.mean(yc * yc, axis=-1, keepdims=True)
    yn = yc * lax.rsqrt(var + LNX_EPS) * lnx_w + lnx_b
    return yn + jnp.sum(r * k2 * r_k, axis=-1, keepdims=True) * v


def _eye():
    n = HEAD_DIM
    return (lax.broadcasted_iota(jnp.int32, (n, n), 0) == lax.broadcasted_iota(jnp.int32, (n, n), 1)).astype(F32)


def rwkv_scan_fwd(r, w, k, v, a, k_k, k_a, r_k, lnx_w, lnx_b):
    H, T, N = r.shape
    tc = SCAN_CHUNK
    seq = pl.BlockSpec((H, tc, N), lambda i: (0, i, 0))
    par = pl.BlockSpec((H, 1, N), lambda i: (0, 0, 0))

    def body(r_ref, w_ref, k_ref, v_ref, a_ref, kk_p, ka_p, rk_p, lw_p, lb_p, o_ref, y_ref, st_ref,
             s_scr, kk_scr, k2_scr, kka_scr):
        @pl.when(pl.program_id(0) == 0)
        def _():
            s_scr[...] = jnp.zeros(s_scr.shape, F32)

        kk, k2, kka = _scan_prep(k_ref[...], a_ref[...], kk_p[...], ka_p[...])
        kk_scr[...] = kk
        k2_scr[...] = k2
        kka_scr[...] = kka
        eye = _eye()

        def step(t, carry):
            row = lambda ref: ref[:, pl.ds(t, 1), :]
            S = s_scr[...]
            sa = -jnp.sum(S * row(kk_scr), axis=-1, keepdims=True)
            v_col = jnp.sum(eye * row(v_ref), axis=-1, keepdims=True)
            S2 = S * row(w_ref) + sa * row(kka_scr) + v_col * row(k2_scr)
            y_col = jnp.sum(S2 * row(r_ref), axis=-1, keepdims=True)
            y_ref[:, pl.ds(t, 1), :] = jnp.sum(eye * y_col, axis=-2, keepdims=True)
            s_scr[...] = S2
            st_ref[t] = S2
            return carry

        lax.fori_loop(0, tc, step, 0)
        o_ref[...] = _scan_post(y_ref[...], r_ref[...], k2, v_ref[...], lw_p[...], lb_p[...], rk_p[...])

    return pl.pallas_call(
        body, name="rwkv_scan_fwd", grid=(T // tc,), in_specs=[seq] * 5 + [par] * 5,
        out_specs=[seq, seq, pl.BlockSpec((tc, H, N, N), lambda i: (i, 0, 0, 0))],
        out_shape=[jax.ShapeDtypeStruct((H, T, N), F32), jax.ShapeDtypeStruct((H, T, N), F32),
                   jax.ShapeDtypeStruct((T, H, N, N), F32)],
        scratch_shapes=[pltpu.VMEM((H, N, N), F32)] + [pltpu.VMEM((H, tc, N), F32)] * 3,
        compiler_params=_cp(("arbitrary",)),
    )(r, w, k, v, a, k_k, k_a, r_k, lnx_w, lnx_b)


def rwkv_scan_bwd(r, w, k, v, a, k_k, k_a, r_k, lnx_w, lnx_b, y, states, dmix):
    H, T, N = r.shape
    tc = SCAN_CHUNK
    nchunk = T // tc
    seq = pl.BlockSpec((H, tc, N), lambda i: (0, nchunk - 1 - i, 0))
    par = pl.BlockSpec((H, 1, N), lambda i: (0, 0, 0))
    st_spec = pl.BlockSpec((tc, H, N, N), lambda i: (nchunk - 1 - i, 0, 0, 0))
    st_prev = pl.BlockSpec((1, H, N, N), lambda i: (jnp.maximum((nchunk - 1 - i) * tc - 1, 0), 0, 0, 0))

    def body(r_ref, w_ref, k_ref, v_ref, a_ref, kk_p, ka_p, rk_p, lw_p, lb_p, y_ref, st_ref, sp_ref, dm_ref,
             dr_ref, dw_ref, dk_ref, dv_ref, da_ref, dkkp_ref, dkap_ref, drkp_ref, dlw_ref, dlb_ref,
             ds_scr, sc_scr, kk_scr, k2_scr, kka_scr, dy_scr, dkk_scr, dk2_scr, dkka_scr):
        i = pl.program_id(0)

        @pl.when(i == 0)
        def _():
            ds_scr[...] = jnp.zeros(ds_scr.shape, F32)
            for ref in (dkkp_ref, dkap_ref, drkp_ref, dlw_ref, dlb_ref):
                ref[...] = jnp.zeros(ref.shape, F32)

        rr, kr, vr, ar = r_ref[...], k_ref[...], v_ref[...], a_ref[...]
        (kk, k2, kka), prep_vjp = jax.vjp(_scan_prep, kr, ar, kk_p[...], ka_p[...])
        kk_scr[...] = kk
        k2_scr[...] = k2
        kka_scr[...] = kka
        _, post_vjp = jax.vjp(_scan_post, y_ref[...], rr, k2, vr, lw_p[...], lb_p[...], rk_p[...])
        dy, dr_b, dk2_b, dv_b, dlw, dlb, drk = post_vjp(dm_ref[...])
        dy_scr[...] = dy
        dlw_ref[...] += dlw
        dlb_ref[...] += dlb
        drkp_ref[...] += drk
        sc_scr[...] = st_ref[tc - 1]
        eye = _eye()

        def step(t, s_prev):
            row = lambda ref: ref[:, pl.ds(t, 1), :]
            s_cur = sc_scr[...]
            kk_t, kka_t, k2_t = row(kk_scr), row(kka_scr), row(k2_scr)
            dy_col = jnp.sum(eye * row(dy_scr), axis=-1, keepdims=True)
            v_col = jnp.sum(eye * row(v_ref), axis=-1, keepdims=True)
            dS = ds_scr[...] + dy_col * row(r_ref)
            dr_ref[:, pl.ds(t, 1), :] = jnp.sum(s_cur * dy_col, axis=-2, keepdims=True)
            dw_ref[:, pl.ds(t, 1), :] = jnp.sum(dS * s_prev, axis=-2, keepdims=True)
            sa = -jnp.sum(s_prev * kk_t, axis=-1, keepdims=True)
            dkka_scr[:, pl.ds(t, 1), :] = jnp.sum(dS * sa, axis=-2, keepdims=True)
            dk2_scr[:, pl.ds(t, 1), :] = jnp.sum(dS * v_col, axis=-2, keepdims=True)
            dsa = jnp.sum(dS * kka_t, axis=-1, keepdims=True)
            dv_col = jnp.sum(dS * k2_t, axis=-1, keepdims=True)
            dv_ref[:, pl.ds(t, 1), :] = jnp.sum(eye * dv_col, axis=-2, keepdims=True)
            dkk_scr[:, pl.ds(t, 1), :] = -jnp.sum(s_prev * dsa, axis=-2, keepdims=True)
            ds_scr[...] = dS * row(w_ref) - dsa * kk_t
            sc_scr[...] = s_prev

        def loop_body(j, carry):
            t = tc - 1 - j
            step(t, st_ref[t - 1])
            return carry

        lax.fori_loop(0, tc - 1, loop_body, 0)
        step(0, jnp.where(i == nchunk - 1, 0.0, 1.0) * sp_ref[0])

        dk2 = dk2_scr[...] + dk2_b
        dk, da, dkkp, dkap = prep_vjp((dkk_scr[...], dk2, dkka_scr[...]))
        dr_ref[...] += dr_b
        dv_ref[...] += dv_b
        dk_ref[...] = dk
        da_ref[...] = da
        dkkp_ref[...] += dkkp
        dkap_ref[...] += dkap

    hs = jax.ShapeDtypeStruct((H, T, N), F32)
    ps = jax.ShapeDtypeStruct((H, 1, N), F32)
    return pl.pallas_call(
        body, name="rwkv_scan_bwd", grid=(nchunk,),
        in_specs=[seq] * 5 + [par] * 5 + [seq, st_spec, st_prev, seq],
        out_specs=[seq] * 5 + [par] * 5, out_shape=[hs] * 5 + [ps] * 5,
        scratch_shapes=[pltpu.VMEM((H, N, N), F32)] * 2 + [pltpu.VMEM((H, tc, N), F32)] * 7,
        compiler_params=_cp(("arbitrary",)),
    )(r, w, k, v, a, k_k, k_a, r_k, lnx_w, lnx_b, y, states, states, dmix)


def gate_fwd(mix, g, y_mem):
    T = mix.shape[0]

    def fn(i, nblk, m, g, ym):
        return jnp.concatenate([m * g, ym], axis=1)

    return _rowmap(fn, name="gate_fwd", T=T, tb=512, ins=[("row", mix), ("row", g), ("row", y_mem)],
                   outs=[("row", RWKV_WIDTH + MEM_WIDTH, MMD)])[0]


def gate_bwd(mix, g, dycat):
    T = mix.shape[0]

    def fn(i, nblk, m, g, dy):
        return dy * g, dy * m

    return _rowmap(fn, name="gate_bwd", T=T, tb=512, ins=[("row", mix), ("row", g), ("row", dycat, RWKV_WIDTH, 0)],
                   outs=[("row", RWKV_WIDTH, F32)] * 2)


def _head_rms(x, gones):
    ms = _dot_exact01(x * x, gones) * (1.0 / HEAD_DIM)
    r = lax.rsqrt(ms + RMS_EPS)
    return r, x * r


def _head_rms_bwd(dxn_g, r, xh, gones):
    return r * (dxn_g - xh * (_dot_exact01(dxn_g * xh, gones) * (1.0 / HEAD_DIM)))


def mem_kv_fwd(mem, norm_g, w_kv, k_norm_t, *, name):
    gones = _group_ones(MEM_WIDTH)

    def body(mem_ref, g_ref, w_ref, kn_ref, go_ref, k_out, v_out):
        _, xh = _rms_stats(mem_ref[...])
        kv = _dot(xh * g_ref[...], w_ref[...])
        _, kh = _head_rms(kv[:, :MEM_WIDTH], go_ref[...])
        k_out[...] = kh * kn_ref[...]
        v_out[...] = kv[:, MEM_WIDTH:]

    return pl.pallas_call(
        body, name=name, out_shape=[jax.ShapeDtypeStruct((N_MEM, MEM_WIDTH), F32)] * 2, compiler_params=_cp(),
    )(mem, norm_g, w_kv, k_norm_t, gones)


def mem_kv_bwd(mem, norm_g, w_kv, k_norm_t, dkn, dv, *, name):
    gones, fold = _group_ones(MEM_WIDTH), _fold_ones(MEM_WIDTH)

    def body(mem_ref, g_ref, w_ref, kn_ref, go_ref, fo_ref, dkn_ref, dv_ref, dw_out, dg_out, dkg_out):
        _, xh = _rms_stats(mem_ref[...])
        hm = xh * g_ref[...]
        kv = _dot(hm, w_ref[...])
        r, kh = _head_rms(kv[:, :MEM_WIDTH], go_ref[...])
        dkn = dkn_ref[...]
        dkg_out[...] = _fold_heads(jnp.sum(dkn * kh, axis=0, keepdims=True), fo_ref[...])
        dkraw = _head_rms_bwd(dkn * kn_ref[...], r, kh, go_ref[...])
        dkv = jnp.concatenate([dkraw, dv_ref[...]], axis=1)
        dw_out[...] = _dot_tn(hm, dkv)
        dg_out[...] = jnp.sum(_dot_nt(dkv, w_ref[...]) * xh, axis=0, keepdims=True)

    return pl.pallas_call(
        body, name=name,
        out_shape=[jax.ShapeDtypeStruct((D_MODEL, 2 * MEM_WIDTH), F32), jax.ShapeDtypeStruct((1, D_MODEL), F32),
                   jax.ShapeDtypeStruct((1, HEAD_DIM), F32)],
        compiler_params=_cp(),
    )(mem, norm_g, w_kv, k_norm_t, gones, fold, dkn, dv)


def _mem_scores(qn, kn, masks, h):
    s = _dot_nt(qn * masks[h], kn) * (1.0 / math.sqrt(HEAD_DIM))
    s = s - jnp.max(s, axis=-1, keepdims=True)
    e = jnp.exp(s)
    return e / jnp.sum(e, axis=-1, keepdims=True)


def mem_attn_fwd(p, colblock, kn, v, q_norm_t, *, name):
    T = p.shape[0]
    gones, masks = _group_ones(MEM_WIDTH), _head_masks(MEM_WIDTH)

    def fn(i, nblk, q, kn, v, qg, go, masks):
        _, qh = _head_rms(q, go)
        qn = qh * qg
        out = jnp.zeros(q.shape, F32)
        for h in range(MEM_WIDTH // HEAD_DIM):
            out = out + _dot(_mem_scores(qn, kn, masks, h), v * masks[h])
        return out

    return _rowmap(fn, name=name, T=T, tb=512,
                   ins=[("row", p, MEM_WIDTH, colblock)] + [("const", c) for c in (kn, v, q_norm_t, gones, masks)],
                   outs=[("row", MEM_WIDTH, F32)])[0]


def mem_attn_bwd(p, colblock, kn, v, q_norm_t, dycat, dcolblock, *, name):
    T = p.shape[0]
    gones, masks, fold = _group_ones(MEM_WIDTH), _head_masks(MEM_WIDTH), _fold_ones(MEM_WIDTH)
    scale = 1.0 / math.sqrt(HEAD_DIM)

    def fn(i, nblk, q, dy, kn, v, qg, go, masks, fo):
        r, qh = _head_rms(q, go)
        qn = qh * qg
        dqn = jnp.zeros(q.shape, F32)
        dkn = jnp.zeros(kn.shape, F32)
        dv = jnp.zeros(v.shape, F32)
        for h in range(MEM_WIDTH // HEAD_DIM):
            pr = _mem_scores(qn, kn, masks, h)
            dyh = dy * masks[h]
            dpr = _dot_nt(dyh, v)
            ds = pr * (dpr - jnp.sum(dpr * pr, axis=-1, keepdims=True)) * scale
            dqn = dqn + _dot(ds, kn * masks[h])
            dkn = dkn + _dot_tn(ds, qn * masks[h])
            dv = dv + _dot_tn(pr, dyh)
        dqg = _fold_heads(jnp.sum(dqn * qh, axis=0, keepdims=True), fo)
        return _head_rms_bwd(dqn * qg, r, qh, go), dkn, dv, dqg

    return _rowmap(fn, name=name, T=T, tb=512,
                   ins=[("row", p, MEM_WIDTH, colblock), ("row", dycat, MEM_WIDTH, dcolblock)]
                   + [("const", c) for c in (kn, v, q_norm_t, gones, masks, fold)],
                   outs=[("row", MEM_WIDTH, F32)],
                   accs=[((N_MEM, MEM_WIDTH), F32), ((N_MEM, MEM_WIDTH), F32), ((1, HEAD_DIM), F32)])


def _ffn_conv(i, u, halo, cw, cb):
    up1 = _shift_down(u, _row_pick(halo, 7), i == 0)
    up2 = _shift_down(up1, _row_pick(halo, 6), i == 0)
    c = cb + cw[0] * up2 + cw[1] * up1 + cw[2] * u
    return up1, up2, c[:, :D_FF], c[:, D_FF:]


def ffn_act_fwd(u, cw, cb, *, name):
    T = u.shape[0]

    def fn(i, nblk, u, halo, c0, c1, c2, cb):
        _, _, gate, val = _ffn_conv(i, u, halo, (c0, c1, c2), cb)
        return jax.nn.silu(gate) * val

    return _rowmap(fn, name=name, T=T, tb=128, ins=[("row", u), ("prev", u)] + [("const", c) for c in (*cw, cb)],
                   outs=[("row", D_FF, MMD)])[0]


def ffn_act_bwd(u, cw, cb, dz, *, name):
    T = u.shape[0]

    def fn(i, nblk, u, halo, c0, c1, c2, cb, dz):
        up1, up2, gate, val = _ffn_conv(i, u, halo, (c0, c1, c2), cb)
        sg = jax.nn.sigmoid(gate)
        dgate = dz * val * sg * (1.0 + gate * (1.0 - sg))
        dval = dz * gate * sg
        dc = jnp.concatenate([dgate, dval], axis=1)
        s = lambda z: jnp.sum(z, axis=0, keepdims=True)
        return dc, s(dc * up2), s(dc * up1), s(dc * u), s(dc)

    return _rowmap(fn, name=name, T=T, tb=128,
                   ins=[("row", u), ("prev", u)] + [("const", c) for c in (*cw, cb)] + [("row", dz)],
                   outs=[("row", 2 * D_FF, F32)], accs=[((1, 2 * D_FF), F32)] * 4)


def conv_bwd(dc, cw, *, name):
    T = dc.shape[0]

    def fn(i, nblk, d, halo, c0, c1, c2):
        last = i == nblk - 1
        n1 = _shift_up(d, _row_pick(halo, 0), last)
        n2 = _shift_up(n1, _row_pick(halo, 1), last)
        return c2 * d + c1 * n1 + c0 * n2

    return _rowmap(fn, name=name, T=T, tb=128, ins=[("row", dc), ("next", dc)] + [("const", c) for c in cw],
                   outs=[("row", 2 * D_FF, MMD)])[0]


def _rope_swap(z):
    lane = lax.broadcasted_iota(jnp.int32, z.shape, 1) % HEAD_DIM
    w = z.shape[1]
    return jnp.where(lane < HEAD_DIM // 2, pltpu.roll(z, w - HEAD_DIM // 2, axis=1), pltpu.roll(z, HEAD_DIM // 2, axis=1))


def rope_tables(T):
    inv = ROPE_THETA ** (-jnp.arange(0, HEAD_DIM, 2, dtype=F32) / HEAD_DIM)
    ang = jnp.arange(T, dtype=F32)[:, None] * inv[None, :]
    cos, sin = jnp.cos(ang), jnp.sin(ang)
    c = jnp.tile(jnp.concatenate([cos, cos], axis=1), (1, DIL_WIDTH // HEAD_DIM))
    s = jnp.tile(jnp.concatenate([-sin, sin], axis=1), (1, DIL_WIDTH // HEAD_DIM))
    return c, s


def qk_fwd(kvp, pb, kg_t, qg_t, cos, sin):
    T = kvp.shape[0]
    gones = _group_ones(DIL_WIDTH)

    def fn(i, nblk, kraw, vraw, qraw, kg, qg, c, s, go):
        outs = []
        for raw, g in ((qraw, qg), (kraw, kg)):
            _, xh = _head_rms(raw, go)
            z = xh * g
            outs.append(z * c + _rope_swap(z) * s)
        return outs[0], outs[1], vraw

    return _rowmap(fn, name="qk_fwd", T=T, tb=256,
                   ins=[("row", kvp, DIL_WIDTH, 0), ("row", kvp, DIL_WIDTH, 1), ("row", pb, DIL_WIDTH, 0)]
                   + [("const", kg_t), ("const", qg_t), ("row", cos), ("row", sin), ("const", gones)],
                   outs=[("row", DIL_WIDTH, MMD)] * 3)


def qk_bwd(kvp, pb, kg_t, qg_t, cos, sin, dq, dk, dv, dq_mem):
    T = kvp.shape[0]
    gones, fold = _group_ones(DIL_WIDTH), _fold_ones(DIL_WIDTH)

    def fn(i, nblk, kraw, qraw, kg, qg, c, s, go, fo, dq, dk, dv, dqm):
        res, dgs = [], []
        for raw, g, d in ((qraw, qg, dq), (kraw, kg, dk)):
            r, xh = _head_rms(raw, go)
            dz = d * c + _rope_swap(d * s)
            dgs.append(_fold_heads(jnp.sum(dz * xh, axis=0, keepdims=True), fo))
            res.append(_head_rms_bwd(dz * g, r, xh, go))
        return (jnp.concatenate([res[0], dqm], axis=1), jnp.concatenate([res[1], dv], axis=1), dgs[0], dgs[1])

    return _rowmap(fn, name="qk_bwd", T=T, tb=256,
                   ins=[("row", kvp, DIL_WIDTH, 0), ("row", pb, DIL_WIDTH, 0), ("const", kg_t), ("const", qg_t),
                        ("row", cos), ("row", sin), ("const", gones), ("const", fold),
                        ("row", dq), ("row", dk), ("row", dv), ("row", dq_mem)],
                   outs=[("row", DIL_WIDTH + MEM_WIDTH, MMD), ("row", 2 * DIL_WIDTH, MMD)],
                   accs=[((1, HEAD_DIM), F32)] * 2)


def _band(kind):
    i = lax.broadcasted_iota(jnp.int32, (DIL_BLOCK, DIL_BLOCK), 0)
    j = lax.broadcasted_iota(jnp.int32, (DIL_BLOCK, DIL_BLOCK), 1)
    return (j <= i) if kind == "cur" else (j >= i)


def dil_attn_fwd(q, k, v, seq_blocks, *, name):
    T, W = q.shape
    nb = T // DIL_BLOCK
    masks = _head_masks(W)
    cur = pl.BlockSpec((DIL_BLOCK, W), lambda n: (n, 0))
    prv = pl.BlockSpec((DIL_BLOCK, W), lambda n: (jnp.maximum(n - 1, 0), 0))
    scale = 1.0 / math.sqrt(HEAD_DIM)

    def body(q_ref, kc_ref, kp_ref, vc_ref, vp_ref, m_ref, o_ref, l_ref):
        n = pl.program_id(0)
        has_prev = (n % seq_blocks) != 0
        q = q_ref[...].astype(F32)
        kc, kp = kc_ref[...].astype(F32), kp_ref[...].astype(F32)
        vc, vp = vc_ref[...].astype(F32), vp_ref[...].astype(F32)
        ok_c = _band("cur")
        ok_p = jnp.logical_and(_band("prev"), has_prev)
        o = jnp.zeros((DIL_BLOCK, W), F32)
        lse = jnp.zeros((DIL_BLOCK, W), F32)
        for h in range(W // HEAD_DIM):
            mh = m_ref[h]
            qh = q * mh
            sc = jnp.where(ok_c, _dot_nt(qh, kc) * scale, NEG_INF)
            sp = jnp.where(ok_p, _dot_nt(qh, kp) * scale, NEG_INF)
            mx = jnp.maximum(jnp.max(sc, axis=-1, keepdims=True), jnp.max(sp, axis=-1, keepdims=True))
            ec, ep = jnp.exp(sc - mx), jnp.exp(sp - mx)
            den = jnp.sum(ec, axis=-1, keepdims=True) + jnp.sum(ep, axis=-1, keepdims=True)
            o = o + (_dot(ec, vc * mh) + _dot(ep, vp * mh)) / den
            lse = lse + (mx + jnp.log(den)) * mh
        o_ref[...] = o
        l_ref[...] = lse

    return pl.pallas_call(
        body, name=name, grid=(nb,), in_specs=[cur, cur, prv, cur, prv, pl.BlockSpec(masks.shape, lambda n: (0, 0, 0))],
        out_specs=[cur, cur], out_shape=[jax.ShapeDtypeStruct((T, W), F32)] * 2,
        compiler_params=_cp(("parallel",)),
    )(q, k, k, v, v, masks)


def dil_attn_bwd(q, k, v, o, lse, do, dlse, seq_blocks, *, name):
    T, W = q.shape
    nb = T // DIL_BLOCK
    masks = _head_masks(W)
    cur = pl.BlockSpec((DIL_BLOCK, W), lambda n: (n, 0))
    prv = pl.BlockSpec((DIL_BLOCK, W), lambda n: (jnp.maximum(n - 1, 0), 0))
    nxt = pl.BlockSpec((DIL_BLOCK, W), lambda n: (jnp.minimum(n + 1, nb - 1), 0))
    scale = 1.0 / math.sqrt(HEAD_DIM)

    def body(qc_ref, qn_ref, kc_ref, kp_ref, vc_ref, vp_ref, oc_ref, on_ref, lc_ref, ln_ref, doc_ref, don_ref,
             dlc_ref, dln_ref, m_ref, dq_ref, dk_ref, dv_ref):
        n = pl.program_id(0)
        has_prev = (n % seq_blocks) != 0
        has_next = jnp.logical_and(((n + 1) % seq_blocks) != 0, n + 1 < nb)
        f = lambda ref: ref[...].astype(F32)
        qc, qn, kc, kp, vc, vp = f(qc_ref), f(qn_ref), f(kc_ref), f(kp_ref), f(vc_ref), f(vp_ref)
        doc, don = doc_ref[...], don_ref[...]
        ok_c = _band("cur")
        ok_p = jnp.logical_and(_band("prev"), has_prev)
        ok_n = jnp.logical_and(_band("prev"), has_next)
        dq = jnp.zeros((DIL_BLOCK, W), F32)
        dk = jnp.zeros((DIL_BLOCK, W), F32)
        dv = jnp.zeros((DIL_BLOCK, W), F32)

        def side(qh, kk, vv, doh, lse_h, corr, ok):
            s = _dot_nt(qh, kk) * scale
            pr = jnp.where(ok, jnp.exp(jnp.where(ok, s, NEG_INF) - lse_h), 0.0)
            ds = pr * (_dot_nt(doh, vv) + corr) * scale
            return pr, ds

        for h in range(W // HEAD_DIM):
            mh = m_ref[h]
            red = lambda z: jnp.sum(z * mh, axis=-1, keepdims=True)
            qh, doh = qc * mh, doc * mh
            lse_h = red(lc_ref[...]) * (1.0 / HEAD_DIM)
            corr = red(dlc_ref[...]) - red(doc * oc_ref[...])
            pr_c, ds_c = side(qh, kc, vc * mh, doh, lse_h, corr, ok_c)
            _, ds_p = side(qh, kp, vp * mh, doh, lse_h, corr, ok_p)
            dq = dq + _dot(ds_c, kc * mh) + _dot(ds_p, kp * mh)
            dk = dk + _dot_tn(ds_c, qh)
            dv = dv + _dot_tn(pr_c, doh)
            qh2, doh2 = qn * mh, don * mh
            lse_2 = red(ln_ref[...]) * (1.0 / HEAD_DIM)
            corr2 = red(dln_ref[...]) - red(don * on_ref[...])
            pr_n, ds_n = side(qh2, kc, vc * mh, doh2, lse_2, corr2, ok_n)
            dk = dk + _dot_tn(ds_n, qh2)
            dv = dv + _dot_tn(pr_n, doh2)
        dq_ref[...] = dq
        dk_ref[...] = dk
        dv_ref[...] = dv

    return pl.pallas_call(
        body, name=name, grid=(nb,),
        in_specs=[cur, nxt, cur, prv, cur, prv, cur, nxt, cur, nxt, cur, nxt, cur, nxt,
                  pl.BlockSpec(masks.shape, lambda n: (0, 0, 0))],
        out_specs=[cur] * 3, out_shape=[jax.ShapeDtypeStruct((T, W), F32)] * 3,
        compiler_params=_cp(("parallel",)),
    )(q, q, k, k, v, v, o, o, lse, lse, do, do, dlse, dlse, masks)


def _mix_weights(ls):
    m = jnp.maximum(jnp.maximum(ls[0], ls[1]), ls[2])
    es = [jnp.exp(l - m) for l in ls]
    den = es[0] + es[1] + es[2]
    return [e / den for e in es]


def mix_fwd(os_, ls, y_mem):
    T = y_mem.shape[0]

    def fn(i, nblk, o0, o1, o2, l0, l1, l2, ym):
        w = _mix_weights((l0, l1, l2))
        return jnp.concatenate([w[0] * o0 + w[1] * o1 + w[2] * o2, ym], axis=1)

    return _rowmap(fn, name="mix_fwd", T=T, tb=512, ins=[("row", z) for z in (*os_, *ls, y_mem)],
                   outs=[("row", 2 * MEM_WIDTH, MMD)])[0]


def mix_bwd(os_, ls, dycat):
    T = dycat.shape[0]

    def fn(i, nblk, o0, o1, o2, l0, l1, l2, dy):
        w = _mix_weights((l0, l1, l2))
        os3 = (o0, o1, o2)
        dws = [dy * o for o in os3]
        tot = w[0] * dws[0] + w[1] * dws[1] + w[2] * dws[2]
        return tuple(wg * dy for wg in w) + tuple(wg * (dw - tot) for wg, dw in zip(w, dws))

    return _rowmap(fn, name="mix_bwd", T=T, tb=512,
                   ins=[("row", z) for z in (*os_, *ls)] + [("row", dycat, MEM_WIDTH, 0)],
                   outs=[("row", MEM_WIDTH, F32)] * 6)


def loss_fwd_bwd(y, target):
    T, D = y.shape

    def fn(i, nblk, y, t):
        e = y - t
        return e * (1.0 / D), jnp.zeros((8, 128), F32) + jnp.sum(e * e) * (0.5 / D)

    return _rowmap(fn, name="loss", T=T, tb=512, ins=[("row", y), ("row", target)], outs=[("row", D, F32)],
                   accs=[((8, 128), F32)])


def _to_heads(z):
    T = z.shape[0]
    return z.reshape(T, -1, HEAD_DIM).transpose(1, 0, 2)


def _from_heads(z):
    return z.transpose(1, 0, 2).reshape(z.shape[1], -1)


def _to_residues(z, dil):
    T, W = z.shape
    return z.reshape(T // dil, dil, W).transpose(1, 0, 2).reshape(T, W)


def _from_residues(z, dil):
    T, W = z.shape
    return z.reshape(dil, T // dil, W).transpose(1, 0, 2).reshape(T, W)


def _pad_rows(w, rows):
    return jnp.concatenate([w, jnp.zeros((rows - w.shape[0], w.shape[1]), w.dtype)], axis=0)


def _tile_heads(g, width):
    return jnp.tile(g.reshape(1, HEAD_DIM), (1, width // HEAD_DIM))


def _conv_rows(W, i):
    return [W["ffn_conv_w"][i][j:j + 1] for j in range(3)]


def _ffn_fwd(x, i, W):
    hn = rms_fwd(x, [W["ffn_norm"][i:i + 1]], name=f"ffn_rms{i}")[0]
    u = _mm(hn, W["ffn_w_up"][i], name=f"ffn_up{i}")
    z = ffn_act_fwd(u, _conv_rows(W, i), W["ffn_conv_b"][i:i + 1], name=f"ffn_act{i}")
    out = _mm(z, W["ffn_w_down"][i], add=x, name=f"ffn_down{i}")
    return out, (x, hn, u, z)


def _ffn_bwd(dout, i, W, saved, G):
    x, hn, u, z = saved
    dz = _mm(dout, W["ffn_w_down"][i], tb=True, name=f"ffn_down_dx{i}")
    G["ffn_w_down"][i] = _mm(z, dout, ta=True, name=f"ffn_down_dw{i}")
    dc, dw0, dw1, dw2, db = ffn_act_bwd(u, _conv_rows(W, i), W["ffn_conv_b"][i:i + 1], dz, name=f"ffn_act_bwd{i}")
    G["ffn_conv_w"][i] = jnp.concatenate([dw0, dw1, dw2], axis=0)
    G["ffn_conv_b"][i] = db[0]
    du = conv_bwd(dc, _conv_rows(W, i), name=f"ffn_conv_bwd{i}")
    dhn = _mm(du, W["ffn_w_up"][i], tb=True, name=f"ffn_up_dx{i}")
    G["ffn_w_up"][i] = _mm(hn, du, ta=True, name=f"ffn_up_dw{i}")
    dx, dg = rms_bwd(x, [W["ffn_norm"][i:i + 1]], [dhn], dout, name=f"ffn_rms_bwd{i}")
    G["ffn_norm"][i] = dg[0]
    return dx


def local_step(x, mem, target, W):
    T = x.shape[0]
    G = {"ffn_w_down": [None, None], "ffn_w_up": [None, None], "ffn_conv_w": [None, None],
         "ffn_conv_b": [None, None], "ffn_norm": [None, None], "attn_norm": [None, None], "mem_norm": [None, None],
         "mem_w_kv": [None, None], "mem_q_norm": [None, None], "mem_k_norm": [None, None]}
    hp = lambda n: W[n][0].reshape(RWKV_HEADS, 1, HEAD_DIM)
    mu, w0, a0 = W["a_mu"], W["a_w0"], W["a_a0"]
    w2p, a2p, g2p = (_pad_rows(W["a_w2"][0], LORA_WIDTH),
                     jnp.concatenate([jnp.zeros((64, RWKV_WIDTH), MMD), W["a_a2"][0],
                                      jnp.zeros((128, RWKV_WIDTH), MMD)], axis=0),
                     jnp.concatenate([jnp.zeros((128, RWKV_WIDTH), MMD), W["a_g2"][0]], axis=0))
    k_k, k_a, lnx_w, lnx_b = hp("a_k_k"), hp("a_k_a"), hp("a_lnx_w"), hp("a_lnx_b")
    r_k = W["a_r_k"][0].reshape(RWKV_HEADS, 1, HEAD_DIM)
    memkv = []
    for i in range(2):
        memkv.append(mem_kv_fwd(mem, W["mem_norm"][i:i + 1], W["mem_w_kv"][i], _tile_heads(W["mem_k_norm"][i], MEM_WIDTH),
                                name=f"mem_kv{i}"))

    h0 = rms_fwd(x, [W["attn_norm"][0:1]], name="attn_rms0")[0]
    p = _mm(h0, W["a_w_in"][0], name="a_in")
    r, k, v, w, a, g = rwkv_pre_fwd(p, mu, w0, a0, w2p, a2p, g2p)
    rh, kh, vh, wh, ah = (_to_heads(z) for z in (r, k, v, w, a))
    mixh, yh, states = rwkv_scan_fwd(rh, wh, kh, vh, ah, k_k, k_a, r_k, lnx_w, lnx_b)
    mix = _from_heads(mixh)
    qg0 = _tile_heads(W["mem_q_norm"][0], MEM_WIDTH)
    y_mem0 = mem_attn_fwd(p, SHIFT_WIDTH // MEM_WIDTH, memkv[0][0], memkv[0][1], qg0, name="mem_attn0")
    ycat0 = gate_fwd(mix, g, y_mem0)
    x1 = _mm(ycat0, W["a_w_out"][0], add=x, name="a_out")
    x2, ffn0 = _ffn_fwd(x1, 0, W)

    h1, hkv = rms_fwd(x2, [W["attn_norm"][1:2], W["kv_norm"].reshape(1, -1)], name="attn_rms1")
    kvp = _mm(hkv, W["kv_w"], name="kv_in")
    pb = _mm(h1, W["b_w_in"][0], name="b_in")
    cos, sin = rope_tables(T)
    kg_t, qg_t = _tile_heads(W["kv_k_norm"], DIL_WIDTH), _tile_heads(W["b_q_norm"][0], DIL_WIDTH)
    q, ksh, vsh = qk_fwd(kvp, pb, kg_t, qg_t, cos, sin)
    os_, ls, grp = [], [], []
    for gi, (win, dil) in enumerate(DIL_GROUPS):
        sl = slice(gi * MEM_WIDTH, (gi + 1) * MEM_WIDTH)
        qg_, kg_, vg_ = (_to_residues(z[:, sl], dil) for z in (q, ksh, vsh))
        o_r, l_r = dil_attn_fwd(qg_, kg_, vg_, T // dil // DIL_BLOCK, name=f"dil_fwd{gi}")
        grp.append((qg_, kg_, vg_, o_r, l_r))
        os_.append(_from_residues(o_r, dil))
        ls.append(_from_residues(l_r, dil))
    qg1 = _tile_heads(W["mem_q_norm"][1], MEM_WIDTH)
    y_mem1 = mem_attn_fwd(pb, DIL_WIDTH // MEM_WIDTH, memkv[1][0], memkv[1][1], qg1, name="mem_attn1")
    ycat1 = mix_fwd(os_, ls, y_mem1)
    x3 = _mm(ycat1, W["b_w_out"][0], add=x2, name="b_out")
    x4, ffn1 = _ffn_fwd(x3, 1, W)

    dx4, loss = loss_fwd_bwd(x4, target)

    dx3 = _ffn_bwd(dx4, 1, W, ffn1, G)
    dycat1 = _mm(dx3, W["b_w_out"][0], tb=True, name="b_out_dx")
    G["b_w_out"] = _mm(ycat1, dx3, ta=True, name="b_out_dw")[None]
    dq_mem1, dkn1, dvm1, dqg1 = mem_attn_bwd(pb, DIL_WIDTH // MEM_WIDTH, memkv[1][0], memkv[1][1], qg1, dycat1, 1,
                                             name="mem_attn_bwd1")
    G["mem_q_norm"][1] = dqg1[0]
    d_os_ls = mix_bwd(os_, ls, dycat1)
    dqs, dks, dvs = [], [], []
    for gi, (win, dil) in enumerate(DIL_GROUPS):
        qg_, kg_, vg_, o_r, l_r = grp[gi]
        do_r, dl_r = _to_residues(d_os_ls[gi], dil), _to_residues(d_os_ls[3 + gi], dil)
        dq_r, dk_r, dv_r = dil_attn_bwd(qg_, kg_, vg_, o_r, l_r, do_r, dl_r, T // dil // DIL_BLOCK, name=f"dil_bwd{gi}")
        dqs.append(_from_residues(dq_r, dil))
        dks.append(_from_residues(dk_r, dil))
        dvs.append(_from_residues(dv_r, dil))
    dq, dk, dv = (jnp.concatenate(z, axis=1) for z in (dqs, dks, dvs))
    dpb, dkvp, dqn_g, dkn_g = qk_bwd(kvp, pb, kg_t, qg_t, cos, sin, dq, dk, dv, dq_mem1)
    G["b_q_norm"] = dqn_g
    G["kv_k_norm"] = dkn_g[0]
    dh1 = _mm(dpb, W["b_w_in"][0], tb=True, name="b_in_dx")
    G["b_w_in"] = _mm(h1, dpb, ta=True, name="b_in_dw")[None]
    dhkv = _mm(dkvp, W["kv_w"], tb=True, name="kv_in_dx")
    G["kv_w"] = _mm(hkv, dkvp, ta=True, name="kv_in_dw")
    dx2, dg1, dgkv = rms_bwd(x2, [W["attn_norm"][1:2], W["kv_norm"].reshape(1, -1)], [dh1, dhkv], dx3,
                             name="attn_rms_bwd1")
    G["attn_norm"][1] = dg1[0]
    G["kv_norm"] = dgkv[0]

    dx1 = _ffn_bwd(dx2, 0, W, ffn0, G)
    dycat0 = _mm(dx1, W["a_w_out"][0], tb=True, name="a_out_dx")
    G["a_w_out"] = _mm(ycat0, dx1, ta=True, name="a_out_dw")[None]
    dq_mem0, dkn0, dvm0, dqg0 = mem_attn_bwd(p, SHIFT_WIDTH // MEM_WIDTH, memkv[0][0], memkv[0][1], qg0, dycat0,
                                             RWKV_WIDTH // MEM_WIDTH, name="mem_attn_bwd0")
    G["mem_q_norm"][0] = dqg0[0]
    dmix, dg = gate_bwd(mix, g, dycat0)
    drh, dwh, dkh, dvh, dah, dkk_p, dka_p, drk_p, dlw_p, dlb_p = rwkv_scan_bwd(
        rh, wh, kh, vh, ah, k_k, k_a, r_k, lnx_w, lnx_b, yh, states, _to_heads(dmix))
    dr, dw, dk_, dv_, da = (_from_heads(z) for z in (drh, dwh, dkh, dvh, dah))
    dxs, dmu, dw0, da0, dw2p, da2p, dg2p = rwkv_pre_bwd(p, mu, w0, a0, w2p, a2p, g2p, dr, dk_, dv_, dw, da, dg)
    dp = shift_bwd(dxs, mu, dq_mem0)
    flat = lambda z: z.reshape(1, RWKV_WIDTH)
    G.update(a_mu=dmu, a_w0=dw0, a_a0=da0, a_w2=dw2p[None, :64], a_a2=da2p[None, 64:128], a_g2=dg2p[None, 128:],
             a_k_k=flat(dkk_p), a_k_a=flat(dka_p), a_r_k=drk_p.reshape(1, RWKV_HEADS, HEAD_DIM),
             a_lnx_w=flat(dlw_p), a_lnx_b=flat(dlb_p))
    dh0 = _mm(dp, W["a_w_in"][0], tb=True, name="a_in_dx")
    G["a_w_in"] = _mm(h0, dp, ta=True, name="a_in_dw")[None]
    grad_x, dg0 = rms_bwd(x, [W["attn_norm"][0:1]], [dh0], dx1, name="attn_rms_bwd0")
    G["attn_norm"][0] = dg0[0]

    for i, (dkn, dvm) in enumerate(((dkn0, dvm0), (dkn1, dvm1))):
        dwkv, dgm, dkg = mem_kv_bwd(mem, W["mem_norm"][i:i + 1], W["mem_w_kv"][i],
                                    _tile_heads(W["mem_k_norm"][i], MEM_WIDTH), dkn, dvm, name=f"mem_kv_bwd{i}")
        G["mem_w_kv"][i], G["mem_norm"][i], G["mem_k_norm"][i] = dwkv, dgm[0], dkg[0]
    for n in list(G):
        if isinstance(G[n], list):
            G[n] = jnp.stack(G[n], axis=0)
    return loss, grad_x, G


HBM_SPEC = pl.BlockSpec(memory_space=pltpu.HBM)


def _mesh_pos():
    return lax.axis_index("x"), lax.axis_index("y"), lax.axis_index("c")


def _other_chips(x, y):
    return [(1 - x, y), (x, 1 - y), (1 - x, 1 - y)]


def _remote(send_sems, recv_sems, k, src, dst, to):
    return pltpu.make_async_remote_copy(src_ref=src, dst_ref=dst, send_sem=send_sems.at[k], recv_sem=recv_sems.at[k],
                                        device_id=to, device_id_type=MESH)


def _comm_call(body, name, ins, out_shape, n_remote, n_local):
    scratch = [pltpu.SemaphoreType.DMA((n_remote,)), pltpu.SemaphoreType.DMA((n_remote,))]
    if n_local:
        scratch.append(pltpu.SemaphoreType.DMA((n_local,)))
    return pl.pallas_call(body, name=name, in_specs=[HBM_SPEC] * len(ins), out_specs=[HBM_SPEC] * len(out_shape),
                          out_shape=out_shape, scratch_shapes=scratch)(*ins)


def comm_gather(wbig, wsm):
    def body(wb, ws, ob, os_, send_sems, recv_sems, loc_sems):
        x, y, c = _mesh_pos()
        s = 2 * x + y
        me, sibling = (x, y, c), (x, y, 1 - c)
        chips = _other_chips(x, y)
        rc = functools.partial(_remote, send_sems, recv_sems)
        loc = [pltpu.make_async_copy(wb, ob.at[s], loc_sems.at[0]), pltpu.make_async_copy(ws, os_.at[s], loc_sems.at[1])]
        for cp in loc:
            cp.start()
        first = []
        for j, (cx, cy) in enumerate(chips):
            first.append(rc(j, wb.at[c], ob.at[s, c], (cx, cy, c)))
            first.append(rc(6 + j, ws, os_.at[s], (cx, cy, c)))
        for cp in first:
            cp.start()
        passed = []
        for j, (cx, cy) in enumerate(chips):
            blk = ob.at[2 * cx + cy, c]
            rc(j, blk, blk, me).wait_recv()
            passed.append(rc(3 + j, blk, blk, sibling))
            passed[-1].start()
        for j, (cx, cy) in enumerate(chips):
            blk = ob.at[2 * cx + cy, 1 - c]
            rc(3 + j, blk, blk, me).wait_recv()
            sb = os_.at[2 * cx + cy]
            rc(6 + j, sb, sb, me).wait_recv()
        for cp in first + passed:
            cp.wait_send()
        for cp in loc:
            cp.wait()

    out_shape = [jax.ShapeDtypeStruct((N_CHIPS, *wbig.shape), wbig.dtype),
                 jax.ShapeDtypeStruct((N_CHIPS, *wsm.shape), wsm.dtype)]
    return _comm_call(body, "comm_gather", [wbig, wsm], out_shape, 9, 2)


def comm_pair_exchange(gb, gs):
    def body(gb_ref, gs_ref, rb_ref, rs_ref, send_sems, recv_sems):
        x, y, c = _mesh_pos()
        sibling = (x, y, 1 - c)
        rc = functools.partial(_remote, send_sems, recv_sems)
        cps = [rc(r, gb_ref.at[r, 1 - c], rb_ref.at[r], sibling) for r in range(N_CHIPS)]
        cps.append(rc(N_CHIPS, gs_ref.at[1 - c], rs_ref, sibling))
        for cp in cps:
            cp.start()
        for cp in cps:
            cp.wait()

    out_shape = [jax.ShapeDtypeStruct((N_CHIPS, *gb.shape[2:]), gb.dtype), jax.ShapeDtypeStruct(gs.shape[1:], gs.dtype)]
    return _comm_call(body, "comm_pair_exchange", [gb, gs], out_shape, N_CHIPS + 1, 0)


def comm_chip_exchange(hb, hs):
    def body(hb_ref, hs_ref, qb_ref, qs_ref, send_sems, recv_sems, loc_sems):
        x, y, c = _mesh_pos()
        s = 2 * x + y
        me = (x, y, c)
        chips = _other_chips(x, y)
        rc = functools.partial(_remote, send_sems, recv_sems)
        loc = [pltpu.make_async_copy(hb_ref.at[s], qb_ref.at[s], loc_sems.at[0]),
               pltpu.make_async_copy(hs_ref, qs_ref.at[s], loc_sems.at[1])]
        for cp in loc:
            cp.start()
        cps = []
        for j, (cx, cy) in enumerate(chips):
            cps.append(rc(j, hb_ref.at[2 * cx + cy], qb_ref.at[s], (cx, cy, c)))
            cps.append(rc(3 + j, hs_ref, qs_ref.at[s], (cx, cy, c)))
        for cp in cps:
            cp.start()
        for j, (cx, cy) in enumerate(chips):
            blk = qb_ref.at[2 * cx + cy]
            rc(j, blk, blk, me).wait_recv()
            sb = qs_ref.at[2 * cx + cy]
            rc(3 + j, sb, sb, me).wait_recv()
        for cp in cps:
            cp.wait_send()
        for cp in loc:
            cp.wait()

    out_shape = [jax.ShapeDtypeStruct(hb.shape, hb.dtype), jax.ShapeDtypeStruct((N_CHIPS, *hs.shape), hs.dtype)]
    return _comm_call(body, "comm_chip_exchange", [hb, hs], out_shape, 6, 2)


def comm_pair_share(gh, gsh):
    def body(gh_ref, gs_ref, ob, os_, send_sems, recv_sems, loc_sems):
        x, y, c = _mesh_pos()
        me, sibling = (x, y, c), (x, y, 1 - c)
        rc = functools.partial(_remote, send_sems, recv_sems)
        loc = [pltpu.make_async_copy(gh_ref, ob.at[c], loc_sems.at[0]), pltpu.make_async_copy(gs_ref, os_.at[c], loc_sems.at[1])]
        cps = [rc(0, gh_ref, ob.at[c], sibling), rc(1, gs_ref, os_.at[c], sibling)]
        for cp in loc + cps:
            cp.start()
        rc(0, ob.at[1 - c], ob.at[1 - c], me).wait_recv()
        rc(1, os_.at[1 - c], os_.at[1 - c], me).wait_recv()
        for cp in cps:
            cp.wait_send()
        for cp in loc:
            cp.wait()

    out_shape = [jax.ShapeDtypeStruct((2, *gh.shape), gh.dtype), jax.ShapeDtypeStruct((2, *gsh.shape), gsh.dtype)]
    return _comm_call(body, "comm_pair_share", [gh, gsh], out_shape, 2, 2)


def add_pairs(a, b, out_dtype, *, name, tb):
    T, L = a.shape
    return _rowmap(lambda i, n, p, q: p + q, name=name, T=T, tb=tb, ins=[("row", a), ("row", b)],
                   outs=[("row", L, out_dtype)])[0]


def add_chips(parts, *, name, tb):
    T, L = parts[0].shape

    def fn(i, n, p0, p1, p2, p3):
        f = lambda z: z.astype(F32)
        return ((f(p0) + f(p1)) + f(p2)) + f(p3)

    return _rowmap(fn, name=name, T=T, tb=tb, ins=[("row", p) for p in parts], outs=[("row", L, F32)])[0]


def adamw(g, w, m, v, *, name, tb):
    T, L = g.shape

    def fn(i, n, g, w, m, v):
        m2 = ADAM_B1 * m + (1.0 - ADAM_B1) * g
        v2 = ADAM_B2 * v + (1.0 - ADAM_B2) * (g * g)
        m_hat = m2 / (1.0 - ADAM_B1 ** ADAM_STEP)
        v_hat = v2 / (1.0 - ADAM_B2 ** ADAM_STEP)
        return -ADAM_LR * (m_hat / (jnp.sqrt(v_hat) + ADAM_EPS) + ADAM_WD * w), m2, v2

    return _rowmap(fn, name=name, T=T, tb=tb, ins=[("row", z) for z in (g, w, m, v)], outs=[("row", L, F32)] * 3)


BIG_LANES = 1024
SMALL_LANES = 128


def _flat_cat(arrs, total, dtype):
    parts = [a.reshape(-1).astype(dtype) for a in arrs]
    n = sum(p.shape[0] for p in parts)
    assert n <= total, (n, total)
    if n < total:
        parts.append(jnp.zeros((total - n,), dtype))
    return jnp.concatenate(parts)


def _split_flat(flat, shapes):
    out, off = [], 0
    for shp in shapes:
        n = math.prod(shp)
        out.append(flat[off:off + n].reshape(shp))
        off += n
    return out


def _round_up(n, m):
    return -(-n // m) * m


def _full_shape(shard_shape, axis):
    return tuple(d * N_CHIPS if i == axis else d for i, d in enumerate(shard_shape))


def kernel(x, mem, attn_norm, a_w_in, a_mu, a_w0, a_w2, a_a0, a_a2, a_g2, a_k_k, a_k_a, a_r_k, a_lnx_w, a_lnx_b, a_w_out, kv_norm, kv_w, kv_k_norm, b_w_in, b_q_norm, b_w_out, mem_norm, mem_w_kv, mem_q_norm, mem_k_norm, ffn_norm, ffn_w_up, ffn_conv_w, ffn_conv_b, ffn_w_down, loss_target, m_attn_norm, m_a_w_in, m_a_mu, m_a_w0, m_a_w2, m_a_a0, m_a_a2, m_a_g2, m_a_k_k, m_a_k_a, m_a_r_k, m_a_lnx_w, m_a_lnx_b, m_a_w_out, m_kv_norm, m_kv_w, m_kv_k_norm, m_b_w_in, m_b_q_norm, m_b_w_out, m_mem_norm, m_mem_w_kv, m_mem_q_norm, m_mem_k_norm, m_ffn_norm, m_ffn_w_up, m_ffn_conv_w, m_ffn_conv_b, m_ffn_w_down, v_attn_norm, v_a_w_in, v_a_mu, v_a_w0, v_a_w2, v_a_a0, v_a_a2, v_a_g2, v_a_k_k, v_a_k_a, v_a_r_k, v_a_lnx_w, v_a_lnx_b, v_a_w_out, v_kv_norm, v_kv_w, v_kv_k_norm, v_b_w_in, v_b_q_norm, v_b_w_out, v_mem_norm, v_mem_w_kv, v_mem_q_norm, v_mem_k_norm, v_ffn_norm, v_ffn_w_up, v_ffn_conv_w, v_ffn_conv_b, v_ffn_w_down):
    args = (attn_norm, a_w_in, a_mu, a_w0, a_w2, a_a0, a_a2, a_g2, a_k_k, a_k_a, a_r_k, a_lnx_w, a_lnx_b, a_w_out, kv_norm, kv_w, kv_k_norm, b_w_in, b_q_norm, b_w_out, mem_norm, mem_w_kv, mem_q_norm, mem_k_norm, ffn_norm, ffn_w_up, ffn_conv_w, ffn_conv_b, ffn_w_down)
    ms = (m_attn_norm, m_a_w_in, m_a_mu, m_a_w0, m_a_w2, m_a_a0, m_a_a2, m_a_g2, m_a_k_k, m_a_k_a, m_a_r_k, m_a_lnx_w, m_a_lnx_b, m_a_w_out, m_kv_norm, m_kv_w, m_kv_k_norm, m_b_w_in, m_b_q_norm, m_b_w_out, m_mem_norm, m_mem_w_kv, m_mem_q_norm, m_mem_k_norm, m_ffn_norm, m_ffn_w_up, m_ffn_conv_w, m_ffn_conv_b, m_ffn_w_down)
    vs = (v_attn_norm, v_a_w_in, v_a_mu, v_a_w0, v_a_w2, v_a_a0, v_a_a2, v_a_g2, v_a_k_k, v_a_k_a, v_a_r_k, v_a_lnx_w, v_a_lnx_b, v_a_w_out, v_kv_norm, v_kv_w, v_kv_k_norm, v_b_w_in, v_b_q_norm, v_b_w_out, v_mem_norm, v_mem_w_kv, v_mem_q_norm, v_mem_k_norm, v_ffn_norm, v_ffn_w_up, v_ffn_conv_w, v_ffn_conv_b, v_ffn_w_down)
    w_sh, m_sh, v_sh = (dict(zip(WEIGHTS, z)) for z in (args, ms, vs))
    xi, yi, ci = _mesh_pos()
    chip = 2 * xi + yi
    big_names, big_axes = [n for n, _ in BIG], dict(BIG)
    ss_names, ss_axes = [n for n, _ in SMALL_SHARDED], dict(SMALL_SHARDED)
    big_shapes = [w_sh[n].shape for n in big_names]
    ss_shapes = [w_sh[n].shape for n in ss_names]
    n_big = sum(math.prod(s) for s in big_shapes)
    assert n_big % (2 * 8 * BIG_LANES) == 0
    mh = n_big // (2 * BIG_LANES)
    n_ss = _round_up(sum(math.prod(s) for s in ss_shapes), 8 * SMALL_LANES)

    wbig = _flat_cat([w_sh[n] for n in big_names], n_big, MMD).reshape(2, mh, BIG_LANES)
    wsm = _flat_cat([w_sh[n] for n in ss_names], n_ss, F32).reshape(-1, SMALL_LANES)
    wbig_all, wsm_all = comm_gather(wbig, wsm)
    wbig_all, wsm_all = wbig_all.reshape(N_CHIPS, -1), wsm_all.reshape(N_CHIPS, -1)
    W = {n: w_sh[n] for n in SMALL_REPL}
    for names, axes, shapes, allv in ((big_names, big_axes, big_shapes, wbig_all), (ss_names, ss_axes, ss_shapes, wsm_all)):
        per_chip = [_split_flat(allv[j], shapes) for j in range(N_CHIPS)]
        for k, n in enumerate(names):
            W[n] = jnp.concatenate([per_chip[j][k] for j in range(N_CHIPS)], axis=axes[n])
    for n in ("a_w2", "a_a2", "a_g2"):
        W[n] = W[n].astype(MMD)

    loss_blk, grad_x, G = local_step(x[0], mem[0], loss_target[0], W)
    loss = lax.psum(loss_blk[0, 0], ("x", "y", "c"))

    big_parts = [jnp.split(G[n], N_CHIPS, axis=big_axes[n]) for n in big_names]
    gbig = jnp.stack([_flat_cat([p[j] for p in big_parts], n_big, F32) for j in range(N_CHIPS)])
    gbig = gbig.reshape(N_CHIPS, 2, mh, BIG_LANES)
    sm_full_names = ss_names + list(SMALL_REPL)
    sm_full_shapes = [_full_shape(w_sh[n].shape, ss_axes[n]) for n in ss_names] + [w_sh[n].shape for n in SMALL_REPL]
    n_smf = _round_up(sum(math.prod(s) for s in sm_full_shapes), 2 * 8 * SMALL_LANES)
    msh = n_smf // (2 * SMALL_LANES)
    gsm = _flat_cat([G[n] for n in sm_full_names], n_smf, F32).reshape(2, msh, SMALL_LANES)
    rb, rs = comm_pair_exchange(gbig, gsm)
    mine_b = lax.dynamic_index_in_dim(gbig, ci, axis=1, keepdims=False)
    mine_s = lax.dynamic_index_in_dim(gsm, ci, axis=0, keepdims=False)
    hb = add_pairs(mine_b.reshape(-1, BIG_LANES), rb.reshape(-1, BIG_LANES), BF16, name="add_pairs_big", tb=128)
    hs = add_pairs(mine_s, rs, F32, name="add_pairs_small", tb=msh)
    qb, qs = comm_chip_exchange(hb.reshape(N_CHIPS, mh, BIG_LANES), hs)
    gh = add_chips([qb[j] for j in range(N_CHIPS)], name="add_chips_big", tb=32)
    gsh = add_chips([qs[j] for j in range(N_CHIPS)], name="add_chips_small", tb=msh)
    gfull, gsfull = comm_pair_share(gh, gsh)

    g_big = gfull.reshape(-1, BIG_LANES)
    pack_big = lambda d: _flat_cat([d[n] for n in big_names], n_big, F32).reshape(-1, BIG_LANES)
    d_big, m_big, v_big = adamw(g_big, pack_big(w_sh), pack_big(m_sh), pack_big(v_sh), name="adamw_big", tb=64)
    sm_full = dict(zip(sm_full_names, _split_flat(gsfull.reshape(-1), sm_full_shapes)))
    g_loc = {}
    for n in ss_names:
        size = w_sh[n].shape[ss_axes[n]]
        g_loc[n] = lax.dynamic_slice_in_dim(sm_full[n], chip * size, size, axis=ss_axes[n])
    for n in SMALL_REPL:
        g_loc[n] = sm_full[n]
    n_sml = _round_up(sum(math.prod(w_sh[n].shape) for n in sm_full_names), 8 * SMALL_LANES)
    pack_sm = lambda d: _flat_cat([d[n] for n in sm_full_names], n_sml, F32).reshape(-1, SMALL_LANES)
    d_sm, m_sm, v_sm = adamw(pack_sm(g_loc), pack_sm(w_sh), pack_sm(m_sh), pack_sm(v_sh), name="adamw_small",
                             tb=n_sml // SMALL_LANES)
    sm_loc_shapes = [w_sh[n].shape for n in sm_full_names]
    res = {}
    for tag, bigv, smv in (("grad", g_big, None), ("delta", d_big, d_sm), ("new_m", m_big, m_sm), ("new_v", v_big, v_sm)):
        res[tag] = dict(zip(big_names, _split_flat(bigv.reshape(-1), big_shapes)))
        res[tag].update(g_loc if smv is None else dict(zip(sm_full_names, _split_flat(smv.reshape(-1), sm_loc_shapes))))
    return (loss, grad_x[None], *[res[tag][n] for tag in ("grad", "delta", "new_m", "new_v") for n in WEIGHTS])
```

```python
import functools
import math

import numpy as np
import jax
import jax.numpy as jnp
from jax import lax
from jax.experimental import pallas as pl
from jax.experimental.pallas import tpu as pltpu

F32 = jnp.float32
BF16 = jnp.bfloat16
MMD = jnp.bfloat16

D_MODEL = 1024
HEAD_DIM = 64
N_MEM = 256
MEM_WIDTH = 256
RWKV_HEADS = 12
RWKV_WIDTH = 768
SHIFT_WIDTH = 2560
LORA_WIDTH = 256
DIL_WIDTH = 768
DIL_GROUPS = ((128, 1), (512, 4), (2048, 16))
DIL_BLOCK = 128
D_FF = 2816
ROPE_THETA = 10000.0
RMS_EPS = 1e-6
LNX_EPS = 64e-5
NEG_INF = -1e30
ADAM_LR = 0.001
ADAM_B1 = 0.9
ADAM_B2 = 0.999
ADAM_EPS = 1e-08
ADAM_WD = 0.01
ADAM_STEP = 10
N_CHIPS = 4
MESH = pl.DeviceIdType.MESH
VMEM_LIMIT_MB = 56
SCAN_CHUNK = 16

BIG = (("a_w_in", 2), ("a_w_out", 1), ("kv_w", 1), ("b_w_in", 1), ("b_w_out", 2), ("mem_w_kv", 1),
       ("ffn_w_up", 2), ("ffn_w_down", 1))
SMALL_SHARDED = (("a_mu", 1), ("a_w0", 1), ("a_w2", 2), ("a_a0", 1), ("a_a2", 2), ("a_g2", 2), ("a_k_k", 1),
                 ("a_k_a", 1), ("a_lnx_w", 1), ("a_lnx_b", 1), ("ffn_conv_w", 2))
SMALL_REPL = ("attn_norm", "a_r_k", "kv_norm", "kv_k_norm", "b_q_norm", "mem_norm", "mem_q_norm", "mem_k_norm",
              "ffn_norm", "ffn_conv_b")
WEIGHTS = ("attn_norm", "a_w_in", "a_mu", "a_w0", "a_w2", "a_a0", "a_a2", "a_g2", "a_k_k", "a_k_a", "a_r_k",
           "a_lnx_w", "a_lnx_b", "a_w_out", "kv_norm", "kv_w", "kv_k_norm", "b_w_in", "b_q_norm", "b_w_out",
           "mem_norm", "mem_w_kv", "mem_q_norm", "mem_k_norm", "ffn_norm", "ffn_w_up", "ffn_conv_w", "ffn_conv_b",
           "ffn_w_down")


def _cp(sem=None, **kw):
    return pltpu.CompilerParams(dimension_semantics=sem, vmem_limit_bytes=VMEM_LIMIT_MB << 20, **kw)


def _tile(n, cands=(512, 256, 128)):
    for c in cands:
        if n % c == 0:
            return c
    return n


def _mm(a, b, *, name, ta=False, tb=False, add=None, out_dtype=F32):
    K, M = a.shape if ta else a.shape[::-1]
    N = b.shape[0] if tb else b.shape[1]
    assert K == (b.shape[1] if tb else b.shape[0])
    tm, tn = _tile(M), _tile(N)
    a_spec = pl.BlockSpec((K, tm), lambda i, j: (0, i)) if ta else pl.BlockSpec((tm, K), lambda i, j: (i, 0))
    b_spec = pl.BlockSpec((tn, K), lambda i, j: (j, 0)) if tb else pl.BlockSpec((K, tn), lambda i, j: (0, j))
    o_spec = pl.BlockSpec((tm, tn), lambda i, j: (i, j))
    dn = (((0,) if ta else (1,), (1,) if tb else (0,)), ((), ()))
    has_add = add is not None

    def body(*refs):
        a_ref, b_ref = refs[0], refs[1]
        o_ref = refs[-1]
        acc = lax.dot_general(a_ref[...].astype(MMD), b_ref[...].astype(MMD), dn, preferred_element_type=F32)
        if has_add:
            acc = acc + refs[2][...]
        o_ref[...] = acc.astype(o_ref.dtype)

    ins = [a, b] + ([add] if has_add else [])
    specs = [a_spec, b_spec] + ([o_spec] if has_add else [])
    return pl.pallas_call(
        body, name=name, grid=(M // tm, N // tn), in_specs=specs, out_specs=o_spec,
        out_shape=jax.ShapeDtypeStruct((M, N), out_dtype), compiler_params=_cp(("parallel", "parallel")),
    )(*ins)


def _rowmap(fn, *, name, T, tb, ins, outs, accs=()):
    nblk = T // tb
    assert T % tb == 0 and tb % 8 == 0
    in_specs, args = [], []
    for spec in ins:
        kind, arr = spec[0], spec[1]
        w, cb = (spec[2], spec[3]) if len(spec) > 2 else (arr.shape[-1], 0)
        if kind == "row":
            in_specs.append(pl.BlockSpec((tb, w), lambda i, cb=cb: (i, cb)))
        elif kind == "prev":
            in_specs.append(pl.BlockSpec((8, w), lambda i, cb=cb: (jnp.maximum(i * (tb // 8) - 1, 0), cb)))
        elif kind == "next":
            in_specs.append(pl.BlockSpec((8, w), lambda i, cb=cb: (jnp.minimum((i + 1) * (tb // 8), T // 8 - 1), cb)))
        elif kind == "const":
            in_specs.append(pl.BlockSpec(arr.shape, lambda i, nd=arr.ndim: (0,) * nd))
        else:
            raise ValueError(kind)
        args.append(arr)
    out_shape, out_specs = [], []
    for kind, w, dt in outs:
        out_shape.append(jax.ShapeDtypeStruct((T, w), dt))
        out_specs.append(pl.BlockSpec((tb, w), lambda i: (i, 0)))
    for shp, dt in accs:
        out_shape.append(jax.ShapeDtypeStruct(shp, dt))
        out_specs.append(pl.BlockSpec(shp, lambda i, nd=len(shp): (0,) * nd))
    n_in, n_out = len(ins), len(outs)

    def body(*refs):
        i = pl.program_id(0)
        vals = [r[...] for r in refs[:n_in]]
        res = fn(i, nblk, *vals)
        if not isinstance(res, (tuple, list)):
            res = (res,)
        assert len(res) == n_out + len(accs), (name, len(res))
        for r, v in zip(refs[n_in:n_in + n_out], res[:n_out]):
            r[...] = v.astype(r.dtype)
        acc_refs = refs[n_in + n_out:]
        if acc_refs:
            @pl.when(i == 0)
            def _():
                for r in acc_refs:
                    r[...] = jnp.zeros(r.shape, r.dtype)

            for r, v in zip(acc_refs, res[n_out:]):
                r[...] += v

    res = pl.pallas_call(
        body, name=name, grid=(nblk,), in_specs=in_specs, out_specs=out_specs, out_shape=out_shape,
        compiler_params=_cp(("arbitrary",)),
    )(*args)
    return res


def _row_pick(halo, r):
    rid = lax.broadcasted_iota(jnp.int32, halo.shape, 0)
    return jnp.sum(jnp.where(rid == r, halo, 0.0), axis=0, keepdims=True)


def _shift_down(x, row_before, is_first):
    rid = lax.broadcasted_iota(jnp.int32, x.shape, 0)
    first = jnp.where(is_first, 0.0, 1.0) * row_before
    return jnp.where(rid == 0, first, pltpu.roll(x, 1, axis=0))


def _shift_up(x, row_after, is_last):
    n = x.shape[0]
    rid = lax.broadcasted_iota(jnp.int32, x.shape, 0)
    last = jnp.where(is_last, 0.0, 1.0) * row_after
    return jnp.where(rid == n - 1, last, pltpu.roll(x, n - 1, axis=0))


def _dot(a, b, dn=(((1,), (0,)), ((), ()))):
    return lax.dot_general(a.astype(MMD), b.astype(MMD), dn, preferred_element_type=F32)


def _dot_nt(a, b):
    return _dot(a, b, (((1,), (1,)), ((), ())))


def _dot_tn(a, b):
    return _dot(a, b, (((0,), (0,)), ((), ())))


def _dot_exact01(x, g01):
    hi = x.astype(BF16)
    lo = (x - hi.astype(F32)).astype(BF16)
    dn = (((1,), (0,)), ((), ()))
    return (lax.dot_general(hi, g01, dn, preferred_element_type=F32)
            + lax.dot_general(lo, g01, dn, preferred_element_type=F32))


def _fold_heads(v, fold):
    return _row_pick(_dot_exact01(jnp.broadcast_to(v, (8, v.shape[1])), fold), 0)


def _group_ones(width):
    idx = np.arange(width) // HEAD_DIM
    return jnp.asarray((idx[:, None] == idx[None, :]).astype(np.float32), BF16)


def _fold_ones(width):
    idx = np.arange(width) % HEAD_DIM
    return jnp.asarray((idx[:, None] == np.arange(HEAD_DIM)[None, :]).astype(np.float32), BF16)


def _head_masks(width):
    idx = np.arange(width) // HEAD_DIM
    return jnp.asarray((idx[None, :] == np.arange(width // HEAD_DIM)[:, None]).astype(np.float32)[:, None, :], F32)


def _rms_stats(x):
    r = lax.rsqrt(jnp.mean(x * x, axis=-1, keepdims=True) + RMS_EPS)
    return r, x * r


def rms_fwd(x, gains, *, name):
    T, D = x.shape

    def fn(i, nblk, xb, *gs):
        _, xh = _rms_stats(xb)
        return tuple(xh * g for g in gs)

    return _rowmap(fn, name=name, T=T, tb=512, ins=[("row", x)] + [("const", g) for g in gains],
                   outs=[("row", D, MMD)] * len(gains))


def rms_bwd(x, gains, dhs, dres, *, name):
    T, D = x.shape
    n = len(gains)

    def fn(i, nblk, xb, dr, *rest):
        gs, ds = rest[:n], rest[n:]
        r, xh = _rms_stats(xb)
        dx = dr
        dgs = []
        for g, dh in zip(gs, ds):
            dgs.append(jnp.sum(dh * xh, axis=0, keepdims=True))
            dxh = dh * g
            dx = dx + r * (dxh - xh * jnp.mean(dxh * xh, axis=-1, keepdims=True))
        return (dx, *dgs)

    return _rowmap(fn, name=name, T=T, tb=512,
                   ins=[("row", x), ("row", dres)] + [("const", g) for g in gains] + [("row", d) for d in dhs],
                   outs=[("row", D, F32)], accs=[((1, D), F32)] * n)


def _pre1_common(i, ps, halo, mu, w0, a0, w2p, a2p, g2p):
    prev = _shift_down(ps, _row_pick(halo, 7), i == 0)
    xs = ps + (prev - ps) * mu
    lo = xs[:, 3 * RWKV_WIDTH:]
    tl, sl = jnp.tanh(lo), jax.nn.sigmoid(lo)
    dec = w0 + _dot(tl, w2p)
    ain = a0 + _dot(lo, a2p)
    g = _dot(sl, g2p)
    wl = -jax.nn.softplus(-dec) - 0.5
    w = jnp.exp(-jnp.exp(wl))
    a = jax.nn.sigmoid(ain)
    return prev, xs, lo, tl, sl, dec, wl, w, a, g


def rwkv_pre_fwd(p, mu, w0, a0, w2p, a2p, g2p):
    T = p.shape[0]

    def fn(i, nblk, ps, halo, mu, w0, a0, w2p, a2p, g2p):
        _, xs, _, _, _, _, _, w, a, g = _pre1_common(i, ps, halo, mu, w0, a0, w2p, a2p, g2p)
        W = RWKV_WIDTH
        return xs[:, :W], xs[:, W:2 * W], xs[:, 2 * W:3 * W], w, a, g

    return _rowmap(fn, name="rwkv_pre_fwd", T=T, tb=256,
                   ins=[("row", p, SHIFT_WIDTH, 0), ("prev", p, SHIFT_WIDTH, 0)]
                   + [("const", c) for c in (mu, w0, a0, w2p, a2p, g2p)],
                   outs=[("row", RWKV_WIDTH, F32)] * 6)


def rwkv_pre_bwd(p, mu, w0, a0, w2p, a2p, g2p, dr, dk, dv, dw, da, dg):
    T = p.shape[0]

    def fn(i, nblk, ps, halo, mu, w0, a0, w2p, a2p, g2p, dr, dk, dv, dw, da, dg):
        prev, xs, lo, tl, sl, dec, wl, w, a, g = _pre1_common(i, ps, halo, mu, w0, a0, w2p, a2p, g2p)
        ddec = dw * (-w * jnp.exp(wl)) * jax.nn.sigmoid(-dec)
        dain = da * a * (1.0 - a)
        dlo = (_dot_nt(ddec, w2p) * (1.0 - tl * tl) + _dot_nt(dain, a2p) + _dot_nt(dg, g2p) * sl * (1.0 - sl))
        dxs = jnp.concatenate([dr, dk, dv, dlo], axis=1)
        dmu = jnp.sum(dxs * (prev - ps), axis=0, keepdims=True)
        return (dxs, dmu, jnp.sum(ddec, axis=0, keepdims=True), jnp.sum(dain, axis=0, keepdims=True),
                _dot_tn(tl, ddec), _dot_tn(lo, dain), _dot_tn(sl, dg))

    return _rowmap(fn, name="rwkv_pre_bwd", T=T, tb=256,
                   ins=[("row", p, SHIFT_WIDTH, 0), ("prev", p, SHIFT_WIDTH, 0)]
                   + [("const", c) for c in (mu, w0, a0, w2p, a2p, g2p)]
                   + [("row", c) for c in (dr, dk, dv, dw, da, dg)],
                   outs=[("row", SHIFT_WIDTH, F32)],
                   accs=[((1, SHIFT_WIDTH), F32), ((1, RWKV_WIDTH), F32), ((1, RWKV_WIDTH), F32)]
                   + [((LORA_WIDTH, RWKV_WIDTH), F32)] * 3)


def shift_bwd(dxs, mu, dq_mem):
    T = dxs.shape[0]

    def fn(i, nblk, d, halo, mu, dq):
        nxt = _shift_up(d, _row_pick(halo, 0), i == nblk - 1)
        return jnp.concatenate([d * (1.0 - mu) + nxt * mu, dq], axis=1)

    return _rowmap(fn, name="shift_bwd", T=T, tb=256,
                   ins=[("row", dxs), ("next", dxs), ("const", mu), ("row", dq_mem)],
                   outs=[("row", SHIFT_WIDTH + MEM_WIDTH, MMD)])[0]


def _scan_prep(k, a, k_k, k_a):
    kk = k * k_k
    kk = kk / jnp.maximum(jnp.sqrt(jnp.sum(kk * kk, axis=-1, keepdims=True)), 1e-12)
    k2 = k * (1.0 + (a - 1.0) * k_a)
    return kk, k2, kk * a


def _scan_post(y, r, k2, v, lnx_w, lnx_b, r_k):
    m = jnp.mean(y, axis=-1, keepdims=True)
    yc = y - m
    var = jnp.mean(yc * yc, axis=-1, keepdims=True)
    yn = yc * lax.rsqrt(var + LNX_EPS) * lnx_w + lnx_b
    return yn + jnp.sum(r * k2 * r_k, axis=-1, keepdims=True) * v


def _eye():
    n = HEAD_DIM
    return (lax.broadcasted_iota(jnp.int32, (n, n), 0) == lax.broadcasted_iota(jnp.int32, (n, n), 1)).astype(F32)


def rwkv_scan_fwd(r, w, k, v, a, k_k, k_a, r_k, lnx_w, lnx_b):
    H, T, N = r.shape
    tc = SCAN_CHUNK
    seq = pl.BlockSpec((H, tc, N), lambda i: (0, i, 0))
    par = pl.BlockSpec((H, 1, N), lambda i: (0, 0, 0))

    def body(r_ref, w_ref, k_ref, v_ref, a_ref, kk_p, ka_p, rk_p, lw_p, lb_p, o_ref, y_ref, st_ref,
             s_scr, kk_scr, k2_scr, kka_scr):
        @pl.when(pl.program_id(0) == 0)
        def _():
            s_scr[...] = jnp.zeros(s_scr.shape, F32)

        kk, k2, kka = _scan_prep(k_ref[...], a_ref[...], kk_p[...], ka_p[...])
        kk_scr[...] = kk
        k2_scr[...] = k2
        kka_scr[...] = kka
        eye = _eye()

        def step(t, carry):
            row = lambda ref: ref[:, pl.ds(t, 1), :]
            S = s_scr[...]
            sa = -jnp.sum(S * row(kk_scr), axis=-1, keepdims=True)
            v_col = jnp.sum(eye * row(v_ref), axis=-1, keepdims=True)
            S2 = S * row(w_ref) + sa * row(kka_scr) + v_col * row(k2_scr)
            y_col = jnp.sum(S2 * row(r_ref), axis=-1, keepdims=True)
            y_ref[:, pl.ds(t, 1), :] = jnp.sum(eye * y_col, axis=-2, keepdims=True)
            s_scr[...] = S2
            st_ref[t] = S2
            return carry

        lax.fori_loop(0, tc, step, 0)
        o_ref[...] = _scan_post(y_ref[...], r_ref[...], k2, v_ref[...], lw_p[...], lb_p[...], rk_p[...])

    return pl.pallas_call(
        body, name="rwkv_scan_fwd", grid=(T // tc,), in_specs=[seq] * 5 + [par] * 5,
        out_specs=[seq, seq, pl.BlockSpec((tc, H, N, N), lambda i: (i, 0, 0, 0))],
        out_shape=[jax.ShapeDtypeStruct((H, T, N), F32), jax.ShapeDtypeStruct((H, T, N), F32),
                   jax.ShapeDtypeStruct((T, H, N, N), F32)],
        scratch_shapes=[pltpu.VMEM((H, N, N), F32)] + [pltpu.VMEM((H, tc, N), F32)] * 3,
        compiler_params=_cp(("arbitrary",)),
    )(r, w, k, v, a, k_k, k_a, r_k, lnx_w, lnx_b)


def rwkv_scan_bwd(r, w, k, v, a, k_k, k_a, r_k, lnx_w, lnx_b, y, states, dmix):
    H, T, N = r.shape
    tc = SCAN_CHUNK
    nchunk = T // tc
    seq = pl.BlockSpec((H, tc, N), lambda i: (0, nchunk - 1 - i, 0))
    par = pl.BlockSpec((H, 1, N), lambda i: (0, 0, 0))
    st_spec = pl.BlockSpec((tc, H, N, N), lambda i: (nchunk - 1 - i, 0, 0, 0))
    st_prev = pl.BlockSpec((1, H, N, N), lambda i: (jnp.maximum((nchunk - 1 - i) * tc - 1, 0), 0, 0, 0))

    def body(r_ref, w_ref, k_ref, v_ref, a_ref, kk_p, ka_p, rk_p, lw_p, lb_p, y_ref, st_ref, sp_ref, dm_ref,
             dr_ref, dw_ref, dk_ref, dv_ref, da_ref, dkkp_ref, dkap_ref, drkp_ref, dlw_ref, dlb_ref,
             ds_scr, sc_scr, kk_scr, k2_scr, kka_scr, dy_scr, dkk_scr, dk2_scr, dkka_scr):
        i = pl.program_id(0)

        @pl.when(i == 0)
        def _():
            ds_scr[...] = jnp.zeros(ds_scr.shape, F32)
            for ref in (dkkp_ref, dkap_ref, drkp_ref, dlw_ref, dlb_ref):
                ref[...] = jnp.zeros(ref.shape, F32)

        rr, kr, vr, ar = r_ref[...], k_ref[...], v_ref[...], a_ref[...]
        (kk, k2, kka), prep_vjp = jax.vjp(_scan_prep, kr, ar, kk_p[...], ka_p[...])
        kk_scr[...] = kk
        k2_scr[...] = k2
        kka_scr[...] = kka
        _, post_vjp = jax.vjp(_scan_post, y_ref[...], rr, k2, vr, lw_p[...], lb_p[...], rk_p[...])
        dy, dr_b, dk2_b, dv_b, dlw, dlb, drk = post_vjp(dm_ref[...])
        dy_scr[...] = dy
        dlw_ref[...] += dlw
        dlb_ref[...] += dlb
        drkp_ref[...] += drk
        sc_scr[...] = st_ref[tc - 1]
        eye = _eye()

        def step(t, s_prev):
            row = lambda ref: ref[:, pl.ds(t, 1), :]
            s_cur = sc_scr[...]
            kk_t, kka_t, k2_t = row(kk_scr), row(kka_scr), row(k2_scr)
            dy_col = jnp.sum(eye * row(dy_scr), axis=-1, keepdims=True)
            v_col = jnp.sum(eye * row(v_ref), axis=-1, keepdims=True)
            dS = ds_scr[...] + dy_col * row(r_ref)
            dr_ref[:, pl.ds(t, 1), :] = jnp.sum(s_cur * dy_col, axis=-2, keepdims=True)
            dw_ref[:, pl.ds(t, 1), :] = jnp.sum(dS * s_prev, axis=-2, keepdims=True)
            sa = -jnp.sum(s_prev * kk_t, axis=-1, keepdims=True)
            dkka_scr[:, pl.ds(t, 1), :] = jnp.sum(dS * sa, axis=-2, keepdims=True)
            dk2_scr[:, pl.ds(t, 1), :] = jnp.sum(dS * v_col, axis=-2, keepdims=True)
            dsa = jnp.sum(dS * kka_t, axis=-1, keepdims=True)
            dv_col = jnp.sum(dS * k2_t, axis=-1, keepdims=True)
            dv_ref[:, pl.ds(t, 1), :] = jnp.sum(eye * dv_col, axis=-2, keepdims=True)
            dkk_scr[:, pl.ds(t, 1), :] = -jnp.sum(s_prev * dsa, axis=-2, keepdims=True)
            ds_scr[...] = dS * row(w_ref) - dsa * kk_t
            sc_scr[...] = s_prev

        def loop_body(j, carry):
            t = tc - 1 - j
            step(t, st_ref[t - 1])
            return carry

        lax.fori_loop(0, tc - 1, loop_body, 0)
        step(0, jnp.where(i == nchunk - 1, 0.0, 1.0) * sp_ref[0])

        dk2 = dk2_scr[...] + dk2_b
        dk, da, dkkp, dkap = prep_vjp((dkk_scr[...], dk2, dkka_scr[...]))
        dr_ref[...] += dr_b
        dv_ref[...] += dv_b
        dk_ref[...] = dk
        da_ref[...] = da
        dkkp_ref[...] += dkkp
        dkap_ref[...] += dkap

    hs = jax.ShapeDtypeStruct((H, T, N), F32)
    ps = jax.ShapeDtypeStruct((H, 1, N), F32)
    return pl.pallas_call(
        body, name="rwkv_scan_bwd", grid=(nchunk,),
        in_specs=[seq] * 5 + [par] * 5 + [seq, st_spec, st_prev, seq],
        out_specs=[seq] * 5 + [par] * 5, out_shape=[hs] * 5 + [ps] * 5,
        scratch_shapes=[pltpu.VMEM((H, N, N), F32)] * 2 + [pltpu.VMEM((H, tc, N), F32)] * 7,
        compiler_params=_cp(("arbitrary",)),
    )(r, w, k, v, a, k_k, k_a, r_k, lnx_w, lnx_b, y, states, states, dmix)


def gate_fwd(mix, g, y_mem):
    T = mix.shape[0]

    def fn(i, nblk, m, g, ym):
        return jnp.concatenate([m * g, ym], axis=1)

    return _rowmap(fn, name="gate_fwd", T=T, tb=512, ins=[("row", mix), ("row", g), ("row", y_mem)],
                   outs=[("row", RWKV_WIDTH + MEM_WIDTH, MMD)])[0]


def gate_bwd(mix, g, dycat):
    T = mix.shape[0]

    def fn(i, nblk, m, g, dy):
        return dy * g, dy * m

    return _rowmap(fn, name="gate_bwd", T=T, tb=512, ins=[("row", mix), ("row", g), ("row", dycat, RWKV_WIDTH, 0)],
                   outs=[("row", RWKV_WIDTH, F32)] * 2)


def _head_rms(x, gones):
    ms = _dot_exact01(x * x, gones) * (1.0 / HEAD_DIM)
    r = lax.rsqrt(ms + RMS_EPS)
    return r, x * r


def _head_rms_bwd(dxn_g, r, xh, gones):
    return r * (dxn_g - xh * (_dot_exact01(dxn_g * xh, gones) * (1.0 / HEAD_DIM)))


def mem_kv_fwd(mem, norm_g, w_kv, k_norm_t, *, name):
    gones = _group_ones(MEM_WIDTH)

    def body(mem_ref, g_ref, w_ref, kn_ref, go_ref, k_out, v_out):
        _, xh = _rms_stats(mem_ref[...])
        kv = _dot(xh * g_ref[...], w_ref[...])
        _, kh = _head_rms(kv[:, :MEM_WIDTH], go_ref[...])
        k_out[...] = kh * kn_ref[...]
        v_out[...] = kv[:, MEM_WIDTH:]

    return pl.pallas_call(
        body, name=name, out_shape=[jax.ShapeDtypeStruct((N_MEM, MEM_WIDTH), F32)] * 2, compiler_params=_cp(),
    )(mem, norm_g, w_kv, k_norm_t, gones)


def mem_kv_bwd(mem, norm_g, w_kv, k_norm_t, dkn, dv, *, name):
    gones, fold = _group_ones(MEM_WIDTH), _fold_ones(MEM_WIDTH)

    def body(mem_ref, g_ref, w_ref, kn_ref, go_ref, fo_ref, dkn_ref, dv_ref, dw_out, dg_out, dkg_out):
        _, xh = _rms_stats(mem_ref[...])
        hm = xh * g_ref[...]
        kv = _dot(hm, w_ref[...])
        r, kh = _head_rms(kv[:, :MEM_WIDTH], go_ref[...])
        dkn = dkn_ref[...]
        dkg_out[...] = _fold_heads(jnp.sum(dkn * kh, axis=0, keepdims=True), fo_ref[...])
        dkraw = _head_rms_bwd(dkn * kn_ref[...], r, kh, go_ref[...])
        dkv = jnp.concatenate([dkraw, dv_ref[...]], axis=1)
        dw_out[...] = _dot_tn(hm, dkv)
        dg_out[...] = jnp.sum(_dot_nt(dkv, w_ref[...]) * xh, axis=0, keepdims=True)

    return pl.pallas_call(
        body, name=name,
        out_shape=[jax.ShapeDtypeStruct((D_MODEL, 2 * MEM_WIDTH), F32), jax.ShapeDtypeStruct((1, D_MODEL), F32),
                   jax.ShapeDtypeStruct((1, HEAD_DIM), F32)],
        compiler_params=_cp(),
    )(mem, norm_g, w_kv, k_norm_t, gones, fold, dkn, dv)


def _mem_scores(qn, kn, masks, h):
    s = _dot_nt(qn * masks[h], kn) * (1.0 / math.sqrt(HEAD_DIM))
    s = s - jnp.max(s, axis=-1, keepdims=True)
    e = jnp.exp(s)
    return e / jnp.sum(e, axis=-1, keepdims=True)


def mem_attn_fwd(p, colblock, kn, v, q_norm_t, *, name):
    T = p.shape[0]
    gones, masks = _group_ones(MEM_WIDTH), _head_masks(MEM_WIDTH)

    def fn(i, nblk, q, kn, v, qg, go, masks):
        _, qh = _head_rms(q, go)
        qn = qh * qg
        out = jnp.zeros(q.shape, F32)
        for h in range(MEM_WIDTH // HEAD_DIM):
            out = out + _dot(_mem_scores(qn, kn, masks, h), v * masks[h])
        return out

    return _rowmap(fn, name=name, T=T, tb=512,
                   ins=[("row", p, MEM_WIDTH, colblock)] + [("const", c) for c in (kn, v, q_norm_t, gones, masks)],
                   outs=[("row", MEM_WIDTH, F32)])[0]


def mem_attn_bwd(p, colblock, kn, v, q_norm_t, dycat, dcolblock, *, name):
    T = p.shape[0]
    gones, masks, fold = _group_ones(MEM_WIDTH), _head_masks(MEM_WIDTH), _fold_ones(MEM_WIDTH)
    scale = 1.0 / math.sqrt(HEAD_DIM)

    def fn(i, nblk, q, dy, kn, v, qg, go, masks, fo):
        r, qh = _head_rms(q, go)
        qn = qh * qg
        dqn = jnp.zeros(q.shape, F32)
        dkn = jnp.zeros(kn.shape, F32)
        dv = jnp.zeros(v.shape, F32)
        for h in range(MEM_WIDTH // HEAD_DIM):
            pr = _mem_scores(qn, kn, masks, h)
            dyh = dy * masks[h]
            dpr = _dot_nt(dyh, v)
            ds = pr * (dpr - jnp.sum(dpr * pr, axis=-1, keepdims=True)) * scale
            dqn = dqn + _dot(ds, kn * masks[h])
            dkn = dkn + _dot_tn(ds, qn * masks[h])
            dv = dv + _dot_tn(pr, dyh)
        dqg = _fold_heads(jnp.sum(dqn * qh, axis=0, keepdims=True), fo)
        return _head_rms_bwd(dqn * qg, r, qh, go), dkn, dv, dqg

    return _rowmap(fn, name=name, T=T, tb=512,
                   ins=[("row", p, MEM_WIDTH, colblock), ("row", dycat, MEM_WIDTH, dcolblock)]
                   + [("const", c) for c in (kn, v, q_norm_t, gones, masks, fold)],
                   outs=[("row", MEM_WIDTH, F32)],
                   accs=[((N_MEM, MEM_WIDTH), F32), ((N_MEM, MEM_WIDTH), F32), ((1, HEAD_DIM), F32)])


def _ffn_conv(i, u, halo, cw, cb):
    up1 = _shift_down(u, _row_pick(halo, 7), i == 0)
    up2 = _shift_down(up1, _row_pick(halo, 6), i == 0)
    c = cb + cw[0] * up2 + cw[1] * up1 + cw[2] * u
    return up1, up2, c[:, :D_FF], c[:, D_FF:]


def ffn_act_fwd(u, cw, cb, *, name):
    T = u.shape[0]

    def fn(i, nblk, u, halo, c0, c1, c2, cb):
        _, _, gate, val = _ffn_conv(i, u, halo, (c0, c1, c2), cb)
        return jax.nn.silu(gate) * val

    return _rowmap(fn, name=name, T=T, tb=128, ins=[("row", u), ("prev", u)] + [("const", c) for c in (*cw, cb)],
                   outs=[("row", D_FF, MMD)])[0]


def ffn_act_bwd(u, cw, cb, dz, *, name):
    T = u.shape[0]

    def fn(i, nblk, u, halo, c0, c1, c2, cb, dz):
        up1, up2, gate, val = _ffn_conv(i, u, halo, (c0, c1, c2), cb)
        sg = jax.nn.sigmoid(gate)
        dgate = dz * val * sg * (1.0 + gate * (1.0 - sg))
        dval = dz * gate * sg
        dc = jnp.concatenate([dgate, dval], axis=1)
        s = lambda z: jnp.sum(z, axis=0, keepdims=True)
        return dc, s(dc * up2), s(dc * up1), s(dc * u), s(dc)

    return _rowmap(fn, name=name, T=T, tb=128,
                   ins=[("row", u), ("prev", u)] + [("const", c) for c in (*cw, cb)] + [("row", dz)],
                   outs=[("row", 2 * D_FF, F32)], accs=[((1, 2 * D_FF), F32)] * 4)


def conv_bwd(dc, cw, *, name):
    T = dc.shape[0]

    def fn(i, nblk, d, halo, c0, c1, c2):
        last = i == nblk - 1
        n1 = _shift_up(d, _row_pick(halo, 0), last)
        n2 = _shift_up(n1, _row_pick(halo, 1), last)
        return c2 * d + c1 * n1 + c0 * n2

    return _rowmap(fn, name=name, T=T, tb=128, ins=[("row", dc), ("next", dc)] + [("const", c) for c in cw],
                   outs=[("row", 2 * D_FF, MMD)])[0]


def _rope_swap(z):
    lane = lax.broadcasted_iota(jnp.int32, z.shape, 1) % HEAD_DIM
    w = z.shape[1]
    return jnp.where(lane < HEAD_DIM // 2, pltpu.roll(z, w - HEAD_DIM // 2, axis=1), pltpu.roll(z, HEAD_DIM // 2, axis=1))


def rope_tables(T):
    inv = ROPE_THETA ** (-jnp.arange(0, HEAD_DIM, 2, dtype=F32) / HEAD_DIM)
    ang = jnp.arange(T, dtype=F32)[:, None] * inv[None, :]
    cos, sin = jnp.cos(ang), jnp.sin(ang)
    return jnp.concatenate([cos, cos, cos, cos], axis=1), jnp.concatenate([-sin, sin, -sin, sin], axis=1)


def _rope_wide(t):
    return jnp.tile(t, (1, DIL_WIDTH // t.shape[1]))


def qk_fwd(kvp, pb, kg_t, qg_t, cos, sin):
    T = kvp.shape[0]
    gones = _group_ones(DIL_WIDTH)

    def fn(i, nblk, kraw, vraw, qraw, kg, qg, c, s, go):
        c, s = _rope_wide(c), _rope_wide(s)
        outs = []
        for raw, g in ((qraw, qg), (kraw, kg)):
            _, xh = _head_rms(raw, go)
            z = xh * g
            outs.append(z * c + _rope_swap(z) * s)
        return outs[0], outs[1], vraw

    return _rowmap(fn, name="qk_fwd", T=T, tb=256,
                   ins=[("row", kvp, DIL_WIDTH, 0), ("row", kvp, DIL_WIDTH, 1), ("row", pb, DIL_WIDTH, 0)]
                   + [("const", kg_t), ("const", qg_t), ("row", cos), ("row", sin), ("const", gones)],
                   outs=[("row", DIL_WIDTH, MMD)] * 3)


def qk_bwd(kvp, pb, kg_t, qg_t, cos, sin, dq, dk, dv, dq_mem):
    T = kvp.shape[0]
    gones, fold = _group_ones(DIL_WIDTH), _fold_ones(DIL_WIDTH)

    def fn(i, nblk, kraw, qraw, kg, qg, c, s, go, fo, dq, dk, dv, dqm):
        c, s = _rope_wide(c), _rope_wide(s)
        res, dgs = [], []
        for raw, g, d in ((qraw, qg, dq), (kraw, kg, dk)):
            r, xh = _head_rms(raw, go)
            dz = d * c + _rope_swap(d * s)
            dgs.append(_fold_heads(jnp.sum(dz * xh, axis=0, keepdims=True), fo))
            res.append(_head_rms_bwd(dz * g, r, xh, go))
        return (jnp.concatenate([res[0], dqm], axis=1), jnp.concatenate([res[1], dv], axis=1), dgs[0], dgs[1])

    return _rowmap(fn, name="qk_bwd", T=T, tb=256,
                   ins=[("row", kvp, DIL_WIDTH, 0), ("row", pb, DIL_WIDTH, 0), ("const", kg_t), ("const", qg_t),
                        ("row", cos), ("row", sin), ("const", gones), ("const", fold),
                        ("row", dq), ("row", dk), ("row", dv), ("row", dq_mem)],
                   outs=[("row", DIL_WIDTH + MEM_WIDTH, MMD), ("row", 2 * DIL_WIDTH, MMD)],
                   accs=[((1, HEAD_DIM), F32)] * 2)


def _band(kind):
    i = lax.broadcasted_iota(jnp.int32, (DIL_BLOCK, DIL_BLOCK), 0)
    j = lax.broadcasted_iota(jnp.int32, (DIL_BLOCK, DIL_BLOCK), 1)
    return (j <= i) if kind == "cur" else (j >= i)


def dil_attn_fwd(q, k, v, seq_blocks, *, name):
    T, W = q.shape
    nb = T // DIL_BLOCK
    masks = _head_masks(W)
    cur = pl.BlockSpec((DIL_BLOCK, W), lambda n: (n, 0))
    prv = pl.BlockSpec((DIL_BLOCK, W), lambda n: (jnp.maximum(n - 1, 0), 0))
    scale = 1.0 / math.sqrt(HEAD_DIM)

    def body(q_ref, kc_ref, kp_ref, vc_ref, vp_ref, m_ref, o_ref, l_ref):
        n = pl.program_id(0)
        has_prev = (n % seq_blocks) != 0
        q = q_ref[...].astype(F32)
        kc, kp = kc_ref[...].astype(F32), kp_ref[...].astype(F32)
        vc, vp = vc_ref[...].astype(F32), vp_ref[...].astype(F32)
        ok_c = _band("cur")
        ok_p = jnp.logical_and(_band("prev"), has_prev)
        o = jnp.zeros((DIL_BLOCK, W), F32)
        lse = jnp.zeros((DIL_BLOCK, W), F32)
        for h in range(W // HEAD_DIM):
            mh = m_ref[h]
            qh = q * mh
            sc = jnp.where(ok_c, _dot_nt(qh, kc) * scale, NEG_INF)
            sp = jnp.where(ok_p, _dot_nt(qh, kp) * scale, NEG_INF)
            mx = jnp.maximum(jnp.max(sc, axis=-1, keepdims=True), jnp.max(sp, axis=-1, keepdims=True))
            ec, ep = jnp.exp(sc - mx), jnp.exp(sp - mx)
            den = jnp.sum(ec, axis=-1, keepdims=True) + jnp.sum(ep, axis=-1, keepdims=True)
            o = o + (_dot(ec, vc * mh) + _dot(ep, vp * mh)) / den
            lse = lse + (mx + jnp.log(den)) * mh
        o_ref[...] = o
        l_ref[...] = lse

    return pl.pallas_call(
        body, name=name, grid=(nb,), in_specs=[cur, cur, prv, cur, prv, pl.BlockSpec(masks.shape, lambda n: (0, 0, 0))],
        out_specs=[cur, cur], out_shape=[jax.ShapeDtypeStruct((T, W), F32)] * 2,
        compiler_params=_cp(("parallel",)),
    )(q, k, k, v, v, masks)


def dil_attn_bwd(q, k, v, o, lse, do, dlse, seq_blocks, *, name):
    T, W = q.shape
    nb = T // DIL_BLOCK
    masks = _head_masks(W)
    cur = pl.BlockSpec((DIL_BLOCK, W), lambda n: (n, 0))
    prv = pl.BlockSpec((DIL_BLOCK, W), lambda n: (jnp.maximum(n - 1, 0), 0))
    nxt = pl.BlockSpec((DIL_BLOCK, W), lambda n: (jnp.minimum(n + 1, nb - 1), 0))
    scale = 1.0 / math.sqrt(HEAD_DIM)

    def body(qc_ref, qn_ref, kc_ref, kp_ref, vc_ref, vp_ref, oc_ref, on_ref, lc_ref, ln_ref, doc_ref, don_ref,
             dlc_ref, dln_ref, m_ref, dq_ref, dk_ref, dv_ref):
        n = pl.program_id(0)
        has_prev = (n % seq_blocks) != 0
        has_next = jnp.logical_and(((n + 1) % seq_blocks) != 0, n + 1 < nb)
        f = lambda ref: ref[...].astype(F32)
        qc, qn, kc, kp, vc, vp = f(qc_ref), f(qn_ref), f(kc_ref), f(kp_ref), f(vc_ref), f(vp_ref)
        doc, don = doc_ref[...], don_ref[...]
        ok_c = _band("cur")
        ok_p = jnp.logical_and(_band("prev"), has_prev)
        ok_n = jnp.logical_and(_band("prev"), has_next)
        dq = jnp.zeros((DIL_BLOCK, W), F32)
        dk = jnp.zeros((DIL_BLOCK, W), F32)
        dv = jnp.zeros((DIL_BLOCK, W), F32)

        def side(qh, kk, vv, doh, lse_h, corr, ok):
            s = _dot_nt(qh, kk) * scale
            pr = jnp.where(ok, jnp.exp(jnp.where(ok, s, NEG_INF) - lse_h), 0.0)
            ds = pr * (_dot_nt(doh, vv) + corr) * scale
            return pr, ds

        for h in range(W // HEAD_DIM):
            mh = m_ref[h]
            red = lambda z: jnp.sum(z * mh, axis=-1, keepdims=True)
            qh, doh = qc * mh, doc * mh
            lse_h = red(lc_ref[...]) * (1.0 / HEAD_DIM)
            corr = red(dlc_ref[...]) - red(doc * oc_ref[...])
            pr_c, ds_c = side(qh, kc, vc * mh, doh, lse_h, corr, ok_c)
            _, ds_p = side(qh, kp, vp * mh, doh, lse_h, corr, ok_p)
            dq = dq + _dot(ds_c, kc * mh) + _dot(ds_p, kp * mh)
            dk = dk + _dot_tn(ds_c, qh)
            dv = dv + _dot_tn(pr_c, doh)
            qh2, doh2 = qn * mh, don * mh
            lse_2 = red(ln_ref[...]) * (1.0 / HEAD_DIM)
            corr2 = red(dln_ref[...]) - red(don * on_ref[...])
            pr_n, ds_n = side(qh2, kc, vc * mh, doh2, lse_2, corr2, ok_n)
            dk = dk + _dot_tn(ds_n, qh2)
            dv = dv + _dot_tn(pr_n, doh2)
        dq_ref[...] = dq
        dk_ref[...] = dk
        dv_ref[...] = dv

    return pl.pallas_call(
        body, name=name, grid=(nb,),
        in_specs=[cur, nxt, cur, prv, cur, prv, cur, nxt, cur, nxt, cur, nxt, cur, nxt,
                  pl.BlockSpec(masks.shape, lambda n: (0, 0, 0))],
        out_specs=[cur] * 3, out_shape=[jax.ShapeDtypeStruct((T, W), F32)] * 3,
        compiler_params=_cp(("parallel",)),
    )(q, q, k, k, v, v, o, o, lse, lse, do, do, dlse, dlse, masks)


def _mix_weights(ls):
    m = jnp.maximum(jnp.maximum(ls[0], ls[1]), ls[2])
    es = [jnp.exp(l - m) for l in ls]
    den = es[0] + es[1] + es[2]
    return [e / den for e in es]


def mix_fwd(os_, ls, y_mem):
    T = y_mem.shape[0]

    def fn(i, nblk, o0, o1, o2, l0, l1, l2, ym):
        w = _mix_weights((l0, l1, l2))
        return jnp.concatenate([w[0] * o0 + w[1] * o1 + w[2] * o2, ym], axis=1)

    return _rowmap(fn, name="mix_fwd", T=T, tb=512, ins=[("row", z) for z in (*os_, *ls, y_mem)],
                   outs=[("row", 2 * MEM_WIDTH, MMD)])[0]


def mix_bwd(os_, ls, dycat):
    T = dycat.shape[0]

    def fn(i, nblk, o0, o1, o2, l0, l1, l2, dy):
        w = _mix_weights((l0, l1, l2))
        os3 = (o0, o1, o2)
        dws = [dy * o for o in os3]
        tot = w[0] * dws[0] + w[1] * dws[1] + w[2] * dws[2]
        return tuple(wg * dy for wg in w) + tuple(wg * (dw - tot) for wg, dw in zip(w, dws))

    return _rowmap(fn, name="mix_bwd", T=T, tb=512,
                   ins=[("row", z) for z in (*os_, *ls)] + [("row", dycat, MEM_WIDTH, 0)],
                   outs=[("row", MEM_WIDTH, F32)] * 6)


def loss_fwd_bwd(y, target):
    T, D = y.shape

    def fn(i, nblk, y, t):
        e = y - t
        return e * (1.0 / D), jnp.zeros((8, 128), F32) + jnp.sum(e * e) * (0.5 / D)

    return _rowmap(fn, name="loss", T=T, tb=512, ins=[("row", y), ("row", target)], outs=[("row", D, F32)],
                   accs=[((8, 128), F32)])


def _to_heads(z):
    T = z.shape[0]
    return z.reshape(T, -1, HEAD_DIM).transpose(1, 0, 2)


def _from_heads(z):
    return z.transpose(1, 0, 2).reshape(z.shape[1], -1)


def _to_residues(z, dil):
    T, W = z.shape
    return z.reshape(T // dil, dil, W).transpose(1, 0, 2).reshape(T, W)


def _from_residues(z, dil):
    T, W = z.shape
    return z.reshape(dil, T // dil, W).transpose(1, 0, 2).reshape(T, W)


def _pad_rows(w, rows):
    return jnp.concatenate([w, jnp.zeros((rows - w.shape[0], w.shape[1]), w.dtype)], axis=0)


def _tile_heads(g, width):
    return jnp.tile(g.reshape(1, HEAD_DIM), (1, width // HEAD_DIM))


def _conv_rows(W, i):
    return [W["ffn_conv_w"][i][j:j + 1] for j in range(3)]


def _ffn_fwd(x, i, W):
    hn = rms_fwd(x, [W["ffn_norm"][i:i + 1]], name=f"ffn_rms{i}")[0]
    u = _mm(hn, W["ffn_w_up"][i], name=f"ffn_up{i}")
    z = ffn_act_fwd(u, _conv_rows(W, i), W["ffn_conv_b"][i:i + 1], name=f"ffn_act{i}")
    out = _mm(z, W["ffn_w_down"][i], add=x, name=f"ffn_down{i}")
    return out, (x, hn, u, z)


def _ffn_bwd(dout, i, W, saved, G):
    x, hn, u, z = saved
    dz = _mm(dout, W["ffn_w_down"][i], tb=True, name=f"ffn_down_dx{i}")
    G["ffn_w_down"][i] = _mm(z, dout, ta=True, name=f"ffn_down_dw{i}")
    dc, dw0, dw1, dw2, db = ffn_act_bwd(u, _conv_rows(W, i), W["ffn_conv_b"][i:i + 1], dz, name=f"ffn_act_bwd{i}")
    G["ffn_conv_w"][i] = jnp.concatenate([dw0, dw1, dw2], axis=0)
    G["ffn_conv_b"][i] = db[0]
    du = conv_bwd(dc, _conv_rows(W, i), name=f"ffn_conv_bwd{i}")
    dhn = _mm(du, W["ffn_w_up"][i], tb=True, name=f"ffn_up_dx{i}")
    G["ffn_w_up"][i] = _mm(hn, du, ta=True, name=f"ffn_up_dw{i}")
    dx, dg = rms_bwd(x, [W["ffn_norm"][i:i + 1]], [dhn], dout, name=f"ffn_rms_bwd{i}")
    G["ffn_norm"][i] = dg[0]
    return dx


def local_step(x, mem, target, W):
    T = x.shape[0]
    G = {"ffn_w_down": [None, None], "ffn_w_up": [None, None], "ffn_conv_w": [None, None],
         "ffn_conv_b": [None, None], "ffn_norm": [None, None], "attn_norm": [None, None], "mem_norm": [None, None],
         "mem_w_kv": [None, None], "mem_q_norm": [None, None], "mem_k_norm": [None, None]}
    hp = lambda n: W[n][0].reshape(RWKV_HEADS, 1, HEAD_DIM)
    mu, w0, a0 = W["a_mu"], W["a_w0"], W["a_a0"]
    w2p, a2p, g2p = (_pad_rows(W["a_w2"][0], LORA_WIDTH),
                     jnp.concatenate([jnp.zeros((64, RWKV_WIDTH), MMD), W["a_a2"][0],
                                      jnp.zeros((128, RWKV_WIDTH), MMD)], axis=0),
                     jnp.concatenate([jnp.zeros((128, RWKV_WIDTH), MMD), W["a_g2"][0]], axis=0))
    k_k, k_a, lnx_w, lnx_b = hp("a_k_k"), hp("a_k_a"), hp("a_lnx_w"), hp("a_lnx_b")
    r_k = W["a_r_k"][0].reshape(RWKV_HEADS, 1, HEAD_DIM)
    memkv = []
    for i in range(2):
        memkv.append(mem_kv_fwd(mem, W["mem_norm"][i:i + 1], W["mem_w_kv"][i], _tile_heads(W["mem_k_norm"][i], MEM_WIDTH),
                                name=f"mem_kv{i}"))

    h0 = rms_fwd(x, [W["attn_norm"][0:1]], name="attn_rms0")[0]
    p = _mm(h0, W["a_w_in"][0], name="a_in")
    r, k, v, w, a, g = rwkv_pre_fwd(p, mu, w0, a0, w2p, a2p, g2p)
    rh, kh, vh, wh, ah = (_to_heads(z) for z in (r, k, v, w, a))
    mixh, yh, states = rwkv_scan_fwd(rh, wh, kh, vh, ah, k_k, k_a, r_k, lnx_w, lnx_b)
    mix = _from_heads(mixh)
    qg0 = _tile_heads(W["mem_q_norm"][0], MEM_WIDTH)
    y_mem0 = mem_attn_fwd(p, SHIFT_WIDTH // MEM_WIDTH, memkv[0][0], memkv[0][1], qg0, name="mem_attn0")
    ycat0 = gate_fwd(mix, g, y_mem0)
    x1 = _mm(ycat0, W["a_w_out"][0], add=x, name="a_out")
    x2, ffn0 = _ffn_fwd(x1, 0, W)

    h1, hkv = rms_fwd(x2, [W["attn_norm"][1:2], W["kv_norm"].reshape(1, -1)], name="attn_rms1")
    kvp = _mm(hkv, W["kv_w"], name="kv_in")
    pb = _mm(h1, W["b_w_in"][0], name="b_in")
    cos, sin = rope_tables(T)
    kg_t, qg_t = _tile_heads(W["kv_k_norm"], DIL_WIDTH), _tile_heads(W["b_q_norm"][0], DIL_WIDTH)
    q, ksh, vsh = qk_fwd(kvp, pb, kg_t, qg_t, cos, sin)
    os_, ls, grp = [], [], []
    for gi, (win, dil) in enumerate(DIL_GROUPS):
        sl = slice(gi * MEM_WIDTH, (gi + 1) * MEM_WIDTH)
        qg_, kg_, vg_ = (_to_residues(z[:, sl], dil) for z in (q, ksh, vsh))
        o_r, l_r = dil_attn_fwd(qg_, kg_, vg_, T // dil // DIL_BLOCK, name=f"dil_fwd{gi}")
        grp.append((qg_, kg_, vg_, o_r, l_r))
        os_.append(_from_residues(o_r, dil))
        ls.append(_from_residues(l_r, dil))
    qg1 = _tile_heads(W["mem_q_norm"][1], MEM_WIDTH)
    y_mem1 = mem_attn_fwd(pb, DIL_WIDTH // MEM_WIDTH, memkv[1][0], memkv[1][1], qg1, name="mem_attn1")
    ycat1 = mix_fwd(os_, ls, y_mem1)
    x3 = _mm(ycat1, W["b_w_out"][0], add=x2, name="b_out")
    x4, ffn1 = _ffn_fwd(x3, 1, W)

    dx4, loss = loss_fwd_bwd(x4, target)

    dx3 = _ffn_bwd(dx4, 1, W, ffn1, G)
    dycat1 = _mm(dx3, W["b_w_out"][0], tb=True, name="b_out_dx")
    G["b_w_out"] = _mm(ycat1, dx3, ta=True, name="b_out_dw")[None]
    dq_mem1, dkn1, dvm1, dqg1 = mem_attn_bwd(pb, DIL_WIDTH // MEM_WIDTH, memkv[1][0], memkv[1][1], qg1, dycat1, 1,
                                             name="mem_attn_bwd1")
    G["mem_q_norm"][1] = dqg1[0]
    d_os_ls = mix_bwd(os_, ls, dycat1)
    dqs, dks, dvs = [], [], []
    for gi, (win, dil) in enumerate(DIL_GROUPS):
        qg_, kg_, vg_, o_r, l_r = grp[gi]
        do_r, dl_r = _to_residues(d_os_ls[gi], dil), _to_residues(d_os_ls[3 + gi], dil)
        dq_r, dk_r, dv_r = dil_attn_bwd(qg_, kg_, vg_, o_r, l_r, do_r, dl_r, T // dil // DIL_BLOCK, name=f"dil_bwd{gi}")
        dqs.append(_from_residues(dq_r, dil))
        dks.append(_from_residues(dk_r, dil))
        dvs.append(_from_residues(dv_r, dil))
    dq, dk, dv = (jnp.concatenate(z, axis=1) for z in (dqs, dks, dvs))
    dpb, dkvp, dqn_g, dkn_g = qk_bwd(kvp, pb, kg_t, qg_t, cos, sin, dq, dk, dv, dq_mem1)
    G["b_q_norm"] = dqn_g
    G["kv_k_norm"] = dkn_g[0]
    dh1 = _mm(dpb, W["b_w_in"][0], tb=True, name="b_in_dx")
    G["b_w_in"] = _mm(h1, dpb, ta=True, name="b_in_dw")[None]
    dhkv = _mm(dkvp, W["kv_w"], tb=True, name="kv_in_dx")
    G["kv_w"] = _mm(hkv, dkvp, ta=True, name="kv_in_dw")
    dx2, dg1, dgkv = rms_bwd(x2, [W["attn_norm"][1:2], W["kv_norm"].reshape(1, -1)], [dh1, dhkv], dx3,
                             name="attn_rms_bwd1")
    G["attn_norm"][1] = dg1[0]
    G["kv_norm"] = dgkv[0]

    dx1 = _ffn_bwd(dx2, 0, W, ffn0, G)
    dycat0 = _mm(dx1, W["a_w_out"][0], tb=True, name="a_out_dx")
    G["a_w_out"] = _mm(ycat0, dx1, ta=True, name="a_out_dw")[None]
    dq_mem0, dkn0, dvm0, dqg0 = mem_attn_bwd(p, SHIFT_WIDTH // MEM_WIDTH, memkv[0][0], memkv[0][1], qg0, dycat0,
                                             RWKV_WIDTH // MEM_WIDTH, name="mem_attn_bwd0")
    G["mem_q_norm"][0] = dqg0[0]
    dmix, dg = gate_bwd(mix, g, dycat0)
    drh, dwh, dkh, dvh, dah, dkk_p, dka_p, drk_p, dlw_p, dlb_p = rwkv_scan_bwd(
        rh, wh, kh, vh, ah, k_k, k_a, r_k, lnx_w, lnx_b, yh, states, _to_heads(dmix))
    dr, dw, dk_, dv_, da = (_from_heads(z) for z in (drh, dwh, dkh, dvh, dah))
    dxs, dmu, dw0, da0, dw2p, da2p, dg2p = rwkv_pre_bwd(p, mu, w0, a0, w2p, a2p, g2p, dr, dk_, dv_, dw, da, dg)
    dp = shift_bwd(dxs, mu, dq_mem0)
    flat = lambda z: z.reshape(1, RWKV_WIDTH)
    G.update(a_mu=dmu, a_w0=dw0, a_a0=da0, a_w2=dw2p[None, :64], a_a2=da2p[None, 64:128], a_g2=dg2p[None, 128:],
             a_k_k=flat(dkk_p), a_k_a=flat(dka_p), a_r_k=drk_p.reshape(1, RWKV_HEADS, HEAD_DIM),
             a_lnx_w=flat(dlw_p), a_lnx_b=flat(dlb_p))
    dh0 = _mm(dp, W["a_w_in"][0], tb=True, name="a_in_dx")
    G["a_w_in"] = _mm(h0, dp, ta=True, name="a_in_dw")[None]
    grad_x, dg0 = rms_bwd(x, [W["attn_norm"][0:1]], [dh0], dx1, name="attn_rms_bwd0")
    G["attn_norm"][0] = dg0[0]

    for i, (dkn, dvm) in enumerate(((dkn0, dvm0), (dkn1, dvm1))):
        dwkv, dgm, dkg = mem_kv_bwd(mem, W["mem_norm"][i:i + 1], W["mem_w_kv"][i],
                                    _tile_heads(W["mem_k_norm"][i], MEM_WIDTH), dkn, dvm, name=f"mem_kv_bwd{i}")
        G["mem_w_kv"][i], G["mem_norm"][i], G["mem_k_norm"][i] = dwkv, dgm[0], dkg[0]
    for n in list(G):
        if isinstance(G[n], list):
            G[n] = jnp.stack(G[n], axis=0)
    return loss, grad_x, G


HBM_SPEC = pl.BlockSpec(memory_space=pltpu.HBM)


def _mesh_pos():
    return lax.axis_index("x"), lax.axis_index("y"), lax.axis_index("c")


def _other_chips(x, y):
    return [(1 - x, y), (x, 1 - y), (1 - x, 1 - y)]


def _remote(send_sems, recv_sems, k, src, dst, to):
    return pltpu.make_async_remote_copy(src_ref=src, dst_ref=dst, send_sem=send_sems.at[k], recv_sem=recv_sems.at[k],
                                        device_id=to, device_id_type=MESH)


def _comm_call(body, name, ins, out_shape, n_remote):
    scratch = [pltpu.SemaphoreType.DMA((n_remote,)), pltpu.SemaphoreType.DMA((n_remote,))]
    return pl.pallas_call(body, name=name, in_specs=[HBM_SPEC] * len(ins), out_specs=[HBM_SPEC] * len(out_shape),
                          out_shape=out_shape, scratch_shapes=scratch)(*ins)


def comm_gather(wbig, wsm):
    def body(wb, ws, ob, os_, send_sems, recv_sems):
        x, y, c = _mesh_pos()
        s = 2 * x + y
        me, sibling = (x, y, c), (x, y, 1 - c)
        chips = _other_chips(x, y)
        rc = functools.partial(_remote, send_sems, recv_sems)
        first = []
        for j, (cx, cy) in enumerate(chips):
            first.append(rc(j, wb.at[c], ob.at[s, c], (cx, cy, c)))
            first.append(rc(6 + j, ws, os_.at[s], (cx, cy, c)))
        for cp in first:
            cp.start()
        passed = []
        for j, (cx, cy) in enumerate(chips):
            blk = ob.at[2 * cx + cy, c]
            rc(j, blk, blk, me).wait_recv()
            passed.append(rc(3 + j, blk, blk, sibling))
            passed[-1].start()
        for j, (cx, cy) in enumerate(chips):
            blk = ob.at[2 * cx + cy, 1 - c]
            rc(3 + j, blk, blk, me).wait_recv()
            sb = os_.at[2 * cx + cy]
            rc(6 + j, sb, sb, me).wait_recv()
        for cp in first + passed:
            cp.wait_send()

    out_shape = [jax.ShapeDtypeStruct((N_CHIPS, *wbig.shape), wbig.dtype),
                 jax.ShapeDtypeStruct((N_CHIPS, *wsm.shape), wsm.dtype)]
    return _comm_call(body, "comm_gather", [wbig, wsm], out_shape, 9)


def comm_pair_exchange(gb, gs):
    def body(gb_ref, gs_ref, rb_ref, rs_ref, send_sems, recv_sems):
        x, y, c = _mesh_pos()
        sibling = (x, y, 1 - c)
        rc = functools.partial(_remote, send_sems, recv_sems)
        cps = [rc(r, gb_ref.at[r, 1 - c], rb_ref.at[r], sibling) for r in range(N_CHIPS)]
        cps.append(rc(N_CHIPS, gs_ref.at[1 - c], rs_ref, sibling))
        for cp in cps:
            cp.start()
        for cp in cps:
            cp.wait()

    out_shape = [jax.ShapeDtypeStruct((N_CHIPS, *gb.shape[2:]), gb.dtype), jax.ShapeDtypeStruct(gs.shape[1:], gs.dtype)]
    return _comm_call(body, "comm_pair_exchange", [gb, gs], out_shape, N_CHIPS + 1)


def comm_chip_exchange(hb, hs):
    def body(hb_ref, hs_ref, qb_ref, qs_ref, send_sems, recv_sems):
        x, y, c = _mesh_pos()
        s = 2 * x + y
        me = (x, y, c)
        chips = _other_chips(x, y)
        rc = functools.partial(_remote, send_sems, recv_sems)
        cps = []
        for j, (cx, cy) in enumerate(chips):
            cps.append(rc(j, hb_ref.at[2 * cx + cy], qb_ref.at[s], (cx, cy, c)))
            cps.append(rc(3 + j, hs_ref, qs_ref.at[s], (cx, cy, c)))
        for cp in cps:
            cp.start()
        for j, (cx, cy) in enumerate(chips):
            blk = qb_ref.at[2 * cx + cy]
            rc(j, blk, blk, me).wait_recv()
            sb = qs_ref.at[2 * cx + cy]
            rc(3 + j, sb, sb, me).wait_recv()
        for cp in cps:
            cp.wait_send()

    out_shape = [jax.ShapeDtypeStruct(hb.shape, hb.dtype), jax.ShapeDtypeStruct((N_CHIPS, *hs.shape), hs.dtype)]
    return _comm_call(body, "comm_chip_exchange", [hb, hs], out_shape, 6)


def comm_pair_share(gh, gsh):
    def body(gh_ref, gs_ref, ob, os_, send_sems, recv_sems):
        x, y, c = _mesh_pos()
        rc = functools.partial(_remote, send_sems, recv_sems)
        cps = [rc(0, gh_ref, ob, (x, y, 1 - c)), rc(1, gs_ref, os_, (x, y, 1 - c))]
        for cp in cps:
            cp.start()
        for cp in cps:
            cp.wait()

    out_shape = [jax.ShapeDtypeStruct(gh.shape, gh.dtype), jax.ShapeDtypeStruct(gsh.shape, gsh.dtype)]
    return _comm_call(body, "comm_pair_share", [gh, gsh], out_shape, 2)


def add_pairs(a, b, out_dtype, *, name, tb):
    T, L = a.shape
    return _rowmap(lambda i, n, p, q: p + q, name=name, T=T, tb=tb, ins=[("row", a), ("row", b)],
                   outs=[("row", L, out_dtype)])[0]


def add_chips(parts, *, name, tb):
    T, L = parts[0].shape

    def fn(i, n, p0, p1, p2, p3):
        f = lambda z: z.astype(F32)
        return ((f(p0) + f(p1)) + f(p2)) + f(p3)

    return _rowmap(fn, name=name, T=T, tb=tb, ins=[("row", p) for p in parts], outs=[("row", L, F32)])[0]


def adamw(g, w, m, v, *, name, tb):
    T, L = g.shape

    def fn(i, n, g, w, m, v):
        m2 = ADAM_B1 * m + (1.0 - ADAM_B1) * g
        v2 = ADAM_B2 * v + (1.0 - ADAM_B2) * (g * g)
        m_hat = m2 / (1.0 - ADAM_B1 ** ADAM_STEP)
        v_hat = v2 / (1.0 - ADAM_B2 ** ADAM_STEP)
        return -ADAM_LR * (m_hat / (jnp.sqrt(v_hat) + ADAM_EPS) + ADAM_WD * w), m2, v2

    return _rowmap(fn, name=name, T=T, tb=tb, ins=[("row", z) for z in (g, w, m, v)], outs=[("row", L, F32)] * 3)


BIG_LANES = 1024
SMALL_LANES = 128


def _flat_cat(arrs, total, dtype):
    parts = [a.reshape(-1).astype(dtype) for a in arrs]
    n = sum(p.shape[0] for p in parts)
    assert n <= total, (n, total)
    if n < total:
        parts.append(jnp.zeros((total - n,), dtype))
    return jnp.concatenate(parts)


def _split_flat(flat, shapes):
    out, off = [], 0
    for shp in shapes:
        n = math.prod(shp)
        out.append(flat[off:off + n].reshape(shp))
        off += n
    return out


def _round_up(n, m):
    return -(-n // m) * m


def _full_shape(shard_shape, axis):
    return tuple(d * N_CHIPS if i == axis else d for i, d in enumerate(shard_shape))


def kernel(x, mem, attn_norm, a_w_in, a_mu, a_w0, a_w2, a_a0, a_a2, a_g2, a_k_k, a_k_a, a_r_k, a_lnx_w, a_lnx_b, a_w_out, kv_norm, kv_w, kv_k_norm, b_w_in, b_q_norm, b_w_out, mem_norm, mem_w_kv, mem_q_norm, mem_k_norm, ffn_norm, ffn_w_up, ffn_conv_w, ffn_conv_b, ffn_w_down, loss_target, m_attn_norm, m_a_w_in, m_a_mu, m_a_w0, m_a_w2, m_a_a0, m_a_a2, m_a_g2, m_a_k_k, m_a_k_a, m_a_r_k, m_a_lnx_w, m_a_lnx_b, m_a_w_out, m_kv_norm, m_kv_w, m_kv_k_norm, m_b_w_in, m_b_q_norm, m_b_w_out, m_mem_norm, m_mem_w_kv, m_mem_q_norm, m_mem_k_norm, m_ffn_norm, m_ffn_w_up, m_ffn_conv_w, m_ffn_conv_b, m_ffn_w_down, v_attn_norm, v_a_w_in, v_a_mu, v_a_w0, v_a_w2, v_a_a0, v_a_a2, v_a_g2, v_a_k_k, v_a_k_a, v_a_r_k, v_a_lnx_w, v_a_lnx_b, v_a_w_out, v_kv_norm, v_kv_w, v_kv_k_norm, v_b_w_in, v_b_q_norm, v_b_w_out, v_mem_norm, v_mem_w_kv, v_mem_q_norm, v_mem_k_norm, v_ffn_norm, v_ffn_w_up, v_ffn_conv_w, v_ffn_conv_b, v_ffn_w_down):
    args = (attn_norm, a_w_in, a_mu, a_w0, a_w2, a_a0, a_a2, a_g2, a_k_k, a_k_a, a_r_k, a_lnx_w, a_lnx_b, a_w_out, kv_norm, kv_w, kv_k_norm, b_w_in, b_q_norm, b_w_out, mem_norm, mem_w_kv, mem_q_norm, mem_k_norm, ffn_norm, ffn_w_up, ffn_conv_w, ffn_conv_b, ffn_w_down)
    ms = (m_attn_norm, m_a_w_in, m_a_mu, m_a_w0, m_a_w2, m_a_a0, m_a_a2, m_a_g2, m_a_k_k, m_a_k_a, m_a_r_k, m_a_lnx_w, m_a_lnx_b, m_a_w_out, m_kv_norm, m_kv_w, m_kv_k_norm, m_b_w_in, m_b_q_norm, m_b_w_out, m_mem_norm, m_mem_w_kv, m_mem_q_norm, m_mem_k_norm, m_ffn_norm, m_ffn_w_up, m_ffn_conv_w, m_ffn_conv_b, m_ffn_w_down)
    vs = (v_attn_norm, v_a_w_in, v_a_mu, v_a_w0, v_a_w2, v_a_a0, v_a_a2, v_a_g2, v_a_k_k, v_a_k_a, v_a_r_k, v_a_lnx_w, v_a_lnx_b, v_a_w_out, v_kv_norm, v_kv_w, v_kv_k_norm, v_b_w_in, v_b_q_norm, v_b_w_out, v_mem_norm, v_mem_w_kv, v_mem_q_norm, v_mem_k_norm, v_ffn_norm, v_ffn_w_up, v_ffn_conv_w, v_ffn_conv_b, v_ffn_w_down)
    w_sh, m_sh, v_sh = (dict(zip(WEIGHTS, z)) for z in (args, ms, vs))
    xi, yi, ci = _mesh_pos()
    chip = 2 * xi + yi
    big_names, big_axes = [n for n, _ in BIG], dict(BIG)
    ss_names, ss_axes = [n for n, _ in SMALL_SHARDED], dict(SMALL_SHARDED)
    big_shapes = [w_sh[n].shape for n in big_names]
    ss_shapes = [w_sh[n].shape for n in ss_names]
    n_big = sum(math.prod(s) for s in big_shapes)
    assert n_big % (2 * 8 * BIG_LANES) == 0
    mh = n_big // (2 * BIG_LANES)
    n_ss = _round_up(sum(math.prod(s) for s in ss_shapes), 8 * SMALL_LANES)

    wbig = _flat_cat([w_sh[n] for n in big_names], n_big, MMD).reshape(2, mh, BIG_LANES)
    wsm = _flat_cat([w_sh[n] for n in ss_names], n_ss, F32).reshape(-1, SMALL_LANES)
    wbig_all, wsm_all = comm_gather(wbig, wsm)
    wbig_all = lax.dynamic_update_index_in_dim(wbig_all, wbig, chip, 0).reshape(N_CHIPS, -1)
    wsm_all = lax.dynamic_update_index_in_dim(wsm_all, wsm, chip, 0).reshape(N_CHIPS, -1)
    W = {n: w_sh[n] for n in SMALL_REPL}
    for names, axes, shapes, allv in ((big_names, big_axes, big_shapes, wbig_all), (ss_names, ss_axes, ss_shapes, wsm_all)):
        per_chip = [_split_flat(allv[j], shapes) for j in range(N_CHIPS)]
        for k, n in enumerate(names):
            W[n] = jnp.concatenate([per_chip[j][k] for j in range(N_CHIPS)], axis=axes[n])
    for n in ("a_w2", "a_a2", "a_g2"):
        W[n] = W[n].astype(MMD)

    loss_blk, grad_x, G = local_step(x[0], mem[0], loss_target[0], W)
    loss = lax.psum(loss_blk[0, 0], ("x", "y", "c"))

    big_parts = [jnp.split(G[n], N_CHIPS, axis=big_axes[n]) for n in big_names]
    gbig = jnp.stack([_flat_cat([p[j] for p in big_parts], n_big, F32) for j in range(N_CHIPS)])
    gbig = gbig.reshape(N_CHIPS, 2, mh, BIG_LANES)
    sm_full_names = ss_names + list(SMALL_REPL)
    sm_full_shapes = [_full_shape(w_sh[n].shape, ss_axes[n]) for n in ss_names] + [w_sh[n].shape for n in SMALL_REPL]
    n_smf = _round_up(sum(math.prod(s) for s in sm_full_shapes), 2 * 8 * SMALL_LANES)
    msh = n_smf // (2 * SMALL_LANES)
    gsm = _flat_cat([G[n] for n in sm_full_names], n_smf, F32).reshape(2, msh, SMALL_LANES)
    rb, rs = comm_pair_exchange(gbig, gsm)
    mine_b = lax.dynamic_index_in_dim(gbig, ci, axis=1, keepdims=False)
    mine_s = lax.dynamic_index_in_dim(gsm, ci, axis=0, keepdims=False)
    hb = add_pairs(mine_b.reshape(-1, BIG_LANES), rb.reshape(-1, BIG_LANES), BF16, name="add_pairs_big", tb=128)
    hs = add_pairs(mine_s, rs, F32, name="add_pairs_small", tb=msh)
    hb = hb.reshape(N_CHIPS, mh, BIG_LANES)
    qb, qs = comm_chip_exchange(hb, hs)
    qb = lax.dynamic_update_index_in_dim(qb, lax.dynamic_index_in_dim(hb, chip, 0, keepdims=False), chip, 0)
    qs = lax.dynamic_update_index_in_dim(qs, hs, chip, 0)
    gh = add_chips([qb[j] for j in range(N_CHIPS)], name="add_chips_big", tb=32)
    gsh = add_chips([qs[j] for j in range(N_CHIPS)], name="add_chips_small", tb=msh)
    rh, rsh = comm_pair_share(gh, gsh)
    gfull = jnp.where(ci == 0, jnp.stack([gh, rh]), jnp.stack([rh, gh]))
    gsfull = jnp.where(ci == 0, jnp.stack([gsh, rsh]), jnp.stack([rsh, gsh]))

    g_big = gfull.reshape(-1, BIG_LANES)
    pack_big = lambda d: _flat_cat([d[n] for n in big_names], n_big, F32).reshape(-1, BIG_LANES)
    d_big, m_big, v_big = adamw(g_big, pack_big(w_sh), pack_big(m_sh), pack_big(v_sh), name="adamw_big", tb=64)
    sm_full = dict(zip(sm_full_names, _split_flat(gsfull.reshape(-1), sm_full_shapes)))
    g_loc = {}
    for n in ss_names:
        size = w_sh[n].shape[ss_axes[n]]
        g_loc[n] = lax.dynamic_slice_in_dim(sm_full[n], chip * size, size, axis=ss_axes[n])
    for n in SMALL_REPL:
        g_loc[n] = sm_full[n]
    n_sml = _round_up(sum(math.prod(w_sh[n].shape) for n in sm_full_names), 8 * SMALL_LANES)
    pack_sm = lambda d: _flat_cat([d[n] for n in sm_full_names], n_sml, F32).reshape(-1, SMALL_LANES)
    d_sm, m_sm, v_sm = adamw(pack_sm(g_loc), pack_sm(w_sh), pack_sm(m_sh), pack_sm(v_sh), name="adamw_small",
                             tb=n_sml // SMALL_LANES)
    sm_loc_shapes = [w_sh[n].shape for n in sm_full_names]
    res = {}
    for tag, bigv, smv in (("grad", g_big, None), ("delta", d_big, d_sm), ("new_m", m_big, m_sm), ("new_v", v_big, v_sm)):
        res[tag] = dict(zip(big_names, _split_flat(bigv.reshape(-1), big_shapes)))
        res[tag].update(g_loc if smv is None else dict(zip(sm_full_names, _split_flat(smv.reshape(-1), sm_loc_shapes))))
    return (loss, grad_x[None], *[res[tag][n] for tag in ("grad", "delta", "new_m", "new_v") for n in WEIGHTS])
```

```python
import functools
import math

import numpy as np
import jax
import jax.numpy as jnp
from jax import lax
from jax.experimental import pallas as pl
from jax.experimental.pallas import tpu as pltpu

F32 = jnp.float32
BF16 = jnp.bfloat16
MMD = jnp.bfloat16

D_MODEL = 1024
HEAD_DIM = 64
N_MEM = 256
MEM_WIDTH = 256
RWKV_HEADS = 12
RWKV_WIDTH = 768
SHIFT_WIDTH = 2560
LORA_WIDTH = 256
DIL_WIDTH = 768
DIL_GROUPS = ((128, 1), (512, 4), (2048, 16))
DIL_BLOCK = 128
D_FF = 2816
ROPE_THETA = 10000.0
RMS_EPS = 1e-6
LNX_EPS = 64e-5
NEG_INF = -1e30
ADAM_LR = 0.001
ADAM_B1 = 0.9
ADAM_B2 = 0.999
ADAM_EPS = 1e-08
ADAM_WD = 0.01
ADAM_STEP = 10
N_CHIPS = 4
MESH = pl.DeviceIdType.MESH
VMEM_LIMIT_MB = 56
SCAN_CHUNK = 16
SCAN_UNROLL = 4
SCAN_UNROLL_BWD = 2

BIG = (("a_w_in", 2), ("a_w_out", 1), ("kv_w", 1), ("b_w_in", 1), ("b_w_out", 2), ("mem_w_kv", 1),
       ("ffn_w_up", 2), ("ffn_w_down", 1))
SMALL_SHARDED = (("a_mu", 1), ("a_w0", 1), ("a_w2", 2), ("a_a0", 1), ("a_a2", 2), ("a_g2", 2), ("a_k_k", 1),
                 ("a_k_a", 1), ("a_lnx_w", 1), ("a_lnx_b", 1), ("ffn_conv_w", 2))
SMALL_REPL = ("attn_norm", "a_r_k", "kv_norm", "kv_k_norm", "b_q_norm", "mem_norm", "mem_q_norm", "mem_k_norm",
              "ffn_norm", "ffn_conv_b")
WEIGHTS = ("attn_norm", "a_w_in", "a_mu", "a_w0", "a_w2", "a_a0", "a_a2", "a_g2", "a_k_k", "a_k_a", "a_r_k",
           "a_lnx_w", "a_lnx_b", "a_w_out", "kv_norm", "kv_w", "kv_k_norm", "b_w_in", "b_q_norm", "b_w_out",
           "mem_norm", "mem_w_kv", "mem_q_norm", "mem_k_norm", "ffn_norm", "ffn_w_up", "ffn_conv_w", "ffn_conv_b",
           "ffn_w_down")


def _cp(sem=None, **kw):
    return pltpu.CompilerParams(dimension_semantics=sem, vmem_limit_bytes=VMEM_LIMIT_MB << 20, **kw)


def _tile(n, cands=(512, 256, 128)):
    for c in cands:
        if n % c == 0:
            return c
    return n


def _mm(a, b, *, name, ta=False, tb=False, add=None, out_dtype=F32):
    K, M = a.shape if ta else a.shape[::-1]
    N = b.shape[0] if tb else b.shape[1]
    assert K == (b.shape[1] if tb else b.shape[0])
    tm, tn = _tile(M), _tile(N)
    a_spec = pl.BlockSpec((K, tm), lambda i, j: (0, i)) if ta else pl.BlockSpec((tm, K), lambda i, j: (i, 0))
    b_spec = pl.BlockSpec((tn, K), lambda i, j: (j, 0)) if tb else pl.BlockSpec((K, tn), lambda i, j: (0, j))
    o_spec = pl.BlockSpec((tm, tn), lambda i, j: (i, j))
    dn = (((0,) if ta else (1,), (1,) if tb else (0,)), ((), ()))
    has_add = add is not None

    def body(*refs):
        a_ref, b_ref = refs[0], refs[1]
        o_ref = refs[-1]
        acc = lax.dot_general(a_ref[...].astype(MMD), b_ref[...].astype(MMD), dn, preferred_element_type=F32)
        if has_add:
            acc = acc + refs[2][...]
        o_ref[...] = acc.astype(o_ref.dtype)

    ins = [a, b] + ([add] if has_add else [])
    specs = [a_spec, b_spec] + ([o_spec] if has_add else [])
    return pl.pallas_call(
        body, name=name, grid=(M // tm, N // tn), in_specs=specs, out_specs=o_spec,
        out_shape=jax.ShapeDtypeStruct((M, N), out_dtype), compiler_params=_cp(("parallel", "parallel")),
    )(*ins)


def _rowmap(fn, *, name, T, tb, ins, outs, accs=()):
    nblk = T // tb
    assert T % tb == 0 and tb % 8 == 0
    in_specs, args = [], []
    for spec in ins:
        kind, arr = spec[0], spec[1]
        w, cb = (spec[2], spec[3]) if len(spec) > 2 else (arr.shape[-1], 0)
        if kind == "row":
            in_specs.append(pl.BlockSpec((tb, w), lambda i, cb=cb: (i, cb)))
        elif kind == "prev":
            in_specs.append(pl.BlockSpec((8, w), lambda i, cb=cb: (jnp.maximum(i * (tb // 8) - 1, 0), cb)))
        elif kind == "next":
            in_specs.append(pl.BlockSpec((8, w), lambda i, cb=cb: (jnp.minimum((i + 1) * (tb // 8), T // 8 - 1), cb)))
        elif kind == "const":
            in_specs.append(pl.BlockSpec(arr.shape, lambda i, nd=arr.ndim: (0,) * nd))
        else:
            raise ValueError(kind)
        args.append(arr)
    out_shape, out_specs = [], []
    for kind, w, dt in outs:
        out_shape.append(jax.ShapeDtypeStruct((T, w), dt))
        out_specs.append(pl.BlockSpec((tb, w), lambda i: (i, 0)))
    for shp, dt in accs:
        out_shape.append(jax.ShapeDtypeStruct(shp, dt))
        out_specs.append(pl.BlockSpec(shp, lambda i, nd=len(shp): (0,) * nd))
    n_in, n_out = len(ins), len(outs)

    def body(*refs):
        i = pl.program_id(0)
        vals = [r[...] for r in refs[:n_in]]
        res = fn(i, nblk, *vals)
        if not isinstance(res, (tuple, list)):
            res = (res,)
        assert len(res) == n_out + len(accs), (name, len(res))
        for r, v in zip(refs[n_in:n_in + n_out], res[:n_out]):
            r[...] = v.astype(r.dtype)
        acc_refs = refs[n_in + n_out:]
        if acc_refs:
            @pl.when(i == 0)
            def _():
                for r in acc_refs:
                    r[...] = jnp.zeros(r.shape, r.dtype)

            for r, v in zip(acc_refs, res[n_out:]):
                r[...] += v

    res = pl.pallas_call(
        body, name=name, grid=(nblk,), in_specs=in_specs, out_specs=out_specs, out_shape=out_shape,
        compiler_params=_cp(("arbitrary",)),
    )(*args)
    return res


def _row_pick(halo, r):
    rid = lax.broadcasted_iota(jnp.int32, halo.shape, 0)
    return jnp.sum(jnp.where(rid == r, halo, 0.0), axis=0, keepdims=True)


def _shift_down(x, row_before, is_first):
    rid = lax.broadcasted_iota(jnp.int32, x.shape, 0)
    first = jnp.where(is_first, 0.0, 1.0) * row_before
    return jnp.where(rid == 0, first, pltpu.roll(x, 1, axis=0))


def _shift_up(x, row_after, is_last):
    n = x.shape[0]
    rid = lax.broadcasted_iota(jnp.int32, x.shape, 0)
    last = jnp.where(is_last, 0.0, 1.0) * row_after
    return jnp.where(rid == n - 1, last, pltpu.roll(x, n - 1, axis=0))


def _dot(a, b, dn=(((1,), (0,)), ((), ()))):
    return lax.dot_general(a.astype(MMD), b.astype(MMD), dn, preferred_element_type=F32)


def _dot_nt(a, b):
    return _dot(a, b, (((1,), (1,)), ((), ())))


def _dot_tn(a, b):
    return _dot(a, b, (((0,), (0,)), ((), ())))


def _dot_exact01(x, g01):
    hi = x.astype(BF16)
    lo = (x - hi.astype(F32)).astype(BF16)
    dn = (((1,), (0,)), ((), ()))
    return (lax.dot_general(hi, g01, dn, preferred_element_type=F32)
            + lax.dot_general(lo, g01, dn, preferred_element_type=F32))


def _fold_heads(v, fold):
    return _row_pick(_dot_exact01(jnp.broadcast_to(v, (8, v.shape[1])), fold), 0)


def _group_ones(width):
    idx = np.arange(width) // HEAD_DIM
    return jnp.asarray((idx[:, None] == idx[None, :]).astype(np.float32), BF16)


def _fold_ones(width):
    idx = np.arange(width) % HEAD_DIM
    return jnp.asarray((idx[:, None] == np.arange(HEAD_DIM)[None, :]).astype(np.float32), BF16)


def _head_masks(width):
    idx = np.arange(width) // HEAD_DIM
    return jnp.asarray((idx[None, :] == np.arange(width // HEAD_DIM)[:, None]).astype(np.float32)[:, None, :], F32)


def _rms_stats(x):
    r = lax.rsqrt(jnp.mean(x * x, axis=-1, keepdims=True) + RMS_EPS)
    return r, x * r


def rms_fwd(x, gains, *, name):
    T, D = x.shape

    def fn(i, nblk, xb, *gs):
        _, xh = _rms_stats(xb)
        return tuple(xh * g for g in gs)

    return _rowmap(fn, name=name, T=T, tb=512, ins=[("row", x)] + [("const", g) for g in gains],
                   outs=[("row", D, MMD)] * len(gains))


def rms_bwd(x, gains, dhs, dres, *, name):
    T, D = x.shape
    n = len(gains)

    def fn(i, nblk, xb, dr, *rest):
        gs, ds = rest[:n], rest[n:]
        r, xh = _rms_stats(xb)
        dx = dr
        dgs = []
        for g, dh in zip(gs, ds):
            dgs.append(jnp.sum(dh * xh, axis=0, keepdims=True))
            dxh = dh * g
            dx = dx + r * (dxh - xh * jnp.mean(dxh * xh, axis=-1, keepdims=True))
        return (dx, *dgs)

    return _rowmap(fn, name=name, T=T, tb=512,
                   ins=[("row", x), ("row", dres)] + [("const", g) for g in gains] + [("row", d) for d in dhs],
                   outs=[("row", D, F32)], accs=[((1, D), F32)] * n)


def _segsum(x):
    first = lax.broadcasted_iota(jnp.int32, (x.shape[0], 128), 1) < HEAD_DIM
    outs = []
    for p in range(x.shape[1] // 128):
        xs = x[:, p * 128:(p + 1) * 128]
        lo = jnp.sum(jnp.where(first, xs, 0.0), axis=-1, keepdims=True)
        hi = jnp.sum(jnp.where(first, 0.0, xs), axis=-1, keepdims=True)
        outs.append(jnp.where(first, lo, hi))
    return jnp.concatenate(outs, axis=1)


def _pre1_common(i, ps, halo, mu, w0, a0, w2p, a2p, g2p, k_k, k_a):
    prev = _shift_down(ps, _row_pick(halo, 7), i == 0)
    xs = ps + (prev - ps) * mu
    lo = xs[:, 3 * RWKV_WIDTH:]
    tl, sl = jnp.tanh(lo), jax.nn.sigmoid(lo)
    dec = w0 + _dot(tl, w2p)
    ain = a0 + _dot(lo, a2p)
    g = _dot(sl, g2p)
    wl = -jax.nn.softplus(-dec) - 0.5
    w = jnp.exp(-jnp.exp(wl))
    a = jax.nn.sigmoid(ain)
    k = xs[:, RWKV_WIDTH:2 * RWKV_WIDTH]
    z = k * k_k
    nrm = jnp.sqrt(_segsum(z * z))
    kk = z / jnp.maximum(nrm, 1e-12)
    return prev, xs, lo, tl, sl, dec, wl, w, a, g, k, nrm, kk


def rwkv_pre_fwd(p, mu, w0, a0, w2p, a2p, g2p, k_k, k_a):
    T = p.shape[0]

    def fn(i, nblk, ps, halo, mu, w0, a0, w2p, a2p, g2p, k_k, k_a):
        _, xs, _, _, _, _, _, w, a, g, k, _, kk = _pre1_common(i, ps, halo, mu, w0, a0, w2p, a2p, g2p, k_k, k_a)
        W = RWKV_WIDTH
        return xs[:, :W], w, k * (1.0 + (a - 1.0) * k_a), xs[:, 2 * W:3 * W], kk, kk * a, g

    return _rowmap(fn, name="rwkv_pre_fwd", T=T, tb=256,
                   ins=[("row", p, SHIFT_WIDTH, 0), ("prev", p, SHIFT_WIDTH, 0)]
                   + [("const", c) for c in (mu, w0, a0, w2p, a2p, g2p, k_k, k_a)],
                   outs=[("row", RWKV_WIDTH, F32)] * 7)


def rwkv_pre_bwd(p, mu, w0, a0, w2p, a2p, g2p, k_k, k_a, drs, dw, dk2s, dvs, dkk, dkka, dg):
    T = p.shape[0]

    def fn(i, nblk, ps, halo, mu, w0, a0, w2p, a2p, g2p, k_k, k_a, dr0, dr1, dw, dk20, dk21, dv0, dv1, dkk, dkka, dg):
        prev, xs, lo, tl, sl, dec, wl, w, a, g, k, nrm, kk = _pre1_common(i, ps, halo, mu, w0, a0, w2p, a2p, g2p, k_k, k_a)
        dk2 = dk20 + dk21
        dkk_t = dkk + dkka * a
        proj = jnp.where(nrm > 1e-12, kk * _segsum(dkk_t * kk), 0.0)
        dz = (dkk_t - proj) / jnp.maximum(nrm, 1e-12)
        dk = dz * k_k + dk2 * (1.0 + (a - 1.0) * k_a)
        da = dkka * kk + dk2 * k * k_a
        ddec = dw * (-w * jnp.exp(wl)) * jax.nn.sigmoid(-dec)
        dain = da * a * (1.0 - a)
        dlo = (_dot_nt(ddec, w2p) * (1.0 - tl * tl) + _dot_nt(dain, a2p) + _dot_nt(dg, g2p) * sl * (1.0 - sl))
        dxs = jnp.concatenate([dr0 + dr1, dk, dv0 + dv1, dlo], axis=1)
        s = lambda z: jnp.sum(z, axis=0, keepdims=True)
        return (dxs, s(dxs * (prev - ps)), s(ddec), s(dain), _dot_tn(tl, ddec), _dot_tn(lo, dain), _dot_tn(sl, dg),
                s(dz * k), s(dk2 * k * (a - 1.0)))

    return _rowmap(fn, name="rwkv_pre_bwd", T=T, tb=128,
                   ins=[("row", p, SHIFT_WIDTH, 0), ("prev", p, SHIFT_WIDTH, 0)]
                   + [("const", c) for c in (mu, w0, a0, w2p, a2p, g2p, k_k, k_a)]
                   + [("row", c) for c in (*drs, dw, *dk2s, *dvs, dkk, dkka, dg)],
                   outs=[("row", SHIFT_WIDTH, F32)],
                   accs=[((1, SHIFT_WIDTH), F32), ((1, RWKV_WIDTH), F32), ((1, RWKV_WIDTH), F32)]
                   + [((LORA_WIDTH, RWKV_WIDTH), F32)] * 3 + [((1, RWKV_WIDTH), F32)] * 2)


def shift_bwd(dxs, mu, dq_mem):
    T = dxs.shape[0]

    def fn(i, nblk, d, halo, mu, dq):
        nxt = _shift_up(d, _row_pick(halo, 0), i == nblk - 1)
        return jnp.concatenate([d * (1.0 - mu) + nxt * mu, dq], axis=1)

    return _rowmap(fn, name="shift_bwd", T=T, tb=256,
                   ins=[("row", dxs), ("next", dxs), ("const", mu), ("row", dq_mem)],
                   outs=[("row", SHIFT_WIDTH + MEM_WIDTH, MMD)])[0]


N_PAIRS = RWKV_HEADS // 2


def _pair_consts():
    row = lax.broadcasted_iota(jnp.int32, (HEAD_DIM, 128), 0)
    lane = lax.broadcasted_iota(jnp.int32, (HEAD_DIM, 128), 1)
    eye2 = jnp.logical_or(lane == row, lane == row + HEAD_DIM).astype(F32)
    li = lax.broadcasted_iota(jnp.int32, (128, 128), 0) < HEAD_DIM
    lj = lax.broadcasted_iota(jnp.int32, (128, 128), 1) < HEAD_DIM
    return eye2, (li == lj).astype(BF16)


def _pair_sum(p, ones2):
    n, m, l = p.shape
    s = lax.dot_general(p.reshape(n * m, l).astype(BF16), ones2, (((1,), (0,)), ((), ())), preferred_element_type=F32)
    return s.reshape(n, m, l)


def _pair_rows(row):
    return jnp.stack([row[:, p * 128:(p + 1) * 128] for p in range(N_PAIRS)], axis=0)


def _pair_flat(rows):
    return jnp.concatenate([rows[p] for p in range(N_PAIRS)], axis=1)


def _split_bf16(v):
    hi = v.astype(BF16).astype(F32)
    return hi, v - hi


def scan_fwd(r, w, k2, v, kk, kka):
    T, W = r.shape
    tc = SCAN_CHUNK
    seq = pl.BlockSpec((tc, W), lambda i: (i, 0))
    one_state = pl.BlockSpec((N_PAIRS, HEAD_DIM, 128), lambda i: (0, 0, 0))

    def body(r_ref, w_ref, k2_ref, v_ref, kk_ref, kka_ref, y_ref, st_ref, fin_ref, s_scr, vhi_scr, vlo_scr):
        @pl.when(pl.program_id(0) == 0)
        def _():
            s_scr[...] = jnp.zeros(s_scr.shape, F32)

        vhi_scr[...], vlo_scr[...] = _split_bf16(v_ref[...])
        eye2, ones2 = _pair_consts()

        def step(t, carry):
            r_t, w_t, k2_t, kk_t, kka_t, vhi_t, vlo_t = (
                _pair_rows(ref[pl.ds(t, 1), :]) for ref in (r_ref, w_ref, k2_ref, kk_ref, kka_ref, vhi_scr, vlo_scr))
            S = s_scr[...]
            sa = -_pair_sum(S * kk_t, ones2)
            vb = _pair_sum(eye2 * vhi_t, ones2) + _pair_sum(eye2 * vlo_t, ones2)
            S2 = S * w_t + sa * kka_t + vb * k2_t
            y_ref[pl.ds(t, 1), :] = _pair_flat(jnp.sum(eye2 * _pair_sum(S2 * r_t, ones2), axis=1, keepdims=True))
            s_scr[...] = S2
            st_ref[t] = S
            return carry

        lax.fori_loop(0, tc, step, 0, unroll=SCAN_UNROLL)
        fin_ref[...] = s_scr[...]

    return pl.pallas_call(
        body, name="rwkv_scan_fwd", grid=(T // tc,), in_specs=[seq] * 6,
        out_specs=[seq, pl.BlockSpec((tc, N_PAIRS, HEAD_DIM, 128), lambda i: (i, 0, 0, 0)), one_state],
        out_shape=[jax.ShapeDtypeStruct((T, W), F32), jax.ShapeDtypeStruct((T, N_PAIRS, HEAD_DIM, 128), F32),
                   jax.ShapeDtypeStruct((N_PAIRS, HEAD_DIM, 128), F32)],
        scratch_shapes=[pltpu.VMEM((N_PAIRS, HEAD_DIM, 128), F32), pltpu.VMEM((tc, W), F32), pltpu.VMEM((tc, W), F32)],
        compiler_params=_cp(("arbitrary",)),
    )(r, w, k2, v, kk, kka)


def scan_bwd(r, w, k2, v, kk, kka, states, final_state, dy):
    T, W = r.shape
    tc = SCAN_CHUNK
    nchunk = T // tc
    seq = pl.BlockSpec((tc, W), lambda i: (nchunk - 1 - i, 0))
    st_spec = pl.BlockSpec((tc, N_PAIRS, HEAD_DIM, 128), lambda i: (nchunk - 1 - i, 0, 0, 0))
    one_state = pl.BlockSpec((N_PAIRS, HEAD_DIM, 128), lambda i: (0, 0, 0))

    def body(r_ref, w_ref, k2_ref, v_ref, kk_ref, kka_ref, st_ref, fin_ref, dy_ref,
             dr_ref, dw_ref, dk2_ref, dv_ref, dkk_ref, dkka_ref, ds_scr, sc_scr, vhi_scr, vlo_scr):
        @pl.when(pl.program_id(0) == 0)
        def _():
            ds_scr[...] = jnp.zeros(ds_scr.shape, F32)
            sc_scr[...] = fin_ref[...]

        vhi_scr[...], vlo_scr[...] = _split_bf16(v_ref[...])
        eye2, ones2 = _pair_consts()
        colsum = lambda z: jnp.sum(z, axis=1, keepdims=True)

        def step(j, carry):
            t = tc - 1 - j
            r_t, w_t, k2_t, kk_t, kka_t, vhi_t, vlo_t, dy_t = (
                _pair_rows(ref[pl.ds(t, 1), :])
                for ref in (r_ref, w_ref, k2_ref, kk_ref, kka_ref, vhi_scr, vlo_scr, dy_ref))
            s_prev, s_cur = st_ref[t], sc_scr[...]
            dyb = _pair_sum(eye2 * dy_t, ones2)
            vb = _pair_sum(eye2 * vhi_t, ones2) + _pair_sum(eye2 * vlo_t, ones2)
            sa = -_pair_sum(s_prev * kk_t, ones2)
            dS = ds_scr[...] + dyb * r_t
            dsa = _pair_sum(dS * kka_t, ones2)
            ds_scr[...] = dS * w_t - dsa * kk_t
            sc_scr[...] = s_prev
            for ref, val in zip((dr_ref, dw_ref, dk2_ref, dv_ref, dkk_ref, dkka_ref),
                                (s_cur * dyb, dS * s_prev, dS * vb, eye2 * _pair_sum(dS * k2_t, ones2),
                                 -(s_prev * dsa), dS * sa)):
                ref[pl.ds(t, 1), :] = _pair_flat(colsum(val))
            return carry

        lax.fori_loop(0, tc, step, 0, unroll=SCAN_UNROLL_BWD)

    out = jax.ShapeDtypeStruct((T, W), F32)
    return pl.pallas_call(
        body, name="rwkv_scan_bwd", grid=(nchunk,), in_specs=[seq] * 6 + [st_spec, one_state, seq],
        out_specs=[seq] * 6, out_shape=[out] * 6,
        scratch_shapes=[pltpu.VMEM((N_PAIRS, HEAD_DIM, 128), F32)] * 2 + [pltpu.VMEM((tc, W), F32)] * 2,
        compiler_params=_cp(("arbitrary",)),
    )(r, w, k2, v, kk, kka, states, final_state, dy)


def _mix_common(y, r, k2, v, lnx_w, lnx_b, r_k):
    yc = y - _segsum(y) * (1.0 / HEAD_DIM)
    rstd = lax.rsqrt(_segsum(yc * yc) * (1.0 / HEAD_DIM) + LNX_EPS)
    yhat = yc * rstd
    s = _segsum(r * k2 * r_k)
    return rstd, yhat, s, yhat * lnx_w + lnx_b + s * v


def mix_gate_fwd(y, r, k2, v, g, y_mem, lnx_w, lnx_b, r_k):
    T = y.shape[0]

    def fn(i, nblk, y, r, k2, v, g, ym, lw, lb, rk):
        mix = _mix_common(y, r, k2, v, lw, lb, rk)[3]
        return jnp.concatenate([mix * g, ym], axis=1)

    return _rowmap(fn, name="mix_gate_fwd", T=T, tb=256,
                   ins=[("row", z) for z in (y, r, k2, v, g, y_mem)] + [("const", c) for c in (lnx_w, lnx_b, r_k)],
                   outs=[("row", RWKV_WIDTH + MEM_WIDTH, MMD)])[0]


def mix_gate_bwd(y, r, k2, v, g, dycat, lnx_w, lnx_b, r_k):
    T = y.shape[0]

    def fn(i, nblk, y, r, k2, v, g, dyc, lw, lb, rk):
        rstd, yhat, s, mix = _mix_common(y, r, k2, v, lw, lb, rk)
        dmix = dyc * g
        dyh = dmix * lw
        inv = 1.0 / HEAD_DIM
        dy = rstd * (dyh - _segsum(dyh) * inv - yhat * (_segsum(dyh * yhat) * inv))
        ds = _segsum(dmix * v)
        cs = lambda z: jnp.sum(z, axis=0, keepdims=True)
        return (dy, ds * k2 * rk, ds * r * rk, dmix * s, dyc * mix, cs(dmix * yhat), cs(dmix), cs(ds * r * k2))

    return _rowmap(fn, name="mix_gate_bwd", T=T, tb=256,
                   ins=[("row", z) for z in (y, r, k2, v, g)] + [("row", dycat, RWKV_WIDTH, 0)]
                   + [("const", c) for c in (lnx_w, lnx_b, r_k)],
                   outs=[("row", RWKV_WIDTH, F32)] * 5, accs=[((1, RWKV_WIDTH), F32)] * 3)


def _head_rms(x, gones):
    ms = _dot_exact01(x * x, gones) * (1.0 / HEAD_DIM)
    r = lax.rsqrt(ms + RMS_EPS)
    return r, x * r


def _head_rms_bwd(dxn_g, r, xh, gones):
    return r * (dxn_g - xh * (_dot_exact01(dxn_g * xh, gones) * (1.0 / HEAD_DIM)))


def mem_kv_fwd(mem, norm_g, w_kv, k_norm_t, *, name):
    gones = _group_ones(MEM_WIDTH)

    def body(mem_ref, g_ref, w_ref, kn_ref, go_ref, k_out, v_out):
        _, xh = _rms_stats(mem_ref[...])
        kv = _dot(xh * g_ref[...], w_ref[...])
        _, kh = _head_rms(kv[:, :MEM_WIDTH], go_ref[...])
        k_out[...] = kh * kn_ref[...]
        v_out[...] = kv[:, MEM_WIDTH:]

    return pl.pallas_call(
        body, name=name, out_shape=[jax.ShapeDtypeStruct((N_MEM, MEM_WIDTH), F32)] * 2, compiler_params=_cp(),
    )(mem, norm_g, w_kv, k_norm_t, gones)


def mem_kv_bwd(mem, norm_g, w_kv, k_norm_t, dkn, dv, *, name):
    gones, fold = _group_ones(MEM_WIDTH), _fold_ones(MEM_WIDTH)

    def body(mem_ref, g_ref, w_ref, kn_ref, go_ref, fo_ref, dkn_ref, dv_ref, dw_out, dg_out, dkg_out):
        _, xh = _rms_stats(mem_ref[...])
        hm = xh * g_ref[...]
        kv = _dot(hm, w_ref[...])
        r, kh = _head_rms(kv[:, :MEM_WIDTH], go_ref[...])
        dkn = dkn_ref[...]
        dkg_out[...] = _fold_heads(jnp.sum(dkn * kh, axis=0, keepdims=True), fo_ref[...])
        dkraw = _head_rms_bwd(dkn * kn_ref[...], r, kh, go_ref[...])
        dkv = jnp.concatenate([dkraw, dv_ref[...]], axis=1)
        dw_out[...] = _dot_tn(hm, dkv)
        dg_out[...] = jnp.sum(_dot_nt(dkv, w_ref[...]) * xh, axis=0, keepdims=True)

    return pl.pallas_call(
        body, name=name,
        out_shape=[jax.ShapeDtypeStruct((D_MODEL, 2 * MEM_WIDTH), F32), jax.ShapeDtypeStruct((1, D_MODEL), F32),
                   jax.ShapeDtypeStruct((1, HEAD_DIM), F32)],
        compiler_params=_cp(),
    )(mem, norm_g, w_kv, k_norm_t, gones, fold, dkn, dv)


def _mem_scores(qn, kn, masks, h):
    s = _dot_nt(qn * masks[h], kn) * (1.0 / math.sqrt(HEAD_DIM))
    s = s - jnp.max(s, axis=-1, keepdims=True)
    e = jnp.exp(s)
    return e / jnp.sum(e, axis=-1, keepdims=True)


def mem_attn_fwd(p, colblock, kn, v, q_norm_t, *, name):
    T = p.shape[0]
    gones, masks = _group_ones(MEM_WIDTH), _head_masks(MEM_WIDTH)

    def fn(i, nblk, q, kn, v, qg, go, masks):
        _, qh = _head_rms(q, go)
        qn = qh * qg
        out = jnp.zeros(q.shape, F32)
        for h in range(MEM_WIDTH // HEAD_DIM):
            out = out + _dot(_mem_scores(qn, kn, masks, h), v * masks[h])
        return out

    return _rowmap(fn, name=name, T=T, tb=512,
                   ins=[("row", p, MEM_WIDTH, colblock)] + [("const", c) for c in (kn, v, q_norm_t, gones, masks)],
                   outs=[("row", MEM_WIDTH, F32)])[0]


def mem_attn_bwd(p, colblock, kn, v, q_norm_t, dycat, dcolblock, *, name):
    T = p.shape[0]
    gones, masks, fold = _group_ones(MEM_WIDTH), _head_masks(MEM_WIDTH), _fold_ones(MEM_WIDTH)
    scale = 1.0 / math.sqrt(HEAD_DIM)

    def fn(i, nblk, q, dy, kn, v, qg, go, masks, fo):
        r, qh = _head_rms(q, go)
        qn = qh * qg
        dqn = jnp.zeros(q.shape, F32)
        dkn = jnp.zeros(kn.shape, F32)
        dv = jnp.zeros(v.shape, F32)
        for h in range(MEM_WIDTH // HEAD_DIM):
            pr = _mem_scores(qn, kn, masks, h)
            dyh = dy * masks[h]
            dpr = _dot_nt(dyh, v)
            ds = pr * (dpr - jnp.sum(dpr * pr, axis=-1, keepdims=True)) * scale
            dqn = dqn + _dot(ds, kn * masks[h])
            dkn = dkn + _dot_tn(ds, qn * masks[h])
            dv = dv + _dot_tn(pr, dyh)
        dqg = _fold_heads(jnp.sum(dqn * qh, axis=0, keepdims=True), fo)
        return _head_rms_bwd(dqn * qg, r, qh, go), dkn, dv, dqg

    return _rowmap(fn, name=name, T=T, tb=512,
                   ins=[("row", p, MEM_WIDTH, colblock), ("row", dycat, MEM_WIDTH, dcolblock)]
                   + [("const", c) for c in (kn, v, q_norm_t, gones, masks, fold)],
                   outs=[("row", MEM_WIDTH, F32)],
                   accs=[((N_MEM, MEM_WIDTH), F32), ((N_MEM, MEM_WIDTH), F32), ((1, HEAD_DIM), F32)])


def _ffn_conv(i, u, halo, cw, cb):
    up1 = _shift_down(u, _row_pick(halo, 7), i == 0)
    up2 = _shift_down(up1, _row_pick(halo, 6), i == 0)
    c = cb + cw[0] * up2 + cw[1] * up1 + cw[2] * u
    return up1, up2, c[:, :D_FF], c[:, D_FF:]


def ffn_act_fwd(u, cw, cb, *, name):
    T = u.shape[0]

    def fn(i, nblk, u, halo, c0, c1, c2, cb):
        _, _, gate, val = _ffn_conv(i, u, halo, (c0, c1, c2), cb)
        return jax.nn.silu(gate) * val

    return _rowmap(fn, name=name, T=T, tb=128, ins=[("row", u), ("prev", u)] + [("const", c) for c in (*cw, cb)],
                   outs=[("row", D_FF, MMD)])[0]


def ffn_act_bwd(u, cw, cb, dz, *, name):
    T = u.shape[0]

    def fn(i, nblk, u, halo, c0, c1, c2, cb, dz):
        up1, up2, gate, val = _ffn_conv(i, u, halo, (c0, c1, c2), cb)
        sg = jax.nn.sigmoid(gate)
        dgate = dz * val * sg * (1.0 + gate * (1.0 - sg))
        dval = dz * gate * sg
        dc = jnp.concatenate([dgate, dval], axis=1)
        s = lambda z: jnp.sum(z, axis=0, keepdims=True)
        return dc, s(dc * up2), s(dc * up1), s(dc * u), s(dc)

    return _rowmap(fn, name=name, T=T, tb=128,
                   ins=[("row", u), ("prev", u)] + [("const", c) for c in (*cw, cb)] + [("row", dz)],
                   outs=[("row", 2 * D_FF, F32)], accs=[((1, 2 * D_FF), F32)] * 4)


def conv_bwd(dc, cw, *, name):
    T = dc.shape[0]

    def fn(i, nblk, d, halo, c0, c1, c2):
        last = i == nblk - 1
        n1 = _shift_up(d, _row_pick(halo, 0), last)
        n2 = _shift_up(n1, _row_pick(halo, 1), last)
        return c2 * d + c1 * n1 + c0 * n2

    return _rowmap(fn, name=name, T=T, tb=128, ins=[("row", dc), ("next", dc)] + [("const", c) for c in cw],
                   outs=[("row", 2 * D_FF, MMD)])[0]


def _rope_swap(z):
    lane = lax.broadcasted_iota(jnp.int32, z.shape, 1) % HEAD_DIM
    w = z.shape[1]
    return jnp.where(lane < HEAD_DIM // 2, pltpu.roll(z, w - HEAD_DIM // 2, axis=1), pltpu.roll(z, HEAD_DIM // 2, axis=1))


def rope_tables(T):
    inv = ROPE_THETA ** (-jnp.arange(0, HEAD_DIM, 2, dtype=F32) / HEAD_DIM)
    ang = jnp.arange(T, dtype=F32)[:, None] * inv[None, :]
    cos, sin = jnp.cos(ang), jnp.sin(ang)
    return jnp.concatenate([cos, cos, cos, cos], axis=1), jnp.concatenate([-sin, sin, -sin, sin], axis=1)


def _rope_wide(t):
    return jnp.tile(t, (1, DIL_WIDTH // t.shape[1]))


def qk_fwd(kvp, pb, kg_t, qg_t, cos, sin):
    T = kvp.shape[0]
    gones = _group_ones(DIL_WIDTH)

    def fn(i, nblk, kraw, vraw, qraw, kg, qg, c, s, go):
        c, s = _rope_wide(c), _rope_wide(s)
        outs = []
        for raw, g in ((qraw, qg), (kraw, kg)):
            _, xh = _head_rms(raw, go)
            z = xh * g
            outs.append(z * c + _rope_swap(z) * s)
        return outs[0], outs[1], vraw

    return _rowmap(fn, name="qk_fwd", T=T, tb=256,
                   ins=[("row", kvp, DIL_WIDTH, 0), ("row", kvp, DIL_WIDTH, 1), ("row", pb, DIL_WIDTH, 0)]
                   + [("const", kg_t), ("const", qg_t), ("row", cos), ("row", sin), ("const", gones)],
                   outs=[("row", DIL_WIDTH, MMD)] * 3)


def qk_bwd(kvp, pb, kg_t, qg_t, cos, sin, dq, dk, dv, dq_mem):
    T = kvp.shape[0]
    gones, fold = _group_ones(DIL_WIDTH), _fold_ones(DIL_WIDTH)

    def fn(i, nblk, kraw, qraw, kg, qg, c, s, go, fo, dq, dk, dv, dqm):
        c, s = _rope_wide(c), _rope_wide(s)
        res, dgs = [], []
        for raw, g, d in ((qraw, qg, dq), (kraw, kg, dk)):
            r, xh = _head_rms(raw, go)
            dz = d * c + _rope_swap(d * s)
            dgs.append(_fold_heads(jnp.sum(dz * xh, axis=0, keepdims=True), fo))
            res.append(_head_rms_bwd(dz * g, r, xh, go))
        return (jnp.concatenate([res[0], dqm], axis=1), jnp.concatenate([res[1], dv], axis=1), dgs[0], dgs[1])

    return _rowmap(fn, name="qk_bwd", T=T, tb=256,
                   ins=[("row", kvp, DIL_WIDTH, 0), ("row", pb, DIL_WIDTH, 0), ("const", kg_t), ("const", qg_t),
                        ("row", cos), ("row", sin), ("const", gones), ("const", fold),
                        ("row", dq), ("row", dk), ("row", dv), ("row", dq_mem)],
                   outs=[("row", DIL_WIDTH + MEM_WIDTH, MMD), ("row", 2 * DIL_WIDTH, MMD)],
                   accs=[((1, HEAD_DIM), F32)] * 2)


def _band(kind):
    i = lax.broadcasted_iota(jnp.int32, (DIL_BLOCK, DIL_BLOCK), 0)
    j = lax.broadcasted_iota(jnp.int32, (DIL_BLOCK, DIL_BLOCK), 1)
    return (j <= i) if kind == "cur" else (j >= i)


def dil_attn_fwd(q, k, v, seq_blocks, *, name):
    T, W = q.shape
    nb = T // DIL_BLOCK
    masks = _head_masks(W)
    cur = pl.BlockSpec((DIL_BLOCK, W), lambda n: (n, 0))
    prv = pl.BlockSpec((DIL_BLOCK, W), lambda n: (jnp.maximum(n - 1, 0), 0))
    scale = 1.0 / math.sqrt(HEAD_DIM)

    def body(q_ref, kc_ref, kp_ref, vc_ref, vp_ref, m_ref, o_ref, l_ref):
        n = pl.program_id(0)
        has_prev = (n % seq_blocks) != 0
        q = q_ref[...].astype(F32)
        kc, kp = kc_ref[...].astype(F32), kp_ref[...].astype(F32)
        vc, vp = vc_ref[...].astype(F32), vp_ref[...].astype(F32)
        ok_c = _band("cur")
        ok_p = jnp.logical_and(_band("prev"), has_prev)
        o = jnp.zeros((DIL_BLOCK, W), F32)
        lse = jnp.zeros((DIL_BLOCK, W), F32)
        for h in range(W // HEAD_DIM):
            mh = m_ref[h]
            qh = q * mh
            sc = jnp.where(ok_c, _dot_nt(qh, kc) * scale, NEG_INF)
            sp = jnp.where(ok_p, _dot_nt(qh, kp) * scale, NEG_INF)
            mx = jnp.maximum(jnp.max(sc, axis=-1, keepdims=True), jnp.max(sp, axis=-1, keepdims=True))
            ec, ep = jnp.exp(sc - mx), jnp.exp(sp - mx)
            den = jnp.sum(ec, axis=-1, keepdims=True) + jnp.sum(ep, axis=-1, keepdims=True)
            o = o + (_dot(ec, vc * mh) + _dot(ep, vp * mh)) / den
            lse = lse + (mx + jnp.log(den)) * mh
        o_ref[...] = o
        l_ref[...] = lse

    return pl.pallas_call(
        body, name=name, grid=(nb,), in_specs=[cur, cur, prv, cur, prv, pl.BlockSpec(masks.shape, lambda n: (0, 0, 0))],
        out_specs=[cur, cur], out_shape=[jax.ShapeDtypeStruct((T, W), F32)] * 2,
        compiler_params=_cp(("parallel",)),
    )(q, k, k, v, v, masks)


def dil_attn_bwd(q, k, v, o, lse, do, dlse, seq_blocks, *, name):
    T, W = q.shape
    nb = T // DIL_BLOCK
    masks = _head_masks(W)
    cur = pl.BlockSpec((DIL_BLOCK, W), lambda n: (n, 0))
    prv = pl.BlockSpec((DIL_BLOCK, W), lambda n: (jnp.maximum(n - 1, 0), 0))
    nxt = pl.BlockSpec((DIL_BLOCK, W), lambda n: (jnp.minimum(n + 1, nb - 1), 0))
    scale = 1.0 / math.sqrt(HEAD_DIM)

    def body(qc_ref, qn_ref, kc_ref, kp_ref, vc_ref, vp_ref, oc_ref, on_ref, lc_ref, ln_ref, doc_ref, don_ref,
             dlc_ref, dln_ref, m_ref, dq_ref, dk_ref, dv_ref):
        n = pl.program_id(0)
        has_prev = (n % seq_blocks) != 0
        has_next = jnp.logical_and(((n + 1) % seq_blocks) != 0, n + 1 < nb)
        f = lambda ref: ref[...].astype(F32)
        qc, qn, kc, kp, vc, vp = f(qc_ref), f(qn_ref), f(kc_ref), f(kp_ref), f(vc_ref), f(vp_ref)
        doc, don = doc_ref[...], don_ref[...]
        ok_c = _band("cur")
        ok_p = jnp.logical_and(_band("prev"), has_prev)
        ok_n = jnp.logical_and(_band("prev"), has_next)
        dq = jnp.zeros((DIL_BLOCK, W), F32)
        dk = jnp.zeros((DIL_BLOCK, W), F32)
        dv = jnp.zeros((DIL_BLOCK, W), F32)

        def side(qh, kk, vv, doh, lse_h, corr, ok):
            s = _dot_nt(qh, kk) * scale
            pr = jnp.where(ok, jnp.exp(jnp.where(ok, s, NEG_INF) - lse_h), 0.0)
            ds = pr * (_dot_nt(doh, vv) + corr) * scale
            return pr, ds

        for h in range(W // HEAD_DIM):
            mh = m_ref[h]
            red = lambda z: jnp.sum(z * mh, axis=-1, keepdims=True)
            qh, doh = qc * mh, doc * mh
            lse_h = red(lc_ref[...]) * (1.0 / HEAD_DIM)
            corr = red(dlc_ref[...]) - red(doc * oc_ref[...])
            pr_c, ds_c = side(qh, kc, vc * mh, doh, lse_h, corr, ok_c)
            _, ds_p = side(qh, kp, vp * mh, doh, lse_h, corr, ok_p)
            dq = dq + _dot(ds_c, kc * mh) + _dot(ds_p, kp * mh)
            dk = dk + _dot_tn(ds_c, qh)
            dv = dv + _dot_tn(pr_c, doh)
            qh2, doh2 = qn * mh, don * mh
            lse_2 = red(ln_ref[...]) * (1.0 / HEAD_DIM)
            corr2 = red(dln_ref[...]) - red(don * on_ref[...])
            pr_n, ds_n = side(qh2, kc, vc * mh, doh2, lse_2, corr2, ok_n)
            dk = dk + _dot_tn(ds_n, qh2)
            dv = dv + _dot_tn(pr_n, doh2)
        dq_ref[...] = dq
        dk_ref[...] = dk
        dv_ref[...] = dv

    return pl.pallas_call(
        body, name=name, grid=(nb,),
        in_specs=[cur, nxt, cur, prv, cur, prv, cur, nxt, cur, nxt, cur, nxt, cur, nxt,
                  pl.BlockSpec(masks.shape, lambda n: (0, 0, 0))],
        out_specs=[cur] * 3, out_shape=[jax.ShapeDtypeStruct((T, W), F32)] * 3,
        compiler_params=_cp(("parallel",)),
    )(q, q, k, k, v, v, o, o, lse, lse, do, do, dlse, dlse, masks)


def _mix_weights(ls):
    m = jnp.maximum(jnp.maximum(ls[0], ls[1]), ls[2])
    es = [jnp.exp(l - m) for l in ls]
    den = es[0] + es[1] + es[2]
    return [e / den for e in es]


def mix_fwd(os_, ls, y_mem):
    T = y_mem.shape[0]

    def fn(i, nblk, o0, o1, o2, l0, l1, l2, ym):
        w = _mix_weights((l0, l1, l2))
        return jnp.concatenate([w[0] * o0 + w[1] * o1 + w[2] * o2, ym], axis=1)

    return _rowmap(fn, name="mix_fwd", T=T, tb=512, ins=[("row", z) for z in (*os_, *ls, y_mem)],
                   outs=[("row", 2 * MEM_WIDTH, MMD)])[0]


def mix_bwd(os_, ls, dycat):
    T = dycat.shape[0]

    def fn(i, nblk, o0, o1, o2, l0, l1, l2, dy):
        w = _mix_weights((l0, l1, l2))
        os3 = (o0, o1, o2)
        dws = [dy * o for o in os3]
        tot = w[0] * dws[0] + w[1] * dws[1] + w[2] * dws[2]
        return tuple(wg * dy for wg in w) + tuple(wg * (dw - tot) for wg, dw in zip(w, dws))

    return _rowmap(fn, name="mix_bwd", T=T, tb=512,
                   ins=[("row", z) for z in (*os_, *ls)] + [("row", dycat, MEM_WIDTH, 0)],
                   outs=[("row", MEM_WIDTH, F32)] * 6)


def loss_fwd_bwd(y, target):
    T, D = y.shape

    def fn(i, nblk, y, t):
        e = y - t
        return e * (1.0 / D), jnp.zeros((8, 128), F32) + jnp.sum(e * e) * (0.5 / D)

    return _rowmap(fn, name="loss", T=T, tb=512, ins=[("row", y), ("row", target)], outs=[("row", D, F32)],
                   accs=[((8, 128), F32)])


def _to_residues(z, dil):
    T, W = z.shape
    return z.reshape(T // dil, dil, W).transpose(1, 0, 2).reshape(T, W)


def _from_residues(z, dil):
    T, W = z.shape
    return z.reshape(dil, T // dil, W).transpose(1, 0, 2).reshape(T, W)


def _pad_rows(w, rows):
    return jnp.concatenate([w, jnp.zeros((rows - w.shape[0], w.shape[1]), w.dtype)], axis=0)


def _tile_heads(g, width):
    return jnp.tile(g.reshape(1, HEAD_DIM), (1, width // HEAD_DIM))


def _conv_rows(W, i):
    return [W["ffn_conv_w"][i][j:j + 1] for j in range(3)]


def _ffn_fwd(x, i, W):
    hn = rms_fwd(x, [W["ffn_norm"][i:i + 1]], name=f"ffn_rms{i}")[0]
    u = _mm(hn, W["ffn_w_up"][i], name=f"ffn_up{i}")
    z = ffn_act_fwd(u, _conv_rows(W, i), W["ffn_conv_b"][i:i + 1], name=f"ffn_act{i}")
    out = _mm(z, W["ffn_w_down"][i], add=x, name=f"ffn_down{i}")
    return out, (x, hn, u, z)


def _ffn_bwd(dout, i, W, saved, G):
    x, hn, u, z = saved
    dz = _mm(dout, W["ffn_w_down"][i], tb=True, name=f"ffn_down_dx{i}")
    G["ffn_w_down"][i] = _mm(z, dout, ta=True, name=f"ffn_down_dw{i}")
    dc, dw0, dw1, dw2, db = ffn_act_bwd(u, _conv_rows(W, i), W["ffn_conv_b"][i:i + 1], dz, name=f"ffn_act_bwd{i}")
    G["ffn_conv_w"][i] = jnp.concatenate([dw0, dw1, dw2], axis=0)
    G["ffn_conv_b"][i] = db[0]
    du = conv_bwd(dc, _conv_rows(W, i), name=f"ffn_conv_bwd{i}")
    dhn = _mm(du, W["ffn_w_up"][i], tb=True, name=f"ffn_up_dx{i}")
    G["ffn_w_up"][i] = _mm(hn, du, ta=True, name=f"ffn_up_dw{i}")
    dx, dg = rms_bwd(x, [W["ffn_norm"][i:i + 1]], [dhn], dout, name=f"ffn_rms_bwd{i}")
    G["ffn_norm"][i] = dg[0]
    return dx


def local_step(x, mem, target, W):
    T = x.shape[0]
    G = {"ffn_w_down": [None, None], "ffn_w_up": [None, None], "ffn_conv_w": [None, None],
         "ffn_conv_b": [None, None], "ffn_norm": [None, None], "attn_norm": [None, None], "mem_norm": [None, None],
         "mem_w_kv": [None, None], "mem_q_norm": [None, None], "mem_k_norm": [None, None]}
    mu, w0, a0 = W["a_mu"], W["a_w0"], W["a_a0"]
    w2p, a2p, g2p = (_pad_rows(W["a_w2"][0], LORA_WIDTH),
                     jnp.concatenate([jnp.zeros((64, RWKV_WIDTH), MMD), W["a_a2"][0],
                                      jnp.zeros((128, RWKV_WIDTH), MMD)], axis=0),
                     jnp.concatenate([jnp.zeros((128, RWKV_WIDTH), MMD), W["a_g2"][0]], axis=0))
    k_k, k_a, lnx_w, lnx_b = W["a_k_k"], W["a_k_a"], W["a_lnx_w"], W["a_lnx_b"]
    r_k = W["a_r_k"].reshape(1, RWKV_WIDTH)
    memkv = []
    for i in range(2):
        memkv.append(mem_kv_fwd(mem, W["mem_norm"][i:i + 1], W["mem_w_kv"][i], _tile_heads(W["mem_k_norm"][i], MEM_WIDTH),
                                name=f"mem_kv{i}"))

    h0 = rms_fwd(x, [W["attn_norm"][0:1]], name="attn_rms0")[0]
    p = _mm(h0, W["a_w_in"][0], name="a_in")
    r, w, k2, v, kk, kka, g = rwkv_pre_fwd(p, mu, w0, a0, w2p, a2p, g2p, k_k, k_a)
    y, states, final_state = scan_fwd(r, w, k2, v, kk, kka)
    qg0 = _tile_heads(W["mem_q_norm"][0], MEM_WIDTH)
    y_mem0 = mem_attn_fwd(p, SHIFT_WIDTH // MEM_WIDTH, memkv[0][0], memkv[0][1], qg0, name="mem_attn0")
    ycat0 = mix_gate_fwd(y, r, k2, v, g, y_mem0, lnx_w, lnx_b, r_k)
    x1 = _mm(ycat0, W["a_w_out"][0], add=x, name="a_out")
    x2, ffn0 = _ffn_fwd(x1, 0, W)

    h1, hkv = rms_fwd(x2, [W["attn_norm"][1:2], W["kv_norm"].reshape(1, -1)], name="attn_rms1")
    kvp = _mm(hkv, W["kv_w"], name="kv_in")
    pb = _mm(h1, W["b_w_in"][0], name="b_in")
    cos, sin = rope_tables(T)
    kg_t, qg_t = _tile_heads(W["kv_k_norm"], DIL_WIDTH), _tile_heads(W["b_q_norm"][0], DIL_WIDTH)
    q, ksh, vsh = qk_fwd(kvp, pb, kg_t, qg_t, cos, sin)
    os_, ls, grp = [], [], []
    for gi, (win, dil) in enumerate(DIL_GROUPS):
        sl = slice(gi * MEM_WIDTH, (gi + 1) * MEM_WIDTH)
        qg_, kg_, vg_ = (_to_residues(z[:, sl], dil) for z in (q, ksh, vsh))
        o_r, l_r = dil_attn_fwd(qg_, kg_, vg_, T // dil // DIL_BLOCK, name=f"dil_fwd{gi}")
        grp.append((qg_, kg_, vg_, o_r, l_r))
        os_.append(_from_residues(o_r, dil))
        ls.append(_from_residues(l_r, dil))
    qg1 = _tile_heads(W["mem_q_norm"][1], MEM_WIDTH)
    y_mem1 = mem_attn_fwd(pb, DIL_WIDTH // MEM_WIDTH, memkv[1][0], memkv[1][1], qg1, name="mem_attn1")
    ycat1 = mix_fwd(os_, ls, y_mem1)
    x3 = _mm(ycat1, W["b_w_out"][0], add=x2, name="b_out")
    x4, ffn1 = _ffn_fwd(x3, 1, W)

    dx4, loss = loss_fwd_bwd(x4, target)

    dx3 = _ffn_bwd(dx4, 1, W, ffn1, G)
    dycat1 = _mm(dx3, W["b_w_out"][0], tb=True, name="b_out_dx")
    G["b_w_out"] = _mm(ycat1, dx3, ta=True, name="b_out_dw")[None]
    dq_mem1, dkn1, dvm1, dqg1 = mem_attn_bwd(pb, DIL_WIDTH // MEM_WIDTH, memkv[1][0], memkv[1][1], qg1, dycat1, 1,
                                             name="mem_attn_bwd1")
    G["mem_q_norm"][1] = dqg1[0]
    d_os_ls = mix_bwd(os_, ls, dycat1)
    dqs, dks, dvs = [], [], []
    for gi, (win, dil) in enumerate(DIL_GROUPS):
        qg_, kg_, vg_, o_r, l_r = grp[gi]
        do_r, dl_r = _to_residues(d_os_ls[gi], dil), _to_residues(d_os_ls[3 + gi], dil)
        dq_r, dk_r, dv_r = dil_attn_bwd(qg_, kg_, vg_, o_r, l_r, do_r, dl_r, T // dil // DIL_BLOCK, name=f"dil_bwd{gi}")
        dqs.append(_from_residues(dq_r, dil))
        dks.append(_from_residues(dk_r, dil))
        dvs.append(_from_residues(dv_r, dil))
    dq, dk, dv = (jnp.concatenate(z, axis=1) for z in (dqs, dks, dvs))
    dpb, dkvp, dqn_g, dkn_g = qk_bwd(kvp, pb, kg_t, qg_t, cos, sin, dq, dk, dv, dq_mem1)
    G["b_q_norm"] = dqn_g
    G["kv_k_norm"] = dkn_g[0]
    dh1 = _mm(dpb, W["b_w_in"][0], tb=True, name="b_in_dx")
    G["b_w_in"] = _mm(h1, dpb, ta=True, name="b_in_dw")[None]
    dhkv = _mm(dkvp, W["kv_w"], tb=True, name="kv_in_dx")
    G["kv_w"] = _mm(hkv, dkvp, ta=True, name="kv_in_dw")
    dx2, dg1, dgkv = rms_bwd(x2, [W["attn_norm"][1:2], W["kv_norm"].reshape(1, -1)], [dh1, dhkv], dx3,
                             name="attn_rms_bwd1")
    G["attn_norm"][1] = dg1[0]
    G["kv_norm"] = dgkv[0]

    dx1 = _ffn_bwd(dx2, 0, W, ffn0, G)
    dycat0 = _mm(dx1, W["a_w_out"][0], tb=True, name="a_out_dx")
    G["a_w_out"] = _mm(ycat0, dx1, ta=True, name="a_out_dw")[None]
    dq_mem0, dkn0, dvm0, dqg0 = mem_attn_bwd(p, SHIFT_WIDTH // MEM_WIDTH, memkv[0][0], memkv[0][1], qg0, dycat0,
                                             RWKV_WIDTH // MEM_WIDTH, name="mem_attn_bwd0")
    G["mem_q_norm"][0] = dqg0[0]
    dy, dr_b, dk2_b, dv_b, dg, dlw, dlb, drk = mix_gate_bwd(y, r, k2, v, g, dycat0, lnx_w, lnx_b, r_k)
    dr, dw, dk2, dv, dkk, dkka = scan_bwd(r, w, k2, v, kk, kka, states, final_state, dy)
    dxs, dmu, dw0, da0, dw2p, da2p, dg2p, dk_k, dk_a = rwkv_pre_bwd(
        p, mu, w0, a0, w2p, a2p, g2p, k_k, k_a, (dr, dr_b), dw, (dk2, dk2_b), (dv, dv_b), dkk, dkka, dg)
    dp = shift_bwd(dxs, mu, dq_mem0)
    G.update(a_mu=dmu, a_w0=dw0, a_a0=da0, a_w2=dw2p[None, :64], a_a2=da2p[None, 64:128], a_g2=dg2p[None, 128:],
             a_k_k=dk_k, a_k_a=dk_a, a_r_k=drk.reshape(1, RWKV_HEADS, HEAD_DIM), a_lnx_w=dlw, a_lnx_b=dlb)
    dh0 = _mm(dp, W["a_w_in"][0], tb=True, name="a_in_dx")
    G["a_w_in"] = _mm(h0, dp, ta=True, name="a_in_dw")[None]
    grad_x, dg0 = rms_bwd(x, [W["attn_norm"][0:1]], [dh0], dx1, name="attn_rms_bwd0")
    G["attn_norm"][0] = dg0[0]

    for i, (dkn, dvm) in enumerate(((dkn0, dvm0), (dkn1, dvm1))):
        dwkv, dgm, dkg = mem_kv_bwd(mem, W["mem_norm"][i:i + 1], W["mem_w_kv"][i],
                                    _tile_heads(W["mem_k_norm"][i], MEM_WIDTH), dkn, dvm, name=f"mem_kv_bwd{i}")
        G["mem_w_kv"][i], G["mem_norm"][i], G["mem_k_norm"][i] = dwkv, dgm[0], dkg[0]
    for n in list(G):
        if isinstance(G[n], list):
            G[n] = jnp.stack(G[n], axis=0)
    return loss, grad_x, G


HBM_SPEC = pl.BlockSpec(memory_space=pltpu.HBM)


def _mesh_pos():
    return lax.axis_index("x"), lax.axis_index("y"), lax.axis_index("c")


def _other_chips(x, y):
    return [(1 - x, y), (x, 1 - y), (1 - x, 1 - y)]


def _remote(send_sems, recv_sems, k, src, dst, to):
    return pltpu.make_async_remote_copy(src_ref=src, dst_ref=dst, send_sem=send_sems.at[k], recv_sem=recv_sems.at[k],
                                        device_id=to, device_id_type=MESH)


def _comm_call(body, name, ins, out_shape, n_remote):
    scratch = [pltpu.SemaphoreType.DMA((n_remote,)), pltpu.SemaphoreType.DMA((n_remote,))]
    return pl.pallas_call(body, name=name, in_specs=[HBM_SPEC] * len(ins), out_specs=[HBM_SPEC] * len(out_shape),
                          out_shape=out_shape, scratch_shapes=scratch)(*ins)


def comm_gather(wbig, wsm):
    def body(wb, ws, ob, os_, send_sems, recv_sems):
        x, y, c = _mesh_pos()
        s = 2 * x + y
        me, sibling = (x, y, c), (x, y, 1 - c)
        chips = _other_chips(x, y)
        rc = functools.partial(_remote, send_sems, recv_sems)
        first = []
        for j, (cx, cy) in enumerate(chips):
            first.append(rc(j, wb.at[c], ob.at[s, c], (cx, cy, c)))
            first.append(rc(6 + j, ws, os_.at[s], (cx, cy, c)))
        for cp in first:
            cp.start()
        passed = []
        for j, (cx, cy) in enumerate(chips):
            blk = ob.at[2 * cx + cy, c]
            rc(j, blk, blk, me).wait_recv()
            passed.append(rc(3 + j, blk, blk, sibling))
            passed[-1].start()
        for j, (cx, cy) in enumerate(chips):
            blk = ob.at[2 * cx + cy, 1 - c]
            rc(3 + j, blk, blk, me).wait_recv()
            sb = os_.at[2 * cx + cy]
            rc(6 + j, sb, sb, me).wait_recv()
        for cp in first + passed:
            cp.wait_send()

    out_shape = [jax.ShapeDtypeStruct((N_CHIPS, *wbig.shape), wbig.dtype),
                 jax.ShapeDtypeStruct((N_CHIPS, *wsm.shape), wsm.dtype)]
    return _comm_call(body, "comm_gather", [wbig, wsm], out_shape, 9)


def comm_pair_exchange(gb, gs):
    def body(gb_ref, gs_ref, rb_ref, rs_ref, send_sems, recv_sems):
        x, y, c = _mesh_pos()
        sibling = (x, y, 1 - c)
        rc = functools.partial(_remote, send_sems, recv_sems)
        cps = [rc(r, gb_ref.at[r, 1 - c], rb_ref.at[r], sibling) for r in range(N_CHIPS)]
        cps.append(rc(N_CHIPS, gs_ref.at[1 - c], rs_ref, sibling))
        for cp in cps:
            cp.start()
        for cp in cps:
            cp.wait()

    out_shape = [jax.ShapeDtypeStruct((N_CHIPS, *gb.shape[2:]), gb.dtype), jax.ShapeDtypeStruct(gs.shape[1:], gs.dtype)]
    return _comm_call(body, "comm_pair_exchange", [gb, gs], out_shape, N_CHIPS + 1)


def comm_chip_exchange(hb, hs):
    def body(hb_ref, hs_ref, qb_ref, qs_ref, send_sems, recv_sems):
        x, y, c = _mesh_pos()
        s = 2 * x + y
        me = (x, y, c)
        chips = _other_chips(x, y)
        rc = functools.partial(_remote, send_sems, recv_sems)
        cps = []
        for j, (cx, cy) in enumerate(chips):
            cps.append(rc(j, hb_ref.at[2 * cx + cy], qb_ref.at[s], (cx, cy, c)))
            cps.append(rc(3 + j, hs_ref, qs_ref.at[s], (cx, cy, c)))
        for cp in cps:
            cp.start()
        for j, (cx, cy) in enumerate(chips):
            blk = qb_ref.at[2 * cx + cy]
            rc(j, blk, blk, me).wait_recv()
            sb = qs_ref.at[2 * cx + cy]
            rc(3 + j, sb, sb, me).wait_recv()
        for cp in cps:
            cp.wait_send()

    out_shape = [jax.ShapeDtypeStruct(hb.shape, hb.dtype), jax.ShapeDtypeStruct((N_CHIPS, *hs.shape), hs.dtype)]
    return _comm_call(body, "comm_chip_exchange", [hb, hs], out_shape, 6)


def comm_pair_share(gh, gsh):
    def body(gh_ref, gs_ref, ob, os_, send_sems, recv_sems):
        x, y, c = _mesh_pos()
        rc = functools.partial(_remote, send_sems, recv_sems)
        cps = [rc(0, gh_ref, ob, (x, y, 1 - c)), rc(1, gs_ref, os_, (x, y, 1 - c))]
        for cp in cps:
            cp.start()
        for cp in cps:
            cp.wait()

    out_shape = [jax.ShapeDtypeStruct(gh.shape, gh.dtype), jax.ShapeDtypeStruct(gsh.shape, gsh.dtype)]
    return _comm_call(body, "comm_pair_share", [gh, gsh], out_shape, 2)


def add_pairs(a, b, out_dtype, *, name, tb):
    T, L = a.shape
    return _rowmap(lambda i, n, p, q: p + q, name=name, T=T, tb=tb, ins=[("row", a), ("row", b)],
                   outs=[("row", L, out_dtype)])[0]


def add_chips(parts, *, name, tb):
    T, L = parts[0].shape

    def fn(i, n, p0, p1, p2, p3):
        f = lambda z: z.astype(F32)
        return ((f(p0) + f(p1)) + f(p2)) + f(p3)

    return _rowmap(fn, name=name, T=T, tb=tb, ins=[("row", p) for p in parts], outs=[("row", L, F32)])[0]


def adamw(g, w, m, v, *, name, tb):
    T, L = g.shape

    def fn(i, n, g, w, m, v):
        m2 = ADAM_B1 * m + (1.0 - ADAM_B1) * g
        v2 = ADAM_B2 * v + (1.0 - ADAM_B2) * (g * g)
        m_hat = m2 / (1.0 - ADAM_B1 ** ADAM_STEP)
        v_hat = v2 / (1.0 - ADAM_B2 ** ADAM_STEP)
        return -ADAM_LR * (m_hat / (jnp.sqrt(v_hat) + ADAM_EPS) + ADAM_WD * w), m2, v2

    return _rowmap(fn, name=name, T=T, tb=tb, ins=[("row", z) for z in (g, w, m, v)], outs=[("row", L, F32)] * 3)


BIG_LANES = 1024
SMALL_LANES = 128


def _flat_cat(arrs, total, dtype):
    parts = [a.reshape(-1).astype(dtype) for a in arrs]
    n = sum(p.shape[0] for p in parts)
    assert n <= total, (n, total)
    if n < total:
        parts.append(jnp.zeros((total - n,), dtype))
    return jnp.concatenate(parts)


def _split_flat(flat, shapes):
    out, off = [], 0
    for shp in shapes:
        n = math.prod(shp)
        out.append(flat[off:off + n].reshape(shp))
        off += n
    return out


def _round_up(n, m):
    return -(-n // m) * m


def _full_shape(shard_shape, axis):
    return tuple(d * N_CHIPS if i == axis else d for i, d in enumerate(shard_shape))


def kernel(x, mem, attn_norm, a_w_in, a_mu, a_w0, a_w2, a_a0, a_a2, a_g2, a_k_k, a_k_a, a_r_k, a_lnx_w, a_lnx_b, a_w_out, kv_norm, kv_w, kv_k_norm, b_w_in, b_q_norm, b_w_out, mem_norm, mem_w_kv, mem_q_norm, mem_k_norm, ffn_norm, ffn_w_up, ffn_conv_w, ffn_conv_b, ffn_w_down, loss_target, m_attn_norm, m_a_w_in, m_a_mu, m_a_w0, m_a_w2, m_a_a0, m_a_a2, m_a_g2, m_a_k_k, m_a_k_a, m_a_r_k, m_a_lnx_w, m_a_lnx_b, m_a_w_out, m_kv_norm, m_kv_w, m_kv_k_norm, m_b_w_in, m_b_q_norm, m_b_w_out, m_mem_norm, m_mem_w_kv, m_mem_q_norm, m_mem_k_norm, m_ffn_norm, m_ffn_w_up, m_ffn_conv_w, m_ffn_conv_b, m_ffn_w_down, v_attn_norm, v_a_w_in, v_a_mu, v_a_w0, v_a_w2, v_a_a0, v_a_a2, v_a_g2, v_a_k_k, v_a_k_a, v_a_r_k, v_a_lnx_w, v_a_lnx_b, v_a_w_out, v_kv_norm, v_kv_w, v_kv_k_norm, v_b_w_in, v_b_q_norm, v_b_w_out, v_mem_norm, v_mem_w_kv, v_mem_q_norm, v_mem_k_norm, v_ffn_norm, v_ffn_w_up, v_ffn_conv_w, v_ffn_conv_b, v_ffn_w_down):
    args = (attn_norm, a_w_in, a_mu, a_w0, a_w2, a_a0, a_a2, a_g2, a_k_k, a_k_a, a_r_k, a_lnx_w, a_lnx_b, a_w_out, kv_norm, kv_w, kv_k_norm, b_w_in, b_q_norm, b_w_out, mem_norm, mem_w_kv, mem_q_norm, mem_k_norm, ffn_norm, ffn_w_up, ffn_conv_w, ffn_conv_b, ffn_w_down)
    ms = (m_attn_norm, m_a_w_in, m_a_mu, m_a_w0, m_a_w2, m_a_a0, m_a_a2, m_a_g2, m_a_k_k, m_a_k_a, m_a_r_k, m_a_lnx_w, m_a_lnx_b, m_a_w_out, m_kv_norm, m_kv_w, m_kv_k_norm, m_b_w_in, m_b_q_norm, m_b_w_out, m_mem_norm, m_mem_w_kv, m_mem_q_norm, m_mem_k_norm, m_ffn_norm, m_ffn_w_up, m_ffn_conv_w, m_ffn_conv_b, m_ffn_w_down)
    vs = (v_attn_norm, v_a_w_in, v_a_mu, v_a_w0, v_a_w2, v_a_a0, v_a_a2, v_a_g2, v_a_k_k, v_a_k_a, v_a_r_k, v_a_lnx_w, v_a_lnx_b, v_a_w_out, v_kv_norm, v_kv_w, v_kv_k_norm, v_b_w_in, v_b_q_norm, v_b_w_out, v_mem_norm, v_mem_w_kv, v_mem_q_norm, v_mem_k_norm, v_ffn_norm, v_ffn_w_up, v_ffn_conv_w, v_ffn_conv_b, v_ffn_w_down)
    w_sh, m_sh, v_sh = (dict(zip(WEIGHTS, z)) for z in (args, ms, vs))
    xi, yi, ci = _mesh_pos()
    chip = 2 * xi + yi
    big_names, big_axes = [n for n, _ in BIG], dict(BIG)
    ss_names, ss_axes = [n for n, _ in SMALL_SHARDED], dict(SMALL_SHARDED)
    big_shapes = [w_sh[n].shape for n in big_names]
    ss_shapes = [w_sh[n].shape for n in ss_names]
    n_big = sum(math.prod(s) for s in big_shapes)
    assert n_big % (2 * 8 * BIG_LANES) == 0
    mh = n_big // (2 * BIG_LANES)
    n_ss = _round_up(sum(math.prod(s) for s in ss_shapes), 8 * SMALL_LANES)

    wbig = _flat_cat([w_sh[n] for n in big_names], n_big, MMD).reshape(2, mh, BIG_LANES)
    wsm = _flat_cat([w_sh[n] for n in ss_names], n_ss, F32).reshape(-1, SMALL_LANES)
    wbig_all, wsm_all = comm_gather(wbig, wsm)
    wbig_all = lax.dynamic_update_index_in_dim(wbig_all, wbig, chip, 0).reshape(N_CHIPS, -1)
    wsm_all = lax.dynamic_update_index_in_dim(wsm_all, wsm, chip, 0).reshape(N_CHIPS, -1)
    W = {n: w_sh[n] for n in SMALL_REPL}
    for names, axes, shapes, allv in ((big_names, big_axes, big_shapes, wbig_all), (ss_names, ss_axes, ss_shapes, wsm_all)):
        per_chip = [_split_flat(allv[j], shapes) for j in range(N_CHIPS)]
        for k, n in enumerate(names):
            W[n] = jnp.concatenate([per_chip[j][k] for j in range(N_CHIPS)], axis=axes[n])
    for n in ("a_w2", "a_a2", "a_g2"):
        W[n] = W[n].astype(MMD)

    loss_blk, grad_x, G = local_step(x[0], mem[0], loss_target[0], W)
    loss = lax.psum(loss_blk[0, 0], ("x", "y", "c"))

    big_parts = [jnp.split(G[n], N_CHIPS, axis=big_axes[n]) for n in big_names]
    gbig = jnp.stack([_flat_cat([p[j] for p in big_parts], n_big, F32) for j in range(N_CHIPS)])
    gbig = gbig.reshape(N_CHIPS, 2, mh, BIG_LANES)
    sm_full_names = ss_names + list(SMALL_REPL)
    sm_full_shapes = [_full_shape(w_sh[n].shape, ss_axes[n]) for n in ss_names] + [w_sh[n].shape for n in SMALL_REPL]
    n_smf = _round_up(sum(math.prod(s) for s in sm_full_shapes), 2 * 8 * SMALL_LANES)
    msh = n_smf // (2 * SMALL_LANES)
    gsm = _flat_cat([G[n] for n in sm_full_names], n_smf, F32).reshape(2, msh, SMALL_LANES)
    rb, rs = comm_pair_exchange(gbig, gsm)
    mine_b = lax.dynamic_index_in_dim(gbig, ci, axis=1, keepdims=False)
    mine_s = lax.dynamic_index_in_dim(gsm, ci, axis=0, keepdims=False)
    hb = add_pairs(mine_b.reshape(-1, BIG_LANES), rb.reshape(-1, BIG_LANES), BF16, name="add_pairs_big", tb=128)
    hs = add_pairs(mine_s, rs, F32, name="add_pairs_small", tb=msh)
    hb = hb.reshape(N_CHIPS, mh, BIG_LANES)
    qb, qs = comm_chip_exchange(hb, hs)
    qb = lax.dynamic_update_index_in_dim(qb, lax.dynamic_index_in_dim(hb, chip, 0, keepdims=False), chip, 0)
    qs = lax.dynamic_update_index_in_dim(qs, hs, chip, 0)
    gh = add_chips([qb[j] for j in range(N_CHIPS)], name="add_chips_big", tb=32)
    gsh = add_chips([qs[j] for j in range(N_CHIPS)], name="add_chips_small", tb=msh)
    rh, rsh = comm_pair_share(gh, gsh)
    gfull = jnp.where(ci == 0, jnp.stack([gh, rh]), jnp.stack([rh, gh]))
    gsfull = jnp.where(ci == 0, jnp.stack([gsh, rsh]), jnp.stack([rsh, gsh]))

    g_big = gfull.reshape(-1, BIG_LANES)
    pack_big = lambda d: _flat_cat([d[n] for n in big_names], n_big, F32).reshape(-1, BIG_LANES)
    d_big, m_big, v_big = adamw(g_big, pack_big(w_sh), pack_big(m_sh), pack_big(v_sh), name="adamw_big", tb=64)
    sm_full = dict(zip(sm_full_names, _split_flat(gsfull.reshape(-1), sm_full_shapes)))
    g_loc = {}
    for n in ss_names:
        size = w_sh[n].shape[ss_axes[n]]
        g_loc[n] = lax.dynamic_slice_in_dim(sm_full[n], chip * size, size, axis=ss_axes[n])
    for n in SMALL_REPL:
        g_loc[n] = sm_full[n]
    n_sml = _round_up(sum(math.prod(w_sh[n].shape) for n in sm_full_names), 8 * SMALL_LANES)
    pack_sm = lambda d: _flat_cat([d[n] for n in sm_full_names], n_sml, F32).reshape(-1, SMALL_LANES)
    d_sm, m_sm, v_sm = adamw(pack_sm(g_loc), pack_sm(w_sh), pack_sm(m_sh), pack_sm(v_sh), name="adamw_small",
                             tb=n_sml // SMALL_LANES)
    sm_loc_shapes = [w_sh[n].shape for n in sm_full_names]
    res = {}
    for tag, bigv, smv in (("grad", g_big, None), ("delta", d_big, d_sm), ("new_m", m_big, m_sm), ("new_v", v_big, v_sm)):
        res[tag] = dict(zip(big_names, _split_flat(bigv.reshape(-1), big_shapes)))
        res[tag].update(g_loc if smv is None else dict(zip(sm_full_names, _split_flat(smv.reshape(-1), sm_loc_shapes))))
    return (loss, grad_x[None], *[res[tag][n] for tag in ("grad", "delta", "new_m", "new_v") for n in WEIGHTS])
```

```python
import functools
import math

import numpy as np
import jax
import jax.numpy as jnp
from jax import lax
from jax.experimental import pallas as pl
from jax.experimental.pallas import tpu as pltpu

F32 = jnp.float32
BF16 = jnp.bfloat16
MMD = jnp.bfloat16

D_MODEL = 1024
HEAD_DIM = 64
N_MEM = 256
MEM_WIDTH = 256
RWKV_HEADS = 12
RWKV_WIDTH = 768
SHIFT_WIDTH = 2560
LORA_WIDTH = 256
DIL_WIDTH = 768
DIL_GROUPS = ((128, 1), (512, 4), (2048, 16))
DIL_BLOCK = 128
D_FF = 2816
ROPE_THETA = 10000.0
RMS_EPS = 1e-6
LNX_EPS = 64e-5
NEG_INF = -1e30
ADAM_LR = 0.001
ADAM_B1 = 0.9
ADAM_B2 = 0.999
ADAM_EPS = 1e-08
ADAM_WD = 0.01
ADAM_STEP = 10
N_CHIPS = 4
MESH = pl.DeviceIdType.MESH
VMEM_LIMIT_MB = 56
SCAN_CHUNK = 16
SCAN_UNROLL = 4
SCAN_UNROLL_BWD = 2

BIG = (("a_w_in", 2), ("a_w_out", 1), ("kv_w", 1), ("b_w_in", 1), ("b_w_out", 2), ("mem_w_kv", 1),
       ("ffn_w_up", 2), ("ffn_w_down", 1))
SMALL_SHARDED = (("a_mu", 1), ("a_w0", 1), ("a_w2", 2), ("a_a0", 1), ("a_a2", 2), ("a_g2", 2), ("a_k_k", 1),
                 ("a_k_a", 1), ("a_lnx_w", 1), ("a_lnx_b", 1), ("ffn_conv_w", 2))
SMALL_REPL = ("attn_norm", "a_r_k", "kv_norm", "kv_k_norm", "b_q_norm", "mem_norm", "mem_q_norm", "mem_k_norm",
              "ffn_norm", "ffn_conv_b")
WEIGHTS = ("attn_norm", "a_w_in", "a_mu", "a_w0", "a_w2", "a_a0", "a_a2", "a_g2", "a_k_k", "a_k_a", "a_r_k",
           "a_lnx_w", "a_lnx_b", "a_w_out", "kv_norm", "kv_w", "kv_k_norm", "b_w_in", "b_q_norm", "b_w_out",
           "mem_norm", "mem_w_kv", "mem_q_norm", "mem_k_norm", "ffn_norm", "ffn_w_up", "ffn_conv_w", "ffn_conv_b",
           "ffn_w_down")


def _cp(sem=None, **kw):
    return pltpu.CompilerParams(dimension_semantics=sem, vmem_limit_bytes=VMEM_LIMIT_MB << 20, **kw)


def _tile(n, cands=(512, 256, 128)):
    for c in cands:
        if n % c == 0:
            return c
    return n


def _mm(a, b, *, name, ta=False, tb=False, add=None, out_dtype=F32):
    K, M = a.shape if ta else a.shape[::-1]
    N = b.shape[0] if tb else b.shape[1]
    assert K == (b.shape[1] if tb else b.shape[0])
    tm, tn = _tile(M, (512, 256, 128) if ta else (1024, 512, 256, 128)), _tile(N)
    a_spec = pl.BlockSpec((K, tm), lambda i, j: (0, i)) if ta else pl.BlockSpec((tm, K), lambda i, j: (i, 0))
    b_spec = pl.BlockSpec((tn, K), lambda i, j: (j, 0)) if tb else pl.BlockSpec((K, tn), lambda i, j: (0, j))
    o_spec = pl.BlockSpec((tm, tn), lambda i, j: (i, j))
    dn = (((0,) if ta else (1,), (1,) if tb else (0,)), ((), ()))
    has_add = add is not None

    def body(*refs):
        a_ref, b_ref = refs[0], refs[1]
        o_ref = refs[-1]
        acc = lax.dot_general(a_ref[...].astype(MMD), b_ref[...].astype(MMD), dn, preferred_element_type=F32)
        if has_add:
            acc = acc + refs[2][...]
        o_ref[...] = acc.astype(o_ref.dtype)

    ins = [a, b] + ([add] if has_add else [])
    specs = [a_spec, b_spec] + ([o_spec] if has_add else [])
    return pl.pallas_call(
        body, name=name, grid=(M // tm, N // tn), in_specs=specs, out_specs=o_spec,
        out_shape=jax.ShapeDtypeStruct((M, N), out_dtype), compiler_params=_cp(("parallel", "parallel")),
    )(*ins)


def _rowmap(fn, *, name, T, tb, ins, outs, accs=()):
    nblk = T // tb
    assert T % tb == 0 and tb % 8 == 0
    in_specs, args = [], []
    for spec in ins:
        kind, arr = spec[0], spec[1]
        w, cb = (spec[2], spec[3]) if len(spec) > 2 else (arr.shape[-1], 0)
        if kind == "row":
            in_specs.append(pl.BlockSpec((tb, w), lambda i, cb=cb: (i, cb)))
        elif kind == "prev":
            in_specs.append(pl.BlockSpec((8, w), lambda i, cb=cb: (jnp.maximum(i * (tb // 8) - 1, 0), cb)))
        elif kind == "next":
            in_specs.append(pl.BlockSpec((8, w), lambda i, cb=cb: (jnp.minimum((i + 1) * (tb // 8), T // 8 - 1), cb)))
        elif kind == "const":
            in_specs.append(pl.BlockSpec(arr.shape, lambda i, nd=arr.ndim: (0,) * nd))
        else:
            raise ValueError(kind)
        args.append(arr)
    out_shape, out_specs = [], []
    for kind, w, dt in outs:
        out_shape.append(jax.ShapeDtypeStruct((T, w), dt))
        out_specs.append(pl.BlockSpec((tb, w), lambda i: (i, 0)))
    for shp, dt in accs:
        out_shape.append(jax.ShapeDtypeStruct(shp, dt))
        out_specs.append(pl.BlockSpec(shp, lambda i, nd=len(shp): (0,) * nd))
    n_in, n_out = len(ins), len(outs)

    def body(*refs):
        i = pl.program_id(0)
        vals = [r[...] for r in refs[:n_in]]
        res = fn(i, nblk, *vals)
        if not isinstance(res, (tuple, list)):
            res = (res,)
        assert len(res) == n_out + len(accs), (name, len(res))
        for r, v in zip(refs[n_in:n_in + n_out], res[:n_out]):
            r[...] = v.astype(r.dtype)
        acc_refs = refs[n_in + n_out:]
        if acc_refs:
            @pl.when(i == 0)
            def _():
                for r in acc_refs:
                    r[...] = jnp.zeros(r.shape, r.dtype)

            for r, v in zip(acc_refs, res[n_out:]):
                r[...] += v

    res = pl.pallas_call(
        body, name=name, grid=(nblk,), in_specs=in_specs, out_specs=out_specs, out_shape=out_shape,
        compiler_params=_cp(("arbitrary",)),
    )(*args)
    return res


def _row_pick(halo, r):
    rid = lax.broadcasted_iota(jnp.int32, halo.shape, 0)
    return jnp.sum(jnp.where(rid == r, halo, 0.0), axis=0, keepdims=True)


def _shift_down(x, row_before, is_first):
    rid = lax.broadcasted_iota(jnp.int32, x.shape, 0)
    first = jnp.where(is_first, 0.0, 1.0) * row_before
    return jnp.where(rid == 0, first, pltpu.roll(x, 1, axis=0))


def _shift_up(x, row_after, is_last):
    n = x.shape[0]
    rid = lax.broadcasted_iota(jnp.int32, x.shape, 0)
    last = jnp.where(is_last, 0.0, 1.0) * row_after
    return jnp.where(rid == n - 1, last, pltpu.roll(x, n - 1, axis=0))


def _dot(a, b, dn=(((1,), (0,)), ((), ()))):
    return lax.dot_general(a.astype(MMD), b.astype(MMD), dn, preferred_element_type=F32)


def _dot_nt(a, b):
    return _dot(a, b, (((1,), (1,)), ((), ())))


def _dot_tn(a, b):
    return _dot(a, b, (((0,), (0,)), ((), ())))


def _dot_exact01(x, g01):
    hi = x.astype(BF16)
    lo = (x - hi.astype(F32)).astype(BF16)
    dn = (((1,), (0,)), ((), ()))
    return (lax.dot_general(hi, g01, dn, preferred_element_type=F32)
            + lax.dot_general(lo, g01, dn, preferred_element_type=F32))


def _fold_heads(v, fold):
    return _row_pick(_dot_exact01(jnp.broadcast_to(v, (8, v.shape[1])), fold), 0)


def _group_ones(width):
    idx = np.arange(width) // HEAD_DIM
    return jnp.asarray((idx[:, None] == idx[None, :]).astype(np.float32), BF16)


def _fold_ones(width):
    idx = np.arange(width) % HEAD_DIM
    return jnp.asarray((idx[:, None] == np.arange(HEAD_DIM)[None, :]).astype(np.float32), BF16)


def _head_masks(width):
    idx = np.arange(width) // HEAD_DIM
    return jnp.asarray((idx[None, :] == np.arange(width // HEAD_DIM)[:, None]).astype(np.float32)[:, None, :], F32)


def _rms_stats(x):
    r = lax.rsqrt(jnp.mean(x * x, axis=-1, keepdims=True) + RMS_EPS)
    return r, x * r


def rms_fwd(x, gains, *, name):
    T, D = x.shape

    def fn(i, nblk, xb, *gs):
        _, xh = _rms_stats(xb)
        return tuple(xh * g for g in gs)

    return _rowmap(fn, name=name, T=T, tb=512, ins=[("row", x)] + [("const", g) for g in gains],
                   outs=[("row", D, MMD)] * len(gains))


def rms_bwd(x, gains, dhs, dres, *, name):
    T, D = x.shape
    n = len(gains)

    def fn(i, nblk, xb, dr, *rest):
        gs, ds = rest[:n], rest[n:]
        r, xh = _rms_stats(xb)
        dx = dr
        dgs = []
        for g, dh in zip(gs, ds):
            dgs.append(jnp.sum(dh * xh, axis=0, keepdims=True))
            dxh = dh * g
            dx = dx + r * (dxh - xh * jnp.mean(dxh * xh, axis=-1, keepdims=True))
        return (dx, *dgs)

    return _rowmap(fn, name=name, T=T, tb=512,
                   ins=[("row", x), ("row", dres)] + [("const", g) for g in gains] + [("row", d) for d in dhs],
                   outs=[("row", D, F32)], accs=[((1, D), F32)] * n)


def _segsum(x):
    first = lax.broadcasted_iota(jnp.int32, (x.shape[0], 128), 1) < HEAD_DIM
    outs = []
    for p in range(x.shape[1] // 128):
        xs = x[:, p * 128:(p + 1) * 128]
        lo = jnp.sum(jnp.where(first, xs, 0.0), axis=-1, keepdims=True)
        hi = jnp.sum(jnp.where(first, 0.0, xs), axis=-1, keepdims=True)
        outs.append(jnp.where(first, lo, hi))
    return jnp.concatenate(outs, axis=1)


def _pre1_common(i, ps, halo, mu, w0, a0, w2p, a2p, g2p, k_k, k_a):
    prev = _shift_down(ps, _row_pick(halo, 7), i == 0)
    xs = ps + (prev - ps) * mu
    lo = xs[:, 3 * RWKV_WIDTH:]
    tl, sl = jnp.tanh(lo), jax.nn.sigmoid(lo)
    dec = w0 + _dot(tl, w2p)
    ain = a0 + _dot(lo, a2p)
    g = _dot(sl, g2p)
    wl = -jax.nn.softplus(-dec) - 0.5
    w = jnp.exp(-jnp.exp(wl))
    a = jax.nn.sigmoid(ain)
    k = xs[:, RWKV_WIDTH:2 * RWKV_WIDTH]
    z = k * k_k
    nrm = jnp.sqrt(_segsum(z * z))
    kk = z / jnp.maximum(nrm, 1e-12)
    return prev, xs, lo, tl, sl, dec, wl, w, a, g, k, nrm, kk


def rwkv_pre_fwd(p, mu, w0, a0, w2p, a2p, g2p, k_k, k_a):
    T = p.shape[0]

    def fn(i, nblk, ps, halo, mu, w0, a0, w2p, a2p, g2p, k_k, k_a):
        _, xs, _, _, _, _, _, w, a, g, k, _, kk = _pre1_common(i, ps, halo, mu, w0, a0, w2p, a2p, g2p, k_k, k_a)
        W = RWKV_WIDTH
        return xs[:, :W], w, k * (1.0 + (a - 1.0) * k_a), xs[:, 2 * W:3 * W], kk, kk * a, g

    return _rowmap(fn, name="rwkv_pre_fwd", T=T, tb=256,
                   ins=[("row", p, SHIFT_WIDTH, 0), ("prev", p, SHIFT_WIDTH, 0)]
                   + [("const", c) for c in (mu, w0, a0, w2p, a2p, g2p, k_k, k_a)],
                   outs=[("row", RWKV_WIDTH, F32)] * 7)


def rwkv_pre_bwd(p, mu, w0, a0, w2p, a2p, g2p, k_k, k_a, drs, dw, dk2s, dvs, dkk, dkka, dg):
    T = p.shape[0]

    def fn(i, nblk, ps, halo, mu, w0, a0, w2p, a2p, g2p, k_k, k_a, dr0, dr1, dw, dk20, dk21, dv0, dv1, dkk, dkka, dg):
        prev, xs, lo, tl, sl, dec, wl, w, a, g, k, nrm, kk = _pre1_common(i, ps, halo, mu, w0, a0, w2p, a2p, g2p, k_k, k_a)
        dk2 = dk20 + dk21
        dkk_t = dkk + dkka * a
        proj = jnp.where(nrm > 1e-12, kk * _segsum(dkk_t * kk), 0.0)
        dz = (dkk_t - proj) / jnp.maximum(nrm, 1e-12)
        dk = dz * k_k + dk2 * (1.0 + (a - 1.0) * k_a)
        da = dkka * kk + dk2 * k * k_a
        ddec = dw * (-w * jnp.exp(wl)) * jax.nn.sigmoid(-dec)
        dain = da * a * (1.0 - a)
        dlo = (_dot_nt(ddec, w2p) * (1.0 - tl * tl) + _dot_nt(dain, a2p) + _dot_nt(dg, g2p) * sl * (1.0 - sl))
        dxs = jnp.concatenate([dr0 + dr1, dk, dv0 + dv1, dlo], axis=1)
        s = lambda z: jnp.sum(z, axis=0, keepdims=True)
        return (dxs, s(dxs * (prev - ps)), s(ddec), s(dain), _dot_tn(tl, ddec), _dot_tn(lo, dain), _dot_tn(sl, dg),
                s(dz * k), s(dk2 * k * (a - 1.0)))

    return _rowmap(fn, name="rwkv_pre_bwd", T=T, tb=128,
                   ins=[("row", p, SHIFT_WIDTH, 0), ("prev", p, SHIFT_WIDTH, 0)]
                   + [("const", c) for c in (mu, w0, a0, w2p, a2p, g2p, k_k, k_a)]
                   + [("row", c) for c in (*drs, dw, *dk2s, *dvs, dkk, dkka, dg)],
                   outs=[("row", SHIFT_WIDTH, F32)],
                   accs=[((1, SHIFT_WIDTH), F32), ((1, RWKV_WIDTH), F32), ((1, RWKV_WIDTH), F32)]
                   + [((LORA_WIDTH, RWKV_WIDTH), F32)] * 3 + [((1, RWKV_WIDTH), F32)] * 2)


def shift_bwd(dxs, mu, dq_mem):
    T = dxs.shape[0]

    def fn(i, nblk, d, halo, mu, dq):
        nxt = _shift_up(d, _row_pick(halo, 0), i == nblk - 1)
        return jnp.concatenate([d * (1.0 - mu) + nxt * mu, dq], axis=1)

    return _rowmap(fn, name="shift_bwd", T=T, tb=256,
                   ins=[("row", dxs), ("next", dxs), ("const", mu), ("row", dq_mem)],
                   outs=[("row", SHIFT_WIDTH + MEM_WIDTH, MMD)])[0]


N_PAIRS = RWKV_HEADS // 2


def _pair_consts():
    row = lax.broadcasted_iota(jnp.int32, (HEAD_DIM, 128), 0)
    lane = lax.broadcasted_iota(jnp.int32, (HEAD_DIM, 128), 1)
    eye2 = jnp.logical_or(lane == row, lane == row + HEAD_DIM).astype(F32)
    li = lax.broadcasted_iota(jnp.int32, (128, 128), 0) < HEAD_DIM
    lj = lax.broadcasted_iota(jnp.int32, (128, 128), 1) < HEAD_DIM
    return eye2, (li == lj).astype(BF16)


def _pair_sum(p, ones2):
    n, m, l = p.shape
    s = lax.dot_general(p.reshape(n * m, l).astype(BF16), ones2, (((1,), (0,)), ((), ())), preferred_element_type=F32)
    return s.reshape(n, m, l)


def _pair_rows(row):
    return jnp.stack([row[:, p * 128:(p + 1) * 128] for p in range(N_PAIRS)], axis=0)


def _pair_flat(rows):
    return jnp.concatenate([rows[p] for p in range(N_PAIRS)], axis=1)


def _split_bf16(v):
    hi = v.astype(BF16).astype(F32)
    return hi, v - hi


def scan_fwd(r, w, k2, v, kk, kka):
    T, W = r.shape
    tc = SCAN_CHUNK
    seq = pl.BlockSpec((tc, W), lambda i: (i, 0))
    one_state = pl.BlockSpec((N_PAIRS, HEAD_DIM, 128), lambda i: (0, 0, 0))

    def body(r_ref, w_ref, k2_ref, v_ref, kk_ref, kka_ref, y_ref, st_ref, fin_ref, s_scr, vhi_scr, vlo_scr):
        @pl.when(pl.program_id(0) == 0)
        def _():
            s_scr[...] = jnp.zeros(s_scr.shape, F32)

        vhi_scr[...], vlo_scr[...] = _split_bf16(v_ref[...])
        eye2, ones2 = _pair_consts()

        def step(t, carry):
            r_t, w_t, k2_t, kk_t, kka_t, vhi_t, vlo_t = (
                _pair_rows(ref[pl.ds(t, 1), :]) for ref in (r_ref, w_ref, k2_ref, kk_ref, kka_ref, vhi_scr, vlo_scr))
            S = s_scr[...]
            sa = -_pair_sum(S * kk_t, ones2)
            vb = _pair_sum(eye2 * vhi_t, ones2) + _pair_sum(eye2 * vlo_t, ones2)
            S2 = S * w_t + sa * kka_t + vb * k2_t
            y_ref[pl.ds(t, 1), :] = _pair_flat(jnp.sum(eye2 * _pair_sum(S2 * r_t, ones2), axis=1, keepdims=True))
            s_scr[...] = S2
            st_ref[t] = S
            return carry

        lax.fori_loop(0, tc, step, 0, unroll=SCAN_UNROLL)
        fin_ref[...] = s_scr[...]

    return pl.pallas_call(
        body, name="rwkv_scan_fwd", grid=(T // tc,), in_specs=[seq] * 6,
        out_specs=[seq, pl.BlockSpec((tc, N_PAIRS, HEAD_DIM, 128), lambda i: (i, 0, 0, 0)), one_state],
        out_shape=[jax.ShapeDtypeStruct((T, W), F32), jax.ShapeDtypeStruct((T, N_PAIRS, HEAD_DIM, 128), F32),
                   jax.ShapeDtypeStruct((N_PAIRS, HEAD_DIM, 128), F32)],
        scratch_shapes=[pltpu.VMEM((N_PAIRS, HEAD_DIM, 128), F32), pltpu.VMEM((tc, W), F32), pltpu.VMEM((tc, W), F32)],
        compiler_params=_cp(("arbitrary",)),
    )(r, w, k2, v, kk, kka)


def scan_bwd(r, w, k2, v, kk, kka, states, final_state, dy):
    T, W = r.shape
    tc = SCAN_CHUNK
    nchunk = T // tc
    seq = pl.BlockSpec((tc, W), lambda i: (nchunk - 1 - i, 0))
    st_spec = pl.BlockSpec((tc, N_PAIRS, HEAD_DIM, 128), lambda i: (nchunk - 1 - i, 0, 0, 0))
    one_state = pl.BlockSpec((N_PAIRS, HEAD_DIM, 128), lambda i: (0, 0, 0))

    def body(r_ref, w_ref, k2_ref, v_ref, kk_ref, kka_ref, st_ref, fin_ref, dy_ref,
             dr_ref, dw_ref, dk2_ref, dv_ref, dkk_ref, dkka_ref, ds_scr, sc_scr, vhi_scr, vlo_scr):
        @pl.when(pl.program_id(0) == 0)
        def _():
            ds_scr[...] = jnp.zeros(ds_scr.shape, F32)
            sc_scr[...] = fin_ref[...]

        vhi_scr[...], vlo_scr[...] = _split_bf16(v_ref[...])
        eye2, ones2 = _pair_consts()
        colsum = lambda z: jnp.sum(z, axis=1, keepdims=True)

        def step(j, carry):
            t = tc - 1 - j
            r_t, w_t, k2_t, kk_t, kka_t, vhi_t, vlo_t, dy_t = (
                _pair_rows(ref[pl.ds(t, 1), :])
                for ref in (r_ref, w_ref, k2_ref, kk_ref, kka_ref, vhi_scr, vlo_scr, dy_ref))
            s_prev, s_cur = st_ref[t], sc_scr[...]
            dyb = _pair_sum(eye2 * dy_t, ones2)
            vb = _pair_sum(eye2 * vhi_t, ones2) + _pair_sum(eye2 * vlo_t, ones2)
            sa = -_pair_sum(s_prev * kk_t, ones2)
            dS = ds_scr[...] + dyb * r_t
            dsa = _pair_sum(dS * kka_t, ones2)
            ds_scr[...] = dS * w_t - dsa * kk_t
            sc_scr[...] = s_prev
            for ref, val in zip((dr_ref, dw_ref, dk2_ref, dv_ref, dkk_ref, dkka_ref),
                                (s_cur * dyb, dS * s_prev, dS * vb, eye2 * _pair_sum(dS * k2_t, ones2),
                                 -(s_prev * dsa), dS * sa)):
                ref[pl.ds(t, 1), :] = _pair_flat(colsum(val))
            return carry

        lax.fori_loop(0, tc, step, 0, unroll=SCAN_UNROLL_BWD)

    out = jax.ShapeDtypeStruct((T, W), F32)
    return pl.pallas_call(
        body, name="rwkv_scan_bwd", grid=(nchunk,), in_specs=[seq] * 6 + [st_spec, one_state, seq],
        out_specs=[seq] * 6, out_shape=[out] * 6,
        scratch_shapes=[pltpu.VMEM((N_PAIRS, HEAD_DIM, 128), F32)] * 2 + [pltpu.VMEM((tc, W), F32)] * 2,
        compiler_params=_cp(("arbitrary",)),
    )(r, w, k2, v, kk, kka, states, final_state, dy)


def _mix_common(y, r, k2, v, lnx_w, lnx_b, r_k):
    yc = y - _segsum(y) * (1.0 / HEAD_DIM)
    rstd = lax.rsqrt(_segsum(yc * yc) * (1.0 / HEAD_DIM) + LNX_EPS)
    yhat = yc * rstd
    s = _segsum(r * k2 * r_k)
    return rstd, yhat, s, yhat * lnx_w + lnx_b + s * v


def mix_gate_fwd(y, r, k2, v, g, y_mem, lnx_w, lnx_b, r_k):
    T = y.shape[0]

    def fn(i, nblk, y, r, k2, v, g, ym, lw, lb, rk):
        mix = _mix_common(y, r, k2, v, lw, lb, rk)[3]
        return jnp.concatenate([mix * g, ym], axis=1)

    return _rowmap(fn, name="mix_gate_fwd", T=T, tb=256,
                   ins=[("row", z) for z in (y, r, k2, v, g, y_mem)] + [("const", c) for c in (lnx_w, lnx_b, r_k)],
                   outs=[("row", RWKV_WIDTH + MEM_WIDTH, MMD)])[0]


def mix_gate_bwd(y, r, k2, v, g, dycat, lnx_w, lnx_b, r_k):
    T = y.shape[0]

    def fn(i, nblk, y, r, k2, v, g, dyc, lw, lb, rk):
        rstd, yhat, s, mix = _mix_common(y, r, k2, v, lw, lb, rk)
        dmix = dyc * g
        dyh = dmix * lw
        inv = 1.0 / HEAD_DIM
        dy = rstd * (dyh - _segsum(dyh) * inv - yhat * (_segsum(dyh * yhat) * inv))
        ds = _segsum(dmix * v)
        cs = lambda z: jnp.sum(z, axis=0, keepdims=True)
        return (dy, ds * k2 * rk, ds * r * rk, dmix * s, dyc * mix, cs(dmix * yhat), cs(dmix), cs(ds * r * k2))

    return _rowmap(fn, name="mix_gate_bwd", T=T, tb=256,
                   ins=[("row", z) for z in (y, r, k2, v, g)] + [("row", dycat, RWKV_WIDTH, 0)]
                   + [("const", c) for c in (lnx_w, lnx_b, r_k)],
                   outs=[("row", RWKV_WIDTH, F32)] * 5, accs=[((1, RWKV_WIDTH), F32)] * 3)


def _head_rms(x, gones):
    ms = _segsum(x * x) * (1.0 / HEAD_DIM)
    r = lax.rsqrt(ms + RMS_EPS)
    return r, x * r


def _head_rms_bwd(dxn_g, r, xh, gones):
    return r * (dxn_g - xh * (_segsum(dxn_g * xh) * (1.0 / HEAD_DIM)))


def mem_kv_fwd(mem, norm_g, w_kv, k_norm_t, *, name):
    gones = _group_ones(MEM_WIDTH)

    def body(mem_ref, g_ref, w_ref, kn_ref, go_ref, k_out, v_out):
        _, xh = _rms_stats(mem_ref[...])
        kv = _dot(xh * g_ref[...], w_ref[...])
        _, kh = _head_rms(kv[:, :MEM_WIDTH], go_ref[...])
        k_out[...] = kh * kn_ref[...]
        v_out[...] = kv[:, MEM_WIDTH:]

    return pl.pallas_call(
        body, name=name, out_shape=[jax.ShapeDtypeStruct((N_MEM, MEM_WIDTH), F32)] * 2, compiler_params=_cp(),
    )(mem, norm_g, w_kv, k_norm_t, gones)


def mem_kv_bwd(mem, norm_g, w_kv, k_norm_t, dkn, dv, *, name):
    gones, fold = _group_ones(MEM_WIDTH), _fold_ones(MEM_WIDTH)

    def body(mem_ref, g_ref, w_ref, kn_ref, go_ref, fo_ref, dkn_ref, dv_ref, dw_out, dg_out, dkg_out):
        _, xh = _rms_stats(mem_ref[...])
        hm = xh * g_ref[...]
        kv = _dot(hm, w_ref[...])
        r, kh = _head_rms(kv[:, :MEM_WIDTH], go_ref[...])
        dkn = dkn_ref[...]
        dkg_out[...] = _fold_heads(jnp.sum(dkn * kh, axis=0, keepdims=True), fo_ref[...])
        dkraw = _head_rms_bwd(dkn * kn_ref[...], r, kh, go_ref[...])
        dkv = jnp.concatenate([dkraw, dv_ref[...]], axis=1)
        dw_out[...] = _dot_tn(hm, dkv)
        dg_out[...] = jnp.sum(_dot_nt(dkv, w_ref[...]) * xh, axis=0, keepdims=True)

    return pl.pallas_call(
        body, name=name,
        out_shape=[jax.ShapeDtypeStruct((D_MODEL, 2 * MEM_WIDTH), F32), jax.ShapeDtypeStruct((1, D_MODEL), F32),
                   jax.ShapeDtypeStruct((1, HEAD_DIM), F32)],
        compiler_params=_cp(),
    )(mem, norm_g, w_kv, k_norm_t, gones, fold, dkn, dv)


def _mem_scores(qn, kn, masks, h):
    s = _dot_nt(qn * masks[h], kn) * (1.0 / math.sqrt(HEAD_DIM))
    s = s - jnp.max(s, axis=-1, keepdims=True)
    e = jnp.exp(s)
    return e / jnp.sum(e, axis=-1, keepdims=True)


def mem_attn_fwd(p, colblock, kn, v, q_norm_t, *, name):
    T = p.shape[0]
    gones, masks = _group_ones(MEM_WIDTH), _head_masks(MEM_WIDTH)

    def fn(i, nblk, q, kn, v, qg, go, masks):
        _, qh = _head_rms(q, go)
        qn = qh * qg
        out = jnp.zeros(q.shape, F32)
        for h in range(MEM_WIDTH // HEAD_DIM):
            out = out + _dot(_mem_scores(qn, kn, masks, h), v * masks[h])
        return out

    return _rowmap(fn, name=name, T=T, tb=512,
                   ins=[("row", p, MEM_WIDTH, colblock)] + [("const", c) for c in (kn, v, q_norm_t, gones, masks)],
                   outs=[("row", MEM_WIDTH, F32)])[0]


def mem_attn_bwd(p, colblock, kn, v, q_norm_t, dycat, dcolblock, *, name):
    T = p.shape[0]
    gones, masks, fold = _group_ones(MEM_WIDTH), _head_masks(MEM_WIDTH), _fold_ones(MEM_WIDTH)
    scale = 1.0 / math.sqrt(HEAD_DIM)

    def fn(i, nblk, q, dy, kn, v, qg, go, masks, fo):
        r, qh = _head_rms(q, go)
        qn = qh * qg
        dqn = jnp.zeros(q.shape, F32)
        dkn = jnp.zeros(kn.shape, F32)
        dv = jnp.zeros(v.shape, F32)
        for h in range(MEM_WIDTH // HEAD_DIM):
            pr = _mem_scores(qn, kn, masks, h)
            dyh = dy * masks[h]
            dpr = _dot_nt(dyh, v)
            ds = pr * (dpr - jnp.sum(dpr * pr, axis=-1, keepdims=True)) * scale
            dqn = dqn + _dot(ds, kn * masks[h])
            dkn = dkn + _dot_tn(ds, qn * masks[h])
            dv = dv + _dot_tn(pr, dyh)
        dqg = _fold_heads(jnp.sum(dqn * qh, axis=0, keepdims=True), fo)
        return _head_rms_bwd(dqn * qg, r, qh, go), dkn, dv, dqg

    return _rowmap(fn, name=name, T=T, tb=512,
                   ins=[("row", p, MEM_WIDTH, colblock), ("row", dycat, MEM_WIDTH, dcolblock)]
                   + [("const", c) for c in (kn, v, q_norm_t, gones, masks, fold)],
                   outs=[("row", MEM_WIDTH, F32)],
                   accs=[((N_MEM, MEM_WIDTH), F32), ((N_MEM, MEM_WIDTH), F32), ((1, HEAD_DIM), F32)])


def _ffn_conv(i, u, halo, cw, cb):
    up1 = _shift_down(u, _row_pick(halo, 7), i == 0)
    up2 = _shift_down(up1, _row_pick(halo, 6), i == 0)
    c = cb + cw[0] * up2 + cw[1] * up1 + cw[2] * u
    return up1, up2, c[:, :D_FF], c[:, D_FF:]


def ffn_act_fwd(u, cw, cb, *, name):
    T = u.shape[0]

    def fn(i, nblk, u, halo, c0, c1, c2, cb):
        _, _, gate, val = _ffn_conv(i, u, halo, (c0, c1, c2), cb)
        return jax.nn.silu(gate) * val

    return _rowmap(fn, name=name, T=T, tb=128, ins=[("row", u), ("prev", u)] + [("const", c) for c in (*cw, cb)],
                   outs=[("row", D_FF, MMD)])[0]


def ffn_act_bwd(u, cw, cb, dz, *, name):
    T = u.shape[0]

    def fn(i, nblk, u, halo, c0, c1, c2, cb, dz):
        up1, up2, gate, val = _ffn_conv(i, u, halo, (c0, c1, c2), cb)
        sg = jax.nn.sigmoid(gate)
        dgate = dz * val * sg * (1.0 + gate * (1.0 - sg))
        dval = dz * gate * sg
        dc = jnp.concatenate([dgate, dval], axis=1)
        s = lambda z: jnp.sum(z, axis=0, keepdims=True)
        return dc, s(dc * up2), s(dc * up1), s(dc * u), s(dc)

    return _rowmap(fn, name=name, T=T, tb=128,
                   ins=[("row", u), ("prev", u)] + [("const", c) for c in (*cw, cb)] + [("row", dz)],
                   outs=[("row", 2 * D_FF, F32)], accs=[((1, 2 * D_FF), F32)] * 4)


def conv_bwd(dc, cw, *, name):
    T = dc.shape[0]

    def fn(i, nblk, d, halo, c0, c1, c2):
        last = i == nblk - 1
        n1 = _shift_up(d, _row_pick(halo, 0), last)
        n2 = _shift_up(n1, _row_pick(halo, 1), last)
        return c2 * d + c1 * n1 + c0 * n2

    return _rowmap(fn, name=name, T=T, tb=128, ins=[("row", dc), ("next", dc)] + [("const", c) for c in cw],
                   outs=[("row", 2 * D_FF, MMD)])[0]


def _rope_swap(z):
    lane = lax.broadcasted_iota(jnp.int32, z.shape, 1) % HEAD_DIM
    w = z.shape[1]
    return jnp.where(lane < HEAD_DIM // 2, pltpu.roll(z, w - HEAD_DIM // 2, axis=1), pltpu.roll(z, HEAD_DIM // 2, axis=1))


def rope_tables(T):
    inv = (np.float32(ROPE_THETA) ** (-np.arange(0, HEAD_DIM, 2, dtype=np.float32) / np.float32(HEAD_DIM))).astype(np.float32)
    ang = (np.arange(T, dtype=np.float32)[:, None] * inv[None, :]).astype(np.float64)
    cos, sin = np.cos(ang).astype(np.float32), np.sin(ang).astype(np.float32)
    return (jnp.asarray(np.concatenate([cos, cos, cos, cos], axis=1)),
            jnp.asarray(np.concatenate([-sin, sin, -sin, sin], axis=1)))


def _rope_wide(t):
    return jnp.tile(t, (1, DIL_WIDTH // t.shape[1]))


def qk_fwd(kvp, pb, kg_t, qg_t, cos, sin):
    T = kvp.shape[0]
    gones = _group_ones(DIL_WIDTH)

    def fn(i, nblk, kraw, vraw, qraw, kg, qg, c, s, go):
        c, s = _rope_wide(c), _rope_wide(s)
        outs = []
        for raw, g in ((qraw, qg), (kraw, kg)):
            _, xh = _head_rms(raw, go)
            z = xh * g
            outs.append(z * c + _rope_swap(z) * s)
        return outs[0], outs[1], vraw

    return _rowmap(fn, name="qk_fwd", T=T, tb=256,
                   ins=[("row", kvp, DIL_WIDTH, 0), ("row", kvp, DIL_WIDTH, 1), ("row", pb, DIL_WIDTH, 0)]
                   + [("const", kg_t), ("const", qg_t), ("row", cos), ("row", sin), ("const", gones)],
                   outs=[("row", DIL_WIDTH, MMD)] * 3)


def qk_bwd(kvp, pb, kg_t, qg_t, cos, sin, dq, dk, dv, dq_mem):
    T = kvp.shape[0]
    gones, fold = _group_ones(DIL_WIDTH), _fold_ones(DIL_WIDTH)

    def fn(i, nblk, kraw, qraw, kg, qg, c, s, go, fo, dq, dk, dv, dqm):
        c, s = _rope_wide(c), _rope_wide(s)
        res, dgs = [], []
        for raw, g, d in ((qraw, qg, dq), (kraw, kg, dk)):
            r, xh = _head_rms(raw, go)
            dz = d * c + _rope_swap(d * s)
            dgs.append(_fold_heads(jnp.sum(dz * xh, axis=0, keepdims=True), fo))
            res.append(_head_rms_bwd(dz * g, r, xh, go))
        return (jnp.concatenate([res[0], dqm], axis=1), jnp.concatenate([res[1], dv], axis=1), dgs[0], dgs[1])

    return _rowmap(fn, name="qk_bwd", T=T, tb=256,
                   ins=[("row", kvp, DIL_WIDTH, 0), ("row", pb, DIL_WIDTH, 0), ("const", kg_t), ("const", qg_t),
                        ("row", cos), ("row", sin), ("const", gones), ("const", fold),
                        ("row", dq), ("row", dk), ("row", dv), ("row", dq_mem)],
                   outs=[("row", DIL_WIDTH + MEM_WIDTH, MMD), ("row", 2 * DIL_WIDTH, MMD)],
                   accs=[((1, HEAD_DIM), F32)] * 2)


def _band(kind):
    i = lax.broadcasted_iota(jnp.int32, (DIL_BLOCK, DIL_BLOCK), 0)
    j = lax.broadcasted_iota(jnp.int32, (DIL_BLOCK, DIL_BLOCK), 1)
    return (j <= i) if kind == "cur" else (j >= i)


def dil_attn_fwd(q, k, v, seq_blocks, *, name):
    T, W = q.shape
    nb = T // DIL_BLOCK
    masks = _head_masks(W)
    cur = pl.BlockSpec((DIL_BLOCK, W), lambda n: (n, 0))
    prv = pl.BlockSpec((DIL_BLOCK, W), lambda n: (jnp.maximum(n - 1, 0), 0))
    scale = 1.0 / math.sqrt(HEAD_DIM)

    def body(q_ref, kc_ref, kp_ref, vc_ref, vp_ref, m_ref, o_ref, l_ref):
        n = pl.program_id(0)
        has_prev = (n % seq_blocks) != 0
        q = q_ref[...].astype(F32)
        kc, kp = kc_ref[...].astype(F32), kp_ref[...].astype(F32)
        vc, vp = vc_ref[...].astype(F32), vp_ref[...].astype(F32)
        ok_c = _band("cur")
        ok_p = jnp.logical_and(_band("prev"), has_prev)
        o = jnp.zeros((DIL_BLOCK, W), F32)
        lse = jnp.zeros((DIL_BLOCK, W), F32)
        for h in range(W // HEAD_DIM):
            mh = m_ref[h]
            qh = q * mh
            sc = jnp.where(ok_c, _dot_nt(qh, kc) * scale, NEG_INF)
            sp = jnp.where(ok_p, _dot_nt(qh, kp) * scale, NEG_INF)
            mx = jnp.maximum(jnp.max(sc, axis=-1, keepdims=True), jnp.max(sp, axis=-1, keepdims=True))
            ec, ep = jnp.exp(sc - mx), jnp.exp(sp - mx)
            den = jnp.sum(ec, axis=-1, keepdims=True) + jnp.sum(ep, axis=-1, keepdims=True)
            o = o + (_dot(ec, vc * mh) + _dot(ep, vp * mh)) / den
            lse = lse + (mx + jnp.log(den)) * mh
        o_ref[...] = o
        l_ref[...] = lse

    return pl.pallas_call(
        body, name=name, grid=(nb,), in_specs=[cur, cur, prv, cur, prv, pl.BlockSpec(masks.shape, lambda n: (0, 0, 0))],
        out_specs=[cur, cur], out_shape=[jax.ShapeDtypeStruct((T, W), F32)] * 2,
        compiler_params=_cp(("parallel",)),
    )(q, k, k, v, v, masks)


def dil_attn_bwd(q, k, v, o, lse, do, dlse, seq_blocks, *, name):
    T, W = q.shape
    nb = T // DIL_BLOCK
    masks = _head_masks(W)
    cur = pl.BlockSpec((DIL_BLOCK, W), lambda n: (n, 0))
    prv = pl.BlockSpec((DIL_BLOCK, W), lambda n: (jnp.maximum(n - 1, 0), 0))
    nxt = pl.BlockSpec((DIL_BLOCK, W), lambda n: (jnp.minimum(n + 1, nb - 1), 0))
    scale = 1.0 / math.sqrt(HEAD_DIM)

    def body(qc_ref, qn_ref, kc_ref, kp_ref, vc_ref, vp_ref, oc_ref, on_ref, lc_ref, ln_ref, doc_ref, don_ref,
             dlc_ref, dln_ref, m_ref, dq_ref, dk_ref, dv_ref):
        n = pl.program_id(0)
        has_prev = (n % seq_blocks) != 0
        has_next = jnp.logical_and(((n + 1) % seq_blocks) != 0, n + 1 < nb)
        f = lambda ref: ref[...].astype(F32)
        qc, qn, kc, kp, vc, vp = f(qc_ref), f(qn_ref), f(kc_ref), f(kp_ref), f(vc_ref), f(vp_ref)
        doc, don = doc_ref[...], don_ref[...]
        ok_c = _band("cur")
        ok_p = jnp.logical_and(_band("prev"), has_prev)
        ok_n = jnp.logical_and(_band("prev"), has_next)
        dq = jnp.zeros((DIL_BLOCK, W), F32)
        dk = jnp.zeros((DIL_BLOCK, W), F32)
        dv = jnp.zeros((DIL_BLOCK, W), F32)

        def side(qh, kk, vv, doh, lse_h, corr, ok):
            s = _dot_nt(qh, kk) * scale
            pr = jnp.where(ok, jnp.exp(jnp.where(ok, s, NEG_INF) - lse_h), 0.0)
            ds = pr * (_dot_nt(doh, vv) + corr) * scale
            return pr, ds

        for h in range(W // HEAD_DIM):
            mh = m_ref[h]
            red = lambda z: jnp.sum(z * mh, axis=-1, keepdims=True)
            qh, doh = qc * mh, doc * mh
            lse_h = red(lc_ref[...]) * (1.0 / HEAD_DIM)
            corr = red(dlc_ref[...]) - red(doc * oc_ref[...])
            pr_c, ds_c = side(qh, kc, vc * mh, doh, lse_h, corr, ok_c)
            _, ds_p = side(qh, kp, vp * mh, doh, lse_h, corr, ok_p)
            dq = dq + _dot(ds_c, kc * mh) + _dot(ds_p, kp * mh)
            dk = dk + _dot_tn(ds_c, qh)
            dv = dv + _dot_tn(pr_c, doh)
            qh2, doh2 = qn * mh, don * mh
            lse_2 = red(ln_ref[...]) * (1.0 / HEAD_DIM)
            corr2 = red(dln_ref[...]) - red(don * on_ref[...])
            pr_n, ds_n = side(qh2, kc, vc * mh, doh2, lse_2, corr2, ok_n)
            dk = dk + _dot_tn(ds_n, qh2)
            dv = dv + _dot_tn(pr_n, doh2)
        dq_ref[...] = dq
        dk_ref[...] = dk
        dv_ref[...] = dv

    return pl.pallas_call(
        body, name=name, grid=(nb,),
        in_specs=[cur, nxt, cur, prv, cur, prv, cur, nxt, cur, nxt, cur, nxt, cur, nxt,
                  pl.BlockSpec(masks.shape, lambda n: (0, 0, 0))],
        out_specs=[cur] * 3, out_shape=[jax.ShapeDtypeStruct((T, W), F32)] * 3,
        compiler_params=_cp(("parallel",)),
    )(q, q, k, k, v, v, o, o, lse, lse, do, do, dlse, dlse, masks)


def _mix_weights(ls):
    m = jnp.maximum(jnp.maximum(ls[0], ls[1]), ls[2])
    es = [jnp.exp(l - m) for l in ls]
    den = es[0] + es[1] + es[2]
    return [e / den for e in es]


def mix_fwd(os_, ls, y_mem):
    T = y_mem.shape[0]

    def fn(i, nblk, o0, o1, o2, l0, l1, l2, ym):
        w = _mix_weights((l0, l1, l2))
        return jnp.concatenate([w[0] * o0 + w[1] * o1 + w[2] * o2, ym], axis=1)

    return _rowmap(fn, name="mix_fwd", T=T, tb=512, ins=[("row", z) for z in (*os_, *ls, y_mem)],
                   outs=[("row", 2 * MEM_WIDTH, MMD)])[0]


def mix_bwd(os_, ls, dycat):
    T = dycat.shape[0]

    def fn(i, nblk, o0, o1, o2, l0, l1, l2, dy):
        w = _mix_weights((l0, l1, l2))
        os3 = (o0, o1, o2)
        dws = [dy * o for o in os3]
        tot = w[0] * dws[0] + w[1] * dws[1] + w[2] * dws[2]
        return tuple(wg * dy for wg in w) + tuple(wg * (dw - tot) for wg, dw in zip(w, dws))

    return _rowmap(fn, name="mix_bwd", T=T, tb=512,
                   ins=[("row", z) for z in (*os_, *ls)] + [("row", dycat, MEM_WIDTH, 0)],
                   outs=[("row", MEM_WIDTH, F32)] * 6)


def loss_fwd_bwd(y, target):
    T, D = y.shape

    def fn(i, nblk, y, t):
        e = y - t
        return e * (1.0 / D), jnp.zeros((8, 128), F32) + jnp.sum(e * e) * (0.5 / D)

    return _rowmap(fn, name="loss", T=T, tb=512, ins=[("row", y), ("row", target)], outs=[("row", D, F32)],
                   accs=[((8, 128), F32)])


def _to_residues(z, dil):
    T, W = z.shape
    return z.reshape(T // dil, dil, W).transpose(1, 0, 2).reshape(T, W)


def _from_residues(z, dil):
    T, W = z.shape
    return z.reshape(dil, T // dil, W).transpose(1, 0, 2).reshape(T, W)


def _pad_rows(w, rows):
    return jnp.concatenate([w, jnp.zeros((rows - w.shape[0], w.shape[1]), w.dtype)], axis=0)


def _tile_heads(g, width):
    return jnp.tile(g.reshape(1, HEAD_DIM), (1, width // HEAD_DIM))


def _conv_rows(W, i):
    return [W["ffn_conv_w"][i][j:j + 1] for j in range(3)]


def _ffn_fwd(x, i, W):
    hn = rms_fwd(x, [W["ffn_norm"][i:i + 1]], name=f"ffn_rms{i}")[0]
    u = _mm(hn, W["ffn_w_up"][i], name=f"ffn_up{i}")
    z = ffn_act_fwd(u, _conv_rows(W, i), W["ffn_conv_b"][i:i + 1], name=f"ffn_act{i}")
    out = _mm(z, W["ffn_w_down"][i], add=x, name=f"ffn_down{i}")
    return out, (x, hn, u, z)


def _ffn_bwd(dout, i, W, saved, G):
    x, hn, u, z = saved
    dz = _mm(dout, W["ffn_w_down"][i], tb=True, name=f"ffn_down_dx{i}")
    G["ffn_w_down"][i] = _mm(z, dout, ta=True, name=f"ffn_down_dw{i}")
    dc, dw0, dw1, dw2, db = ffn_act_bwd(u, _conv_rows(W, i), W["ffn_conv_b"][i:i + 1], dz, name=f"ffn_act_bwd{i}")
    G["ffn_conv_w"][i] = jnp.concatenate([dw0, dw1, dw2], axis=0)
    G["ffn_conv_b"][i] = db[0]
    du = conv_bwd(dc, _conv_rows(W, i), name=f"ffn_conv_bwd{i}")
    dhn = _mm(du, W["ffn_w_up"][i], tb=True, name=f"ffn_up_dx{i}")
    G["ffn_w_up"][i] = _mm(hn, du, ta=True, name=f"ffn_up_dw{i}")
    dx, dg = rms_bwd(x, [W["ffn_norm"][i:i + 1]], [dhn], dout, name=f"ffn_rms_bwd{i}")
    G["ffn_norm"][i] = dg[0]
    return dx


def local_step(x, mem, target, W):
    T = x.shape[0]
    G = {"ffn_w_down": [None, None], "ffn_w_up": [None, None], "ffn_conv_w": [None, None],
         "ffn_conv_b": [None, None], "ffn_norm": [None, None], "attn_norm": [None, None], "mem_norm": [None, None],
         "mem_w_kv": [None, None], "mem_q_norm": [None, None], "mem_k_norm": [None, None]}
    mu, w0, a0 = W["a_mu"], W["a_w0"], W["a_a0"]
    w2p, a2p, g2p = (_pad_rows(W["a_w2"][0], LORA_WIDTH),
                     jnp.concatenate([jnp.zeros((64, RWKV_WIDTH), MMD), W["a_a2"][0],
                                      jnp.zeros((128, RWKV_WIDTH), MMD)], axis=0),
                     jnp.concatenate([jnp.zeros((128, RWKV_WIDTH), MMD), W["a_g2"][0]], axis=0))
    k_k, k_a, lnx_w, lnx_b = W["a_k_k"], W["a_k_a"], W["a_lnx_w"], W["a_lnx_b"]
    r_k = W["a_r_k"].reshape(1, RWKV_WIDTH)
    memkv = []
    for i in range(2):
        memkv.append(mem_kv_fwd(mem, W["mem_norm"][i:i + 1], W["mem_w_kv"][i], _tile_heads(W["mem_k_norm"][i], MEM_WIDTH),
                                name=f"mem_kv{i}"))

    h0 = rms_fwd(x, [W["attn_norm"][0:1]], name="attn_rms0")[0]
    p = _mm(h0, W["a_w_in"][0], name="a_in")
    r, w, k2, v, kk, kka, g = rwkv_pre_fwd(p, mu, w0, a0, w2p, a2p, g2p, k_k, k_a)
    y, states, final_state = scan_fwd(r, w, k2, v, kk, kka)
    qg0 = _tile_heads(W["mem_q_norm"][0], MEM_WIDTH)
    y_mem0 = mem_attn_fwd(p, SHIFT_WIDTH // MEM_WIDTH, memkv[0][0], memkv[0][1], qg0, name="mem_attn0")
    ycat0 = mix_gate_fwd(y, r, k2, v, g, y_mem0, lnx_w, lnx_b, r_k)
    x1 = _mm(ycat0, W["a_w_out"][0], add=x, name="a_out")
    x2, ffn0 = _ffn_fwd(x1, 0, W)

    h1, hkv = rms_fwd(x2, [W["attn_norm"][1:2], W["kv_norm"].reshape(1, -1)], name="attn_rms1")
    kvp = _mm(hkv, W["kv_w"], name="kv_in")
    pb = _mm(h1, W["b_w_in"][0], name="b_in")
    cos, sin = rope_tables(T)
    kg_t, qg_t = _tile_heads(W["kv_k_norm"], DIL_WIDTH), _tile_heads(W["b_q_norm"][0], DIL_WIDTH)
    q, ksh, vsh = qk_fwd(kvp, pb, kg_t, qg_t, cos, sin)
    os_, ls, grp = [], [], []
    for gi, (win, dil) in enumerate(DIL_GROUPS):
        sl = slice(gi * MEM_WIDTH, (gi + 1) * MEM_WIDTH)
        qg_, kg_, vg_ = (_to_residues(z[:, sl], dil) for z in (q, ksh, vsh))
        o_r, l_r = dil_attn_fwd(qg_, kg_, vg_, T // dil // DIL_BLOCK, name=f"dil_fwd{gi}")
        grp.append((qg_, kg_, vg_, o_r, l_r))
        os_.append(_from_residues(o_r, dil))
        ls.append(_from_residues(l_r, dil))
    qg1 = _tile_heads(W["mem_q_norm"][1], MEM_WIDTH)
    y_mem1 = mem_attn_fwd(pb, DIL_WIDTH // MEM_WIDTH, memkv[1][0], memkv[1][1], qg1, name="mem_attn1")
    ycat1 = mix_fwd(os_, ls, y_mem1)
    x3 = _mm(ycat1, W["b_w_out"][0], add=x2, name="b_out")
    x4, ffn1 = _ffn_fwd(x3, 1, W)

    dx4, loss = loss_fwd_bwd(x4, target)

    dx3 = _ffn_bwd(dx4, 1, W, ffn1, G)
    dycat1 = _mm(dx3, W["b_w_out"][0], tb=True, name="b_out_dx")
    G["b_w_out"] = _mm(ycat1, dx3, ta=True, name="b_out_dw")[None]
    dq_mem1, dkn1, dvm1, dqg1 = mem_attn_bwd(pb, DIL_WIDTH // MEM_WIDTH, memkv[1][0], memkv[1][1], qg1, dycat1, 1,
                                             name="mem_attn_bwd1")
    G["mem_q_norm"][1] = dqg1[0]
    d_os_ls = mix_bwd(os_, ls, dycat1)
    dqs, dks, dvs = [], [], []
    for gi, (win, dil) in enumerate(DIL_GROUPS):
        qg_, kg_, vg_, o_r, l_r = grp[gi]
        do_r, dl_r = _to_residues(d_os_ls[gi], dil), _to_residues(d_os_ls[3 + gi], dil)
        dq_r, dk_r, dv_r = dil_attn_bwd(qg_, kg_, vg_, o_r, l_r, do_r, dl_r, T // dil // DIL_BLOCK, name=f"dil_bwd{gi}")
        dqs.append(_from_residues(dq_r, dil))
        dks.append(_from_residues(dk_r, dil))
        dvs.append(_from_residues(dv_r, dil))
    dq, dk, dv = (jnp.concatenate(z, axis=1) for z in (dqs, dks, dvs))
    dpb, dkvp, dqn_g, dkn_g = qk_bwd(kvp, pb, kg_t, qg_t, cos, sin, dq, dk, dv, dq_mem1)
    G["b_q_norm"] = dqn_g
    G["kv_k_norm"] = dkn_g[0]
    dh1 = _mm(dpb, W["b_w_in"][0], tb=True, name="b_in_dx")
    G["b_w_in"] = _mm(h1, dpb, ta=True, name="b_in_dw")[None]
    dhkv = _mm(dkvp, W["kv_w"], tb=True, name="kv_in_dx")
    G["kv_w"] = _mm(hkv, dkvp, ta=True, name="kv_in_dw")
    dx2, dg1, dgkv = rms_bwd(x2, [W["attn_norm"][1:2], W["kv_norm"].reshape(1, -1)], [dh1, dhkv], dx3,
                             name="attn_rms_bwd1")
    G["attn_norm"][1] = dg1[0]
    G["kv_norm"] = dgkv[0]

    dx1 = _ffn_bwd(dx2, 0, W, ffn0, G)
    dycat0 = _mm(dx1, W["a_w_out"][0], tb=True, name="a_out_dx")
    G["a_w_out"] = _mm(ycat0, dx1, ta=True, name="a_out_dw")[None]
    dq_mem0, dkn0, dvm0, dqg0 = mem_attn_bwd(p, SHIFT_WIDTH // MEM_WIDTH, memkv[0][0], memkv[0][1], qg0, dycat0,
                                             RWKV_WIDTH // MEM_WIDTH, name="mem_attn_bwd0")
    G["mem_q_norm"][0] = dqg0[0]
    dy, dr_b, dk2_b, dv_b, dg, dlw, dlb, drk = mix_gate_bwd(y, r, k2, v, g, dycat0, lnx_w, lnx_b, r_k)
    dr, dw, dk2, dv, dkk, dkka = scan_bwd(r, w, k2, v, kk, kka, states, final_state, dy)
    dxs, dmu, dw0, da0, dw2p, da2p, dg2p, dk_k, dk_a = rwkv_pre_bwd(
        p, mu, w0, a0, w2p, a2p, g2p, k_k, k_a, (dr, dr_b), dw, (dk2, dk2_b), (dv, dv_b), dkk, dkka, dg)
    dp = shift_bwd(dxs, mu, dq_mem0)
    G.update(a_mu=dmu, a_w0=dw0, a_a0=da0, a_w2=dw2p[None, :64], a_a2=da2p[None, 64:128], a_g2=dg2p[None, 128:],
             a_k_k=dk_k, a_k_a=dk_a, a_r_k=drk.reshape(1, RWKV_HEADS, HEAD_DIM), a_lnx_w=dlw, a_lnx_b=dlb)
    dh0 = _mm(dp, W["a_w_in"][0], tb=True, name="a_in_dx")
    G["a_w_in"] = _mm(h0, dp, ta=True, name="a_in_dw")[None]
    grad_x, dg0 = rms_bwd(x, [W["attn_norm"][0:1]], [dh0], dx1, name="attn_rms_bwd0")
    G["attn_norm"][0] = dg0[0]

    for i, (dkn, dvm) in enumerate(((dkn0, dvm0), (dkn1, dvm1))):
        dwkv, dgm, dkg = mem_kv_bwd(mem, W["mem_norm"][i:i + 1], W["mem_w_kv"][i],
                                    _tile_heads(W["mem_k_norm"][i], MEM_WIDTH), dkn, dvm, name=f"mem_kv_bwd{i}")
        G["mem_w_kv"][i], G["mem_norm"][i], G["mem_k_norm"][i] = dwkv, dgm[0], dkg[0]
    for n in list(G):
        if isinstance(G[n], list):
            G[n] = jnp.stack(G[n], axis=0)
    return loss, grad_x, G


HBM_SPEC = pl.BlockSpec(memory_space=pltpu.HBM)


def _mesh_pos():
    return lax.axis_index("x"), lax.axis_index("y"), lax.axis_index("c")


def _other_chips(x, y):
    return [(1 - x, y), (x, 1 - y), (1 - x, 1 - y)]


def _remote(send_sems, recv_sems, k, src, dst, to):
    return pltpu.make_async_remote_copy(src_ref=src, dst_ref=dst, send_sem=send_sems.at[k], recv_sem=recv_sems.at[k],
                                        device_id=to, device_id_type=MESH)


def _comm_call(body, name, ins, out_shape, n_remote):
    scratch = [pltpu.SemaphoreType.DMA((n_remote,)), pltpu.SemaphoreType.DMA((n_remote,))]
    return pl.pallas_call(body, name=name, in_specs=[HBM_SPEC] * len(ins), out_specs=[HBM_SPEC] * len(out_shape),
                          out_shape=out_shape, scratch_shapes=scratch)(*ins)


def comm_gather(wbig, wsm):
    def body(wb, ws, ob, os_, send_sems, recv_sems):
        x, y, c = _mesh_pos()
        s = 2 * x + y
        me, sibling = (x, y, c), (x, y, 1 - c)
        chips = _other_chips(x, y)
        rc = functools.partial(_remote, send_sems, recv_sems)
        first = []
        for j, (cx, cy) in enumerate(chips):
            first.append(rc(j, wb.at[c], ob.at[s, c], (cx, cy, c)))
            first.append(rc(6 + j, ws, os_.at[s], (cx, cy, c)))
        for cp in first:
            cp.start()
        passed = []
        for j, (cx, cy) in enumerate(chips):
            blk = ob.at[2 * cx + cy, c]
            rc(j, blk, blk, me).wait_recv()
            passed.append(rc(3 + j, blk, blk, sibling))
            passed[-1].start()
        for j, (cx, cy) in enumerate(chips):
            blk = ob.at[2 * cx + cy, 1 - c]
            rc(3 + j, blk, blk, me).wait_recv()
            sb = os_.at[2 * cx + cy]
            rc(6 + j, sb, sb, me).wait_recv()
        for cp in first + passed:
            cp.wait_send()

    out_shape = [jax.ShapeDtypeStruct((N_CHIPS, *wbig.shape), wbig.dtype),
                 jax.ShapeDtypeStruct((N_CHIPS, *wsm.shape), wsm.dtype)]
    return _comm_call(body, "comm_gather", [wbig, wsm], out_shape, 9)


def comm_pair_exchange(gb, gs):
    def body(gb_ref, gs_ref, rb_ref, rs_ref, send_sems, recv_sems):
        x, y, c = _mesh_pos()
        sibling = (x, y, 1 - c)
        rc = functools.partial(_remote, send_sems, recv_sems)
        cps = [rc(r, gb_ref.at[r, 1 - c], rb_ref.at[r], sibling) for r in range(N_CHIPS)]
        cps.append(rc(N_CHIPS, gs_ref.at[1 - c], rs_ref, sibling))
        for cp in cps:
            cp.start()
        for cp in cps:
            cp.wait()

    out_shape = [jax.ShapeDtypeStruct((N_CHIPS, *gb.shape[2:]), gb.dtype), jax.ShapeDtypeStruct(gs.shape[1:], gs.dtype)]
    return _comm_call(body, "comm_pair_exchange", [gb, gs], out_shape, N_CHIPS + 1)


def comm_chip_exchange(hb, hs):
    def body(hb_ref, hs_ref, qb_ref, qs_ref, send_sems, recv_sems):
        x, y, c = _mesh_pos()
        s = 2 * x + y
        me = (x, y, c)
        chips = _other_chips(x, y)
        rc = functools.partial(_remote, send_sems, recv_sems)
        cps = []
        for j, (cx, cy) in enumerate(chips):
            cps.append(rc(j, hb_ref.at[2 * cx + cy], qb_ref.at[s], (cx, cy, c)))
            cps.append(rc(3 + j, hs_ref, qs_ref.at[s], (cx, cy, c)))
        for cp in cps:
            cp.start()
        for j, (cx, cy) in enumerate(chips):
            blk = qb_ref.at[2 * cx + cy]
            rc(j, blk, blk, me).wait_recv()
            sb = qs_ref.at[2 * cx + cy]
            rc(3 + j, sb, sb, me).wait_recv()
        for cp in cps:
            cp.wait_send()

    out_shape = [jax.ShapeDtypeStruct(hb.shape, hb.dtype), jax.ShapeDtypeStruct((N_CHIPS, *hs.shape), hs.dtype)]
    return _comm_call(body, "comm_chip_exchange", [hb, hs], out_shape, 6)


def comm_pair_share(gh, gsh):
    def body(gh_ref, gs_ref, ob, os_, send_sems, recv_sems):
        x, y, c = _mesh_pos()
        rc = functools.partial(_remote, send_sems, recv_sems)
        cps = [rc(0, gh_ref, ob, (x, y, 1 - c)), rc(1, gs_ref, os_, (x, y, 1 - c))]
        for cp in cps:
            cp.start()
        for cp in cps:
            cp.wait()

    out_shape = [jax.ShapeDtypeStruct(gh.shape, gh.dtype), jax.ShapeDtypeStruct(gsh.shape, gsh.dtype)]
    return _comm_call(body, "comm_pair_share", [gh, gsh], out_shape, 2)


def add_pairs(a, b, out_dtype, *, name, tb):
    T, L = a.shape
    return _rowmap(lambda i, n, p, q: p + q, name=name, T=T, tb=tb, ins=[("row", a), ("row", b)],
                   outs=[("row", L, out_dtype)])[0]


def add_chips(parts, *, name, tb):
    T, L = parts[0].shape

    def fn(i, n, p0, p1, p2, p3):
        f = lambda z: z.astype(F32)
        return ((f(p0) + f(p1)) + f(p2)) + f(p3)

    return _rowmap(fn, name=name, T=T, tb=tb, ins=[("row", p) for p in parts], outs=[("row", L, F32)])[0]


def adamw(g, w, m, v, *, name, tb):
    T, L = g.shape

    def fn(i, n, g, w, m, v):
        m2 = ADAM_B1 * m + (1.0 - ADAM_B1) * g
        v2 = ADAM_B2 * v + (1.0 - ADAM_B2) * (g * g)
        m_hat = m2 / (1.0 - ADAM_B1 ** ADAM_STEP)
        v_hat = v2 / (1.0 - ADAM_B2 ** ADAM_STEP)
        return -ADAM_LR * (m_hat / (jnp.sqrt(v_hat) + ADAM_EPS) + ADAM_WD * w), m2, v2

    return _rowmap(fn, name=name, T=T, tb=tb, ins=[("row", z) for z in (g, w, m, v)], outs=[("row", L, F32)] * 3)


BIG_LANES = 1024
SMALL_LANES = 128


def _flat_cat(arrs, total, dtype):
    parts = [a.reshape(-1).astype(dtype) for a in arrs]
    n = sum(p.shape[0] for p in parts)
    assert n <= total, (n, total)
    if n < total:
        parts.append(jnp.zeros((total - n,), dtype))
    return jnp.concatenate(parts)


def _split_flat(flat, shapes):
    out, off = [], 0
    for shp in shapes:
        n = math.prod(shp)
        out.append(flat[off:off + n].reshape(shp))
        off += n
    return out


def _round_up(n, m):
    return -(-n // m) * m


def _full_shape(shard_shape, axis):
    return tuple(d * N_CHIPS if i == axis else d for i, d in enumerate(shard_shape))


def kernel(x, mem, attn_norm, a_w_in, a_mu, a_w0, a_w2, a_a0, a_a2, a_g2, a_k_k, a_k_a, a_r_k, a_lnx_w, a_lnx_b, a_w_out, kv_norm, kv_w, kv_k_norm, b_w_in, b_q_norm, b_w_out, mem_norm, mem_w_kv, mem_q_norm, mem_k_norm, ffn_norm, ffn_w_up, ffn_conv_w, ffn_conv_b, ffn_w_down, loss_target, m_attn_norm, m_a_w_in, m_a_mu, m_a_w0, m_a_w2, m_a_a0, m_a_a2, m_a_g2, m_a_k_k, m_a_k_a, m_a_r_k, m_a_lnx_w, m_a_lnx_b, m_a_w_out, m_kv_norm, m_kv_w, m_kv_k_norm, m_b_w_in, m_b_q_norm, m_b_w_out, m_mem_norm, m_mem_w_kv, m_mem_q_norm, m_mem_k_norm, m_ffn_norm, m_ffn_w_up, m_ffn_conv_w, m_ffn_conv_b, m_ffn_w_down, v_attn_norm, v_a_w_in, v_a_mu, v_a_w0, v_a_w2, v_a_a0, v_a_a2, v_a_g2, v_a_k_k, v_a_k_a, v_a_r_k, v_a_lnx_w, v_a_lnx_b, v_a_w_out, v_kv_norm, v_kv_w, v_kv_k_norm, v_b_w_in, v_b_q_norm, v_b_w_out, v_mem_norm, v_mem_w_kv, v_mem_q_norm, v_mem_k_norm, v_ffn_norm, v_ffn_w_up, v_ffn_conv_w, v_ffn_conv_b, v_ffn_w_down):
    args = (attn_norm, a_w_in, a_mu, a_w0, a_w2, a_a0, a_a2, a_g2, a_k_k, a_k_a, a_r_k, a_lnx_w, a_lnx_b, a_w_out, kv_norm, kv_w, kv_k_norm, b_w_in, b_q_norm, b_w_out, mem_norm, mem_w_kv, mem_q_norm, mem_k_norm, ffn_norm, ffn_w_up, ffn_conv_w, ffn_conv_b, ffn_w_down)
    ms = (m_attn_norm, m_a_w_in, m_a_mu, m_a_w0, m_a_w2, m_a_a0, m_a_a2, m_a_g2, m_a_k_k, m_a_k_a, m_a_r_k, m_a_lnx_w, m_a_lnx_b, m_a_w_out, m_kv_norm, m_kv_w, m_kv_k_norm, m_b_w_in, m_b_q_norm, m_b_w_out, m_mem_norm, m_mem_w_kv, m_mem_q_norm, m_mem_k_norm, m_ffn_norm, m_ffn_w_up, m_ffn_conv_w, m_ffn_conv_b, m_ffn_w_down)
    vs = (v_attn_norm, v_a_w_in, v_a_mu, v_a_w0, v_a_w2, v_a_a0, v_a_a2, v_a_g2, v_a_k_k, v_a_k_a, v_a_r_k, v_a_lnx_w, v_a_lnx_b, v_a_w_out, v_kv_norm, v_kv_w, v_kv_k_norm, v_b_w_in, v_b_q_norm, v_b_w_out, v_mem_norm, v_mem_w_kv, v_mem_q_norm, v_mem_k_norm, v_ffn_norm, v_ffn_w_up, v_ffn_conv_w, v_ffn_conv_b, v_ffn_w_down)
    w_sh, m_sh, v_sh = (dict(zip(WEIGHTS, z)) for z in (args, ms, vs))
    xi, yi, ci = _mesh_pos()
    chip = 2 * xi + yi
    big_names, big_axes = [n for n, _ in BIG], dict(BIG)
    ss_names, ss_axes = [n for n, _ in SMALL_SHARDED], dict(SMALL_SHARDED)
    big_shapes = [w_sh[n].shape for n in big_names]
    ss_shapes = [w_sh[n].shape for n in ss_names]
    n_big = sum(math.prod(s) for s in big_shapes)
    assert n_big % (2 * 8 * BIG_LANES) == 0
    mh = n_big // (2 * BIG_LANES)
    n_ss = _round_up(sum(math.prod(s) for s in ss_shapes), 8 * SMALL_LANES)

    wbig = _flat_cat([w_sh[n] for n in big_names], n_big, MMD).reshape(2, mh, BIG_LANES)
    wsm = _flat_cat([w_sh[n] for n in ss_names], n_ss, F32).reshape(-1, SMALL_LANES)
    wbig_all, wsm_all = comm_gather(wbig, wsm)
    wbig_all = lax.dynamic_update_index_in_dim(wbig_all, wbig, chip, 0).reshape(N_CHIPS, -1)
    wsm_all = lax.dynamic_update_index_in_dim(wsm_all, wsm, chip, 0).reshape(N_CHIPS, -1)
    W = {n: w_sh[n] for n in SMALL_REPL}
    for names, axes, shapes, allv in ((big_names, big_axes, big_shapes, wbig_all), (ss_names, ss_axes, ss_shapes, wsm_all)):
        per_chip = [_split_flat(allv[j], shapes) for j in range(N_CHIPS)]
        for k, n in enumerate(names):
            W[n] = jnp.concatenate([per_chip[j][k] for j in range(N_CHIPS)], axis=axes[n])
    for n in ("a_w2", "a_a2", "a_g2"):
        W[n] = W[n].astype(MMD)

    loss_blk, grad_x, G = local_step(x[0], mem[0], loss_target[0], W)
    loss = lax.psum(loss_blk[0, 0], ("x", "y", "c"))

    big_parts = [jnp.split(G[n], N_CHIPS, axis=big_axes[n]) for n in big_names]
    gbig = jnp.stack([_flat_cat([p[j] for p in big_parts], n_big, F32) for j in range(N_CHIPS)])
    gbig = gbig.reshape(N_CHIPS, 2, mh, BIG_LANES)
    sm_full_names = ss_names + list(SMALL_REPL)
    sm_full_shapes = [_full_shape(w_sh[n].shape, ss_axes[n]) for n in ss_names] + [w_sh[n].shape for n in SMALL_REPL]
    n_smf = _round_up(sum(math.prod(s) for s in sm_full_shapes), 2 * 8 * SMALL_LANES)
    msh = n_smf // (2 * SMALL_LANES)
    gsm = _flat_cat([G[n] for n in sm_full_names], n_smf, F32).reshape(2, msh, SMALL_LANES)
    rb, rs = comm_pair_exchange(gbig, gsm)
    mine_b = lax.dynamic_index_in_dim(gbig, ci, axis=1, keepdims=False)
    mine_s = lax.dynamic_index_in_dim(gsm, ci, axis=0, keepdims=False)
    hb = add_pairs(mine_b.reshape(-1, BIG_LANES), rb.reshape(-1, BIG_LANES), BF16, name="add_pairs_big", tb=128)
    hs = add_pairs(mine_s, rs, F32, name="add_pairs_small", tb=msh)
    hb = hb.reshape(N_CHIPS, mh, BIG_LANES)
    qb, qs = comm_chip_exchange(hb, hs)
    qb = lax.dynamic_update_index_in_dim(qb, lax.dynamic_index_in_dim(hb, chip, 0, keepdims=False), chip, 0)
    qs = lax.dynamic_update_index_in_dim(qs, hs, chip, 0)
    gh = add_chips([qb[j] for j in range(N_CHIPS)], name="add_chips_big", tb=32)
    gsh = add_chips([qs[j] for j in range(N_CHIPS)], name="add_chips_small", tb=msh)
    rh, rsh = comm_pair_share(gh, gsh)
    gfull = jnp.where(ci == 0, jnp.stack([gh, rh]), jnp.stack([rh, gh]))
    gsfull = jnp.where(ci == 0, jnp.stack([gsh, rsh]), jnp.stack([rsh, gsh]))

    res = {tag: {} for tag in ("grad", "delta", "new_m", "new_v")}
    for n, g in zip(big_names, _split_flat(gfull.reshape(-1), big_shapes)):
        shp = w_sh[n].shape
        rows = lambda z: z.reshape(-1, shp[-1])
        nrow = math.prod(shp[:-1])
        tb = next(t for t in (512, 256, 128, 64) if nrow % t == 0 and t * shp[-1] <= (1 << 19))
        outs = adamw(rows(g), rows(w_sh[n]), rows(m_sh[n]), rows(v_sh[n]), name=f"adamw_{n}", tb=tb)
        res["grad"][n] = g
        for tag, o in zip(("delta", "new_m", "new_v"), outs):
            res[tag][n] = o.reshape(shp)
    sm_full = dict(zip(sm_full_names, _split_flat(gsfull.reshape(-1), sm_full_shapes)))
    g_loc = {}
    for n in ss_names:
        size = w_sh[n].shape[ss_axes[n]]
        g_loc[n] = lax.dynamic_slice_in_dim(sm_full[n], chip * size, size, axis=ss_axes[n])
    for n in SMALL_REPL:
        g_loc[n] = sm_full[n]
    n_sml = _round_up(sum(math.prod(w_sh[n].shape) for n in sm_full_names), 8 * SMALL_LANES)
    pack_sm = lambda d: _flat_cat([d[n] for n in sm_full_names], n_sml, F32).reshape(-1, SMALL_LANES)
    d_sm, m_sm, v_sm = adamw(pack_sm(g_loc), pack_sm(w_sh), pack_sm(m_sh), pack_sm(v_sh), name="adamw_small",
                             tb=n_sml // SMALL_LANES)
    sm_loc_shapes = [w_sh[n].shape for n in sm_full_names]
    res["grad"].update(g_loc)
    for tag, smv in (("delta", d_sm), ("new_m", m_sm), ("new_v", v_sm)):
        res[tag].update(dict(zip(sm_full_names, _split_flat(smv.reshape(-1), sm_loc_shapes))))
    return (loss, grad_x[None], *[res[tag][n] for tag in ("grad", "delta", "new_m", "new_v") for n in WEIGHTS])
```

```python
import functools
import math

import numpy as np
import jax
import jax.numpy as jnp
from jax import lax
from jax.experimental import pallas as pl
from jax.experimental.pallas import tpu as pltpu

F32 = jnp.float32
BF16 = jnp.bfloat16
MMD = jnp.bfloat16

D_MODEL = 1024
HEAD_DIM = 64
N_MEM = 256
MEM_WIDTH = 256
RWKV_HEADS = 12
RWKV_WIDTH = 768
SHIFT_WIDTH = 2560
LORA_WIDTH = 256
DIL_WIDTH = 768
DIL_GROUPS = ((128, 1), (512, 4), (2048, 16))
DIL_BLOCK = 128
D_FF = 2816
ROPE_THETA = 10000.0
RMS_EPS = 1e-6
LNX_EPS = 64e-5
NEG_INF = -1e30
ADAM_LR = 0.001
ADAM_B1 = 0.9
ADAM_B2 = 0.999
ADAM_EPS = 1e-08
ADAM_WD = 0.01
ADAM_STEP = 10
N_CHIPS = 4
MESH = pl.DeviceIdType.MESH
VMEM_LIMIT_MB = 56
SCAN_CHUNK = 16
SCAN_UNROLL = 4
SCAN_UNROLL_BWD = 2

BIG = (("a_w_in", 2), ("a_w_out", 1), ("kv_w", 1), ("b_w_in", 1), ("b_w_out", 2), ("mem_w_kv", 1),
       ("ffn_w_up", 2), ("ffn_w_down", 1))
EARLY_BIG = ("a_w_in",)
SMALL_SHARDED = (("a_mu", 1), ("a_w0", 1), ("a_w2", 2), ("a_a0", 1), ("a_a2", 2), ("a_g2", 2), ("a_k_k", 1),
                 ("a_k_a", 1), ("a_lnx_w", 1), ("a_lnx_b", 1), ("ffn_conv_w", 2))
SMALL_REPL = ("attn_norm", "a_r_k", "kv_norm", "kv_k_norm", "b_q_norm", "mem_norm", "mem_q_norm", "mem_k_norm",
              "ffn_norm", "ffn_conv_b")
WEIGHTS = ("attn_norm", "a_w_in", "a_mu", "a_w0", "a_w2", "a_a0", "a_a2", "a_g2", "a_k_k", "a_k_a", "a_r_k",
           "a_lnx_w", "a_lnx_b", "a_w_out", "kv_norm", "kv_w", "kv_k_norm", "b_w_in", "b_q_norm", "b_w_out",
           "mem_norm", "mem_w_kv", "mem_q_norm", "mem_k_norm", "ffn_norm", "ffn_w_up", "ffn_conv_w", "ffn_conv_b",
           "ffn_w_down")


def _cp(sem=None, **kw):
    return pltpu.CompilerParams(dimension_semantics=sem, vmem_limit_bytes=VMEM_LIMIT_MB << 20, **kw)


def _tile(n, cands=(512, 256, 128)):
    for c in cands:
        if n % c == 0:
            return c
    return n


def _mm(a, b, *, name, ta=False, tb=False, add=None, out_dtype=F32):
    K, M = a.shape if ta else a.shape[::-1]
    N = b.shape[0] if tb else b.shape[1]
    assert K == (b.shape[1] if tb else b.shape[0])
    tm, tn = _tile(M, (512, 256, 128) if ta else (1024, 512, 256, 128)), _tile(N)
    a_spec = pl.BlockSpec((K, tm), lambda i, j: (0, i)) if ta else pl.BlockSpec((tm, K), lambda i, j: (i, 0))
    b_spec = pl.BlockSpec((tn, K), lambda i, j: (j, 0)) if tb else pl.BlockSpec((K, tn), lambda i, j: (0, j))
    o_spec = pl.BlockSpec((tm, tn), lambda i, j: (i, j))
    dn = (((0,) if ta else (1,), (1,) if tb else (0,)), ((), ()))
    has_add = add is not None

    def body(*refs):
        a_ref, b_ref = refs[0], refs[1]
        o_ref = refs[-1]
        acc = lax.dot_general(a_ref[...].astype(MMD), b_ref[...].astype(MMD), dn, preferred_element_type=F32)
        if has_add:
            acc = acc + refs[2][...]
        o_ref[...] = acc.astype(o_ref.dtype)

    ins = [a, b] + ([add] if has_add else [])
    specs = [a_spec, b_spec] + ([o_spec] if has_add else [])
    return pl.pallas_call(
        body, name=name, grid=(M // tm, N // tn), in_specs=specs, out_specs=o_spec,
        out_shape=jax.ShapeDtypeStruct((M, N), out_dtype), compiler_params=_cp(("parallel", "parallel")),
    )(*ins)


def _rowmap(fn, *, name, T, tb, ins, outs, accs=()):
    nblk = T // tb
    assert T % tb == 0 and tb % 8 == 0
    in_specs, args = [], []
    for spec in ins:
        kind, arr = spec[0], spec[1]
        w, cb = (spec[2], spec[3]) if len(spec) > 2 else (arr.shape[-1], 0)
        if kind == "row":
            in_specs.append(pl.BlockSpec((tb, w), lambda i, cb=cb: (i, cb)))
        elif kind == "prev":
            in_specs.append(pl.BlockSpec((8, w), lambda i, cb=cb: (jnp.maximum(i * (tb // 8) - 1, 0), cb)))
        elif kind == "next":
            in_specs.append(pl.BlockSpec((8, w), lambda i, cb=cb: (jnp.minimum((i + 1) * (tb // 8), T // 8 - 1), cb)))
        elif kind == "const":
            in_specs.append(pl.BlockSpec(arr.shape, lambda i, nd=arr.ndim: (0,) * nd))
        else:
            raise ValueError(kind)
        args.append(arr)
    out_shape, out_specs = [], []
    for kind, w, dt in outs:
        out_shape.append(jax.ShapeDtypeStruct((T, w), dt))
        out_specs.append(pl.BlockSpec((tb, w), lambda i: (i, 0)))
    for shp, dt in accs:
        out_shape.append(jax.ShapeDtypeStruct(shp, dt))
        out_specs.append(pl.BlockSpec(shp, lambda i, nd=len(shp): (0,) * nd))
    n_in, n_out = len(ins), len(outs)

    def body(*refs):
        i = pl.program_id(0)
        vals = [r[...] for r in refs[:n_in]]
        res = fn(i, nblk, *vals)
        if not isinstance(res, (tuple, list)):
            res = (res,)
        assert len(res) == n_out + len(accs), (name, len(res))
        for r, v in zip(refs[n_in:n_in + n_out], res[:n_out]):
            r[...] = v.astype(r.dtype)
        acc_refs = refs[n_in + n_out:]
        if acc_refs:
            @pl.when(i == 0)
            def _():
                for r in acc_refs:
                    r[...] = jnp.zeros(r.shape, r.dtype)

            for r, v in zip(acc_refs, res[n_out:]):
                r[...] += v

    res = pl.pallas_call(
        body, name=name, grid=(nblk,), in_specs=in_specs, out_specs=out_specs, out_shape=out_shape,
        compiler_params=_cp(("arbitrary",)),
    )(*args)
    return res


def _row_pick(halo, r):
    rid = lax.broadcasted_iota(jnp.int32, halo.shape, 0)
    return jnp.sum(jnp.where(rid == r, halo, 0.0), axis=0, keepdims=True)


def _shift_down(x, row_before, is_first):
    rid = lax.broadcasted_iota(jnp.int32, x.shape, 0)
    first = jnp.where(is_first, 0.0, 1.0) * row_before
    return jnp.where(rid == 0, first, pltpu.roll(x, 1, axis=0))


def _shift_up(x, row_after, is_last):
    n = x.shape[0]
    rid = lax.broadcasted_iota(jnp.int32, x.shape, 0)
    last = jnp.where(is_last, 0.0, 1.0) * row_after
    return jnp.where(rid == n - 1, last, pltpu.roll(x, n - 1, axis=0))


def _dot(a, b, dn=(((1,), (0,)), ((), ()))):
    return lax.dot_general(a.astype(MMD), b.astype(MMD), dn, preferred_element_type=F32)


def _dot_nt(a, b):
    return _dot(a, b, (((1,), (1,)), ((), ())))


def _dot_tn(a, b):
    return _dot(a, b, (((0,), (0,)), ((), ())))


def _dot_exact01(x, g01):
    hi = x.astype(BF16)
    lo = (x - hi.astype(F32)).astype(BF16)
    dn = (((1,), (0,)), ((), ()))
    return (lax.dot_general(hi, g01, dn, preferred_element_type=F32)
            + lax.dot_general(lo, g01, dn, preferred_element_type=F32))


def _fold_heads(v, fold):
    return _row_pick(_dot_exact01(jnp.broadcast_to(v, (8, v.shape[1])), fold), 0)


def _group_ones(width):
    idx = np.arange(width) // HEAD_DIM
    return jnp.asarray((idx[:, None] == idx[None, :]).astype(np.float32), BF16)


def _fold_ones(width):
    idx = np.arange(width) % HEAD_DIM
    return jnp.asarray((idx[:, None] == np.arange(HEAD_DIM)[None, :]).astype(np.float32), BF16)


def _head_masks(width):
    idx = np.arange(width) // HEAD_DIM
    return jnp.asarray((idx[None, :] == np.arange(width // HEAD_DIM)[:, None]).astype(np.float32)[:, None, :], F32)


def _rms_stats(x):
    r = lax.rsqrt(jnp.mean(x * x, axis=-1, keepdims=True) + RMS_EPS)
    return r, x * r


def rms_fwd(x, gains, *, name):
    T, D = x.shape

    def fn(i, nblk, xb, *gs):
        _, xh = _rms_stats(xb)
        return tuple(xh * g for g in gs)

    return _rowmap(fn, name=name, T=T, tb=512, ins=[("row", x)] + [("const", g) for g in gains],
                   outs=[("row", D, MMD)] * len(gains))


def rms_bwd(x, gains, dhs, dres, *, name):
    T, D = x.shape
    n = len(gains)

    def fn(i, nblk, xb, dr, *rest):
        gs, ds = rest[:n], rest[n:]
        r, xh = _rms_stats(xb)
        dx = dr
        dgs = []
        for g, dh in zip(gs, ds):
            dgs.append(jnp.sum(dh * xh, axis=0, keepdims=True))
            dxh = dh * g
            dx = dx + r * (dxh - xh * jnp.mean(dxh * xh, axis=-1, keepdims=True))
        return (dx, *dgs)

    return _rowmap(fn, name=name, T=T, tb=512,
                   ins=[("row", x), ("row", dres)] + [("const", g) for g in gains] + [("row", d) for d in dhs],
                   outs=[("row", D, F32)], accs=[((1, D), F32)] * n)


def _segsum(x):
    first = lax.broadcasted_iota(jnp.int32, (x.shape[0], 128), 1) < HEAD_DIM
    outs = []
    for p in range(x.shape[1] // 128):
        xs = x[:, p * 128:(p + 1) * 128]
        lo = jnp.sum(jnp.where(first, xs, 0.0), axis=-1, keepdims=True)
        hi = jnp.sum(jnp.where(first, 0.0, xs), axis=-1, keepdims=True)
        outs.append(jnp.where(first, lo, hi))
    return jnp.concatenate(outs, axis=1)


def _pre1_common(i, ps, halo, mu, w0, a0, w2p, a2p, g2p, k_k, k_a):
    prev = _shift_down(ps, _row_pick(halo, 7), i == 0)
    xs = ps + (prev - ps) * mu
    lo = xs[:, 3 * RWKV_WIDTH:]
    tl, sl = jnp.tanh(lo), jax.nn.sigmoid(lo)
    dec = w0 + _dot(tl, w2p)
    ain = a0 + _dot(lo, a2p)
    g = _dot(sl, g2p)
    wl = -jax.nn.softplus(-dec) - 0.5
    w = jnp.exp(-jnp.exp(wl))
    a = jax.nn.sigmoid(ain)
    k = xs[:, RWKV_WIDTH:2 * RWKV_WIDTH]
    z = k * k_k
    nrm = jnp.sqrt(_segsum(z * z))
    kk = z / jnp.maximum(nrm, 1e-12)
    return prev, xs, lo, tl, sl, dec, wl, w, a, g, k, nrm, kk


def rwkv_pre_fwd(p, mu, w0, a0, w2p, a2p, g2p, k_k, k_a):
    T = p.shape[0]

    def fn(i, nblk, ps, halo, mu, w0, a0, w2p, a2p, g2p, k_k, k_a):
        _, xs, _, _, _, _, _, w, a, g, k, _, kk = _pre1_common(i, ps, halo, mu, w0, a0, w2p, a2p, g2p, k_k, k_a)
        W = RWKV_WIDTH
        return xs[:, :W], w, k * (1.0 + (a - 1.0) * k_a), xs[:, 2 * W:3 * W], kk, kk * a, g

    return _rowmap(fn, name="rwkv_pre_fwd", T=T, tb=256,
                   ins=[("row", p, SHIFT_WIDTH, 0), ("prev", p, SHIFT_WIDTH, 0)]
                   + [("const", c) for c in (mu, w0, a0, w2p, a2p, g2p, k_k, k_a)],
                   outs=[("row", RWKV_WIDTH, F32)] * 7)


def rwkv_pre_bwd(p, mu, w0, a0, w2p, a2p, g2p, k_k, k_a, drs, dw, dk2s, dvs, dkk, dkka, dg):
    T = p.shape[0]

    def fn(i, nblk, ps, halo, mu, w0, a0, w2p, a2p, g2p, k_k, k_a, dr0, dr1, dw, dk20, dk21, dv0, dv1, dkk, dkka, dg):
        prev, xs, lo, tl, sl, dec, wl, w, a, g, k, nrm, kk = _pre1_common(i, ps, halo, mu, w0, a0, w2p, a2p, g2p, k_k, k_a)
        dk2 = dk20 + dk21
        dkk_t = dkk + dkka * a
        proj = jnp.where(nrm > 1e-12, kk * _segsum(dkk_t * kk), 0.0)
        dz = (dkk_t - proj) / jnp.maximum(nrm, 1e-12)
        dk = dz * k_k + dk2 * (1.0 + (a - 1.0) * k_a)
        da = dkka * kk + dk2 * k * k_a
        ddec = dw * (-w * jnp.exp(wl)) * jax.nn.sigmoid(-dec)
        dain = da * a * (1.0 - a)
        dlo = (_dot_nt(ddec, w2p) * (1.0 - tl * tl) + _dot_nt(dain, a2p) + _dot_nt(dg, g2p) * sl * (1.0 - sl))
        dxs = jnp.concatenate([dr0 + dr1, dk, dv0 + dv1, dlo], axis=1)
        s = lambda z: jnp.sum(z, axis=0, keepdims=True)
        return (dxs, s(dxs * (prev - ps)), s(ddec), s(dain), _dot_tn(tl, ddec), _dot_tn(lo, dain), _dot_tn(sl, dg),
                s(dz * k), s(dk2 * k * (a - 1.0)))

    return _rowmap(fn, name="rwkv_pre_bwd", T=T, tb=128,
                   ins=[("row", p, SHIFT_WIDTH, 0), ("prev", p, SHIFT_WIDTH, 0)]
                   + [("const", c) for c in (mu, w0, a0, w2p, a2p, g2p, k_k, k_a)]
                   + [("row", c) for c in (*drs, dw, *dk2s, *dvs, dkk, dkka, dg)],
                   outs=[("row", SHIFT_WIDTH, F32)],
                   accs=[((1, SHIFT_WIDTH), F32), ((1, RWKV_WIDTH), F32), ((1, RWKV_WIDTH), F32)]
                   + [((LORA_WIDTH, RWKV_WIDTH), F32)] * 3 + [((1, RWKV_WIDTH), F32)] * 2)


def shift_bwd(dxs, mu, dq_mem):
    T = dxs.shape[0]

    def fn(i, nblk, d, halo, mu, dq):
        nxt = _shift_up(d, _row_pick(halo, 0), i == nblk - 1)
        return jnp.concatenate([d * (1.0 - mu) + nxt * mu, dq], axis=1)

    return _rowmap(fn, name="shift_bwd", T=T, tb=256,
                   ins=[("row", dxs), ("next", dxs), ("const", mu), ("row", dq_mem)],
                   outs=[("row", SHIFT_WIDTH + MEM_WIDTH, MMD)])[0]


N_PAIRS = RWKV_HEADS // 2


def _pair_consts():
    row = lax.broadcasted_iota(jnp.int32, (HEAD_DIM, 128), 0)
    lane = lax.broadcasted_iota(jnp.int32, (HEAD_DIM, 128), 1)
    eye2 = jnp.logical_or(lane == row, lane == row + HEAD_DIM).astype(F32)
    li = lax.broadcasted_iota(jnp.int32, (128, 128), 0) < HEAD_DIM
    lj = lax.broadcasted_iota(jnp.int32, (128, 128), 1) < HEAD_DIM
    return eye2, (li == lj).astype(BF16)


def _pair_sum(p, ones2):
    n, m, l = p.shape
    s = lax.dot_general(p.reshape(n * m, l).astype(BF16), ones2, (((1,), (0,)), ((), ())), preferred_element_type=F32)
    return s.reshape(n, m, l)


def _pair_rows(row):
    return jnp.stack([row[:, p * 128:(p + 1) * 128] for p in range(N_PAIRS)], axis=0)


def _pair_flat(rows):
    return jnp.concatenate([rows[p] for p in range(N_PAIRS)], axis=1)


def _split_bf16(v):
    hi = v.astype(BF16).astype(F32)
    return hi, v - hi


def _gather_copies(src, dst, send_sems, recv_sems):
    x, y, c = _mesh_pos()
    s = 2 * x + y
    me, sibling = (x, y, c), (x, y, 1 - c)
    rc = functools.partial(_remote, send_sems, recv_sems)
    ici, land, fwd, arrived = [], [], [], []
    for j, (cx, cy) in enumerate(_other_chips(x, y)):
        ici.append(rc(j, src.at[c], dst.at[s, c], (cx, cy, c)))
        blk, blk2 = dst.at[2 * cx + cy, c], dst.at[2 * cx + cy, 1 - c]
        land.append(rc(j, blk, blk, me))
        fwd.append(rc(3 + j, blk, blk, sibling))
        arrived.append(rc(3 + j, blk2, blk2, me))
    return ici, land, fwd, arrived


def scan_fwd(r, w, k2, v, kk, kka, gather=None):
    T, W = r.shape
    tc = SCAN_CHUNK
    nchunk = T // tc
    seq = pl.BlockSpec((tc, W), lambda i: (i, 0))
    one_state = pl.BlockSpec((N_PAIRS, HEAD_DIM, 128), lambda i: (0, 0, 0))

    def body(r_ref, w_ref, k2_ref, v_ref, kk_ref, kka_ref, *rest):
        if gather is None:
            y_ref, st_ref, fin_ref, s_scr, vhi_scr, vlo_scr = rest
        else:
            src, y_ref, st_ref, fin_ref, dst, s_scr, vhi_scr, vlo_scr, send_sems, recv_sems = rest
            ici, land, fwd, arrived = _gather_copies(src, dst, send_sems, recv_sems)

            @pl.when(pl.program_id(0) == 0)
            def _():
                for cp in ici:
                    cp.start()

            @pl.when(pl.program_id(0) == nchunk // 2)
            def _():
                for a, f in zip(land, fwd):
                    a.wait_recv()
                    f.start()

        @pl.when(pl.program_id(0) == 0)
        def _():
            s_scr[...] = jnp.zeros(s_scr.shape, F32)

        vhi_scr[...], vlo_scr[...] = _split_bf16(v_ref[...])
        eye2, ones2 = _pair_consts()

        def step(t, carry):
            r_t, w_t, k2_t, kk_t, kka_t, vhi_t, vlo_t = (
                _pair_rows(ref[pl.ds(t, 1), :]) for ref in (r_ref, w_ref, k2_ref, kk_ref, kka_ref, vhi_scr, vlo_scr))
            S = s_scr[...]
            sa = -_pair_sum(S * kk_t, ones2)
            vb = _pair_sum(eye2 * vhi_t, ones2) + _pair_sum(eye2 * vlo_t, ones2)
            S2 = S * w_t + sa * kka_t + vb * k2_t
            y_ref[pl.ds(t, 1), :] = _pair_flat(jnp.sum(eye2 * _pair_sum(S2 * r_t, ones2), axis=1, keepdims=True))
            s_scr[...] = S2
            st_ref[t] = S
            return carry

        lax.fori_loop(0, tc, step, 0, unroll=SCAN_UNROLL)
        fin_ref[...] = s_scr[...]

        if gather is not None:
            @pl.when(pl.program_id(0) == nchunk - 1)
            def _():
                for a in arrived:
                    a.wait_recv()
                for cp in ici + fwd:
                    cp.wait_send()

    in_specs = [seq] * 6
    out_specs = [seq, pl.BlockSpec((tc, N_PAIRS, HEAD_DIM, 128), lambda i: (i, 0, 0, 0)), one_state]
    out_shape = [jax.ShapeDtypeStruct((T, W), F32), jax.ShapeDtypeStruct((T, N_PAIRS, HEAD_DIM, 128), F32),
                 jax.ShapeDtypeStruct((N_PAIRS, HEAD_DIM, 128), F32)]
    scratch = [pltpu.VMEM((N_PAIRS, HEAD_DIM, 128), F32), pltpu.VMEM((tc, W), F32), pltpu.VMEM((tc, W), F32)]
    args = [r, w, k2, v, kk, kka]
    if gather is not None:
        in_specs.append(HBM_SPEC)
        out_specs.append(HBM_SPEC)
        out_shape.append(jax.ShapeDtypeStruct((N_CHIPS, *gather.shape), gather.dtype))
        scratch += [pltpu.SemaphoreType.DMA((6,)), pltpu.SemaphoreType.DMA((6,))]
        args.append(gather)
    return pl.pallas_call(
        body, name="rwkv_scan_fwd", grid=(nchunk,), in_specs=in_specs, out_specs=out_specs, out_shape=out_shape,
        scratch_shapes=scratch, compiler_params=_cp(("arbitrary",)),
    )(*args)


PEER_FLIPS = tuple((fx, fy, fc) for fx in (0, 1) for fy in (0, 1) for fc in (0, 1))[1:]


def scan_bwd(r, w, k2, v, kk, kka, states, final_state, dy, scatter=None):
    T, W = r.shape
    tc = SCAN_CHUNK
    nchunk = T // tc
    seq = pl.BlockSpec((tc, W), lambda i: (nchunk - 1 - i, 0))
    st_spec = pl.BlockSpec((tc, N_PAIRS, HEAD_DIM, 128), lambda i: (nchunk - 1 - i, 0, 0, 0))
    one_state = pl.BlockSpec((N_PAIRS, HEAD_DIM, 128), lambda i: (0, 0, 0))

    def body(r_ref, w_ref, k2_ref, v_ref, kk_ref, kka_ref, st_ref, fin_ref, dy_ref, *rest):
        if scatter is None:
            dr_ref, dw_ref, dk2_ref, dv_ref, dkk_ref, dkka_ref, ds_scr, sc_scr, vhi_scr, vlo_scr = rest
        else:
            (src, dr_ref, dw_ref, dk2_ref, dv_ref, dkk_ref, dkka_ref, dst, ds_scr, sc_scr, vhi_scr, vlo_scr,
             send_sems, recv_sems) = rest
            x, y, c = _mesh_pos()
            copies = []
            for k, (fx, fy, fc) in enumerate(PEER_FLIPS):
                px, py, pc = (1 - x if fx else x), (1 - y if fy else y), (1 - c if fc else c)
                copies.append(_remote(send_sems, recv_sems, k, src.at[2 * px + py, pc], dst.at[k], (px, py, pc)))

            @pl.when(pl.program_id(0) == 0)
            def _():
                for cp in copies:
                    cp.start()

        @pl.when(pl.program_id(0) == 0)
        def _():
            ds_scr[...] = jnp.zeros(ds_scr.shape, F32)
            sc_scr[...] = fin_ref[...]

        vhi_scr[...], vlo_scr[...] = _split_bf16(v_ref[...])
        eye2, ones2 = _pair_consts()
        colsum = lambda z: jnp.sum(z, axis=1, keepdims=True)

        def step(j, carry):
            t = tc - 1 - j
            r_t, w_t, k2_t, kk_t, kka_t, vhi_t, vlo_t, dy_t = (
                _pair_rows(ref[pl.ds(t, 1), :])
                for ref in (r_ref, w_ref, k2_ref, kk_ref, kka_ref, vhi_scr, vlo_scr, dy_ref))
            s_prev, s_cur = st_ref[t], sc_scr[...]
            dyb = _pair_sum(eye2 * dy_t, ones2)
            vb = _pair_sum(eye2 * vhi_t, ones2) + _pair_sum(eye2 * vlo_t, ones2)
            sa = -_pair_sum(s_prev * kk_t, ones2)
            dS = ds_scr[...] + dyb * r_t
            dsa = _pair_sum(dS * kka_t, ones2)
            ds_scr[...] = dS * w_t - dsa * kk_t
            sc_scr[...] = s_prev
            for ref, val in zip((dr_ref, dw_ref, dk2_ref, dv_ref, dkk_ref, dkka_ref),
                                (s_cur * dyb, dS * s_prev, dS * vb, eye2 * _pair_sum(dS * k2_t, ones2),
                                 -(s_prev * dsa), dS * sa)):
                ref[pl.ds(t, 1), :] = _pair_flat(colsum(val))
            return carry

        lax.fori_loop(0, tc, step, 0, unroll=SCAN_UNROLL_BWD)

        if scatter is not None:
            @pl.when(pl.program_id(0) == nchunk - 1)
            def _():
                for cp in copies:
                    cp.wait()

    in_specs = [seq] * 6 + [st_spec, one_state, seq]
    out_specs = [seq] * 6
    out_shape = [jax.ShapeDtypeStruct((T, W), F32)] * 6
    scratch = [pltpu.VMEM((N_PAIRS, HEAD_DIM, 128), F32)] * 2 + [pltpu.VMEM((tc, W), F32)] * 2
    args = [r, w, k2, v, kk, kka, states, final_state, dy]
    if scatter is not None:
        in_specs.append(HBM_SPEC)
        out_specs.append(HBM_SPEC)
        out_shape.append(jax.ShapeDtypeStruct((len(PEER_FLIPS), *scatter.shape[2:]), scatter.dtype))
        scratch += [pltpu.SemaphoreType.DMA((len(PEER_FLIPS),)), pltpu.SemaphoreType.DMA((len(PEER_FLIPS),))]
        args.append(scatter)
    return pl.pallas_call(
        body, name="rwkv_scan_bwd", grid=(nchunk,), in_specs=in_specs, out_specs=out_specs, out_shape=out_shape,
        scratch_shapes=scratch, compiler_params=_cp(("arbitrary",)),
    )(*args)


def _mix_common(y, r, k2, v, lnx_w, lnx_b, r_k):
    yc = y - _segsum(y) * (1.0 / HEAD_DIM)
    rstd = lax.rsqrt(_segsum(yc * yc) * (1.0 / HEAD_DIM) + LNX_EPS)
    yhat = yc * rstd
    s = _segsum(r * k2 * r_k)
    return rstd, yhat, s, yhat * lnx_w + lnx_b + s * v


def mix_gate_fwd(y, r, k2, v, g, y_mem, lnx_w, lnx_b, r_k):
    T = y.shape[0]

    def fn(i, nblk, y, r, k2, v, g, ym, lw, lb, rk):
        mix = _mix_common(y, r, k2, v, lw, lb, rk)[3]
        return jnp.concatenate([mix * g, ym], axis=1)

    return _rowmap(fn, name="mix_gate_fwd", T=T, tb=256,
                   ins=[("row", z) for z in (y, r, k2, v, g, y_mem)] + [("const", c) for c in (lnx_w, lnx_b, r_k)],
                   outs=[("row", RWKV_WIDTH + MEM_WIDTH, MMD)])[0]


def mix_gate_bwd(y, r, k2, v, g, dycat, lnx_w, lnx_b, r_k):
    T = y.shape[0]

    def fn(i, nblk, y, r, k2, v, g, dyc, lw, lb, rk):
        rstd, yhat, s, mix = _mix_common(y, r, k2, v, lw, lb, rk)
        dmix = dyc * g
        dyh = dmix * lw
        inv = 1.0 / HEAD_DIM
        dy = rstd * (dyh - _segsum(dyh) * inv - yhat * (_segsum(dyh * yhat) * inv))
        ds = _segsum(dmix * v)
        cs = lambda z: jnp.sum(z, axis=0, keepdims=True)
        return (dy, ds * k2 * rk, ds * r * rk, dmix * s, dyc * mix, cs(dmix * yhat), cs(dmix), cs(ds * r * k2))

    return _rowmap(fn, name="mix_gate_bwd", T=T, tb=256,
                   ins=[("row", z) for z in (y, r, k2, v, g)] + [("row", dycat, RWKV_WIDTH, 0)]
                   + [("const", c) for c in (lnx_w, lnx_b, r_k)],
                   outs=[("row", RWKV_WIDTH, F32)] * 5, accs=[((1, RWKV_WIDTH), F32)] * 3)


def _head_rms(x, gones):
    ms = _segsum(x * x) * (1.0 / HEAD_DIM)
    r = lax.rsqrt(ms + RMS_EPS)
    return r, x * r


def _head_rms_bwd(dxn_g, r, xh, gones):
    return r * (dxn_g - xh * (_segsum(dxn_g * xh) * (1.0 / HEAD_DIM)))


def mem_kv_fwd(mem, norm_g, w_kv, k_norm_t, *, name):
    gones = _group_ones(MEM_WIDTH)

    def body(mem_ref, g_ref, w_ref, kn_ref, go_ref, k_out, v_out):
        _, xh = _rms_stats(mem_ref[...])
        kv = _dot(xh * g_ref[...], w_ref[...])
        _, kh = _head_rms(kv[:, :MEM_WIDTH], go_ref[...])
        k_out[...] = kh * kn_ref[...]
        v_out[...] = kv[:, MEM_WIDTH:]

    return pl.pallas_call(
        body, name=name, out_shape=[jax.ShapeDtypeStruct((N_MEM, MEM_WIDTH), F32)] * 2, compiler_params=_cp(),
    )(mem, norm_g, w_kv, k_norm_t, gones)


def mem_kv_bwd(mem, norm_g, w_kv, k_norm_t, dkn, dv, *, name):
    gones, fold = _group_ones(MEM_WIDTH), _fold_ones(MEM_WIDTH)

    def body(mem_ref, g_ref, w_ref, kn_ref, go_ref, fo_ref, dkn_ref, dv_ref, dw_out, dg_out, dkg_out):
        _, xh = _rms_stats(mem_ref[...])
        hm = xh * g_ref[...]
        kv = _dot(hm, w_ref[...])
        r, kh = _head_rms(kv[:, :MEM_WIDTH], go_ref[...])
        dkn = dkn_ref[...]
        dkg_out[...] = _fold_heads(jnp.sum(dkn * kh, axis=0, keepdims=True), fo_ref[...])
        dkraw = _head_rms_bwd(dkn * kn_ref[...], r, kh, go_ref[...])
        dkv = jnp.concatenate([dkraw, dv_ref[...]], axis=1)
        dw_out[...] = _dot_tn(hm, dkv)
        dg_out[...] = jnp.sum(_dot_nt(dkv, w_ref[...]) * xh, axis=0, keepdims=True)

    return pl.pallas_call(
        body, name=name,
        out_shape=[jax.ShapeDtypeStruct((D_MODEL, 2 * MEM_WIDTH), F32), jax.ShapeDtypeStruct((1, D_MODEL), F32),
                   jax.ShapeDtypeStruct((1, HEAD_DIM), F32)],
        compiler_params=_cp(),
    )(mem, norm_g, w_kv, k_norm_t, gones, fold, dkn, dv)


def _mem_scores(qn, kn, masks, h):
    s = _dot_nt(qn * masks[h], kn) * (1.0 / math.sqrt(HEAD_DIM))
    s = s - jnp.max(s, axis=-1, keepdims=True)
    e = jnp.exp(s)
    return e / jnp.sum(e, axis=-1, keepdims=True)


def mem_attn_fwd(p, colblock, kn, v, q_norm_t, *, name):
    T = p.shape[0]
    gones, masks = _group_ones(MEM_WIDTH), _head_masks(MEM_WIDTH)

    def fn(i, nblk, q, kn, v, qg, go, masks):
        _, qh = _head_rms(q, go)
        qn = qh * qg
        out = jnp.zeros(q.shape, F32)
        for h in range(MEM_WIDTH // HEAD_DIM):
            out = out + _dot(_mem_scores(qn, kn, masks, h), v * masks[h])
        return out

    return _rowmap(fn, name=name, T=T, tb=512,
                   ins=[("row", p, MEM_WIDTH, colblock)] + [("const", c) for c in (kn, v, q_norm_t, gones, masks)],
                   outs=[("row", MEM_WIDTH, F32)])[0]


def mem_attn_bwd(p, colblock, kn, v, q_norm_t, dycat, dcolblock, *, name):
    T = p.shape[0]
    gones, masks, fold = _group_ones(MEM_WIDTH), _head_masks(MEM_WIDTH), _fold_ones(MEM_WIDTH)
    scale = 1.0 / math.sqrt(HEAD_DIM)

    def fn(i, nblk, q, dy, kn, v, qg, go, masks, fo):
        r, qh = _head_rms(q, go)
        qn = qh * qg
        dqn = jnp.zeros(q.shape, F32)
        dkn = jnp.zeros(kn.shape, F32)
        dv = jnp.zeros(v.shape, F32)
        for h in range(MEM_WIDTH // HEAD_DIM):
            pr = _mem_scores(qn, kn, masks, h)
            dyh = dy * masks[h]
            dpr = _dot_nt(dyh, v)
            ds = pr * (dpr - jnp.sum(dpr * pr, axis=-1, keepdims=True)) * scale
            dqn = dqn + _dot(ds, kn * masks[h])
            dkn = dkn + _dot_tn(ds, qn * masks[h])
            dv = dv + _dot_tn(pr, dyh)
        dqg = _fold_heads(jnp.sum(dqn * qh, axis=0, keepdims=True), fo)
        return _head_rms_bwd(dqn * qg, r, qh, go), dkn, dv, dqg

    return _rowmap(fn, name=name, T=T, tb=512,
                   ins=[("row", p, MEM_WIDTH, colblock), ("row", dycat, MEM_WIDTH, dcolblock)]
                   + [("const", c) for c in (kn, v, q_norm_t, gones, masks, fold)],
                   outs=[("row", MEM_WIDTH, F32)],
                   accs=[((N_MEM, MEM_WIDTH), F32), ((N_MEM, MEM_WIDTH), F32), ((1, HEAD_DIM), F32)])


def _ffn_conv(i, u, halo, cw, cb):
    up1 = _shift_down(u, _row_pick(halo, 7), i == 0)
    up2 = _shift_down(up1, _row_pick(halo, 6), i == 0)
    c = cb + cw[0] * up2 + cw[1] * up1 + cw[2] * u
    return up1, up2, c[:, :D_FF], c[:, D_FF:]


def ffn_act_fwd(u, cw, cb, *, name):
    T = u.shape[0]

    def fn(i, nblk, u, halo, c0, c1, c2, cb):
        _, _, gate, val = _ffn_conv(i, u, halo, (c0, c1, c2), cb)
        return jax.nn.silu(gate) * val

    return _rowmap(fn, name=name, T=T, tb=128, ins=[("row", u), ("prev", u)] + [("const", c) for c in (*cw, cb)],
                   outs=[("row", D_FF, MMD)])[0]


def ffn_act_bwd(u, cw, cb, dz, *, name):
    T = u.shape[0]

    def fn(i, nblk, u, halo, c0, c1, c2, cb, dz):
        up1, up2, gate, val = _ffn_conv(i, u, halo, (c0, c1, c2), cb)
        sg = jax.nn.sigmoid(gate)
        dgate = dz * val * sg * (1.0 + gate * (1.0 - sg))
        dval = dz * gate * sg
        dc = jnp.concatenate([dgate, dval], axis=1)
        s = lambda z: jnp.sum(z, axis=0, keepdims=True)
        return dc, s(dc * up2), s(dc * up1), s(dc * u), s(dc)

    return _rowmap(fn, name=name, T=T, tb=128,
                   ins=[("row", u), ("prev", u)] + [("const", c) for c in (*cw, cb)] + [("row", dz)],
                   outs=[("row", 2 * D_FF, F32)], accs=[((1, 2 * D_FF), F32)] * 4)


def conv_bwd(dc, cw, *, name):
    T = dc.shape[0]

    def fn(i, nblk, d, halo, c0, c1, c2):
        last = i == nblk - 1
        n1 = _shift_up(d, _row_pick(halo, 0), last)
        n2 = _shift_up(n1, _row_pick(halo, 1), last)
        return c2 * d + c1 * n1 + c0 * n2

    return _rowmap(fn, name=name, T=T, tb=128, ins=[("row", dc), ("next", dc)] + [("const", c) for c in cw],
                   outs=[("row", 2 * D_FF, MMD)])[0]


def _rope_swap(z):
    lane = lax.broadcasted_iota(jnp.int32, z.shape, 1) % HEAD_DIM
    w = z.shape[1]
    return jnp.where(lane < HEAD_DIM // 2, pltpu.roll(z, w - HEAD_DIM // 2, axis=1), pltpu.roll(z, HEAD_DIM // 2, axis=1))


def rope_tables(T):
    inv = (np.float32(ROPE_THETA) ** (-np.arange(0, HEAD_DIM, 2, dtype=np.float32) / np.float32(HEAD_DIM))).astype(np.float32)
    ang = (np.arange(T, dtype=np.float32)[:, None] * inv[None, :]).astype(np.float64)
    cos, sin = np.cos(ang).astype(np.float32), np.sin(ang).astype(np.float32)
    return (jnp.asarray(np.concatenate([cos, cos, cos, cos], axis=1)),
            jnp.asarray(np.concatenate([-sin, sin, -sin, sin], axis=1)))


def _rope_wide(t):
    return jnp.tile(t, (1, DIL_WIDTH // t.shape[1]))


def qk_fwd(kvp, pb, kg_t, qg_t, cos, sin):
    T = kvp.shape[0]
    gones = _group_ones(DIL_WIDTH)

    def fn(i, nblk, kraw, vraw, qraw, kg, qg, c, s, go):
        c, s = _rope_wide(c), _rope_wide(s)
        outs = []
        for raw, g in ((qraw, qg), (kraw, kg)):
            _, xh = _head_rms(raw, go)
            z = xh * g
            outs.append(z * c + _rope_swap(z) * s)
        return outs[0], outs[1], vraw

    return _rowmap(fn, name="qk_fwd", T=T, tb=256,
                   ins=[("row", kvp, DIL_WIDTH, 0), ("row", kvp, DIL_WIDTH, 1), ("row", pb, DIL_WIDTH, 0)]
                   + [("const", kg_t), ("const", qg_t), ("row", cos), ("row", sin), ("const", gones)],
                   outs=[("row", DIL_WIDTH, MMD)] * 3)


def qk_bwd(kvp, pb, kg_t, qg_t, cos, sin, dq, dk, dv, dq_mem):
    T = kvp.shape[0]
    gones, fold = _group_ones(DIL_WIDTH), _fold_ones(DIL_WIDTH)

    def fn(i, nblk, kraw, qraw, kg, qg, c, s, go, fo, dq, dk, dv, dqm):
        c, s = _rope_wide(c), _rope_wide(s)
        res, dgs = [], []
        for raw, g, d in ((qraw, qg, dq), (kraw, kg, dk)):
            r, xh = _head_rms(raw, go)
            dz = d * c + _rope_swap(d * s)
            dgs.append(_fold_heads(jnp.sum(dz * xh, axis=0, keepdims=True), fo))
            res.append(_head_rms_bwd(dz * g, r, xh, go))
        return (jnp.concatenate([res[0], dqm], axis=1), jnp.concatenate([res[1], dv], axis=1), dgs[0], dgs[1])

    return _rowmap(fn, name="qk_bwd", T=T, tb=256,
                   ins=[("row", kvp, DIL_WIDTH, 0), ("row", pb, DIL_WIDTH, 0), ("const", kg_t), ("const", qg_t),
                        ("row", cos), ("row", sin), ("const", gones), ("const", fold),
                        ("row", dq), ("row", dk), ("row", dv), ("row", dq_mem)],
                   outs=[("row", DIL_WIDTH + MEM_WIDTH, MMD), ("row", 2 * DIL_WIDTH, MMD)],
                   accs=[((1, HEAD_DIM), F32)] * 2)


def _band(kind):
    i = lax.broadcasted_iota(jnp.int32, (DIL_BLOCK, DIL_BLOCK), 0)
    j = lax.broadcasted_iota(jnp.int32, (DIL_BLOCK, DIL_BLOCK), 1)
    return (j <= i) if kind == "cur" else (j >= i)


def dil_attn_fwd(q, k, v, seq_blocks, *, name):
    T, W = q.shape
    nb = T // DIL_BLOCK
    masks = _head_masks(W)
    cur = pl.BlockSpec((DIL_BLOCK, W), lambda n: (n, 0))
    prv = pl.BlockSpec((DIL_BLOCK, W), lambda n: (jnp.maximum(n - 1, 0), 0))
    scale = 1.0 / math.sqrt(HEAD_DIM)

    def body(q_ref, kc_ref, kp_ref, vc_ref, vp_ref, m_ref, o_ref, l_ref):
        n = pl.program_id(0)
        has_prev = (n % seq_blocks) != 0
        q = q_ref[...].astype(F32)
        kc, kp = kc_ref[...].astype(F32), kp_ref[...].astype(F32)
        vc, vp = vc_ref[...].astype(F32), vp_ref[...].astype(F32)
        ok_c = _band("cur")
        ok_p = jnp.logical_and(_band("prev"), has_prev)
        o = jnp.zeros((DIL_BLOCK, W), F32)
        lse = jnp.zeros((DIL_BLOCK, W), F32)
        for h in range(W // HEAD_DIM):
            mh = m_ref[h]
            qh = q * mh
            sc = jnp.where(ok_c, _dot_nt(qh, kc) * scale, NEG_INF)
            sp = jnp.where(ok_p, _dot_nt(qh, kp) * scale, NEG_INF)
            mx = jnp.maximum(jnp.max(sc, axis=-1, keepdims=True), jnp.max(sp, axis=-1, keepdims=True))
            ec, ep = jnp.exp(sc - mx), jnp.exp(sp - mx)
            den = jnp.sum(ec, axis=-1, keepdims=True) + jnp.sum(ep, axis=-1, keepdims=True)
            o = o + (_dot(ec, vc * mh) + _dot(ep, vp * mh)) / den
            lse = lse + (mx + jnp.log(den)) * mh
        o_ref[...] = o
        l_ref[...] = lse

    return pl.pallas_call(
        body, name=name, grid=(nb,), in_specs=[cur, cur, prv, cur, prv, pl.BlockSpec(masks.shape, lambda n: (0, 0, 0))],
        out_specs=[cur, cur], out_shape=[jax.ShapeDtypeStruct((T, W), F32)] * 2,
        compiler_params=_cp(("parallel",)),
    )(q, k, k, v, v, masks)


def dil_attn_bwd(q, k, v, o, lse, do, dlse, seq_blocks, *, name):
    T, W = q.shape
    nb = T // DIL_BLOCK
    masks = _head_masks(W)
    cur = pl.BlockSpec((DIL_BLOCK, W), lambda n: (n, 0))
    prv = pl.BlockSpec((DIL_BLOCK, W), lambda n: (jnp.maximum(n - 1, 0), 0))
    nxt = pl.BlockSpec((DIL_BLOCK, W), lambda n: (jnp.minimum(n + 1, nb - 1), 0))
    scale = 1.0 / math.sqrt(HEAD_DIM)

    def body(qc_ref, qn_ref, kc_ref, kp_ref, vc_ref, vp_ref, oc_ref, on_ref, lc_ref, ln_ref, doc_ref, don_ref,
             dlc_ref, dln_ref, m_ref, dq_ref, dk_ref, dv_ref):
        n = pl.program_id(0)
        has_prev = (n % seq_blocks) != 0
        has_next = jnp.logical_and(((n + 1) % seq_blocks) != 0, n + 1 < nb)
        f = lambda ref: ref[...].astype(F32)
        qc, qn, kc, kp, vc, vp = f(qc_ref), f(qn_ref), f(kc_ref), f(kp_ref), f(vc_ref), f(vp_ref)
        doc, don = doc_ref[...], don_ref[...]
        ok_c = _band("cur")
        ok_p = jnp.logical_and(_band("prev"), has_prev)
        ok_n = jnp.logical_and(_band("prev"), has_next)
        dq = jnp.zeros((DIL_BLOCK, W), F32)
        dk = jnp.zeros((DIL_BLOCK, W), F32)
        dv = jnp.zeros((DIL_BLOCK, W), F32)

        def side(qh, kk, vv, doh, lse_h, corr, ok):
            s = _dot_nt(qh, kk) * scale
            pr = jnp.where(ok, jnp.exp(jnp.where(ok, s, NEG_INF) - lse_h), 0.0)
            ds = pr * (_dot_nt(doh, vv) + corr) * scale
            return pr, ds

        for h in range(W // HEAD_DIM):
            mh = m_ref[h]
            red = lambda z: jnp.sum(z * mh, axis=-1, keepdims=True)
            qh, doh = qc * mh, doc * mh
            lse_h = red(lc_ref[...]) * (1.0 / HEAD_DIM)
            corr = red(dlc_ref[...]) - red(doc * oc_ref[...])
            pr_c, ds_c = side(qh, kc, vc * mh, doh, lse_h, corr, ok_c)
            _, ds_p = side(qh, kp, vp * mh, doh, lse_h, corr, ok_p)
            dq = dq + _dot(ds_c, kc * mh) + _dot(ds_p, kp * mh)
            dk = dk + _dot_tn(ds_c, qh)
            dv = dv + _dot_tn(pr_c, doh)
            qh2, doh2 = qn * mh, don * mh
            lse_2 = red(ln_ref[...]) * (1.0 / HEAD_DIM)
            corr2 = red(dln_ref[...]) - red(don * on_ref[...])
            pr_n, ds_n = side(qh2, kc, vc * mh, doh2, lse_2, corr2, ok_n)
            dk = dk + _dot_tn(ds_n, qh2)
            dv = dv + _dot_tn(pr_n, doh2)
        dq_ref[...] = dq
        dk_ref[...] = dk
        dv_ref[...] = dv

    return pl.pallas_call(
        body, name=name, grid=(nb,),
        in_specs=[cur, nxt, cur, prv, cur, prv, cur, nxt, cur, nxt, cur, nxt, cur, nxt,
                  pl.BlockSpec(masks.shape, lambda n: (0, 0, 0))],
        out_specs=[cur] * 3, out_shape=[jax.ShapeDtypeStruct((T, W), F32)] * 3,
        compiler_params=_cp(("parallel",)),
    )(q, q, k, k, v, v, o, o, lse, lse, do, do, dlse, dlse, masks)


def _mix_weights(ls):
    m = jnp.maximum(jnp.maximum(ls[0], ls[1]), ls[2])
    es = [jnp.exp(l - m) for l in ls]
    den = es[0] + es[1] + es[2]
    return [e / den for e in es]


def mix_fwd(os_, ls, y_mem):
    T = y_mem.shape[0]

    def fn(i, nblk, o0, o1, o2, l0, l1, l2, ym):
        w = _mix_weights((l0, l1, l2))
        return jnp.concatenate([w[0] * o0 + w[1] * o1 + w[2] * o2, ym], axis=1)

    return _rowmap(fn, name="mix_fwd", T=T, tb=512, ins=[("row", z) for z in (*os_, *ls, y_mem)],
                   outs=[("row", 2 * MEM_WIDTH, MMD)])[0]


def mix_bwd(os_, ls, dycat):
    T = dycat.shape[0]

    def fn(i, nblk, o0, o1, o2, l0, l1, l2, dy):
        w = _mix_weights((l0, l1, l2))
        os3 = (o0, o1, o2)
        dws = [dy * o for o in os3]
        tot = w[0] * dws[0] + w[1] * dws[1] + w[2] * dws[2]
        return tuple(wg * dy for wg in w) + tuple(wg * (dw - tot) for wg, dw in zip(w, dws))

    return _rowmap(fn, name="mix_bwd", T=T, tb=512,
                   ins=[("row", z) for z in (*os_, *ls)] + [("row", dycat, MEM_WIDTH, 0)],
                   outs=[("row", MEM_WIDTH, F32)] * 6)


def loss_fwd_bwd(y, target):
    T, D = y.shape

    def fn(i, nblk, y, t):
        e = y - t
        return e * (1.0 / D), jnp.zeros((8, 128), F32) + jnp.sum(e * e) * (0.5 / D)

    return _rowmap(fn, name="loss", T=T, tb=512, ins=[("row", y), ("row", target)], outs=[("row", D, F32)],
                   accs=[((8, 128), F32)])


def _to_residues(z, dil):
    T, W = z.shape
    return z.reshape(T // dil, dil, W).transpose(1, 0, 2).reshape(T, W)


def _from_residues(z, dil):
    T, W = z.shape
    return z.reshape(dil, T // dil, W).transpose(1, 0, 2).reshape(T, W)


def _pad_rows(w, rows):
    return jnp.concatenate([w, jnp.zeros((rows - w.shape[0], w.shape[1]), w.dtype)], axis=0)


def _tile_heads(g, width):
    return jnp.tile(g.reshape(1, HEAD_DIM), (1, width // HEAD_DIM))


def _conv_rows(W, i):
    return [W["ffn_conv_w"][i][j:j + 1] for j in range(3)]


def _ffn_fwd(x, i, W):
    hn = rms_fwd(x, [W["ffn_norm"][i:i + 1]], name=f"ffn_rms{i}")[0]
    u = _mm(hn, W["ffn_w_up"][i], name=f"ffn_up{i}")
    z = ffn_act_fwd(u, _conv_rows(W, i), W["ffn_conv_b"][i:i + 1], name=f"ffn_act{i}")
    out = _mm(z, W["ffn_w_down"][i], add=x, name=f"ffn_down{i}")
    return out, (x, hn, u, z)


def _ffn_bwd(dout, i, W, saved, G):
    x, hn, u, z = saved
    dz = _mm(dout, W["ffn_w_down"][i], tb=True, name=f"ffn_down_dx{i}")
    G["ffn_w_down"][i] = _mm(z, dout, ta=True, name=f"ffn_down_dw{i}")
    dc, dw0, dw1, dw2, db = ffn_act_bwd(u, _conv_rows(W, i), W["ffn_conv_b"][i:i + 1], dz, name=f"ffn_act_bwd{i}")
    G["ffn_conv_w"][i] = jnp.concatenate([dw0, dw1, dw2], axis=0)
    G["ffn_conv_b"][i] = db[0]
    du = conv_bwd(dc, _conv_rows(W, i), name=f"ffn_conv_bwd{i}")
    dhn = _mm(du, W["ffn_w_up"][i], tb=True, name=f"ffn_up_dx{i}")
    G["ffn_w_up"][i] = _mm(hn, du, ta=True, name=f"ffn_up_dw{i}")
    dx, dg = rms_bwd(x, [W["ffn_norm"][i:i + 1]], [dhn], dout, name=f"ffn_rms_bwd{i}")
    G["ffn_norm"][i] = dg[0]
    return dx


def local_step(x, mem, target, W, late=None):
    T = x.shape[0]
    W = dict(W)
    G = {"ffn_w_down": [None, None], "ffn_w_up": [None, None], "ffn_conv_w": [None, None],
         "ffn_conv_b": [None, None], "ffn_norm": [None, None], "attn_norm": [None, None], "mem_norm": [None, None],
         "mem_w_kv": [None, None], "mem_q_norm": [None, None], "mem_k_norm": [None, None]}
    mu, w0, a0 = W["a_mu"], W["a_w0"], W["a_a0"]
    w2p, a2p, g2p = (_pad_rows(W["a_w2"][0], LORA_WIDTH),
                     jnp.concatenate([jnp.zeros((64, RWKV_WIDTH), MMD), W["a_a2"][0],
                                      jnp.zeros((128, RWKV_WIDTH), MMD)], axis=0),
                     jnp.concatenate([jnp.zeros((128, RWKV_WIDTH), MMD), W["a_g2"][0]], axis=0))
    k_k, k_a, lnx_w, lnx_b = W["a_k_k"], W["a_k_a"], W["a_lnx_w"], W["a_lnx_b"]
    r_k = W["a_r_k"].reshape(1, RWKV_WIDTH)

    h0 = rms_fwd(x, [W["attn_norm"][0:1]], name="attn_rms0")[0]
    p = _mm(h0, W["a_w_in"][0], name="a_in")
    r, w, k2, v, kk, kka, g = rwkv_pre_fwd(p, mu, w0, a0, w2p, a2p, g2p, k_k, k_a)
    if late is None:
        y, states, final_state = scan_fwd(r, w, k2, v, kk, kka)
    else:
        y, states, final_state, gathered = scan_fwd(r, w, k2, v, kk, kka, gather=late[0])
        W.update(late[1](gathered))
    memkv = []
    for i in range(2):
        memkv.append(mem_kv_fwd(mem, W["mem_norm"][i:i + 1], W["mem_w_kv"][i], _tile_heads(W["mem_k_norm"][i], MEM_WIDTH),
                                name=f"mem_kv{i}"))
    qg0 = _tile_heads(W["mem_q_norm"][0], MEM_WIDTH)
    y_mem0 = mem_attn_fwd(p, SHIFT_WIDTH // MEM_WIDTH, memkv[0][0], memkv[0][1], qg0, name="mem_attn0")
    ycat0 = mix_gate_fwd(y, r, k2, v, g, y_mem0, lnx_w, lnx_b, r_k)
    x1 = _mm(ycat0, W["a_w_out"][0], add=x, name="a_out")
    x2, ffn0 = _ffn_fwd(x1, 0, W)

    h1, hkv = rms_fwd(x2, [W["attn_norm"][1:2], W["kv_norm"].reshape(1, -1)], name="attn_rms1")
    kvp = _mm(hkv, W["kv_w"], name="kv_in")
    pb = _mm(h1, W["b_w_in"][0], name="b_in")
    cos, sin = rope_tables(T)
    kg_t, qg_t = _tile_heads(W["kv_k_norm"], DIL_WIDTH), _tile_heads(W["b_q_norm"][0], DIL_WIDTH)
    q, ksh, vsh = qk_fwd(kvp, pb, kg_t, qg_t, cos, sin)
    os_, ls, grp = [], [], []
    for gi, (win, dil) in enumerate(DIL_GROUPS):
        sl = slice(gi * MEM_WIDTH, (gi + 1) * MEM_WIDTH)
        qg_, kg_, vg_ = (_to_residues(z[:, sl], dil) for z in (q, ksh, vsh))
        o_r, l_r = dil_attn_fwd(qg_, kg_, vg_, T // dil // DIL_BLOCK, name=f"dil_fwd{gi}")
        grp.append((qg_, kg_, vg_, o_r, l_r))
        os_.append(_from_residues(o_r, dil))
        ls.append(_from_residues(l_r, dil))
    qg1 = _tile_heads(W["mem_q_norm"][1], MEM_WIDTH)
    y_mem1 = mem_attn_fwd(pb, DIL_WIDTH // MEM_WIDTH, memkv[1][0], memkv[1][1], qg1, name="mem_attn1")
    ycat1 = mix_fwd(os_, ls, y_mem1)
    x3 = _mm(ycat1, W["b_w_out"][0], add=x2, name="b_out")
    x4, ffn1 = _ffn_fwd(x3, 1, W)

    dx4, loss = loss_fwd_bwd(x4, target)

    dx3 = _ffn_bwd(dx4, 1, W, ffn1, G)
    dycat1 = _mm(dx3, W["b_w_out"][0], tb=True, name="b_out_dx")
    G["b_w_out"] = _mm(ycat1, dx3, ta=True, name="b_out_dw")[None]
    dq_mem1, dkn1, dvm1, dqg1 = mem_attn_bwd(pb, DIL_WIDTH // MEM_WIDTH, memkv[1][0], memkv[1][1], qg1, dycat1, 1,
                                             name="mem_attn_bwd1")
    G["mem_q_norm"][1] = dqg1[0]
    d_os_ls = mix_bwd(os_, ls, dycat1)
    dqs, dks, dvs = [], [], []
    for gi, (win, dil) in enumerate(DIL_GROUPS):
        qg_, kg_, vg_, o_r, l_r = grp[gi]
        do_r, dl_r = _to_residues(d_os_ls[gi], dil), _to_residues(d_os_ls[3 + gi], dil)
        dq_r, dk_r, dv_r = dil_attn_bwd(qg_, kg_, vg_, o_r, l_r, do_r, dl_r, T // dil // DIL_BLOCK, name=f"dil_bwd{gi}")
        dqs.append(_from_residues(dq_r, dil))
        dks.append(_from_residues(dk_r, dil))
        dvs.append(_from_residues(dv_r, dil))
    dq, dk, dv = (jnp.concatenate(z, axis=1) for z in (dqs, dks, dvs))
    dpb, dkvp, dqn_g, dkn_g = qk_bwd(kvp, pb, kg_t, qg_t, cos, sin, dq, dk, dv, dq_mem1)
    G["b_q_norm"] = dqn_g
    G["kv_k_norm"] = dkn_g[0]
    dh1 = _mm(dpb, W["b_w_in"][0], tb=True, name="b_in_dx")
    G["b_w_in"] = _mm(h1, dpb, ta=True, name="b_in_dw")[None]
    dhkv = _mm(dkvp, W["kv_w"], tb=True, name="kv_in_dx")
    G["kv_w"] = _mm(hkv, dkvp, ta=True, name="kv_in_dw")
    dx2, dg1, dgkv = rms_bwd(x2, [W["attn_norm"][1:2], W["kv_norm"].reshape(1, -1)], [dh1, dhkv], dx3,
                             name="attn_rms_bwd1")
    G["attn_norm"][1] = dg1[0]
    G["kv_norm"] = dgkv[0]

    dx1 = _ffn_bwd(dx2, 0, W, ffn0, G)
    dycat0 = _mm(dx1, W["a_w_out"][0], tb=True, name="a_out_dx")
    G["a_w_out"] = _mm(ycat0, dx1, ta=True, name="a_out_dw")[None]
    dq_mem0, dkn0, dvm0, dqg0 = mem_attn_bwd(p, SHIFT_WIDTH // MEM_WIDTH, memkv[0][0], memkv[0][1], qg0, dycat0,
                                             RWKV_WIDTH // MEM_WIDTH, name="mem_attn_bwd0")
    G["mem_q_norm"][0] = dqg0[0]
    dy, dr_b, dk2_b, dv_b, dg, dlw, dlb, drk = mix_gate_bwd(y, r, k2, v, g, dycat0, lnx_w, lnx_b, r_k)
    for i, (dkn, dvm) in enumerate(((dkn0, dvm0), (dkn1, dvm1))):
        dwkv, dgm, dkg = mem_kv_bwd(mem, W["mem_norm"][i:i + 1], W["mem_w_kv"][i],
                                    _tile_heads(W["mem_k_norm"][i], MEM_WIDTH), dkn, dvm, name=f"mem_kv_bwd{i}")
        G["mem_w_kv"][i], G["mem_norm"][i], G["mem_k_norm"][i] = dwkv, dgm[0], dkg[0]
    for n in list(G):
        if isinstance(G[n], list) and all(z is not None for z in G[n]):
            G[n] = jnp.stack(G[n], axis=0)
    late_out = None
    if late is None:
        dr, dw, dk2, dv, dkk, dkka = scan_bwd(r, w, k2, v, kk, kka, states, final_state, dy)
    else:
        pieces = late[2](G)
        dr, dw, dk2, dv, dkk, dkka, received = scan_bwd(r, w, k2, v, kk, kka, states, final_state, dy, scatter=pieces)
        late_out = (received, pieces)
    dxs, dmu, dw0, da0, dw2p, da2p, dg2p, dk_k, dk_a = rwkv_pre_bwd(
        p, mu, w0, a0, w2p, a2p, g2p, k_k, k_a, (dr, dr_b), dw, (dk2, dk2_b), (dv, dv_b), dkk, dkka, dg)
    dp = shift_bwd(dxs, mu, dq_mem0)
    G.update(a_mu=dmu, a_w0=dw0, a_a0=da0, a_w2=dw2p[None, :64], a_a2=da2p[None, 64:128], a_g2=dg2p[None, 128:],
             a_k_k=dk_k, a_k_a=dk_a, a_r_k=drk.reshape(1, RWKV_HEADS, HEAD_DIM), a_lnx_w=dlw, a_lnx_b=dlb)
    dh0 = _mm(dp, W["a_w_in"][0], tb=True, name="a_in_dx")
    G["a_w_in"] = _mm(h0, dp, ta=True, name="a_in_dw")[None]
    grad_x, dg0 = rms_bwd(x, [W["attn_norm"][0:1]], [dh0], dx1, name="attn_rms_bwd0")
    G["attn_norm"][0] = dg0[0]
    G["attn_norm"] = jnp.stack(G["attn_norm"], axis=0)
    return loss, grad_x, G, late_out


HBM_SPEC = pl.BlockSpec(memory_space=pltpu.HBM)


def _mesh_pos():
    return lax.axis_index("x"), lax.axis_index("y"), lax.axis_index("c")


def _other_chips(x, y):
    return [(1 - x, y), (x, 1 - y), (1 - x, 1 - y)]


def _remote(send_sems, recv_sems, k, src, dst, to):
    return pltpu.make_async_remote_copy(src_ref=src, dst_ref=dst, send_sem=send_sems.at[k], recv_sem=recv_sems.at[k],
                                        device_id=to, device_id_type=MESH)


def _comm_call(body, name, ins, out_shape, n_remote):
    scratch = [pltpu.SemaphoreType.DMA((n_remote,)), pltpu.SemaphoreType.DMA((n_remote,))]
    return pl.pallas_call(body, name=name, in_specs=[HBM_SPEC] * len(ins), out_specs=[HBM_SPEC] * len(out_shape),
                          out_shape=out_shape, scratch_shapes=scratch)(*ins)


def comm_gather(wbig, wsm):
    def body(wb, ws, ob, os_, send_sems, recv_sems):
        x, y, c = _mesh_pos()
        s = 2 * x + y
        me, sibling = (x, y, c), (x, y, 1 - c)
        chips = _other_chips(x, y)
        rc = functools.partial(_remote, send_sems, recv_sems)
        first = []
        for j, (cx, cy) in enumerate(chips):
            first.append(rc(j, wb.at[c], ob.at[s, c], (cx, cy, c)))
            first.append(rc(6 + j, ws, os_.at[s], (cx, cy, c)))
        for cp in first:
            cp.start()
        passed = []
        for j, (cx, cy) in enumerate(chips):
            blk = ob.at[2 * cx + cy, c]
            rc(j, blk, blk, me).wait_recv()
            passed.append(rc(3 + j, blk, blk, sibling))
            passed[-1].start()
        for j, (cx, cy) in enumerate(chips):
            blk = ob.at[2 * cx + cy, 1 - c]
            rc(3 + j, blk, blk, me).wait_recv()
            sb = os_.at[2 * cx + cy]
            rc(6 + j, sb, sb, me).wait_recv()
        for cp in first + passed:
            cp.wait_send()

    out_shape = [jax.ShapeDtypeStruct((N_CHIPS, *wbig.shape), wbig.dtype),
                 jax.ShapeDtypeStruct((N_CHIPS, *wsm.shape), wsm.dtype)]
    return _comm_call(body, "comm_gather", [wbig, wsm], out_shape, 9)


def comm_pair_exchange(gb, gs):
    def body(gb_ref, gs_ref, rb_ref, rs_ref, send_sems, recv_sems):
        x, y, c = _mesh_pos()
        sibling = (x, y, 1 - c)
        rc = functools.partial(_remote, send_sems, recv_sems)
        cps = [rc(r, gb_ref.at[r, 1 - c], rb_ref.at[r], sibling) for r in range(N_CHIPS)]
        cps.append(rc(N_CHIPS, gs_ref.at[1 - c], rs_ref, sibling))
        for cp in cps:
            cp.start()
        for cp in cps:
            cp.wait()

    out_shape = [jax.ShapeDtypeStruct((N_CHIPS, *gb.shape[2:]), gb.dtype), jax.ShapeDtypeStruct(gs.shape[1:], gs.dtype)]
    return _comm_call(body, "comm_pair_exchange", [gb, gs], out_shape, N_CHIPS + 1)


def comm_chip_exchange(hb, hs):
    def body(hb_ref, hs_ref, qb_ref, qs_ref, send_sems, recv_sems):
        x, y, c = _mesh_pos()
        s = 2 * x + y
        me = (x, y, c)
        chips = _other_chips(x, y)
        rc = functools.partial(_remote, send_sems, recv_sems)
        cps = []
        for j, (cx, cy) in enumerate(chips):
            cps.append(rc(j, hb_ref.at[2 * cx + cy], qb_ref.at[s], (cx, cy, c)))
            cps.append(rc(3 + j, hs_ref, qs_ref.at[s], (cx, cy, c)))
        for cp in cps:
            cp.start()
        for j, (cx, cy) in enumerate(chips):
            blk = qb_ref.at[2 * cx + cy]
            rc(j, blk, blk, me).wait_recv()
            sb = qs_ref.at[2 * cx + cy]
            rc(3 + j, sb, sb, me).wait_recv()
        for cp in cps:
            cp.wait_send()

    out_shape = [jax.ShapeDtypeStruct(hb.shape, hb.dtype), jax.ShapeDtypeStruct((N_CHIPS, *hs.shape), hs.dtype)]
    return _comm_call(body, "comm_chip_exchange", [hb, hs], out_shape, 6)


def comm_pair_share(halves):
    n = len(halves)

    def body(*refs):
        x, y, c = _mesh_pos()
        send_sems, recv_sems = refs[2 * n], refs[2 * n + 1]
        cps = [_remote(send_sems, recv_sems, k, refs[k], refs[n + k], (x, y, 1 - c)) for k in range(n)]
        for cp in cps:
            cp.start()
        for cp in cps:
            cp.wait()

    out_shape = [jax.ShapeDtypeStruct(h.shape, h.dtype) for h in halves]
    return _comm_call(body, "comm_pair_share", list(halves), out_shape, n)


def add_pairs(a, b, out_dtype, *, name, tb):
    T, L = a.shape
    return _rowmap(lambda i, n, p, q: p + q, name=name, T=T, tb=tb, ins=[("row", a), ("row", b)],
                   outs=[("row", L, out_dtype)])[0]


def add_chips(parts, *, name, tb):
    T, L = parts[0].shape

    def fn(i, n, *ps):
        acc = ps[0].astype(F32)
        for p in ps[1:]:
            acc = acc + p.astype(F32)
        return acc

    return _rowmap(fn, name=name, T=T, tb=tb, ins=[("row", p) for p in parts], outs=[("row", L, F32)])[0]


def adamw(g, w, m, v, *, name, tb):
    T, L = g.shape

    def fn(i, n, g, w, m, v):
        m2 = ADAM_B1 * m + (1.0 - ADAM_B1) * g
        v2 = ADAM_B2 * v + (1.0 - ADAM_B2) * (g * g)
        m_hat = m2 / (1.0 - ADAM_B1 ** ADAM_STEP)
        v_hat = v2 / (1.0 - ADAM_B2 ** ADAM_STEP)
        return -ADAM_LR * (m_hat / (jnp.sqrt(v_hat) + ADAM_EPS) + ADAM_WD * w), m2, v2

    return _rowmap(fn, name=name, T=T, tb=tb, ins=[("row", z) for z in (g, w, m, v)], outs=[("row", L, F32)] * 3)


BIG_LANES = 1024
SMALL_LANES = 128


def _flat_cat(arrs, total, dtype):
    parts = [a.reshape(-1).astype(dtype) for a in arrs]
    n = sum(p.shape[0] for p in parts)
    assert n <= total, (n, total)
    if n < total:
        parts.append(jnp.zeros((total - n,), dtype))
    return jnp.concatenate(parts)


def _split_flat(flat, shapes):
    out, off = [], 0
    for shp in shapes:
        n = math.prod(shp)
        out.append(flat[off:off + n].reshape(shp))
        off += n
    return out


def _round_up(n, m):
    return -(-n // m) * m


def _full_shape(shard_shape, axis):
    return tuple(d * N_CHIPS if i == axis else d for i, d in enumerate(shard_shape))


def kernel(x, mem, attn_norm, a_w_in, a_mu, a_w0, a_w2, a_a0, a_a2, a_g2, a_k_k, a_k_a, a_r_k, a_lnx_w, a_lnx_b, a_w_out, kv_norm, kv_w, kv_k_norm, b_w_in, b_q_norm, b_w_out, mem_norm, mem_w_kv, mem_q_norm, mem_k_norm, ffn_norm, ffn_w_up, ffn_conv_w, ffn_conv_b, ffn_w_down, loss_target, m_attn_norm, m_a_w_in, m_a_mu, m_a_w0, m_a_w2, m_a_a0, m_a_a2, m_a_g2, m_a_k_k, m_a_k_a, m_a_r_k, m_a_lnx_w, m_a_lnx_b, m_a_w_out, m_kv_norm, m_kv_w, m_kv_k_norm, m_b_w_in, m_b_q_norm, m_b_w_out, m_mem_norm, m_mem_w_kv, m_mem_q_norm, m_mem_k_norm, m_ffn_norm, m_ffn_w_up, m_ffn_conv_w, m_ffn_conv_b, m_ffn_w_down, v_attn_norm, v_a_w_in, v_a_mu, v_a_w0, v_a_w2, v_a_a0, v_a_a2, v_a_g2, v_a_k_k, v_a_k_a, v_a_r_k, v_a_lnx_w, v_a_lnx_b, v_a_w_out, v_kv_norm, v_kv_w, v_kv_k_norm, v_b_w_in, v_b_q_norm, v_b_w_out, v_mem_norm, v_mem_w_kv, v_mem_q_norm, v_mem_k_norm, v_ffn_norm, v_ffn_w_up, v_ffn_conv_w, v_ffn_conv_b, v_ffn_w_down):
    args = (attn_norm, a_w_in, a_mu, a_w0, a_w2, a_a0, a_a2, a_g2, a_k_k, a_k_a, a_r_k, a_lnx_w, a_lnx_b, a_w_out, kv_norm, kv_w, kv_k_norm, b_w_in, b_q_norm, b_w_out, mem_norm, mem_w_kv, mem_q_norm, mem_k_norm, ffn_norm, ffn_w_up, ffn_conv_w, ffn_conv_b, ffn_w_down)
    ms = (m_attn_norm, m_a_w_in, m_a_mu, m_a_w0, m_a_w2, m_a_a0, m_a_a2, m_a_g2, m_a_k_k, m_a_k_a, m_a_r_k, m_a_lnx_w, m_a_lnx_b, m_a_w_out, m_kv_norm, m_kv_w, m_kv_k_norm, m_b_w_in, m_b_q_norm, m_b_w_out, m_mem_norm, m_mem_w_kv, m_mem_q_norm, m_mem_k_norm, m_ffn_norm, m_ffn_w_up, m_ffn_conv_w, m_ffn_conv_b, m_ffn_w_down)
    vs = (v_attn_norm, v_a_w_in, v_a_mu, v_a_w0, v_a_w2, v_a_a0, v_a_a2, v_a_g2, v_a_k_k, v_a_k_a, v_a_r_k, v_a_lnx_w, v_a_lnx_b, v_a_w_out, v_kv_norm, v_kv_w, v_kv_k_norm, v_b_w_in, v_b_q_norm, v_b_w_out, v_mem_norm, v_mem_w_kv, v_mem_q_norm, v_mem_k_norm, v_ffn_norm, v_ffn_w_up, v_ffn_conv_w, v_ffn_conv_b, v_ffn_w_down)
    w_sh, m_sh, v_sh = (dict(zip(WEIGHTS, z)) for z in (args, ms, vs))
    xi, yi, ci = _mesh_pos()
    chip = 2 * xi + yi
    axes = {**dict(BIG), **dict(SMALL_SHARDED)}
    early_names = [n for n, _ in BIG if n in EARLY_BIG]
    late_names = [n for n, _ in BIG if n not in EARLY_BIG]
    ss_names, ss_axes = [n for n, _ in SMALL_SHARDED], dict(SMALL_SHARDED)
    shapes_of = lambda names: [w_sh[n].shape for n in names]
    count = lambda names: sum(math.prod(s) for s in shapes_of(names))
    n_early, n_late = count(early_names), count(late_names)
    assert n_early % (2 * 16 * BIG_LANES) == 0 and n_late % (2 * 16 * BIG_LANES) == 0
    mh, mh_late = n_early // (2 * BIG_LANES), n_late // (2 * BIG_LANES)
    n_ss = _round_up(count(ss_names), 8 * SMALL_LANES)

    def shard_pack(names, total, dtype, source):
        return _flat_cat([source[n] for n in names], total, dtype)

    def unshard(names, gathered):
        per_chip = [_split_flat(gathered[j], shapes_of(names)) for j in range(N_CHIPS)]
        return {n: jnp.concatenate([per_chip[j][k] for j in range(N_CHIPS)], axis=axes[n]) for k, n in enumerate(names)}

    def by_chip(names, total, dtype, grads):
        parts = [jnp.split(grads[n], N_CHIPS, axis=axes[n]) for n in names]
        return jnp.stack([_flat_cat([p[j] for p in parts], total, dtype) for j in range(N_CHIPS)])

    wbig = shard_pack(early_names, n_early, MMD, w_sh).reshape(2, mh, BIG_LANES)
    wsm = shard_pack(ss_names, n_ss, F32, w_sh).reshape(-1, SMALL_LANES)
    wbig_all, wsm_all = comm_gather(wbig, wsm)
    wbig_all = lax.dynamic_update_index_in_dim(wbig_all, wbig, chip, 0).reshape(N_CHIPS, -1)
    wsm_all = lax.dynamic_update_index_in_dim(wsm_all, wsm, chip, 0).reshape(N_CHIPS, -1)
    W = {n: w_sh[n] for n in SMALL_REPL}
    W.update(unshard(early_names, wbig_all))
    W.update(unshard(ss_names, wsm_all))
    for n in ("a_w2", "a_a2", "a_g2"):
        W[n] = W[n].astype(MMD)
    wlate = shard_pack(late_names, n_late, MMD, w_sh).reshape(2, mh_late, BIG_LANES)

    def unpack_late(gathered):
        full = lax.dynamic_update_index_in_dim(gathered, wlate, chip, 0)
        return unshard(late_names, full.reshape(N_CHIPS, -1))

    def pack_late(grads):
        return by_chip(late_names, n_late, BF16, grads).reshape(N_CHIPS, 2, mh_late, BIG_LANES)

    loss_blk, grad_x, G, (received, pieces) = local_step(x[0], mem[0], loss_target[0], W,
                                                          late=(wlate, unpack_late, pack_late))
    loss = lax.psum(loss_blk[0, 0], ("x", "y", "c"))

    mine = lax.dynamic_index_in_dim(lax.dynamic_index_in_dim(pieces, chip, 0, keepdims=False), ci, 0, keepdims=False)
    gh_late = add_chips([received[k] for k in range(len(PEER_FLIPS))] + [mine], name="add_pieces_late", tb=32)
    gbig = by_chip(early_names, n_early, F32, G).reshape(N_CHIPS, 2, mh, BIG_LANES)
    sm_full_names = ss_names + list(SMALL_REPL)
    sm_full_shapes = [_full_shape(w_sh[n].shape, ss_axes[n]) for n in ss_names] + [w_sh[n].shape for n in SMALL_REPL]
    n_smf = _round_up(sum(math.prod(s) for s in sm_full_shapes), 2 * 8 * SMALL_LANES)
    msh = n_smf // (2 * SMALL_LANES)
    gsm = _flat_cat([G[n] for n in sm_full_names], n_smf, F32).reshape(2, msh, SMALL_LANES)
    rb, rs = comm_pair_exchange(gbig, gsm)
    mine_b = lax.dynamic_index_in_dim(gbig, ci, axis=1, keepdims=False)
    mine_s = lax.dynamic_index_in_dim(gsm, ci, axis=0, keepdims=False)
    hb = add_pairs(mine_b.reshape(-1, BIG_LANES), rb.reshape(-1, BIG_LANES), BF16, name="add_pairs_big", tb=128)
    hs = add_pairs(mine_s, rs, F32, name="add_pairs_small", tb=msh)
    hb = hb.reshape(N_CHIPS, mh, BIG_LANES)
    qb, qs = comm_chip_exchange(hb, hs)
    qb = lax.dynamic_update_index_in_dim(qb, lax.dynamic_index_in_dim(hb, chip, 0, keepdims=False), chip, 0)
    qs = lax.dynamic_update_index_in_dim(qs, hs, chip, 0)
    gh = add_chips([qb[j] for j in range(N_CHIPS)], name="add_chips_big", tb=32)
    gsh = add_chips([qs[j] for j in range(N_CHIPS)], name="add_chips_small", tb=msh)
    rh, rh_late, rsh = comm_pair_share([gh, gh_late, gsh])
    both = lambda mine_, theirs: jnp.where(ci == 0, jnp.stack([mine_, theirs]), jnp.stack([theirs, mine_]))
    gfull, gfull_late, gsfull = both(gh, rh), both(gh_late, rh_late), both(gsh, rsh)

    res = {tag: {} for tag in ("grad", "delta", "new_m", "new_v")}
    big_grads = (list(zip(early_names, _split_flat(gfull.reshape(-1), shapes_of(early_names))))
                 + list(zip(late_names, _split_flat(gfull_late.reshape(-1), shapes_of(late_names)))))
    for n, g in big_grads:
        shp = w_sh[n].shape
        rows = lambda z: z.reshape(-1, shp[-1])
        nrow = math.prod(shp[:-1])
        tb = next(t for t in (512, 256, 128, 64) if nrow % t == 0 and t * shp[-1] <= (1 << 19))
        outs = adamw(rows(g), rows(w_sh[n]), rows(m_sh[n]), rows(v_sh[n]), name=f"adamw_{n}", tb=tb)
        res["grad"][n] = g
        for tag, o in zip(("delta", "new_m", "new_v"), outs):
            res[tag][n] = o.reshape(shp)
    sm_full = dict(zip(sm_full_names, _split_flat(gsfull.reshape(-1), sm_full_shapes)))
    g_loc = {}
    for n in ss_names:
        size = w_sh[n].shape[ss_axes[n]]
        g_loc[n] = lax.dynamic_slice_in_dim(sm_full[n], chip * size, size, axis=ss_axes[n])
    for n in SMALL_REPL:
        g_loc[n] = sm_full[n]
    n_sml = _round_up(sum(math.prod(w_sh[n].shape) for n in sm_full_names), 8 * SMALL_LANES)
    pack_sm = lambda d: _flat_cat([d[n] for n in sm_full_names], n_sml, F32).reshape(-1, SMALL_LANES)
    d_sm, m_sm, v_sm = adamw(pack_sm(g_loc), pack_sm(w_sh), pack_sm(m_sh), pack_sm(v_sh), name="adamw_small",
                             tb=n_sml // SMALL_LANES)
    sm_loc_shapes = [w_sh[n].shape for n in sm_full_names]
    res["grad"].update(g_loc)
    for tag, smv in (("delta", d_sm), ("new_m", m_sm), ("new_v", v_sm)):
        res[tag].update(dict(zip(sm_full_names, _split_flat(smv.reshape(-1), sm_loc_shapes))))
    return (loss, grad_x[None], *[res[tag][n] for tag in ("grad", "delta", "new_m", "new_v") for n in WEIGHTS])
```

```python
import functools
import math

import numpy as np
import jax
import jax.numpy as jnp
from jax import lax
from jax.experimental import pallas as pl
from jax.experimental.pallas import tpu as pltpu

F32 = jnp.float32
BF16 = jnp.bfloat16
MMD = jnp.bfloat16

D_MODEL = 1024
HEAD_DIM = 64
N_MEM = 256
MEM_WIDTH = 256
RWKV_HEADS = 12
RWKV_WIDTH = 768
SHIFT_WIDTH = 2560
LORA_WIDTH = 256
DIL_WIDTH = 768
DIL_GROUPS = ((128, 1), (512, 4), (2048, 16))
DIL_BLOCK = 128
D_FF = 2816
ROPE_THETA = 10000.0
RMS_EPS = 1e-6
LNX_EPS = 64e-5
NEG_INF = -1e30
ADAM_LR = 0.001
ADAM_B1 = 0.9
ADAM_B2 = 0.999
ADAM_EPS = 1e-08
ADAM_WD = 0.01
ADAM_STEP = 10
N_CHIPS = 4
MESH = pl.DeviceIdType.MESH
VMEM_LIMIT_MB = 56
SCAN_CHUNK = 16
SCAN_UNROLL = 4
SCAN_UNROLL_BWD = 2

BIG = (("a_w_in", 2), ("a_w_out", 1), ("kv_w", 1), ("b_w_in", 1), ("b_w_out", 2), ("mem_w_kv", 1),
       ("ffn_w_up", 2), ("ffn_w_down", 1))
EARLY_BIG = ("a_w_in",)
NATURAL_BIG = "ffn_w_up"
SMALL_SHARDED = (("a_mu", 1), ("a_w0", 1), ("a_w2", 2), ("a_a0", 1), ("a_a2", 2), ("a_g2", 2), ("a_k_k", 1),
                 ("a_k_a", 1), ("a_lnx_w", 1), ("a_lnx_b", 1), ("ffn_conv_w", 2))
SMALL_REPL = ("attn_norm", "a_r_k", "kv_norm", "kv_k_norm", "b_q_norm", "mem_norm", "mem_q_norm", "mem_k_norm",
              "ffn_norm", "ffn_conv_b")
WEIGHTS = ("attn_norm", "a_w_in", "a_mu", "a_w0", "a_w2", "a_a0", "a_a2", "a_g2", "a_k_k", "a_k_a", "a_r_k",
           "a_lnx_w", "a_lnx_b", "a_w_out", "kv_norm", "kv_w", "kv_k_norm", "b_w_in", "b_q_norm", "b_w_out",
           "mem_norm", "mem_w_kv", "mem_q_norm", "mem_k_norm", "ffn_norm", "ffn_w_up", "ffn_conv_w", "ffn_conv_b",
           "ffn_w_down")


def _cp(sem=None, **kw):
    return pltpu.CompilerParams(dimension_semantics=sem, vmem_limit_bytes=VMEM_LIMIT_MB << 20, **kw)


def _tile(n, cands=(512, 256, 128)):
    for c in cands:
        if n % c == 0:
            return c
    return n


def _mm(a, b, *, name, ta=False, tb=False, add=None, out_dtype=F32):
    K, M = a.shape if ta else a.shape[::-1]
    N = b.shape[0] if tb else b.shape[1]
    assert K == (b.shape[1] if tb else b.shape[0])
    tm, tn = _tile(M, (512, 256, 128) if ta else (1024, 512, 256, 128)), _tile(N)
    a_spec = pl.BlockSpec((K, tm), lambda i, j: (0, i)) if ta else pl.BlockSpec((tm, K), lambda i, j: (i, 0))
    b_spec = pl.BlockSpec((tn, K), lambda i, j: (j, 0)) if tb else pl.BlockSpec((K, tn), lambda i, j: (0, j))
    o_spec = pl.BlockSpec((tm, tn), lambda i, j: (i, j))
    dn = (((0,) if ta else (1,), (1,) if tb else (0,)), ((), ()))
    has_add = add is not None

    def body(*refs):
        a_ref, b_ref = refs[0], refs[1]
        o_ref = refs[-1]
        acc = lax.dot_general(a_ref[...].astype(MMD), b_ref[...].astype(MMD), dn, preferred_element_type=F32)
        if has_add:
            acc = acc + refs[2][...]
        o_ref[...] = acc.astype(o_ref.dtype)

    ins = [a, b] + ([add] if has_add else [])
    specs = [a_spec, b_spec] + ([o_spec] if has_add else [])
    return pl.pallas_call(
        body, name=name, grid=(M // tm, N // tn), in_specs=specs, out_specs=o_spec,
        out_shape=jax.ShapeDtypeStruct((M, N), out_dtype), compiler_params=_cp(("parallel", "parallel")),
    )(*ins)


def _rowmap(fn, *, name, T, tb, ins, outs, accs=()):
    nblk = T // tb
    assert T % tb == 0 and tb % 8 == 0
    in_specs, args = [], []
    for spec in ins:
        kind, arr = spec[0], spec[1]
        w, cb = (spec[2], spec[3]) if len(spec) > 2 else (arr.shape[-1], 0)
        if kind == "row":
            in_specs.append(pl.BlockSpec((tb, w), lambda i, cb=cb: (i, cb)))
        elif kind == "prev":
            in_specs.append(pl.BlockSpec((8, w), lambda i, cb=cb: (jnp.maximum(i * (tb // 8) - 1, 0), cb)))
        elif kind == "next":
            in_specs.append(pl.BlockSpec((8, w), lambda i, cb=cb: (jnp.minimum((i + 1) * (tb // 8), T // 8 - 1), cb)))
        elif kind == "const":
            in_specs.append(pl.BlockSpec(arr.shape, lambda i, nd=arr.ndim: (0,) * nd))
        else:
            raise ValueError(kind)
        args.append(arr)
    out_shape, out_specs = [], []
    for kind, w, dt in outs:
        out_shape.append(jax.ShapeDtypeStruct((T, w), dt))
        out_specs.append(pl.BlockSpec((tb, w), lambda i: (i, 0)))
    for shp, dt in accs:
        out_shape.append(jax.ShapeDtypeStruct(shp, dt))
        out_specs.append(pl.BlockSpec(shp, lambda i, nd=len(shp): (0,) * nd))
    n_in, n_out = len(ins), len(outs)

    def body(*refs):
        i = pl.program_id(0)
        vals = [r[...] for r in refs[:n_in]]
        res = fn(i, nblk, *vals)
        if not isinstance(res, (tuple, list)):
            res = (res,)
        assert len(res) == n_out + len(accs), (name, len(res))
        for r, v in zip(refs[n_in:n_in + n_out], res[:n_out]):
            r[...] = v.astype(r.dtype)
        acc_refs = refs[n_in + n_out:]
        if acc_refs:
            @pl.when(i == 0)
            def _():
                for r in acc_refs:
                    r[...] = jnp.zeros(r.shape, r.dtype)

            for r, v in zip(acc_refs, res[n_out:]):
                r[...] += v

    res = pl.pallas_call(
        body, name=name, grid=(nblk,), in_specs=in_specs, out_specs=out_specs, out_shape=out_shape,
        compiler_params=_cp(("arbitrary",)),
    )(*args)
    return res


def _row_pick(halo, r):
    rid = lax.broadcasted_iota(jnp.int32, halo.shape, 0)
    return jnp.sum(jnp.where(rid == r, halo, 0.0), axis=0, keepdims=True)


def _shift_down(x, row_before, is_first):
    rid = lax.broadcasted_iota(jnp.int32, x.shape, 0)
    first = jnp.where(is_first, 0.0, 1.0) * row_before
    return jnp.where(rid == 0, first, pltpu.roll(x, 1, axis=0))


def _shift_up(x, row_after, is_last):
    n = x.shape[0]
    rid = lax.broadcasted_iota(jnp.int32, x.shape, 0)
    last = jnp.where(is_last, 0.0, 1.0) * row_after
    return jnp.where(rid == n - 1, last, pltpu.roll(x, n - 1, axis=0))


def _dot(a, b, dn=(((1,), (0,)), ((), ()))):
    return lax.dot_general(a.astype(MMD), b.astype(MMD), dn, preferred_element_type=F32)


def _dot_nt(a, b):
    return _dot(a, b, (((1,), (1,)), ((), ())))


def _dot_tn(a, b):
    return _dot(a, b, (((0,), (0,)), ((), ())))


def _dot_exact01(x, g01):
    hi = x.astype(BF16)
    lo = (x - hi.astype(F32)).astype(BF16)
    dn = (((1,), (0,)), ((), ()))
    return (lax.dot_general(hi, g01, dn, preferred_element_type=F32)
            + lax.dot_general(lo, g01, dn, preferred_element_type=F32))


def _fold_heads(v, fold):
    return _row_pick(_dot_exact01(jnp.broadcast_to(v, (8, v.shape[1])), fold), 0)


def _group_ones(width):
    idx = np.arange(width) // HEAD_DIM
    return jnp.asarray((idx[:, None] == idx[None, :]).astype(np.float32), BF16)


def _fold_ones(width):
    idx = np.arange(width) % HEAD_DIM
    return jnp.asarray((idx[:, None] == np.arange(HEAD_DIM)[None, :]).astype(np.float32), BF16)


def _head_masks(width):
    idx = np.arange(width) // HEAD_DIM
    return jnp.asarray((idx[None, :] == np.arange(width // HEAD_DIM)[:, None]).astype(np.float32)[:, None, :], F32)


def _rms_stats(x):
    r = lax.rsqrt(jnp.mean(x * x, axis=-1, keepdims=True) + RMS_EPS)
    return r, x * r


def rms_fwd(x, gains, *, name):
    T, D = x.shape

    def fn(i, nblk, xb, *gs):
        _, xh = _rms_stats(xb)
        return tuple(xh * g for g in gs)

    return _rowmap(fn, name=name, T=T, tb=512, ins=[("row", x)] + [("const", g) for g in gains],
                   outs=[("row", D, MMD)] * len(gains))


def rms_bwd(x, gains, dhs, dres, *, name):
    T, D = x.shape
    n = len(gains)

    def fn(i, nblk, xb, dr, *rest):
        gs, ds = rest[:n], rest[n:]
        r, xh = _rms_stats(xb)
        dx = dr
        dgs = []
        for g, dh in zip(gs, ds):
            dgs.append(jnp.sum(dh * xh, axis=0, keepdims=True))
            dxh = dh * g
            dx = dx + r * (dxh - xh * jnp.mean(dxh * xh, axis=-1, keepdims=True))
        return (dx, *dgs)

    return _rowmap(fn, name=name, T=T, tb=512,
                   ins=[("row", x), ("row", dres)] + [("const", g) for g in gains] + [("row", d) for d in dhs],
                   outs=[("row", D, F32)], accs=[((1, D), F32)] * n)


def _segsum(x):
    first = lax.broadcasted_iota(jnp.int32, (x.shape[0], 128), 1) < HEAD_DIM
    outs = []
    for p in range(x.shape[1] // 128):
        xs = x[:, p * 128:(p + 1) * 128]
        lo = jnp.sum(jnp.where(first, xs, 0.0), axis=-1, keepdims=True)
        hi = jnp.sum(jnp.where(first, 0.0, xs), axis=-1, keepdims=True)
        outs.append(jnp.where(first, lo, hi))
    return jnp.concatenate(outs, axis=1)


def _pre1_common(i, ps, halo, mu, w0, a0, w2p, a2p, g2p, k_k, k_a):
    prev = _shift_down(ps, _row_pick(halo, 7), i == 0)
    xs = ps + (prev - ps) * mu
    lo = xs[:, 3 * RWKV_WIDTH:]
    tl, sl = jnp.tanh(lo), jax.nn.sigmoid(lo)
    dec = w0 + _dot(tl, w2p)
    ain = a0 + _dot(lo, a2p)
    g = _dot(sl, g2p)
    wl = -jax.nn.softplus(-dec) - 0.5
    w = jnp.exp(-jnp.exp(wl))
    a = jax.nn.sigmoid(ain)
    k = xs[:, RWKV_WIDTH:2 * RWKV_WIDTH]
    z = k * k_k
    nrm = jnp.sqrt(_segsum(z * z))
    kk = z / jnp.maximum(nrm, 1e-12)
    return prev, xs, lo, tl, sl, dec, wl, w, a, g, k, nrm, kk


def rwkv_pre_fwd(p, mu, w0, a0, w2p, a2p, g2p, k_k, k_a):
    T = p.shape[0]

    def fn(i, nblk, ps, halo, mu, w0, a0, w2p, a2p, g2p, k_k, k_a):
        _, xs, _, _, _, _, _, w, a, g, k, _, kk = _pre1_common(i, ps, halo, mu, w0, a0, w2p, a2p, g2p, k_k, k_a)
        W = RWKV_WIDTH
        return xs[:, :W], w, k * (1.0 + (a - 1.0) * k_a), xs[:, 2 * W:3 * W], kk, kk * a, g

    return _rowmap(fn, name="rwkv_pre_fwd", T=T, tb=256,
                   ins=[("row", p, SHIFT_WIDTH, 0), ("prev", p, SHIFT_WIDTH, 0)]
                   + [("const", c) for c in (mu, w0, a0, w2p, a2p, g2p, k_k, k_a)],
                   outs=[("row", RWKV_WIDTH, F32)] * 7)


def rwkv_pre_bwd(p, mu, w0, a0, w2p, a2p, g2p, k_k, k_a, drs, dw, dk2s, dvs, dkk, dkka, dg):
    T = p.shape[0]

    def fn(i, nblk, ps, halo, mu, w0, a0, w2p, a2p, g2p, k_k, k_a, dr0, dr1, dw, dk20, dk21, dv0, dv1, dkk, dkka, dg):
        prev, xs, lo, tl, sl, dec, wl, w, a, g, k, nrm, kk = _pre1_common(i, ps, halo, mu, w0, a0, w2p, a2p, g2p, k_k, k_a)
        dk2 = dk20 + dk21
        dkk_t = dkk + dkka * a
        proj = jnp.where(nrm > 1e-12, kk * _segsum(dkk_t * kk), 0.0)
        dz = (dkk_t - proj) / jnp.maximum(nrm, 1e-12)
        dk = dz * k_k + dk2 * (1.0 + (a - 1.0) * k_a)
        da = dkka * kk + dk2 * k * k_a
        ddec = dw * (-w * jnp.exp(wl)) * jax.nn.sigmoid(-dec)
        dain = da * a * (1.0 - a)
        dlo = (_dot_nt(ddec, w2p) * (1.0 - tl * tl) + _dot_nt(dain, a2p) + _dot_nt(dg, g2p) * sl * (1.0 - sl))
        dxs = jnp.concatenate([dr0 + dr1, dk, dv0 + dv1, dlo], axis=1)
        s = lambda z: jnp.sum(z, axis=0, keepdims=True)
        return (dxs, s(dxs * (prev - ps)), s(ddec), s(dain), _dot_tn(tl, ddec), _dot_tn(lo, dain), _dot_tn(sl, dg),
                s(dz * k), s(dk2 * k * (a - 1.0)))

    return _rowmap(fn, name="rwkv_pre_bwd", T=T, tb=128,
                   ins=[("row", p, SHIFT_WIDTH, 0), ("prev", p, SHIFT_WIDTH, 0)]
                   + [("const", c) for c in (mu, w0, a0, w2p, a2p, g2p, k_k, k_a)]
                   + [("row", c) for c in (*drs, dw, *dk2s, *dvs, dkk, dkka, dg)],
                   outs=[("row", SHIFT_WIDTH, F32)],
                   accs=[((1, SHIFT_WIDTH), F32), ((1, RWKV_WIDTH), F32), ((1, RWKV_WIDTH), F32)]
                   + [((LORA_WIDTH, RWKV_WIDTH), F32)] * 3 + [((1, RWKV_WIDTH), F32)] * 2)


def shift_bwd(dxs, mu, dq_mem):
    T = dxs.shape[0]

    def fn(i, nblk, d, halo, mu, dq):
        nxt = _shift_up(d, _row_pick(halo, 0), i == nblk - 1)
        return jnp.concatenate([d * (1.0 - mu) + nxt * mu, dq], axis=1)

    return _rowmap(fn, name="shift_bwd", T=T, tb=256,
                   ins=[("row", dxs), ("next", dxs), ("const", mu), ("row", dq_mem)],
                   outs=[("row", SHIFT_WIDTH + MEM_WIDTH, MMD)])[0]


N_PAIRS = RWKV_HEADS // 2


def _pair_consts():
    row = lax.broadcasted_iota(jnp.int32, (HEAD_DIM, 128), 0)
    lane = lax.broadcasted_iota(jnp.int32, (HEAD_DIM, 128), 1)
    eye2 = jnp.logical_or(lane == row, lane == row + HEAD_DIM).astype(F32)
    li = lax.broadcasted_iota(jnp.int32, (128, 128), 0) < HEAD_DIM
    lj = lax.broadcasted_iota(jnp.int32, (128, 128), 1) < HEAD_DIM
    return eye2, (li == lj).astype(BF16)


def _pair_sum(p, ones2):
    n, m, l = p.shape
    s = lax.dot_general(p.reshape(n * m, l).astype(BF16), ones2, (((1,), (0,)), ((), ())), preferred_element_type=F32)
    return s.reshape(n, m, l)


def _pair_rows(row):
    return jnp.stack([row[:, p * 128:(p + 1) * 128] for p in range(N_PAIRS)], axis=0)


def _pair_flat(rows):
    return jnp.concatenate([rows[p] for p in range(N_PAIRS)], axis=1)


def _split_bf16(v):
    hi = v.astype(BF16).astype(F32)
    return hi, v - hi


def _gather_copies(srcs, dsts, send_sems, recv_sems):
    x, y, c = _mesh_pos()
    s = 2 * x + y
    me, sibling = (x, y, c), (x, y, 1 - c)
    rc = functools.partial(_remote, send_sems, recv_sems)
    ici, land, fwd, arrived = [], [], [], []
    for b, (src, dst) in enumerate(zip(srcs, dsts)):
        for j, (cx, cy) in enumerate(_other_chips(x, y)):
            k = 6 * b + j
            ici.append(rc(k, src.at[c], dst.at[s, c], (cx, cy, c)))
            blk, blk2 = dst.at[2 * cx + cy, c], dst.at[2 * cx + cy, 1 - c]
            land.append(rc(k, blk, blk, me))
            fwd.append(rc(k + 3, blk, blk, sibling))
            arrived.append(rc(k + 3, blk2, blk2, me))
    return ici, land, fwd, arrived


def scan_fwd(r, w, k2, v, kk, kka, gather=None):
    T, W = r.shape
    tc = SCAN_CHUNK
    nchunk = T // tc
    seq = pl.BlockSpec((tc, W), lambda i: (i, 0))
    one_state = pl.BlockSpec((N_PAIRS, HEAD_DIM, 128), lambda i: (0, 0, 0))
    nb = 0 if gather is None else len(gather)

    def body(r_ref, w_ref, k2_ref, v_ref, kk_ref, kka_ref, *rest):
        if gather is None:
            y_ref, st_ref, fin_ref, s_scr, vhi_scr, vlo_scr = rest
        else:
            srcs, (y_ref, st_ref, fin_ref), dsts = rest[:nb], rest[nb:nb + 3], rest[nb + 3:2 * nb + 3]
            s_scr, vhi_scr, vlo_scr, send_sems, recv_sems = rest[2 * nb + 3:]
            ici, land, fwd, arrived = _gather_copies(srcs, dsts, send_sems, recv_sems)

            @pl.when(pl.program_id(0) == 0)
            def _():
                for cp in ici:
                    cp.start()

            @pl.when(pl.program_id(0) == nchunk // 2)
            def _():
                for a, f in zip(land, fwd):
                    a.wait_recv()
                    f.start()

        @pl.when(pl.program_id(0) == 0)
        def _():
            s_scr[...] = jnp.zeros(s_scr.shape, F32)

        vhi_scr[...], vlo_scr[...] = _split_bf16(v_ref[...])
        eye2, ones2 = _pair_consts()

        def step(t, carry):
            r_t, w_t, k2_t, kk_t, kka_t, vhi_t, vlo_t = (
                _pair_rows(ref[pl.ds(t, 1), :]) for ref in (r_ref, w_ref, k2_ref, kk_ref, kka_ref, vhi_scr, vlo_scr))
            S = s_scr[...]
            sa = -_pair_sum(S * kk_t, ones2)
            vb = _pair_sum(eye2 * vhi_t, ones2) + _pair_sum(eye2 * vlo_t, ones2)
            S2 = S * w_t + sa * kka_t + vb * k2_t
            y_ref[pl.ds(t, 1), :] = _pair_flat(jnp.sum(eye2 * _pair_sum(S2 * r_t, ones2), axis=1, keepdims=True))
            s_scr[...] = S2
            st_ref[t] = S
            return carry

        lax.fori_loop(0, tc, step, 0, unroll=SCAN_UNROLL)
        fin_ref[...] = s_scr[...]

        if gather is not None:
            @pl.when(pl.program_id(0) == nchunk - 1)
            def _():
                for a in arrived:
                    a.wait_recv()
                for cp in ici + fwd:
                    cp.wait_send()

    in_specs = [seq] * 6
    out_specs = [seq, pl.BlockSpec((tc, N_PAIRS, HEAD_DIM, 128), lambda i: (i, 0, 0, 0)), one_state]
    out_shape = [jax.ShapeDtypeStruct((T, W), F32), jax.ShapeDtypeStruct((T, N_PAIRS, HEAD_DIM, 128), F32),
                 jax.ShapeDtypeStruct((N_PAIRS, HEAD_DIM, 128), F32)]
    scratch = [pltpu.VMEM((N_PAIRS, HEAD_DIM, 128), F32), pltpu.VMEM((tc, W), F32), pltpu.VMEM((tc, W), F32)]
    args = [r, w, k2, v, kk, kka]
    if gather is not None:
        in_specs += [HBM_SPEC] * nb
        out_specs += [HBM_SPEC] * nb
        out_shape += [jax.ShapeDtypeStruct((N_CHIPS, *g.shape), g.dtype) for g in gather]
        scratch += [pltpu.SemaphoreType.DMA((6 * nb,)), pltpu.SemaphoreType.DMA((6 * nb,))]
        args += list(gather)
    return pl.pallas_call(
        body, name="rwkv_scan_fwd", grid=(nchunk,), in_specs=in_specs, out_specs=out_specs, out_shape=out_shape,
        scratch_shapes=scratch, compiler_params=_cp(("arbitrary",)),
    )(*args)


PEER_FLIPS = tuple((fx, fy, fc) for fx in (0, 1) for fy in (0, 1) for fc in (0, 1))[1:]


def scan_bwd(r, w, k2, v, kk, kka, states, final_state, dy, scatter=None):
    T, W = r.shape
    tc = SCAN_CHUNK
    nchunk = T // tc
    seq = pl.BlockSpec((tc, W), lambda i: (nchunk - 1 - i, 0))
    st_spec = pl.BlockSpec((tc, N_PAIRS, HEAD_DIM, 128), lambda i: (nchunk - 1 - i, 0, 0, 0))
    one_state = pl.BlockSpec((N_PAIRS, HEAD_DIM, 128), lambda i: (0, 0, 0))
    nb, npeer = (0 if scatter is None else len(scatter)), len(PEER_FLIPS)

    def body(r_ref, w_ref, k2_ref, v_ref, kk_ref, kka_ref, st_ref, fin_ref, dy_ref, *rest):
        if scatter is None:
            dr_ref, dw_ref, dk2_ref, dv_ref, dkk_ref, dkka_ref, ds_scr, sc_scr, vhi_scr, vlo_scr = rest
        else:
            srcs, dsts = rest[:nb], rest[nb + 6:2 * nb + 6]
            dr_ref, dw_ref, dk2_ref, dv_ref, dkk_ref, dkka_ref = rest[nb:nb + 6]
            ds_scr, sc_scr, vhi_scr, vlo_scr, send_sems, recv_sems = rest[2 * nb + 6:]
            x, y, c = _mesh_pos()
            copies = []
            for b, (src, dst) in enumerate(zip(srcs, dsts)):
                for k, (fx, fy, fc) in enumerate(PEER_FLIPS):
                    px, py, pc = (1 - x if fx else x), (1 - y if fy else y), (1 - c if fc else c)
                    copies.append(_remote(send_sems, recv_sems, npeer * b + k, src.at[2 * px + py, pc], dst.at[k],
                                          (px, py, pc)))

            @pl.when(pl.program_id(0) == 0)
            def _():
                for cp in copies:
                    cp.start()

        @pl.when(pl.program_id(0) == 0)
        def _():
            ds_scr[...] = jnp.zeros(ds_scr.shape, F32)
            sc_scr[...] = fin_ref[...]

        vhi_scr[...], vlo_scr[...] = _split_bf16(v_ref[...])
        eye2, ones2 = _pair_consts()
        colsum = lambda z: jnp.sum(z, axis=1, keepdims=True)

        def step(j, carry):
            t = tc - 1 - j
            r_t, w_t, k2_t, kk_t, kka_t, vhi_t, vlo_t, dy_t = (
                _pair_rows(ref[pl.ds(t, 1), :])
                for ref in (r_ref, w_ref, k2_ref, kk_ref, kka_ref, vhi_scr, vlo_scr, dy_ref))
            s_prev, s_cur = st_ref[t], sc_scr[...]
            dyb = _pair_sum(eye2 * dy_t, ones2)
            vb = _pair_sum(eye2 * vhi_t, ones2) + _pair_sum(eye2 * vlo_t, ones2)
            sa = -_pair_sum(s_prev * kk_t, ones2)
            dS = ds_scr[...] + dyb * r_t
            dsa = _pair_sum(dS * kka_t, ones2)
            ds_scr[...] = dS * w_t - dsa * kk_t
            sc_scr[...] = s_prev
            for ref, val in zip((dr_ref, dw_ref, dk2_ref, dv_ref, dkk_ref, dkka_ref),
                                (s_cur * dyb, dS * s_prev, dS * vb, eye2 * _pair_sum(dS * k2_t, ones2),
                                 -(s_prev * dsa), dS * sa)):
                ref[pl.ds(t, 1), :] = _pair_flat(colsum(val))
            return carry

        lax.fori_loop(0, tc, step, 0, unroll=SCAN_UNROLL_BWD)

        if scatter is not None:
            @pl.when(pl.program_id(0) == nchunk - 1)
            def _():
                for cp in copies:
                    cp.wait()

    in_specs = [seq] * 6 + [st_spec, one_state, seq]
    out_specs = [seq] * 6
    out_shape = [jax.ShapeDtypeStruct((T, W), F32)] * 6
    scratch = [pltpu.VMEM((N_PAIRS, HEAD_DIM, 128), F32)] * 2 + [pltpu.VMEM((tc, W), F32)] * 2
    args = [r, w, k2, v, kk, kka, states, final_state, dy]
    if scatter is not None:
        in_specs += [HBM_SPEC] * nb
        out_specs += [HBM_SPEC] * nb
        out_shape += [jax.ShapeDtypeStruct((npeer, *s.shape[2:]), s.dtype) for s in scatter]
        scratch += [pltpu.SemaphoreType.DMA((npeer * nb,)), pltpu.SemaphoreType.DMA((npeer * nb,))]
        args += list(scatter)
    return pl.pallas_call(
        body, name="rwkv_scan_bwd", grid=(nchunk,), in_specs=in_specs, out_specs=out_specs, out_shape=out_shape,
        scratch_shapes=scratch, compiler_params=_cp(("arbitrary",)),
    )(*args)


def _mix_common(y, r, k2, v, lnx_w, lnx_b, r_k):
    yc = y - _segsum(y) * (1.0 / HEAD_DIM)
    rstd = lax.rsqrt(_segsum(yc * yc) * (1.0 / HEAD_DIM) + LNX_EPS)
    yhat = yc * rstd
    s = _segsum(r * k2 * r_k)
    return rstd, yhat, s, yhat * lnx_w + lnx_b + s * v


def mix_gate_fwd(y, r, k2, v, g, y_mem, lnx_w, lnx_b, r_k):
    T = y.shape[0]

    def fn(i, nblk, y, r, k2, v, g, ym, lw, lb, rk):
        mix = _mix_common(y, r, k2, v, lw, lb, rk)[3]
        return jnp.concatenate([mix * g, ym], axis=1)

    return _rowmap(fn, name="mix_gate_fwd", T=T, tb=256,
                   ins=[("row", z) for z in (y, r, k2, v, g, y_mem)] + [("const", c) for c in (lnx_w, lnx_b, r_k)],
                   outs=[("row", RWKV_WIDTH + MEM_WIDTH, MMD)])[0]


def mix_gate_bwd(y, r, k2, v, g, dycat, lnx_w, lnx_b, r_k):
    T = y.shape[0]

    def fn(i, nblk, y, r, k2, v, g, dyc, lw, lb, rk):
        rstd, yhat, s, mix = _mix_common(y, r, k2, v, lw, lb, rk)
        dmix = dyc * g
        dyh = dmix * lw
        inv = 1.0 / HEAD_DIM
        dy = rstd * (dyh - _segsum(dyh) * inv - yhat * (_segsum(dyh * yhat) * inv))
        ds = _segsum(dmix * v)
        cs = lambda z: jnp.sum(z, axis=0, keepdims=True)
        return (dy, ds * k2 * rk, ds * r * rk, dmix * s, dyc * mix, cs(dmix * yhat), cs(dmix), cs(ds * r * k2))

    return _rowmap(fn, name="mix_gate_bwd", T=T, tb=256,
                   ins=[("row", z) for z in (y, r, k2, v, g)] + [("row", dycat, RWKV_WIDTH, 0)]
                   + [("const", c) for c in (lnx_w, lnx_b, r_k)],
                   outs=[("row", RWKV_WIDTH, F32)] * 5, accs=[((1, RWKV_WIDTH), F32)] * 3)


def _head_rms(x, gones):
    ms = _segsum(x * x) * (1.0 / HEAD_DIM)
    r = lax.rsqrt(ms + RMS_EPS)
    return r, x * r


def _head_rms_bwd(dxn_g, r, xh, gones):
    return r * (dxn_g - xh * (_segsum(dxn_g * xh) * (1.0 / HEAD_DIM)))


def mem_kv_fwd(mem, norm_g, w_kv, k_norm_t, *, name):
    gones = _group_ones(MEM_WIDTH)

    def body(mem_ref, g_ref, w_ref, kn_ref, go_ref, k_out, v_out):
        _, xh = _rms_stats(mem_ref[...])
        kv = _dot(xh * g_ref[...], w_ref[...])
        _, kh = _head_rms(kv[:, :MEM_WIDTH], go_ref[...])
        k_out[...] = kh * kn_ref[...]
        v_out[...] = kv[:, MEM_WIDTH:]

    return pl.pallas_call(
        body, name=name, out_shape=[jax.ShapeDtypeStruct((N_MEM, MEM_WIDTH), F32)] * 2, compiler_params=_cp(),
    )(mem, norm_g, w_kv, k_norm_t, gones)


def mem_kv_bwd(mem, norm_g, w_kv, k_norm_t, dkn, dv, *, name):
    gones, fold = _group_ones(MEM_WIDTH), _fold_ones(MEM_WIDTH)

    def body(mem_ref, g_ref, w_ref, kn_ref, go_ref, fo_ref, dkn_ref, dv_ref, dw_out, dg_out, dkg_out):
        _, xh = _rms_stats(mem_ref[...])
        hm = xh * g_ref[...]
        kv = _dot(hm, w_ref[...])
        r, kh = _head_rms(kv[:, :MEM_WIDTH], go_ref[...])
        dkn = dkn_ref[...]
        dkg_out[...] = _fold_heads(jnp.sum(dkn * kh, axis=0, keepdims=True), fo_ref[...])
        dkraw = _head_rms_bwd(dkn * kn_ref[...], r, kh, go_ref[...])
        dkv = jnp.concatenate([dkraw, dv_ref[...]], axis=1)
        dw_out[...] = _dot_tn(hm, dkv)
        dg_out[...] = jnp.sum(_dot_nt(dkv, w_ref[...]) * xh, axis=0, keepdims=True)

    return pl.pallas_call(
        body, name=name,
        out_shape=[jax.ShapeDtypeStruct((D_MODEL, 2 * MEM_WIDTH), F32), jax.ShapeDtypeStruct((1, D_MODEL), F32),
                   jax.ShapeDtypeStruct((1, HEAD_DIM), F32)],
        compiler_params=_cp(),
    )(mem, norm_g, w_kv, k_norm_t, gones, fold, dkn, dv)


def _mem_scores(qn, kn, masks, h):
    s = _dot_nt(qn * masks[h], kn) * (1.0 / math.sqrt(HEAD_DIM))
    s = s - jnp.max(s, axis=-1, keepdims=True)
    e = jnp.exp(s)
    return e / jnp.sum(e, axis=-1, keepdims=True)


def mem_attn_fwd(p, colblock, kn, v, q_norm_t, *, name):
    T = p.shape[0]
    gones, masks = _group_ones(MEM_WIDTH), _head_masks(MEM_WIDTH)

    def fn(i, nblk, q, kn, v, qg, go, masks):
        _, qh = _head_rms(q, go)
        qn = qh * qg
        out = jnp.zeros(q.shape, F32)
        for h in range(MEM_WIDTH // HEAD_DIM):
            out = out + _dot(_mem_scores(qn, kn, masks, h), v * masks[h])
        return out

    return _rowmap(fn, name=name, T=T, tb=512,
                   ins=[("row", p, MEM_WIDTH, colblock)] + [("const", c) for c in (kn, v, q_norm_t, gones, masks)],
                   outs=[("row", MEM_WIDTH, F32)])[0]


def mem_attn_bwd(p, colblock, kn, v, q_norm_t, dycat, dcolblock, *, name):
    T = p.shape[0]
    gones, masks, fold = _group_ones(MEM_WIDTH), _head_masks(MEM_WIDTH), _fold_ones(MEM_WIDTH)
    scale = 1.0 / math.sqrt(HEAD_DIM)

    def fn(i, nblk, q, dy, kn, v, qg, go, masks, fo):
        r, qh = _head_rms(q, go)
        qn = qh * qg
        dqn = jnp.zeros(q.shape, F32)
        dkn = jnp.zeros(kn.shape, F32)
        dv = jnp.zeros(v.shape, F32)
        for h in range(MEM_WIDTH // HEAD_DIM):
            pr = _mem_scores(qn, kn, masks, h)
            dyh = dy * masks[h]
            dpr = _dot_nt(dyh, v)
            ds = pr * (dpr - jnp.sum(dpr * pr, axis=-1, keepdims=True)) * scale
            dqn = dqn + _dot(ds, kn * masks[h])
            dkn = dkn + _dot_tn(ds, qn * masks[h])
            dv = dv + _dot_tn(pr, dyh)
        dqg = _fold_heads(jnp.sum(dqn * qh, axis=0, keepdims=True), fo)
        return _head_rms_bwd(dqn * qg, r, qh, go), dkn, dv, dqg

    return _rowmap(fn, name=name, T=T, tb=512,
                   ins=[("row", p, MEM_WIDTH, colblock), ("row", dycat, MEM_WIDTH, dcolblock)]
                   + [("const", c) for c in (kn, v, q_norm_t, gones, masks, fold)],
                   outs=[("row", MEM_WIDTH, F32)],
                   accs=[((N_MEM, MEM_WIDTH), F32), ((N_MEM, MEM_WIDTH), F32), ((1, HEAD_DIM), F32)])


def _ffn_conv(i, u, halo, cw, cb):
    up1 = _shift_down(u, _row_pick(halo, 7), i == 0)
    up2 = _shift_down(up1, _row_pick(halo, 6), i == 0)
    c = cb + cw[0] * up2 + cw[1] * up1 + cw[2] * u
    return up1, up2, c[:, :D_FF], c[:, D_FF:]


def ffn_act_fwd(u, cw, cb, *, name):
    T = u.shape[0]

    def fn(i, nblk, u, halo, c0, c1, c2, cb):
        _, _, gate, val = _ffn_conv(i, u, halo, (c0, c1, c2), cb)
        return jax.nn.silu(gate) * val

    return _rowmap(fn, name=name, T=T, tb=128, ins=[("row", u), ("prev", u)] + [("const", c) for c in (*cw, cb)],
                   outs=[("row", D_FF, MMD)])[0]


def ffn_act_bwd(u, cw, cb, dz, *, name):
    T = u.shape[0]

    def fn(i, nblk, u, halo, c0, c1, c2, cb, dz):
        up1, up2, gate, val = _ffn_conv(i, u, halo, (c0, c1, c2), cb)
        sg = jax.nn.sigmoid(gate)
        dgate = dz * val * sg * (1.0 + gate * (1.0 - sg))
        dval = dz * gate * sg
        dc = jnp.concatenate([dgate, dval], axis=1)
        s = lambda z: jnp.sum(z, axis=0, keepdims=True)
        return dc, s(dc * up2), s(dc * up1), s(dc * u), s(dc)

    return _rowmap(fn, name=name, T=T, tb=128,
                   ins=[("row", u), ("prev", u)] + [("const", c) for c in (*cw, cb)] + [("row", dz)],
                   outs=[("row", 2 * D_FF, F32)], accs=[((1, 2 * D_FF), F32)] * 4)


def conv_bwd(dc, cw, *, name):
    T = dc.shape[0]

    def fn(i, nblk, d, halo, c0, c1, c2):
        last = i == nblk - 1
        n1 = _shift_up(d, _row_pick(halo, 0), last)
        n2 = _shift_up(n1, _row_pick(halo, 1), last)
        return c2 * d + c1 * n1 + c0 * n2

    return _rowmap(fn, name=name, T=T, tb=128, ins=[("row", dc), ("next", dc)] + [("const", c) for c in cw],
                   outs=[("row", 2 * D_FF, MMD)])[0]


def _rope_swap(z):
    lane = lax.broadcasted_iota(jnp.int32, z.shape, 1) % HEAD_DIM
    w = z.shape[1]
    return jnp.where(lane < HEAD_DIM // 2, pltpu.roll(z, w - HEAD_DIM // 2, axis=1), pltpu.roll(z, HEAD_DIM // 2, axis=1))


def rope_tables(T):
    inv = (np.float32(ROPE_THETA) ** (-np.arange(0, HEAD_DIM, 2, dtype=np.float32) / np.float32(HEAD_DIM))).astype(np.float32)
    ang = (np.arange(T, dtype=np.float32)[:, None] * inv[None, :]).astype(np.float64)
    cos, sin = np.cos(ang).astype(np.float32), np.sin(ang).astype(np.float32)
    return (jnp.asarray(np.concatenate([cos, cos, cos, cos], axis=1)),
            jnp.asarray(np.concatenate([-sin, sin, -sin, sin], axis=1)))


def _rope_wide(t):
    return jnp.tile(t, (1, DIL_WIDTH // t.shape[1]))


def qk_fwd(kvp, pb, kg_t, qg_t, cos, sin):
    T = kvp.shape[0]
    gones = _group_ones(DIL_WIDTH)

    def fn(i, nblk, kraw, vraw, qraw, kg, qg, c, s, go):
        c, s = _rope_wide(c), _rope_wide(s)
        outs = []
        for raw, g in ((qraw, qg), (kraw, kg)):
            _, xh = _head_rms(raw, go)
            z = xh * g
            outs.append(z * c + _rope_swap(z) * s)
        return outs[0], outs[1], vraw

    return _rowmap(fn, name="qk_fwd", T=T, tb=256,
                   ins=[("row", kvp, DIL_WIDTH, 0), ("row", kvp, DIL_WIDTH, 1), ("row", pb, DIL_WIDTH, 0)]
                   + [("const", kg_t), ("const", qg_t), ("row", cos), ("row", sin), ("const", gones)],
                   outs=[("row", DIL_WIDTH, MMD)] * 3)


def qk_bwd(kvp, pb, kg_t, qg_t, cos, sin, dq, dk, dv, dq_mem):
    T = kvp.shape[0]
    gones, fold = _group_ones(DIL_WIDTH), _fold_ones(DIL_WIDTH)

    def fn(i, nblk, kraw, qraw, kg, qg, c, s, go, fo, dq, dk, dv, dqm):
        c, s = _rope_wide(c), _rope_wide(s)
        res, dgs = [], []
        for raw, g, d in ((qraw, qg, dq), (kraw, kg, dk)):
            r, xh = _head_rms(raw, go)
            dz = d * c + _rope_swap(d * s)
            dgs.append(_fold_heads(jnp.sum(dz * xh, axis=0, keepdims=True), fo))
            res.append(_head_rms_bwd(dz * g, r, xh, go))
        return (jnp.concatenate([res[0], dqm], axis=1), jnp.concatenate([res[1], dv], axis=1), dgs[0], dgs[1])

    return _rowmap(fn, name="qk_bwd", T=T, tb=256,
                   ins=[("row", kvp, DIL_WIDTH, 0), ("row", pb, DIL_WIDTH, 0), ("const", kg_t), ("const", qg_t),
                        ("row", cos), ("row", sin), ("const", gones), ("const", fold),
                        ("row", dq), ("row", dk), ("row", dv), ("row", dq_mem)],
                   outs=[("row", DIL_WIDTH + MEM_WIDTH, MMD), ("row", 2 * DIL_WIDTH, MMD)],
                   accs=[((1, HEAD_DIM), F32)] * 2)


def _band(kind):
    i = lax.broadcasted_iota(jnp.int32, (DIL_BLOCK, DIL_BLOCK), 0)
    j = lax.broadcasted_iota(jnp.int32, (DIL_BLOCK, DIL_BLOCK), 1)
    return (j <= i) if kind == "cur" else (j >= i)


def dil_attn_fwd(q, k, v, seq_blocks, *, name):
    T, W = q.shape
    nb = T // DIL_BLOCK
    masks = _head_masks(W)
    cur = pl.BlockSpec((DIL_BLOCK, W), lambda n: (n, 0))
    prv = pl.BlockSpec((DIL_BLOCK, W), lambda n: (jnp.maximum(n - 1, 0), 0))
    scale = 1.0 / math.sqrt(HEAD_DIM)

    def body(q_ref, kc_ref, kp_ref, vc_ref, vp_ref, m_ref, o_ref, l_ref):
        n = pl.program_id(0)
        has_prev = (n % seq_blocks) != 0
        q = q_ref[...].astype(F32)
        kc, kp = kc_ref[...].astype(F32), kp_ref[...].astype(F32)
        vc, vp = vc_ref[...].astype(F32), vp_ref[...].astype(F32)
        ok_c = _band("cur")
        ok_p = jnp.logical_and(_band("prev"), has_prev)
        o = jnp.zeros((DIL_BLOCK, W), F32)
        lse = jnp.zeros((DIL_BLOCK, W), F32)
        for h in range(W // HEAD_DIM):
            mh = m_ref[h]
            qh = q * mh
            sc = jnp.where(ok_c, _dot_nt(qh, kc) * scale, NEG_INF)
            sp = jnp.where(ok_p, _dot_nt(qh, kp) * scale, NEG_INF)
            mx = jnp.maximum(jnp.max(sc, axis=-1, keepdims=True), jnp.max(sp, axis=-1, keepdims=True))
            ec, ep = jnp.exp(sc - mx), jnp.exp(sp - mx)
            den = jnp.sum(ec, axis=-1, keepdims=True) + jnp.sum(ep, axis=-1, keepdims=True)
            o = o + (_dot(ec, vc * mh) + _dot(ep, vp * mh)) / den
            lse = lse + (mx + jnp.log(den)) * mh
        o_ref[...] = o
        l_ref[...] = lse

    return pl.pallas_call(
        body, name=name, grid=(nb,), in_specs=[cur, cur, prv, cur, prv, pl.BlockSpec(masks.shape, lambda n: (0, 0, 0))],
        out_specs=[cur, cur], out_shape=[jax.ShapeDtypeStruct((T, W), F32)] * 2,
        compiler_params=_cp(("parallel",)),
    )(q, k, k, v, v, masks)


def dil_attn_bwd(q, k, v, o, lse, do, dlse, seq_blocks, *, name):
    T, W = q.shape
    nb = T // DIL_BLOCK
    masks = _head_masks(W)
    cur = pl.BlockSpec((DIL_BLOCK, W), lambda n: (n, 0))
    prv = pl.BlockSpec((DIL_BLOCK, W), lambda n: (jnp.maximum(n - 1, 0), 0))
    nxt = pl.BlockSpec((DIL_BLOCK, W), lambda n: (jnp.minimum(n + 1, nb - 1), 0))
    scale = 1.0 / math.sqrt(HEAD_DIM)

    def body(qc_ref, qn_ref, kc_ref, kp_ref, vc_ref, vp_ref, oc_ref, on_ref, lc_ref, ln_ref, doc_ref, don_ref,
             dlc_ref, dln_ref, m_ref, dq_ref, dk_ref, dv_ref):
        n = pl.program_id(0)
        has_prev = (n % seq_blocks) != 0
        has_next = jnp.logical_and(((n + 1) % seq_blocks) != 0, n + 1 < nb)
        f = lambda ref: ref[...].astype(F32)
        qc, qn, kc, kp, vc, vp = f(qc_ref), f(qn_ref), f(kc_ref), f(kp_ref), f(vc_ref), f(vp_ref)
        doc, don = doc_ref[...], don_ref[...]
        ok_c = _band("cur")
        ok_p = jnp.logical_and(_band("prev"), has_prev)
        ok_n = jnp.logical_and(_band("prev"), has_next)
        dq = jnp.zeros((DIL_BLOCK, W), F32)
        dk = jnp.zeros((DIL_BLOCK, W), F32)
        dv = jnp.zeros((DIL_BLOCK, W), F32)

        def side(qh, kk, vv, doh, lse_h, corr, ok):
            s = _dot_nt(qh, kk) * scale
            pr = jnp.where(ok, jnp.exp(jnp.where(ok, s, NEG_INF) - lse_h), 0.0)
            ds = pr * (_dot_nt(doh, vv) + corr) * scale
            return pr, ds

        for h in range(W // HEAD_DIM):
            mh = m_ref[h]
            red = lambda z: jnp.sum(z * mh, axis=-1, keepdims=True)
            qh, doh = qc * mh, doc * mh
            lse_h = red(lc_ref[...]) * (1.0 / HEAD_DIM)
            corr = red(dlc_ref[...]) - red(doc * oc_ref[...])
            pr_c, ds_c = side(qh, kc, vc * mh, doh, lse_h, corr, ok_c)
            _, ds_p = side(qh, kp, vp * mh, doh, lse_h, corr, ok_p)
            dq = dq + _dot(ds_c, kc * mh) + _dot(ds_p, kp * mh)
            dk = dk + _dot_tn(ds_c, qh)
            dv = dv + _dot_tn(pr_c, doh)
            qh2, doh2 = qn * mh, don * mh
            lse_2 = red(ln_ref[...]) * (1.0 / HEAD_DIM)
            corr2 = red(dln_ref[...]) - red(don * on_ref[...])
            pr_n, ds_n = side(qh2, kc, vc * mh, doh2, lse_2, corr2, ok_n)
            dk = dk + _dot_tn(ds_n, qh2)
            dv = dv + _dot_tn(pr_n, doh2)
        dq_ref[...] = dq
        dk_ref[...] = dk
        dv_ref[...] = dv

    return pl.pallas_call(
        body, name=name, grid=(nb,),
        in_specs=[cur, nxt, cur, prv, cur, prv, cur, nxt, cur, nxt, cur, nxt, cur, nxt,
                  pl.BlockSpec(masks.shape, lambda n: (0, 0, 0))],
        out_specs=[cur] * 3, out_shape=[jax.ShapeDtypeStruct((T, W), F32)] * 3,
        compiler_params=_cp(("parallel",)),
    )(q, q, k, k, v, v, o, o, lse, lse, do, do, dlse, dlse, masks)


def _mix_weights(ls):
    m = jnp.maximum(jnp.maximum(ls[0], ls[1]), ls[2])
    es = [jnp.exp(l - m) for l in ls]
    den = es[0] + es[1] + es[2]
    return [e / den for e in es]


def mix_fwd(os_, ls, y_mem):
    T = y_mem.shape[0]

    def fn(i, nblk, o0, o1, o2, l0, l1, l2, ym):
        w = _mix_weights((l0, l1, l2))
        return jnp.concatenate([w[0] * o0 + w[1] * o1 + w[2] * o2, ym], axis=1)

    return _rowmap(fn, name="mix_fwd", T=T, tb=512, ins=[("row", z) for z in (*os_, *ls, y_mem)],
                   outs=[("row", 2 * MEM_WIDTH, MMD)])[0]


def mix_bwd(os_, ls, dycat):
    T = dycat.shape[0]

    def fn(i, nblk, o0, o1, o2, l0, l1, l2, dy):
        w = _mix_weights((l0, l1, l2))
        os3 = (o0, o1, o2)
        dws = [dy * o for o in os3]
        tot = w[0] * dws[0] + w[1] * dws[1] + w[2] * dws[2]
        return tuple(wg * dy for wg in w) + tuple(wg * (dw - tot) for wg, dw in zip(w, dws))

    return _rowmap(fn, name="mix_bwd", T=T, tb=512,
                   ins=[("row", z) for z in (*os_, *ls)] + [("row", dycat, MEM_WIDTH, 0)],
                   outs=[("row", MEM_WIDTH, F32)] * 6)


def loss_fwd_bwd(y, target):
    T, D = y.shape

    def fn(i, nblk, y, t):
        e = y - t
        return e * (1.0 / D), jnp.zeros((8, 128), F32) + jnp.sum(e * e) * (0.5 / D)

    return _rowmap(fn, name="loss", T=T, tb=512, ins=[("row", y), ("row", target)], outs=[("row", D, F32)],
                   accs=[((8, 128), F32)])


def _to_residues(z, dil):
    T, W = z.shape
    return z.reshape(T // dil, dil, W).transpose(1, 0, 2).reshape(T, W)


def _from_residues(z, dil):
    T, W = z.shape
    return z.reshape(dil, T // dil, W).transpose(1, 0, 2).reshape(T, W)


def _pad_rows(w, rows):
    return jnp.concatenate([w, jnp.zeros((rows - w.shape[0], w.shape[1]), w.dtype)], axis=0)


def _tile_heads(g, width):
    return jnp.tile(g.reshape(1, HEAD_DIM), (1, width // HEAD_DIM))


def _conv_rows(W, i):
    return [W["ffn_conv_w"][i][j:j + 1] for j in range(3)]


def _ffn_fwd(x, i, W):
    hn = rms_fwd(x, [W["ffn_norm"][i:i + 1]], name=f"ffn_rms{i}")[0]
    u = _mm(hn, W["ffn_w_up"][i], name=f"ffn_up{i}")
    z = ffn_act_fwd(u, _conv_rows(W, i), W["ffn_conv_b"][i:i + 1], name=f"ffn_act{i}")
    out = _mm(z, W["ffn_w_down"][i], add=x, name=f"ffn_down{i}")
    return out, (x, hn, u, z)


def _ffn_bwd(dout, i, W, saved, G):
    x, hn, u, z = saved
    dz = _mm(dout, W["ffn_w_down"][i], tb=True, name=f"ffn_down_dx{i}")
    G["ffn_w_down"][i] = _mm(z, dout, ta=True, name=f"ffn_down_dw{i}")
    dc, dw0, dw1, dw2, db = ffn_act_bwd(u, _conv_rows(W, i), W["ffn_conv_b"][i:i + 1], dz, name=f"ffn_act_bwd{i}")
    G["ffn_conv_w"][i] = jnp.concatenate([dw0, dw1, dw2], axis=0)
    G["ffn_conv_b"][i] = db[0]
    du = conv_bwd(dc, _conv_rows(W, i), name=f"ffn_conv_bwd{i}")
    dhn = _mm(du, W["ffn_w_up"][i], tb=True, name=f"ffn_up_dx{i}")
    G["ffn_w_up"][i] = _mm(hn, du, ta=True, name=f"ffn_up_dw{i}")
    dx, dg = rms_bwd(x, [W["ffn_norm"][i:i + 1]], [dhn], dout, name=f"ffn_rms_bwd{i}")
    G["ffn_norm"][i] = dg[0]
    return dx


def local_step(x, mem, target, W, late=None):
    T = x.shape[0]
    W = dict(W)
    G = {"ffn_w_down": [None, None], "ffn_w_up": [None, None], "ffn_conv_w": [None, None],
         "ffn_conv_b": [None, None], "ffn_norm": [None, None], "attn_norm": [None, None], "mem_norm": [None, None],
         "mem_w_kv": [None, None], "mem_q_norm": [None, None], "mem_k_norm": [None, None]}
    mu, w0, a0 = W["a_mu"], W["a_w0"], W["a_a0"]
    w2p, a2p, g2p = (_pad_rows(W["a_w2"][0], LORA_WIDTH),
                     jnp.concatenate([jnp.zeros((64, RWKV_WIDTH), MMD), W["a_a2"][0],
                                      jnp.zeros((128, RWKV_WIDTH), MMD)], axis=0),
                     jnp.concatenate([jnp.zeros((128, RWKV_WIDTH), MMD), W["a_g2"][0]], axis=0))
    k_k, k_a, lnx_w, lnx_b = W["a_k_k"], W["a_k_a"], W["a_lnx_w"], W["a_lnx_b"]
    r_k = W["a_r_k"].reshape(1, RWKV_WIDTH)

    h0 = rms_fwd(x, [W["attn_norm"][0:1]], name="attn_rms0")[0]
    p = _mm(h0, W["a_w_in"][0], name="a_in")
    r, w, k2, v, kk, kka, g = rwkv_pre_fwd(p, mu, w0, a0, w2p, a2p, g2p, k_k, k_a)
    if late is None:
        y, states, final_state = scan_fwd(r, w, k2, v, kk, kka)
    else:
        y, states, final_state, *gathered = scan_fwd(r, w, k2, v, kk, kka, gather=late[0])
        W.update(late[1](gathered))
    memkv = []
    for i in range(2):
        memkv.append(mem_kv_fwd(mem, W["mem_norm"][i:i + 1], W["mem_w_kv"][i], _tile_heads(W["mem_k_norm"][i], MEM_WIDTH),
                                name=f"mem_kv{i}"))
    qg0 = _tile_heads(W["mem_q_norm"][0], MEM_WIDTH)
    y_mem0 = mem_attn_fwd(p, SHIFT_WIDTH // MEM_WIDTH, memkv[0][0], memkv[0][1], qg0, name="mem_attn0")
    ycat0 = mix_gate_fwd(y, r, k2, v, g, y_mem0, lnx_w, lnx_b, r_k)
    x1 = _mm(ycat0, W["a_w_out"][0], add=x, name="a_out")
    x2, ffn0 = _ffn_fwd(x1, 0, W)

    h1, hkv = rms_fwd(x2, [W["attn_norm"][1:2], W["kv_norm"].reshape(1, -1)], name="attn_rms1")
    kvp = _mm(hkv, W["kv_w"], name="kv_in")
    pb = _mm(h1, W["b_w_in"][0], name="b_in")
    cos, sin = rope_tables(T)
    kg_t, qg_t = _tile_heads(W["kv_k_norm"], DIL_WIDTH), _tile_heads(W["b_q_norm"][0], DIL_WIDTH)
    q, ksh, vsh = qk_fwd(kvp, pb, kg_t, qg_t, cos, sin)
    os_, ls, grp = [], [], []
    for gi, (win, dil) in enumerate(DIL_GROUPS):
        sl = slice(gi * MEM_WIDTH, (gi + 1) * MEM_WIDTH)
        qg_, kg_, vg_ = (_to_residues(z[:, sl], dil) for z in (q, ksh, vsh))
        o_r, l_r = dil_attn_fwd(qg_, kg_, vg_, T // dil // DIL_BLOCK, name=f"dil_fwd{gi}")
        grp.append((qg_, kg_, vg_, o_r, l_r))
        os_.append(_from_residues(o_r, dil))
        ls.append(_from_residues(l_r, dil))
    qg1 = _tile_heads(W["mem_q_norm"][1], MEM_WIDTH)
    y_mem1 = mem_attn_fwd(pb, DIL_WIDTH // MEM_WIDTH, memkv[1][0], memkv[1][1], qg1, name="mem_attn1")
    ycat1 = mix_fwd(os_, ls, y_mem1)
    x3 = _mm(ycat1, W["b_w_out"][0], add=x2, name="b_out")
    x4, ffn1 = _ffn_fwd(x3, 1, W)

    dx4, loss = loss_fwd_bwd(x4, target)

    dx3 = _ffn_bwd(dx4, 1, W, ffn1, G)
    dycat1 = _mm(dx3, W["b_w_out"][0], tb=True, name="b_out_dx")
    G["b_w_out"] = _mm(ycat1, dx3, ta=True, name="b_out_dw")[None]
    dq_mem1, dkn1, dvm1, dqg1 = mem_attn_bwd(pb, DIL_WIDTH // MEM_WIDTH, memkv[1][0], memkv[1][1], qg1, dycat1, 1,
                                             name="mem_attn_bwd1")
    G["mem_q_norm"][1] = dqg1[0]
    d_os_ls = mix_bwd(os_, ls, dycat1)
    dqs, dks, dvs = [], [], []
    for gi, (win, dil) in enumerate(DIL_GROUPS):
        qg_, kg_, vg_, o_r, l_r = grp[gi]
        do_r, dl_r = _to_residues(d_os_ls[gi], dil), _to_residues(d_os_ls[3 + gi], dil)
        dq_r, dk_r, dv_r = dil_attn_bwd(qg_, kg_, vg_, o_r, l_r, do_r, dl_r, T // dil // DIL_BLOCK, name=f"dil_bwd{gi}")
        dqs.append(_from_residues(dq_r, dil))
        dks.append(_from_residues(dk_r, dil))
        dvs.append(_from_residues(dv_r, dil))
    dq, dk, dv = (jnp.concatenate(z, axis=1) for z in (dqs, dks, dvs))
    dpb, dkvp, dqn_g, dkn_g = qk_bwd(kvp, pb, kg_t, qg_t, cos, sin, dq, dk, dv, dq_mem1)
    G["b_q_norm"] = dqn_g
    G["kv_k_norm"] = dkn_g[0]
    dh1 = _mm(dpb, W["b_w_in"][0], tb=True, name="b_in_dx")
    G["b_w_in"] = _mm(h1, dpb, ta=True, name="b_in_dw")[None]
    dhkv = _mm(dkvp, W["kv_w"], tb=True, name="kv_in_dx")
    G["kv_w"] = _mm(hkv, dkvp, ta=True, name="kv_in_dw")
    dx2, dg1, dgkv = rms_bwd(x2, [W["attn_norm"][1:2], W["kv_norm"].reshape(1, -1)], [dh1, dhkv], dx3,
                             name="attn_rms_bwd1")
    G["attn_norm"][1] = dg1[0]
    G["kv_norm"] = dgkv[0]

    dx1 = _ffn_bwd(dx2, 0, W, ffn0, G)
    dycat0 = _mm(dx1, W["a_w_out"][0], tb=True, name="a_out_dx")
    G["a_w_out"] = _mm(ycat0, dx1, ta=True, name="a_out_dw")[None]
    dq_mem0, dkn0, dvm0, dqg0 = mem_attn_bwd(p, SHIFT_WIDTH // MEM_WIDTH, memkv[0][0], memkv[0][1], qg0, dycat0,
                                             RWKV_WIDTH // MEM_WIDTH, name="mem_attn_bwd0")
    G["mem_q_norm"][0] = dqg0[0]
    dy, dr_b, dk2_b, dv_b, dg, dlw, dlb, drk = mix_gate_bwd(y, r, k2, v, g, dycat0, lnx_w, lnx_b, r_k)
    for i, (dkn, dvm) in enumerate(((dkn0, dvm0), (dkn1, dvm1))):
        dwkv, dgm, dkg = mem_kv_bwd(mem, W["mem_norm"][i:i + 1], W["mem_w_kv"][i],
                                    _tile_heads(W["mem_k_norm"][i], MEM_WIDTH), dkn, dvm, name=f"mem_kv_bwd{i}")
        G["mem_w_kv"][i], G["mem_norm"][i], G["mem_k_norm"][i] = dwkv, dgm[0], dkg[0]
    for n in list(G):
        if isinstance(G[n], list) and all(z is not None for z in G[n]):
            G[n] = jnp.stack(G[n], axis=0)
    late_out = None
    if late is None:
        dr, dw, dk2, dv, dkk, dkka = scan_bwd(r, w, k2, v, kk, kka, states, final_state, dy)
    else:
        pieces = late[2](G)
        dr, dw, dk2, dv, dkk, dkka, *received = scan_bwd(r, w, k2, v, kk, kka, states, final_state, dy, scatter=pieces)
        late_out = (received, pieces)
    dxs, dmu, dw0, da0, dw2p, da2p, dg2p, dk_k, dk_a = rwkv_pre_bwd(
        p, mu, w0, a0, w2p, a2p, g2p, k_k, k_a, (dr, dr_b), dw, (dk2, dk2_b), (dv, dv_b), dkk, dkka, dg)
    dp = shift_bwd(dxs, mu, dq_mem0)
    G.update(a_mu=dmu, a_w0=dw0, a_a0=da0, a_w2=dw2p[None, :64], a_a2=da2p[None, 64:128], a_g2=dg2p[None, 128:],
             a_k_k=dk_k, a_k_a=dk_a, a_r_k=drk.reshape(1, RWKV_HEADS, HEAD_DIM), a_lnx_w=dlw, a_lnx_b=dlb)
    dh0 = _mm(dp, W["a_w_in"][0], tb=True, name="a_in_dx")
    G["a_w_in"] = _mm(h0, dp, ta=True, name="a_in_dw")[None]
    grad_x, dg0 = rms_bwd(x, [W["attn_norm"][0:1]], [dh0], dx1, name="attn_rms_bwd0")
    G["attn_norm"][0] = dg0[0]
    G["attn_norm"] = jnp.stack(G["attn_norm"], axis=0)
    return loss, grad_x, G, late_out


HBM_SPEC = pl.BlockSpec(memory_space=pltpu.HBM)


def _mesh_pos():
    return lax.axis_index("x"), lax.axis_index("y"), lax.axis_index("c")


def _other_chips(x, y):
    return [(1 - x, y), (x, 1 - y), (1 - x, 1 - y)]


def _remote(send_sems, recv_sems, k, src, dst, to):
    return pltpu.make_async_remote_copy(src_ref=src, dst_ref=dst, send_sem=send_sems.at[k], recv_sem=recv_sems.at[k],
                                        device_id=to, device_id_type=MESH)


def _comm_call(body, name, ins, out_shape, n_remote):
    scratch = [pltpu.SemaphoreType.DMA((n_remote,)), pltpu.SemaphoreType.DMA((n_remote,))]
    return pl.pallas_call(body, name=name, in_specs=[HBM_SPEC] * len(ins), out_specs=[HBM_SPEC] * len(out_shape),
                          out_shape=out_shape, scratch_shapes=scratch)(*ins)


def comm_gather(wbig, wsm):
    def body(wb, ws, ob, os_, send_sems, recv_sems):
        x, y, c = _mesh_pos()
        s = 2 * x + y
        me, sibling = (x, y, c), (x, y, 1 - c)
        chips = _other_chips(x, y)
        rc = functools.partial(_remote, send_sems, recv_sems)
        first = []
        for j, (cx, cy) in enumerate(chips):
            first.append(rc(j, wb.at[c], ob.at[s, c], (cx, cy, c)))
            first.append(rc(6 + j, ws, os_.at[s], (cx, cy, c)))
        for cp in first:
            cp.start()
        passed = []
        for j, (cx, cy) in enumerate(chips):
            blk = ob.at[2 * cx + cy, c]
            rc(j, blk, blk, me).wait_recv()
            passed.append(rc(3 + j, blk, blk, sibling))
            passed[-1].start()
        for j, (cx, cy) in enumerate(chips):
            blk = ob.at[2 * cx + cy, 1 - c]
            rc(3 + j, blk, blk, me).wait_recv()
            sb = os_.at[2 * cx + cy]
            rc(6 + j, sb, sb, me).wait_recv()
        for cp in first + passed:
            cp.wait_send()

    out_shape = [jax.ShapeDtypeStruct((N_CHIPS, *wbig.shape), wbig.dtype),
                 jax.ShapeDtypeStruct((N_CHIPS, *wsm.shape), wsm.dtype)]
    return _comm_call(body, "comm_gather", [wbig, wsm], out_shape, 9)


def comm_pair_exchange(gb, gs):
    def body(gb_ref, gs_ref, rb_ref, rs_ref, send_sems, recv_sems):
        x, y, c = _mesh_pos()
        sibling = (x, y, 1 - c)
        rc = functools.partial(_remote, send_sems, recv_sems)
        cps = [rc(r, gb_ref.at[r, 1 - c], rb_ref.at[r], sibling) for r in range(N_CHIPS)]
        cps.append(rc(N_CHIPS, gs_ref.at[1 - c], rs_ref, sibling))
        for cp in cps:
            cp.start()
        for cp in cps:
            cp.wait()

    out_shape = [jax.ShapeDtypeStruct((N_CHIPS, *gb.shape[2:]), gb.dtype), jax.ShapeDtypeStruct(gs.shape[1:], gs.dtype)]
    return _comm_call(body, "comm_pair_exchange", [gb, gs], out_shape, N_CHIPS + 1)


def comm_chip_exchange(hb, hs):
    def body(hb_ref, hs_ref, qb_ref, qs_ref, send_sems, recv_sems):
        x, y, c = _mesh_pos()
        s = 2 * x + y
        me = (x, y, c)
        chips = _other_chips(x, y)
        rc = functools.partial(_remote, send_sems, recv_sems)
        cps = []
        for j, (cx, cy) in enumerate(chips):
            cps.append(rc(j, hb_ref.at[2 * cx + cy], qb_ref.at[s], (cx, cy, c)))
            cps.append(rc(3 + j, hs_ref, qs_ref.at[s], (cx, cy, c)))
        for cp in cps:
            cp.start()
        for j, (cx, cy) in enumerate(chips):
            blk = qb_ref.at[2 * cx + cy]
            rc(j, blk, blk, me).wait_recv()
            sb = qs_ref.at[2 * cx + cy]
            rc(3 + j, sb, sb, me).wait_recv()
        for cp in cps:
            cp.wait_send()

    out_shape = [jax.ShapeDtypeStruct(hb.shape, hb.dtype), jax.ShapeDtypeStruct((N_CHIPS, *hs.shape), hs.dtype)]
    return _comm_call(body, "comm_chip_exchange", [hb, hs], out_shape, 6)


def comm_pair_share(halves):
    n = len(halves)

    def body(*refs):
        x, y, c = _mesh_pos()
        send_sems, recv_sems = refs[2 * n], refs[2 * n + 1]
        cps = [_remote(send_sems, recv_sems, k, refs[k], refs[n + k], (x, y, 1 - c)) for k in range(n)]
        for cp in cps:
            cp.start()
        for cp in cps:
            cp.wait()

    out_shape = [jax.ShapeDtypeStruct(h.shape, h.dtype) for h in halves]
    return _comm_call(body, "comm_pair_share", list(halves), out_shape, n)


def add_pairs(a, b, out_dtype, *, name, tb):
    T, L = a.shape
    return _rowmap(lambda i, n, p, q: p + q, name=name, T=T, tb=tb, ins=[("row", a), ("row", b)],
                   outs=[("row", L, out_dtype)])[0]


def add_chips(parts, *, name, tb):
    T, L = parts[0].shape

    def fn(i, n, *ps):
        acc = ps[0].astype(F32)
        for p in ps[1:]:
            acc = acc + p.astype(F32)
        return acc

    return _rowmap(fn, name=name, T=T, tb=tb, ins=[("row", p) for p in parts], outs=[("row", L, F32)])[0]


def adamw(g, w, m, v, *, name, tb):
    T, L = g.shape

    def fn(i, n, g, w, m, v):
        m2 = ADAM_B1 * m + (1.0 - ADAM_B1) * g
        v2 = ADAM_B2 * v + (1.0 - ADAM_B2) * (g * g)
        m_hat = m2 / (1.0 - ADAM_B1 ** ADAM_STEP)
        v_hat = v2 / (1.0 - ADAM_B2 ** ADAM_STEP)
        return -ADAM_LR * (m_hat / (jnp.sqrt(v_hat) + ADAM_EPS) + ADAM_WD * w), m2, v2

    return _rowmap(fn, name=name, T=T, tb=tb, ins=[("row", z) for z in (g, w, m, v)], outs=[("row", L, F32)] * 3)


BIG_LANES = 1024
SMALL_LANES = 128


def _flat_cat(arrs, total, dtype):
    parts = [a.reshape(-1).astype(dtype) for a in arrs]
    n = sum(p.shape[0] for p in parts)
    assert n <= total, (n, total)
    if n < total:
        parts.append(jnp.zeros((total - n,), dtype))
    return jnp.concatenate(parts)


def _split_flat(flat, shapes):
    out, off = [], 0
    for shp in shapes:
        n = math.prod(shp)
        out.append(flat[off:off + n].reshape(shp))
        off += n
    return out


def _round_up(n, m):
    return -(-n // m) * m


def _full_shape(shard_shape, axis):
    return tuple(d * N_CHIPS if i == axis else d for i, d in enumerate(shard_shape))


def kernel(x, mem, attn_norm, a_w_in, a_mu, a_w0, a_w2, a_a0, a_a2, a_g2, a_k_k, a_k_a, a_r_k, a_lnx_w, a_lnx_b, a_w_out, kv_norm, kv_w, kv_k_norm, b_w_in, b_q_norm, b_w_out, mem_norm, mem_w_kv, mem_q_norm, mem_k_norm, ffn_norm, ffn_w_up, ffn_conv_w, ffn_conv_b, ffn_w_down, loss_target, m_attn_norm, m_a_w_in, m_a_mu, m_a_w0, m_a_w2, m_a_a0, m_a_a2, m_a_g2, m_a_k_k, m_a_k_a, m_a_r_k, m_a_lnx_w, m_a_lnx_b, m_a_w_out, m_kv_norm, m_kv_w, m_kv_k_norm, m_b_w_in, m_b_q_norm, m_b_w_out, m_mem_norm, m_mem_w_kv, m_mem_q_norm, m_mem_k_norm, m_ffn_norm, m_ffn_w_up, m_ffn_conv_w, m_ffn_conv_b, m_ffn_w_down, v_attn_norm, v_a_w_in, v_a_mu, v_a_w0, v_a_w2, v_a_a0, v_a_a2, v_a_g2, v_a_k_k, v_a_k_a, v_a_r_k, v_a_lnx_w, v_a_lnx_b, v_a_w_out, v_kv_norm, v_kv_w, v_kv_k_norm, v_b_w_in, v_b_q_norm, v_b_w_out, v_mem_norm, v_mem_w_kv, v_mem_q_norm, v_mem_k_norm, v_ffn_norm, v_ffn_w_up, v_ffn_conv_w, v_ffn_conv_b, v_ffn_w_down):
    args = (attn_norm, a_w_in, a_mu, a_w0, a_w2, a_a0, a_a2, a_g2, a_k_k, a_k_a, a_r_k, a_lnx_w, a_lnx_b, a_w_out, kv_norm, kv_w, kv_k_norm, b_w_in, b_q_norm, b_w_out, mem_norm, mem_w_kv, mem_q_norm, mem_k_norm, ffn_norm, ffn_w_up, ffn_conv_w, ffn_conv_b, ffn_w_down)
    ms = (m_attn_norm, m_a_w_in, m_a_mu, m_a_w0, m_a_w2, m_a_a0, m_a_a2, m_a_g2, m_a_k_k, m_a_k_a, m_a_r_k, m_a_lnx_w, m_a_lnx_b, m_a_w_out, m_kv_norm, m_kv_w, m_kv_k_norm, m_b_w_in, m_b_q_norm, m_b_w_out, m_mem_norm, m_mem_w_kv, m_mem_q_norm, m_mem_k_norm, m_ffn_norm, m_ffn_w_up, m_ffn_conv_w, m_ffn_conv_b, m_ffn_w_down)
    vs = (v_attn_norm, v_a_w_in, v_a_mu, v_a_w0, v_a_w2, v_a_a0, v_a_a2, v_a_g2, v_a_k_k, v_a_k_a, v_a_r_k, v_a_lnx_w, v_a_lnx_b, v_a_w_out, v_kv_norm, v_kv_w, v_kv_k_norm, v_b_w_in, v_b_q_norm, v_b_w_out, v_mem_norm, v_mem_w_kv, v_mem_q_norm, v_mem_k_norm, v_ffn_norm, v_ffn_w_up, v_ffn_conv_w, v_ffn_conv_b, v_ffn_w_down)
    w_sh, m_sh, v_sh = (dict(zip(WEIGHTS, z)) for z in (args, ms, vs))
    xi, yi, ci = _mesh_pos()
    chip = 2 * xi + yi
    axes = {**dict(BIG), **dict(SMALL_SHARDED)}
    early_names = [n for n, _ in BIG if n in EARLY_BIG]
    late_names = [n for n, _ in BIG if n not in EARLY_BIG and n != NATURAL_BIG]
    ss_names, ss_axes = [n for n, _ in SMALL_SHARDED], dict(SMALL_SHARDED)
    shapes_of = lambda names: [w_sh[n].shape for n in names]
    count = lambda names: sum(math.prod(s) for s in shapes_of(names))
    n_early, n_late = count(early_names), count(late_names)
    assert n_early % (2 * 16 * BIG_LANES) == 0 and n_late % (2 * 16 * BIG_LANES) == 0
    mh, mh_late = n_early // (2 * BIG_LANES), n_late // (2 * BIG_LANES)
    n_ss = _round_up(count(ss_names), 8 * SMALL_LANES)

    def shard_pack(names, total, dtype, source):
        return _flat_cat([source[n] for n in names], total, dtype)

    def unshard(names, gathered):
        per_chip = [_split_flat(gathered[j], shapes_of(names)) for j in range(N_CHIPS)]
        return {n: jnp.concatenate([per_chip[j][k] for j in range(N_CHIPS)], axis=axes[n]) for k, n in enumerate(names)}

    def by_chip(names, total, dtype, grads):
        parts = [jnp.split(grads[n], N_CHIPS, axis=axes[n]) for n in names]
        return jnp.stack([_flat_cat([p[j] for p in parts], total, dtype) for j in range(N_CHIPS)])

    wbig = shard_pack(early_names, n_early, MMD, w_sh).reshape(2, mh, BIG_LANES)
    wsm = shard_pack(ss_names, n_ss, F32, w_sh).reshape(-1, SMALL_LANES)
    wbig_all, wsm_all = comm_gather(wbig, wsm)
    wbig_all = lax.dynamic_update_index_in_dim(wbig_all, wbig, chip, 0).reshape(N_CHIPS, -1)
    wsm_all = lax.dynamic_update_index_in_dim(wsm_all, wsm, chip, 0).reshape(N_CHIPS, -1)
    W = {n: w_sh[n] for n in SMALL_REPL}
    W.update(unshard(early_names, wbig_all))
    W.update(unshard(ss_names, wsm_all))
    for n in ("a_w2", "a_a2", "a_g2"):
        W[n] = W[n].astype(MMD)
    wlate = shard_pack(late_names, n_late, MMD, w_sh).reshape(2, mh_late, BIG_LANES)
    nat_axis = axes[NATURAL_BIG]
    wnat = w_sh[NATURAL_BIG].astype(MMD)
    assert wnat.shape[0] == 2 and nat_axis != 0

    def unpack_late(gathered):
        full = lax.dynamic_update_index_in_dim(gathered[0], wlate, chip, 0)
        out = unshard(late_names, full.reshape(N_CHIPS, -1))
        nat = lax.dynamic_update_index_in_dim(gathered[1], wnat, chip, 0)
        out[NATURAL_BIG] = jnp.concatenate([nat[j] for j in range(N_CHIPS)], axis=nat_axis)
        return out

    def pack_late(grads):
        return [by_chip(late_names, n_late, BF16, grads).reshape(N_CHIPS, 2, mh_late, BIG_LANES),
                jnp.stack(jnp.split(grads[NATURAL_BIG].astype(BF16), N_CHIPS, axis=nat_axis))]

    loss_blk, grad_x, G, (received, pieces) = local_step(x[0], mem[0], loss_target[0], W,
                                                          late=([wlate, wnat], unpack_late, pack_late))
    loss = lax.psum(loss_blk[0, 0], ("x", "y", "c"))

    own_piece = lambda p: lax.dynamic_index_in_dim(lax.dynamic_index_in_dim(p, chip, 0, keepdims=False), ci, 0,
                                                   keepdims=False)
    gh_late = add_chips([received[0][k] for k in range(len(PEER_FLIPS))] + [own_piece(pieces[0])],
                        name="add_pieces_late", tb=32)
    gh_nat = add_chips([received[1][k] for k in range(len(PEER_FLIPS))] + [own_piece(pieces[1])],
                       name="add_pieces_natural", tb=32)
    gbig = by_chip(early_names, n_early, F32, G).reshape(N_CHIPS, 2, mh, BIG_LANES)
    sm_full_names = ss_names + list(SMALL_REPL)
    sm_full_shapes = [_full_shape(w_sh[n].shape, ss_axes[n]) for n in ss_names] + [w_sh[n].shape for n in SMALL_REPL]
    n_smf = _round_up(sum(math.prod(s) for s in sm_full_shapes), 2 * 8 * SMALL_LANES)
    msh = n_smf // (2 * SMALL_LANES)
    gsm = _flat_cat([G[n] for n in sm_full_names], n_smf, F32).reshape(2, msh, SMALL_LANES)
    rb, rs = comm_pair_exchange(gbig, gsm)
    mine_b = lax.dynamic_index_in_dim(gbig, ci, axis=1, keepdims=False)
    mine_s = lax.dynamic_index_in_dim(gsm, ci, axis=0, keepdims=False)
    hb = add_pairs(mine_b.reshape(-1, BIG_LANES), rb.reshape(-1, BIG_LANES), BF16, name="add_pairs_big", tb=128)
    hs = add_pairs(mine_s, rs, F32, name="add_pairs_small", tb=msh)
    hb = hb.reshape(N_CHIPS, mh, BIG_LANES)
    qb, qs = comm_chip_exchange(hb, hs)
    qb = lax.dynamic_update_index_in_dim(qb, lax.dynamic_index_in_dim(hb, chip, 0, keepdims=False), chip, 0)
    qs = lax.dynamic_update_index_in_dim(qs, hs, chip, 0)
    gh = add_chips([qb[j] for j in range(N_CHIPS)], name="add_chips_big", tb=32)
    gsh = add_chips([qs[j] for j in range(N_CHIPS)], name="add_chips_small", tb=msh)
    rh, rh_late, rh_nat, rsh = comm_pair_share([gh, gh_late, gh_nat, gsh])
    both = lambda mine_, theirs: jnp.where(ci == 0, jnp.stack([mine_, theirs]), jnp.stack([theirs, mine_]))
    gfull, gfull_late, gfull_nat, gsfull = both(gh, rh), both(gh_late, rh_late), both(gh_nat, rh_nat), both(gsh, rsh)

    res = {tag: {} for tag in ("grad", "delta", "new_m", "new_v")}
    big_grads = (list(zip(early_names, _split_flat(gfull.reshape(-1), shapes_of(early_names))))
                 + list(zip(late_names, _split_flat(gfull_late.reshape(-1), shapes_of(late_names))))
                 + [(NATURAL_BIG, gfull_nat)])
    for n, g in big_grads:
        shp = w_sh[n].shape
        rows = lambda z: z.reshape(-1, shp[-1])
        nrow = math.prod(shp[:-1])
        tb = next(t for t in (512, 256, 128, 64) if nrow % t == 0 and t * shp[-1] <= (1 << 19))
        outs = adamw(rows(g), rows(w_sh[n]), rows(m_sh[n]), rows(v_sh[n]), name=f"adamw_{n}", tb=tb)
        res["grad"][n] = g
        for tag, o in zip(("delta", "new_m", "new_v"), outs):
            res[tag][n] = o.reshape(shp)
    sm_full = dict(zip(sm_full_names, _split_flat(gsfull.reshape(-1), sm_full_shapes)))
    g_loc = {}
    for n in ss_names:
        size = w_sh[n].shape[ss_axes[n]]
        g_loc[n] = lax.dynamic_slice_in_dim(sm_full[n], chip * size, size, axis=ss_axes[n])
    for n in SMALL_REPL:
        g_loc[n] = sm_full[n]
    n_sml = _round_up(sum(math.prod(w_sh[n].shape) for n in sm_full_names), 8 * SMALL_LANES)
    pack_sm = lambda d: _flat_cat([d[n] for n in sm_full_names], n_sml, F32).reshape(-1, SMALL_LANES)
    d_sm, m_sm, v_sm = adamw(pack_sm(g_loc), pack_sm(w_sh), pack_sm(m_sh), pack_sm(v_sh), name="adamw_small",
                             tb=n_sml // SMALL_LANES)
    sm_loc_shapes = [w_sh[n].shape for n in sm_full_names]
    res["grad"].update(g_loc)
    for tag, smv in (("delta", d_sm), ("new_m", m_sm), ("new_v", v_sm)):
        res[tag].update(dict(zip(sm_full_names, _split_flat(smv.reshape(-1), sm_loc_shapes))))
    return (loss, grad_x[None], *[res[tag][n] for tag in ("grad", "delta", "new_m", "new_v") for n in WEIGHTS])
```

```python
import functools
import math

import numpy as np
import jax
import jax.numpy as jnp
from jax import lax
from jax.experimental import pallas as pl
from jax.experimental.pallas import tpu as pltpu

F32 = jnp.float32
BF16 = jnp.bfloat16
MMD = jnp.bfloat16

D_MODEL = 1024
HEAD_DIM = 64
N_MEM = 256
MEM_WIDTH = 256
RWKV_HEADS = 12
RWKV_WIDTH = 768
SHIFT_WIDTH = 2560
LORA_WIDTH = 256
DIL_WIDTH = 768
DIL_GROUPS = ((128, 1), (512, 4), (2048, 16))
DIL_BLOCK = 128
D_FF = 2816
ROPE_THETA = 10000.0
RMS_EPS = 1e-6
LNX_EPS = 64e-5
NEG_INF = -1e30
ADAM_LR = 0.001
ADAM_B1 = 0.9
ADAM_B2 = 0.999
ADAM_EPS = 1e-08
ADAM_WD = 0.01
ADAM_STEP = 10
N_CHIPS = 4
MESH = pl.DeviceIdType.MESH
VMEM_LIMIT_MB = 56
SCAN_CHUNK = 32
SCAN_UNROLL = 4
SCAN_UNROLL_BWD = 2

BIG = (("a_w_in", 2), ("a_w_out", 1), ("kv_w", 1), ("b_w_in", 1), ("b_w_out", 2), ("mem_w_kv", 1),
       ("ffn_w_up", 2), ("ffn_w_down", 1))
EARLY_BIG = ("a_w_in",)
NATURAL_BIG = "ffn_w_up"
SMALL_SHARDED = (("a_mu", 1), ("a_w0", 1), ("a_w2", 2), ("a_a0", 1), ("a_a2", 2), ("a_g2", 2), ("a_k_k", 1),
                 ("a_k_a", 1), ("a_lnx_w", 1), ("a_lnx_b", 1), ("ffn_conv_w", 2))
SMALL_REPL = ("attn_norm", "a_r_k", "kv_norm", "kv_k_norm", "b_q_norm", "mem_norm", "mem_q_norm", "mem_k_norm",
              "ffn_norm", "ffn_conv_b")
WEIGHTS = ("attn_norm", "a_w_in", "a_mu", "a_w0", "a_w2", "a_a0", "a_a2", "a_g2", "a_k_k", "a_k_a", "a_r_k",
           "a_lnx_w", "a_lnx_b", "a_w_out", "kv_norm", "kv_w", "kv_k_norm", "b_w_in", "b_q_norm", "b_w_out",
           "mem_norm", "mem_w_kv", "mem_q_norm", "mem_k_norm", "ffn_norm", "ffn_w_up", "ffn_conv_w", "ffn_conv_b",
           "ffn_w_down")


def _cp(sem=None, **kw):
    return pltpu.CompilerParams(dimension_semantics=sem, vmem_limit_bytes=VMEM_LIMIT_MB << 20, **kw)


def _tile(n, cands=(512, 256, 128)):
    for c in cands:
        if n % c == 0:
            return c
    return n


def _mm(a, b, *, name, ta=False, tb=False, add=None, out_dtype=F32):
    K, M = a.shape if ta else a.shape[::-1]
    N = b.shape[0] if tb else b.shape[1]
    assert K == (b.shape[1] if tb else b.shape[0])
    tm, tn = _tile(M, (512, 256, 128) if ta else (1024, 512, 256, 128)), _tile(N, (512, 1408, 256, 128))
    mi, ni = ((lambda o, i: i), (lambda o, i: o)) if ta else ((lambda o, i: o), (lambda o, i: i))
    grid = (N // tn, M // tm) if ta else (M // tm, N // tn)
    a_blk, b_blk = ((K, tm) if ta else (tm, K)), ((tn, K) if tb else (K, tn))
    a_spec = pl.BlockSpec(a_blk, (lambda o, i: (0, mi(o, i))) if ta else (lambda o, i: (mi(o, i), 0)))
    b_spec = pl.BlockSpec(b_blk, (lambda o, i: (ni(o, i), 0)) if tb else (lambda o, i: (0, ni(o, i))))
    o_spec = pl.BlockSpec((tm, tn), lambda o, i: (mi(o, i), ni(o, i)))
    dn = (((0,) if ta else (1,), (1,) if tb else (0,)), ((), ()))
    has_add = add is not None
    kept = 1 if ta else 0
    cache = (b if ta else a).dtype != MMD

    def body(*refs):
        vals = [refs[0], refs[1]]
        o_ref = refs[2 + has_add]
        if cache:
            scr = refs[-1]

            @pl.when(pl.program_id(1) == 0)
            def _():
                scr[...] = vals[kept][...].astype(MMD)

            vals[kept] = scr
        acc = lax.dot_general(vals[0][...].astype(MMD), vals[1][...].astype(MMD), dn, preferred_element_type=F32)
        if has_add:
            acc = acc + refs[2][...]
        o_ref[...] = acc.astype(o_ref.dtype)

    ins = [a, b] + ([add] if has_add else [])
    specs = [a_spec, b_spec] + ([o_spec] if has_add else [])
    return pl.pallas_call(
        body, name=name, grid=grid, in_specs=specs, out_specs=o_spec,
        out_shape=jax.ShapeDtypeStruct((M, N), out_dtype),
        scratch_shapes=[pltpu.VMEM(b_blk if ta else a_blk, MMD)] if cache else [],
        compiler_params=_cp(("parallel", "arbitrary")),
    )(*ins)


def _rowmap(fn, *, name, T, tb, ins, outs, accs=()):
    nblk = T // tb
    assert T % tb == 0 and tb % 8 == 0
    in_specs, args = [], []
    for spec in ins:
        kind, arr = spec[0], spec[1]
        w, cb = (spec[2], spec[3]) if len(spec) > 2 else (arr.shape[-1], 0)
        if kind == "row":
            in_specs.append(pl.BlockSpec((tb, w), lambda i, cb=cb: (i, cb)))
        elif kind == "prev":
            in_specs.append(pl.BlockSpec((8, w), lambda i, cb=cb: (jnp.maximum(i * (tb // 8) - 1, 0), cb)))
        elif kind == "next":
            in_specs.append(pl.BlockSpec((8, w), lambda i, cb=cb: (jnp.minimum((i + 1) * (tb // 8), T // 8 - 1), cb)))
        elif kind == "const":
            in_specs.append(pl.BlockSpec(arr.shape, lambda i, nd=arr.ndim: (0,) * nd))
        else:
            raise ValueError(kind)
        args.append(arr)
    out_shape, out_specs = [], []
    for kind, w, dt in outs:
        out_shape.append(jax.ShapeDtypeStruct((T, w), dt))
        out_specs.append(pl.BlockSpec((tb, w), lambda i: (i, 0)))
    for shp, dt in accs:
        out_shape.append(jax.ShapeDtypeStruct(shp, dt))
        out_specs.append(pl.BlockSpec(shp, lambda i, nd=len(shp): (0,) * nd))
    n_in, n_out = len(ins), len(outs)

    def body(*refs):
        i = pl.program_id(0)
        vals = [r[...] for r in refs[:n_in]]
        res = fn(i, nblk, *vals)
        if not isinstance(res, (tuple, list)):
            res = (res,)
        assert len(res) == n_out + len(accs), (name, len(res))
        for r, v in zip(refs[n_in:n_in + n_out], res[:n_out]):
            r[...] = v.astype(r.dtype)
        acc_refs = refs[n_in + n_out:]
        if acc_refs:
            @pl.when(i == 0)
            def _():
                for r in acc_refs:
                    r[...] = jnp.zeros(r.shape, r.dtype)

            for r, v in zip(acc_refs, res[n_out:]):
                r[...] += v

    res = pl.pallas_call(
        body, name=name, grid=(nblk,), in_specs=in_specs, out_specs=out_specs, out_shape=out_shape,
        compiler_params=_cp(("arbitrary",)),
    )(*args)
    return res


def _row_pick(halo, r):
    rid = lax.broadcasted_iota(jnp.int32, halo.shape, 0)
    return jnp.sum(jnp.where(rid == r, halo, 0.0), axis=0, keepdims=True)


def _shift_down(x, row_before, is_first):
    rid = lax.broadcasted_iota(jnp.int32, x.shape, 0)
    first = jnp.where(is_first, 0.0, 1.0) * row_before
    return jnp.where(rid == 0, first, pltpu.roll(x, 1, axis=0))


def _shift_up(x, row_after, is_last):
    n = x.shape[0]
    rid = lax.broadcasted_iota(jnp.int32, x.shape, 0)
    last = jnp.where(is_last, 0.0, 1.0) * row_after
    return jnp.where(rid == n - 1, last, pltpu.roll(x, n - 1, axis=0))


def _dot(a, b, dn=(((1,), (0,)), ((), ()))):
    return lax.dot_general(a.astype(MMD), b.astype(MMD), dn, preferred_element_type=F32)


def _dot_nt(a, b):
    return _dot(a, b, (((1,), (1,)), ((), ())))


def _dot_tn(a, b):
    return _dot(a, b, (((0,), (0,)), ((), ())))


def _dot_exact01(x, g01):
    hi = x.astype(BF16)
    lo = (x - hi.astype(F32)).astype(BF16)
    dn = (((1,), (0,)), ((), ()))
    return (lax.dot_general(hi, g01, dn, preferred_element_type=F32)
            + lax.dot_general(lo, g01, dn, preferred_element_type=F32))


def _fold_heads(v, fold):
    return _row_pick(_dot_exact01(jnp.broadcast_to(v, (8, v.shape[1])), fold), 0)


def _group_ones(width):
    idx = np.arange(width) // HEAD_DIM
    return jnp.asarray((idx[:, None] == idx[None, :]).astype(np.float32), BF16)


def _fold_ones(width):
    idx = np.arange(width) % HEAD_DIM
    return jnp.asarray((idx[:, None] == np.arange(HEAD_DIM)[None, :]).astype(np.float32), BF16)


def _head_masks(width):
    idx = np.arange(width) // HEAD_DIM
    return jnp.asarray((idx[None, :] == np.arange(width // HEAD_DIM)[:, None]).astype(np.float32)[:, None, :], F32)


def _rms_stats(x):
    r = lax.rsqrt(jnp.mean(x * x, axis=-1, keepdims=True) + RMS_EPS)
    return r, x * r


def rms_fwd(x, gains, *, name):
    T, D = x.shape

    def fn(i, nblk, xb, *gs):
        _, xh = _rms_stats(xb)
        return tuple(xh * g for g in gs)

    return _rowmap(fn, name=name, T=T, tb=512, ins=[("row", x)] + [("const", g) for g in gains],
                   outs=[("row", D, MMD)] * len(gains))


def rms_bwd(x, gains, dhs, dres, *, name):
    T, D = x.shape
    n = len(gains)

    def fn(i, nblk, xb, dr, *rest):
        gs, ds = rest[:n], rest[n:]
        r, xh = _rms_stats(xb)
        dx = dr
        dgs = []
        for g, dh in zip(gs, ds):
            dgs.append(jnp.sum(dh * xh, axis=0, keepdims=True))
            dxh = dh * g
            dx = dx + r * (dxh - xh * jnp.mean(dxh * xh, axis=-1, keepdims=True))
        return (dx, *dgs)

    return _rowmap(fn, name=name, T=T, tb=512,
                   ins=[("row", x), ("row", dres)] + [("const", g) for g in gains] + [("row", d) for d in dhs],
                   outs=[("row", D, F32)], accs=[((1, D), F32)] * n)


def _segsum(x):
    first = lax.broadcasted_iota(jnp.int32, (x.shape[0], 128), 1) < HEAD_DIM
    outs = []
    for p in range(x.shape[1] // 128):
        xs = x[:, p * 128:(p + 1) * 128]
        lo = jnp.sum(jnp.where(first, xs, 0.0), axis=-1, keepdims=True)
        hi = jnp.sum(jnp.where(first, 0.0, xs), axis=-1, keepdims=True)
        outs.append(jnp.where(first, lo, hi))
    return jnp.concatenate(outs, axis=1)


def _pre1_common(i, ps, halo, mu, w0, a0, w2p, a2p, g2p, k_k, k_a):
    prev = _shift_down(ps, _row_pick(halo, 7), i == 0)
    xs = ps + (prev - ps) * mu
    lo = xs[:, 3 * RWKV_WIDTH:]
    tl, sl = jnp.tanh(lo), jax.nn.sigmoid(lo)
    dec = w0 + _dot(tl, w2p)
    ain = a0 + _dot(lo, a2p)
    g = _dot(sl, g2p)
    wl = -jax.nn.softplus(-dec) - 0.5
    w = jnp.exp(-jnp.exp(wl))
    a = jax.nn.sigmoid(ain)
    k = xs[:, RWKV_WIDTH:2 * RWKV_WIDTH]
    z = k * k_k
    nrm = jnp.sqrt(_segsum(z * z))
    kk = z / jnp.maximum(nrm, 1e-12)
    return prev, xs, lo, tl, sl, dec, wl, w, a, g, k, nrm, kk


def rwkv_pre_fwd(p, mu, w0, a0, w2p, a2p, g2p, k_k, k_a):
    T = p.shape[0]

    def fn(i, nblk, ps, halo, mu, w0, a0, w2p, a2p, g2p, k_k, k_a):
        _, xs, _, _, _, _, _, w, a, g, k, _, kk = _pre1_common(i, ps, halo, mu, w0, a0, w2p, a2p, g2p, k_k, k_a)
        W = RWKV_WIDTH
        return xs[:, :W], w, k * (1.0 + (a - 1.0) * k_a), xs[:, 2 * W:3 * W], kk, kk * a, g

    return _rowmap(fn, name="rwkv_pre_fwd", T=T, tb=256,
                   ins=[("row", p, SHIFT_WIDTH, 0), ("prev", p, SHIFT_WIDTH, 0)]
                   + [("const", c) for c in (mu, w0, a0, w2p, a2p, g2p, k_k, k_a)],
                   outs=[("row", RWKV_WIDTH, F32)] * 7)


def rwkv_pre_bwd(p, mu, w0, a0, w2p, a2p, g2p, k_k, k_a, drs, dw, dk2s, dvs, dkk, dkka, dg):
    T = p.shape[0]

    def fn(i, nblk, ps, halo, mu, w0, a0, w2p, a2p, g2p, k_k, k_a, dr0, dr1, dw, dk20, dk21, dv0, dv1, dkk, dkka, dg):
        prev, xs, lo, tl, sl, dec, wl, w, a, g, k, nrm, kk = _pre1_common(i, ps, halo, mu, w0, a0, w2p, a2p, g2p, k_k, k_a)
        dk2 = dk20 + dk21
        dkk_t = dkk + dkka * a
        proj = jnp.where(nrm > 1e-12, kk * _segsum(dkk_t * kk), 0.0)
        dz = (dkk_t - proj) / jnp.maximum(nrm, 1e-12)
        dk = dz * k_k + dk2 * (1.0 + (a - 1.0) * k_a)
        da = dkka * kk + dk2 * k * k_a
        ddec = dw * (-w * jnp.exp(wl)) * jax.nn.sigmoid(-dec)
        dain = da * a * (1.0 - a)
        dlo = (_dot_nt(ddec, w2p) * (1.0 - tl * tl) + _dot_nt(dain, a2p) + _dot_nt(dg, g2p) * sl * (1.0 - sl))
        dxs = jnp.concatenate([dr0 + dr1, dk, dv0 + dv1, dlo], axis=1)
        s = lambda z: jnp.sum(z, axis=0, keepdims=True)
        return (dxs, s(dxs * (prev - ps)), s(ddec), s(dain), _dot_tn(tl, ddec), _dot_tn(lo, dain), _dot_tn(sl, dg),
                s(dz * k), s(dk2 * k * (a - 1.0)))

    return _rowmap(fn, name="rwkv_pre_bwd", T=T, tb=128,
                   ins=[("row", p, SHIFT_WIDTH, 0), ("prev", p, SHIFT_WIDTH, 0)]
                   + [("const", c) for c in (mu, w0, a0, w2p, a2p, g2p, k_k, k_a)]
                   + [("row", c) for c in (*drs, dw, *dk2s, *dvs, dkk, dkka, dg)],
                   outs=[("row", SHIFT_WIDTH, F32)],
                   accs=[((1, SHIFT_WIDTH), F32), ((1, RWKV_WIDTH), F32), ((1, RWKV_WIDTH), F32)]
                   + [((LORA_WIDTH, RWKV_WIDTH), F32)] * 3 + [((1, RWKV_WIDTH), F32)] * 2)


def shift_bwd(dxs, mu, dq_mem):
    T = dxs.shape[0]

    def fn(i, nblk, d, halo, mu, dq):
        nxt = _shift_up(d, _row_pick(halo, 0), i == nblk - 1)
        return jnp.concatenate([d * (1.0 - mu) + nxt * mu, dq], axis=1)

    return _rowmap(fn, name="shift_bwd", T=T, tb=256,
                   ins=[("row", dxs), ("next", dxs), ("const", mu), ("row", dq_mem)],
                   outs=[("row", SHIFT_WIDTH + MEM_WIDTH, MMD)])[0]


N_PAIRS = RWKV_HEADS // 2


def _pair_consts():
    row = lax.broadcasted_iota(jnp.int32, (HEAD_DIM, 128), 0)
    lane = lax.broadcasted_iota(jnp.int32, (HEAD_DIM, 128), 1)
    eye2 = jnp.logical_or(lane == row, lane == row + HEAD_DIM).astype(F32)
    li = lax.broadcasted_iota(jnp.int32, (128, 128), 0) < HEAD_DIM
    lj = lax.broadcasted_iota(jnp.int32, (128, 128), 1) < HEAD_DIM
    return eye2, (li == lj).astype(BF16)


def _pair_sum(p, ones2):
    n, m, l = p.shape
    s = lax.dot_general(p.reshape(n * m, l).astype(BF16), ones2, (((1,), (0,)), ((), ())), preferred_element_type=F32)
    return s.reshape(n, m, l)


def _pair_rows(row):
    return jnp.stack([row[:, p * 128:(p + 1) * 128] for p in range(N_PAIRS)], axis=0)


def _pair_flat(rows):
    return jnp.concatenate([rows[p] for p in range(N_PAIRS)], axis=1)


def _split_bf16(v):
    hi = v.astype(BF16).astype(F32)
    return hi, v - hi


def _gather_copies(srcs, dsts, send_sems, recv_sems):
    x, y, c = _mesh_pos()
    s = 2 * x + y
    me, sibling = (x, y, c), (x, y, 1 - c)
    rc = functools.partial(_remote, send_sems, recv_sems)
    ici, land, fwd, arrived = [], [], [], []
    for b, (src, dst) in enumerate(zip(srcs, dsts)):
        for j, (cx, cy) in enumerate(_other_chips(x, y)):
            k = 6 * b + j
            ici.append(rc(k, src.at[c], dst.at[s, c], (cx, cy, c)))
            blk, blk2 = dst.at[2 * cx + cy, c], dst.at[2 * cx + cy, 1 - c]
            land.append(rc(k, blk, blk, me))
            fwd.append(rc(k + 3, blk, blk, sibling))
            arrived.append(rc(k + 3, blk2, blk2, me))
    return ici, land, fwd, arrived


def scan_fwd(r, w, k2, v, kk, kka, gather=None):
    T, W = r.shape
    tc = SCAN_CHUNK
    nchunk = T // tc
    seq = pl.BlockSpec((tc, W), lambda i: (i, 0))
    one_state = pl.BlockSpec((N_PAIRS, HEAD_DIM, 128), lambda i: (0, 0, 0))
    nb = 0 if gather is None else len(gather)

    def body(r_ref, w_ref, k2_ref, v_ref, kk_ref, kka_ref, *rest):
        if gather is None:
            y_ref, st_ref, fin_ref, s_scr, vhi_scr, vlo_scr = rest
        else:
            srcs, (y_ref, st_ref, fin_ref), dsts = rest[:nb], rest[nb:nb + 3], rest[nb + 3:2 * nb + 3]
            s_scr, vhi_scr, vlo_scr, send_sems, recv_sems = rest[2 * nb + 3:]
            ici, land, fwd, arrived = _gather_copies(srcs, dsts, send_sems, recv_sems)

            @pl.when(pl.program_id(0) == 0)
            def _():
                for cp in ici:
                    cp.start()

            @pl.when(pl.program_id(0) == nchunk // 2)
            def _():
                for a, f in zip(land, fwd):
                    a.wait_recv()
                    f.start()

        @pl.when(pl.program_id(0) == 0)
        def _():
            s_scr[...] = jnp.zeros(s_scr.shape, F32)

        vhi_scr[...], vlo_scr[...] = _split_bf16(v_ref[...])
        eye2, ones2 = _pair_consts()

        def step(t, carry):
            r_t, w_t, k2_t, kk_t, kka_t, vhi_t, vlo_t = (
                _pair_rows(ref[pl.ds(t, 1), :]) for ref in (r_ref, w_ref, k2_ref, kk_ref, kka_ref, vhi_scr, vlo_scr))
            S = s_scr[...]
            sa = -_pair_sum(S * kk_t, ones2)
            vb = _pair_sum(eye2 * vhi_t, ones2) + _pair_sum(eye2 * vlo_t, ones2)
            S2 = S * w_t + sa * kka_t + vb * k2_t
            y_ref[pl.ds(t, 1), :] = _pair_flat(jnp.sum(eye2 * _pair_sum(S2 * r_t, ones2), axis=1, keepdims=True))
            s_scr[...] = S2
            st_ref[t] = S
            return carry

        lax.fori_loop(0, tc, step, 0, unroll=SCAN_UNROLL)
        fin_ref[...] = s_scr[...]

        if gather is not None:
            @pl.when(pl.program_id(0) == nchunk - 1)
            def _():
                for a in arrived:
                    a.wait_recv()
                for cp in ici + fwd:
                    cp.wait_send()

    in_specs = [seq] * 6
    out_specs = [seq, pl.BlockSpec((tc, N_PAIRS, HEAD_DIM, 128), lambda i: (i, 0, 0, 0)), one_state]
    out_shape = [jax.ShapeDtypeStruct((T, W), F32), jax.ShapeDtypeStruct((T, N_PAIRS, HEAD_DIM, 128), F32),
                 jax.ShapeDtypeStruct((N_PAIRS, HEAD_DIM, 128), F32)]
    scratch = [pltpu.VMEM((N_PAIRS, HEAD_DIM, 128), F32), pltpu.VMEM((tc, W), F32), pltpu.VMEM((tc, W), F32)]
    args = [r, w, k2, v, kk, kka]
    if gather is not None:
        in_specs += [HBM_SPEC] * nb
        out_specs += [HBM_SPEC] * nb
        out_shape += [jax.ShapeDtypeStruct((N_CHIPS, *g.shape), g.dtype) for g in gather]
        scratch += [pltpu.SemaphoreType.DMA((6 * nb,)), pltpu.SemaphoreType.DMA((6 * nb,))]
        args += list(gather)
    return pl.pallas_call(
        body, name="rwkv_scan_fwd", grid=(nchunk,), in_specs=in_specs, out_specs=out_specs, out_shape=out_shape,
        scratch_shapes=scratch, compiler_params=_cp(("arbitrary",)),
    )(*args)


PEER_FLIPS = tuple((fx, fy, fc) for fx in (0, 1) for fy in (0, 1) for fc in (0, 1))[1:]


def scan_bwd(r, w, k2, v, kk, kka, states, final_state, dy, scatter=None):
    T, W = r.shape
    tc = SCAN_CHUNK
    nchunk = T // tc
    seq = pl.BlockSpec((tc, W), lambda i: (nchunk - 1 - i, 0))
    st_spec = pl.BlockSpec((tc, N_PAIRS, HEAD_DIM, 128), lambda i: (nchunk - 1 - i, 0, 0, 0))
    one_state = pl.BlockSpec((N_PAIRS, HEAD_DIM, 128), lambda i: (0, 0, 0))
    nb, npeer = (0 if scatter is None else len(scatter)), len(PEER_FLIPS)

    def body(r_ref, w_ref, k2_ref, v_ref, kk_ref, kka_ref, st_ref, fin_ref, dy_ref, *rest):
        if scatter is None:
            dr_ref, dw_ref, dk2_ref, dv_ref, dkk_ref, dkka_ref, ds_scr, sc_scr, vhi_scr, vlo_scr = rest
        else:
            srcs, dsts = rest[:nb], rest[nb + 6:2 * nb + 6]
            dr_ref, dw_ref, dk2_ref, dv_ref, dkk_ref, dkka_ref = rest[nb:nb + 6]
            ds_scr, sc_scr, vhi_scr, vlo_scr, send_sems, recv_sems = rest[2 * nb + 6:]
            x, y, c = _mesh_pos()
            copies = []
            for b, (src, dst) in enumerate(zip(srcs, dsts)):
                for k, (fx, fy, fc) in enumerate(PEER_FLIPS):
                    px, py, pc = (1 - x if fx else x), (1 - y if fy else y), (1 - c if fc else c)
                    copies.append(_remote(send_sems, recv_sems, npeer * b + k, src.at[2 * px + py, pc], dst.at[k],
                                          (px, py, pc)))

            @pl.when(pl.program_id(0) == 0)
            def _():
                for cp in copies:
                    cp.start()

        @pl.when(pl.program_id(0) == 0)
        def _():
            ds_scr[...] = jnp.zeros(ds_scr.shape, F32)
            sc_scr[...] = fin_ref[...]

        vhi_scr[...], vlo_scr[...] = _split_bf16(v_ref[...])
        eye2, ones2 = _pair_consts()
        colsum = lambda z: jnp.sum(z, axis=1, keepdims=True)

        def step(j, carry):
            t = tc - 1 - j
            r_t, w_t, k2_t, kk_t, kka_t, vhi_t, vlo_t, dy_t = (
                _pair_rows(ref[pl.ds(t, 1), :])
                for ref in (r_ref, w_ref, k2_ref, kk_ref, kka_ref, vhi_scr, vlo_scr, dy_ref))
            s_prev, s_cur = st_ref[t], sc_scr[...]
            dyb = _pair_sum(eye2 * dy_t, ones2)
            vb = _pair_sum(eye2 * vhi_t, ones2) + _pair_sum(eye2 * vlo_t, ones2)
            sa = -_pair_sum(s_prev * kk_t, ones2)
            dS = ds_scr[...] + dyb * r_t
            dsa = _pair_sum(dS * kka_t, ones2)
            ds_scr[...] = dS * w_t - dsa * kk_t
            sc_scr[...] = s_prev
            for ref, val in zip((dr_ref, dw_ref, dk2_ref, dv_ref, dkk_ref, dkka_ref),
                                (s_cur * dyb, dS * s_prev, dS * vb, eye2 * _pair_sum(dS * k2_t, ones2),
                                 -(s_prev * dsa), dS * sa)):
                ref[pl.ds(t, 1), :] = _pair_flat(colsum(val))
            return carry

        lax.fori_loop(0, tc, step, 0, unroll=SCAN_UNROLL_BWD)

        if scatter is not None:
            @pl.when(pl.program_id(0) == nchunk - 1)
            def _():
                for cp in copies:
                    cp.wait()

    in_specs = [seq] * 6 + [st_spec, one_state, seq]
    out_specs = [seq] * 6
    out_shape = [jax.ShapeDtypeStruct((T, W), F32)] * 6
    scratch = [pltpu.VMEM((N_PAIRS, HEAD_DIM, 128), F32)] * 2 + [pltpu.VMEM((tc, W), F32)] * 2
    args = [r, w, k2, v, kk, kka, states, final_state, dy]
    if scatter is not None:
        in_specs += [HBM_SPEC] * nb
        out_specs += [HBM_SPEC] * nb
        out_shape += [jax.ShapeDtypeStruct((npeer, *s.shape[2:]), s.dtype) for s in scatter]
        scratch += [pltpu.SemaphoreType.DMA((npeer * nb,)), pltpu.SemaphoreType.DMA((npeer * nb,))]
        args += list(scatter)
    return pl.pallas_call(
        body, name="rwkv_scan_bwd", grid=(nchunk,), in_specs=in_specs, out_specs=out_specs, out_shape=out_shape,
        scratch_shapes=scratch, compiler_params=_cp(("arbitrary",)),
    )(*args)


def _mix_common(y, r, k2, v, lnx_w, lnx_b, r_k):
    yc = y - _segsum(y) * (1.0 / HEAD_DIM)
    rstd = lax.rsqrt(_segsum(yc * yc) * (1.0 / HEAD_DIM) + LNX_EPS)
    yhat = yc * rstd
    s = _segsum(r * k2 * r_k)
    return rstd, yhat, s, yhat * lnx_w + lnx_b + s * v


def mix_gate_fwd(y, r, k2, v, g, y_mem, lnx_w, lnx_b, r_k):
    T = y.shape[0]

    def fn(i, nblk, y, r, k2, v, g, ym, lw, lb, rk):
        mix = _mix_common(y, r, k2, v, lw, lb, rk)[3]
        return jnp.concatenate([mix * g, ym], axis=1)

    return _rowmap(fn, name="mix_gate_fwd", T=T, tb=256,
                   ins=[("row", z) for z in (y, r, k2, v, g, y_mem)] + [("const", c) for c in (lnx_w, lnx_b, r_k)],
                   outs=[("row", RWKV_WIDTH + MEM_WIDTH, MMD)])[0]


def mix_gate_bwd(y, r, k2, v, g, dycat, lnx_w, lnx_b, r_k):
    T = y.shape[0]

    def fn(i, nblk, y, r, k2, v, g, dyc, lw, lb, rk):
        rstd, yhat, s, mix = _mix_common(y, r, k2, v, lw, lb, rk)
        dmix = dyc * g
        dyh = dmix * lw
        inv = 1.0 / HEAD_DIM
        dy = rstd * (dyh - _segsum(dyh) * inv - yhat * (_segsum(dyh * yhat) * inv))
        ds = _segsum(dmix * v)
        cs = lambda z: jnp.sum(z, axis=0, keepdims=True)
        return (dy, ds * k2 * rk, ds * r * rk, dmix * s, dyc * mix, cs(dmix * yhat), cs(dmix), cs(ds * r * k2))

    return _rowmap(fn, name="mix_gate_bwd", T=T, tb=256,
                   ins=[("row", z) for z in (y, r, k2, v, g)] + [("row", dycat, RWKV_WIDTH, 0)]
                   + [("const", c) for c in (lnx_w, lnx_b, r_k)],
                   outs=[("row", RWKV_WIDTH, F32)] * 5, accs=[((1, RWKV_WIDTH), F32)] * 3)


def _head_rms(x, gones):
    ms = _segsum(x * x) * (1.0 / HEAD_DIM)
    r = lax.rsqrt(ms + RMS_EPS)
    return r, x * r


def _head_rms_bwd(dxn_g, r, xh, gones):
    return r * (dxn_g - xh * (_segsum(dxn_g * xh) * (1.0 / HEAD_DIM)))


def mem_kv_fwd(mem, norm_g, w_kv, k_norm_t, *, name):
    gones = _group_ones(MEM_WIDTH)

    def body(mem_ref, g_ref, w_ref, kn_ref, go_ref, k_out, v_out):
        _, xh = _rms_stats(mem_ref[...])
        kv = _dot(xh * g_ref[...], w_ref[...])
        _, kh = _head_rms(kv[:, :MEM_WIDTH], go_ref[...])
        k_out[...] = kh * kn_ref[...]
        v_out[...] = kv[:, MEM_WIDTH:]

    return pl.pallas_call(
        body, name=name, out_shape=[jax.ShapeDtypeStruct((N_MEM, MEM_WIDTH), F32)] * 2, compiler_params=_cp(),
    )(mem, norm_g, w_kv, k_norm_t, gones)


def mem_kv_bwd(mem, norm_g, w_kv, k_norm_t, dkn, dv, *, name):
    gones, fold = _group_ones(MEM_WIDTH), _fold_ones(MEM_WIDTH)

    def body(mem_ref, g_ref, w_ref, kn_ref, go_ref, fo_ref, dkn_ref, dv_ref, dw_out, dg_out, dkg_out):
        _, xh = _rms_stats(mem_ref[...])
        hm = xh * g_ref[...]
        kv = _dot(hm, w_ref[...])
        r, kh = _head_rms(kv[:, :MEM_WIDTH], go_ref[...])
        dkn = dkn_ref[...]
        dkg_out[...] = _fold_heads(jnp.sum(dkn * kh, axis=0, keepdims=True), fo_ref[...])
        dkraw = _head_rms_bwd(dkn * kn_ref[...], r, kh, go_ref[...])
        dkv = jnp.concatenate([dkraw, dv_ref[...]], axis=1)
        dw_out[...] = _dot_tn(hm, dkv)
        dg_out[...] = jnp.sum(_dot_nt(dkv, w_ref[...]) * xh, axis=0, keepdims=True)

    return pl.pallas_call(
        body, name=name,
        out_shape=[jax.ShapeDtypeStruct((D_MODEL, 2 * MEM_WIDTH), F32), jax.ShapeDtypeStruct((1, D_MODEL), F32),
                   jax.ShapeDtypeStruct((1, HEAD_DIM), F32)],
        compiler_params=_cp(),
    )(mem, norm_g, w_kv, k_norm_t, gones, fold, dkn, dv)


def _mem_scores(qn, kn, masks, h):
    s = _dot_nt(qn * masks[h], kn) * (1.0 / math.sqrt(HEAD_DIM))
    s = s - jnp.max(s, axis=-1, keepdims=True)
    e = jnp.exp(s)
    return e / jnp.sum(e, axis=-1, keepdims=True)


def mem_attn_fwd(p, colblock, kn, v, q_norm_t, *, name):
    T = p.shape[0]
    gones, masks = _group_ones(MEM_WIDTH), _head_masks(MEM_WIDTH)

    def fn(i, nblk, q, kn, v, qg, go, masks):
        _, qh = _head_rms(q, go)
        qn = qh * qg
        out = jnp.zeros(q.shape, F32)
        for h in range(MEM_WIDTH // HEAD_DIM):
            out = out + _dot(_mem_scores(qn, kn, masks, h), v * masks[h])
        return out

    return _rowmap(fn, name=name, T=T, tb=512,
                   ins=[("row", p, MEM_WIDTH, colblock)] + [("const", c) for c in (kn, v, q_norm_t, gones, masks)],
                   outs=[("row", MEM_WIDTH, F32)])[0]


def mem_attn_bwd(p, colblock, kn, v, q_norm_t, dycat, dcolblock, *, name):
    T = p.shape[0]
    gones, masks, fold = _group_ones(MEM_WIDTH), _head_masks(MEM_WIDTH), _fold_ones(MEM_WIDTH)
    scale = 1.0 / math.sqrt(HEAD_DIM)

    def fn(i, nblk, q, dy, kn, v, qg, go, masks, fo):
        r, qh = _head_rms(q, go)
        qn = qh * qg
        dqn = jnp.zeros(q.shape, F32)
        dkn = jnp.zeros(kn.shape, F32)
        dv = jnp.zeros(v.shape, F32)
        for h in range(MEM_WIDTH // HEAD_DIM):
            pr = _mem_scores(qn, kn, masks, h)
            dyh = dy * masks[h]
            dpr = _dot_nt(dyh, v)
            ds = pr * (dpr - jnp.sum(dpr * pr, axis=-1, keepdims=True)) * scale
            dqn = dqn + _dot(ds, kn * masks[h])
            dkn = dkn + _dot_tn(ds, qn * masks[h])
            dv = dv + _dot_tn(pr, dyh)
        dqg = _fold_heads(jnp.sum(dqn * qh, axis=0, keepdims=True), fo)
        return _head_rms_bwd(dqn * qg, r, qh, go), dkn, dv, dqg

    return _rowmap(fn, name=name, T=T, tb=512,
                   ins=[("row", p, MEM_WIDTH, colblock), ("row", dycat, MEM_WIDTH, dcolblock)]
                   + [("const", c) for c in (kn, v, q_norm_t, gones, masks, fold)],
                   outs=[("row", MEM_WIDTH, F32)],
                   accs=[((N_MEM, MEM_WIDTH), F32), ((N_MEM, MEM_WIDTH), F32), ((1, HEAD_DIM), F32)])


def _ffn_conv(i, u, halo, cw, cb):
    up1 = _shift_down(u, _row_pick(halo, 7), i == 0)
    up2 = _shift_down(up1, _row_pick(halo, 6), i == 0)
    c = cb + cw[0] * up2 + cw[1] * up1 + cw[2] * u
    return up1, up2, c[:, :D_FF], c[:, D_FF:]


def ffn_act_fwd(u, cw, cb, *, name):
    T = u.shape[0]

    def fn(i, nblk, u, halo, c0, c1, c2, cb):
        _, _, gate, val = _ffn_conv(i, u, halo, (c0, c1, c2), cb)
        return jax.nn.silu(gate) * val

    return _rowmap(fn, name=name, T=T, tb=128, ins=[("row", u), ("prev", u)] + [("const", c) for c in (*cw, cb)],
                   outs=[("row", D_FF, MMD)])[0]


def ffn_act_bwd(u, cw, cb, dz, *, name):
    T = u.shape[0]
    tb = 128

    def fn(i, nblk, u, halo, unext, c0, c1, c2, cb, dz, dznext):
        ue = jnp.concatenate([u, unext], axis=0)
        dze = jnp.concatenate([dz, jnp.where(i == nblk - 1, 0.0, 1.0) * dznext], axis=0)
        up1, up2, gate, val = _ffn_conv(i, ue, halo, (c0, c1, c2), cb)
        sg = jax.nn.sigmoid(gate)
        dce = jnp.concatenate([dze * val * sg * (1.0 + gate * (1.0 - sg)), dze * gate * sg], axis=1)
        rows = dce.shape[0]
        du = (c2 * dce + c1 * pltpu.roll(dce, rows - 1, axis=0) + c0 * pltpu.roll(dce, rows - 2, axis=0))[:tb]
        dc = dce[:tb]
        s = lambda z: jnp.sum(z, axis=0, keepdims=True)
        return du, s(dc * up2[:tb]), s(dc * up1[:tb]), s(dc * u), s(dc)

    return _rowmap(fn, name=name, T=T, tb=tb,
                   ins=[("row", u), ("prev", u), ("next", u)] + [("const", c) for c in (*cw, cb)]
                   + [("row", dz), ("next", dz)],
                   outs=[("row", 2 * D_FF, MMD)], accs=[((1, 2 * D_FF), F32)] * 4)


def _rope_swap(z):
    lane = lax.broadcasted_iota(jnp.int32, z.shape, 1) % HEAD_DIM
    w = z.shape[1]
    return jnp.where(lane < HEAD_DIM // 2, pltpu.roll(z, w - HEAD_DIM // 2, axis=1), pltpu.roll(z, HEAD_DIM // 2, axis=1))


def rope_tables(T):
    inv = (np.float32(ROPE_THETA) ** (-np.arange(0, HEAD_DIM, 2, dtype=np.float32) / np.float32(HEAD_DIM))).astype(np.float32)
    ang = (np.arange(T, dtype=np.float32)[:, None] * inv[None, :]).astype(np.float64)
    cos, sin = np.cos(ang).astype(np.float32), np.sin(ang).astype(np.float32)
    return (jnp.asarray(np.concatenate([cos, cos, cos, cos], axis=1)),
            jnp.asarray(np.concatenate([-sin, sin, -sin, sin], axis=1)))


def _rope_wide(t):
    return jnp.tile(t, (1, DIL_WIDTH // t.shape[1]))


def qk_fwd(kvp, pb, kg_t, qg_t, cos, sin):
    T = kvp.shape[0]
    gones = _group_ones(DIL_WIDTH)

    def fn(i, nblk, kraw, vraw, qraw, kg, qg, c, s, go):
        c, s = _rope_wide(c), _rope_wide(s)
        outs = []
        for raw, g in ((qraw, qg), (kraw, kg)):
            _, xh = _head_rms(raw, go)
            z = xh * g
            outs.append(z * c + _rope_swap(z) * s)
        return outs[0], outs[1], vraw

    return _rowmap(fn, name="qk_fwd", T=T, tb=256,
                   ins=[("row", kvp, DIL_WIDTH, 0), ("row", kvp, DIL_WIDTH, 1), ("row", pb, DIL_WIDTH, 0)]
                   + [("const", kg_t), ("const", qg_t), ("row", cos), ("row", sin), ("const", gones)],
                   outs=[("row", DIL_WIDTH, MMD)] * 3)


def qk_bwd(kvp, pb, kg_t, qg_t, cos, sin, dq, dk, dv, dq_mem):
    T = kvp.shape[0]
    gones, fold = _group_ones(DIL_WIDTH), _fold_ones(DIL_WIDTH)

    def fn(i, nblk, kraw, qraw, kg, qg, c, s, go, fo, dq, dk, dv, dqm):
        c, s = _rope_wide(c), _rope_wide(s)
        res, dgs = [], []
        for raw, g, d in ((qraw, qg, dq), (kraw, kg, dk)):
            r, xh = _head_rms(raw, go)
            dz = d * c + _rope_swap(d * s)
            dgs.append(_fold_heads(jnp.sum(dz * xh, axis=0, keepdims=True), fo))
            res.append(_head_rms_bwd(dz * g, r, xh, go))
        return (jnp.concatenate([res[0], dqm], axis=1), jnp.concatenate([res[1], dv], axis=1), dgs[0], dgs[1])

    return _rowmap(fn, name="qk_bwd", T=T, tb=256,
                   ins=[("row", kvp, DIL_WIDTH, 0), ("row", pb, DIL_WIDTH, 0), ("const", kg_t), ("const", qg_t),
                        ("row", cos), ("row", sin), ("const", gones), ("const", fold),
                        ("row", dq), ("row", dk), ("row", dv), ("row", dq_mem)],
                   outs=[("row", DIL_WIDTH + MEM_WIDTH, MMD), ("row", 2 * DIL_WIDTH, MMD)],
                   accs=[((1, HEAD_DIM), F32)] * 2)


def _band(kind):
    i = lax.broadcasted_iota(jnp.int32, (DIL_BLOCK, DIL_BLOCK), 0)
    j = lax.broadcasted_iota(jnp.int32, (DIL_BLOCK, DIL_BLOCK), 1)
    return (j <= i) if kind == "cur" else (j >= i)


def dil_attn_fwd(q, k, v, seq_blocks, *, name):
    T, W = q.shape
    nb = T // DIL_BLOCK
    masks = _head_masks(W)
    cur = pl.BlockSpec((DIL_BLOCK, W), lambda n: (n, 0))
    prv = pl.BlockSpec((DIL_BLOCK, W), lambda n: (jnp.maximum(n - 1, 0), 0))
    scale = 1.0 / math.sqrt(HEAD_DIM)

    def body(q_ref, kc_ref, kp_ref, vc_ref, vp_ref, m_ref, o_ref, l_ref):
        n = pl.program_id(0)
        has_prev = (n % seq_blocks) != 0
        q = q_ref[...].astype(F32)
        kc, kp = kc_ref[...].astype(F32), kp_ref[...].astype(F32)
        vc, vp = vc_ref[...].astype(F32), vp_ref[...].astype(F32)
        ok_c = _band("cur")
        ok_p = jnp.logical_and(_band("prev"), has_prev)
        o = jnp.zeros((DIL_BLOCK, W), F32)
        lse = jnp.zeros((DIL_BLOCK, W), F32)
        for h in range(W // HEAD_DIM):
            mh = m_ref[h]
            qh = q * mh
            sc = jnp.where(ok_c, _dot_nt(qh, kc) * scale, NEG_INF)
            sp = jnp.where(ok_p, _dot_nt(qh, kp) * scale, NEG_INF)
            mx = jnp.maximum(jnp.max(sc, axis=-1, keepdims=True), jnp.max(sp, axis=-1, keepdims=True))
            ec, ep = jnp.exp(sc - mx), jnp.exp(sp - mx)
            den = jnp.sum(ec, axis=-1, keepdims=True) + jnp.sum(ep, axis=-1, keepdims=True)
            o = o + (_dot(ec, vc * mh) + _dot(ep, vp * mh)) / den
            lse = lse + (mx + jnp.log(den)) * mh
        o_ref[...] = o
        l_ref[...] = lse

    return pl.pallas_call(
        body, name=name, grid=(nb,), in_specs=[cur, cur, prv, cur, prv, pl.BlockSpec(masks.shape, lambda n: (0, 0, 0))],
        out_specs=[cur, cur], out_shape=[jax.ShapeDtypeStruct((T, W), F32)] * 2,
        compiler_params=_cp(("parallel",)),
    )(q, k, k, v, v, masks)


def dil_attn_bwd(q, k, v, o, lse, do, dlse, seq_blocks, *, name):
    T, W = q.shape
    nb = T // DIL_BLOCK
    masks = _head_masks(W)
    cur = pl.BlockSpec((DIL_BLOCK, W), lambda n: (n, 0))
    prv = pl.BlockSpec((DIL_BLOCK, W), lambda n: (jnp.maximum(n - 1, 0), 0))
    nxt = pl.BlockSpec((DIL_BLOCK, W), lambda n: (jnp.minimum(n + 1, nb - 1), 0))
    scale = 1.0 / math.sqrt(HEAD_DIM)

    def body(qc_ref, qn_ref, kc_ref, kp_ref, vc_ref, vp_ref, oc_ref, on_ref, lc_ref, ln_ref, doc_ref, don_ref,
             dlc_ref, dln_ref, m_ref, dq_ref, dk_ref, dv_ref):
        n = pl.program_id(0)
        has_prev = (n % seq_blocks) != 0
        has_next = jnp.logical_and(((n + 1) % seq_blocks) != 0, n + 1 < nb)
        f = lambda ref: ref[...].astype(F32)
        qc, qn, kc, kp, vc, vp = f(qc_ref), f(qn_ref), f(kc_ref), f(kp_ref), f(vc_ref), f(vp_ref)
        doc, don = doc_ref[...], don_ref[...]
        ok_c = _band("cur")
        ok_p = jnp.logical_and(_band("prev"), has_prev)
        ok_n = jnp.logical_and(_band("prev"), has_next)
        dq = jnp.zeros((DIL_BLOCK, W), F32)
        dk = jnp.zeros((DIL_BLOCK, W), F32)
        dv = jnp.zeros((DIL_BLOCK, W), F32)

        def side(qh, kk, vv, doh, lse_h, corr, ok):
            s = _dot_nt(qh, kk) * scale
            pr = jnp.where(ok, jnp.exp(jnp.where(ok, s, NEG_INF) - lse_h), 0.0)
            ds = pr * (_dot_nt(doh, vv) + corr) * scale
            return pr, ds

        for h in range(W // HEAD_DIM):
            mh = m_ref[h]
            red = lambda z: jnp.sum(z * mh, axis=-1, keepdims=True)
            qh, doh = qc * mh, doc * mh
            lse_h = red(lc_ref[...]) * (1.0 / HEAD_DIM)
            corr = red(dlc_ref[...]) - red(doc * oc_ref[...])
            pr_c, ds_c = side(qh, kc, vc * mh, doh, lse_h, corr, ok_c)
            _, ds_p = side(qh, kp, vp * mh, doh, lse_h, corr, ok_p)
            dq = dq + _dot(ds_c, kc * mh) + _dot(ds_p, kp * mh)
            dk = dk + _dot_tn(ds_c, qh)
            dv = dv + _dot_tn(pr_c, doh)
            qh2, doh2 = qn * mh, don * mh
            lse_2 = red(ln_ref[...]) * (1.0 / HEAD_DIM)
            corr2 = red(dln_ref[...]) - red(don * on_ref[...])
            pr_n, ds_n = side(qh2, kc, vc * mh, doh2, lse_2, corr2, ok_n)
            dk = dk + _dot_tn(ds_n, qh2)
            dv = dv + _dot_tn(pr_n, doh2)
        dq_ref[...] = dq
        dk_ref[...] = dk
        dv_ref[...] = dv

    return pl.pallas_call(
        body, name=name, grid=(nb,),
        in_specs=[cur, nxt, cur, prv, cur, prv, cur, nxt, cur, nxt, cur, nxt, cur, nxt,
                  pl.BlockSpec(masks.shape, lambda n: (0, 0, 0))],
        out_specs=[cur] * 3, out_shape=[jax.ShapeDtypeStruct((T, W), F32)] * 3,
        compiler_params=_cp(("parallel",)),
    )(q, q, k, k, v, v, o, o, lse, lse, do, do, dlse, dlse, masks)


def _mix_weights(ls):
    m = jnp.maximum(jnp.maximum(ls[0], ls[1]), ls[2])
    es = [jnp.exp(l - m) for l in ls]
    den = es[0] + es[1] + es[2]
    return [e / den for e in es]


def mix_fwd(os_, ls, y_mem):
    T = y_mem.shape[0]

    def fn(i, nblk, o0, o1, o2, l0, l1, l2, ym):
        w = _mix_weights((l0, l1, l2))
        return jnp.concatenate([w[0] * o0 + w[1] * o1 + w[2] * o2, ym], axis=1)

    return _rowmap(fn, name="mix_fwd", T=T, tb=512, ins=[("row", z) for z in (*os_, *ls, y_mem)],
                   outs=[("row", 2 * MEM_WIDTH, MMD)])[0]


def mix_bwd(os_, ls, dycat):
    T = dycat.shape[0]

    def fn(i, nblk, o0, o1, o2, l0, l1, l2, dy):
        w = _mix_weights((l0, l1, l2))
        os3 = (o0, o1, o2)
        dws = [dy * o for o in os3]
        tot = w[0] * dws[0] + w[1] * dws[1] + w[2] * dws[2]
        return tuple(wg * dy for wg in w) + tuple(wg * (dw - tot) for wg, dw in zip(w, dws))

    return _rowmap(fn, name="mix_bwd", T=T, tb=512,
                   ins=[("row", z) for z in (*os_, *ls)] + [("row", dycat, MEM_WIDTH, 0)],
                   outs=[("row", MEM_WIDTH, F32)] * 6)


def loss_fwd_bwd(y, target):
    T, D = y.shape

    def fn(i, nblk, y, t):
        e = y - t
        return e * (1.0 / D), jnp.zeros((8, 128), F32) + jnp.sum(e * e) * (0.5 / D)

    return _rowmap(fn, name="loss", T=T, tb=512, ins=[("row", y), ("row", target)], outs=[("row", D, F32)],
                   accs=[((8, 128), F32)])


def _to_residues(z, dil):
    T, W = z.shape
    return z.reshape(T // dil, dil, W).transpose(1, 0, 2).reshape(T, W)


def _from_residues(z, dil):
    T, W = z.shape
    return z.reshape(dil, T // dil, W).transpose(1, 0, 2).reshape(T, W)


def _pad_rows(w, rows):
    return jnp.concatenate([w, jnp.zeros((rows - w.shape[0], w.shape[1]), w.dtype)], axis=0)


def _tile_heads(g, width):
    return jnp.tile(g.reshape(1, HEAD_DIM), (1, width // HEAD_DIM))


def _conv_rows(W, i):
    return [W["ffn_conv_w"][i][j:j + 1] for j in range(3)]


def _ffn_fwd(x, i, W):
    hn = rms_fwd(x, [W["ffn_norm"][i:i + 1]], name=f"ffn_rms{i}")[0]
    u = _mm(hn, W["ffn_w_up"][i], name=f"ffn_up{i}")
    z = ffn_act_fwd(u, _conv_rows(W, i), W["ffn_conv_b"][i:i + 1], name=f"ffn_act{i}")
    out = _mm(z, W["ffn_w_down"][i], add=x, name=f"ffn_down{i}")
    return out, (x, hn, u, z)


def _ffn_bwd(dout, i, W, saved, G):
    x, hn, u, z = saved
    dz = _mm(dout, W["ffn_w_down"][i], tb=True, name=f"ffn_down_dx{i}")
    G["ffn_w_down"][i] = _mm(z, dout, ta=True, name=f"ffn_down_dw{i}")
    du, dw0, dw1, dw2, db = ffn_act_bwd(u, _conv_rows(W, i), W["ffn_conv_b"][i:i + 1], dz, name=f"ffn_act_bwd{i}")
    G["ffn_conv_w"][i] = jnp.concatenate([dw0, dw1, dw2], axis=0)
    G["ffn_conv_b"][i] = db[0]
    dhn = _mm(du, W["ffn_w_up"][i], tb=True, name=f"ffn_up_dx{i}")
    G["ffn_w_up"][i] = _mm(hn, du, ta=True, name=f"ffn_up_dw{i}")
    dx, dg = rms_bwd(x, [W["ffn_norm"][i:i + 1]], [dhn], dout, name=f"ffn_rms_bwd{i}")
    G["ffn_norm"][i] = dg[0]
    return dx


def local_step(x, mem, target, W, late=None):
    T = x.shape[0]
    W = dict(W)
    G = {"ffn_w_down": [None, None], "ffn_w_up": [None, None], "ffn_conv_w": [None, None],
         "ffn_conv_b": [None, None], "ffn_norm": [None, None], "attn_norm": [None, None], "mem_norm": [None, None],
         "mem_w_kv": [None, None], "mem_q_norm": [None, None], "mem_k_norm": [None, None]}
    mu, w0, a0 = W["a_mu"], W["a_w0"], W["a_a0"]
    w2p, a2p, g2p = (_pad_rows(W["a_w2"][0], LORA_WIDTH),
                     jnp.concatenate([jnp.zeros((64, RWKV_WIDTH), MMD), W["a_a2"][0],
                                      jnp.zeros((128, RWKV_WIDTH), MMD)], axis=0),
                     jnp.concatenate([jnp.zeros((128, RWKV_WIDTH), MMD), W["a_g2"][0]], axis=0))
    k_k, k_a, lnx_w, lnx_b = W["a_k_k"], W["a_k_a"], W["a_lnx_w"], W["a_lnx_b"]
    r_k = W["a_r_k"].reshape(1, RWKV_WIDTH)

    h0 = rms_fwd(x, [W["attn_norm"][0:1]], name="attn_rms0")[0]
    p = _mm(h0, W["a_w_in"][0], name="a_in")
    r, w, k2, v, kk, kka, g = rwkv_pre_fwd(p, mu, w0, a0, w2p, a2p, g2p, k_k, k_a)
    if late is None:
        y, states, final_state = scan_fwd(r, w, k2, v, kk, kka)
    else:
        y, states, final_state, *gathered = scan_fwd(r, w, k2, v, kk, kka, gather=late[0])
        W.update(late[1](gathered))
    memkv = []
    for i in range(2):
        memkv.append(mem_kv_fwd(mem, W["mem_norm"][i:i + 1], W["mem_w_kv"][i], _tile_heads(W["mem_k_norm"][i], MEM_WIDTH),
                                name=f"mem_kv{i}"))
    qg0 = _tile_heads(W["mem_q_norm"][0], MEM_WIDTH)
    y_mem0 = mem_attn_fwd(p, SHIFT_WIDTH // MEM_WIDTH, memkv[0][0], memkv[0][1], qg0, name="mem_attn0")
    ycat0 = mix_gate_fwd(y, r, k2, v, g, y_mem0, lnx_w, lnx_b, r_k)
    x1 = _mm(ycat0, W["a_w_out"][0], add=x, name="a_out")
    x2, ffn0 = _ffn_fwd(x1, 0, W)

    h1, hkv = rms_fwd(x2, [W["attn_norm"][1:2], W["kv_norm"].reshape(1, -1)], name="attn_rms1")
    kvp = _mm(hkv, W["kv_w"], name="kv_in")
    pb = _mm(h1, W["b_w_in"][0], name="b_in")
    cos, sin = rope_tables(T)
    kg_t, qg_t = _tile_heads(W["kv_k_norm"], DIL_WIDTH), _tile_heads(W["b_q_norm"][0], DIL_WIDTH)
    q, ksh, vsh = qk_fwd(kvp, pb, kg_t, qg_t, cos, sin)
    os_, ls, grp = [], [], []
    for gi, (win, dil) in enumerate(DIL_GROUPS):
        sl = slice(gi * MEM_WIDTH, (gi + 1) * MEM_WIDTH)
        qg_, kg_, vg_ = (_to_residues(z[:, sl], dil) for z in (q, ksh, vsh))
        o_r, l_r = dil_attn_fwd(qg_, kg_, vg_, T // dil // DIL_BLOCK, name=f"dil_fwd{gi}")
        grp.append((qg_, kg_, vg_, o_r, l_r))
        os_.append(_from_residues(o_r, dil))
        ls.append(_from_residues(l_r, dil))
    qg1 = _tile_heads(W["mem_q_norm"][1], MEM_WIDTH)
    y_mem1 = mem_attn_fwd(pb, DIL_WIDTH // MEM_WIDTH, memkv[1][0], memkv[1][1], qg1, name="mem_attn1")
    ycat1 = mix_fwd(os_, ls, y_mem1)
    x3 = _mm(ycat1, W["b_w_out"][0], add=x2, name="b_out")
    x4, ffn1 = _ffn_fwd(x3, 1, W)

    dx4, loss = loss_fwd_bwd(x4, target)

    dx3 = _ffn_bwd(dx4, 1, W, ffn1, G)
    dycat1 = _mm(dx3, W["b_w_out"][0], tb=True, name="b_out_dx")
    G["b_w_out"] = _mm(ycat1, dx3, ta=True, name="b_out_dw")[None]
    dq_mem1, dkn1, dvm1, dqg1 = mem_attn_bwd(pb, DIL_WIDTH // MEM_WIDTH, memkv[1][0], memkv[1][1], qg1, dycat1, 1,
                                             name="mem_attn_bwd1")
    G["mem_q_norm"][1] = dqg1[0]
    d_os_ls = mix_bwd(os_, ls, dycat1)
    dqs, dks, dvs = [], [], []
    for gi, (win, dil) in enumerate(DIL_GROUPS):
        qg_, kg_, vg_, o_r, l_r = grp[gi]
        do_r, dl_r = _to_residues(d_os_ls[gi], dil), _to_residues(d_os_ls[3 + gi], dil)
        dq_r, dk_r, dv_r = dil_attn_bwd(qg_, kg_, vg_, o_r, l_r, do_r, dl_r, T // dil // DIL_BLOCK, name=f"dil_bwd{gi}")
        dqs.append(_from_residues(dq_r, dil))
        dks.append(_from_residues(dk_r, dil))
        dvs.append(_from_residues(dv_r, dil))
    dq, dk, dv = (jnp.concatenate(z, axis=1) for z in (dqs, dks, dvs))
    dpb, dkvp, dqn_g, dkn_g = qk_bwd(kvp, pb, kg_t, qg_t, cos, sin, dq, dk, dv, dq_mem1)
    G["b_q_norm"] = dqn_g
    G["kv_k_norm"] = dkn_g[0]
    dh1 = _mm(dpb, W["b_w_in"][0], tb=True, name="b_in_dx")
    G["b_w_in"] = _mm(h1, dpb, ta=True, name="b_in_dw")[None]
    dhkv = _mm(dkvp, W["kv_w"], tb=True, name="kv_in_dx")
    G["kv_w"] = _mm(hkv, dkvp, ta=True, name="kv_in_dw")
    dx2, dg1, dgkv = rms_bwd(x2, [W["attn_norm"][1:2], W["kv_norm"].reshape(1, -1)], [dh1, dhkv], dx3,
                             name="attn_rms_bwd1")
    G["attn_norm"][1] = dg1[0]
    G["kv_norm"] = dgkv[0]

    dx1 = _ffn_bwd(dx2, 0, W, ffn0, G)
    dycat0 = _mm(dx1, W["a_w_out"][0], tb=True, name="a_out_dx")
    G["a_w_out"] = _mm(ycat0, dx1, ta=True, name="a_out_dw")[None]
    dq_mem0, dkn0, dvm0, dqg0 = mem_attn_bwd(p, SHIFT_WIDTH // MEM_WIDTH, memkv[0][0], memkv[0][1], qg0, dycat0,
                                             RWKV_WIDTH // MEM_WIDTH, name="mem_attn_bwd0")
    G["mem_q_norm"][0] = dqg0[0]
    dy, dr_b, dk2_b, dv_b, dg, dlw, dlb, drk = mix_gate_bwd(y, r, k2, v, g, dycat0, lnx_w, lnx_b, r_k)
    for i, (dkn, dvm) in enumerate(((dkn0, dvm0), (dkn1, dvm1))):
        dwkv, dgm, dkg = mem_kv_bwd(mem, W["mem_norm"][i:i + 1], W["mem_w_kv"][i],
                                    _tile_heads(W["mem_k_norm"][i], MEM_WIDTH), dkn, dvm, name=f"mem_kv_bwd{i}")
        G["mem_w_kv"][i], G["mem_norm"][i], G["mem_k_norm"][i] = dwkv, dgm[0], dkg[0]
    for n in list(G):
        if isinstance(G[n], list) and all(z is not None for z in G[n]):
            G[n] = jnp.stack(G[n], axis=0)
    late_out = None
    if late is None:
        dr, dw, dk2, dv, dkk, dkka = scan_bwd(r, w, k2, v, kk, kka, states, final_state, dy)
    else:
        pieces = late[2](G)
        dr, dw, dk2, dv, dkk, dkka, *received = scan_bwd(r, w, k2, v, kk, kka, states, final_state, dy, scatter=pieces)
        late_out = (received, pieces)
    dxs, dmu, dw0, da0, dw2p, da2p, dg2p, dk_k, dk_a = rwkv_pre_bwd(
        p, mu, w0, a0, w2p, a2p, g2p, k_k, k_a, (dr, dr_b), dw, (dk2, dk2_b), (dv, dv_b), dkk, dkka, dg)
    dp = shift_bwd(dxs, mu, dq_mem0)
    G.update(a_mu=dmu, a_w0=dw0, a_a0=da0, a_w2=dw2p[None, :64], a_a2=da2p[None, 64:128], a_g2=dg2p[None, 128:],
             a_k_k=dk_k, a_k_a=dk_a, a_r_k=drk.reshape(1, RWKV_HEADS, HEAD_DIM), a_lnx_w=dlw, a_lnx_b=dlb)
    dh0 = _mm(dp, W["a_w_in"][0], tb=True, name="a_in_dx")
    G["a_w_in"] = _mm(h0, dp, ta=True, name="a_in_dw")[None]
    grad_x, dg0 = rms_bwd(x, [W["attn_norm"][0:1]], [dh0], dx1, name="attn_rms_bwd0")
    G["attn_norm"][0] = dg0[0]
    G["attn_norm"] = jnp.stack(G["attn_norm"], axis=0)
    return loss, grad_x, G, late_out


HBM_SPEC = pl.BlockSpec(memory_space=pltpu.HBM)


def _mesh_pos():
    return lax.axis_index("x"), lax.axis_index("y"), lax.axis_index("c")


def _other_chips(x, y):
    return [(1 - x, y), (x, 1 - y), (1 - x, 1 - y)]


def _remote(send_sems, recv_sems, k, src, dst, to):
    return pltpu.make_async_remote_copy(src_ref=src, dst_ref=dst, send_sem=send_sems.at[k], recv_sem=recv_sems.at[k],
                                        device_id=to, device_id_type=MESH)


def _comm_call(body, name, ins, out_shape, n_remote):
    scratch = [pltpu.SemaphoreType.DMA((n_remote,)), pltpu.SemaphoreType.DMA((n_remote,))]
    return pl.pallas_call(body, name=name, in_specs=[HBM_SPEC] * len(ins), out_specs=[HBM_SPEC] * len(out_shape),
                          out_shape=out_shape, scratch_shapes=scratch)(*ins)


def comm_gather(wbig, wsm):
    def body(wb, ws, ob, os_, send_sems, recv_sems):
        x, y, c = _mesh_pos()
        s = 2 * x + y
        me, sibling = (x, y, c), (x, y, 1 - c)
        chips = _other_chips(x, y)
        rc = functools.partial(_remote, send_sems, recv_sems)
        first = []
        for j, (cx, cy) in enumerate(chips):
            first.append(rc(j, wb.at[c], ob.at[s, c], (cx, cy, c)))
            first.append(rc(6 + j, ws, os_.at[s], (cx, cy, c)))
        for cp in first:
            cp.start()
        passed = []
        for j, (cx, cy) in enumerate(chips):
            blk = ob.at[2 * cx + cy, c]
            rc(j, blk, blk, me).wait_recv()
            passed.append(rc(3 + j, blk, blk, sibling))
            passed[-1].start()
        for j, (cx, cy) in enumerate(chips):
            blk = ob.at[2 * cx + cy, 1 - c]
            rc(3 + j, blk, blk, me).wait_recv()
            sb = os_.at[2 * cx + cy]
            rc(6 + j, sb, sb, me).wait_recv()
        for cp in first + passed:
            cp.wait_send()

    out_shape = [jax.ShapeDtypeStruct((N_CHIPS, *wbig.shape), wbig.dtype),
                 jax.ShapeDtypeStruct((N_CHIPS, *wsm.shape), wsm.dtype)]
    return _comm_call(body, "comm_gather", [wbig, wsm], out_shape, 9)


def comm_pair_exchange(gb, gs):
    def body(gb_ref, gs_ref, rb_ref, rs_ref, send_sems, recv_sems):
        x, y, c = _mesh_pos()
        sibling = (x, y, 1 - c)
        rc = functools.partial(_remote, send_sems, recv_sems)
        cps = [rc(r, gb_ref.at[r, 1 - c], rb_ref.at[r], sibling) for r in range(N_CHIPS)]
        cps.append(rc(N_CHIPS, gs_ref.at[1 - c], rs_ref, sibling))
        for cp in cps:
            cp.start()
        for cp in cps:
            cp.wait()

    out_shape = [jax.ShapeDtypeStruct((N_CHIPS, *gb.shape[2:]), gb.dtype), jax.ShapeDtypeStruct(gs.shape[1:], gs.dtype)]
    return _comm_call(body, "comm_pair_exchange", [gb, gs], out_shape, N_CHIPS + 1)


def comm_chip_exchange(hb, hs):
    def body(hb_ref, hs_ref, qb_ref, qs_ref, send_sems, recv_sems):
        x, y, c = _mesh_pos()
        s = 2 * x + y
        me = (x, y, c)
        chips = _other_chips(x, y)
        rc = functools.partial(_remote, send_sems, recv_sems)
        cps = []
        for j, (cx, cy) in enumerate(chips):
            cps.append(rc(j, hb_ref.at[2 * cx + cy], qb_ref.at[s], (cx, cy, c)))
            cps.append(rc(3 + j, hs_ref, qs_ref.at[s], (cx, cy, c)))
        for cp in cps:
            cp.start()
        for j, (cx, cy) in enumerate(chips):
            blk = qb_ref.at[2 * cx + cy]
            rc(j, blk, blk, me).wait_recv()
            sb = qs_ref.at[2 * cx + cy]
            rc(3 + j, sb, sb, me).wait_recv()
        for cp in cps:
            cp.wait_send()

    out_shape = [jax.ShapeDtypeStruct(hb.shape, hb.dtype), jax.ShapeDtypeStruct((N_CHIPS, *hs.shape), hs.dtype)]
    return _comm_call(body, "comm_chip_exchange", [hb, hs], out_shape, 6)


def comm_pair_share(halves):
    n = len(halves)

    def body(*refs):
        x, y, c = _mesh_pos()
        send_sems, recv_sems = refs[2 * n], refs[2 * n + 1]
        cps = [_remote(send_sems, recv_sems, k, refs[k], refs[n + k], (x, y, 1 - c)) for k in range(n)]
        for cp in cps:
            cp.start()
        for cp in cps:
            cp.wait()

    out_shape = [jax.ShapeDtypeStruct(h.shape, h.dtype) for h in halves]
    return _comm_call(body, "comm_pair_share", list(halves), out_shape, n)


def add_pairs(a, b, out_dtype, *, name, tb):
    T, L = a.shape
    return _rowmap(lambda i, n, p, q: p + q, name=name, T=T, tb=tb, ins=[("row", a), ("row", b)],
                   outs=[("row", L, out_dtype)])[0]


def add_chips(parts, *, name, tb):
    T, L = parts[0].shape

    def fn(i, n, *ps):
        acc = ps[0].astype(F32)
        for p in ps[1:]:
            acc = acc + p.astype(F32)
        return acc

    return _rowmap(fn, name=name, T=T, tb=tb, ins=[("row", p) for p in parts], outs=[("row", L, F32)])[0]


def adamw(g, w, m, v, *, name, tb):
    T, L = g.shape

    def fn(i, n, g, w, m, v):
        m2 = ADAM_B1 * m + (1.0 - ADAM_B1) * g
        v2 = ADAM_B2 * v + (1.0 - ADAM_B2) * (g * g)
        m_hat = m2 / (1.0 - ADAM_B1 ** ADAM_STEP)
        v_hat = v2 / (1.0 - ADAM_B2 ** ADAM_STEP)
        return -ADAM_LR * (m_hat / (jnp.sqrt(v_hat) + ADAM_EPS) + ADAM_WD * w), m2, v2

    return _rowmap(fn, name=name, T=T, tb=tb, ins=[("row", z) for z in (g, w, m, v)], outs=[("row", L, F32)] * 3)


BIG_LANES = 1024
SMALL_LANES = 128


def _flat_cat(arrs, total, dtype):
    parts = [a.reshape(-1).astype(dtype) for a in arrs]
    n = sum(p.shape[0] for p in parts)
    assert n <= total, (n, total)
    if n < total:
        parts.append(jnp.zeros((total - n,), dtype))
    return jnp.concatenate(parts)


def _split_flat(flat, shapes):
    out, off = [], 0
    for shp in shapes:
        n = math.prod(shp)
        out.append(flat[off:off + n].reshape(shp))
        off += n
    return out


def _round_up(n, m):
    return -(-n // m) * m


def _full_shape(shard_shape, axis):
    return tuple(d * N_CHIPS if i == axis else d for i, d in enumerate(shard_shape))


def kernel(x, mem, attn_norm, a_w_in, a_mu, a_w0, a_w2, a_a0, a_a2, a_g2, a_k_k, a_k_a, a_r_k, a_lnx_w, a_lnx_b, a_w_out, kv_norm, kv_w, kv_k_norm, b_w_in, b_q_norm, b_w_out, mem_norm, mem_w_kv, mem_q_norm, mem_k_norm, ffn_norm, ffn_w_up, ffn_conv_w, ffn_conv_b, ffn_w_down, loss_target, m_attn_norm, m_a_w_in, m_a_mu, m_a_w0, m_a_w2, m_a_a0, m_a_a2, m_a_g2, m_a_k_k, m_a_k_a, m_a_r_k, m_a_lnx_w, m_a_lnx_b, m_a_w_out, m_kv_norm, m_kv_w, m_kv_k_norm, m_b_w_in, m_b_q_norm, m_b_w_out, m_mem_norm, m_mem_w_kv, m_mem_q_norm, m_mem_k_norm, m_ffn_norm, m_ffn_w_up, m_ffn_conv_w, m_ffn_conv_b, m_ffn_w_down, v_attn_norm, v_a_w_in, v_a_mu, v_a_w0, v_a_w2, v_a_a0, v_a_a2, v_a_g2, v_a_k_k, v_a_k_a, v_a_r_k, v_a_lnx_w, v_a_lnx_b, v_a_w_out, v_kv_norm, v_kv_w, v_kv_k_norm, v_b_w_in, v_b_q_norm, v_b_w_out, v_mem_norm, v_mem_w_kv, v_mem_q_norm, v_mem_k_norm, v_ffn_norm, v_ffn_w_up, v_ffn_conv_w, v_ffn_conv_b, v_ffn_w_down):
    args = (attn_norm, a_w_in, a_mu, a_w0, a_w2, a_a0, a_a2, a_g2, a_k_k, a_k_a, a_r_k, a_lnx_w, a_lnx_b, a_w_out, kv_norm, kv_w, kv_k_norm, b_w_in, b_q_norm, b_w_out, mem_norm, mem_w_kv, mem_q_norm, mem_k_norm, ffn_norm, ffn_w_up, ffn_conv_w, ffn_conv_b, ffn_w_down)
    ms = (m_attn_norm, m_a_w_in, m_a_mu, m_a_w0, m_a_w2, m_a_a0, m_a_a2, m_a_g2, m_a_k_k, m_a_k_a, m_a_r_k, m_a_lnx_w, m_a_lnx_b, m_a_w_out, m_kv_norm, m_kv_w, m_kv_k_norm, m_b_w_in, m_b_q_norm, m_b_w_out, m_mem_norm, m_mem_w_kv, m_mem_q_norm, m_mem_k_norm, m_ffn_norm, m_ffn_w_up, m_ffn_conv_w, m_ffn_conv_b, m_ffn_w_down)
    vs = (v_attn_norm, v_a_w_in, v_a_mu, v_a_w0, v_a_w2, v_a_a0, v_a_a2, v_a_g2, v_a_k_k, v_a_k_a, v_a_r_k, v_a_lnx_w, v_a_lnx_b, v_a_w_out, v_kv_norm, v_kv_w, v_kv_k_norm, v_b_w_in, v_b_q_norm, v_b_w_out, v_mem_norm, v_mem_w_kv, v_mem_q_norm, v_mem_k_norm, v_ffn_norm, v_ffn_w_up, v_ffn_conv_w, v_ffn_conv_b, v_ffn_w_down)
    w_sh, m_sh, v_sh = (dict(zip(WEIGHTS, z)) for z in (args, ms, vs))
    xi, yi, ci = _mesh_pos()
    chip = 2 * xi + yi
    axes = {**dict(BIG), **dict(SMALL_SHARDED)}
    early_names = [n for n, _ in BIG if n in EARLY_BIG]
    late_names = [n for n, _ in BIG if n not in EARLY_BIG and n != NATURAL_BIG]
    ss_names, ss_axes = [n for n, _ in SMALL_SHARDED], dict(SMALL_SHARDED)
    shapes_of = lambda names: [w_sh[n].shape for n in names]
    count = lambda names: sum(math.prod(s) for s in shapes_of(names))
    n_early, n_late = count(early_names), count(late_names)
    assert n_early % (2 * 16 * BIG_LANES) == 0 and n_late % (2 * 16 * BIG_LANES) == 0
    mh, mh_late = n_early // (2 * BIG_LANES), n_late // (2 * BIG_LANES)
    n_ss = _round_up(count(ss_names), 8 * SMALL_LANES)

    def shard_pack(names, total, dtype, source):
        return _flat_cat([source[n] for n in names], total, dtype)

    def unshard(names, gathered):
        per_chip = [_split_flat(gathered[j], shapes_of(names)) for j in range(N_CHIPS)]
        return {n: jnp.concatenate([per_chip[j][k] for j in range(N_CHIPS)], axis=axes[n]) for k, n in enumerate(names)}

    def by_chip(names, total, dtype, grads):
        parts = [jnp.split(grads[n], N_CHIPS, axis=axes[n]) for n in names]
        return jnp.stack([_flat_cat([p[j] for p in parts], total, dtype) for j in range(N_CHIPS)])

    wbig = shard_pack(early_names, n_early, MMD, w_sh).reshape(2, mh, BIG_LANES)
    wsm = shard_pack(ss_names, n_ss, F32, w_sh).reshape(-1, SMALL_LANES)
    wbig_all, wsm_all = comm_gather(wbig, wsm)
    wbig_all = lax.dynamic_update_index_in_dim(wbig_all, wbig, chip, 0).reshape(N_CHIPS, -1)
    wsm_all = lax.dynamic_update_index_in_dim(wsm_all, wsm, chip, 0).reshape(N_CHIPS, -1)
    W = {n: w_sh[n] for n in SMALL_REPL}
    W.update(unshard(early_names, wbig_all))
    W.update(unshard(ss_names, wsm_all))
    for n in ("a_w2", "a_a2", "a_g2"):
        W[n] = W[n].astype(MMD)
    wlate = shard_pack(late_names, n_late, MMD, w_sh).reshape(2, mh_late, BIG_LANES)
    nat_axis = axes[NATURAL_BIG]
    wnat = w_sh[NATURAL_BIG].astype(MMD)
    assert wnat.shape[0] == 2 and nat_axis != 0

    def unpack_late(gathered):
        full = lax.dynamic_update_index_in_dim(gathered[0], wlate, chip, 0)
        out = unshard(late_names, full.reshape(N_CHIPS, -1))
        nat = lax.dynamic_update_index_in_dim(gathered[1], wnat, chip, 0)
        out[NATURAL_BIG] = jnp.concatenate([nat[j] for j in range(N_CHIPS)], axis=nat_axis)
        return out

    def pack_late(grads):
        return [by_chip(late_names, n_late, BF16, grads).reshape(N_CHIPS, 2, mh_late, BIG_LANES),
                jnp.stack(jnp.split(grads[NATURAL_BIG].astype(BF16), N_CHIPS, axis=nat_axis))]

    loss_blk, grad_x, G, (received, pieces) = local_step(x[0], mem[0], loss_target[0], W,
                                                          late=([wlate, wnat], unpack_late, pack_late))
    loss = lax.psum(loss_blk[0, 0], ("x", "y", "c"))

    own_piece = lambda p: lax.dynamic_index_in_dim(lax.dynamic_index_in_dim(p, chip, 0, keepdims=False), ci, 0,
                                                   keepdims=False)
    gh_late = add_chips([received[0][k] for k in range(len(PEER_FLIPS))] + [own_piece(pieces[0])],
                        name="add_pieces_late", tb=32)
    gh_nat = add_chips([received[1][k] for k in range(len(PEER_FLIPS))] + [own_piece(pieces[1])],
                       name="add_pieces_natural", tb=32)
    gbig = by_chip(early_names, n_early, F32, G).reshape(N_CHIPS, 2, mh, BIG_LANES)
    sm_full_names = ss_names + list(SMALL_REPL)
    sm_full_shapes = [_full_shape(w_sh[n].shape, ss_axes[n]) for n in ss_names] + [w_sh[n].shape for n in SMALL_REPL]
    n_smf = _round_up(sum(math.prod(s) for s in sm_full_shapes), 2 * 8 * SMALL_LANES)
    msh = n_smf // (2 * SMALL_LANES)
    gsm = _flat_cat([G[n] for n in sm_full_names], n_smf, F32).reshape(2, msh, SMALL_LANES)
    rb, rs = comm_pair_exchange(gbig, gsm)
    mine_b = lax.dynamic_index_in_dim(gbig, ci, axis=1, keepdims=False)
    mine_s = lax.dynamic_index_in_dim(gsm, ci, axis=0, keepdims=False)
    hb = add_pairs(mine_b.reshape(-1, BIG_LANES), rb.reshape(-1, BIG_LANES), BF16, name="add_pairs_big", tb=128)
    hs = add_pairs(mine_s, rs, F32, name="add_pairs_small", tb=msh)
    hb = hb.reshape(N_CHIPS, mh, BIG_LANES)
    qb, qs = comm_chip_exchange(hb, hs)
    qb = lax.dynamic_update_index_in_dim(qb, lax.dynamic_index_in_dim(hb, chip, 0, keepdims=False), chip, 0)
    qs = lax.dynamic_update_index_in_dim(qs, hs, chip, 0)
    gh = add_chips([qb[j] for j in range(N_CHIPS)], name="add_chips_big", tb=32)
    gsh = add_chips([qs[j] for j in range(N_CHIPS)], name="add_chips_small", tb=msh)
    rh, rh_late, rh_nat, rsh = comm_pair_share([gh, gh_late, gh_nat, gsh])
    both = lambda mine_, theirs: jnp.where(ci == 0, jnp.stack([mine_, theirs]), jnp.stack([theirs, mine_]))
    gfull, gfull_late, gfull_nat, gsfull = both(gh, rh), both(gh_late, rh_late), both(gh_nat, rh_nat), both(gsh, rsh)

    res = {tag: {} for tag in ("grad", "delta", "new_m", "new_v")}
    big_grads = (list(zip(early_names, _split_flat(gfull.reshape(-1), shapes_of(early_names))))
                 + list(zip(late_names, _split_flat(gfull_late.reshape(-1), shapes_of(late_names))))
                 + [(NATURAL_BIG, gfull_nat)])
    for n, g in big_grads:
        shp = w_sh[n].shape
        rows = lambda z: z.reshape(-1, shp[-1])
        nrow = math.prod(shp[:-1])
        tb = next(t for t in (512, 256, 128, 64) if nrow % t == 0 and t * shp[-1] <= (1 << 19))
        outs = adamw(rows(g), rows(w_sh[n]), rows(m_sh[n]), rows(v_sh[n]), name=f"adamw_{n}", tb=tb)
        res["grad"][n] = g
        for tag, o in zip(("delta", "new_m", "new_v"), outs):
            res[tag][n] = o.reshape(shp)
    sm_full = dict(zip(sm_full_names, _split_flat(gsfull.reshape(-1), sm_full_shapes)))
    g_loc = {}
    for n in ss_names:
        size = w_sh[n].shape[ss_axes[n]]
        g_loc[n] = lax.dynamic_slice_in_dim(sm_full[n], chip * size, size, axis=ss_axes[n])
    for n in SMALL_REPL:
        g_loc[n] = sm_full[n]
    n_sml = _round_up(sum(math.prod(w_sh[n].shape) for n in sm_full_names), 8 * SMALL_LANES)
    pack_sm = lambda d: _flat_cat([d[n] for n in sm_full_names], n_sml, F32).reshape(-1, SMALL_LANES)
    d_sm, m_sm, v_sm = adamw(pack_sm(g_loc), pack_sm(w_sh), pack_sm(m_sh), pack_sm(v_sh), name="adamw_small",
                             tb=n_sml // SMALL_LANES)
    sm_loc_shapes = [w_sh[n].shape for n in sm_full_names]
    res["grad"].update(g_loc)
    for tag, smv in (("delta", d_sm), ("new_m", m_sm), ("new_v", v_sm)):
        res[tag].update(dict(zip(sm_full_names, _split_flat(smv.reshape(-1), sm_loc_shapes))))
    return (loss, grad_x[None], *[res[tag][n] for tag in ("grad", "delta", "new_m", "new_v") for n in WEIGHTS])
```

```python
import functools
import math

import numpy as np
import jax
import jax.numpy as jnp
from jax import lax
from jax.experimental import pallas as pl
from jax.experimental.pallas import tpu as pltpu

F32 = jnp.float32
BF16 = jnp.bfloat16
MMD = jnp.bfloat16

D_MODEL = 1024
HEAD_DIM = 64
N_MEM = 256
MEM_WIDTH = 256
RWKV_HEADS = 12
RWKV_WIDTH = 768
SHIFT_WIDTH = 2560
LORA_WIDTH = 256
DIL_WIDTH = 768
DIL_GROUPS = ((128, 1), (512, 4), (2048, 16))
DIL_BLOCK = 128
D_FF = 2816
ROPE_THETA = 10000.0
RMS_EPS = 1e-6
LNX_EPS = 64e-5
NEG_INF = -1e30
ADAM_LR = 0.001
ADAM_B1 = 0.9
ADAM_B2 = 0.999
ADAM_EPS = 1e-08
ADAM_WD = 0.01
ADAM_STEP = 10
N_CHIPS = 4
MESH = pl.DeviceIdType.MESH
VMEM_LIMIT_MB = 56
SCAN_CHUNK = 32
SCAN_UNROLL = 8
SCAN_UNROLL_BWD = 4

BIG = (("a_w_in", 2), ("a_w_out", 1), ("kv_w", 1), ("b_w_in", 1), ("b_w_out", 2), ("mem_w_kv", 1),
       ("ffn_w_up", 2), ("ffn_w_down", 1))
EARLY_BIG = ("a_w_in",)
NATURAL_BIG = "ffn_w_up"
SMALL_SHARDED = (("a_mu", 1), ("a_w0", 1), ("a_w2", 2), ("a_a0", 1), ("a_a2", 2), ("a_g2", 2), ("a_k_k", 1),
                 ("a_k_a", 1), ("a_lnx_w", 1), ("a_lnx_b", 1), ("ffn_conv_w", 2))
SMALL_REPL = ("attn_norm", "a_r_k", "kv_norm", "kv_k_norm", "b_q_norm", "mem_norm", "mem_q_norm", "mem_k_norm",
              "ffn_norm", "ffn_conv_b")
WEIGHTS = ("attn_norm", "a_w_in", "a_mu", "a_w0", "a_w2", "a_a0", "a_a2", "a_g2", "a_k_k", "a_k_a", "a_r_k",
           "a_lnx_w", "a_lnx_b", "a_w_out", "kv_norm", "kv_w", "kv_k_norm", "b_w_in", "b_q_norm", "b_w_out",
           "mem_norm", "mem_w_kv", "mem_q_norm", "mem_k_norm", "ffn_norm", "ffn_w_up", "ffn_conv_w", "ffn_conv_b",
           "ffn_w_down")


def _cp(sem=None, **kw):
    return pltpu.CompilerParams(dimension_semantics=sem, vmem_limit_bytes=VMEM_LIMIT_MB << 20, **kw)


def _tile(n, cands=(512, 256, 128)):
    for c in cands:
        if n % c == 0:
            return c
    return n


def _mm(a, b, *, name, ta=False, tb=False, add=None, out_dtype=F32):
    K, M = a.shape if ta else a.shape[::-1]
    N = b.shape[0] if tb else b.shape[1]
    assert K == (b.shape[1] if tb else b.shape[0])
    tm, tn = _tile(M, (512, 256, 128) if ta else (1024, 512, 256, 128)), _tile(N, (512, 1408, 256, 128))
    bytes_of = lambda z: z.size * z.dtype.itemsize
    kept = 1 if bytes_of(b) + bytes_of(a) * (N // tn) < bytes_of(a) + bytes_of(b) * (M // tm) else 0
    mi, ni = ((lambda o, i: i), (lambda o, i: o)) if kept else ((lambda o, i: o), (lambda o, i: i))
    grid = (N // tn, M // tm) if kept else (M // tm, N // tn)
    a_blk, b_blk = ((K, tm) if ta else (tm, K)), ((tn, K) if tb else (K, tn))
    a_spec = pl.BlockSpec(a_blk, (lambda o, i: (0, mi(o, i))) if ta else (lambda o, i: (mi(o, i), 0)))
    b_spec = pl.BlockSpec(b_blk, (lambda o, i: (ni(o, i), 0)) if tb else (lambda o, i: (0, ni(o, i))))
    o_spec = pl.BlockSpec((tm, tn), lambda o, i: (mi(o, i), ni(o, i)))
    dn = (((0,) if ta else (1,), (1,) if tb else (0,)), ((), ()))
    has_add = add is not None
    cache = (b if kept else a).dtype != MMD

    def body(*refs):
        vals = [refs[0], refs[1]]
        o_ref = refs[2 + has_add]
        if cache:
            scr = refs[-1]

            @pl.when(pl.program_id(1) == 0)
            def _():
                scr[...] = vals[kept][...].astype(MMD)

            vals[kept] = scr
        acc = lax.dot_general(vals[0][...].astype(MMD), vals[1][...].astype(MMD), dn, preferred_element_type=F32)
        if has_add:
            acc = acc + refs[2][...]
        o_ref[...] = acc.astype(o_ref.dtype)

    ins = [a, b] + ([add] if has_add else [])
    specs = [a_spec, b_spec] + ([o_spec] if has_add else [])
    return pl.pallas_call(
        body, name=name, grid=grid, in_specs=specs, out_specs=o_spec,
        out_shape=jax.ShapeDtypeStruct((M, N), out_dtype),
        scratch_shapes=[pltpu.VMEM(b_blk if kept else a_blk, MMD)] if cache else [],
        compiler_params=_cp(("parallel", "arbitrary")),
    )(*ins)


def _rowmap(fn, *, name, T, tb, ins, outs, accs=()):
    nblk = T // tb
    assert T % tb == 0 and tb % 8 == 0
    in_specs, args = [], []
    for spec in ins:
        kind, arr = spec[0], spec[1]
        w, cb = (spec[2], spec[3]) if len(spec) > 2 else (arr.shape[-1], 0)
        if kind == "row":
            in_specs.append(pl.BlockSpec((tb, w), lambda i, cb=cb: (i, cb)))
        elif kind == "prev":
            in_specs.append(pl.BlockSpec((8, w), lambda i, cb=cb: (jnp.maximum(i * (tb // 8) - 1, 0), cb)))
        elif kind == "next":
            in_specs.append(pl.BlockSpec((8, w), lambda i, cb=cb: (jnp.minimum((i + 1) * (tb // 8), T // 8 - 1), cb)))
        elif kind == "const":
            in_specs.append(pl.BlockSpec(arr.shape, lambda i, nd=arr.ndim: (0,) * nd))
        else:
            raise ValueError(kind)
        args.append(arr)
    out_shape, out_specs = [], []
    for kind, w, dt in outs:
        out_shape.append(jax.ShapeDtypeStruct((T, w), dt))
        out_specs.append(pl.BlockSpec((tb, w), lambda i: (i, 0)))
    for shp, dt in accs:
        out_shape.append(jax.ShapeDtypeStruct(shp, dt))
        out_specs.append(pl.BlockSpec(shp, lambda i, nd=len(shp): (0,) * nd))
    n_in, n_out = len(ins), len(outs)

    def body(*refs):
        i = pl.program_id(0)
        vals = [r[...] for r in refs[:n_in]]
        res = fn(i, nblk, *vals)
        if not isinstance(res, (tuple, list)):
            res = (res,)
        assert len(res) == n_out + len(accs), (name, len(res))
        for r, v in zip(refs[n_in:n_in + n_out], res[:n_out]):
            r[...] = v.astype(r.dtype)
        acc_refs = refs[n_in + n_out:]
        if acc_refs:
            @pl.when(i == 0)
            def _():
                for r in acc_refs:
                    r[...] = jnp.zeros(r.shape, r.dtype)

            for r, v in zip(acc_refs, res[n_out:]):
                r[...] += v

    res = pl.pallas_call(
        body, name=name, grid=(nblk,), in_specs=in_specs, out_specs=out_specs, out_shape=out_shape,
        compiler_params=_cp(("arbitrary",)),
    )(*args)
    return res


def _row_pick(halo, r):
    rid = lax.broadcasted_iota(jnp.int32, halo.shape, 0)
    return jnp.sum(jnp.where(rid == r, halo, 0.0), axis=0, keepdims=True)


def _shift_down(x, row_before, is_first):
    rid = lax.broadcasted_iota(jnp.int32, x.shape, 0)
    first = jnp.where(is_first, 0.0, 1.0) * row_before
    return jnp.where(rid == 0, first, pltpu.roll(x, 1, axis=0))


def _shift_up(x, row_after, is_last):
    n = x.shape[0]
    rid = lax.broadcasted_iota(jnp.int32, x.shape, 0)
    last = jnp.where(is_last, 0.0, 1.0) * row_after
    return jnp.where(rid == n - 1, last, pltpu.roll(x, n - 1, axis=0))


def _dot(a, b, dn=(((1,), (0,)), ((), ()))):
    return lax.dot_general(a.astype(MMD), b.astype(MMD), dn, preferred_element_type=F32)


def _dot_nt(a, b):
    return _dot(a, b, (((1,), (1,)), ((), ())))


def _dot_tn(a, b):
    return _dot(a, b, (((0,), (0,)), ((), ())))


def _dot_exact01(x, g01):
    hi = x.astype(BF16)
    lo = (x - hi.astype(F32)).astype(BF16)
    dn = (((1,), (0,)), ((), ()))
    return (lax.dot_general(hi, g01, dn, preferred_element_type=F32)
            + lax.dot_general(lo, g01, dn, preferred_element_type=F32))


def _fold_heads(v, fold):
    return _row_pick(_dot_exact01(jnp.broadcast_to(v, (8, v.shape[1])), fold), 0)


def _group_ones(width):
    idx = np.arange(width) // HEAD_DIM
    return jnp.asarray((idx[:, None] == idx[None, :]).astype(np.float32), BF16)


def _fold_ones(width):
    idx = np.arange(width) % HEAD_DIM
    return jnp.asarray((idx[:, None] == np.arange(HEAD_DIM)[None, :]).astype(np.float32), BF16)


def _head_masks(width):
    idx = np.arange(width) // HEAD_DIM
    return jnp.asarray((idx[None, :] == np.arange(width // HEAD_DIM)[:, None]).astype(np.float32)[:, None, :], F32)


def _rms_stats(x):
    r = lax.rsqrt(jnp.mean(x * x, axis=-1, keepdims=True) + RMS_EPS)
    return r, x * r


def rms_fwd(x, gains, *, name):
    T, D = x.shape

    def fn(i, nblk, xb, *gs):
        _, xh = _rms_stats(xb)
        return tuple(xh * g for g in gs)

    return _rowmap(fn, name=name, T=T, tb=512, ins=[("row", x)] + [("const", g) for g in gains],
                   outs=[("row", D, MMD)] * len(gains))


def rms_bwd(x, gains, dhs, dres, *, name):
    T, D = x.shape
    n = len(gains)

    def fn(i, nblk, xb, dr, *rest):
        gs, ds = rest[:n], rest[n:]
        r, xh = _rms_stats(xb)
        dx = dr
        dgs = []
        for g, dh in zip(gs, ds):
            dgs.append(jnp.sum(dh * xh, axis=0, keepdims=True))
            dxh = dh * g
            dx = dx + r * (dxh - xh * jnp.mean(dxh * xh, axis=-1, keepdims=True))
        return (dx, *dgs)

    return _rowmap(fn, name=name, T=T, tb=512,
                   ins=[("row", x), ("row", dres)] + [("const", g) for g in gains] + [("row", d) for d in dhs],
                   outs=[("row", D, F32)], accs=[((1, D), F32)] * n)


def _segsum(x):
    first = lax.broadcasted_iota(jnp.int32, (x.shape[0], 128), 1) < HEAD_DIM
    outs = []
    for p in range(x.shape[1] // 128):
        xs = x[:, p * 128:(p + 1) * 128]
        lo = jnp.sum(jnp.where(first, xs, 0.0), axis=-1, keepdims=True)
        hi = jnp.sum(jnp.where(first, 0.0, xs), axis=-1, keepdims=True)
        outs.append(jnp.where(first, lo, hi))
    return jnp.concatenate(outs, axis=1)


def _pre1_common(i, ps, halo, mu, w0, a0, w2p, a2p, g2p, k_k, k_a):
    prev = _shift_down(ps, _row_pick(halo, 7), i == 0)
    xs = ps + (prev - ps) * mu
    lo = xs[:, 3 * RWKV_WIDTH:]
    tl, sl = jnp.tanh(lo), jax.nn.sigmoid(lo)
    dec = w0 + _dot(tl, w2p)
    ain = a0 + _dot(lo, a2p)
    g = _dot(sl, g2p)
    wl = -jax.nn.softplus(-dec) - 0.5
    w = jnp.exp(-jnp.exp(wl))
    a = jax.nn.sigmoid(ain)
    k = xs[:, RWKV_WIDTH:2 * RWKV_WIDTH]
    z = k * k_k
    nrm = jnp.sqrt(_segsum(z * z))
    kk = z / jnp.maximum(nrm, 1e-12)
    return prev, xs, lo, tl, sl, dec, wl, w, a, g, k, nrm, kk


def rwkv_pre_fwd(p, mu, w0, a0, w2p, a2p, g2p, k_k, k_a):
    T = p.shape[0]

    def fn(i, nblk, ps, halo, mu, w0, a0, w2p, a2p, g2p, k_k, k_a):
        _, xs, _, _, _, _, _, w, a, g, k, _, kk = _pre1_common(i, ps, halo, mu, w0, a0, w2p, a2p, g2p, k_k, k_a)
        W = RWKV_WIDTH
        return xs[:, :W], w, k * (1.0 + (a - 1.0) * k_a), xs[:, 2 * W:3 * W], kk, kk * a, g

    return _rowmap(fn, name="rwkv_pre_fwd", T=T, tb=256,
                   ins=[("row", p, SHIFT_WIDTH, 0), ("prev", p, SHIFT_WIDTH, 0)]
                   + [("const", c) for c in (mu, w0, a0, w2p, a2p, g2p, k_k, k_a)],
                   outs=[("row", RWKV_WIDTH, F32)] * 7)


def rwkv_pre_bwd(p, mu, w0, a0, w2p, a2p, g2p, k_k, k_a, drs, dw, dk2s, dvs, dkk, dkka, dg):
    T = p.shape[0]

    def fn(i, nblk, ps, halo, mu, w0, a0, w2p, a2p, g2p, k_k, k_a, dr0, dr1, dw, dk20, dk21, dv0, dv1, dkk, dkka, dg):
        prev, xs, lo, tl, sl, dec, wl, w, a, g, k, nrm, kk = _pre1_common(i, ps, halo, mu, w0, a0, w2p, a2p, g2p, k_k, k_a)
        dk2 = dk20 + dk21
        dkk_t = dkk + dkka * a
        proj = jnp.where(nrm > 1e-12, kk * _segsum(dkk_t * kk), 0.0)
        dz = (dkk_t - proj) / jnp.maximum(nrm, 1e-12)
        dk = dz * k_k + dk2 * (1.0 + (a - 1.0) * k_a)
        da = dkka * kk + dk2 * k * k_a
        ddec = dw * (-w * jnp.exp(wl)) * jax.nn.sigmoid(-dec)
        dain = da * a * (1.0 - a)
        dlo = (_dot_nt(ddec, w2p) * (1.0 - tl * tl) + _dot_nt(dain, a2p) + _dot_nt(dg, g2p) * sl * (1.0 - sl))
        dxs = jnp.concatenate([dr0 + dr1, dk, dv0 + dv1, dlo], axis=1)
        s = lambda z: jnp.sum(z, axis=0, keepdims=True)
        return (dxs, s(dxs * (prev - ps)), s(ddec), s(dain), _dot_tn(tl, ddec), _dot_tn(lo, dain), _dot_tn(sl, dg),
                s(dz * k), s(dk2 * k * (a - 1.0)))

    return _rowmap(fn, name="rwkv_pre_bwd", T=T, tb=128,
                   ins=[("row", p, SHIFT_WIDTH, 0), ("prev", p, SHIFT_WIDTH, 0)]
                   + [("const", c) for c in (mu, w0, a0, w2p, a2p, g2p, k_k, k_a)]
                   + [("row", c) for c in (*drs, dw, *dk2s, *dvs, dkk, dkka, dg)],
                   outs=[("row", SHIFT_WIDTH, F32)],
                   accs=[((1, SHIFT_WIDTH), F32), ((1, RWKV_WIDTH), F32), ((1, RWKV_WIDTH), F32)]
                   + [((LORA_WIDTH, RWKV_WIDTH), F32)] * 3 + [((1, RWKV_WIDTH), F32)] * 2)


def shift_bwd(dxs, mu, dq_mem):
    T = dxs.shape[0]

    def fn(i, nblk, d, halo, mu, dq):
        nxt = _shift_up(d, _row_pick(halo, 0), i == nblk - 1)
        return jnp.concatenate([d * (1.0 - mu) + nxt * mu, dq], axis=1)

    return _rowmap(fn, name="shift_bwd", T=T, tb=256,
                   ins=[("row", dxs), ("next", dxs), ("const", mu), ("row", dq_mem)],
                   outs=[("row", SHIFT_WIDTH + MEM_WIDTH, MMD)])[0]


N_PAIRS = RWKV_HEADS // 2


def _pair_consts():
    row = lax.broadcasted_iota(jnp.int32, (HEAD_DIM, 128), 0)
    lane = lax.broadcasted_iota(jnp.int32, (HEAD_DIM, 128), 1)
    eye2 = jnp.logical_or(lane == row, lane == row + HEAD_DIM).astype(F32)
    li = lax.broadcasted_iota(jnp.int32, (128, 128), 0) < HEAD_DIM
    lj = lax.broadcasted_iota(jnp.int32, (128, 128), 1) < HEAD_DIM
    return eye2, (li == lj).astype(BF16)


def _pair_sum(p, ones2):
    n, m, l = p.shape
    s = lax.dot_general(p.reshape(n * m, l).astype(BF16), ones2, (((1,), (0,)), ((), ())), preferred_element_type=F32)
    return s.reshape(n, m, l)


def _pair_rows(row):
    return jnp.stack([row[:, p * 128:(p + 1) * 128] for p in range(N_PAIRS)], axis=0)


def _pair_flat(rows):
    return jnp.concatenate([rows[p] for p in range(N_PAIRS)], axis=1)


def _split_bf16(v):
    hi = v.astype(BF16).astype(F32)
    return hi, v - hi


def _gather_copies(srcs, dsts, send_sems, recv_sems):
    x, y, c = _mesh_pos()
    s = 2 * x + y
    me, sibling = (x, y, c), (x, y, 1 - c)
    rc = functools.partial(_remote, send_sems, recv_sems)
    ici, land, fwd, arrived = [], [], [], []
    for b, (src, dst) in enumerate(zip(srcs, dsts)):
        for j, (cx, cy) in enumerate(_other_chips(x, y)):
            k = 6 * b + j
            ici.append(rc(k, src.at[c], dst.at[s, c], (cx, cy, c)))
            blk, blk2 = dst.at[2 * cx + cy, c], dst.at[2 * cx + cy, 1 - c]
            land.append(rc(k, blk, blk, me))
            fwd.append(rc(k + 3, blk, blk, sibling))
            arrived.append(rc(k + 3, blk2, blk2, me))
    return ici, land, fwd, arrived


def scan_fwd(r, w, k2, v, kk, kka, gather=None):
    T, W = r.shape
    tc = SCAN_CHUNK
    nchunk = T // tc
    seq = pl.BlockSpec((tc, W), lambda i: (i, 0))
    one_state = pl.BlockSpec((N_PAIRS, HEAD_DIM, 128), lambda i: (0, 0, 0))
    nb = 0 if gather is None else len(gather)

    def body(r_ref, w_ref, k2_ref, v_ref, kk_ref, kka_ref, *rest):
        if gather is None:
            y_ref, st_ref, fin_ref, s_scr, vhi_scr, vlo_scr = rest
        else:
            srcs, (y_ref, st_ref, fin_ref), dsts = rest[:nb], rest[nb:nb + 3], rest[nb + 3:2 * nb + 3]
            s_scr, vhi_scr, vlo_scr, send_sems, recv_sems = rest[2 * nb + 3:]
            ici, land, fwd, arrived = _gather_copies(srcs, dsts, send_sems, recv_sems)

            @pl.when(pl.program_id(0) == 0)
            def _():
                for cp in ici:
                    cp.start()

            @pl.when(pl.program_id(0) == nchunk // 2)
            def _():
                for a, f in zip(land, fwd):
                    a.wait_recv()
                    f.start()

        @pl.when(pl.program_id(0) == 0)
        def _():
            s_scr[...] = jnp.zeros(s_scr.shape, F32)

        vhi_scr[...], vlo_scr[...] = _split_bf16(v_ref[...])
        eye2, ones2 = _pair_consts()
        eye2b = eye2.astype(BF16)

        def step(t, carry):
            r_t, w_t, k2_t, kk_t, kka_t, vhi_t, vlo_t = (
                _pair_rows(ref[pl.ds(t, 1), :]) for ref in (r_ref, w_ref, k2_ref, kk_ref, kka_ref, vhi_scr, vlo_scr))
            S = s_scr[...]
            sa = -_pair_sum(S * kk_t, ones2)
            vb = _pair_sum(eye2b * vhi_t.astype(BF16), ones2) + _pair_sum(eye2b * vlo_t.astype(BF16), ones2)
            S2 = S * w_t + sa * kka_t + vb * k2_t
            y_ref[pl.ds(t, 1), :] = _pair_flat(jnp.sum(eye2 * _pair_sum(S2 * r_t, ones2), axis=1, keepdims=True))
            s_scr[...] = S2
            st_ref[t] = S
            return carry

        lax.fori_loop(0, tc, step, 0, unroll=SCAN_UNROLL)
        fin_ref[...] = s_scr[...]

        if gather is not None:
            @pl.when(pl.program_id(0) == nchunk - 1)
            def _():
                for a in arrived:
                    a.wait_recv()
                for cp in ici + fwd:
                    cp.wait_send()

    in_specs = [seq] * 6
    out_specs = [seq, pl.BlockSpec((tc, N_PAIRS, HEAD_DIM, 128), lambda i: (i, 0, 0, 0)), one_state]
    out_shape = [jax.ShapeDtypeStruct((T, W), F32), jax.ShapeDtypeStruct((T, N_PAIRS, HEAD_DIM, 128), F32),
                 jax.ShapeDtypeStruct((N_PAIRS, HEAD_DIM, 128), F32)]
    scratch = [pltpu.VMEM((N_PAIRS, HEAD_DIM, 128), F32), pltpu.VMEM((tc, W), F32), pltpu.VMEM((tc, W), F32)]
    args = [r, w, k2, v, kk, kka]
    if gather is not None:
        in_specs += [HBM_SPEC] * nb
        out_specs += [HBM_SPEC] * nb
        out_shape += [jax.ShapeDtypeStruct((N_CHIPS, *g.shape), g.dtype) for g in gather]
        scratch += [pltpu.SemaphoreType.DMA((6 * nb,)), pltpu.SemaphoreType.DMA((6 * nb,))]
        args += list(gather)
    return pl.pallas_call(
        body, name="rwkv_scan_fwd", grid=(nchunk,), in_specs=in_specs, out_specs=out_specs, out_shape=out_shape,
        scratch_shapes=scratch, compiler_params=_cp(("arbitrary",)),
    )(*args)


PEER_FLIPS = tuple((fx, fy, fc) for fx in (0, 1) for fy in (0, 1) for fc in (0, 1))[1:]


def scan_bwd(r, w, k2, v, kk, kka, states, final_state, dy, scatter=None):
    T, W = r.shape
    tc = SCAN_CHUNK
    nchunk = T // tc
    seq = pl.BlockSpec((tc, W), lambda i: (nchunk - 1 - i, 0))
    st_spec = pl.BlockSpec((tc, N_PAIRS, HEAD_DIM, 128), lambda i: (nchunk - 1 - i, 0, 0, 0))
    one_state = pl.BlockSpec((N_PAIRS, HEAD_DIM, 128), lambda i: (0, 0, 0))
    nb, npeer = (0 if scatter is None else len(scatter)), len(PEER_FLIPS)

    def body(r_ref, w_ref, k2_ref, v_ref, kk_ref, kka_ref, st_ref, fin_ref, dy_ref, *rest):
        if scatter is None:
            dr_ref, dw_ref, dk2_ref, dv_ref, dkk_ref, dkka_ref, ds_scr, sc_scr, vhi_scr, vlo_scr = rest
        else:
            srcs, dsts = rest[:nb], rest[nb + 6:2 * nb + 6]
            dr_ref, dw_ref, dk2_ref, dv_ref, dkk_ref, dkka_ref = rest[nb:nb + 6]
            ds_scr, sc_scr, vhi_scr, vlo_scr, send_sems, recv_sems = rest[2 * nb + 6:]
            x, y, c = _mesh_pos()
            copies = []
            for b, (src, dst) in enumerate(zip(srcs, dsts)):
                for k, (fx, fy, fc) in enumerate(PEER_FLIPS):
                    px, py, pc = (1 - x if fx else x), (1 - y if fy else y), (1 - c if fc else c)
                    copies.append(_remote(send_sems, recv_sems, npeer * b + k, src.at[2 * px + py, pc], dst.at[k],
                                          (px, py, pc)))

            @pl.when(pl.program_id(0) == 0)
            def _():
                for cp in copies:
                    cp.start()

        @pl.when(pl.program_id(0) == 0)
        def _():
            ds_scr[...] = jnp.zeros(ds_scr.shape, F32)
            sc_scr[...] = fin_ref[...]

        vhi_scr[...], vlo_scr[...] = _split_bf16(v_ref[...])
        eye2, ones2 = _pair_consts()
        eye2b = eye2.astype(BF16)
        colsum = lambda z: jnp.sum(z, axis=1, keepdims=True)

        def step(j, carry):
            t = tc - 1 - j
            r_t, w_t, k2_t, kk_t, kka_t, vhi_t, vlo_t, dy_t = (
                _pair_rows(ref[pl.ds(t, 1), :])
                for ref in (r_ref, w_ref, k2_ref, kk_ref, kka_ref, vhi_scr, vlo_scr, dy_ref))
            s_prev, s_cur = st_ref[t], sc_scr[...]
            dyb = _pair_sum(eye2b * dy_t.astype(BF16), ones2)
            vb = _pair_sum(eye2b * vhi_t.astype(BF16), ones2) + _pair_sum(eye2b * vlo_t.astype(BF16), ones2)
            sa = -_pair_sum(s_prev * kk_t, ones2)
            dS = ds_scr[...] + dyb * r_t
            dsa = _pair_sum(dS * kka_t, ones2)
            ds_scr[...] = dS * w_t - dsa * kk_t
            sc_scr[...] = s_prev
            for ref, val in zip((dr_ref, dw_ref, dk2_ref, dv_ref, dkk_ref, dkka_ref),
                                (s_cur * dyb, dS * s_prev, dS * vb, eye2 * _pair_sum(dS * k2_t, ones2),
                                 -(s_prev * dsa), dS * sa)):
                ref[pl.ds(t, 1), :] = _pair_flat(colsum(val))
            return carry

        lax.fori_loop(0, tc, step, 0, unroll=SCAN_UNROLL_BWD)

        if scatter is not None:
            @pl.when(pl.program_id(0) == nchunk - 1)
            def _():
                for cp in copies:
                    cp.wait()

    in_specs = [seq] * 6 + [st_spec, one_state, seq]
    out_specs = [seq] * 6
    out_shape = [jax.ShapeDtypeStruct((T, W), F32)] * 6
    scratch = [pltpu.VMEM((N_PAIRS, HEAD_DIM, 128), F32)] * 2 + [pltpu.VMEM((tc, W), F32)] * 2
    args = [r, w, k2, v, kk, kka, states, final_state, dy]
    if scatter is not None:
        in_specs += [HBM_SPEC] * nb
        out_specs += [HBM_SPEC] * nb
        out_shape += [jax.ShapeDtypeStruct((npeer, *s.shape[2:]), s.dtype) for s in scatter]
        scratch += [pltpu.SemaphoreType.DMA((npeer * nb,)), pltpu.SemaphoreType.DMA((npeer * nb,))]
        args += list(scatter)
    return pl.pallas_call(
        body, name="rwkv_scan_bwd", grid=(nchunk,), in_specs=in_specs, out_specs=out_specs, out_shape=out_shape,
        scratch_shapes=scratch, compiler_params=_cp(("arbitrary",)),
    )(*args)


def _mix_common(y, r, k2, v, lnx_w, lnx_b, r_k):
    yc = y - _segsum(y) * (1.0 / HEAD_DIM)
    rstd = lax.rsqrt(_segsum(yc * yc) * (1.0 / HEAD_DIM) + LNX_EPS)
    yhat = yc * rstd
    s = _segsum(r * k2 * r_k)
    return rstd, yhat, s, yhat * lnx_w + lnx_b + s * v


def mix_gate_fwd(y, r, k2, v, g, y_mem, lnx_w, lnx_b, r_k):
    T = y.shape[0]

    def fn(i, nblk, y, r, k2, v, g, ym, lw, lb, rk):
        mix = _mix_common(y, r, k2, v, lw, lb, rk)[3]
        return jnp.concatenate([mix * g, ym], axis=1)

    return _rowmap(fn, name="mix_gate_fwd", T=T, tb=256,
                   ins=[("row", z) for z in (y, r, k2, v, g, y_mem)] + [("const", c) for c in (lnx_w, lnx_b, r_k)],
                   outs=[("row", RWKV_WIDTH + MEM_WIDTH, MMD)])[0]


def mix_gate_bwd(y, r, k2, v, g, dycat, lnx_w, lnx_b, r_k):
    T = y.shape[0]

    def fn(i, nblk, y, r, k2, v, g, dyc, lw, lb, rk):
        rstd, yhat, s, mix = _mix_common(y, r, k2, v, lw, lb, rk)
        dmix = dyc * g
        dyh = dmix * lw
        inv = 1.0 / HEAD_DIM
        dy = rstd * (dyh - _segsum(dyh) * inv - yhat * (_segsum(dyh * yhat) * inv))
        ds = _segsum(dmix * v)
        cs = lambda z: jnp.sum(z, axis=0, keepdims=True)
        return (dy, ds * k2 * rk, ds * r * rk, dmix * s, dyc * mix, cs(dmix * yhat), cs(dmix), cs(ds * r * k2))

    return _rowmap(fn, name="mix_gate_bwd", T=T, tb=256,
                   ins=[("row", z) for z in (y, r, k2, v, g)] + [("row", dycat, RWKV_WIDTH, 0)]
                   + [("const", c) for c in (lnx_w, lnx_b, r_k)],
                   outs=[("row", RWKV_WIDTH, F32)] * 5, accs=[((1, RWKV_WIDTH), F32)] * 3)


def _head_rms(x, gones):
    ms = _segsum(x * x) * (1.0 / HEAD_DIM)
    r = lax.rsqrt(ms + RMS_EPS)
    return r, x * r


def _head_rms_bwd(dxn_g, r, xh, gones):
    return r * (dxn_g - xh * (_segsum(dxn_g * xh) * (1.0 / HEAD_DIM)))


def mem_kv_fwd(mem, norm_g, w_kv, k_norm_t, *, name):
    gones = _group_ones(MEM_WIDTH)

    def body(mem_ref, g_ref, w_ref, kn_ref, go_ref, k_out, v_out):
        _, xh = _rms_stats(mem_ref[...])
        kv = _dot(xh * g_ref[...], w_ref[...])
        _, kh = _head_rms(kv[:, :MEM_WIDTH], go_ref[...])
        k_out[...] = kh * kn_ref[...]
        v_out[...] = kv[:, MEM_WIDTH:]

    return pl.pallas_call(
        body, name=name, out_shape=[jax.ShapeDtypeStruct((N_MEM, MEM_WIDTH), F32)] * 2, compiler_params=_cp(),
    )(mem, norm_g, w_kv, k_norm_t, gones)


def mem_kv_bwd(mem, norm_g, w_kv, k_norm_t, dkn, dv, *, name):
    gones, fold = _group_ones(MEM_WIDTH), _fold_ones(MEM_WIDTH)

    def body(mem_ref, g_ref, w_ref, kn_ref, go_ref, fo_ref, dkn_ref, dv_ref, dw_out, dg_out, dkg_out):
        _, xh = _rms_stats(mem_ref[...])
        hm = xh * g_ref[...]
        kv = _dot(hm, w_ref[...])
        r, kh = _head_rms(kv[:, :MEM_WIDTH], go_ref[...])
        dkn = dkn_ref[...]
        dkg_out[...] = _fold_heads(jnp.sum(dkn * kh, axis=0, keepdims=True), fo_ref[...])
        dkraw = _head_rms_bwd(dkn * kn_ref[...], r, kh, go_ref[...])
        dkv = jnp.concatenate([dkraw, dv_ref[...]], axis=1)
        dw_out[...] = _dot_tn(hm, dkv)
        dg_out[...] = jnp.sum(_dot_nt(dkv, w_ref[...]) * xh, axis=0, keepdims=True)

    return pl.pallas_call(
        body, name=name,
        out_shape=[jax.ShapeDtypeStruct((D_MODEL, 2 * MEM_WIDTH), F32), jax.ShapeDtypeStruct((1, D_MODEL), F32),
                   jax.ShapeDtypeStruct((1, HEAD_DIM), F32)],
        compiler_params=_cp(),
    )(mem, norm_g, w_kv, k_norm_t, gones, fold, dkn, dv)


def _mem_scores(qn, kn, masks, h):
    s = _dot_nt(qn * masks[h], kn) * (1.0 / math.sqrt(HEAD_DIM))
    s = s - jnp.max(s, axis=-1, keepdims=True)
    e = jnp.exp(s)
    return e / jnp.sum(e, axis=-1, keepdims=True)


def mem_attn_fwd(p, colblock, kn, v, q_norm_t, *, name):
    T = p.shape[0]
    gones, masks = _group_ones(MEM_WIDTH), _head_masks(MEM_WIDTH)

    def fn(i, nblk, q, kn, v, qg, go, masks):
        _, qh = _head_rms(q, go)
        qn = qh * qg
        out = jnp.zeros(q.shape, F32)
        for h in range(MEM_WIDTH // HEAD_DIM):
            out = out + _dot(_mem_scores(qn, kn, masks, h), v * masks[h])
        return out

    return _rowmap(fn, name=name, T=T, tb=512,
                   ins=[("row", p, MEM_WIDTH, colblock)] + [("const", c) for c in (kn, v, q_norm_t, gones, masks)],
                   outs=[("row", MEM_WIDTH, F32)])[0]


def mem_attn_bwd(p, colblock, kn, v, q_norm_t, dycat, dcolblock, *, name):
    T = p.shape[0]
    gones, masks, fold = _group_ones(MEM_WIDTH), _head_masks(MEM_WIDTH), _fold_ones(MEM_WIDTH)
    scale = 1.0 / math.sqrt(HEAD_DIM)

    def fn(i, nblk, q, dy, kn, v, qg, go, masks, fo):
        r, qh = _head_rms(q, go)
        qn = qh * qg
        dqn = jnp.zeros(q.shape, F32)
        dkn = jnp.zeros(kn.shape, F32)
        dv = jnp.zeros(v.shape, F32)
        for h in range(MEM_WIDTH // HEAD_DIM):
            pr = _mem_scores(qn, kn, masks, h)
            dyh = dy * masks[h]
            dpr = _dot_nt(dyh, v)
            ds = pr * (dpr - jnp.sum(dpr * pr, axis=-1, keepdims=True)) * scale
            dqn = dqn + _dot(ds, kn * masks[h])
            dkn = dkn + _dot_tn(ds, qn * masks[h])
            dv = dv + _dot_tn(pr, dyh)
        dqg = _fold_heads(jnp.sum(dqn * qh, axis=0, keepdims=True), fo)
        return _head_rms_bwd(dqn * qg, r, qh, go), dkn, dv, dqg

    return _rowmap(fn, name=name, T=T, tb=512,
                   ins=[("row", p, MEM_WIDTH, colblock), ("row", dycat, MEM_WIDTH, dcolblock)]
                   + [("const", c) for c in (kn, v, q_norm_t, gones, masks, fold)],
                   outs=[("row", MEM_WIDTH, F32)],
                   accs=[((N_MEM, MEM_WIDTH), F32), ((N_MEM, MEM_WIDTH), F32), ((1, HEAD_DIM), F32)])


def _ffn_conv(i, u, halo, cw, cb):
    up1 = _shift_down(u, _row_pick(halo, 7), i == 0)
    up2 = _shift_down(up1, _row_pick(halo, 6), i == 0)
    c = cb + cw[0] * up2 + cw[1] * up1 + cw[2] * u
    return up1, up2, c[:, :D_FF], c[:, D_FF:]


def ffn_act_fwd(u, cw, cb, *, name):
    T = u.shape[0]

    def fn(i, nblk, u, halo, c0, c1, c2, cb):
        _, _, gate, val = _ffn_conv(i, u, halo, (c0, c1, c2), cb)
        return jax.nn.silu(gate) * val

    return _rowmap(fn, name=name, T=T, tb=128, ins=[("row", u), ("prev", u)] + [("const", c) for c in (*cw, cb)],
                   outs=[("row", D_FF, MMD)])[0]


def ffn_act_bwd(u, cw, cb, dz, *, name):
    T = u.shape[0]
    tb = 128

    def fn(i, nblk, u, halo, unext, c0, c1, c2, cb, dz, dznext):
        ue = jnp.concatenate([u, unext], axis=0)
        dze = jnp.concatenate([dz, jnp.where(i == nblk - 1, 0.0, 1.0) * dznext], axis=0)
        up1, up2, gate, val = _ffn_conv(i, ue, halo, (c0, c1, c2), cb)
        sg = jax.nn.sigmoid(gate)
        dce = jnp.concatenate([dze * val * sg * (1.0 + gate * (1.0 - sg)), dze * gate * sg], axis=1)
        rows = dce.shape[0]
        du = (c2 * dce + c1 * pltpu.roll(dce, rows - 1, axis=0) + c0 * pltpu.roll(dce, rows - 2, axis=0))[:tb]
        dc = dce[:tb]
        s = lambda z: jnp.sum(z, axis=0, keepdims=True)
        return du, s(dc * up2[:tb]), s(dc * up1[:tb]), s(dc * u), s(dc)

    return _rowmap(fn, name=name, T=T, tb=tb,
                   ins=[("row", u), ("prev", u), ("next", u)] + [("const", c) for c in (*cw, cb)]
                   + [("row", dz), ("next", dz)],
                   outs=[("row", 2 * D_FF, MMD)], accs=[((1, 2 * D_FF), F32)] * 4)


def _rope_swap(z):
    lane = lax.broadcasted_iota(jnp.int32, z.shape, 1) % HEAD_DIM
    w = z.shape[1]
    return jnp.where(lane < HEAD_DIM // 2, pltpu.roll(z, w - HEAD_DIM // 2, axis=1), pltpu.roll(z, HEAD_DIM // 2, axis=1))


def rope_tables(T):
    inv = (np.float32(ROPE_THETA) ** (-np.arange(0, HEAD_DIM, 2, dtype=np.float32) / np.float32(HEAD_DIM))).astype(np.float32)
    ang = (np.arange(T, dtype=np.float32)[:, None] * inv[None, :]).astype(np.float64)
    cos, sin = np.cos(ang).astype(np.float32), np.sin(ang).astype(np.float32)
    return (jnp.asarray(np.concatenate([cos, cos, cos, cos], axis=1)),
            jnp.asarray(np.concatenate([-sin, sin, -sin, sin], axis=1)))


def _rope_wide(t):
    return jnp.tile(t, (1, DIL_WIDTH // t.shape[1]))


def qk_fwd(kvp, pb, kg_t, qg_t, cos, sin):
    T = kvp.shape[0]
    gones = _group_ones(DIL_WIDTH)

    def fn(i, nblk, kraw, vraw, qraw, kg, qg, c, s, go):
        c, s = _rope_wide(c), _rope_wide(s)
        outs = []
        for raw, g in ((qraw, qg), (kraw, kg)):
            _, xh = _head_rms(raw, go)
            z = xh * g
            outs.append(z * c + _rope_swap(z) * s)
        return outs[0], outs[1], vraw

    return _rowmap(fn, name="qk_fwd", T=T, tb=256,
                   ins=[("row", kvp, DIL_WIDTH, 0), ("row", kvp, DIL_WIDTH, 1), ("row", pb, DIL_WIDTH, 0)]
                   + [("const", kg_t), ("const", qg_t), ("row", cos), ("row", sin), ("const", gones)],
                   outs=[("row", DIL_WIDTH, MMD)] * 3)


def qk_bwd(kvp, pb, kg_t, qg_t, cos, sin, dq, dk, dv, dq_mem):
    T = kvp.shape[0]
    gones, fold = _group_ones(DIL_WIDTH), _fold_ones(DIL_WIDTH)

    def fn(i, nblk, kraw, qraw, kg, qg, c, s, go, fo, dq, dk, dv, dqm):
        c, s = _rope_wide(c), _rope_wide(s)
        res, dgs = [], []
        for raw, g, d in ((qraw, qg, dq), (kraw, kg, dk)):
            r, xh = _head_rms(raw, go)
            dz = d * c + _rope_swap(d * s)
            dgs.append(_fold_heads(jnp.sum(dz * xh, axis=0, keepdims=True), fo))
            res.append(_head_rms_bwd(dz * g, r, xh, go))
        return (jnp.concatenate([res[0], dqm], axis=1), jnp.concatenate([res[1], dv], axis=1), dgs[0], dgs[1])

    return _rowmap(fn, name="qk_bwd", T=T, tb=256,
                   ins=[("row", kvp, DIL_WIDTH, 0), ("row", pb, DIL_WIDTH, 0), ("const", kg_t), ("const", qg_t),
                        ("row", cos), ("row", sin), ("const", gones), ("const", fold),
                        ("row", dq), ("row", dk), ("row", dv), ("row", dq_mem)],
                   outs=[("row", DIL_WIDTH + MEM_WIDTH, MMD), ("row", 2 * DIL_WIDTH, MMD)],
                   accs=[((1, HEAD_DIM), F32)] * 2)


def _band(kind):
    i = lax.broadcasted_iota(jnp.int32, (DIL_BLOCK, DIL_BLOCK), 0)
    j = lax.broadcasted_iota(jnp.int32, (DIL_BLOCK, DIL_BLOCK), 1)
    return (j <= i) if kind == "cur" else (j >= i)


def dil_attn_fwd(q, k, v, seq_blocks, *, name):
    T, W = q.shape
    nb = T // DIL_BLOCK
    masks = _head_masks(W)
    cur = pl.BlockSpec((DIL_BLOCK, W), lambda n: (n, 0))
    prv = pl.BlockSpec((DIL_BLOCK, W), lambda n: (jnp.maximum(n - 1, 0), 0))
    scale = 1.0 / math.sqrt(HEAD_DIM)

    def body(q_ref, kc_ref, kp_ref, vc_ref, vp_ref, m_ref, o_ref, l_ref):
        n = pl.program_id(0)
        has_prev = (n % seq_blocks) != 0
        q = q_ref[...].astype(F32)
        kc, kp = kc_ref[...].astype(F32), kp_ref[...].astype(F32)
        vc, vp = vc_ref[...].astype(F32), vp_ref[...].astype(F32)
        ok_c = _band("cur")
        ok_p = jnp.logical_and(_band("prev"), has_prev)
        o = jnp.zeros((DIL_BLOCK, W), F32)
        lse = jnp.zeros((DIL_BLOCK, W), F32)
        for h in range(W // HEAD_DIM):
            mh = m_ref[h]
            qh = q * mh
            sc = jnp.where(ok_c, _dot_nt(qh, kc) * scale, NEG_INF)
            sp = jnp.where(ok_p, _dot_nt(qh, kp) * scale, NEG_INF)
            mx = jnp.maximum(jnp.max(sc, axis=-1, keepdims=True), jnp.max(sp, axis=-1, keepdims=True))
            ec, ep = jnp.exp(sc - mx), jnp.exp(sp - mx)
            den = jnp.sum(ec, axis=-1, keepdims=True) + jnp.sum(ep, axis=-1, keepdims=True)
            o = o + (_dot(ec, vc * mh) + _dot(ep, vp * mh)) / den
            lse = lse + (mx + jnp.log(den)) * mh
        o_ref[...] = o
        l_ref[...] = lse

    return pl.pallas_call(
        body, name=name, grid=(nb,), in_specs=[cur, cur, prv, cur, prv, pl.BlockSpec(masks.shape, lambda n: (0, 0, 0))],
        out_specs=[cur, cur], out_shape=[jax.ShapeDtypeStruct((T, W), F32)] * 2,
        compiler_params=_cp(("parallel",)),
    )(q, k, k, v, v, masks)


def dil_attn_bwd(q, k, v, o, lse, do, dlse, seq_blocks, *, name):
    T, W = q.shape
    nb = T // DIL_BLOCK
    masks = _head_masks(W)
    cur = pl.BlockSpec((DIL_BLOCK, W), lambda n: (n, 0))
    prv = pl.BlockSpec((DIL_BLOCK, W), lambda n: (jnp.maximum(n - 1, 0), 0))
    nxt = pl.BlockSpec((DIL_BLOCK, W), lambda n: (jnp.minimum(n + 1, nb - 1), 0))
    scale = 1.0 / math.sqrt(HEAD_DIM)

    def body(qc_ref, qn_ref, kc_ref, kp_ref, vc_ref, vp_ref, oc_ref, on_ref, lc_ref, ln_ref, doc_ref, don_ref,
             dlc_ref, dln_ref, m_ref, dq_ref, dk_ref, dv_ref):
        n = pl.program_id(0)
        has_prev = (n % seq_blocks) != 0
        has_next = jnp.logical_and(((n + 1) % seq_blocks) != 0, n + 1 < nb)
        f = lambda ref: ref[...].astype(F32)
        qc, qn, kc, kp, vc, vp = f(qc_ref), f(qn_ref), f(kc_ref), f(kp_ref), f(vc_ref), f(vp_ref)
        doc, don = doc_ref[...], don_ref[...]
        ok_c = _band("cur")
        ok_p = jnp.logical_and(_band("prev"), has_prev)
        ok_n = jnp.logical_and(_band("prev"), has_next)
        dq = jnp.zeros((DIL_BLOCK, W), F32)
        dk = jnp.zeros((DIL_BLOCK, W), F32)
        dv = jnp.zeros((DIL_BLOCK, W), F32)

        def side(qh, kk, vv, doh, lse_h, corr, ok):
            s = _dot_nt(qh, kk) * scale
            pr = jnp.where(ok, jnp.exp(jnp.where(ok, s, NEG_INF) - lse_h), 0.0)
            ds = pr * (_dot_nt(doh, vv) + corr) * scale
            return pr, ds

        for h in range(W // HEAD_DIM):
            mh = m_ref[h]
            red = lambda z: jnp.sum(z * mh, axis=-1, keepdims=True)
            qh, doh = qc * mh, doc * mh
            lse_h = red(lc_ref[...]) * (1.0 / HEAD_DIM)
            corr = red(dlc_ref[...]) - red(doc * oc_ref[...])
            pr_c, ds_c = side(qh, kc, vc * mh, doh, lse_h, corr, ok_c)
            _, ds_p = side(qh, kp, vp * mh, doh, lse_h, corr, ok_p)
            dq = dq + _dot(ds_c, kc * mh) + _dot(ds_p, kp * mh)
            dk = dk + _dot_tn(ds_c, qh)
            dv = dv + _dot_tn(pr_c, doh)
            qh2, doh2 = qn * mh, don * mh
            lse_2 = red(ln_ref[...]) * (1.0 / HEAD_DIM)
            corr2 = red(dln_ref[...]) - red(don * on_ref[...])
            pr_n, ds_n = side(qh2, kc, vc * mh, doh2, lse_2, corr2, ok_n)
            dk = dk + _dot_tn(ds_n, qh2)
            dv = dv + _dot_tn(pr_n, doh2)
        dq_ref[...] = dq
        dk_ref[...] = dk
        dv_ref[...] = dv

    return pl.pallas_call(
        body, name=name, grid=(nb,),
        in_specs=[cur, nxt, cur, prv, cur, prv, cur, nxt, cur, nxt, cur, nxt, cur, nxt,
                  pl.BlockSpec(masks.shape, lambda n: (0, 0, 0))],
        out_specs=[cur] * 3, out_shape=[jax.ShapeDtypeStruct((T, W), F32)] * 3,
        compiler_params=_cp(("parallel",)),
    )(q, q, k, k, v, v, o, o, lse, lse, do, do, dlse, dlse, masks)


def _mix_weights(ls):
    m = jnp.maximum(jnp.maximum(ls[0], ls[1]), ls[2])
    es = [jnp.exp(l - m) for l in ls]
    den = es[0] + es[1] + es[2]
    return [e / den for e in es]


def mix_fwd(os_, ls, y_mem):
    T = y_mem.shape[0]

    def fn(i, nblk, o0, o1, o2, l0, l1, l2, ym):
        w = _mix_weights((l0, l1, l2))
        return jnp.concatenate([w[0] * o0 + w[1] * o1 + w[2] * o2, ym], axis=1)

    return _rowmap(fn, name="mix_fwd", T=T, tb=512, ins=[("row", z) for z in (*os_, *ls, y_mem)],
                   outs=[("row", 2 * MEM_WIDTH, MMD)])[0]


def mix_bwd(os_, ls, dycat):
    T = dycat.shape[0]

    def fn(i, nblk, o0, o1, o2, l0, l1, l2, dy):
        w = _mix_weights((l0, l1, l2))
        os3 = (o0, o1, o2)
        dws = [dy * o for o in os3]
        tot = w[0] * dws[0] + w[1] * dws[1] + w[2] * dws[2]
        return tuple(wg * dy for wg in w) + tuple(wg * (dw - tot) for wg, dw in zip(w, dws))

    return _rowmap(fn, name="mix_bwd", T=T, tb=512,
                   ins=[("row", z) for z in (*os_, *ls)] + [("row", dycat, MEM_WIDTH, 0)],
                   outs=[("row", MEM_WIDTH, F32)] * 6)


def loss_fwd_bwd(y, target):
    T, D = y.shape

    def fn(i, nblk, y, t):
        e = y - t
        return e * (1.0 / D), jnp.zeros((8, 128), F32) + jnp.sum(e * e) * (0.5 / D)

    return _rowmap(fn, name="loss", T=T, tb=512, ins=[("row", y), ("row", target)], outs=[("row", D, F32)],
                   accs=[((8, 128), F32)])


def _to_residues(z, dil):
    T, W = z.shape
    return z.reshape(T // dil, dil, W).transpose(1, 0, 2).reshape(T, W)


def _from_residues(z, dil):
    T, W = z.shape
    return z.reshape(dil, T // dil, W).transpose(1, 0, 2).reshape(T, W)


def _pad_rows(w, rows):
    return jnp.concatenate([w, jnp.zeros((rows - w.shape[0], w.shape[1]), w.dtype)], axis=0)


def _tile_heads(g, width):
    return jnp.tile(g.reshape(1, HEAD_DIM), (1, width // HEAD_DIM))


def _conv_rows(W, i):
    return [W["ffn_conv_w"][i][j:j + 1] for j in range(3)]


def _ffn_fwd(x, i, W):
    hn = rms_fwd(x, [W["ffn_norm"][i:i + 1]], name=f"ffn_rms{i}")[0]
    u = _mm(hn, W["ffn_w_up"][i], name=f"ffn_up{i}")
    z = ffn_act_fwd(u, _conv_rows(W, i), W["ffn_conv_b"][i:i + 1], name=f"ffn_act{i}")
    out = _mm(z, W["ffn_w_down"][i], add=x, name=f"ffn_down{i}")
    return out, (x, hn, u, z)


def _ffn_bwd(dout, i, W, saved, G):
    x, hn, u, z = saved
    dz = _mm(dout, W["ffn_w_down"][i], tb=True, name=f"ffn_down_dx{i}")
    G["ffn_w_down"][i] = _mm(z, dout, ta=True, name=f"ffn_down_dw{i}")
    du, dw0, dw1, dw2, db = ffn_act_bwd(u, _conv_rows(W, i), W["ffn_conv_b"][i:i + 1], dz, name=f"ffn_act_bwd{i}")
    G["ffn_conv_w"][i] = jnp.concatenate([dw0, dw1, dw2], axis=0)
    G["ffn_conv_b"][i] = db[0]
    dhn = _mm(du, W["ffn_w_up"][i], tb=True, name=f"ffn_up_dx{i}")
    G["ffn_w_up"][i] = _mm(hn, du, ta=True, name=f"ffn_up_dw{i}")
    dx, dg = rms_bwd(x, [W["ffn_norm"][i:i + 1]], [dhn], dout, name=f"ffn_rms_bwd{i}")
    G["ffn_norm"][i] = dg[0]
    return dx


def local_step(x, mem, target, W, late=None):
    T = x.shape[0]
    W = dict(W)
    G = {"ffn_w_down": [None, None], "ffn_w_up": [None, None], "ffn_conv_w": [None, None],
         "ffn_conv_b": [None, None], "ffn_norm": [None, None], "attn_norm": [None, None], "mem_norm": [None, None],
         "mem_w_kv": [None, None], "mem_q_norm": [None, None], "mem_k_norm": [None, None]}
    mu, w0, a0 = W["a_mu"], W["a_w0"], W["a_a0"]
    w2p, a2p, g2p = (_pad_rows(W["a_w2"][0], LORA_WIDTH),
                     jnp.concatenate([jnp.zeros((64, RWKV_WIDTH), MMD), W["a_a2"][0],
                                      jnp.zeros((128, RWKV_WIDTH), MMD)], axis=0),
                     jnp.concatenate([jnp.zeros((128, RWKV_WIDTH), MMD), W["a_g2"][0]], axis=0))
    k_k, k_a, lnx_w, lnx_b = W["a_k_k"], W["a_k_a"], W["a_lnx_w"], W["a_lnx_b"]
    r_k = W["a_r_k"].reshape(1, RWKV_WIDTH)

    h0 = rms_fwd(x, [W["attn_norm"][0:1]], name="attn_rms0")[0]
    p = _mm(h0, W["a_w_in"][0], name="a_in")
    r, w, k2, v, kk, kka, g = rwkv_pre_fwd(p, mu, w0, a0, w2p, a2p, g2p, k_k, k_a)
    if late is None:
        y, states, final_state = scan_fwd(r, w, k2, v, kk, kka)
    else:
        y, states, final_state, *gathered = scan_fwd(r, w, k2, v, kk, kka, gather=late[0])
        W.update(late[1](gathered))
    memkv = []
    for i in range(2):
        memkv.append(mem_kv_fwd(mem, W["mem_norm"][i:i + 1], W["mem_w_kv"][i], _tile_heads(W["mem_k_norm"][i], MEM_WIDTH),
                                name=f"mem_kv{i}"))
    qg0 = _tile_heads(W["mem_q_norm"][0], MEM_WIDTH)
    y_mem0 = mem_attn_fwd(p, SHIFT_WIDTH // MEM_WIDTH, memkv[0][0], memkv[0][1], qg0, name="mem_attn0")
    ycat0 = mix_gate_fwd(y, r, k2, v, g, y_mem0, lnx_w, lnx_b, r_k)
    x1 = _mm(ycat0, W["a_w_out"][0], add=x, name="a_out")
    x2, ffn0 = _ffn_fwd(x1, 0, W)

    h1, hkv = rms_fwd(x2, [W["attn_norm"][1:2], W["kv_norm"].reshape(1, -1)], name="attn_rms1")
    kvp = _mm(hkv, W["kv_w"], name="kv_in")
    pb = _mm(h1, W["b_w_in"][0], name="b_in")
    cos, sin = rope_tables(T)
    kg_t, qg_t = _tile_heads(W["kv_k_norm"], DIL_WIDTH), _tile_heads(W["b_q_norm"][0], DIL_WIDTH)
    q, ksh, vsh = qk_fwd(kvp, pb, kg_t, qg_t, cos, sin)
    os_, ls, grp = [], [], []
    for gi, (win, dil) in enumerate(DIL_GROUPS):
        sl = slice(gi * MEM_WIDTH, (gi + 1) * MEM_WIDTH)
        qg_, kg_, vg_ = (_to_residues(z[:, sl], dil) for z in (q, ksh, vsh))
        o_r, l_r = dil_attn_fwd(qg_, kg_, vg_, T // dil // DIL_BLOCK, name=f"dil_fwd{gi}")
        grp.append((qg_, kg_, vg_, o_r, l_r))
        os_.append(_from_residues(o_r, dil))
        ls.append(_from_residues(l_r, dil))
    qg1 = _tile_heads(W["mem_q_norm"][1], MEM_WIDTH)
    y_mem1 = mem_attn_fwd(pb, DIL_WIDTH // MEM_WIDTH, memkv[1][0], memkv[1][1], qg1, name="mem_attn1")
    ycat1 = mix_fwd(os_, ls, y_mem1)
    x3 = _mm(ycat1, W["b_w_out"][0], add=x2, name="b_out")
    x4, ffn1 = _ffn_fwd(x3, 1, W)

    dx4, loss = loss_fwd_bwd(x4, target)

    dx3 = _ffn_bwd(dx4, 1, W, ffn1, G)
    dycat1 = _mm(dx3, W["b_w_out"][0], tb=True, name="b_out_dx")
    G["b_w_out"] = _mm(ycat1, dx3, ta=True, name="b_out_dw")[None]
    dq_mem1, dkn1, dvm1, dqg1 = mem_attn_bwd(pb, DIL_WIDTH // MEM_WIDTH, memkv[1][0], memkv[1][1], qg1, dycat1, 1,
                                             name="mem_attn_bwd1")
    G["mem_q_norm"][1] = dqg1[0]
    d_os_ls = mix_bwd(os_, ls, dycat1)
    dqs, dks, dvs = [], [], []
    for gi, (win, dil) in enumerate(DIL_GROUPS):
        qg_, kg_, vg_, o_r, l_r = grp[gi]
        do_r, dl_r = _to_residues(d_os_ls[gi], dil), _to_residues(d_os_ls[3 + gi], dil)
        dq_r, dk_r, dv_r = dil_attn_bwd(qg_, kg_, vg_, o_r, l_r, do_r, dl_r, T // dil // DIL_BLOCK, name=f"dil_bwd{gi}")
        dqs.append(_from_residues(dq_r, dil))
        dks.append(_from_residues(dk_r, dil))
        dvs.append(_from_residues(dv_r, dil))
    dq, dk, dv = (jnp.concatenate(z, axis=1) for z in (dqs, dks, dvs))
    dpb, dkvp, dqn_g, dkn_g = qk_bwd(kvp, pb, kg_t, qg_t, cos, sin, dq, dk, dv, dq_mem1)
    G["b_q_norm"] = dqn_g
    G["kv_k_norm"] = dkn_g[0]
    dh1 = _mm(dpb, W["b_w_in"][0], tb=True, name="b_in_dx")
    G["b_w_in"] = _mm(h1, dpb, ta=True, name="b_in_dw")[None]
    dhkv = _mm(dkvp, W["kv_w"], tb=True, name="kv_in_dx")
    G["kv_w"] = _mm(hkv, dkvp, ta=True, name="kv_in_dw")
    dx2, dg1, dgkv = rms_bwd(x2, [W["attn_norm"][1:2], W["kv_norm"].reshape(1, -1)], [dh1, dhkv], dx3,
                             name="attn_rms_bwd1")
    G["attn_norm"][1] = dg1[0]
    G["kv_norm"] = dgkv[0]

    dx1 = _ffn_bwd(dx2, 0, W, ffn0, G)
    dycat0 = _mm(dx1, W["a_w_out"][0], tb=True, name="a_out_dx")
    G["a_w_out"] = _mm(ycat0, dx1, ta=True, name="a_out_dw")[None]
    dq_mem0, dkn0, dvm0, dqg0 = mem_attn_bwd(p, SHIFT_WIDTH // MEM_WIDTH, memkv[0][0], memkv[0][1], qg0, dycat0,
                                             RWKV_WIDTH // MEM_WIDTH, name="mem_attn_bwd0")
    G["mem_q_norm"][0] = dqg0[0]
    dy, dr_b, dk2_b, dv_b, dg, dlw, dlb, drk = mix_gate_bwd(y, r, k2, v, g, dycat0, lnx_w, lnx_b, r_k)
    for i, (dkn, dvm) in enumerate(((dkn0, dvm0), (dkn1, dvm1))):
        dwkv, dgm, dkg = mem_kv_bwd(mem, W["mem_norm"][i:i + 1], W["mem_w_kv"][i],
                                    _tile_heads(W["mem_k_norm"][i], MEM_WIDTH), dkn, dvm, name=f"mem_kv_bwd{i}")
        G["mem_w_kv"][i], G["mem_norm"][i], G["mem_k_norm"][i] = dwkv, dgm[0], dkg[0]
    for n in list(G):
        if isinstance(G[n], list) and all(z is not None for z in G[n]):
            G[n] = jnp.stack(G[n], axis=0)
    late_out = None
    if late is None:
        dr, dw, dk2, dv, dkk, dkka = scan_bwd(r, w, k2, v, kk, kka, states, final_state, dy)
    else:
        pieces = late[2](G)
        dr, dw, dk2, dv, dkk, dkka, *received = scan_bwd(r, w, k2, v, kk, kka, states, final_state, dy, scatter=pieces)
        late_out = (received, pieces)
    dxs, dmu, dw0, da0, dw2p, da2p, dg2p, dk_k, dk_a = rwkv_pre_bwd(
        p, mu, w0, a0, w2p, a2p, g2p, k_k, k_a, (dr, dr_b), dw, (dk2, dk2_b), (dv, dv_b), dkk, dkka, dg)
    dp = shift_bwd(dxs, mu, dq_mem0)
    G.update(a_mu=dmu, a_w0=dw0, a_a0=da0, a_w2=dw2p[None, :64], a_a2=da2p[None, 64:128], a_g2=dg2p[None, 128:],
             a_k_k=dk_k, a_k_a=dk_a, a_r_k=drk.reshape(1, RWKV_HEADS, HEAD_DIM), a_lnx_w=dlw, a_lnx_b=dlb)
    dh0 = _mm(dp, W["a_w_in"][0], tb=True, name="a_in_dx")
    G["a_w_in"] = _mm(h0, dp, ta=True, name="a_in_dw")[None]
    grad_x, dg0 = rms_bwd(x, [W["attn_norm"][0:1]], [dh0], dx1, name="attn_rms_bwd0")
    G["attn_norm"][0] = dg0[0]
    G["attn_norm"] = jnp.stack(G["attn_norm"], axis=0)
    return loss, grad_x, G, late_out


HBM_SPEC = pl.BlockSpec(memory_space=pltpu.HBM)


def _mesh_pos():
    return lax.axis_index("x"), lax.axis_index("y"), lax.axis_index("c")


def _other_chips(x, y):
    return [(1 - x, y), (x, 1 - y), (1 - x, 1 - y)]


def _remote(send_sems, recv_sems, k, src, dst, to):
    return pltpu.make_async_remote_copy(src_ref=src, dst_ref=dst, send_sem=send_sems.at[k], recv_sem=recv_sems.at[k],
                                        device_id=to, device_id_type=MESH)


def _comm_call(body, name, ins, out_shape, n_remote):
    scratch = [pltpu.SemaphoreType.DMA((n_remote,)), pltpu.SemaphoreType.DMA((n_remote,))]
    return pl.pallas_call(body, name=name, in_specs=[HBM_SPEC] * len(ins), out_specs=[HBM_SPEC] * len(out_shape),
                          out_shape=out_shape, scratch_shapes=scratch)(*ins)


def comm_gather(wbig, wsm):
    def body(wb, ws, ob, os_, send_sems, recv_sems):
        x, y, c = _mesh_pos()
        s = 2 * x + y
        me, sibling = (x, y, c), (x, y, 1 - c)
        chips = _other_chips(x, y)
        rc = functools.partial(_remote, send_sems, recv_sems)
        first = []
        for j, (cx, cy) in enumerate(chips):
            first.append(rc(j, wb.at[c], ob.at[s, c], (cx, cy, c)))
            first.append(rc(6 + j, ws, os_.at[s], (cx, cy, c)))
        for cp in first:
            cp.start()
        passed = []
        for j, (cx, cy) in enumerate(chips):
            blk = ob.at[2 * cx + cy, c]
            rc(j, blk, blk, me).wait_recv()
            passed.append(rc(3 + j, blk, blk, sibling))
            passed[-1].start()
        for j, (cx, cy) in enumerate(chips):
            blk = ob.at[2 * cx + cy, 1 - c]
            rc(3 + j, blk, blk, me).wait_recv()
            sb = os_.at[2 * cx + cy]
            rc(6 + j, sb, sb, me).wait_recv()
        for cp in first + passed:
            cp.wait_send()

    out_shape = [jax.ShapeDtypeStruct((N_CHIPS, *wbig.shape), wbig.dtype),
                 jax.ShapeDtypeStruct((N_CHIPS, *wsm.shape), wsm.dtype)]
    return _comm_call(body, "comm_gather", [wbig, wsm], out_shape, 9)


def comm_pair_exchange(gb, gs):
    def body(gb_ref, gs_ref, rb_ref, rs_ref, send_sems, recv_sems):
        x, y, c = _mesh_pos()
        sibling = (x, y, 1 - c)
        rc = functools.partial(_remote, send_sems, recv_sems)
        cps = [rc(r, gb_ref.at[r, 1 - c], rb_ref.at[r], sibling) for r in range(N_CHIPS)]
        cps.append(rc(N_CHIPS, gs_ref.at[1 - c], rs_ref, sibling))
        for cp in cps:
            cp.start()
        for cp in cps:
            cp.wait()

    out_shape = [jax.ShapeDtypeStruct((N_CHIPS, *gb.shape[2:]), gb.dtype), jax.ShapeDtypeStruct(gs.shape[1:], gs.dtype)]
    return _comm_call(body, "comm_pair_exchange", [gb, gs], out_shape, N_CHIPS + 1)


def comm_chip_exchange(hb, hs):
    def body(hb_ref, hs_ref, qb_ref, qs_ref, send_sems, recv_sems):
        x, y, c = _mesh_pos()
        s = 2 * x + y
        me = (x, y, c)
        chips = _other_chips(x, y)
        rc = functools.partial(_remote, send_sems, recv_sems)
        cps = []
        for j, (cx, cy) in enumerate(chips):
            cps.append(rc(j, hb_ref.at[2 * cx + cy], qb_ref.at[s], (cx, cy, c)))
            cps.append(rc(3 + j, hs_ref, qs_ref.at[s], (cx, cy, c)))
        for cp in cps:
            cp.start()
        for j, (cx, cy) in enumerate(chips):
            blk = qb_ref.at[2 * cx + cy]
            rc(j, blk, blk, me).wait_recv()
            sb = qs_ref.at[2 * cx + cy]
            rc(3 + j, sb, sb, me).wait_recv()
        for cp in cps:
            cp.wait_send()

    out_shape = [jax.ShapeDtypeStruct(hb.shape, hb.dtype), jax.ShapeDtypeStruct((N_CHIPS, *hs.shape), hs.dtype)]
    return _comm_call(body, "comm_chip_exchange", [hb, hs], out_shape, 6)


def comm_pair_share(halves):
    n = len(halves)

    def body(*refs):
        x, y, c = _mesh_pos()
        send_sems, recv_sems = refs[2 * n], refs[2 * n + 1]
        cps = [_remote(send_sems, recv_sems, k, refs[k], refs[n + k], (x, y, 1 - c)) for k in range(n)]
        for cp in cps:
            cp.start()
        for cp in cps:
            cp.wait()

    out_shape = [jax.ShapeDtypeStruct(h.shape, h.dtype) for h in halves]
    return _comm_call(body, "comm_pair_share", list(halves), out_shape, n)


def add_pairs(a, b, out_dtype, *, name, tb):
    T, L = a.shape
    return _rowmap(lambda i, n, p, q: p + q, name=name, T=T, tb=tb, ins=[("row", a), ("row", b)],
                   outs=[("row", L, out_dtype)])[0]


def add_chips(parts, *, name, tb):
    T, L = parts[0].shape

    def fn(i, n, *ps):
        acc = ps[0].astype(F32)
        for p in ps[1:]:
            acc = acc + p.astype(F32)
        return acc

    return _rowmap(fn, name=name, T=T, tb=tb, ins=[("row", p) for p in parts], outs=[("row", L, F32)])[0]


def adamw(g, w, m, v, *, name, tb):
    T, L = g.shape

    def fn(i, n, g, w, m, v):
        m2 = ADAM_B1 * m + (1.0 - ADAM_B1) * g
        v2 = ADAM_B2 * v + (1.0 - ADAM_B2) * (g * g)
        m_hat = m2 / (1.0 - ADAM_B1 ** ADAM_STEP)
        v_hat = v2 / (1.0 - ADAM_B2 ** ADAM_STEP)
        return -ADAM_LR * (m_hat / (jnp.sqrt(v_hat) + ADAM_EPS) + ADAM_WD * w), m2, v2

    return _rowmap(fn, name=name, T=T, tb=tb, ins=[("row", z) for z in (g, w, m, v)], outs=[("row", L, F32)] * 3)


BIG_LANES = 1024
SMALL_LANES = 128


def _flat_cat(arrs, total, dtype):
    parts = [a.reshape(-1).astype(dtype) for a in arrs]
    n = sum(p.shape[0] for p in parts)
    assert n <= total, (n, total)
    if n < total:
        parts.append(jnp.zeros((total - n,), dtype))
    return jnp.concatenate(parts)


def _split_flat(flat, shapes):
    out, off = [], 0
    for shp in shapes:
        n = math.prod(shp)
        out.append(flat[off:off + n].reshape(shp))
        off += n
    return out


def _round_up(n, m):
    return -(-n // m) * m


def _full_shape(shard_shape, axis):
    return tuple(d * N_CHIPS if i == axis else d for i, d in enumerate(shard_shape))


def kernel(x, mem, attn_norm, a_w_in, a_mu, a_w0, a_w2, a_a0, a_a2, a_g2, a_k_k, a_k_a, a_r_k, a_lnx_w, a_lnx_b, a_w_out, kv_norm, kv_w, kv_k_norm, b_w_in, b_q_norm, b_w_out, mem_norm, mem_w_kv, mem_q_norm, mem_k_norm, ffn_norm, ffn_w_up, ffn_conv_w, ffn_conv_b, ffn_w_down, loss_target, m_attn_norm, m_a_w_in, m_a_mu, m_a_w0, m_a_w2, m_a_a0, m_a_a2, m_a_g2, m_a_k_k, m_a_k_a, m_a_r_k, m_a_lnx_w, m_a_lnx_b, m_a_w_out, m_kv_norm, m_kv_w, m_kv_k_norm, m_b_w_in, m_b_q_norm, m_b_w_out, m_mem_norm, m_mem_w_kv, m_mem_q_norm, m_mem_k_norm, m_ffn_norm, m_ffn_w_up, m_ffn_conv_w, m_ffn_conv_b, m_ffn_w_down, v_attn_norm, v_a_w_in, v_a_mu, v_a_w0, v_a_w2, v_a_a0, v_a_a2, v_a_g2, v_a_k_k, v_a_k_a, v_a_r_k, v_a_lnx_w, v_a_lnx_b, v_a_w_out, v_kv_norm, v_kv_w, v_kv_k_norm, v_b_w_in, v_b_q_norm, v_b_w_out, v_mem_norm, v_mem_w_kv, v_mem_q_norm, v_mem_k_norm, v_ffn_norm, v_ffn_w_up, v_ffn_conv_w, v_ffn_conv_b, v_ffn_w_down):
    args = (attn_norm, a_w_in, a_mu, a_w0, a_w2, a_a0, a_a2, a_g2, a_k_k, a_k_a, a_r_k, a_lnx_w, a_lnx_b, a_w_out, kv_norm, kv_w, kv_k_norm, b_w_in, b_q_norm, b_w_out, mem_norm, mem_w_kv, mem_q_norm, mem_k_norm, ffn_norm, ffn_w_up, ffn_conv_w, ffn_conv_b, ffn_w_down)
    ms = (m_attn_norm, m_a_w_in, m_a_mu, m_a_w0, m_a_w2, m_a_a0, m_a_a2, m_a_g2, m_a_k_k, m_a_k_a, m_a_r_k, m_a_lnx_w, m_a_lnx_b, m_a_w_out, m_kv_norm, m_kv_w, m_kv_k_norm, m_b_w_in, m_b_q_norm, m_b_w_out, m_mem_norm, m_mem_w_kv, m_mem_q_norm, m_mem_k_norm, m_ffn_norm, m_ffn_w_up, m_ffn_conv_w, m_ffn_conv_b, m_ffn_w_down)
    vs = (v_attn_norm, v_a_w_in, v_a_mu, v_a_w0, v_a_w2, v_a_a0, v_a_a2, v_a_g2, v_a_k_k, v_a_k_a, v_a_r_k, v_a_lnx_w, v_a_lnx_b, v_a_w_out, v_kv_norm, v_kv_w, v_kv_k_norm, v_b_w_in, v_b_q_norm, v_b_w_out, v_mem_norm, v_mem_w_kv, v_mem_q_norm, v_mem_k_norm, v_ffn_norm, v_ffn_w_up, v_ffn_conv_w, v_ffn_conv_b, v_ffn_w_down)
    w_sh, m_sh, v_sh = (dict(zip(WEIGHTS, z)) for z in (args, ms, vs))
    xi, yi, ci = _mesh_pos()
    chip = 2 * xi + yi
    axes = {**dict(BIG), **dict(SMALL_SHARDED)}
    early_names = [n for n, _ in BIG if n in EARLY_BIG]
    late_names = [n for n, _ in BIG if n not in EARLY_BIG and n != NATURAL_BIG]
    ss_names, ss_axes = [n for n, _ in SMALL_SHARDED], dict(SMALL_SHARDED)
    shapes_of = lambda names: [w_sh[n].shape for n in names]
    count = lambda names: sum(math.prod(s) for s in shapes_of(names))
    n_early, n_late = count(early_names), count(late_names)
    assert n_early % (2 * 16 * BIG_LANES) == 0 and n_late % (2 * 16 * BIG_LANES) == 0
    mh, mh_late = n_early // (2 * BIG_LANES), n_late // (2 * BIG_LANES)
    n_ss = _round_up(count(ss_names), 8 * SMALL_LANES)

    def shard_pack(names, total, dtype, source):
        return _flat_cat([source[n] for n in names], total, dtype)

    def unshard(names, gathered):
        per_chip = [_split_flat(gathered[j], shapes_of(names)) for j in range(N_CHIPS)]
        return {n: jnp.concatenate([per_chip[j][k] for j in range(N_CHIPS)], axis=axes[n]) for k, n in enumerate(names)}

    def by_chip(names, total, dtype, grads):
        parts = [jnp.split(grads[n], N_CHIPS, axis=axes[n]) for n in names]
        return jnp.stack([_flat_cat([p[j] for p in parts], total, dtype) for j in range(N_CHIPS)])

    wbig = shard_pack(early_names, n_early, MMD, w_sh).reshape(2, mh, BIG_LANES)
    wsm = shard_pack(ss_names, n_ss, F32, w_sh).reshape(-1, SMALL_LANES)
    wbig_all, wsm_all = comm_gather(wbig, wsm)
    wbig_all = lax.dynamic_update_index_in_dim(wbig_all, wbig, chip, 0).reshape(N_CHIPS, -1)
    wsm_all = lax.dynamic_update_index_in_dim(wsm_all, wsm, chip, 0).reshape(N_CHIPS, -1)
    W = {n: w_sh[n] for n in SMALL_REPL}
    W.update(unshard(early_names, wbig_all))
    W.update(unshard(ss_names, wsm_all))
    for n in ("a_w2", "a_a2", "a_g2"):
        W[n] = W[n].astype(MMD)
    wlate = shard_pack(late_names, n_late, MMD, w_sh).reshape(2, mh_late, BIG_LANES)
    nat_axis = axes[NATURAL_BIG]
    wnat = w_sh[NATURAL_BIG].astype(MMD)
    assert wnat.shape[0] == 2 and nat_axis != 0

    def unpack_late(gathered):
        full = lax.dynamic_update_index_in_dim(gathered[0], wlate, chip, 0)
        out = unshard(late_names, full.reshape(N_CHIPS, -1))
        nat = lax.dynamic_update_index_in_dim(gathered[1], wnat, chip, 0)
        out[NATURAL_BIG] = jnp.concatenate([nat[j] for j in range(N_CHIPS)], axis=nat_axis)
        return out

    def pack_late(grads):
        return [by_chip(late_names, n_late, BF16, grads).reshape(N_CHIPS, 2, mh_late, BIG_LANES),
                jnp.stack(jnp.split(grads[NATURAL_BIG].astype(BF16), N_CHIPS, axis=nat_axis))]

    loss_blk, grad_x, G, (received, pieces) = local_step(x[0], mem[0], loss_target[0], W,
                                                          late=([wlate, wnat], unpack_late, pack_late))
    loss = lax.psum(loss_blk[0, 0], ("x", "y", "c"))

    own_piece = lambda p: lax.dynamic_index_in_dim(lax.dynamic_index_in_dim(p, chip, 0, keepdims=False), ci, 0,
                                                   keepdims=False)
    gh_late = add_chips([received[0][k] for k in range(len(PEER_FLIPS))] + [own_piece(pieces[0])],
                        name="add_pieces_late", tb=32)
    gh_nat = add_chips([received[1][k] for k in range(len(PEER_FLIPS))] + [own_piece(pieces[1])],
                       name="add_pieces_natural", tb=32)
    gbig = by_chip(early_names, n_early, F32, G).reshape(N_CHIPS, 2, mh, BIG_LANES)
    sm_full_names = ss_names + list(SMALL_REPL)
    sm_full_shapes = [_full_shape(w_sh[n].shape, ss_axes[n]) for n in ss_names] + [w_sh[n].shape for n in SMALL_REPL]
    n_smf = _round_up(sum(math.prod(s) for s in sm_full_shapes), 2 * 8 * SMALL_LANES)
    msh = n_smf // (2 * SMALL_LANES)
    gsm = _flat_cat([G[n] for n in sm_full_names], n_smf, F32).reshape(2, msh, SMALL_LANES)
    rb, rs = comm_pair_exchange(gbig, gsm)
    mine_b = lax.dynamic_index_in_dim(gbig, ci, axis=1, keepdims=False)
    mine_s = lax.dynamic_index_in_dim(gsm, ci, axis=0, keepdims=False)
    hb = add_pairs(mine_b.reshape(-1, BIG_LANES), rb.reshape(-1, BIG_LANES), BF16, name="add_pairs_big", tb=128)
    hs = add_pairs(mine_s, rs, F32, name="add_pairs_small", tb=msh)
    hb = hb.reshape(N_CHIPS, mh, BIG_LANES)
    qb, qs = comm_chip_exchange(hb, hs)
    qb = lax.dynamic_update_index_in_dim(qb, lax.dynamic_index_in_dim(hb, chip, 0, keepdims=False), chip, 0)
    qs = lax.dynamic_update_index_in_dim(qs, hs, chip, 0)
    gh = add_chips([qb[j] for j in range(N_CHIPS)], name="add_chips_big", tb=32)
    gsh = add_chips([qs[j] for j in range(N_CHIPS)], name="add_chips_small", tb=msh)
    rh, rh_late, rh_nat, rsh = comm_pair_share([gh, gh_late, gh_nat, gsh])
    both = lambda mine_, theirs: jnp.where(ci == 0, jnp.stack([mine_, theirs]), jnp.stack([theirs, mine_]))
    gfull, gfull_late, gfull_nat, gsfull = both(gh, rh), both(gh_late, rh_late), both(gh_nat, rh_nat), both(gsh, rsh)

    res = {tag: {} for tag in ("grad", "delta", "new_m", "new_v")}
    big_grads = (list(zip(early_names, _split_flat(gfull.reshape(-1), shapes_of(early_names))))
                 + list(zip(late_names, _split_flat(gfull_late.reshape(-1), shapes_of(late_names))))
                 + [(NATURAL_BIG, gfull_nat)])
    for n, g in big_grads:
        shp = w_sh[n].shape
        rows = lambda z: z.reshape(-1, shp[-1])
        nrow = math.prod(shp[:-1])
        tb = next(t for t in (512, 256, 128, 64) if nrow % t == 0 and t * shp[-1] <= (1 << 19))
        outs = adamw(rows(g), rows(w_sh[n]), rows(m_sh[n]), rows(v_sh[n]), name=f"adamw_{n}", tb=tb)
        res["grad"][n] = g
        for tag, o in zip(("delta", "new_m", "new_v"), outs):
            res[tag][n] = o.reshape(shp)
    sm_full = dict(zip(sm_full_names, _split_flat(gsfull.reshape(-1), sm_full_shapes)))
    g_loc = {}
    for n in ss_names:
        size = w_sh[n].shape[ss_axes[n]]
        g_loc[n] = lax.dynamic_slice_in_dim(sm_full[n], chip * size, size, axis=ss_axes[n])
    for n in SMALL_REPL:
        g_loc[n] = sm_full[n]
    n_sml = _round_up(sum(math.prod(w_sh[n].shape) for n in sm_full_names), 8 * SMALL_LANES)
    pack_sm = lambda d: _flat_cat([d[n] for n in sm_full_names], n_sml, F32).reshape(-1, SMALL_LANES)
    d_sm, m_sm, v_sm = adamw(pack_sm(g_loc), pack_sm(w_sh), pack_sm(m_sh), pack_sm(v_sh), name="adamw_small",
                             tb=n_sml // SMALL_LANES)
    sm_loc_shapes = [w_sh[n].shape for n in sm_full_names]
    res["grad"].update(g_loc)
    for tag, smv in (("delta", d_sm), ("new_m", m_sm), ("new_v", v_sm)):
        res[tag].update(dict(zip(sm_full_names, _split_flat(smv.reshape(-1), sm_loc_shapes))))
    return (loss, grad_x[None], *[res[tag][n] for tag in ("grad", "delta", "new_m", "new_v") for n in WEIGHTS])
```

```python
import functools
import math

import numpy as np
import jax
import jax.numpy as jnp
from jax import lax
from jax.experimental import pallas as pl
from jax.experimental.pallas import tpu as pltpu

F32 = jnp.float32
BF16 = jnp.bfloat16
MMD = jnp.bfloat16

D_MODEL = 1024
HEAD_DIM = 64
N_MEM = 256
MEM_WIDTH = 256
RWKV_HEADS = 12
RWKV_WIDTH = 768
SHIFT_WIDTH = 2560
LORA_WIDTH = 256
DIL_WIDTH = 768
DIL_GROUPS = ((128, 1), (512, 4), (2048, 16))
DIL_BLOCK = 128
D_FF = 2816
ROPE_THETA = 10000.0
RMS_EPS = 1e-6
LNX_EPS = 64e-5
NEG_INF = -1e30
ADAM_LR = 0.001
ADAM_B1 = 0.9
ADAM_B2 = 0.999
ADAM_EPS = 1e-08
ADAM_WD = 0.01
ADAM_STEP = 10
N_CHIPS = 4
MESH = pl.DeviceIdType.MESH
VMEM_LIMIT_MB = 56
SCAN_CHUNK = 64
SCAN_UNROLL = 16
SCAN_UNROLL_BWD = 8

BIG = (("a_w_in", 2), ("a_w_out", 1), ("kv_w", 1), ("b_w_in", 1), ("b_w_out", 2), ("mem_w_kv", 1),
       ("ffn_w_up", 2), ("ffn_w_down", 1))
EARLY_BIG = ("a_w_in",)
NATURAL_BIG = "ffn_w_up"
SMALL_SHARDED = (("a_mu", 1), ("a_w0", 1), ("a_w2", 2), ("a_a0", 1), ("a_a2", 2), ("a_g2", 2), ("a_k_k", 1),
                 ("a_k_a", 1), ("a_lnx_w", 1), ("a_lnx_b", 1), ("ffn_conv_w", 2))
SMALL_REPL = ("attn_norm", "a_r_k", "kv_norm", "kv_k_norm", "b_q_norm", "mem_norm", "mem_q_norm", "mem_k_norm",
              "ffn_norm", "ffn_conv_b")
WEIGHTS = ("attn_norm", "a_w_in", "a_mu", "a_w0", "a_w2", "a_a0", "a_a2", "a_g2", "a_k_k", "a_k_a", "a_r_k",
           "a_lnx_w", "a_lnx_b", "a_w_out", "kv_norm", "kv_w", "kv_k_norm", "b_w_in", "b_q_norm", "b_w_out",
           "mem_norm", "mem_w_kv", "mem_q_norm", "mem_k_norm", "ffn_norm", "ffn_w_up", "ffn_conv_w", "ffn_conv_b",
           "ffn_w_down")


def _cp(sem=None, **kw):
    return pltpu.CompilerParams(dimension_semantics=sem, vmem_limit_bytes=VMEM_LIMIT_MB << 20, **kw)


def _tile(n, cands=(512, 256, 128)):
    for c in cands:
        if n % c == 0:
            return c
    return n


def _mm(a, b, *, name, ta=False, tb=False, add=None, out_dtype=F32):
    K, M = a.shape if ta else a.shape[::-1]
    N = b.shape[0] if tb else b.shape[1]
    assert K == (b.shape[1] if tb else b.shape[0])
    tm, tn = _tile(M, (512, 256, 128) if ta else (1024, 512, 256, 128)), _tile(N, (512, 1408, 256, 128))
    bytes_of = lambda z: z.size * z.dtype.itemsize
    kept = 1 if bytes_of(b) + bytes_of(a) * (N // tn) < bytes_of(a) + bytes_of(b) * (M // tm) else 0
    mi, ni = ((lambda o, i: i), (lambda o, i: o)) if kept else ((lambda o, i: o), (lambda o, i: i))
    grid = (N // tn, M // tm) if kept else (M // tm, N // tn)
    a_blk, b_blk = ((K, tm) if ta else (tm, K)), ((tn, K) if tb else (K, tn))
    a_spec = pl.BlockSpec(a_blk, (lambda o, i: (0, mi(o, i))) if ta else (lambda o, i: (mi(o, i), 0)))
    b_spec = pl.BlockSpec(b_blk, (lambda o, i: (ni(o, i), 0)) if tb else (lambda o, i: (0, ni(o, i))))
    o_spec = pl.BlockSpec((tm, tn), lambda o, i: (mi(o, i), ni(o, i)))
    dn = (((0,) if ta else (1,), (1,) if tb else (0,)), ((), ()))
    has_add = add is not None
    cache = (b if kept else a).dtype != MMD

    def body(*refs):
        vals = [refs[0], refs[1]]
        o_ref = refs[2 + has_add]
        if cache:
            scr = refs[-1]

            @pl.when(pl.program_id(1) == 0)
            def _():
                scr[...] = vals[kept][...].astype(MMD)

            vals[kept] = scr
        acc = lax.dot_general(vals[0][...].astype(MMD), vals[1][...].astype(MMD), dn, preferred_element_type=F32)
        if has_add:
            acc = acc + refs[2][...]
        o_ref[...] = acc.astype(o_ref.dtype)

    ins = [a, b] + ([add] if has_add else [])
    specs = [a_spec, b_spec] + ([o_spec] if has_add else [])
    return pl.pallas_call(
        body, name=name, grid=grid, in_specs=specs, out_specs=o_spec,
        out_shape=jax.ShapeDtypeStruct((M, N), out_dtype),
        scratch_shapes=[pltpu.VMEM(b_blk if kept else a_blk, MMD)] if cache else [],
        compiler_params=_cp(("parallel", "arbitrary")),
    )(*ins)


def _rowmap(fn, *, name, T, tb, ins, outs, accs=()):
    nblk = T // tb
    assert T % tb == 0 and tb % 8 == 0
    in_specs, args = [], []
    for spec in ins:
        kind, arr = spec[0], spec[1]
        w, cb = (spec[2], spec[3]) if len(spec) > 2 else (arr.shape[-1], 0)
        if kind == "row":
            in_specs.append(pl.BlockSpec((tb, w), lambda i, cb=cb: (i, cb)))
        elif kind == "prev":
            in_specs.append(pl.BlockSpec((8, w), lambda i, cb=cb: (jnp.maximum(i * (tb // 8) - 1, 0), cb)))
        elif kind == "next":
            in_specs.append(pl.BlockSpec((8, w), lambda i, cb=cb: (jnp.minimum((i + 1) * (tb // 8), T // 8 - 1), cb)))
        elif kind == "const":
            in_specs.append(pl.BlockSpec(arr.shape, lambda i, nd=arr.ndim: (0,) * nd))
        else:
            raise ValueError(kind)
        args.append(arr)
    out_shape, out_specs = [], []
    for kind, w, dt in outs:
        out_shape.append(jax.ShapeDtypeStruct((T, w), dt))
        out_specs.append(pl.BlockSpec((tb, w), lambda i: (i, 0)))
    for shp, dt in accs:
        out_shape.append(jax.ShapeDtypeStruct(shp, dt))
        out_specs.append(pl.BlockSpec(shp, lambda i, nd=len(shp): (0,) * nd))
    n_in, n_out = len(ins), len(outs)

    def body(*refs):
        i = pl.program_id(0)
        vals = [r[...] for r in refs[:n_in]]
        res = fn(i, nblk, *vals)
        if not isinstance(res, (tuple, list)):
            res = (res,)
        assert len(res) == n_out + len(accs), (name, len(res))
        for r, v in zip(refs[n_in:n_in + n_out], res[:n_out]):
            r[...] = v.astype(r.dtype)
        acc_refs = refs[n_in + n_out:]
        if acc_refs:
            @pl.when(i == 0)
            def _():
                for r in acc_refs:
                    r[...] = jnp.zeros(r.shape, r.dtype)

            for r, v in zip(acc_refs, res[n_out:]):
                r[...] += v

    res = pl.pallas_call(
        body, name=name, grid=(nblk,), in_specs=in_specs, out_specs=out_specs, out_shape=out_shape,
        compiler_params=_cp(("arbitrary",)),
    )(*args)
    return res


def _row_pick(halo, r):
    rid = lax.broadcasted_iota(jnp.int32, halo.shape, 0)
    return jnp.sum(jnp.where(rid == r, halo, 0.0), axis=0, keepdims=True)


def _shift_down(x, row_before, is_first):
    rid = lax.broadcasted_iota(jnp.int32, x.shape, 0)
    first = jnp.where(is_first, 0.0, 1.0) * row_before
    return jnp.where(rid == 0, first, pltpu.roll(x, 1, axis=0))


def _shift_up(x, row_after, is_last):
    n = x.shape[0]
    rid = lax.broadcasted_iota(jnp.int32, x.shape, 0)
    last = jnp.where(is_last, 0.0, 1.0) * row_after
    return jnp.where(rid == n - 1, last, pltpu.roll(x, n - 1, axis=0))


def _dot(a, b, dn=(((1,), (0,)), ((), ()))):
    return lax.dot_general(a.astype(MMD), b.astype(MMD), dn, preferred_element_type=F32)


def _dot_nt(a, b):
    return _dot(a, b, (((1,), (1,)), ((), ())))


def _dot_tn(a, b):
    return _dot(a, b, (((0,), (0,)), ((), ())))


def _dot_exact01(x, g01):
    hi = x.astype(BF16)
    lo = (x - hi.astype(F32)).astype(BF16)
    dn = (((1,), (0,)), ((), ()))
    return (lax.dot_general(hi, g01, dn, preferred_element_type=F32)
            + lax.dot_general(lo, g01, dn, preferred_element_type=F32))


def _fold_heads(v, fold):
    return _row_pick(_dot_exact01(jnp.broadcast_to(v, (8, v.shape[1])), fold), 0)


def _group_ones(width):
    idx = np.arange(width) // HEAD_DIM
    return jnp.asarray((idx[:, None] == idx[None, :]).astype(np.float32), BF16)


def _fold_ones(width):
    idx = np.arange(width) % HEAD_DIM
    return jnp.asarray((idx[:, None] == np.arange(HEAD_DIM)[None, :]).astype(np.float32), BF16)


def _head_masks(width):
    idx = np.arange(width) // HEAD_DIM
    return jnp.asarray((idx[None, :] == np.arange(width // HEAD_DIM)[:, None]).astype(np.float32)[:, None, :], F32)


def _rms_stats(x):
    r = lax.rsqrt(jnp.mean(x * x, axis=-1, keepdims=True) + RMS_EPS)
    return r, x * r


def rms_fwd(x, gains, *, name):
    T, D = x.shape

    def fn(i, nblk, xb, *gs):
        _, xh = _rms_stats(xb)
        return tuple(xh * g for g in gs)

    return _rowmap(fn, name=name, T=T, tb=512, ins=[("row", x)] + [("const", g) for g in gains],
                   outs=[("row", D, MMD)] * len(gains))


def rms_bwd(x, gains, dhs, dres, *, name):
    T, D = x.shape
    n = len(gains)

    def fn(i, nblk, xb, dr, *rest):
        gs, ds = rest[:n], rest[n:]
        r, xh = _rms_stats(xb)
        dx = dr
        dgs = []
        for g, dh in zip(gs, ds):
            dgs.append(jnp.sum(dh * xh, axis=0, keepdims=True))
            dxh = dh * g
            dx = dx + r * (dxh - xh * jnp.mean(dxh * xh, axis=-1, keepdims=True))
        return (dx, *dgs)

    return _rowmap(fn, name=name, T=T, tb=512,
                   ins=[("row", x), ("row", dres)] + [("const", g) for g in gains] + [("row", d) for d in dhs],
                   outs=[("row", D, F32)], accs=[((1, D), F32)] * n)


def _segsum(x):
    first = lax.broadcasted_iota(jnp.int32, (x.shape[0], 128), 1) < HEAD_DIM
    outs = []
    for p in range(x.shape[1] // 128):
        xs = x[:, p * 128:(p + 1) * 128]
        lo = jnp.sum(jnp.where(first, xs, 0.0), axis=-1, keepdims=True)
        hi = jnp.sum(jnp.where(first, 0.0, xs), axis=-1, keepdims=True)
        outs.append(jnp.where(first, lo, hi))
    return jnp.concatenate(outs, axis=1)


def _pre1_common(i, ps, halo, mu, w0, a0, w2p, a2p, g2p, k_k, k_a):
    prev = _shift_down(ps, _row_pick(halo, 7), i == 0)
    xs = ps + (prev - ps) * mu
    lo = xs[:, 3 * RWKV_WIDTH:]
    tl, sl = jnp.tanh(lo), jax.nn.sigmoid(lo)
    dec = w0 + _dot(tl, w2p)
    ain = a0 + _dot(lo, a2p)
    g = _dot(sl, g2p)
    wl = -jax.nn.softplus(-dec) - 0.5
    w = jnp.exp(-jnp.exp(wl))
    a = jax.nn.sigmoid(ain)
    k = xs[:, RWKV_WIDTH:2 * RWKV_WIDTH]
    z = k * k_k
    nrm = jnp.sqrt(_segsum(z * z))
    kk = z / jnp.maximum(nrm, 1e-12)
    return prev, xs, lo, tl, sl, dec, wl, w, a, g, k, nrm, kk


def rwkv_pre_fwd(p, mu, w0, a0, w2p, a2p, g2p, k_k, k_a):
    T = p.shape[0]

    def fn(i, nblk, ps, halo, mu, w0, a0, w2p, a2p, g2p, k_k, k_a):
        _, xs, _, _, _, _, _, w, a, g, k, _, kk = _pre1_common(i, ps, halo, mu, w0, a0, w2p, a2p, g2p, k_k, k_a)
        W = RWKV_WIDTH
        return xs[:, :W], w, k * (1.0 + (a - 1.0) * k_a), xs[:, 2 * W:3 * W], kk, kk * a, g

    return _rowmap(fn, name="rwkv_pre_fwd", T=T, tb=256,
                   ins=[("row", p, SHIFT_WIDTH, 0), ("prev", p, SHIFT_WIDTH, 0)]
                   + [("const", c) for c in (mu, w0, a0, w2p, a2p, g2p, k_k, k_a)],
                   outs=[("row", RWKV_WIDTH, F32)] * 7)


def rwkv_pre_bwd(p, mu, w0, a0, w2p, a2p, g2p, k_k, k_a, drs, dw, dk2s, dvs, dkk, dkka, dg):
    T = p.shape[0]

    def fn(i, nblk, ps, halo, mu, w0, a0, w2p, a2p, g2p, k_k, k_a, dr0, dr1, dw, dk20, dk21, dv0, dv1, dkk, dkka, dg):
        prev, xs, lo, tl, sl, dec, wl, w, a, g, k, nrm, kk = _pre1_common(i, ps, halo, mu, w0, a0, w2p, a2p, g2p, k_k, k_a)
        dk2 = dk20 + dk21
        dkk_t = dkk + dkka * a
        proj = jnp.where(nrm > 1e-12, kk * _segsum(dkk_t * kk), 0.0)
        dz = (dkk_t - proj) / jnp.maximum(nrm, 1e-12)
        dk = dz * k_k + dk2 * (1.0 + (a - 1.0) * k_a)
        da = dkka * kk + dk2 * k * k_a
        ddec = dw * (-w * jnp.exp(wl)) * jax.nn.sigmoid(-dec)
        dain = da * a * (1.0 - a)
        dlo = (_dot_nt(ddec, w2p) * (1.0 - tl * tl) + _dot_nt(dain, a2p) + _dot_nt(dg, g2p) * sl * (1.0 - sl))
        dxs = jnp.concatenate([dr0 + dr1, dk, dv0 + dv1, dlo], axis=1)
        s = lambda z: jnp.sum(z, axis=0, keepdims=True)
        return (dxs, s(dxs * (prev - ps)), s(ddec), s(dain), _dot_tn(tl, ddec), _dot_tn(lo, dain), _dot_tn(sl, dg),
                s(dz * k), s(dk2 * k * (a - 1.0)))

    return _rowmap(fn, name="rwkv_pre_bwd", T=T, tb=128,
                   ins=[("row", p, SHIFT_WIDTH, 0), ("prev", p, SHIFT_WIDTH, 0)]
                   + [("const", c) for c in (mu, w0, a0, w2p, a2p, g2p, k_k, k_a)]
                   + [("row", c) for c in (*drs, dw, *dk2s, *dvs, dkk, dkka, dg)],
                   outs=[("row", SHIFT_WIDTH, F32)],
                   accs=[((1, SHIFT_WIDTH), F32), ((1, RWKV_WIDTH), F32), ((1, RWKV_WIDTH), F32)]
                   + [((LORA_WIDTH, RWKV_WIDTH), F32)] * 3 + [((1, RWKV_WIDTH), F32)] * 2)


def shift_bwd(dxs, mu, dq_mem):
    T = dxs.shape[0]

    def fn(i, nblk, d, halo, mu, dq):
        nxt = _shift_up(d, _row_pick(halo, 0), i == nblk - 1)
        return jnp.concatenate([d * (1.0 - mu) + nxt * mu, dq], axis=1)

    return _rowmap(fn, name="shift_bwd", T=T, tb=256,
                   ins=[("row", dxs), ("next", dxs), ("const", mu), ("row", dq_mem)],
                   outs=[("row", SHIFT_WIDTH + MEM_WIDTH, MMD)])[0]


N_PAIRS = RWKV_HEADS // 2


def _pair_consts():
    row = lax.broadcasted_iota(jnp.int32, (HEAD_DIM, 128), 0)
    lane = lax.broadcasted_iota(jnp.int32, (HEAD_DIM, 128), 1)
    eye2 = jnp.logical_or(lane == row, lane == row + HEAD_DIM).astype(F32)
    li = lax.broadcasted_iota(jnp.int32, (128, 128), 0) < HEAD_DIM
    lj = lax.broadcasted_iota(jnp.int32, (128, 128), 1) < HEAD_DIM
    return eye2, (li == lj).astype(BF16)


def _pair_sum(p, ones2):
    n, m, l = p.shape
    s = lax.dot_general(p.reshape(n * m, l).astype(BF16), ones2, (((1,), (0,)), ((), ())), preferred_element_type=F32)
    return s.reshape(n, m, l)


def _pair_rows(row):
    return jnp.stack([row[:, p * 128:(p + 1) * 128] for p in range(N_PAIRS)], axis=0)


def _pair_flat(rows):
    return jnp.concatenate([rows[p] for p in range(N_PAIRS)], axis=1)


def _split_bf16(v):
    hi = v.astype(BF16).astype(F32)
    return hi, v - hi


def _gather_copies(srcs, dsts, send_sems, recv_sems):
    x, y, c = _mesh_pos()
    s = 2 * x + y
    me, sibling = (x, y, c), (x, y, 1 - c)
    rc = functools.partial(_remote, send_sems, recv_sems)
    ici, land, fwd, arrived = [], [], [], []
    for b, (src, dst) in enumerate(zip(srcs, dsts)):
        for j, (cx, cy) in enumerate(_other_chips(x, y)):
            k = 6 * b + j
            ici.append(rc(k, src.at[c], dst.at[s, c], (cx, cy, c)))
            blk, blk2 = dst.at[2 * cx + cy, c], dst.at[2 * cx + cy, 1 - c]
            land.append(rc(k, blk, blk, me))
            fwd.append(rc(k + 3, blk, blk, sibling))
            arrived.append(rc(k + 3, blk2, blk2, me))
    return ici, land, fwd, arrived


def scan_fwd(r, w, k2, v, kk, kka, gather=None):
    T, W = r.shape
    tc = SCAN_CHUNK
    nchunk = T // tc
    seq = pl.BlockSpec((tc, W), lambda i: (i, 0))
    one_state = pl.BlockSpec((N_PAIRS, HEAD_DIM, 128), lambda i: (0, 0, 0))
    nb = 0 if gather is None else len(gather)

    def body(r_ref, w_ref, k2_ref, v_ref, kk_ref, kka_ref, *rest):
        if gather is None:
            y_ref, st_ref, fin_ref, s_scr, vhi_scr, vlo_scr = rest
        else:
            srcs, (y_ref, st_ref, fin_ref), dsts = rest[:nb], rest[nb:nb + 3], rest[nb + 3:2 * nb + 3]
            s_scr, vhi_scr, vlo_scr, send_sems, recv_sems = rest[2 * nb + 3:]
            ici, land, fwd, arrived = _gather_copies(srcs, dsts, send_sems, recv_sems)

            @pl.when(pl.program_id(0) == 0)
            def _():
                for cp in ici:
                    cp.start()

            @pl.when(pl.program_id(0) == nchunk // 2)
            def _():
                for a, f in zip(land, fwd):
                    a.wait_recv()
                    f.start()

        @pl.when(pl.program_id(0) == 0)
        def _():
            s_scr[...] = jnp.zeros(s_scr.shape, F32)

        vhi_scr[...], vlo_scr[...] = _split_bf16(v_ref[...])
        eye2, ones2 = _pair_consts()
        eye2b = eye2.astype(BF16)

        def step(t, carry):
            r_t, w_t, k2_t, kk_t, kka_t, vhi_t, vlo_t = (
                _pair_rows(ref[pl.ds(t, 1), :]) for ref in (r_ref, w_ref, k2_ref, kk_ref, kka_ref, vhi_scr, vlo_scr))
            S = s_scr[...]
            sa = -_pair_sum(S * kk_t, ones2)
            vb = _pair_sum(eye2b * vhi_t.astype(BF16), ones2) + _pair_sum(eye2b * vlo_t.astype(BF16), ones2)
            S2 = S * w_t + sa * kka_t + vb * k2_t
            y_ref[pl.ds(t, 1), :] = _pair_flat(jnp.sum(eye2 * _pair_sum(S2 * r_t, ones2), axis=1, keepdims=True))
            s_scr[...] = S2
            st_ref[t] = S
            return carry

        lax.fori_loop(0, tc, step, 0, unroll=SCAN_UNROLL)
        fin_ref[...] = s_scr[...]

        if gather is not None:
            @pl.when(pl.program_id(0) == nchunk - 1)
            def _():
                for a in arrived:
                    a.wait_recv()
                for cp in ici + fwd:
                    cp.wait_send()

    in_specs = [seq] * 6
    out_specs = [seq, pl.BlockSpec((tc, N_PAIRS, HEAD_DIM, 128), lambda i: (i, 0, 0, 0)), one_state]
    out_shape = [jax.ShapeDtypeStruct((T, W), F32), jax.ShapeDtypeStruct((T, N_PAIRS, HEAD_DIM, 128), F32),
                 jax.ShapeDtypeStruct((N_PAIRS, HEAD_DIM, 128), F32)]
    scratch = [pltpu.VMEM((N_PAIRS, HEAD_DIM, 128), F32), pltpu.VMEM((tc, W), F32), pltpu.VMEM((tc, W), F32)]
    args = [r, w, k2, v, kk, kka]
    if gather is not None:
        in_specs += [HBM_SPEC] * nb
        out_specs += [HBM_SPEC] * nb
        out_shape += [jax.ShapeDtypeStruct((N_CHIPS, *g.shape), g.dtype) for g in gather]
        scratch += [pltpu.SemaphoreType.DMA((6 * nb,)), pltpu.SemaphoreType.DMA((6 * nb,))]
        args += list(gather)
    return pl.pallas_call(
        body, name="rwkv_scan_fwd", grid=(nchunk,), in_specs=in_specs, out_specs=out_specs, out_shape=out_shape,
        scratch_shapes=scratch, compiler_params=_cp(("arbitrary",)),
    )(*args)


PEER_FLIPS = tuple((fx, fy, fc) for fx in (0, 1) for fy in (0, 1) for fc in (0, 1))[1:]


def scan_bwd(r, w, k2, v, kk, kka, states, final_state, dy, scatter=None):
    T, W = r.shape
    tc = SCAN_CHUNK
    nchunk = T // tc
    seq = pl.BlockSpec((tc, W), lambda i: (nchunk - 1 - i, 0))
    st_spec = pl.BlockSpec((tc, N_PAIRS, HEAD_DIM, 128), lambda i: (nchunk - 1 - i, 0, 0, 0))
    one_state = pl.BlockSpec((N_PAIRS, HEAD_DIM, 128), lambda i: (0, 0, 0))
    nb, npeer = (0 if scatter is None else len(scatter)), len(PEER_FLIPS)

    def body(r_ref, w_ref, k2_ref, v_ref, kk_ref, kka_ref, st_ref, fin_ref, dy_ref, *rest):
        if scatter is None:
            dr_ref, dw_ref, dk2_ref, dv_ref, dkk_ref, dkka_ref, ds_scr, sc_scr, vhi_scr, vlo_scr = rest
        else:
            srcs, dsts = rest[:nb], rest[nb + 6:2 * nb + 6]
            dr_ref, dw_ref, dk2_ref, dv_ref, dkk_ref, dkka_ref = rest[nb:nb + 6]
            ds_scr, sc_scr, vhi_scr, vlo_scr, send_sems, recv_sems = rest[2 * nb + 6:]
            x, y, c = _mesh_pos()
            copies = []
            for b, (src, dst) in enumerate(zip(srcs, dsts)):
                for k, (fx, fy, fc) in enumerate(PEER_FLIPS):
                    px, py, pc = (1 - x if fx else x), (1 - y if fy else y), (1 - c if fc else c)
                    copies.append(_remote(send_sems, recv_sems, npeer * b + k, src.at[2 * px + py, pc], dst.at[k],
                                          (px, py, pc)))

            @pl.when(pl.program_id(0) == 0)
            def _():
                for cp in copies:
                    cp.start()

        @pl.when(pl.program_id(0) == 0)
        def _():
            ds_scr[...] = jnp.zeros(ds_scr.shape, F32)
            sc_scr[...] = fin_ref[...]

        vhi_scr[...], vlo_scr[...] = _split_bf16(v_ref[...])
        eye2, ones2 = _pair_consts()
        eye2b = eye2.astype(BF16)
        colsum = lambda z: jnp.sum(z, axis=1, keepdims=True)

        def step(j, carry):
            t = tc - 1 - j
            r_t, w_t, k2_t, kk_t, kka_t, vhi_t, vlo_t, dy_t = (
                _pair_rows(ref[pl.ds(t, 1), :])
                for ref in (r_ref, w_ref, k2_ref, kk_ref, kka_ref, vhi_scr, vlo_scr, dy_ref))
            s_prev, s_cur = st_ref[t], sc_scr[...]
            dyb = _pair_sum(eye2b * dy_t.astype(BF16), ones2)
            vb = _pair_sum(eye2b * vhi_t.astype(BF16), ones2) + _pair_sum(eye2b * vlo_t.astype(BF16), ones2)
            sa = -_pair_sum(s_prev * kk_t, ones2)
            dS = ds_scr[...] + dyb * r_t
            dsa = _pair_sum(dS * kka_t, ones2)
            ds_scr[...] = dS * w_t - dsa * kk_t
            sc_scr[...] = s_prev
            for ref, val in zip((dr_ref, dw_ref, dk2_ref, dv_ref, dkk_ref, dkka_ref),
                                (s_cur * dyb, dS * s_prev, dS * vb, eye2 * _pair_sum(dS * k2_t, ones2),
                                 -(s_prev * dsa), dS * sa)):
                ref[pl.ds(t, 1), :] = _pair_flat(colsum(val))
            return carry

        lax.fori_loop(0, tc, step, 0, unroll=SCAN_UNROLL_BWD)

        if scatter is not None:
            @pl.when(pl.program_id(0) == nchunk - 1)
            def _():
                for cp in copies:
                    cp.wait()

    in_specs = [seq] * 6 + [st_spec, one_state, seq]
    out_specs = [seq] * 6
    out_shape = [jax.ShapeDtypeStruct((T, W), F32)] * 6
    scratch = [pltpu.VMEM((N_PAIRS, HEAD_DIM, 128), F32)] * 2 + [pltpu.VMEM((tc, W), F32)] * 2
    args = [r, w, k2, v, kk, kka, states, final_state, dy]
    if scatter is not None:
        in_specs += [HBM_SPEC] * nb
        out_specs += [HBM_SPEC] * nb
        out_shape += [jax.ShapeDtypeStruct((npeer, *s.shape[2:]), s.dtype) for s in scatter]
        scratch += [pltpu.SemaphoreType.DMA((npeer * nb,)), pltpu.SemaphoreType.DMA((npeer * nb,))]
        args += list(scatter)
    return pl.pallas_call(
        body, name="rwkv_scan_bwd", grid=(nchunk,), in_specs=in_specs, out_specs=out_specs, out_shape=out_shape,
        scratch_shapes=scratch, compiler_params=_cp(("arbitrary",)),
    )(*args)


def _mix_common(y, r, k2, v, lnx_w, lnx_b, r_k):
    yc = y - _segsum(y) * (1.0 / HEAD_DIM)
    rstd = lax.rsqrt(_segsum(yc * yc) * (1.0 / HEAD_DIM) + LNX_EPS)
    yhat = yc * rstd
    s = _segsum(r * k2 * r_k)
    return rstd, yhat, s, yhat * lnx_w + lnx_b + s * v


def mix_gate_fwd(y, r, k2, v, g, y_mem, lnx_w, lnx_b, r_k):
    T = y.shape[0]

    def fn(i, nblk, y, r, k2, v, g, ym, lw, lb, rk):
        mix = _mix_common(y, r, k2, v, lw, lb, rk)[3]
        return jnp.concatenate([mix * g, ym], axis=1)

    return _rowmap(fn, name="mix_gate_fwd", T=T, tb=256,
                   ins=[("row", z) for z in (y, r, k2, v, g, y_mem)] + [("const", c) for c in (lnx_w, lnx_b, r_k)],
                   outs=[("row", RWKV_WIDTH + MEM_WIDTH, MMD)])[0]


def mix_gate_bwd(y, r, k2, v, g, dycat, lnx_w, lnx_b, r_k):
    T = y.shape[0]

    def fn(i, nblk, y, r, k2, v, g, dyc, lw, lb, rk):
        rstd, yhat, s, mix = _mix_common(y, r, k2, v, lw, lb, rk)
        dmix = dyc * g
        dyh = dmix * lw
        inv = 1.0 / HEAD_DIM
        dy = rstd * (dyh - _segsum(dyh) * inv - yhat * (_segsum(dyh * yhat) * inv))
        ds = _segsum(dmix * v)
        cs = lambda z: jnp.sum(z, axis=0, keepdims=True)
        return (dy, ds * k2 * rk, ds * r * rk, dmix * s, dyc * mix, cs(dmix * yhat), cs(dmix), cs(ds * r * k2))

    return _rowmap(fn, name="mix_gate_bwd", T=T, tb=256,
                   ins=[("row", z) for z in (y, r, k2, v, g)] + [("row", dycat, RWKV_WIDTH, 0)]
                   + [("const", c) for c in (lnx_w, lnx_b, r_k)],
                   outs=[("row", RWKV_WIDTH, F32)] * 5, accs=[((1, RWKV_WIDTH), F32)] * 3)


def _head_rms(x, gones):
    ms = _segsum(x * x) * (1.0 / HEAD_DIM)
    r = lax.rsqrt(ms + RMS_EPS)
    return r, x * r


def _head_rms_bwd(dxn_g, r, xh, gones):
    return r * (dxn_g - xh * (_segsum(dxn_g * xh) * (1.0 / HEAD_DIM)))


def mem_kv_fwd(mem, norm_g, w_kv, k_norm_t, *, name):
    gones = _group_ones(MEM_WIDTH)

    def body(mem_ref, g_ref, w_ref, kn_ref, go_ref, k_out, v_out):
        _, xh = _rms_stats(mem_ref[...])
        kv = _dot(xh * g_ref[...], w_ref[...])
        _, kh = _head_rms(kv[:, :MEM_WIDTH], go_ref[...])
        k_out[...] = kh * kn_ref[...]
        v_out[...] = kv[:, MEM_WIDTH:]

    return pl.pallas_call(
        body, name=name, out_shape=[jax.ShapeDtypeStruct((N_MEM, MEM_WIDTH), F32)] * 2, compiler_params=_cp(),
    )(mem, norm_g, w_kv, k_norm_t, gones)


def mem_kv_bwd(mem, norm_g, w_kv, k_norm_t, dkn, dv, *, name):
    gones, fold = _group_ones(MEM_WIDTH), _fold_ones(MEM_WIDTH)

    def body(mem_ref, g_ref, w_ref, kn_ref, go_ref, fo_ref, dkn_ref, dv_ref, dw_out, dg_out, dkg_out):
        _, xh = _rms_stats(mem_ref[...])
        hm = xh * g_ref[...]
        kv = _dot(hm, w_ref[...])
        r, kh = _head_rms(kv[:, :MEM_WIDTH], go_ref[...])
        dkn = dkn_ref[...]
        dkg_out[...] = _fold_heads(jnp.sum(dkn * kh, axis=0, keepdims=True), fo_ref[...])
        dkraw = _head_rms_bwd(dkn * kn_ref[...], r, kh, go_ref[...])
        dkv = jnp.concatenate([dkraw, dv_ref[...]], axis=1)
        dw_out[...] = _dot_tn(hm, dkv)
        dg_out[...] = jnp.sum(_dot_nt(dkv, w_ref[...]) * xh, axis=0, keepdims=True)

    return pl.pallas_call(
        body, name=name,
        out_shape=[jax.ShapeDtypeStruct((D_MODEL, 2 * MEM_WIDTH), F32), jax.ShapeDtypeStruct((1, D_MODEL), F32),
                   jax.ShapeDtypeStruct((1, HEAD_DIM), F32)],
        compiler_params=_cp(),
    )(mem, norm_g, w_kv, k_norm_t, gones, fold, dkn, dv)


def _mem_scores(qn, kn, masks, h):
    s = _dot_nt(qn * masks[h], kn) * (1.0 / math.sqrt(HEAD_DIM))
    s = s - jnp.max(s, axis=-1, keepdims=True)
    e = jnp.exp(s)
    return e / jnp.sum(e, axis=-1, keepdims=True)


def mem_attn_fwd(p, colblock, kn, v, q_norm_t, *, name):
    T = p.shape[0]
    gones, masks = _group_ones(MEM_WIDTH), _head_masks(MEM_WIDTH)

    def fn(i, nblk, q, kn, v, qg, go, masks):
        _, qh = _head_rms(q, go)
        qn = qh * qg
        out = jnp.zeros(q.shape, F32)
        for h in range(MEM_WIDTH // HEAD_DIM):
            out = out + _dot(_mem_scores(qn, kn, masks, h), v * masks[h])
        return out

    return _rowmap(fn, name=name, T=T, tb=512,
                   ins=[("row", p, MEM_WIDTH, colblock)] + [("const", c) for c in (kn, v, q_norm_t, gones, masks)],
                   outs=[("row", MEM_WIDTH, F32)])[0]


def mem_attn_bwd(p, colblock, kn, v, q_norm_t, dycat, dcolblock, *, name):
    T = p.shape[0]
    gones, masks, fold = _group_ones(MEM_WIDTH), _head_masks(MEM_WIDTH), _fold_ones(MEM_WIDTH)
    scale = 1.0 / math.sqrt(HEAD_DIM)

    def fn(i, nblk, q, dy, kn, v, qg, go, masks, fo):
        r, qh = _head_rms(q, go)
        qn = qh * qg
        dqn = jnp.zeros(q.shape, F32)
        dkn = jnp.zeros(kn.shape, F32)
        dv = jnp.zeros(v.shape, F32)
        for h in range(MEM_WIDTH // HEAD_DIM):
            pr = _mem_scores(qn, kn, masks, h)
            dyh = dy * masks[h]
            dpr = _dot_nt(dyh, v)
            ds = pr * (dpr - jnp.sum(dpr * pr, axis=-1, keepdims=True)) * scale
            dqn = dqn + _dot(ds, kn * masks[h])
            dkn = dkn + _dot_tn(ds, qn * masks[h])
            dv = dv + _dot_tn(pr, dyh)
        dqg = _fold_heads(jnp.sum(dqn * qh, axis=0, keepdims=True), fo)
        return _head_rms_bwd(dqn * qg, r, qh, go), dkn, dv, dqg

    return _rowmap(fn, name=name, T=T, tb=512,
                   ins=[("row", p, MEM_WIDTH, colblock), ("row", dycat, MEM_WIDTH, dcolblock)]
                   + [("const", c) for c in (kn, v, q_norm_t, gones, masks, fold)],
                   outs=[("row", MEM_WIDTH, F32)],
                   accs=[((N_MEM, MEM_WIDTH), F32), ((N_MEM, MEM_WIDTH), F32), ((1, HEAD_DIM), F32)])


def _ffn_conv(i, u, halo, cw, cb):
    up1 = _shift_down(u, _row_pick(halo, 7), i == 0)
    up2 = _shift_down(up1, _row_pick(halo, 6), i == 0)
    c = cb + cw[0] * up2 + cw[1] * up1 + cw[2] * u
    return up1, up2, c[:, :D_FF], c[:, D_FF:]


def ffn_act_fwd(u, cw, cb, *, name):
    T = u.shape[0]

    def fn(i, nblk, u, halo, c0, c1, c2, cb):
        _, _, gate, val = _ffn_conv(i, u, halo, (c0, c1, c2), cb)
        return jax.nn.silu(gate) * val

    return _rowmap(fn, name=name, T=T, tb=128, ins=[("row", u), ("prev", u)] + [("const", c) for c in (*cw, cb)],
                   outs=[("row", D_FF, MMD)])[0]


def ffn_act_bwd(u, cw, cb, dz, *, name):
    T = u.shape[0]
    tb = 128

    def fn(i, nblk, u, halo, unext, c0, c1, c2, cb, dz, dznext):
        ue = jnp.concatenate([u, unext], axis=0)
        dze = jnp.concatenate([dz, jnp.where(i == nblk - 1, 0.0, 1.0) * dznext], axis=0)
        up1, up2, gate, val = _ffn_conv(i, ue, halo, (c0, c1, c2), cb)
        sg = jax.nn.sigmoid(gate)
        dce = jnp.concatenate([dze * val * sg * (1.0 + gate * (1.0 - sg)), dze * gate * sg], axis=1)
        rows = dce.shape[0]
        du = (c2 * dce + c1 * pltpu.roll(dce, rows - 1, axis=0) + c0 * pltpu.roll(dce, rows - 2, axis=0))[:tb]
        dc = dce[:tb]
        s = lambda z: jnp.sum(z, axis=0, keepdims=True)
        return du, s(dc * up2[:tb]), s(dc * up1[:tb]), s(dc * u), s(dc)

    return _rowmap(fn, name=name, T=T, tb=tb,
                   ins=[("row", u), ("prev", u), ("next", u)] + [("const", c) for c in (*cw, cb)]
                   + [("row", dz), ("next", dz)],
                   outs=[("row", 2 * D_FF, MMD)], accs=[((1, 2 * D_FF), F32)] * 4)


def _rope_swap(z):
    lane = lax.broadcasted_iota(jnp.int32, z.shape, 1) % HEAD_DIM
    w = z.shape[1]
    return jnp.where(lane < HEAD_DIM // 2, pltpu.roll(z, w - HEAD_DIM // 2, axis=1), pltpu.roll(z, HEAD_DIM // 2, axis=1))


def rope_tables(T):
    inv = (np.float32(ROPE_THETA) ** (-np.arange(0, HEAD_DIM, 2, dtype=np.float32) / np.float32(HEAD_DIM))).astype(np.float32)
    ang = (np.arange(T, dtype=np.float32)[:, None] * inv[None, :]).astype(np.float64)
    cos, sin = np.cos(ang).astype(np.float32), np.sin(ang).astype(np.float32)
    return (jnp.asarray(np.concatenate([cos, cos, cos, cos], axis=1)),
            jnp.asarray(np.concatenate([-sin, sin, -sin, sin], axis=1)))


def _rope_wide(t):
    return jnp.tile(t, (1, DIL_WIDTH // t.shape[1]))


def qk_fwd(kvp, pb, kg_t, qg_t, cos, sin):
    T = kvp.shape[0]
    gones = _group_ones(DIL_WIDTH)

    def fn(i, nblk, kraw, vraw, qraw, kg, qg, c, s, go):
        c, s = _rope_wide(c), _rope_wide(s)
        outs = []
        for raw, g in ((qraw, qg), (kraw, kg)):
            _, xh = _head_rms(raw, go)
            z = xh * g
            outs.append(z * c + _rope_swap(z) * s)
        return outs[0], outs[1], vraw

    return _rowmap(fn, name="qk_fwd", T=T, tb=256,
                   ins=[("row", kvp, DIL_WIDTH, 0), ("row", kvp, DIL_WIDTH, 1), ("row", pb, DIL_WIDTH, 0)]
                   + [("const", kg_t), ("const", qg_t), ("row", cos), ("row", sin), ("const", gones)],
                   outs=[("row", DIL_WIDTH, MMD)] * 3)


def qk_bwd(kvp, pb, kg_t, qg_t, cos, sin, dq, dk, dv, dq_mem):
    T = kvp.shape[0]
    gones, fold = _group_ones(DIL_WIDTH), _fold_ones(DIL_WIDTH)

    def fn(i, nblk, kraw, qraw, kg, qg, c, s, go, fo, dq, dk, dv, dqm):
        c, s = _rope_wide(c), _rope_wide(s)
        res, dgs = [], []
        for raw, g, d in ((qraw, qg, dq), (kraw, kg, dk)):
            r, xh = _head_rms(raw, go)
            dz = d * c + _rope_swap(d * s)
            dgs.append(_fold_heads(jnp.sum(dz * xh, axis=0, keepdims=True), fo))
            res.append(_head_rms_bwd(dz * g, r, xh, go))
        return (jnp.concatenate([res[0], dqm], axis=1), jnp.concatenate([res[1], dv], axis=1), dgs[0], dgs[1])

    return _rowmap(fn, name="qk_bwd", T=T, tb=256,
                   ins=[("row", kvp, DIL_WIDTH, 0), ("row", pb, DIL_WIDTH, 0), ("const", kg_t), ("const", qg_t),
                        ("row", cos), ("row", sin), ("const", gones), ("const", fold),
                        ("row", dq), ("row", dk), ("row", dv), ("row", dq_mem)],
                   outs=[("row", DIL_WIDTH + MEM_WIDTH, MMD), ("row", 2 * DIL_WIDTH, MMD)],
                   accs=[((1, HEAD_DIM), F32)] * 2)


def _band(kind):
    i = lax.broadcasted_iota(jnp.int32, (DIL_BLOCK, DIL_BLOCK), 0)
    j = lax.broadcasted_iota(jnp.int32, (DIL_BLOCK, DIL_BLOCK), 1)
    return (j <= i) if kind == "cur" else (j >= i)


def dil_attn_fwd(q, k, v, seq_blocks, *, name):
    T, W = q.shape
    nb = T // DIL_BLOCK
    masks = _head_masks(W)
    cur = pl.BlockSpec((DIL_BLOCK, W), lambda n: (n, 0))
    prv = pl.BlockSpec((DIL_BLOCK, W), lambda n: (jnp.maximum(n - 1, 0), 0))
    scale = 1.0 / math.sqrt(HEAD_DIM)

    def body(q_ref, kc_ref, kp_ref, vc_ref, vp_ref, m_ref, o_ref, l_ref):
        n = pl.program_id(0)
        has_prev = (n % seq_blocks) != 0
        q = q_ref[...].astype(F32)
        kc, kp = kc_ref[...].astype(F32), kp_ref[...].astype(F32)
        vc, vp = vc_ref[...].astype(F32), vp_ref[...].astype(F32)
        ok_c = _band("cur")
        ok_p = jnp.logical_and(_band("prev"), has_prev)
        o = jnp.zeros((DIL_BLOCK, W), F32)
        lse = jnp.zeros((DIL_BLOCK, W), F32)
        for h in range(W // HEAD_DIM):
            mh = m_ref[h]
            qh = q * mh
            sc = jnp.where(ok_c, _dot_nt(qh, kc) * scale, NEG_INF)
            sp = jnp.where(ok_p, _dot_nt(qh, kp) * scale, NEG_INF)
            mx = jnp.maximum(jnp.max(sc, axis=-1, keepdims=True), jnp.max(sp, axis=-1, keepdims=True))
            ec, ep = jnp.exp(sc - mx), jnp.exp(sp - mx)
            den = jnp.sum(ec, axis=-1, keepdims=True) + jnp.sum(ep, axis=-1, keepdims=True)
            o = o + (_dot(ec, vc * mh) + _dot(ep, vp * mh)) / den
            lse = lse + (mx + jnp.log(den)) * mh
        o_ref[...] = o
        l_ref[...] = lse

    return pl.pallas_call(
        body, name=name, grid=(nb,), in_specs=[cur, cur, prv, cur, prv, pl.BlockSpec(masks.shape, lambda n: (0, 0, 0))],
        out_specs=[cur, cur], out_shape=[jax.ShapeDtypeStruct((T, W), F32)] * 2,
        compiler_params=_cp(("parallel",)),
    )(q, k, k, v, v, masks)


def dil_attn_bwd(q, k, v, o, lse, do, dlse, seq_blocks, *, name):
    T, W = q.shape
    nb = T // DIL_BLOCK
    masks = _head_masks(W)
    cur = pl.BlockSpec((DIL_BLOCK, W), lambda n: (n, 0))
    prv = pl.BlockSpec((DIL_BLOCK, W), lambda n: (jnp.maximum(n - 1, 0), 0))
    nxt = pl.BlockSpec((DIL_BLOCK, W), lambda n: (jnp.minimum(n + 1, nb - 1), 0))
    scale = 1.0 / math.sqrt(HEAD_DIM)

    def body(qc_ref, qn_ref, kc_ref, kp_ref, vc_ref, vp_ref, oc_ref, on_ref, lc_ref, ln_ref, doc_ref, don_ref,
             dlc_ref, dln_ref, m_ref, dq_ref, dk_ref, dv_ref):
        n = pl.program_id(0)
        has_prev = (n % seq_blocks) != 0
        has_next = jnp.logical_and(((n + 1) % seq_blocks) != 0, n + 1 < nb)
        f = lambda ref: ref[...].astype(F32)
        qc, qn, kc, kp, vc, vp = f(qc_ref), f(qn_ref), f(kc_ref), f(kp_ref), f(vc_ref), f(vp_ref)
        doc, don = doc_ref[...], don_ref[...]
        ok_c = _band("cur")
        ok_p = jnp.logical_and(_band("prev"), has_prev)
        ok_n = jnp.logical_and(_band("prev"), has_next)
        dq = jnp.zeros((DIL_BLOCK, W), F32)
        dk = jnp.zeros((DIL_BLOCK, W), F32)
        dv = jnp.zeros((DIL_BLOCK, W), F32)

        def side(qh, kk, vv, doh, lse_h, corr, ok):
            s = _dot_nt(qh, kk) * scale
            pr = jnp.where(ok, jnp.exp(jnp.where(ok, s, NEG_INF) - lse_h), 0.0)
            ds = pr * (_dot_nt(doh, vv) + corr) * scale
            return pr, ds

        for h in range(W // HEAD_DIM):
            mh = m_ref[h]
            red = lambda z: jnp.sum(z * mh, axis=-1, keepdims=True)
            qh, doh = qc * mh, doc * mh
            lse_h = red(lc_ref[...]) * (1.0 / HEAD_DIM)
            corr = red(dlc_ref[...]) - red(doc * oc_ref[...])
            pr_c, ds_c = side(qh, kc, vc * mh, doh, lse_h, corr, ok_c)
            _, ds_p = side(qh, kp, vp * mh, doh, lse_h, corr, ok_p)
            dq = dq + _dot(ds_c, kc * mh) + _dot(ds_p, kp * mh)
            dk = dk + _dot_tn(ds_c, qh)
            dv = dv + _dot_tn(pr_c, doh)
            qh2, doh2 = qn * mh, don * mh
            lse_2 = red(ln_ref[...]) * (1.0 / HEAD_DIM)
            corr2 = red(dln_ref[...]) - red(don * on_ref[...])
            pr_n, ds_n = side(qh2, kc, vc * mh, doh2, lse_2, corr2, ok_n)
            dk = dk + _dot_tn(ds_n, qh2)
            dv = dv + _dot_tn(pr_n, doh2)
        dq_ref[...] = dq
        dk_ref[...] = dk
        dv_ref[...] = dv

    return pl.pallas_call(
        body, name=name, grid=(nb,),
        in_specs=[cur, nxt, cur, prv, cur, prv, cur, nxt, cur, nxt, cur, nxt, cur, nxt,
                  pl.BlockSpec(masks.shape, lambda n: (0, 0, 0))],
        out_specs=[cur] * 3, out_shape=[jax.ShapeDtypeStruct((T, W), F32)] * 3,
        compiler_params=_cp(("parallel",)),
    )(q, q, k, k, v, v, o, o, lse, lse, do, do, dlse, dlse, masks)


def _mix_weights(ls):
    m = jnp.maximum(jnp.maximum(ls[0], ls[1]), ls[2])
    es = [jnp.exp(l - m) for l in ls]
    den = es[0] + es[1] + es[2]
    return [e / den for e in es]


def mix_fwd(os_, ls, y_mem):
    T = y_mem.shape[0]

    def fn(i, nblk, o0, o1, o2, l0, l1, l2, ym):
        w = _mix_weights((l0, l1, l2))
        return jnp.concatenate([w[0] * o0 + w[1] * o1 + w[2] * o2, ym], axis=1)

    return _rowmap(fn, name="mix_fwd", T=T, tb=512, ins=[("row", z) for z in (*os_, *ls, y_mem)],
                   outs=[("row", 2 * MEM_WIDTH, MMD)])[0]


def mix_bwd(os_, ls, dycat):
    T = dycat.shape[0]

    def fn(i, nblk, o0, o1, o2, l0, l1, l2, dy):
        w = _mix_weights((l0, l1, l2))
        os3 = (o0, o1, o2)
        dws = [dy * o for o in os3]
        tot = w[0] * dws[0] + w[1] * dws[1] + w[2] * dws[2]
        return tuple(wg * dy for wg in w) + tuple(wg * (dw - tot) for wg, dw in zip(w, dws))

    return _rowmap(fn, name="mix_bwd", T=T, tb=512,
                   ins=[("row", z) for z in (*os_, *ls)] + [("row", dycat, MEM_WIDTH, 0)],
                   outs=[("row", MEM_WIDTH, F32)] * 6)


def loss_fwd_bwd(y, target):
    T, D = y.shape

    def fn(i, nblk, y, t):
        e = y - t
        return e * (1.0 / D), jnp.zeros((8, 128), F32) + jnp.sum(e * e) * (0.5 / D)

    return _rowmap(fn, name="loss", T=T, tb=512, ins=[("row", y), ("row", target)], outs=[("row", D, F32)],
                   accs=[((8, 128), F32)])


def _to_residues(z, dil):
    T, W = z.shape
    return z.reshape(T // dil, dil, W).transpose(1, 0, 2).reshape(T, W)


def _from_residues(z, dil):
    T, W = z.shape
    return z.reshape(dil, T // dil, W).transpose(1, 0, 2).reshape(T, W)


def _pad_rows(w, rows):
    return jnp.concatenate([w, jnp.zeros((rows - w.shape[0], w.shape[1]), w.dtype)], axis=0)


def _tile_heads(g, width):
    return jnp.tile(g.reshape(1, HEAD_DIM), (1, width // HEAD_DIM))


def _conv_rows(W, i):
    return [W["ffn_conv_w"][i][j:j + 1] for j in range(3)]


def _ffn_fwd(x, i, W):
    hn = rms_fwd(x, [W["ffn_norm"][i:i + 1]], name=f"ffn_rms{i}")[0]
    u = _mm(hn, W["ffn_w_up"][i], name=f"ffn_up{i}")
    z = ffn_act_fwd(u, _conv_rows(W, i), W["ffn_conv_b"][i:i + 1], name=f"ffn_act{i}")
    out = _mm(z, W["ffn_w_down"][i], add=x, name=f"ffn_down{i}")
    return out, (x, hn, u, z)


def _ffn_bwd(dout, i, W, saved, G):
    x, hn, u, z = saved
    dz = _mm(dout, W["ffn_w_down"][i], tb=True, name=f"ffn_down_dx{i}")
    G["ffn_w_down"][i] = _mm(z, dout, ta=True, name=f"ffn_down_dw{i}")
    du, dw0, dw1, dw2, db = ffn_act_bwd(u, _conv_rows(W, i), W["ffn_conv_b"][i:i + 1], dz, name=f"ffn_act_bwd{i}")
    G["ffn_conv_w"][i] = jnp.concatenate([dw0, dw1, dw2], axis=0)
    G["ffn_conv_b"][i] = db[0]
    dhn = _mm(du, W["ffn_w_up"][i], tb=True, name=f"ffn_up_dx{i}")
    G["ffn_w_up"][i] = _mm(hn, du, ta=True, name=f"ffn_up_dw{i}")
    dx, dg = rms_bwd(x, [W["ffn_norm"][i:i + 1]], [dhn], dout, name=f"ffn_rms_bwd{i}")
    G["ffn_norm"][i] = dg[0]
    return dx


def local_step(x, mem, target, W, late=None):
    T = x.shape[0]
    W = dict(W)
    G = {"ffn_w_down": [None, None], "ffn_w_up": [None, None], "ffn_conv_w": [None, None],
         "ffn_conv_b": [None, None], "ffn_norm": [None, None], "attn_norm": [None, None], "mem_norm": [None, None],
         "mem_w_kv": [None, None], "mem_q_norm": [None, None], "mem_k_norm": [None, None]}
    mu, w0, a0 = W["a_mu"], W["a_w0"], W["a_a0"]
    w2p, a2p, g2p = (_pad_rows(W["a_w2"][0], LORA_WIDTH),
                     jnp.concatenate([jnp.zeros((64, RWKV_WIDTH), MMD), W["a_a2"][0],
                                      jnp.zeros((128, RWKV_WIDTH), MMD)], axis=0),
                     jnp.concatenate([jnp.zeros((128, RWKV_WIDTH), MMD), W["a_g2"][0]], axis=0))
    k_k, k_a, lnx_w, lnx_b = W["a_k_k"], W["a_k_a"], W["a_lnx_w"], W["a_lnx_b"]
    r_k = W["a_r_k"].reshape(1, RWKV_WIDTH)

    h0 = rms_fwd(x, [W["attn_norm"][0:1]], name="attn_rms0")[0]
    p = _mm(h0, W["a_w_in"][0], name="a_in")
    r, w, k2, v, kk, kka, g = rwkv_pre_fwd(p, mu, w0, a0, w2p, a2p, g2p, k_k, k_a)
    if late is None:
        y, states, final_state = scan_fwd(r, w, k2, v, kk, kka)
    else:
        y, states, final_state, *gathered = scan_fwd(r, w, k2, v, kk, kka, gather=late[0])
        W.update(late[1](gathered))
    memkv = []
    for i in range(2):
        memkv.append(mem_kv_fwd(mem, W["mem_norm"][i:i + 1], W["mem_w_kv"][i], _tile_heads(W["mem_k_norm"][i], MEM_WIDTH),
                                name=f"mem_kv{i}"))
    qg0 = _tile_heads(W["mem_q_norm"][0], MEM_WIDTH)
    y_mem0 = mem_attn_fwd(p, SHIFT_WIDTH // MEM_WIDTH, memkv[0][0], memkv[0][1], qg0, name="mem_attn0")
    ycat0 = mix_gate_fwd(y, r, k2, v, g, y_mem0, lnx_w, lnx_b, r_k)
    x1 = _mm(ycat0, W["a_w_out"][0], add=x, name="a_out")
    x2, ffn0 = _ffn_fwd(x1, 0, W)

    h1, hkv = rms_fwd(x2, [W["attn_norm"][1:2], W["kv_norm"].reshape(1, -1)], name="attn_rms1")
    kvp = _mm(hkv, W["kv_w"], name="kv_in")
    pb = _mm(h1, W["b_w_in"][0], name="b_in")
    cos, sin = rope_tables(T)
    kg_t, qg_t = _tile_heads(W["kv_k_norm"], DIL_WIDTH), _tile_heads(W["b_q_norm"][0], DIL_WIDTH)
    q, ksh, vsh = qk_fwd(kvp, pb, kg_t, qg_t, cos, sin)
    os_, ls, grp = [], [], []
    for gi, (win, dil) in enumerate(DIL_GROUPS):
        sl = slice(gi * MEM_WIDTH, (gi + 1) * MEM_WIDTH)
        qg_, kg_, vg_ = (_to_residues(z[:, sl], dil) for z in (q, ksh, vsh))
        o_r, l_r = dil_attn_fwd(qg_, kg_, vg_, T // dil // DIL_BLOCK, name=f"dil_fwd{gi}")
        grp.append((qg_, kg_, vg_, o_r, l_r))
        os_.append(_from_residues(o_r, dil))
        ls.append(_from_residues(l_r, dil))
    qg1 = _tile_heads(W["mem_q_norm"][1], MEM_WIDTH)
    y_mem1 = mem_attn_fwd(pb, DIL_WIDTH // MEM_WIDTH, memkv[1][0], memkv[1][1], qg1, name="mem_attn1")
    ycat1 = mix_fwd(os_, ls, y_mem1)
    x3 = _mm(ycat1, W["b_w_out"][0], add=x2, name="b_out")
    x4, ffn1 = _ffn_fwd(x3, 1, W)

    dx4, loss = loss_fwd_bwd(x4, target)

    dx3 = _ffn_bwd(dx4, 1, W, ffn1, G)
    dycat1 = _mm(dx3, W["b_w_out"][0], tb=True, name="b_out_dx")
    G["b_w_out"] = _mm(ycat1, dx3, ta=True, name="b_out_dw")[None]
    dq_mem1, dkn1, dvm1, dqg1 = mem_attn_bwd(pb, DIL_WIDTH // MEM_WIDTH, memkv[1][0], memkv[1][1], qg1, dycat1, 1,
                                             name="mem_attn_bwd1")
    G["mem_q_norm"][1] = dqg1[0]
    d_os_ls = mix_bwd(os_, ls, dycat1)
    dqs, dks, dvs = [], [], []
    for gi, (win, dil) in enumerate(DIL_GROUPS):
        qg_, kg_, vg_, o_r, l_r = grp[gi]
        do_r, dl_r = _to_residues(d_os_ls[gi], dil), _to_residues(d_os_ls[3 + gi], dil)
        dq_r, dk_r, dv_r = dil_attn_bwd(qg_, kg_, vg_, o_r, l_r, do_r, dl_r, T // dil // DIL_BLOCK, name=f"dil_bwd{gi}")
        dqs.append(_from_residues(dq_r, dil))
        dks.append(_from_residues(dk_r, dil))
        dvs.append(_from_residues(dv_r, dil))
    dq, dk, dv = (jnp.concatenate(z, axis=1) for z in (dqs, dks, dvs))
    dpb, dkvp, dqn_g, dkn_g = qk_bwd(kvp, pb, kg_t, qg_t, cos, sin, dq, dk, dv, dq_mem1)
    G["b_q_norm"] = dqn_g
    G["kv_k_norm"] = dkn_g[0]
    dh1 = _mm(dpb, W["b_w_in"][0], tb=True, name="b_in_dx")
    G["b_w_in"] = _mm(h1, dpb, ta=True, name="b_in_dw")[None]
    dhkv = _mm(dkvp, W["kv_w"], tb=True, name="kv_in_dx")
    G["kv_w"] = _mm(hkv, dkvp, ta=True, name="kv_in_dw")
    dx2, dg1, dgkv = rms_bwd(x2, [W["attn_norm"][1:2], W["kv_norm"].reshape(1, -1)], [dh1, dhkv], dx3,
                             name="attn_rms_bwd1")
    G["attn_norm"][1] = dg1[0]
    G["kv_norm"] = dgkv[0]

    dx1 = _ffn_bwd(dx2, 0, W, ffn0, G)
    dycat0 = _mm(dx1, W["a_w_out"][0], tb=True, name="a_out_dx")
    G["a_w_out"] = _mm(ycat0, dx1, ta=True, name="a_out_dw")[None]
    dq_mem0, dkn0, dvm0, dqg0 = mem_attn_bwd(p, SHIFT_WIDTH // MEM_WIDTH, memkv[0][0], memkv[0][1], qg0, dycat0,
                                             RWKV_WIDTH // MEM_WIDTH, name="mem_attn_bwd0")
    G["mem_q_norm"][0] = dqg0[0]
    dy, dr_b, dk2_b, dv_b, dg, dlw, dlb, drk = mix_gate_bwd(y, r, k2, v, g, dycat0, lnx_w, lnx_b, r_k)
    for i, (dkn, dvm) in enumerate(((dkn0, dvm0), (dkn1, dvm1))):
        dwkv, dgm, dkg = mem_kv_bwd(mem, W["mem_norm"][i:i + 1], W["mem_w_kv"][i],
                                    _tile_heads(W["mem_k_norm"][i], MEM_WIDTH), dkn, dvm, name=f"mem_kv_bwd{i}")
        G["mem_w_kv"][i], G["mem_norm"][i], G["mem_k_norm"][i] = dwkv, dgm[0], dkg[0]
    for n in list(G):
        if isinstance(G[n], list) and all(z is not None for z in G[n]):
            G[n] = jnp.stack(G[n], axis=0)
    late_out = None
    if late is None:
        dr, dw, dk2, dv, dkk, dkka = scan_bwd(r, w, k2, v, kk, kka, states, final_state, dy)
    else:
        pieces = late[2](G)
        dr, dw, dk2, dv, dkk, dkka, *received = scan_bwd(r, w, k2, v, kk, kka, states, final_state, dy, scatter=pieces)
        late_out = (received, pieces)
    dxs, dmu, dw0, da0, dw2p, da2p, dg2p, dk_k, dk_a = rwkv_pre_bwd(
        p, mu, w0, a0, w2p, a2p, g2p, k_k, k_a, (dr, dr_b), dw, (dk2, dk2_b), (dv, dv_b), dkk, dkka, dg)
    dp = shift_bwd(dxs, mu, dq_mem0)
    G.update(a_mu=dmu, a_w0=dw0, a_a0=da0, a_w2=dw2p[None, :64], a_a2=da2p[None, 64:128], a_g2=dg2p[None, 128:],
             a_k_k=dk_k, a_k_a=dk_a, a_r_k=drk.reshape(1, RWKV_HEADS, HEAD_DIM), a_lnx_w=dlw, a_lnx_b=dlb)
    dh0 = _mm(dp, W["a_w_in"][0], tb=True, name="a_in_dx")
    G["a_w_in"] = _mm(h0, dp, ta=True, name="a_in_dw")[None]
    grad_x, dg0 = rms_bwd(x, [W["attn_norm"][0:1]], [dh0], dx1, name="attn_rms_bwd0")
    G["attn_norm"][0] = dg0[0]
    G["attn_norm"] = jnp.stack(G["attn_norm"], axis=0)
    return loss, grad_x, G, late_out


HBM_SPEC = pl.BlockSpec(memory_space=pltpu.HBM)


def _mesh_pos():
    return lax.axis_index("x"), lax.axis_index("y"), lax.axis_index("c")


def _other_chips(x, y):
    return [(1 - x, y), (x, 1 - y), (1 - x, 1 - y)]


def _remote(send_sems, recv_sems, k, src, dst, to):
    return pltpu.make_async_remote_copy(src_ref=src, dst_ref=dst, send_sem=send_sems.at[k], recv_sem=recv_sems.at[k],
                                        device_id=to, device_id_type=MESH)


def _comm_call(body, name, ins, out_shape, n_remote):
    scratch = [pltpu.SemaphoreType.DMA((n_remote,)), pltpu.SemaphoreType.DMA((n_remote,))]
    return pl.pallas_call(body, name=name, in_specs=[HBM_SPEC] * len(ins), out_specs=[HBM_SPEC] * len(out_shape),
                          out_shape=out_shape, scratch_shapes=scratch)(*ins)


def comm_gather(wbig, wsm):
    def body(wb, ws, ob, os_, send_sems, recv_sems):
        x, y, c = _mesh_pos()
        s = 2 * x + y
        me, sibling = (x, y, c), (x, y, 1 - c)
        chips = _other_chips(x, y)
        rc = functools.partial(_remote, send_sems, recv_sems)
        first = []
        for j, (cx, cy) in enumerate(chips):
            first.append(rc(j, wb.at[c], ob.at[s, c], (cx, cy, c)))
            first.append(rc(6 + j, ws, os_.at[s], (cx, cy, c)))
        for cp in first:
            cp.start()
        passed = []
        for j, (cx, cy) in enumerate(chips):
            blk = ob.at[2 * cx + cy, c]
            rc(j, blk, blk, me).wait_recv()
            passed.append(rc(3 + j, blk, blk, sibling))
            passed[-1].start()
        for j, (cx, cy) in enumerate(chips):
            blk = ob.at[2 * cx + cy, 1 - c]
            rc(3 + j, blk, blk, me).wait_recv()
            sb = os_.at[2 * cx + cy]
            rc(6 + j, sb, sb, me).wait_recv()
        for cp in first + passed:
            cp.wait_send()

    out_shape = [jax.ShapeDtypeStruct((N_CHIPS, *wbig.shape), wbig.dtype),
                 jax.ShapeDtypeStruct((N_CHIPS, *wsm.shape), wsm.dtype)]
    return _comm_call(body, "comm_gather", [wbig, wsm], out_shape, 9)


def comm_pair_exchange(gb, gs):
    def body(gb_ref, gs_ref, rb_ref, rs_ref, send_sems, recv_sems):
        x, y, c = _mesh_pos()
        sibling = (x, y, 1 - c)
        rc = functools.partial(_remote, send_sems, recv_sems)
        cps = [rc(r, gb_ref.at[r, 1 - c], rb_ref.at[r], sibling) for r in range(N_CHIPS)]
        cps.append(rc(N_CHIPS, gs_ref.at[1 - c], rs_ref, sibling))
        for cp in cps:
            cp.start()
        for cp in cps:
            cp.wait()

    out_shape = [jax.ShapeDtypeStruct((N_CHIPS, *gb.shape[2:]), gb.dtype), jax.ShapeDtypeStruct(gs.shape[1:], gs.dtype)]
    return _comm_call(body, "comm_pair_exchange", [gb, gs], out_shape, N_CHIPS + 1)


def comm_chip_exchange(hb, hs):
    def body(hb_ref, hs_ref, qb_ref, qs_ref, send_sems, recv_sems):
        x, y, c = _mesh_pos()
        s = 2 * x + y
        me = (x, y, c)
        chips = _other_chips(x, y)
        rc = functools.partial(_remote, send_sems, recv_sems)
        cps = []
        for j, (cx, cy) in enumerate(chips):
            cps.append(rc(j, hb_ref.at[2 * cx + cy], qb_ref.at[s], (cx, cy, c)))
            cps.append(rc(3 + j, hs_ref, qs_ref.at[s], (cx, cy, c)))
        for cp in cps:
            cp.start()
        for j, (cx, cy) in enumerate(chips):
            blk = qb_ref.at[2 * cx + cy]
            rc(j, blk, blk, me).wait_recv()
            sb = qs_ref.at[2 * cx + cy]
            rc(3 + j, sb, sb, me).wait_recv()
        for cp in cps:
            cp.wait_send()

    out_shape = [jax.ShapeDtypeStruct(hb.shape, hb.dtype), jax.ShapeDtypeStruct((N_CHIPS, *hs.shape), hs.dtype)]
    return _comm_call(body, "comm_chip_exchange", [hb, hs], out_shape, 6)


def comm_pair_share(halves):
    n = len(halves)

    def body(*refs):
        x, y, c = _mesh_pos()
        send_sems, recv_sems = refs[2 * n], refs[2 * n + 1]
        cps = [_remote(send_sems, recv_sems, k, refs[k], refs[n + k], (x, y, 1 - c)) for k in range(n)]
        for cp in cps:
            cp.start()
        for cp in cps:
            cp.wait()

    out_shape = [jax.ShapeDtypeStruct(h.shape, h.dtype) for h in halves]
    return _comm_call(body, "comm_pair_share", list(halves), out_shape, n)


def add_pairs(a, b, out_dtype, *, name, tb):
    T, L = a.shape
    return _rowmap(lambda i, n, p, q: p + q, name=name, T=T, tb=tb, ins=[("row", a), ("row", b)],
                   outs=[("row", L, out_dtype)])[0]


def add_chips(parts, *, name, tb):
    T, L = parts[0].shape

    def fn(i, n, *ps):
        acc = ps[0].astype(F32)
        for p in ps[1:]:
            acc = acc + p.astype(F32)
        return acc

    return _rowmap(fn, name=name, T=T, tb=tb, ins=[("row", p) for p in parts], outs=[("row", L, F32)])[0]


def adamw(g, w, m, v, *, name, tb):
    T, L = g.shape

    def fn(i, n, g, w, m, v):
        m2 = ADAM_B1 * m + (1.0 - ADAM_B1) * g
        v2 = ADAM_B2 * v + (1.0 - ADAM_B2) * (g * g)
        m_hat = m2 / (1.0 - ADAM_B1 ** ADAM_STEP)
        v_hat = v2 / (1.0 - ADAM_B2 ** ADAM_STEP)
        return -ADAM_LR * (m_hat / (jnp.sqrt(v_hat) + ADAM_EPS) + ADAM_WD * w), m2, v2

    return _rowmap(fn, name=name, T=T, tb=tb, ins=[("row", z) for z in (g, w, m, v)], outs=[("row", L, F32)] * 3)


BIG_LANES = 1024
SMALL_LANES = 128


def _flat_cat(arrs, total, dtype):
    parts = [a.reshape(-1).astype(dtype) for a in arrs]
    n = sum(p.shape[0] for p in parts)
    assert n <= total, (n, total)
    if n < total:
        parts.append(jnp.zeros((total - n,), dtype))
    return jnp.concatenate(parts)


def _split_flat(flat, shapes):
    out, off = [], 0
    for shp in shapes:
        n = math.prod(shp)
        out.append(flat[off:off + n].reshape(shp))
        off += n
    return out


def _round_up(n, m):
    return -(-n // m) * m


def _full_shape(shard_shape, axis):
    return tuple(d * N_CHIPS if i == axis else d for i, d in enumerate(shard_shape))


def kernel(x, mem, attn_norm, a_w_in, a_mu, a_w0, a_w2, a_a0, a_a2, a_g2, a_k_k, a_k_a, a_r_k, a_lnx_w, a_lnx_b, a_w_out, kv_norm, kv_w, kv_k_norm, b_w_in, b_q_norm, b_w_out, mem_norm, mem_w_kv, mem_q_norm, mem_k_norm, ffn_norm, ffn_w_up, ffn_conv_w, ffn_conv_b, ffn_w_down, loss_target, m_attn_norm, m_a_w_in, m_a_mu, m_a_w0, m_a_w2, m_a_a0, m_a_a2, m_a_g2, m_a_k_k, m_a_k_a, m_a_r_k, m_a_lnx_w, m_a_lnx_b, m_a_w_out, m_kv_norm, m_kv_w, m_kv_k_norm, m_b_w_in, m_b_q_norm, m_b_w_out, m_mem_norm, m_mem_w_kv, m_mem_q_norm, m_mem_k_norm, m_ffn_norm, m_ffn_w_up, m_ffn_conv_w, m_ffn_conv_b, m_ffn_w_down, v_attn_norm, v_a_w_in, v_a_mu, v_a_w0, v_a_w2, v_a_a0, v_a_a2, v_a_g2, v_a_k_k, v_a_k_a, v_a_r_k, v_a_lnx_w, v_a_lnx_b, v_a_w_out, v_kv_norm, v_kv_w, v_kv_k_norm, v_b_w_in, v_b_q_norm, v_b_w_out, v_mem_norm, v_mem_w_kv, v_mem_q_norm, v_mem_k_norm, v_ffn_norm, v_ffn_w_up, v_ffn_conv_w, v_ffn_conv_b, v_ffn_w_down):
    args = (attn_norm, a_w_in, a_mu, a_w0, a_w2, a_a0, a_a2, a_g2, a_k_k, a_k_a, a_r_k, a_lnx_w, a_lnx_b, a_w_out, kv_norm, kv_w, kv_k_norm, b_w_in, b_q_norm, b_w_out, mem_norm, mem_w_kv, mem_q_norm, mem_k_norm, ffn_norm, ffn_w_up, ffn_conv_w, ffn_conv_b, ffn_w_down)
    ms = (m_attn_norm, m_a_w_in, m_a_mu, m_a_w0, m_a_w2, m_a_a0, m_a_a2, m_a_g2, m_a_k_k, m_a_k_a, m_a_r_k, m_a_lnx_w, m_a_lnx_b, m_a_w_out, m_kv_norm, m_kv_w, m_kv_k_norm, m_b_w_in, m_b_q_norm, m_b_w_out, m_mem_norm, m_mem_w_kv, m_mem_q_norm, m_mem_k_norm, m_ffn_norm, m_ffn_w_up, m_ffn_conv_w, m_ffn_conv_b, m_ffn_w_down)
    vs = (v_attn_norm, v_a_w_in, v_a_mu, v_a_w0, v_a_w2, v_a_a0, v_a_a2, v_a_g2, v_a_k_k, v_a_k_a, v_a_r_k, v_a_lnx_w, v_a_lnx_b, v_a_w_out, v_kv_norm, v_kv_w, v_kv_k_norm, v_b_w_in, v_b_q_norm, v_b_w_out, v_mem_norm, v_mem_w_kv, v_mem_q_norm, v_mem_k_norm, v_ffn_norm, v_ffn_w_up, v_ffn_conv_w, v_ffn_conv_b, v_ffn_w_down)
    w_sh, m_sh, v_sh = (dict(zip(WEIGHTS, z)) for z in (args, ms, vs))
    xi, yi, ci = _mesh_pos()
    chip = 2 * xi + yi
    axes = {**dict(BIG), **dict(SMALL_SHARDED)}
    early_names = [n for n, _ in BIG if n in EARLY_BIG]
    late_names = [n for n, _ in BIG if n not in EARLY_BIG and n != NATURAL_BIG]
    ss_names, ss_axes = [n for n, _ in SMALL_SHARDED], dict(SMALL_SHARDED)
    shapes_of = lambda names: [w_sh[n].shape for n in names]
    count = lambda names: sum(math.prod(s) for s in shapes_of(names))
    n_early, n_late = count(early_names), count(late_names)
    assert n_early % (2 * 16 * BIG_LANES) == 0 and n_late % (2 * 16 * BIG_LANES) == 0
    mh, mh_late = n_early // (2 * BIG_LANES), n_late // (2 * BIG_LANES)
    n_ss = _round_up(count(ss_names), 8 * SMALL_LANES)

    def shard_pack(names, total, dtype, source):
        return _flat_cat([source[n] for n in names], total, dtype)

    def unshard(names, gathered):
        per_chip = [_split_flat(gathered[j], shapes_of(names)) for j in range(N_CHIPS)]
        return {n: jnp.concatenate([per_chip[j][k] for j in range(N_CHIPS)], axis=axes[n]) for k, n in enumerate(names)}

    def by_chip(names, total, dtype, grads):
        parts = [jnp.split(grads[n], N_CHIPS, axis=axes[n]) for n in names]
        return jnp.stack([_flat_cat([p[j] for p in parts], total, dtype) for j in range(N_CHIPS)])

    wbig = shard_pack(early_names, n_early, MMD, w_sh).reshape(2, mh, BIG_LANES)
    wsm = shard_pack(ss_names, n_ss, F32, w_sh).reshape(-1, SMALL_LANES)
    wbig_all, wsm_all = comm_gather(wbig, wsm)
    wbig_all = lax.dynamic_update_index_in_dim(wbig_all, wbig, chip, 0).reshape(N_CHIPS, -1)
    wsm_all = lax.dynamic_update_index_in_dim(wsm_all, wsm, chip, 0).reshape(N_CHIPS, -1)
    W = {n: w_sh[n] for n in SMALL_REPL}
    W.update(unshard(early_names, wbig_all))
    W.update(unshard(ss_names, wsm_all))
    for n in ("a_w2", "a_a2", "a_g2"):
        W[n] = W[n].astype(MMD)
    wlate = shard_pack(late_names, n_late, MMD, w_sh).reshape(2, mh_late, BIG_LANES)
    nat_axis = axes[NATURAL_BIG]
    wnat = w_sh[NATURAL_BIG].astype(MMD)
    assert wnat.shape[0] == 2 and nat_axis != 0

    def unpack_late(gathered):
        full = lax.dynamic_update_index_in_dim(gathered[0], wlate, chip, 0)
        out = unshard(late_names, full.reshape(N_CHIPS, -1))
        nat = lax.dynamic_update_index_in_dim(gathered[1], wnat, chip, 0)
        out[NATURAL_BIG] = jnp.concatenate([nat[j] for j in range(N_CHIPS)], axis=nat_axis)
        return out

    def pack_late(grads):
        return [by_chip(late_names, n_late, BF16, grads).reshape(N_CHIPS, 2, mh_late, BIG_LANES),
                jnp.stack(jnp.split(grads[NATURAL_BIG].astype(BF16), N_CHIPS, axis=nat_axis))]

    loss_blk, grad_x, G, (received, pieces) = local_step(x[0], mem[0], loss_target[0], W,
                                                          late=([wlate, wnat], unpack_late, pack_late))
    loss = lax.psum(loss_blk[0, 0], ("x", "y", "c"))

    own_piece = lambda p: lax.dynamic_index_in_dim(lax.dynamic_index_in_dim(p, chip, 0, keepdims=False), ci, 0,
                                                   keepdims=False)
    gh_late = add_chips([received[0][k] for k in range(len(PEER_FLIPS))] + [own_piece(pieces[0])],
                        name="add_pieces_late", tb=32)
    gh_nat = add_chips([received[1][k] for k in range(len(PEER_FLIPS))] + [own_piece(pieces[1])],
                       name="add_pieces_natural", tb=32)
    gbig = by_chip(early_names, n_early, F32, G).reshape(N_CHIPS, 2, mh, BIG_LANES)
    sm_full_names = ss_names + list(SMALL_REPL)
    sm_full_shapes = [_full_shape(w_sh[n].shape, ss_axes[n]) for n in ss_names] + [w_sh[n].shape for n in SMALL_REPL]
    n_smf = _round_up(sum(math.prod(s) for s in sm_full_shapes), 2 * 8 * SMALL_LANES)
    msh = n_smf // (2 * SMALL_LANES)
    gsm = _flat_cat([G[n] for n in sm_full_names], n_smf, F32).reshape(2, msh, SMALL_LANES)
    rb, rs = comm_pair_exchange(gbig, gsm)
    mine_b = lax.dynamic_index_in_dim(gbig, ci, axis=1, keepdims=False)
    mine_s = lax.dynamic_index_in_dim(gsm, ci, axis=0, keepdims=False)
    hb = add_pairs(mine_b.reshape(-1, BIG_LANES), rb.reshape(-1, BIG_LANES), BF16, name="add_pairs_big", tb=128)
    hs = add_pairs(mine_s, rs, F32, name="add_pairs_small", tb=msh)
    hb = hb.reshape(N_CHIPS, mh, BIG_LANES)
    qb, qs = comm_chip_exchange(hb, hs)
    qb = lax.dynamic_update_index_in_dim(qb, lax.dynamic_index_in_dim(hb, chip, 0, keepdims=False), chip, 0)
    qs = lax.dynamic_update_index_in_dim(qs, hs, chip, 0)
    gh = add_chips([qb[j] for j in range(N_CHIPS)], name="add_chips_big", tb=32)
    gsh = add_chips([qs[j] for j in range(N_CHIPS)], name="add_chips_small", tb=msh)
    rh, rh_late, rh_nat, rsh = comm_pair_share([gh, gh_late, gh_nat, gsh])
    both = lambda mine_, theirs: jnp.where(ci == 0, jnp.stack([mine_, theirs]), jnp.stack([theirs, mine_]))
    gfull, gfull_late, gfull_nat, gsfull = both(gh, rh), both(gh_late, rh_late), both(gh_nat, rh_nat), both(gsh, rsh)

    res = {tag: {} for tag in ("grad", "delta", "new_m", "new_v")}
    big_grads = (list(zip(early_names, _split_flat(gfull.reshape(-1), shapes_of(early_names))))
                 + list(zip(late_names, _split_flat(gfull_late.reshape(-1), shapes_of(late_names))))
                 + [(NATURAL_BIG, gfull_nat)])
    for n, g in big_grads:
        shp = w_sh[n].shape
        rows = lambda z: z.reshape(-1, shp[-1])
        nrow = math.prod(shp[:-1])
        tb = next(t for t in (512, 256, 128, 64) if nrow % t == 0 and t * shp[-1] <= (1 << 19))
        outs = adamw(rows(g), rows(w_sh[n]), rows(m_sh[n]), rows(v_sh[n]), name=f"adamw_{n}", tb=tb)
        res["grad"][n] = g
        for tag, o in zip(("delta", "new_m", "new_v"), outs):
            res[tag][n] = o.reshape(shp)
    sm_full = dict(zip(sm_full_names, _split_flat(gsfull.reshape(-1), sm_full_shapes)))
    g_loc = {}
    for n in ss_names:
        size = w_sh[n].shape[ss_axes[n]]
        g_loc[n] = lax.dynamic_slice_in_dim(sm_full[n], chip * size, size, axis=ss_axes[n])
    for n in SMALL_REPL:
        g_loc[n] = sm_full[n]
    n_sml = _round_up(sum(math.prod(w_sh[n].shape) for n in sm_full_names), 8 * SMALL_LANES)
    pack_sm = lambda d: _flat_cat([d[n] for n in sm_full_names], n_sml, F32).reshape(-1, SMALL_LANES)
    d_sm, m_sm, v_sm = adamw(pack_sm(g_loc), pack_sm(w_sh), pack_sm(m_sh), pack_sm(v_sh), name="adamw_small",
                             tb=n_sml // SMALL_LANES)
    sm_loc_shapes = [w_sh[n].shape for n in sm_full_names]
    res["grad"].update(g_loc)
    for tag, smv in (("delta", d_sm), ("new_m", m_sm), ("new_v", v_sm)):
        res[tag].update(dict(zip(sm_full_names, _split_flat(smv.reshape(-1), sm_loc_shapes))))
    return (loss, grad_x[None], *[res[tag][n] for tag in ("grad", "delta", "new_m", "new_v") for n in WEIGHTS])
```

```python
import functools
import math

import numpy as np
import jax
import jax.numpy as jnp
from jax import lax
from jax.experimental import pallas as pl
from jax.experimental.pallas import tpu as pltpu

F32 = jnp.float32
BF16 = jnp.bfloat16
MMD = jnp.bfloat16

D_MODEL = 1024
HEAD_DIM = 64
N_MEM = 256
MEM_WIDTH = 256
RWKV_HEADS = 12
RWKV_WIDTH = 768
SHIFT_WIDTH = 2560
LORA_WIDTH = 256
DIL_WIDTH = 768
DIL_GROUPS = ((128, 1), (512, 4), (2048, 16))
DIL_BLOCK = 128
D_FF = 2816
ROPE_THETA = 10000.0
RMS_EPS = 1e-6
LNX_EPS = 64e-5
NEG_INF = -1e30
ADAM_LR = 0.001
ADAM_B1 = 0.9
ADAM_B2 = 0.999
ADAM_EPS = 1e-08
ADAM_WD = 0.01
ADAM_STEP = 10
N_CHIPS = 4
MESH = pl.DeviceIdType.MESH
VMEM_LIMIT_MB = 56
SCAN_CHUNK = 64
SCAN_UNROLL = 16
SCAN_UNROLL_BWD = 8

BIG = (("a_w_in", 2), ("a_w_out", 1), ("kv_w", 1), ("b_w_in", 1), ("b_w_out", 2), ("mem_w_kv", 1),
       ("ffn_w_up", 2), ("ffn_w_down", 1))
EARLY_BIG = ("a_w_in",)
NATURAL_BIG = "ffn_w_up"
SMALL_SHARDED = (("a_mu", 1), ("a_w0", 1), ("a_w2", 2), ("a_a0", 1), ("a_a2", 2), ("a_g2", 2), ("a_k_k", 1),
                 ("a_k_a", 1), ("a_lnx_w", 1), ("a_lnx_b", 1), ("ffn_conv_w", 2))
SMALL_REPL = ("attn_norm", "a_r_k", "kv_norm", "kv_k_norm", "b_q_norm", "mem_norm", "mem_q_norm", "mem_k_norm",
              "ffn_norm", "ffn_conv_b")
WEIGHTS = ("attn_norm", "a_w_in", "a_mu", "a_w0", "a_w2", "a_a0", "a_a2", "a_g2", "a_k_k", "a_k_a", "a_r_k",
           "a_lnx_w", "a_lnx_b", "a_w_out", "kv_norm", "kv_w", "kv_k_norm", "b_w_in", "b_q_norm", "b_w_out",
           "mem_norm", "mem_w_kv", "mem_q_norm", "mem_k_norm", "ffn_norm", "ffn_w_up", "ffn_conv_w", "ffn_conv_b",
           "ffn_w_down")


def _cp(sem=None, **kw):
    return pltpu.CompilerParams(dimension_semantics=sem, vmem_limit_bytes=VMEM_LIMIT_MB << 20, **kw)


def _tile(n, cands=(512, 256, 128)):
    for c in cands:
        if n % c == 0:
            return c
    return n


def _mm(a, b, *, name, ta=False, tb=False, add=None, out_dtype=F32):
    K, M = a.shape if ta else a.shape[::-1]
    N = b.shape[0] if tb else b.shape[1]
    assert K == (b.shape[1] if tb else b.shape[0])
    tm, tn = _tile(M, (512, 256, 128) if ta else (1024, 512, 256, 128)), _tile(N, (512, 1408, 256, 128))
    bytes_of = lambda z: z.size * z.dtype.itemsize
    kept = 1 if bytes_of(b) + bytes_of(a) * (N // tn) < bytes_of(a) + bytes_of(b) * (M // tm) else 0
    mi, ni = ((lambda o, i: i), (lambda o, i: o)) if kept else ((lambda o, i: o), (lambda o, i: i))
    grid = (N // tn, M // tm) if kept else (M // tm, N // tn)
    a_blk, b_blk = ((K, tm) if ta else (tm, K)), ((tn, K) if tb else (K, tn))
    a_spec = pl.BlockSpec(a_blk, (lambda o, i: (0, mi(o, i))) if ta else (lambda o, i: (mi(o, i), 0)))
    b_spec = pl.BlockSpec(b_blk, (lambda o, i: (ni(o, i), 0)) if tb else (lambda o, i: (0, ni(o, i))))
    o_spec = pl.BlockSpec((tm, tn), lambda o, i: (mi(o, i), ni(o, i)))
    dn = (((0,) if ta else (1,), (1,) if tb else (0,)), ((), ()))
    has_add = add is not None
    cache = (b if kept else a).dtype != MMD

    def body(*refs):
        vals = [refs[0], refs[1]]
        o_ref = refs[2 + has_add]
        if cache:
            scr = refs[-1]

            @pl.when(pl.program_id(1) == 0)
            def _():
                scr[...] = vals[kept][...].astype(MMD)

            vals[kept] = scr
        acc = lax.dot_general(vals[0][...].astype(MMD), vals[1][...].astype(MMD), dn, preferred_element_type=F32)
        if has_add:
            acc = acc + refs[2][...]
        o_ref[...] = acc.astype(o_ref.dtype)

    ins = [a, b] + ([add] if has_add else [])
    specs = [a_spec, b_spec] + ([o_spec] if has_add else [])
    return pl.pallas_call(
        body, name=name, grid=grid, in_specs=specs, out_specs=o_spec,
        out_shape=jax.ShapeDtypeStruct((M, N), out_dtype),
        scratch_shapes=[pltpu.VMEM(b_blk if kept else a_blk, MMD)] if cache else [],
        compiler_params=_cp(("parallel", "arbitrary")),
    )(*ins)


def _rowmap(fn, *, name, T, tb, ins, outs, accs=()):
    nblk = T // tb
    assert T % tb == 0 and tb % 8 == 0
    in_specs, args = [], []
    for spec in ins:
        kind, arr = spec[0], spec[1]
        w, cb = (spec[2], spec[3]) if len(spec) > 2 else (arr.shape[-1], 0)
        if kind == "row":
            in_specs.append(pl.BlockSpec((tb, w), lambda i, cb=cb: (i, cb)))
        elif kind == "prev":
            in_specs.append(pl.BlockSpec((8, w), lambda i, cb=cb: (jnp.maximum(i * (tb // 8) - 1, 0), cb)))
        elif kind == "next":
            in_specs.append(pl.BlockSpec((8, w), lambda i, cb=cb: (jnp.minimum((i + 1) * (tb // 8), T // 8 - 1), cb)))
        elif kind == "const":
            in_specs.append(pl.BlockSpec(arr.shape, lambda i, nd=arr.ndim: (0,) * nd))
        else:
            raise ValueError(kind)
        args.append(arr)
    out_shape, out_specs = [], []
    for kind, w, dt in outs:
        out_shape.append(jax.ShapeDtypeStruct((T, w), dt))
        out_specs.append(pl.BlockSpec((tb, w), lambda i: (i, 0)))
    for shp, dt in accs:
        out_shape.append(jax.ShapeDtypeStruct(shp, dt))
        out_specs.append(pl.BlockSpec(shp, lambda i, nd=len(shp): (0,) * nd))
    n_in, n_out = len(ins), len(outs)

    def body(*refs):
        i = pl.program_id(0)
        vals = [r[...] for r in refs[:n_in]]
        res = fn(i, nblk, *vals)
        if not isinstance(res, (tuple, list)):
            res = (res,)
        assert len(res) == n_out + len(accs), (name, len(res))
        for r, v in zip(refs[n_in:n_in + n_out], res[:n_out]):
            r[...] = v.astype(r.dtype)
        acc_refs = refs[n_in + n_out:]
        if acc_refs:
            @pl.when(i == 0)
            def _():
                for r in acc_refs:
                    r[...] = jnp.zeros(r.shape, r.dtype)

            for r, v in zip(acc_refs, res[n_out:]):
                r[...] += v

    res = pl.pallas_call(
        body, name=name, grid=(nblk,), in_specs=in_specs, out_specs=out_specs, out_shape=out_shape,
        compiler_params=_cp(("arbitrary",)),
    )(*args)
    return res


def _row_pick(halo, r):
    rid = lax.broadcasted_iota(jnp.int32, halo.shape, 0)
    return jnp.sum(jnp.where(rid == r, halo, 0.0), axis=0, keepdims=True)


def _shift_down(x, row_before, is_first):
    rid = lax.broadcasted_iota(jnp.int32, x.shape, 0)
    first = jnp.where(is_first, 0.0, 1.0) * row_before
    return jnp.where(rid == 0, first, pltpu.roll(x, 1, axis=0))


def _shift_up(x, row_after, is_last):
    n = x.shape[0]
    rid = lax.broadcasted_iota(jnp.int32, x.shape, 0)
    last = jnp.where(is_last, 0.0, 1.0) * row_after
    return jnp.where(rid == n - 1, last, pltpu.roll(x, n - 1, axis=0))


def _dot(a, b, dn=(((1,), (0,)), ((), ()))):
    return lax.dot_general(a.astype(MMD), b.astype(MMD), dn, preferred_element_type=F32)


def _dot_nt(a, b):
    return _dot(a, b, (((1,), (1,)), ((), ())))


def _dot_tn(a, b):
    return _dot(a, b, (((0,), (0,)), ((), ())))


def _dot_exact01(x, g01):
    hi = x.astype(BF16)
    lo = (x - hi.astype(F32)).astype(BF16)
    dn = (((1,), (0,)), ((), ()))
    return (lax.dot_general(hi, g01, dn, preferred_element_type=F32)
            + lax.dot_general(lo, g01, dn, preferred_element_type=F32))


def _fold_heads(v, fold):
    return _row_pick(_dot_exact01(jnp.broadcast_to(v, (8, v.shape[1])), fold), 0)


def _fold_ones(width):
    idx = np.arange(width) % HEAD_DIM
    return jnp.asarray((idx[:, None] == np.arange(HEAD_DIM)[None, :]).astype(np.float32), BF16)


def _head_masks(width):
    idx = np.arange(width) // HEAD_DIM
    return jnp.asarray((idx[None, :] == np.arange(width // HEAD_DIM)[:, None]).astype(np.float32)[:, None, :], F32)


def _rms_stats(x):
    r = lax.rsqrt(jnp.mean(x * x, axis=-1, keepdims=True) + RMS_EPS)
    return r, x * r


def rms_fwd(x, gains, *, name):
    T, D = x.shape

    def fn(i, nblk, xb, *gs):
        _, xh = _rms_stats(xb)
        return tuple(xh * g for g in gs)

    return _rowmap(fn, name=name, T=T, tb=512, ins=[("row", x)] + [("const", g) for g in gains],
                   outs=[("row", D, MMD)] * len(gains))


def rms_bwd(x, gains, dhs, dres, *, name):
    T, D = x.shape
    n = len(gains)

    def fn(i, nblk, xb, dr, *rest):
        gs, ds = rest[:n], rest[n:]
        r, xh = _rms_stats(xb)
        dx = dr
        dgs = []
        for g, dh in zip(gs, ds):
            dgs.append(jnp.sum(dh * xh, axis=0, keepdims=True))
            dxh = dh * g
            dx = dx + r * (dxh - xh * jnp.mean(dxh * xh, axis=-1, keepdims=True))
        return (dx, *dgs)

    return _rowmap(fn, name=name, T=T, tb=512,
                   ins=[("row", x), ("row", dres)] + [("const", g) for g in gains] + [("row", d) for d in dhs],
                   outs=[("row", D, F32)], accs=[((1, D), F32)] * n)


def _segsum(x):
    first = lax.broadcasted_iota(jnp.int32, (x.shape[0], 128), 1) < HEAD_DIM
    outs = []
    for p in range(x.shape[1] // 128):
        xs = x[:, p * 128:(p + 1) * 128]
        lo = jnp.sum(jnp.where(first, xs, 0.0), axis=-1, keepdims=True)
        hi = jnp.sum(jnp.where(first, 0.0, xs), axis=-1, keepdims=True)
        outs.append(jnp.where(first, lo, hi))
    return jnp.concatenate(outs, axis=1)


def _pre1_common(i, ps, halo, mu, w0, a0, w2p, a2p, g2p, k_k, k_a):
    prev = _shift_down(ps, _row_pick(halo, 7), i == 0)
    xs = ps + (prev - ps) * mu
    lo = xs[:, 3 * RWKV_WIDTH:]
    tl, sl = jnp.tanh(lo), jax.nn.sigmoid(lo)
    dec = w0 + _dot(tl, w2p)
    ain = a0 + _dot(lo, a2p)
    g = _dot(sl, g2p)
    wl = -jax.nn.softplus(-dec) - 0.5
    w = jnp.exp(-jnp.exp(wl))
    a = jax.nn.sigmoid(ain)
    k = xs[:, RWKV_WIDTH:2 * RWKV_WIDTH]
    z = k * k_k
    nrm = jnp.sqrt(_segsum(z * z))
    kk = z / jnp.maximum(nrm, 1e-12)
    return prev, xs, lo, tl, sl, dec, wl, w, a, g, k, nrm, kk


def rwkv_pre_fwd(p, mu, w0, a0, w2p, a2p, g2p, k_k, k_a):
    T = p.shape[0]

    def fn(i, nblk, ps, halo, mu, w0, a0, w2p, a2p, g2p, k_k, k_a):
        _, xs, _, _, _, _, _, w, a, g, k, _, kk = _pre1_common(i, ps, halo, mu, w0, a0, w2p, a2p, g2p, k_k, k_a)
        W = RWKV_WIDTH
        return xs[:, :W], w, k * (1.0 + (a - 1.0) * k_a), xs[:, 2 * W:3 * W], kk, kk * a, g

    return _rowmap(fn, name="rwkv_pre_fwd", T=T, tb=256,
                   ins=[("row", p, SHIFT_WIDTH, 0), ("prev", p, SHIFT_WIDTH, 0)]
                   + [("const", c) for c in (mu, w0, a0, w2p, a2p, g2p, k_k, k_a)],
                   outs=[("row", RWKV_WIDTH, F32)] * 7)


def rwkv_pre_bwd(p, mu, w0, a0, w2p, a2p, g2p, k_k, k_a, drs, dw, dk2s, dvs, dkk, dkka, dg):
    T = p.shape[0]

    def fn(i, nblk, ps, halo, mu, w0, a0, w2p, a2p, g2p, k_k, k_a, dr0, dr1, dw, dk20, dk21, dv0, dv1, dkk, dkka, dg):
        prev, xs, lo, tl, sl, dec, wl, w, a, g, k, nrm, kk = _pre1_common(i, ps, halo, mu, w0, a0, w2p, a2p, g2p, k_k, k_a)
        dk2 = dk20 + dk21
        dkk_t = dkk + dkka * a
        proj = jnp.where(nrm > 1e-12, kk * _segsum(dkk_t * kk), 0.0)
        dz = (dkk_t - proj) / jnp.maximum(nrm, 1e-12)
        dk = dz * k_k + dk2 * (1.0 + (a - 1.0) * k_a)
        da = dkka * kk + dk2 * k * k_a
        ddec = dw * (-w * jnp.exp(wl)) * jax.nn.sigmoid(-dec)
        dain = da * a * (1.0 - a)
        dlo = (_dot_nt(ddec, w2p) * (1.0 - tl * tl) + _dot_nt(dain, a2p) + _dot_nt(dg, g2p) * sl * (1.0 - sl))
        dxs = jnp.concatenate([dr0 + dr1, dk, dv0 + dv1, dlo], axis=1)
        s = lambda z: jnp.sum(z, axis=0, keepdims=True)
        return (dxs, s(dxs * (prev - ps)), s(ddec), s(dain), _dot_tn(tl, ddec), _dot_tn(lo, dain), _dot_tn(sl, dg),
                s(dz * k), s(dk2 * k * (a - 1.0)))

    return _rowmap(fn, name="rwkv_pre_bwd", T=T, tb=128,
                   ins=[("row", p, SHIFT_WIDTH, 0), ("prev", p, SHIFT_WIDTH, 0)]
                   + [("const", c) for c in (mu, w0, a0, w2p, a2p, g2p, k_k, k_a)]
                   + [("row", c) for c in (*drs, dw, *dk2s, *dvs, dkk, dkka, dg)],
                   outs=[("row", SHIFT_WIDTH, F32)],
                   accs=[((1, SHIFT_WIDTH), F32), ((1, RWKV_WIDTH), F32), ((1, RWKV_WIDTH), F32)]
                   + [((LORA_WIDTH, RWKV_WIDTH), F32)] * 3 + [((1, RWKV_WIDTH), F32)] * 2)


def shift_bwd(dxs, mu, dq_mem):
    T = dxs.shape[0]

    def fn(i, nblk, d, halo, mu, dq):
        nxt = _shift_up(d, _row_pick(halo, 0), i == nblk - 1)
        return jnp.concatenate([d * (1.0 - mu) + nxt * mu, dq], axis=1)

    return _rowmap(fn, name="shift_bwd", T=T, tb=256,
                   ins=[("row", dxs), ("next", dxs), ("const", mu), ("row", dq_mem)],
                   outs=[("row", SHIFT_WIDTH + MEM_WIDTH, MMD)])[0]


N_PAIRS = RWKV_HEADS // 2


def _pair_consts():
    row = lax.broadcasted_iota(jnp.int32, (HEAD_DIM, 128), 0)
    lane = lax.broadcasted_iota(jnp.int32, (HEAD_DIM, 128), 1)
    eye2 = jnp.logical_or(lane == row, lane == row + HEAD_DIM).astype(F32)
    li = lax.broadcasted_iota(jnp.int32, (128, 128), 0) < HEAD_DIM
    lj = lax.broadcasted_iota(jnp.int32, (128, 128), 1) < HEAD_DIM
    return eye2, (li == lj).astype(BF16)


def _pair_sum(p, ones2):
    n, m, l = p.shape
    s = lax.dot_general(p.reshape(n * m, l).astype(BF16), ones2, (((1,), (0,)), ((), ())), preferred_element_type=F32)
    return s.reshape(n, m, l)


def _pair_rows(row):
    return jnp.stack([row[:, p * 128:(p + 1) * 128] for p in range(N_PAIRS)], axis=0)


def _pair_flat(rows):
    return jnp.concatenate([rows[p] for p in range(N_PAIRS)], axis=1)


def _split_bf16(v):
    hi = v.astype(BF16).astype(F32)
    return hi, v - hi


def _gather_copies(srcs, dsts, send_sems, recv_sems):
    x, y, c = _mesh_pos()
    s = 2 * x + y
    me, sibling = (x, y, c), (x, y, 1 - c)
    rc = functools.partial(_remote, send_sems, recv_sems)
    ici, land, fwd, arrived = [], [], [], []
    for b, (src, dst) in enumerate(zip(srcs, dsts)):
        for j, (cx, cy) in enumerate(_other_chips(x, y)):
            k = 6 * b + j
            ici.append(rc(k, src.at[c], dst.at[s, c], (cx, cy, c)))
            blk, blk2 = dst.at[2 * cx + cy, c], dst.at[2 * cx + cy, 1 - c]
            land.append(rc(k, blk, blk, me))
            fwd.append(rc(k + 3, blk, blk, sibling))
            arrived.append(rc(k + 3, blk2, blk2, me))
    return ici, land, fwd, arrived


def scan_fwd(r, w, k2, v, kk, kka, gather=None):
    T, W = r.shape
    tc = SCAN_CHUNK
    nchunk = T // tc
    seq = pl.BlockSpec((tc, W), lambda i: (i, 0))
    one_state = pl.BlockSpec((N_PAIRS, HEAD_DIM, 128), lambda i: (0, 0, 0))
    nb = 0 if gather is None else len(gather)

    def body(r_ref, w_ref, k2_ref, v_ref, kk_ref, kka_ref, *rest):
        if gather is None:
            y_ref, st_ref, fin_ref, s_scr, vhi_scr, vlo_scr = rest
        else:
            srcs, (y_ref, st_ref, fin_ref), dsts = rest[:nb], rest[nb:nb + 3], rest[nb + 3:2 * nb + 3]
            s_scr, vhi_scr, vlo_scr, send_sems, recv_sems = rest[2 * nb + 3:]
            ici, land, fwd, arrived = _gather_copies(srcs, dsts, send_sems, recv_sems)

            @pl.when(pl.program_id(0) == 0)
            def _():
                for cp in ici:
                    cp.start()

            @pl.when(pl.program_id(0) == nchunk // 2)
            def _():
                for a, f in zip(land, fwd):
                    a.wait_recv()
                    f.start()

        @pl.when(pl.program_id(0) == 0)
        def _():
            s_scr[...] = jnp.zeros(s_scr.shape, F32)

        vhi_scr[...], vlo_scr[...] = _split_bf16(v_ref[...])
        eye2, ones2 = _pair_consts()
        eye2b = eye2.astype(BF16)

        def step(t, carry):
            r_t, w_t, k2_t, kk_t, kka_t, vhi_t, vlo_t = (
                _pair_rows(ref[pl.ds(t, 1), :]) for ref in (r_ref, w_ref, k2_ref, kk_ref, kka_ref, vhi_scr, vlo_scr))
            S = s_scr[...]
            sa = -_pair_sum(S * kk_t, ones2)
            vb = _pair_sum(eye2b * vhi_t.astype(BF16), ones2) + _pair_sum(eye2b * vlo_t.astype(BF16), ones2)
            S2 = S * w_t + sa * kka_t + vb * k2_t
            y_ref[pl.ds(t, 1), :] = _pair_flat(jnp.sum(eye2 * _pair_sum(S2 * r_t, ones2), axis=1, keepdims=True))
            s_scr[...] = S2
            st_ref[t] = S
            return carry

        lax.fori_loop(0, tc, step, 0, unroll=SCAN_UNROLL)
        fin_ref[...] = s_scr[...]

        if gather is not None:
            @pl.when(pl.program_id(0) == nchunk - 1)
            def _():
                for a in arrived:
                    a.wait_recv()
                for cp in ici + fwd:
                    cp.wait_send()

    in_specs = [seq] * 6
    out_specs = [seq, pl.BlockSpec((tc, N_PAIRS, HEAD_DIM, 128), lambda i: (i, 0, 0, 0)), one_state]
    out_shape = [jax.ShapeDtypeStruct((T, W), F32), jax.ShapeDtypeStruct((T, N_PAIRS, HEAD_DIM, 128), F32),
                 jax.ShapeDtypeStruct((N_PAIRS, HEAD_DIM, 128), F32)]
    scratch = [pltpu.VMEM((N_PAIRS, HEAD_DIM, 128), F32), pltpu.VMEM((tc, W), F32), pltpu.VMEM((tc, W), F32)]
    args = [r, w, k2, v, kk, kka]
    if gather is not None:
        in_specs += [HBM_SPEC] * nb
        out_specs += [HBM_SPEC] * nb
        out_shape += [jax.ShapeDtypeStruct((N_CHIPS, *g.shape), g.dtype) for g in gather]
        scratch += [pltpu.SemaphoreType.DMA((6 * nb,)), pltpu.SemaphoreType.DMA((6 * nb,))]
        args += list(gather)
    return pl.pallas_call(
        body, name="rwkv_scan_fwd", grid=(nchunk,), in_specs=in_specs, out_specs=out_specs, out_shape=out_shape,
        scratch_shapes=scratch, compiler_params=_cp(("arbitrary",)),
    )(*args)


PEER_FLIPS = tuple((fx, fy, fc) for fx in (0, 1) for fy in (0, 1) for fc in (0, 1))[1:]


def scan_bwd(r, w, k2, v, kk, kka, states, final_state, dy, scatter=None):
    T, W = r.shape
    tc = SCAN_CHUNK
    nchunk = T // tc
    seq = pl.BlockSpec((tc, W), lambda i: (nchunk - 1 - i, 0))
    st_spec = pl.BlockSpec((tc, N_PAIRS, HEAD_DIM, 128), lambda i: (nchunk - 1 - i, 0, 0, 0))
    one_state = pl.BlockSpec((N_PAIRS, HEAD_DIM, 128), lambda i: (0, 0, 0))
    nb, npeer = (0 if scatter is None else len(scatter)), len(PEER_FLIPS)

    def body(r_ref, w_ref, k2_ref, v_ref, kk_ref, kka_ref, st_ref, fin_ref, dy_ref, *rest):
        if scatter is None:
            dr_ref, dw_ref, dk2_ref, dv_ref, dkk_ref, dkka_ref, ds_scr, sc_scr, vhi_scr, vlo_scr = rest
        else:
            srcs, dsts = rest[:nb], rest[nb + 6:2 * nb + 6]
            dr_ref, dw_ref, dk2_ref, dv_ref, dkk_ref, dkka_ref = rest[nb:nb + 6]
            ds_scr, sc_scr, vhi_scr, vlo_scr, send_sems, recv_sems = rest[2 * nb + 6:]
            x, y, c = _mesh_pos()
            copies = []
            for b, (src, dst) in enumerate(zip(srcs, dsts)):
                for k, (fx, fy, fc) in enumerate(PEER_FLIPS):
                    px, py, pc = (1 - x if fx else x), (1 - y if fy else y), (1 - c if fc else c)
                    copies.append(_remote(send_sems, recv_sems, npeer * b + k, src.at[2 * px + py, pc], dst.at[k],
                                          (px, py, pc)))

            @pl.when(pl.program_id(0) == 0)
            def _():
                for cp in copies:
                    cp.start()

        @pl.when(pl.program_id(0) == 0)
        def _():
            ds_scr[...] = jnp.zeros(ds_scr.shape, F32)
            sc_scr[...] = fin_ref[...]

        vhi_scr[...], vlo_scr[...] = _split_bf16(v_ref[...])
        eye2, ones2 = _pair_consts()
        eye2b = eye2.astype(BF16)
        colsum = lambda z: jnp.sum(z, axis=1, keepdims=True)

        def step(j, carry):
            t = tc - 1 - j
            r_t, w_t, k2_t, kk_t, kka_t, vhi_t, vlo_t, dy_t = (
                _pair_rows(ref[pl.ds(t, 1), :])
                for ref in (r_ref, w_ref, k2_ref, kk_ref, kka_ref, vhi_scr, vlo_scr, dy_ref))
            s_prev, s_cur = st_ref[t], sc_scr[...]
            dyb = _pair_sum(eye2b * dy_t.astype(BF16), ones2)
            vb = _pair_sum(eye2b * vhi_t.astype(BF16), ones2) + _pair_sum(eye2b * vlo_t.astype(BF16), ones2)
            sa = -_pair_sum(s_prev * kk_t, ones2)
            dS = ds_scr[...] + dyb * r_t
            dsa = _pair_sum(dS * kka_t, ones2)
            ds_scr[...] = dS * w_t - dsa * kk_t
            sc_scr[...] = s_prev
            for ref, val in zip((dr_ref, dw_ref, dk2_ref, dv_ref, dkk_ref, dkka_ref),
                                (s_cur * dyb, dS * s_prev, dS * vb, eye2 * _pair_sum(dS * k2_t, ones2),
                                 -(s_prev * dsa), dS * sa)):
                ref[pl.ds(t, 1), :] = _pair_flat(colsum(val))
            return carry

        lax.fori_loop(0, tc, step, 0, unroll=SCAN_UNROLL_BWD)

        if scatter is not None:
            @pl.when(pl.program_id(0) == nchunk - 1)
            def _():
                for cp in copies:
                    cp.wait()

    in_specs = [seq] * 6 + [st_spec, one_state, seq]
    out_specs = [seq] * 6
    out_shape = [jax.ShapeDtypeStruct((T, W), F32)] * 6
    scratch = [pltpu.VMEM((N_PAIRS, HEAD_DIM, 128), F32)] * 2 + [pltpu.VMEM((tc, W), F32)] * 2
    args = [r, w, k2, v, kk, kka, states, final_state, dy]
    if scatter is not None:
        in_specs += [HBM_SPEC] * nb
        out_specs += [HBM_SPEC] * nb
        out_shape += [jax.ShapeDtypeStruct((npeer, *s.shape[2:]), s.dtype) for s in scatter]
        scratch += [pltpu.SemaphoreType.DMA((npeer * nb,)), pltpu.SemaphoreType.DMA((npeer * nb,))]
        args += list(scatter)
    return pl.pallas_call(
        body, name="rwkv_scan_bwd", grid=(nchunk,), in_specs=in_specs, out_specs=out_specs, out_shape=out_shape,
        scratch_shapes=scratch, compiler_params=_cp(("arbitrary",)),
    )(*args)


def _mix_common(y, r, k2, v, lnx_w, lnx_b, r_k):
    yc = y - _segsum(y) * (1.0 / HEAD_DIM)
    rstd = lax.rsqrt(_segsum(yc * yc) * (1.0 / HEAD_DIM) + LNX_EPS)
    yhat = yc * rstd
    s = _segsum(r * k2 * r_k)
    return rstd, yhat, s, yhat * lnx_w + lnx_b + s * v


def mix_gate_fwd(y, r, k2, v, g, y_mem, lnx_w, lnx_b, r_k):
    T = y.shape[0]

    def fn(i, nblk, y, r, k2, v, g, ym, lw, lb, rk):
        mix = _mix_common(y, r, k2, v, lw, lb, rk)[3]
        return jnp.concatenate([mix * g, ym], axis=1)

    return _rowmap(fn, name="mix_gate_fwd", T=T, tb=256,
                   ins=[("row", z) for z in (y, r, k2, v, g, y_mem)] + [("const", c) for c in (lnx_w, lnx_b, r_k)],
                   outs=[("row", RWKV_WIDTH + MEM_WIDTH, MMD)])[0]


def mix_gate_bwd(y, r, k2, v, g, dycat, lnx_w, lnx_b, r_k):
    T = y.shape[0]

    def fn(i, nblk, y, r, k2, v, g, dyc, lw, lb, rk):
        rstd, yhat, s, mix = _mix_common(y, r, k2, v, lw, lb, rk)
        dmix = dyc * g
        dyh = dmix * lw
        inv = 1.0 / HEAD_DIM
        dy = rstd * (dyh - _segsum(dyh) * inv - yhat * (_segsum(dyh * yhat) * inv))
        ds = _segsum(dmix * v)
        cs = lambda z: jnp.sum(z, axis=0, keepdims=True)
        return (dy, ds * k2 * rk, ds * r * rk, dmix * s, dyc * mix, cs(dmix * yhat), cs(dmix), cs(ds * r * k2))

    return _rowmap(fn, name="mix_gate_bwd", T=T, tb=256,
                   ins=[("row", z) for z in (y, r, k2, v, g)] + [("row", dycat, RWKV_WIDTH, 0)]
                   + [("const", c) for c in (lnx_w, lnx_b, r_k)],
                   outs=[("row", RWKV_WIDTH, F32)] * 5, accs=[((1, RWKV_WIDTH), F32)] * 3)


def _head_rms(x):
    ms = _segsum(x * x) * (1.0 / HEAD_DIM)
    r = lax.rsqrt(ms + RMS_EPS)
    return r, x * r


def _head_rms_bwd(dxn_g, r, xh):
    return r * (dxn_g - xh * (_segsum(dxn_g * xh) * (1.0 / HEAD_DIM)))


def mem_kv_fwd(mem, norm_g, w_kv, k_norm_t, *, name):
    def body(mem_ref, g_ref, w_ref, kn_ref, k_out, v_out):
        _, xh = _rms_stats(mem_ref[...])
        kv = _dot(xh * g_ref[...], w_ref[...])
        _, kh = _head_rms(kv[:, :MEM_WIDTH])
        k_out[...] = kh * kn_ref[...]
        v_out[...] = kv[:, MEM_WIDTH:]

    return pl.pallas_call(
        body, name=name, out_shape=[jax.ShapeDtypeStruct((N_MEM, MEM_WIDTH), F32)] * 2, compiler_params=_cp(),
    )(mem, norm_g, w_kv, k_norm_t)


def mem_kv_bwd(mem, norm_g, w_kv, k_norm_t, dkn, dv, *, name):
    fold = _fold_ones(MEM_WIDTH)

    def body(mem_ref, g_ref, w_ref, kn_ref, fo_ref, dkn_ref, dv_ref, dw_out, dg_out, dkg_out):
        _, xh = _rms_stats(mem_ref[...])
        hm = xh * g_ref[...]
        kv = _dot(hm, w_ref[...])
        r, kh = _head_rms(kv[:, :MEM_WIDTH])
        dkn = dkn_ref[...]
        dkg_out[...] = _fold_heads(jnp.sum(dkn * kh, axis=0, keepdims=True), fo_ref[...])
        dkraw = _head_rms_bwd(dkn * kn_ref[...], r, kh)
        dkv = jnp.concatenate([dkraw, dv_ref[...]], axis=1)
        dw_out[...] = _dot_tn(hm, dkv)
        dg_out[...] = jnp.sum(_dot_nt(dkv, w_ref[...]) * xh, axis=0, keepdims=True)

    return pl.pallas_call(
        body, name=name,
        out_shape=[jax.ShapeDtypeStruct((D_MODEL, 2 * MEM_WIDTH), F32), jax.ShapeDtypeStruct((1, D_MODEL), F32),
                   jax.ShapeDtypeStruct((1, HEAD_DIM), F32)],
        compiler_params=_cp(),
    )(mem, norm_g, w_kv, k_norm_t, fold, dkn, dv)


def _mem_scores(qn, kn, masks, h):
    s = _dot_nt(qn * masks[h], kn) * (1.0 / math.sqrt(HEAD_DIM))
    s = s - jnp.max(s, axis=-1, keepdims=True)
    e = jnp.exp(s)
    return e / jnp.sum(e, axis=-1, keepdims=True)


def mem_attn_fwd(p, colblock, kn, v, q_norm_t, *, name):
    T = p.shape[0]
    masks = _head_masks(MEM_WIDTH)

    def fn(i, nblk, q, kn, v, qg, masks):
        _, qh = _head_rms(q)
        qn = qh * qg
        out = jnp.zeros(q.shape, F32)
        for h in range(MEM_WIDTH // HEAD_DIM):
            out = out + _dot(_mem_scores(qn, kn, masks, h), v * masks[h])
        return out

    return _rowmap(fn, name=name, T=T, tb=512,
                   ins=[("row", p, MEM_WIDTH, colblock)] + [("const", c) for c in (kn, v, q_norm_t, masks)],
                   outs=[("row", MEM_WIDTH, F32)])[0]


def mem_attn_bwd(p, colblock, kn, v, q_norm_t, dycat, dcolblock, *, name):
    T = p.shape[0]
    masks, fold = _head_masks(MEM_WIDTH), _fold_ones(MEM_WIDTH)
    scale = 1.0 / math.sqrt(HEAD_DIM)

    def fn(i, nblk, q, dy, kn, v, qg, masks, fo):
        r, qh = _head_rms(q)
        qn = qh * qg
        dqn = jnp.zeros(q.shape, F32)
        dkn = jnp.zeros(kn.shape, F32)
        dv = jnp.zeros(v.shape, F32)
        for h in range(MEM_WIDTH // HEAD_DIM):
            pr = _mem_scores(qn, kn, masks, h)
            dyh = dy * masks[h]
            dpr = _dot_nt(dyh, v)
            ds = pr * (dpr - jnp.sum(dpr * pr, axis=-1, keepdims=True)) * scale
            dqn = dqn + _dot(ds, kn * masks[h])
            dkn = dkn + _dot_tn(ds, qn * masks[h])
            dv = dv + _dot_tn(pr, dyh)
        dqg = _fold_heads(jnp.sum(dqn * qh, axis=0, keepdims=True), fo)
        return _head_rms_bwd(dqn * qg, r, qh), dkn, dv, dqg

    return _rowmap(fn, name=name, T=T, tb=512,
                   ins=[("row", p, MEM_WIDTH, colblock), ("row", dycat, MEM_WIDTH, dcolblock)]
                   + [("const", c) for c in (kn, v, q_norm_t, masks, fold)],
                   outs=[("row", MEM_WIDTH, F32)],
                   accs=[((N_MEM, MEM_WIDTH), F32), ((N_MEM, MEM_WIDTH), F32), ((1, HEAD_DIM), F32)])


def _ffn_conv(i, u, halo, cw, cb):
    up1 = _shift_down(u, _row_pick(halo, 7), i == 0)
    up2 = _shift_down(up1, _row_pick(halo, 6), i == 0)
    c = cb + cw[0] * up2 + cw[1] * up1 + cw[2] * u
    return up1, up2, c[:, :D_FF], c[:, D_FF:]


def ffn_act_fwd(u, cw, cb, *, name):
    T = u.shape[0]

    def fn(i, nblk, u, halo, c0, c1, c2, cb):
        _, _, gate, val = _ffn_conv(i, u, halo, (c0, c1, c2), cb)
        return jax.nn.silu(gate) * val

    return _rowmap(fn, name=name, T=T, tb=128, ins=[("row", u), ("prev", u)] + [("const", c) for c in (*cw, cb)],
                   outs=[("row", D_FF, MMD)])[0]


def ffn_act_bwd(u, cw, cb, dz, *, name):
    T = u.shape[0]
    tb = 128

    def fn(i, nblk, u, halo, unext, c0, c1, c2, cb, dz, dznext):
        ue = jnp.concatenate([u, unext], axis=0)
        dze = jnp.concatenate([dz, jnp.where(i == nblk - 1, 0.0, 1.0) * dznext], axis=0)
        up1, up2, gate, val = _ffn_conv(i, ue, halo, (c0, c1, c2), cb)
        sg = jax.nn.sigmoid(gate)
        dce = jnp.concatenate([dze * val * sg * (1.0 + gate * (1.0 - sg)), dze * gate * sg], axis=1)
        rows = dce.shape[0]
        du = (c2 * dce + c1 * pltpu.roll(dce, rows - 1, axis=0) + c0 * pltpu.roll(dce, rows - 2, axis=0))[:tb]
        dc = dce[:tb]
        s = lambda z: jnp.sum(z, axis=0, keepdims=True)
        return du, s(dc * up2[:tb]), s(dc * up1[:tb]), s(dc * u), s(dc)

    return _rowmap(fn, name=name, T=T, tb=tb,
                   ins=[("row", u), ("prev", u), ("next", u)] + [("const", c) for c in (*cw, cb)]
                   + [("row", dz), ("next", dz)],
                   outs=[("row", 2 * D_FF, MMD)], accs=[((1, 2 * D_FF), F32)] * 4)


def _rope_swap(z):
    lane = lax.broadcasted_iota(jnp.int32, z.shape, 1) % HEAD_DIM
    w = z.shape[1]
    return jnp.where(lane < HEAD_DIM // 2, pltpu.roll(z, w - HEAD_DIM // 2, axis=1), pltpu.roll(z, HEAD_DIM // 2, axis=1))


def rope_tables(T):
    inv = (np.float32(ROPE_THETA) ** (-np.arange(0, HEAD_DIM, 2, dtype=np.float32) / np.float32(HEAD_DIM))).astype(np.float32)
    ang = (np.arange(T, dtype=np.float32)[:, None] * inv[None, :]).astype(np.float64)
    cos, sin = np.cos(ang).astype(np.float32), np.sin(ang).astype(np.float32)
    return (jnp.asarray(np.concatenate([cos, cos, cos, cos], axis=1)),
            jnp.asarray(np.concatenate([-sin, sin, -sin, sin], axis=1)))


def _rope_wide(t):
    return jnp.tile(t, (1, DIL_WIDTH // t.shape[1]))


def qk_fwd(kvp, pb, kg_t, qg_t, cos, sin):
    T = kvp.shape[0]

    def fn(i, nblk, kraw, vraw, qraw, kg, qg, c, s):
        c, s = _rope_wide(c), _rope_wide(s)
        outs = []
        for raw, g in ((qraw, qg), (kraw, kg)):
            _, xh = _head_rms(raw)
            z = xh * g
            outs.append(z * c + _rope_swap(z) * s)
        return outs[0], outs[1], vraw

    return _rowmap(fn, name="qk_fwd", T=T, tb=256,
                   ins=[("row", kvp, DIL_WIDTH, 0), ("row", kvp, DIL_WIDTH, 1), ("row", pb, DIL_WIDTH, 0)]
                   + [("const", kg_t), ("const", qg_t), ("row", cos), ("row", sin)],
                   outs=[("row", DIL_WIDTH, MMD)] * 3)


def qk_bwd(kvp, pb, kg_t, qg_t, cos, sin, dq, dk, dv, dq_mem):
    T = kvp.shape[0]
    fold = _fold_ones(DIL_WIDTH)

    def fn(i, nblk, kraw, qraw, kg, qg, c, s, fo, dq, dk, dv, dqm):
        c, s = _rope_wide(c), _rope_wide(s)
        res, dgs = [], []
        for raw, g, d in ((qraw, qg, dq), (kraw, kg, dk)):
            r, xh = _head_rms(raw)
            dz = d * c + _rope_swap(d * s)
            dgs.append(_fold_heads(jnp.sum(dz * xh, axis=0, keepdims=True), fo))
            res.append(_head_rms_bwd(dz * g, r, xh))
        return (jnp.concatenate([res[0], dqm], axis=1), jnp.concatenate([res[1], dv], axis=1), dgs[0], dgs[1])

    return _rowmap(fn, name="qk_bwd", T=T, tb=256,
                   ins=[("row", kvp, DIL_WIDTH, 0), ("row", pb, DIL_WIDTH, 0), ("const", kg_t), ("const", qg_t),
                        ("row", cos), ("row", sin), ("const", fold),
                        ("row", dq), ("row", dk), ("row", dv), ("row", dq_mem)],
                   outs=[("row", DIL_WIDTH + MEM_WIDTH, MMD), ("row", 2 * DIL_WIDTH, MMD)],
                   accs=[((1, HEAD_DIM), F32)] * 2)


def _band(kind):
    i = lax.broadcasted_iota(jnp.int32, (DIL_BLOCK, DIL_BLOCK), 0)
    j = lax.broadcasted_iota(jnp.int32, (DIL_BLOCK, DIL_BLOCK), 1)
    return (j <= i) if kind == "cur" else (j >= i)


def dil_attn_fwd(q, k, v, seq_blocks, *, name):
    T, W = q.shape
    nb = T // DIL_BLOCK
    masks = _head_masks(W)
    cur = pl.BlockSpec((DIL_BLOCK, W), lambda n: (n, 0))
    prv = pl.BlockSpec((DIL_BLOCK, W), lambda n: (jnp.maximum(n - 1, 0), 0))
    scale = 1.0 / math.sqrt(HEAD_DIM)

    def body(q_ref, kc_ref, kp_ref, vc_ref, vp_ref, m_ref, o_ref, l_ref):
        n = pl.program_id(0)
        has_prev = (n % seq_blocks) != 0
        q = q_ref[...].astype(F32)
        kc, kp = kc_ref[...].astype(F32), kp_ref[...].astype(F32)
        vc, vp = vc_ref[...].astype(F32), vp_ref[...].astype(F32)
        ok_c = _band("cur")
        ok_p = jnp.logical_and(_band("prev"), has_prev)
        o = jnp.zeros((DIL_BLOCK, W), F32)
        lse = jnp.zeros((DIL_BLOCK, W), F32)
        for h in range(W // HEAD_DIM):
            mh = m_ref[h]
            qh = q * mh
            sc = jnp.where(ok_c, _dot_nt(qh, kc) * scale, NEG_INF)
            sp = jnp.where(ok_p, _dot_nt(qh, kp) * scale, NEG_INF)
            mx = jnp.maximum(jnp.max(sc, axis=-1, keepdims=True), jnp.max(sp, axis=-1, keepdims=True))
            ec, ep = jnp.exp(sc - mx), jnp.exp(sp - mx)
            den = jnp.sum(ec, axis=-1, keepdims=True) + jnp.sum(ep, axis=-1, keepdims=True)
            o = o + (_dot(ec, vc * mh) + _dot(ep, vp * mh)) / den
            lse = lse + (mx + jnp.log(den)) * mh
        o_ref[...] = o
        l_ref[...] = lse

    return pl.pallas_call(
        body, name=name, grid=(nb,), in_specs=[cur, cur, prv, cur, prv, pl.BlockSpec(masks.shape, lambda n: (0, 0, 0))],
        out_specs=[cur, cur], out_shape=[jax.ShapeDtypeStruct((T, W), F32)] * 2,
        compiler_params=_cp(("parallel",)),
    )(q, k, k, v, v, masks)


def dil_attn_bwd(q, k, v, o, lse, do, dlse, seq_blocks, *, name):
    T, W = q.shape
    nb = T // DIL_BLOCK
    masks = _head_masks(W)
    cur = pl.BlockSpec((DIL_BLOCK, W), lambda n: (n, 0))
    prv = pl.BlockSpec((DIL_BLOCK, W), lambda n: (jnp.maximum(n - 1, 0), 0))
    nxt = pl.BlockSpec((DIL_BLOCK, W), lambda n: (jnp.minimum(n + 1, nb - 1), 0))
    scale = 1.0 / math.sqrt(HEAD_DIM)

    def body(qc_ref, qn_ref, kc_ref, kp_ref, vc_ref, vp_ref, oc_ref, on_ref, lc_ref, ln_ref, doc_ref, don_ref,
             dlc_ref, dln_ref, m_ref, dq_ref, dk_ref, dv_ref):
        n = pl.program_id(0)
        has_prev = (n % seq_blocks) != 0
        has_next = jnp.logical_and(((n + 1) % seq_blocks) != 0, n + 1 < nb)
        f = lambda ref: ref[...].astype(F32)
        qc, qn, kc, kp, vc, vp = f(qc_ref), f(qn_ref), f(kc_ref), f(kp_ref), f(vc_ref), f(vp_ref)
        doc, don = doc_ref[...], don_ref[...]
        ok_c = _band("cur")
        ok_p = jnp.logical_and(_band("prev"), has_prev)
        ok_n = jnp.logical_and(_band("prev"), has_next)
        dq = jnp.zeros((DIL_BLOCK, W), F32)
        dk = jnp.zeros((DIL_BLOCK, W), F32)
        dv = jnp.zeros((DIL_BLOCK, W), F32)

        def side(qh, kk, vv, doh, lse_h, corr, ok):
            s = _dot_nt(qh, kk) * scale
            pr = jnp.where(ok, jnp.exp(jnp.where(ok, s, NEG_INF) - lse_h), 0.0)
            ds = pr * (_dot_nt(doh, vv) + corr) * scale
            return pr, ds

        for h in range(W // HEAD_DIM):
            mh = m_ref[h]
            red = lambda z: jnp.sum(z * mh, axis=-1, keepdims=True)
            qh, doh = qc * mh, doc * mh
            lse_h = red(lc_ref[...]) * (1.0 / HEAD_DIM)
            corr = red(dlc_ref[...]) - red(doc * oc_ref[...])
            pr_c, ds_c = side(qh, kc, vc * mh, doh, lse_h, corr, ok_c)
            _, ds_p = side(qh, kp, vp * mh, doh, lse_h, corr, ok_p)
            dq = dq + _dot(ds_c, kc * mh) + _dot(ds_p, kp * mh)
            dk = dk + _dot_tn(ds_c, qh)
            dv = dv + _dot_tn(pr_c, doh)
            qh2, doh2 = qn * mh, don * mh
            lse_2 = red(ln_ref[...]) * (1.0 / HEAD_DIM)
            corr2 = red(dln_ref[...]) - red(don * on_ref[...])
            pr_n, ds_n = side(qh2, kc, vc * mh, doh2, lse_2, corr2, ok_n)
            dk = dk + _dot_tn(ds_n, qh2)
            dv = dv + _dot_tn(pr_n, doh2)
        dq_ref[...] = dq
        dk_ref[...] = dk
        dv_ref[...] = dv

    return pl.pallas_call(
        body, name=name, grid=(nb,),
        in_specs=[cur, nxt, cur, prv, cur, prv, cur, nxt, cur, nxt, cur, nxt, cur, nxt,
                  pl.BlockSpec(masks.shape, lambda n: (0, 0, 0))],
        out_specs=[cur] * 3, out_shape=[jax.ShapeDtypeStruct((T, W), F32)] * 3,
        compiler_params=_cp(("parallel",)),
    )(q, q, k, k, v, v, o, o, lse, lse, do, do, dlse, dlse, masks)


def _mix_weights(ls):
    m = jnp.maximum(jnp.maximum(ls[0], ls[1]), ls[2])
    es = [jnp.exp(l - m) for l in ls]
    den = es[0] + es[1] + es[2]
    return [e / den for e in es]


def mix_fwd(os_, ls, y_mem):
    T = y_mem.shape[0]

    def fn(i, nblk, o0, o1, o2, l0, l1, l2, ym):
        w = _mix_weights((l0, l1, l2))
        return jnp.concatenate([w[0] * o0 + w[1] * o1 + w[2] * o2, ym], axis=1)

    return _rowmap(fn, name="mix_fwd", T=T, tb=512, ins=[("row", z) for z in (*os_, *ls, y_mem)],
                   outs=[("row", 2 * MEM_WIDTH, MMD)])[0]


def mix_bwd(os_, ls, dycat):
    T = dycat.shape[0]

    def fn(i, nblk, o0, o1, o2, l0, l1, l2, dy):
        w = _mix_weights((l0, l1, l2))
        os3 = (o0, o1, o2)
        dws = [dy * o for o in os3]
        tot = w[0] * dws[0] + w[1] * dws[1] + w[2] * dws[2]
        return tuple(wg * dy for wg in w) + tuple(wg * (dw - tot) for wg, dw in zip(w, dws))

    return _rowmap(fn, name="mix_bwd", T=T, tb=512,
                   ins=[("row", z) for z in (*os_, *ls)] + [("row", dycat, MEM_WIDTH, 0)],
                   outs=[("row", MEM_WIDTH, F32)] * 6)


def loss_fwd_bwd(y, target):
    T, D = y.shape

    def fn(i, nblk, y, t):
        e = y - t
        return e * (1.0 / D), jnp.zeros((8, 128), F32) + jnp.sum(e * e) * (0.5 / D)

    return _rowmap(fn, name="loss", T=T, tb=512, ins=[("row", y), ("row", target)], outs=[("row", D, F32)],
                   accs=[((8, 128), F32)])


def _to_residues(z, dil):
    T, W = z.shape
    return z.reshape(T // dil, dil, W).transpose(1, 0, 2).reshape(T, W)


def _from_residues(z, dil):
    T, W = z.shape
    return z.reshape(dil, T // dil, W).transpose(1, 0, 2).reshape(T, W)


def _pad_rows(w, rows):
    return jnp.concatenate([w, jnp.zeros((rows - w.shape[0], w.shape[1]), w.dtype)], axis=0)


def _tile_heads(g, width):
    return jnp.tile(g.reshape(1, HEAD_DIM), (1, width // HEAD_DIM))


def _conv_rows(W, i):
    return [W["ffn_conv_w"][i][j:j + 1] for j in range(3)]


def _ffn_fwd(x, i, W):
    hn = rms_fwd(x, [W["ffn_norm"][i:i + 1]], name=f"ffn_rms{i}")[0]
    u = _mm(hn, W["ffn_w_up"][i], name=f"ffn_up{i}")
    z = ffn_act_fwd(u, _conv_rows(W, i), W["ffn_conv_b"][i:i + 1], name=f"ffn_act{i}")
    out = _mm(z, W["ffn_w_down"][i], add=x, name=f"ffn_down{i}")
    return out, (x, hn, u, z)


def _ffn_bwd(dout, i, W, saved, G):
    x, hn, u, z = saved
    dz = _mm(dout, W["ffn_w_down"][i], tb=True, name=f"ffn_down_dx{i}")
    G["ffn_w_down"][i] = _mm(z, dout, ta=True, name=f"ffn_down_dw{i}")
    du, dw0, dw1, dw2, db = ffn_act_bwd(u, _conv_rows(W, i), W["ffn_conv_b"][i:i + 1], dz, name=f"ffn_act_bwd{i}")
    G["ffn_conv_w"][i] = jnp.concatenate([dw0, dw1, dw2], axis=0)
    G["ffn_conv_b"][i] = db[0]
    dhn = _mm(du, W["ffn_w_up"][i], tb=True, name=f"ffn_up_dx{i}")
    G["ffn_w_up"][i] = _mm(hn, du, ta=True, name=f"ffn_up_dw{i}")
    dx, dg = rms_bwd(x, [W["ffn_norm"][i:i + 1]], [dhn], dout, name=f"ffn_rms_bwd{i}")
    G["ffn_norm"][i] = dg[0]
    return dx


def local_step(x, mem, target, W, late=None):
    T = x.shape[0]
    W = dict(W)
    G = {"ffn_w_down": [None, None], "ffn_w_up": [None, None], "ffn_conv_w": [None, None],
         "ffn_conv_b": [None, None], "ffn_norm": [None, None], "attn_norm": [None, None], "mem_norm": [None, None],
         "mem_w_kv": [None, None], "mem_q_norm": [None, None], "mem_k_norm": [None, None]}
    mu, w0, a0 = W["a_mu"], W["a_w0"], W["a_a0"]
    w2p, a2p, g2p = (_pad_rows(W["a_w2"][0], LORA_WIDTH),
                     jnp.concatenate([jnp.zeros((64, RWKV_WIDTH), MMD), W["a_a2"][0],
                                      jnp.zeros((128, RWKV_WIDTH), MMD)], axis=0),
                     jnp.concatenate([jnp.zeros((128, RWKV_WIDTH), MMD), W["a_g2"][0]], axis=0))
    k_k, k_a, lnx_w, lnx_b = W["a_k_k"], W["a_k_a"], W["a_lnx_w"], W["a_lnx_b"]
    r_k = W["a_r_k"].reshape(1, RWKV_WIDTH)

    h0 = rms_fwd(x, [W["attn_norm"][0:1]], name="attn_rms0")[0]
    p = _mm(h0, W["a_w_in"][0], name="a_in")
    r, w, k2, v, kk, kka, g = rwkv_pre_fwd(p, mu, w0, a0, w2p, a2p, g2p, k_k, k_a)
    if late is None:
        y, states, final_state = scan_fwd(r, w, k2, v, kk, kka)
    else:
        y, states, final_state, *gathered = scan_fwd(r, w, k2, v, kk, kka, gather=late[0])
        W.update(late[1](gathered))
    memkv = []
    for i in range(2):
        memkv.append(mem_kv_fwd(mem, W["mem_norm"][i:i + 1], W["mem_w_kv"][i], _tile_heads(W["mem_k_norm"][i], MEM_WIDTH),
                                name=f"mem_kv{i}"))
    qg0 = _tile_heads(W["mem_q_norm"][0], MEM_WIDTH)
    y_mem0 = mem_attn_fwd(p, SHIFT_WIDTH // MEM_WIDTH, memkv[0][0], memkv[0][1], qg0, name="mem_attn0")
    ycat0 = mix_gate_fwd(y, r, k2, v, g, y_mem0, lnx_w, lnx_b, r_k)
    x1 = _mm(ycat0, W["a_w_out"][0], add=x, name="a_out")
    x2, ffn0 = _ffn_fwd(x1, 0, W)

    h1, hkv = rms_fwd(x2, [W["attn_norm"][1:2], W["kv_norm"].reshape(1, -1)], name="attn_rms1")
    kvp = _mm(hkv, W["kv_w"], name="kv_in")
    pb = _mm(h1, W["b_w_in"][0], name="b_in")
    cos, sin = rope_tables(T)
    kg_t, qg_t = _tile_heads(W["kv_k_norm"], DIL_WIDTH), _tile_heads(W["b_q_norm"][0], DIL_WIDTH)
    q, ksh, vsh = qk_fwd(kvp, pb, kg_t, qg_t, cos, sin)
    os_, ls, grp = [], [], []
    for gi, (win, dil) in enumerate(DIL_GROUPS):
        sl = slice(gi * MEM_WIDTH, (gi + 1) * MEM_WIDTH)
        qg_, kg_, vg_ = (_to_residues(z[:, sl], dil) for z in (q, ksh, vsh))
        o_r, l_r = dil_attn_fwd(qg_, kg_, vg_, T // dil // DIL_BLOCK, name=f"dil_fwd{gi}")
        grp.append((qg_, kg_, vg_, o_r, l_r))
        os_.append(_from_residues(o_r, dil))
        ls.append(_from_residues(l_r, dil))
    qg1 = _tile_heads(W["mem_q_norm"][1], MEM_WIDTH)
    y_mem1 = mem_attn_fwd(pb, DIL_WIDTH // MEM_WIDTH, memkv[1][0], memkv[1][1], qg1, name="mem_attn1")
    ycat1 = mix_fwd(os_, ls, y_mem1)
    x3 = _mm(ycat1, W["b_w_out"][0], add=x2, name="b_out")
    x4, ffn1 = _ffn_fwd(x3, 1, W)

    dx4, loss = loss_fwd_bwd(x4, target)

    dx3 = _ffn_bwd(dx4, 1, W, ffn1, G)
    dycat1 = _mm(dx3, W["b_w_out"][0], tb=True, name="b_out_dx")
    G["b_w_out"] = _mm(ycat1, dx3, ta=True, name="b_out_dw")[None]
    dq_mem1, dkn1, dvm1, dqg1 = mem_attn_bwd(pb, DIL_WIDTH // MEM_WIDTH, memkv[1][0], memkv[1][1], qg1, dycat1, 1,
                                             name="mem_attn_bwd1")
    G["mem_q_norm"][1] = dqg1[0]
    d_os_ls = mix_bwd(os_, ls, dycat1)
    dqs, dks, dvs = [], [], []
    for gi, (win, dil) in enumerate(DIL_GROUPS):
        qg_, kg_, vg_, o_r, l_r = grp[gi]
        do_r, dl_r = _to_residues(d_os_ls[gi], dil), _to_residues(d_os_ls[3 + gi], dil)
        dq_r, dk_r, dv_r = dil_attn_bwd(qg_, kg_, vg_, o_r, l_r, do_r, dl_r, T // dil // DIL_BLOCK, name=f"dil_bwd{gi}")
        dqs.append(_from_residues(dq_r, dil))
        dks.append(_from_residues(dk_r, dil))
        dvs.append(_from_residues(dv_r, dil))
    dq, dk, dv = (jnp.concatenate(z, axis=1) for z in (dqs, dks, dvs))
    dpb, dkvp, dqn_g, dkn_g = qk_bwd(kvp, pb, kg_t, qg_t, cos, sin, dq, dk, dv, dq_mem1)
    G["b_q_norm"] = dqn_g
    G["kv_k_norm"] = dkn_g[0]
    dh1 = _mm(dpb, W["b_w_in"][0], tb=True, name="b_in_dx")
    G["b_w_in"] = _mm(h1, dpb, ta=True, name="b_in_dw")[None]
    dhkv = _mm(dkvp, W["kv_w"], tb=True, name="kv_in_dx")
    G["kv_w"] = _mm(hkv, dkvp, ta=True, name="kv_in_dw")
    dx2, dg1, dgkv = rms_bwd(x2, [W["attn_norm"][1:2], W["kv_norm"].reshape(1, -1)], [dh1, dhkv], dx3,
                             name="attn_rms_bwd1")
    G["attn_norm"][1] = dg1[0]
    G["kv_norm"] = dgkv[0]

    dx1 = _ffn_bwd(dx2, 0, W, ffn0, G)
    dycat0 = _mm(dx1, W["a_w_out"][0], tb=True, name="a_out_dx")
    G["a_w_out"] = _mm(ycat0, dx1, ta=True, name="a_out_dw")[None]
    dq_mem0, dkn0, dvm0, dqg0 = mem_attn_bwd(p, SHIFT_WIDTH // MEM_WIDTH, memkv[0][0], memkv[0][1], qg0, dycat0,
                                             RWKV_WIDTH // MEM_WIDTH, name="mem_attn_bwd0")
    G["mem_q_norm"][0] = dqg0[0]
    dy, dr_b, dk2_b, dv_b, dg, dlw, dlb, drk = mix_gate_bwd(y, r, k2, v, g, dycat0, lnx_w, lnx_b, r_k)
    for i, (dkn, dvm) in enumerate(((dkn0, dvm0), (dkn1, dvm1))):
        dwkv, dgm, dkg = mem_kv_bwd(mem, W["mem_norm"][i:i + 1], W["mem_w_kv"][i],
                                    _tile_heads(W["mem_k_norm"][i], MEM_WIDTH), dkn, dvm, name=f"mem_kv_bwd{i}")
        G["mem_w_kv"][i], G["mem_norm"][i], G["mem_k_norm"][i] = dwkv, dgm[0], dkg[0]
    for n in list(G):
        if isinstance(G[n], list) and all(z is not None for z in G[n]):
            G[n] = jnp.stack(G[n], axis=0)
    late_out = None
    if late is None:
        dr, dw, dk2, dv, dkk, dkka = scan_bwd(r, w, k2, v, kk, kka, states, final_state, dy)
    else:
        pieces = late[2](G)
        dr, dw, dk2, dv, dkk, dkka, *received = scan_bwd(r, w, k2, v, kk, kka, states, final_state, dy, scatter=pieces)
        late_out = (received, pieces)
    dxs, dmu, dw0, da0, dw2p, da2p, dg2p, dk_k, dk_a = rwkv_pre_bwd(
        p, mu, w0, a0, w2p, a2p, g2p, k_k, k_a, (dr, dr_b), dw, (dk2, dk2_b), (dv, dv_b), dkk, dkka, dg)
    dp = shift_bwd(dxs, mu, dq_mem0)
    G.update(a_mu=dmu, a_w0=dw0, a_a0=da0, a_w2=dw2p[None, :64], a_a2=da2p[None, 64:128], a_g2=dg2p[None, 128:],
             a_k_k=dk_k, a_k_a=dk_a, a_r_k=drk.reshape(1, RWKV_HEADS, HEAD_DIM), a_lnx_w=dlw, a_lnx_b=dlb)
    dh0 = _mm(dp, W["a_w_in"][0], tb=True, name="a_in_dx")
    G["a_w_in"] = _mm(h0, dp, ta=True, name="a_in_dw")[None]
    grad_x, dg0 = rms_bwd(x, [W["attn_norm"][0:1]], [dh0], dx1, name="attn_rms_bwd0")
    G["attn_norm"][0] = dg0[0]
    G["attn_norm"] = jnp.stack(G["attn_norm"], axis=0)
    return loss, grad_x, G, late_out


HBM_SPEC = pl.BlockSpec(memory_space=pltpu.HBM)


def _mesh_pos():
    return lax.axis_index("x"), lax.axis_index("y"), lax.axis_index("c")


def _other_chips(x, y):
    return [(1 - x, y), (x, 1 - y), (1 - x, 1 - y)]


def _remote(send_sems, recv_sems, k, src, dst, to):
    return pltpu.make_async_remote_copy(src_ref=src, dst_ref=dst, send_sem=send_sems.at[k], recv_sem=recv_sems.at[k],
                                        device_id=to, device_id_type=MESH)


def _comm_call(body, name, ins, out_shape, n_remote):
    scratch = [pltpu.SemaphoreType.DMA((n_remote,)), pltpu.SemaphoreType.DMA((n_remote,))]
    return pl.pallas_call(body, name=name, in_specs=[HBM_SPEC] * len(ins), out_specs=[HBM_SPEC] * len(out_shape),
                          out_shape=out_shape, scratch_shapes=scratch)(*ins)


def comm_gather(wbig, wsm):
    def body(wb, ws, ob, os_, send_sems, recv_sems):
        x, y, c = _mesh_pos()
        s = 2 * x + y
        me, sibling = (x, y, c), (x, y, 1 - c)
        chips = _other_chips(x, y)
        rc = functools.partial(_remote, send_sems, recv_sems)
        first = []
        for j, (cx, cy) in enumerate(chips):
            first.append(rc(j, wb.at[c], ob.at[s, c], (cx, cy, c)))
            first.append(rc(6 + j, ws, os_.at[s], (cx, cy, c)))
        for cp in first:
            cp.start()
        passed = []
        for j, (cx, cy) in enumerate(chips):
            blk = ob.at[2 * cx + cy, c]
            rc(j, blk, blk, me).wait_recv()
            passed.append(rc(3 + j, blk, blk, sibling))
            passed[-1].start()
        for j, (cx, cy) in enumerate(chips):
            blk = ob.at[2 * cx + cy, 1 - c]
            rc(3 + j, blk, blk, me).wait_recv()
            sb = os_.at[2 * cx + cy]
            rc(6 + j, sb, sb, me).wait_recv()
        for cp in first + passed:
            cp.wait_send()

    out_shape = [jax.ShapeDtypeStruct((N_CHIPS, *wbig.shape), wbig.dtype),
                 jax.ShapeDtypeStruct((N_CHIPS, *wsm.shape), wsm.dtype)]
    return _comm_call(body, "comm_gather", [wbig, wsm], out_shape, 9)


def comm_pair_exchange(gb, gs):
    def body(gb_ref, gs_ref, rb_ref, rs_ref, send_sems, recv_sems):
        x, y, c = _mesh_pos()
        sibling = (x, y, 1 - c)
        rc = functools.partial(_remote, send_sems, recv_sems)
        cps = [rc(r, gb_ref.at[r, 1 - c], rb_ref.at[r], sibling) for r in range(N_CHIPS)]
        cps.append(rc(N_CHIPS, gs_ref.at[1 - c], rs_ref, sibling))
        for cp in cps:
            cp.start()
        for cp in cps:
            cp.wait()

    out_shape = [jax.ShapeDtypeStruct((N_CHIPS, *gb.shape[2:]), gb.dtype), jax.ShapeDtypeStruct(gs.shape[1:], gs.dtype)]
    return _comm_call(body, "comm_pair_exchange", [gb, gs], out_shape, N_CHIPS + 1)


def comm_chip_exchange(hb, hs):
    def body(hb_ref, hs_ref, qb_ref, qs_ref, send_sems, recv_sems):
        x, y, c = _mesh_pos()
        s = 2 * x + y
        me = (x, y, c)
        chips = _other_chips(x, y)
        rc = functools.partial(_remote, send_sems, recv_sems)
        cps = []
        for j, (cx, cy) in enumerate(chips):
            cps.append(rc(j, hb_ref.at[2 * cx + cy], qb_ref.at[s], (cx, cy, c)))
            cps.append(rc(3 + j, hs_ref, qs_ref.at[s], (cx, cy, c)))
        for cp in cps:
            cp.start()
        for j, (cx, cy) in enumerate(chips):
            blk = qb_ref.at[2 * cx + cy]
            rc(j, blk, blk, me).wait_recv()
            sb = qs_ref.at[2 * cx + cy]
            rc(3 + j, sb, sb, me).wait_recv()
        for cp in cps:
            cp.wait_send()

    out_shape = [jax.ShapeDtypeStruct(hb.shape, hb.dtype), jax.ShapeDtypeStruct((N_CHIPS, *hs.shape), hs.dtype)]
    return _comm_call(body, "comm_chip_exchange", [hb, hs], out_shape, 6)


def comm_pair_share(halves):
    n = len(halves)

    def body(*refs):
        x, y, c = _mesh_pos()
        send_sems, recv_sems = refs[2 * n], refs[2 * n + 1]
        cps = [_remote(send_sems, recv_sems, k, refs[k], refs[n + k], (x, y, 1 - c)) for k in range(n)]
        for cp in cps:
            cp.start()
        for cp in cps:
            cp.wait()

    out_shape = [jax.ShapeDtypeStruct(h.shape, h.dtype) for h in halves]
    return _comm_call(body, "comm_pair_share", list(halves), out_shape, n)


def add_pairs(a, b, out_dtype, *, name, tb):
    T, L = a.shape
    return _rowmap(lambda i, n, p, q: p + q, name=name, T=T, tb=tb, ins=[("row", a), ("row", b)],
                   outs=[("row", L, out_dtype)])[0]


def add_chips(parts, *, name, tb):
    T, L = parts[0].shape

    def fn(i, n, *ps):
        acc = ps[0].astype(F32)
        for p in ps[1:]:
            acc = acc + p.astype(F32)
        return acc

    return _rowmap(fn, name=name, T=T, tb=tb, ins=[("row", p) for p in parts], outs=[("row", L, F32)])[0]


def adamw(g, w, m, v, *, name, tb):
    T, L = g.shape

    def fn(i, n, g, w, m, v):
        m2 = ADAM_B1 * m + (1.0 - ADAM_B1) * g
        v2 = ADAM_B2 * v + (1.0 - ADAM_B2) * (g * g)
        m_hat = m2 / (1.0 - ADAM_B1 ** ADAM_STEP)
        v_hat = v2 / (1.0 - ADAM_B2 ** ADAM_STEP)
        return -ADAM_LR * (m_hat / (jnp.sqrt(v_hat) + ADAM_EPS) + ADAM_WD * w), m2, v2

    return _rowmap(fn, name=name, T=T, tb=tb, ins=[("row", z) for z in (g, w, m, v)], outs=[("row", L, F32)] * 3)


BIG_LANES = 1024
SMALL_LANES = 128


def _flat_cat(arrs, total, dtype):
    parts = [a.reshape(-1).astype(dtype) for a in arrs]
    n = sum(p.shape[0] for p in parts)
    assert n <= total, (n, total)
    if n < total:
        parts.append(jnp.zeros((total - n,), dtype))
    return jnp.concatenate(parts)


def _split_flat(flat, shapes):
    out, off = [], 0
    for shp in shapes:
        n = math.prod(shp)
        out.append(flat[off:off + n].reshape(shp))
        off += n
    return out


def _round_up(n, m):
    return -(-n // m) * m


def _full_shape(shard_shape, axis):
    return tuple(d * N_CHIPS if i == axis else d for i, d in enumerate(shard_shape))


def kernel(x, mem, attn_norm, a_w_in, a_mu, a_w0, a_w2, a_a0, a_a2, a_g2, a_k_k, a_k_a, a_r_k, a_lnx_w, a_lnx_b, a_w_out, kv_norm, kv_w, kv_k_norm, b_w_in, b_q_norm, b_w_out, mem_norm, mem_w_kv, mem_q_norm, mem_k_norm, ffn_norm, ffn_w_up, ffn_conv_w, ffn_conv_b, ffn_w_down, loss_target, m_attn_norm, m_a_w_in, m_a_mu, m_a_w0, m_a_w2, m_a_a0, m_a_a2, m_a_g2, m_a_k_k, m_a_k_a, m_a_r_k, m_a_lnx_w, m_a_lnx_b, m_a_w_out, m_kv_norm, m_kv_w, m_kv_k_norm, m_b_w_in, m_b_q_norm, m_b_w_out, m_mem_norm, m_mem_w_kv, m_mem_q_norm, m_mem_k_norm, m_ffn_norm, m_ffn_w_up, m_ffn_conv_w, m_ffn_conv_b, m_ffn_w_down, v_attn_norm, v_a_w_in, v_a_mu, v_a_w0, v_a_w2, v_a_a0, v_a_a2, v_a_g2, v_a_k_k, v_a_k_a, v_a_r_k, v_a_lnx_w, v_a_lnx_b, v_a_w_out, v_kv_norm, v_kv_w, v_kv_k_norm, v_b_w_in, v_b_q_norm, v_b_w_out, v_mem_norm, v_mem_w_kv, v_mem_q_norm, v_mem_k_norm, v_ffn_norm, v_ffn_w_up, v_ffn_conv_w, v_ffn_conv_b, v_ffn_w_down):
    args = (attn_norm, a_w_in, a_mu, a_w0, a_w2, a_a0, a_a2, a_g2, a_k_k, a_k_a, a_r_k, a_lnx_w, a_lnx_b, a_w_out, kv_norm, kv_w, kv_k_norm, b_w_in, b_q_norm, b_w_out, mem_norm, mem_w_kv, mem_q_norm, mem_k_norm, ffn_norm, ffn_w_up, ffn_conv_w, ffn_conv_b, ffn_w_down)
    ms = (m_attn_norm, m_a_w_in, m_a_mu, m_a_w0, m_a_w2, m_a_a0, m_a_a2, m_a_g2, m_a_k_k, m_a_k_a, m_a_r_k, m_a_lnx_w, m_a_lnx_b, m_a_w_out, m_kv_norm, m_kv_w, m_kv_k_norm, m_b_w_in, m_b_q_norm, m_b_w_out, m_mem_norm, m_mem_w_kv, m_mem_q_norm, m_mem_k_norm, m_ffn_norm, m_ffn_w_up, m_ffn_conv_w, m_ffn_conv_b, m_ffn_w_down)
    vs = (v_attn_norm, v_a_w_in, v_a_mu, v_a_w0, v_a_w2, v_a_a0, v_a_a2, v_a_g2, v_a_k_k, v_a_k_a, v_a_r_k, v_a_lnx_w, v_a_lnx_b, v_a_w_out, v_kv_norm, v_kv_w, v_kv_k_norm, v_b_w_in, v_b_q_norm, v_b_w_out, v_mem_norm, v_mem_w_kv, v_mem_q_norm, v_mem_k_norm, v_ffn_norm, v_ffn_w_up, v_ffn_conv_w, v_ffn_conv_b, v_ffn_w_down)
    w_sh, m_sh, v_sh = (dict(zip(WEIGHTS, z)) for z in (args, ms, vs))
    xi, yi, ci = _mesh_pos()
    chip = 2 * xi + yi
    axes = {**dict(BIG), **dict(SMALL_SHARDED)}
    early_names = [n for n, _ in BIG if n in EARLY_BIG]
    late_names = [n for n, _ in BIG if n not in EARLY_BIG and n != NATURAL_BIG]
    ss_names, ss_axes = [n for n, _ in SMALL_SHARDED], dict(SMALL_SHARDED)
    shapes_of = lambda names: [w_sh[n].shape for n in names]
    count = lambda names: sum(math.prod(s) for s in shapes_of(names))
    n_early, n_late = count(early_names), count(late_names)
    assert n_early % (2 * 16 * BIG_LANES) == 0 and n_late % (2 * 16 * BIG_LANES) == 0
    mh, mh_late = n_early // (2 * BIG_LANES), n_late // (2 * BIG_LANES)
    n_ss = _round_up(count(ss_names), 8 * SMALL_LANES)

    def shard_pack(names, total, dtype, source):
        return _flat_cat([source[n] for n in names], total, dtype)

    def unshard(names, gathered):
        per_chip = [_split_flat(gathered[j], shapes_of(names)) for j in range(N_CHIPS)]
        return {n: jnp.concatenate([per_chip[j][k] for j in range(N_CHIPS)], axis=axes[n]) for k, n in enumerate(names)}

    def by_chip(names, total, dtype, grads):
        parts = [jnp.split(grads[n], N_CHIPS, axis=axes[n]) for n in names]
        return jnp.stack([_flat_cat([p[j] for p in parts], total, dtype) for j in range(N_CHIPS)])

    wbig = shard_pack(early_names, n_early, MMD, w_sh).reshape(2, mh, BIG_LANES)
    wsm = shard_pack(ss_names, n_ss, F32, w_sh).reshape(-1, SMALL_LANES)
    wbig_all, wsm_all = comm_gather(wbig, wsm)
    wbig_all = lax.dynamic_update_index_in_dim(wbig_all, wbig, chip, 0).reshape(N_CHIPS, -1)
    wsm_all = lax.dynamic_update_index_in_dim(wsm_all, wsm, chip, 0).reshape(N_CHIPS, -1)
    W = {n: w_sh[n] for n in SMALL_REPL}
    W.update(unshard(early_names, wbig_all))
    W.update(unshard(ss_names, wsm_all))
    for n in ("a_w2", "a_a2", "a_g2"):
        W[n] = W[n].astype(MMD)
    wlate = shard_pack(late_names, n_late, MMD, w_sh).reshape(2, mh_late, BIG_LANES)
    nat_axis = axes[NATURAL_BIG]
    wnat = w_sh[NATURAL_BIG].astype(MMD)
    assert wnat.shape[0] == 2 and nat_axis != 0

    def unpack_late(gathered):
        full = lax.dynamic_update_index_in_dim(gathered[0], wlate, chip, 0)
        out = unshard(late_names, full.reshape(N_CHIPS, -1))
        nat = lax.dynamic_update_index_in_dim(gathered[1], wnat, chip, 0)
        out[NATURAL_BIG] = jnp.concatenate([nat[j] for j in range(N_CHIPS)], axis=nat_axis)
        return out

    def pack_late(grads):
        return [by_chip(late_names, n_late, BF16, grads).reshape(N_CHIPS, 2, mh_late, BIG_LANES),
                jnp.stack(jnp.split(grads[NATURAL_BIG].astype(BF16), N_CHIPS, axis=nat_axis))]

    loss_blk, grad_x, G, (received, pieces) = local_step(x[0], mem[0], loss_target[0], W,
                                                          late=([wlate, wnat], unpack_late, pack_late))
    loss = lax.psum(loss_blk[0, 0], ("x", "y", "c"))

    own_piece = lambda p: lax.dynamic_index_in_dim(lax.dynamic_index_in_dim(p, chip, 0, keepdims=False), ci, 0,
                                                   keepdims=False)
    gh_late = add_chips([received[0][k] for k in range(len(PEER_FLIPS))] + [own_piece(pieces[0])],
                        name="add_pieces_late", tb=32)
    gh_nat = add_chips([received[1][k] for k in range(len(PEER_FLIPS))] + [own_piece(pieces[1])],
                       name="add_pieces_natural", tb=32)
    gbig = by_chip(early_names, n_early, F32, G).reshape(N_CHIPS, 2, mh, BIG_LANES)
    sm_full_names = ss_names + list(SMALL_REPL)
    sm_full_shapes = [_full_shape(w_sh[n].shape, ss_axes[n]) for n in ss_names] + [w_sh[n].shape for n in SMALL_REPL]
    n_smf = _round_up(sum(math.prod(s) for s in sm_full_shapes), 2 * 8 * SMALL_LANES)
    msh = n_smf // (2 * SMALL_LANES)
    gsm = _flat_cat([G[n] for n in sm_full_names], n_smf, F32).reshape(2, msh, SMALL_LANES)
    rb, rs = comm_pair_exchange(gbig, gsm)
    mine_b = lax.dynamic_index_in_dim(gbig, ci, axis=1, keepdims=False)
    mine_s = lax.dynamic_index_in_dim(gsm, ci, axis=0, keepdims=False)
    hb = add_pairs(mine_b.reshape(-1, BIG_LANES), rb.reshape(-1, BIG_LANES), BF16, name="add_pairs_big", tb=128)
    hs = add_pairs(mine_s, rs, F32, name="add_pairs_small", tb=msh)
    hb = hb.reshape(N_CHIPS, mh, BIG_LANES)
    qb, qs = comm_chip_exchange(hb, hs)
    qb = lax.dynamic_update_index_in_dim(qb, lax.dynamic_index_in_dim(hb, chip, 0, keepdims=False), chip, 0)
    qs = lax.dynamic_update_index_in_dim(qs, hs, chip, 0)
    gh = add_chips([qb[j] for j in range(N_CHIPS)], name="add_chips_big", tb=32)
    gsh = add_chips([qs[j] for j in range(N_CHIPS)], name="add_chips_small", tb=msh)
    rh, rh_late, rh_nat, rsh = comm_pair_share([gh, gh_late, gh_nat, gsh])
    both = lambda mine_, theirs: jnp.where(ci == 0, jnp.stack([mine_, theirs]), jnp.stack([theirs, mine_]))
    gfull, gfull_late, gfull_nat, gsfull = both(gh, rh), both(gh_late, rh_late), both(gh_nat, rh_nat), both(gsh, rsh)

    res = {tag: {} for tag in ("grad", "delta", "new_m", "new_v")}
    big_grads = (list(zip(early_names, _split_flat(gfull.reshape(-1), shapes_of(early_names))))
                 + list(zip(late_names, _split_flat(gfull_late.reshape(-1), shapes_of(late_names))))
                 + [(NATURAL_BIG, gfull_nat)])
    for n, g in big_grads:
        shp = w_sh[n].shape
        rows = lambda z: z.reshape(-1, shp[-1])
        nrow = math.prod(shp[:-1])
        tb = next(t for t in (512, 256, 128, 64) if nrow % t == 0 and t * shp[-1] <= (1 << 19))
        outs = adamw(rows(g), rows(w_sh[n]), rows(m_sh[n]), rows(v_sh[n]), name=f"adamw_{n}", tb=tb)
        res["grad"][n] = g
        for tag, o in zip(("delta", "new_m", "new_v"), outs):
            res[tag][n] = o.reshape(shp)
    sm_full = dict(zip(sm_full_names, _split_flat(gsfull.reshape(-1), sm_full_shapes)))
    g_loc = {}
    for n in ss_names:
        size = w_sh[n].shape[ss_axes[n]]
        g_loc[n] = lax.dynamic_slice_in_dim(sm_full[n], chip * size, size, axis=ss_axes[n])
    for n in SMALL_REPL:
        g_loc[n] = sm_full[n]
    n_sml = _round_up(sum(math.prod(w_sh[n].shape) for n in sm_full_names), 8 * SMALL_LANES)
    pack_sm = lambda d: _flat_cat([d[n] for n in sm_full_names], n_sml, F32).reshape(-1, SMALL_LANES)
    d_sm, m_sm, v_sm = adamw(pack_sm(g_loc), pack_sm(w_sh), pack_sm(m_sh), pack_sm(v_sh), name="adamw_small",
                             tb=n_sml // SMALL_LANES)
    sm_loc_shapes = [w_sh[n].shape for n in sm_full_names]
    res["grad"].update(g_loc)
    for tag, smv in (("delta", d_sm), ("new_m", m_sm), ("new_v", v_sm)):
        res[tag].update(dict(zip(sm_full_names, _split_flat(smv.reshape(-1), sm_loc_shapes))))
    return (loss, grad_x[None], *[res[tag][n] for tag in ("grad", "delta", "new_m", "new_v") for n in WEIGHTS])
```

```python
import functools
import math

import numpy as np
import jax
import jax.numpy as jnp
from jax import lax
from jax.experimental import pallas as pl
from jax.experimental.pallas import tpu as pltpu

F32 = jnp.float32
BF16 = jnp.bfloat16
MMD = jnp.bfloat16

D_MODEL = 1024
HEAD_DIM = 64
N_MEM = 256
MEM_WIDTH = 256
RWKV_HEADS = 12
RWKV_WIDTH = 768
SHIFT_WIDTH = 2560
LORA_WIDTH = 256
DIL_WIDTH = 768
DIL_GROUPS = ((128, 1), (512, 4), (2048, 16))
DIL_BLOCK = 128
D_FF = 2816
ROPE_THETA = 10000.0
RMS_EPS = 1e-6
LNX_EPS = 64e-5
NEG_INF = -1e30
ADAM_LR = 0.001
ADAM_B1 = 0.9
ADAM_B2 = 0.999
ADAM_EPS = 1e-08
ADAM_WD = 0.01
ADAM_STEP = 10
N_CHIPS = 4
MESH = pl.DeviceIdType.MESH
VMEM_LIMIT_MB = 56
SCAN_CHUNK = 64
SCAN_UNROLL = 16
SCAN_UNROLL_BWD = 8

BIG = (("a_w_in", 2), ("a_w_out", 1), ("kv_w", 1), ("b_w_in", 1), ("b_w_out", 2), ("mem_w_kv", 1),
       ("ffn_w_up", 2), ("ffn_w_down", 1))
EARLY_BIG = ("a_w_in",)
NATURAL_BIG = "ffn_w_up"
SMALL_SHARDED = (("a_mu", 1), ("a_w0", 1), ("a_w2", 2), ("a_a0", 1), ("a_a2", 2), ("a_g2", 2), ("a_k_k", 1),
                 ("a_k_a", 1), ("a_lnx_w", 1), ("a_lnx_b", 1), ("ffn_conv_w", 2))
SMALL_REPL = ("attn_norm", "a_r_k", "kv_norm", "kv_k_norm", "b_q_norm", "mem_norm", "mem_q_norm", "mem_k_norm",
              "ffn_norm", "ffn_conv_b")
WEIGHTS = ("attn_norm", "a_w_in", "a_mu", "a_w0", "a_w2", "a_a0", "a_a2", "a_g2", "a_k_k", "a_k_a", "a_r_k",
           "a_lnx_w", "a_lnx_b", "a_w_out", "kv_norm", "kv_w", "kv_k_norm", "b_w_in", "b_q_norm", "b_w_out",
           "mem_norm", "mem_w_kv", "mem_q_norm", "mem_k_norm", "ffn_norm", "ffn_w_up", "ffn_conv_w", "ffn_conv_b",
           "ffn_w_down")


def _cp(sem=None, **kw):
    return pltpu.CompilerParams(dimension_semantics=sem, vmem_limit_bytes=VMEM_LIMIT_MB << 20, **kw)


def _tile(n, cands=(512, 256, 128)):
    for c in cands:
        if n % c == 0:
            return c
    return n


def _mm(a, b, *, name, ta=False, tb=False, add=None, out_dtype=F32):
    K, M = a.shape if ta else a.shape[::-1]
    N = b.shape[0] if tb else b.shape[1]
    assert K == (b.shape[1] if tb else b.shape[0])
    tm, tn = _tile(M, (512, 256, 128) if ta else (1024, 512, 256, 128)), _tile(N, (512, 1408, 256, 128))
    bytes_of = lambda z: z.size * z.dtype.itemsize
    kept = 1 if bytes_of(b) + bytes_of(a) * (N // tn) < bytes_of(a) + bytes_of(b) * (M // tm) else 0
    mi, ni = ((lambda o, i: i), (lambda o, i: o)) if kept else ((lambda o, i: o), (lambda o, i: i))
    grid = (N // tn, M // tm) if kept else (M // tm, N // tn)
    a_blk, b_blk = ((K, tm) if ta else (tm, K)), ((tn, K) if tb else (K, tn))
    a_spec = pl.BlockSpec(a_blk, (lambda o, i: (0, mi(o, i))) if ta else (lambda o, i: (mi(o, i), 0)))
    b_spec = pl.BlockSpec(b_blk, (lambda o, i: (ni(o, i), 0)) if tb else (lambda o, i: (0, ni(o, i))))
    o_spec = pl.BlockSpec((tm, tn), lambda o, i: (mi(o, i), ni(o, i)))
    dn = (((0,) if ta else (1,), (1,) if tb else (0,)), ((), ()))
    has_add = add is not None
    cache = (b if kept else a).dtype != MMD

    def body(*refs):
        vals = [refs[0], refs[1]]
        o_ref = refs[2 + has_add]
        if cache:
            scr = refs[-1]

            @pl.when(pl.program_id(1) == 0)
            def _():
                scr[...] = vals[kept][...].astype(MMD)

            vals[kept] = scr
        acc = lax.dot_general(vals[0][...].astype(MMD), vals[1][...].astype(MMD), dn, preferred_element_type=F32)
        if has_add:
            acc = acc + refs[2][...]
        o_ref[...] = acc.astype(o_ref.dtype)

    ins = [a, b] + ([add] if has_add else [])
    specs = [a_spec, b_spec] + ([o_spec] if has_add else [])
    return pl.pallas_call(
        body, name=name, grid=grid, in_specs=specs, out_specs=o_spec,
        out_shape=jax.ShapeDtypeStruct((M, N), out_dtype),
        scratch_shapes=[pltpu.VMEM(b_blk if kept else a_blk, MMD)] if cache else [],
        compiler_params=_cp(("parallel", "arbitrary")),
    )(*ins)


def _rowmap(fn, *, name, T, tb, ins, outs, accs=()):
    nblk = T // tb
    assert T % tb == 0 and tb % 8 == 0
    in_specs, args = [], []
    for spec in ins:
        kind, arr = spec[0], spec[1]
        w, cb = (spec[2], spec[3]) if len(spec) > 2 else (arr.shape[-1], 0)
        if kind == "row":
            in_specs.append(pl.BlockSpec((tb, w), lambda i, cb=cb: (i, cb)))
        elif kind == "prev":
            in_specs.append(pl.BlockSpec((8, w), lambda i, cb=cb: (jnp.maximum(i * (tb // 8) - 1, 0), cb)))
        elif kind == "next":
            in_specs.append(pl.BlockSpec((8, w), lambda i, cb=cb: (jnp.minimum((i + 1) * (tb // 8), T // 8 - 1), cb)))
        elif kind == "const":
            in_specs.append(pl.BlockSpec(arr.shape, lambda i, nd=arr.ndim: (0,) * nd))
        else:
            raise ValueError(kind)
        args.append(arr)
    out_shape, out_specs = [], []
    for kind, w, dt in outs:
        out_shape.append(jax.ShapeDtypeStruct((T, w), dt))
        out_specs.append(pl.BlockSpec((tb, w), lambda i: (i, 0)))
    for shp, dt in accs:
        out_shape.append(jax.ShapeDtypeStruct(shp, dt))
        out_specs.append(pl.BlockSpec(shp, lambda i, nd=len(shp): (0,) * nd))
    n_in, n_out = len(ins), len(outs)

    def body(*refs):
        i = pl.program_id(0)
        vals = [r[...] for r in refs[:n_in]]
        res = fn(i, nblk, *vals)
        if not isinstance(res, (tuple, list)):
            res = (res,)
        assert len(res) == n_out + len(accs), (name, len(res))
        for r, v in zip(refs[n_in:n_in + n_out], res[:n_out]):
            r[...] = v.astype(r.dtype)
        acc_refs = refs[n_in + n_out:]
        if acc_refs:
            @pl.when(i == 0)
            def _():
                for r in acc_refs:
                    r[...] = jnp.zeros(r.shape, r.dtype)

            for r, v in zip(acc_refs, res[n_out:]):
                r[...] += v

    res = pl.pallas_call(
        body, name=name, grid=(nblk,), in_specs=in_specs, out_specs=out_specs, out_shape=out_shape,
        compiler_params=_cp(("arbitrary",)),
    )(*args)
    return res


def _row_pick(halo, r):
    rid = lax.broadcasted_iota(jnp.int32, halo.shape, 0)
    return jnp.sum(jnp.where(rid == r, halo, 0.0), axis=0, keepdims=True)


def _shift_down(x, row_before, is_first):
    rid = lax.broadcasted_iota(jnp.int32, x.shape, 0)
    first = jnp.where(is_first, 0.0, 1.0) * row_before
    return jnp.where(rid == 0, first, pltpu.roll(x, 1, axis=0))


def _shift_up(x, row_after, is_last):
    n = x.shape[0]
    rid = lax.broadcasted_iota(jnp.int32, x.shape, 0)
    last = jnp.where(is_last, 0.0, 1.0) * row_after
    return jnp.where(rid == n - 1, last, pltpu.roll(x, n - 1, axis=0))


def _dot(a, b, dn=(((1,), (0,)), ((), ()))):
    return lax.dot_general(a.astype(MMD), b.astype(MMD), dn, preferred_element_type=F32)


def _dot_nt(a, b):
    return _dot(a, b, (((1,), (1,)), ((), ())))


def _dot_tn(a, b):
    return _dot(a, b, (((0,), (0,)), ((), ())))


def _dot_exact01(x, g01):
    hi = x.astype(BF16)
    lo = (x - hi.astype(F32)).astype(BF16)
    dn = (((1,), (0,)), ((), ()))
    return (lax.dot_general(hi, g01, dn, preferred_element_type=F32)
            + lax.dot_general(lo, g01, dn, preferred_element_type=F32))


def _fold_heads(v, fold):
    return _row_pick(_dot_exact01(jnp.broadcast_to(v, (8, v.shape[1])), fold), 0)


def _fold_ones(width):
    idx = np.arange(width) % HEAD_DIM
    return jnp.asarray((idx[:, None] == np.arange(HEAD_DIM)[None, :]).astype(np.float32), BF16)


def _head_masks(width):
    idx = np.arange(width) // HEAD_DIM
    return jnp.asarray((idx[None, :] == np.arange(width // HEAD_DIM)[:, None]).astype(np.float32)[:, None, :], F32)


def _rms_stats(x):
    r = lax.rsqrt(jnp.mean(x * x, axis=-1, keepdims=True) + RMS_EPS)
    return r, x * r


def rms_fwd(x, gains, *, name):
    T, D = x.shape

    def fn(i, nblk, xb, *gs):
        _, xh = _rms_stats(xb)
        return tuple(xh * g for g in gs)

    return _rowmap(fn, name=name, T=T, tb=512, ins=[("row", x)] + [("const", g) for g in gains],
                   outs=[("row", D, MMD)] * len(gains))


def rms_bwd(x, gains, dhs, dres, *, name):
    T, D = x.shape
    n = len(gains)

    def fn(i, nblk, xb, dr, *rest):
        gs, ds = rest[:n], rest[n:]
        r, xh = _rms_stats(xb)
        dx = dr
        dgs = []
        for g, dh in zip(gs, ds):
            dgs.append(jnp.sum(dh * xh, axis=0, keepdims=True))
            dxh = dh * g
            dx = dx + r * (dxh - xh * jnp.mean(dxh * xh, axis=-1, keepdims=True))
        return (dx, *dgs)

    return _rowmap(fn, name=name, T=T, tb=512,
                   ins=[("row", x), ("row", dres)] + [("const", g) for g in gains] + [("row", d) for d in dhs],
                   outs=[("row", D, F32)], accs=[((1, D), F32)] * n)


def _segsum(x):
    first = lax.broadcasted_iota(jnp.int32, (x.shape[0], 128), 1) < HEAD_DIM
    outs = []
    for p in range(x.shape[1] // 128):
        xs = x[:, p * 128:(p + 1) * 128]
        lo = jnp.sum(jnp.where(first, xs, 0.0), axis=-1, keepdims=True)
        hi = jnp.sum(jnp.where(first, 0.0, xs), axis=-1, keepdims=True)
        outs.append(jnp.where(first, lo, hi))
    return jnp.concatenate(outs, axis=1)


def _pre1_common(i, ps, halo, mu, w0, a0, w2p, a2p, g2p, k_k, k_a):
    prev = _shift_down(ps, _row_pick(halo, 7), i == 0)
    xs = ps + (prev - ps) * mu
    lo = xs[:, 3 * RWKV_WIDTH:]
    tl, sl = jnp.tanh(lo), jax.nn.sigmoid(lo)
    dec = w0 + _dot(tl, w2p)
    ain = a0 + _dot(lo, a2p)
    g = _dot(sl, g2p)
    wl = -jax.nn.softplus(-dec) - 0.5
    w = jnp.exp(-jnp.exp(wl))
    a = jax.nn.sigmoid(ain)
    k = xs[:, RWKV_WIDTH:2 * RWKV_WIDTH]
    z = k * k_k
    nrm = jnp.sqrt(_segsum(z * z))
    kk = z / jnp.maximum(nrm, 1e-12)
    return prev, xs, lo, tl, sl, dec, wl, w, a, g, k, nrm, kk


def rwkv_pre_fwd(p, mu, w0, a0, w2p, a2p, g2p, k_k, k_a):
    T = p.shape[0]

    def fn(i, nblk, ps, halo, mu, w0, a0, w2p, a2p, g2p, k_k, k_a):
        _, xs, _, _, _, _, _, w, a, g, k, _, kk = _pre1_common(i, ps, halo, mu, w0, a0, w2p, a2p, g2p, k_k, k_a)
        W = RWKV_WIDTH
        return xs[:, :W], w, k * (1.0 + (a - 1.0) * k_a), xs[:, 2 * W:3 * W], kk, kk * a, g

    return _rowmap(fn, name="rwkv_pre_fwd", T=T, tb=256,
                   ins=[("row", p, SHIFT_WIDTH, 0), ("prev", p, SHIFT_WIDTH, 0)]
                   + [("const", c) for c in (mu, w0, a0, w2p, a2p, g2p, k_k, k_a)],
                   outs=[("row", RWKV_WIDTH, F32)] * 7)


def rwkv_pre_bwd(p, mu, w0, a0, w2p, a2p, g2p, k_k, k_a, drs, dw, dk2s, dvs, dkk, dkka, dg):
    T = p.shape[0]

    def fn(i, nblk, ps, halo, mu, w0, a0, w2p, a2p, g2p, k_k, k_a, dr0, dr1, dw, dk20, dk21, dv0, dv1, dkk, dkka, dg):
        prev, xs, lo, tl, sl, dec, wl, w, a, g, k, nrm, kk = _pre1_common(i, ps, halo, mu, w0, a0, w2p, a2p, g2p, k_k, k_a)
        dk2 = dk20 + dk21
        dkk_t = dkk + dkka * a
        proj = jnp.where(nrm > 1e-12, kk * _segsum(dkk_t * kk), 0.0)
        dz = (dkk_t - proj) / jnp.maximum(nrm, 1e-12)
        dk = dz * k_k + dk2 * (1.0 + (a - 1.0) * k_a)
        da = dkka * kk + dk2 * k * k_a
        ddec = dw * (-w * jnp.exp(wl)) * jax.nn.sigmoid(-dec)
        dain = da * a * (1.0 - a)
        dlo = (_dot_nt(ddec, w2p) * (1.0 - tl * tl) + _dot_nt(dain, a2p) + _dot_nt(dg, g2p) * sl * (1.0 - sl))
        dxs = jnp.concatenate([dr0 + dr1, dk, dv0 + dv1, dlo], axis=1)
        s = lambda z: jnp.sum(z, axis=0, keepdims=True)
        return (dxs, s(dxs * (prev - ps)), s(ddec), s(dain), _dot_tn(tl, ddec), _dot_tn(lo, dain), _dot_tn(sl, dg),
                s(dz * k), s(dk2 * k * (a - 1.0)))

    return _rowmap(fn, name="rwkv_pre_bwd", T=T, tb=128,
                   ins=[("row", p, SHIFT_WIDTH, 0), ("prev", p, SHIFT_WIDTH, 0)]
                   + [("const", c) for c in (mu, w0, a0, w2p, a2p, g2p, k_k, k_a)]
                   + [("row", c) for c in (*drs, dw, *dk2s, *dvs, dkk, dkka, dg)],
                   outs=[("row", SHIFT_WIDTH, F32)],
                   accs=[((1, SHIFT_WIDTH), F32), ((1, RWKV_WIDTH), F32), ((1, RWKV_WIDTH), F32)]
                   + [((LORA_WIDTH, RWKV_WIDTH), F32)] * 3 + [((1, RWKV_WIDTH), F32)] * 2)


def shift_bwd(dxs, mu, dq_mem):
    T = dxs.shape[0]

    def fn(i, nblk, d, halo, mu, dq):
        nxt = _shift_up(d, _row_pick(halo, 0), i == nblk - 1)
        return jnp.concatenate([d * (1.0 - mu) + nxt * mu, dq], axis=1)

    return _rowmap(fn, name="shift_bwd", T=T, tb=256,
                   ins=[("row", dxs), ("next", dxs), ("const", mu), ("row", dq_mem)],
                   outs=[("row", SHIFT_WIDTH + MEM_WIDTH, MMD)])[0]


N_PAIRS = RWKV_HEADS // 2


def _pair_consts():
    row = lax.broadcasted_iota(jnp.int32, (HEAD_DIM, 128), 0)
    lane = lax.broadcasted_iota(jnp.int32, (HEAD_DIM, 128), 1)
    eye2 = jnp.logical_or(lane == row, lane == row + HEAD_DIM).astype(F32)
    li = lax.broadcasted_iota(jnp.int32, (128, 128), 0) < HEAD_DIM
    lj = lax.broadcasted_iota(jnp.int32, (128, 128), 1) < HEAD_DIM
    return eye2, (li == lj).astype(BF16)


def _pair_sum(p, ones2):
    n, m, l = p.shape
    s = lax.dot_general(p.reshape(n * m, l).astype(BF16), ones2, (((1,), (0,)), ((), ())), preferred_element_type=F32)
    return s.reshape(n, m, l)


def _pair_rows(row):
    return jnp.stack([row[:, p * 128:(p + 1) * 128] for p in range(N_PAIRS)], axis=0)


def _pair_flat(rows):
    return jnp.concatenate([rows[p] for p in range(N_PAIRS)], axis=1)


def _split_bf16(v):
    hi = v.astype(BF16).astype(F32)
    return hi, v - hi


def _gather_copies(srcs, dsts, send_sems, recv_sems):
    x, y, c = _mesh_pos()
    s = 2 * x + y
    me, sibling = (x, y, c), (x, y, 1 - c)
    rc = functools.partial(_remote, send_sems, recv_sems)
    ici, land, fwd, arrived = [], [], [], []
    for b, (src, dst) in enumerate(zip(srcs, dsts)):
        for j, (cx, cy) in enumerate(_other_chips(x, y)):
            k = 6 * b + j
            ici.append(rc(k, src.at[c], dst.at[s, c], (cx, cy, c)))
            blk, blk2 = dst.at[2 * cx + cy, c], dst.at[2 * cx + cy, 1 - c]
            land.append(rc(k, blk, blk, me))
            fwd.append(rc(k + 3, blk, blk, sibling))
            arrived.append(rc(k + 3, blk2, blk2, me))
    return ici, land, fwd, arrived


def scan_fwd(r, w, k2, v, kk, kka, gather=None):
    T, W = r.shape
    tc = SCAN_CHUNK
    nchunk = T // tc
    seq = pl.BlockSpec((tc, W), lambda i: (i, 0))
    one_state = pl.BlockSpec((N_PAIRS, HEAD_DIM, 128), lambda i: (0, 0, 0))
    nb = 0 if gather is None else len(gather)

    def body(r_ref, w_ref, k2_ref, v_ref, kk_ref, kka_ref, *rest):
        if gather is None:
            y_ref, st_ref, fin_ref, s_scr, vhi_scr, vlo_scr = rest
        else:
            srcs, (y_ref, st_ref, fin_ref), dsts = rest[:nb], rest[nb:nb + 3], rest[nb + 3:2 * nb + 3]
            s_scr, vhi_scr, vlo_scr, send_sems, recv_sems = rest[2 * nb + 3:]
            ici, land, fwd, arrived = _gather_copies(srcs, dsts, send_sems, recv_sems)

            @pl.when(pl.program_id(0) == 0)
            def _():
                for cp in ici:
                    cp.start()

            @pl.when(pl.program_id(0) == nchunk // 2)
            def _():
                for a, f in zip(land, fwd):
                    a.wait_recv()
                    f.start()

        @pl.when(pl.program_id(0) == 0)
        def _():
            s_scr[...] = jnp.zeros(s_scr.shape, F32)

        vhi_scr[...], vlo_scr[...] = _split_bf16(v_ref[...])
        eye2, ones2 = _pair_consts()
        eye2b = eye2.astype(BF16)

        def step(t, carry):
            r_t, w_t, k2_t, kk_t, kka_t, vhi_t, vlo_t = (
                _pair_rows(ref[pl.ds(t, 1), :]) for ref in (r_ref, w_ref, k2_ref, kk_ref, kka_ref, vhi_scr, vlo_scr))
            S = s_scr[...]
            sa = -_pair_sum(S * kk_t, ones2)
            vb = _pair_sum(eye2b * vhi_t.astype(BF16), ones2) + _pair_sum(eye2b * vlo_t.astype(BF16), ones2)
            S2 = S * w_t + sa * kka_t + vb * k2_t
            y_ref[pl.ds(t, 1), :] = _pair_flat(jnp.sum(eye2 * _pair_sum(S2 * r_t, ones2), axis=1, keepdims=True))
            s_scr[...] = S2
            st_ref[t] = S
            return carry

        lax.fori_loop(0, tc, step, 0, unroll=SCAN_UNROLL)
        fin_ref[...] = s_scr[...]

        if gather is not None:
            @pl.when(pl.program_id(0) == nchunk - 1)
            def _():
                for a in arrived:
                    a.wait_recv()
                for cp in ici + fwd:
                    cp.wait_send()

    in_specs = [seq] * 6
    out_specs = [seq, pl.BlockSpec((tc, N_PAIRS, HEAD_DIM, 128), lambda i: (i, 0, 0, 0)), one_state]
    out_shape = [jax.ShapeDtypeStruct((T, W), F32), jax.ShapeDtypeStruct((T, N_PAIRS, HEAD_DIM, 128), F32),
                 jax.ShapeDtypeStruct((N_PAIRS, HEAD_DIM, 128), F32)]
    scratch = [pltpu.VMEM((N_PAIRS, HEAD_DIM, 128), F32), pltpu.VMEM((tc, W), F32), pltpu.VMEM((tc, W), F32)]
    args = [r, w, k2, v, kk, kka]
    if gather is not None:
        in_specs += [HBM_SPEC] * nb
        out_specs += [HBM_SPEC] * nb
        out_shape += [jax.ShapeDtypeStruct((N_CHIPS, *g.shape), g.dtype) for g in gather]
        scratch += [pltpu.SemaphoreType.DMA((6 * nb,)), pltpu.SemaphoreType.DMA((6 * nb,))]
        args += list(gather)
    return pl.pallas_call(
        body, name="rwkv_scan_fwd", grid=(nchunk,), in_specs=in_specs, out_specs=out_specs, out_shape=out_shape,
        scratch_shapes=scratch, compiler_params=_cp(("arbitrary",)),
    )(*args)


PEER_FLIPS = tuple((fx, fy, fc) for fx in (0, 1) for fy in (0, 1) for fc in (0, 1))[1:]


def scan_bwd(r, w, k2, v, kk, kka, states, final_state, dy, scatter=None):
    T, W = r.shape
    tc = SCAN_CHUNK
    nchunk = T // tc
    seq = pl.BlockSpec((tc, W), lambda i: (nchunk - 1 - i, 0))
    st_spec = pl.BlockSpec((tc, N_PAIRS, HEAD_DIM, 128), lambda i: (nchunk - 1 - i, 0, 0, 0))
    one_state = pl.BlockSpec((N_PAIRS, HEAD_DIM, 128), lambda i: (0, 0, 0))
    nb, npeer = (0 if scatter is None else len(scatter)), len(PEER_FLIPS)

    def body(r_ref, w_ref, k2_ref, v_ref, kk_ref, kka_ref, st_ref, fin_ref, dy_ref, *rest):
        if scatter is None:
            dr_ref, dw_ref, dk2_ref, dv_ref, dkk_ref, dkka_ref, ds_scr, sc_scr, vhi_scr, vlo_scr = rest
        else:
            srcs, dsts = rest[:nb], rest[nb + 6:2 * nb + 6]
            dr_ref, dw_ref, dk2_ref, dv_ref, dkk_ref, dkka_ref = rest[nb:nb + 6]
            ds_scr, sc_scr, vhi_scr, vlo_scr, send_sems, recv_sems = rest[2 * nb + 6:]
            x, y, c = _mesh_pos()
            copies = []
            for b, (src, dst) in enumerate(zip(srcs, dsts)):
                for k, (fx, fy, fc) in enumerate(PEER_FLIPS):
                    px, py, pc = (1 - x if fx else x), (1 - y if fy else y), (1 - c if fc else c)
                    copies.append(_remote(send_sems, recv_sems, npeer * b + k, src.at[2 * px + py, pc], dst.at[k],
                                          (px, py, pc)))

            @pl.when(pl.program_id(0) == 0)
            def _():
                for cp in copies:
                    cp.start()

        @pl.when(pl.program_id(0) == 0)
        def _():
            ds_scr[...] = jnp.zeros(ds_scr.shape, F32)
            sc_scr[...] = fin_ref[...]

        vhi_scr[...], vlo_scr[...] = _split_bf16(v_ref[...])
        eye2, ones2 = _pair_consts()
        eye2b = eye2.astype(BF16)
        colsum = lambda z: jnp.sum(z, axis=1, keepdims=True)

        def step(j, carry):
            t = tc - 1 - j
            r_t, w_t, k2_t, kk_t, kka_t, vhi_t, vlo_t, dy_t = (
                _pair_rows(ref[pl.ds(t, 1), :])
                for ref in (r_ref, w_ref, k2_ref, kk_ref, kka_ref, vhi_scr, vlo_scr, dy_ref))
            s_prev, s_cur = st_ref[t], sc_scr[...]
            dyb = _pair_sum(eye2b * dy_t.astype(BF16), ones2)
            vb = _pair_sum(eye2b * vhi_t.astype(BF16), ones2) + _pair_sum(eye2b * vlo_t.astype(BF16), ones2)
            sa = -_pair_sum(s_prev * kk_t, ones2)
            dS = ds_scr[...] + dyb * r_t
            dsa = _pair_sum(dS * kka_t, ones2)
            ds_scr[...] = dS * w_t - dsa * kk_t
            sc_scr[...] = s_prev
            for ref, val in zip((dr_ref, dw_ref, dk2_ref, dv_ref, dkk_ref, dkka_ref),
                                (s_cur * dyb, dS * s_prev, dS * vb, eye2 * _pair_sum(dS * k2_t, ones2),
                                 -(s_prev * dsa), dS * sa)):
                ref[pl.ds(t, 1), :] = _pair_flat(colsum(val))
            return carry

        lax.fori_loop(0, tc, step, 0, unroll=SCAN_UNROLL_BWD)

        if scatter is not None:
            @pl.when(pl.program_id(0) == nchunk - 1)
            def _():
                for cp in copies:
                    cp.wait()

    in_specs = [seq] * 6 + [st_spec, one_state, seq]
    out_specs = [seq] * 6
    out_shape = [jax.ShapeDtypeStruct((T, W), F32)] * 6
    scratch = [pltpu.VMEM((N_PAIRS, HEAD_DIM, 128), F32)] * 2 + [pltpu.VMEM((tc, W), F32)] * 2
    args = [r, w, k2, v, kk, kka, states, final_state, dy]
    if scatter is not None:
        in_specs += [HBM_SPEC] * nb
        out_specs += [HBM_SPEC] * nb
        out_shape += [jax.ShapeDtypeStruct((npeer, *s.shape[2:]), s.dtype) for s in scatter]
        scratch += [pltpu.SemaphoreType.DMA((npeer * nb,)), pltpu.SemaphoreType.DMA((npeer * nb,))]
        args += list(scatter)
    return pl.pallas_call(
        body, name="rwkv_scan_bwd", grid=(nchunk,), in_specs=in_specs, out_specs=out_specs, out_shape=out_shape,
        scratch_shapes=scratch, compiler_params=_cp(("arbitrary",)),
    )(*args)


def _mix_common(y, r, k2, v, lnx_w, lnx_b, r_k):
    yc = y - _segsum(y) * (1.0 / HEAD_DIM)
    rstd = lax.rsqrt(_segsum(yc * yc) * (1.0 / HEAD_DIM) + LNX_EPS)
    yhat = yc * rstd
    s = _segsum(r * k2 * r_k)
    return rstd, yhat, s, yhat * lnx_w + lnx_b + s * v


def mix_gate_fwd(y, r, k2, v, g, y_mem, lnx_w, lnx_b, r_k):
    T = y.shape[0]

    def fn(i, nblk, y, r, k2, v, g, ym, lw, lb, rk):
        mix = _mix_common(y, r, k2, v, lw, lb, rk)[3]
        return jnp.concatenate([mix * g, ym], axis=1)

    return _rowmap(fn, name="mix_gate_fwd", T=T, tb=256,
                   ins=[("row", z) for z in (y, r, k2, v, g, y_mem)] + [("const", c) for c in (lnx_w, lnx_b, r_k)],
                   outs=[("row", RWKV_WIDTH + MEM_WIDTH, MMD)])[0]


def mix_gate_bwd(y, r, k2, v, g, dycat, lnx_w, lnx_b, r_k):
    T = y.shape[0]

    def fn(i, nblk, y, r, k2, v, g, dyc, lw, lb, rk):
        rstd, yhat, s, mix = _mix_common(y, r, k2, v, lw, lb, rk)
        dmix = dyc * g
        dyh = dmix * lw
        inv = 1.0 / HEAD_DIM
        dy = rstd * (dyh - _segsum(dyh) * inv - yhat * (_segsum(dyh * yhat) * inv))
        ds = _segsum(dmix * v)
        cs = lambda z: jnp.sum(z, axis=0, keepdims=True)
        return (dy, ds * k2 * rk, ds * r * rk, dmix * s, dyc * mix, cs(dmix * yhat), cs(dmix), cs(ds * r * k2))

    return _rowmap(fn, name="mix_gate_bwd", T=T, tb=256,
                   ins=[("row", z) for z in (y, r, k2, v, g)] + [("row", dycat, RWKV_WIDTH, 0)]
                   + [("const", c) for c in (lnx_w, lnx_b, r_k)],
                   outs=[("row", RWKV_WIDTH, F32)] * 5, accs=[((1, RWKV_WIDTH), F32)] * 3)


def _head_rms(x):
    ms = _segsum(x * x) * (1.0 / HEAD_DIM)
    r = lax.rsqrt(ms + RMS_EPS)
    return r, x * r


def _head_rms_bwd(dxn_g, r, xh):
    return r * (dxn_g - xh * (_segsum(dxn_g * xh) * (1.0 / HEAD_DIM)))


def mem_kv_fwd(mem, norm_g, w_kv, k_norm_t, *, name):
    def body(mem_ref, g_ref, w_ref, kn_ref, k_out, v_out):
        _, xh = _rms_stats(mem_ref[...])
        kv = _dot(xh * g_ref[...], w_ref[...])
        _, kh = _head_rms(kv[:, :MEM_WIDTH])
        k_out[...] = kh * kn_ref[...]
        v_out[...] = kv[:, MEM_WIDTH:]

    return pl.pallas_call(
        body, name=name, out_shape=[jax.ShapeDtypeStruct((N_MEM, MEM_WIDTH), F32)] * 2, compiler_params=_cp(),
    )(mem, norm_g, w_kv, k_norm_t)


def mem_kv_bwd(mem, norm_g, w_kv, k_norm_t, dkn, dv, *, name):
    fold = _fold_ones(MEM_WIDTH)

    def body(mem_ref, g_ref, w_ref, kn_ref, fo_ref, dkn_ref, dv_ref, dw_out, dg_out, dkg_out):
        _, xh = _rms_stats(mem_ref[...])
        hm = xh * g_ref[...]
        kv = _dot(hm, w_ref[...])
        r, kh = _head_rms(kv[:, :MEM_WIDTH])
        dkn = dkn_ref[...]
        dkg_out[...] = _fold_heads(jnp.sum(dkn * kh, axis=0, keepdims=True), fo_ref[...])
        dkraw = _head_rms_bwd(dkn * kn_ref[...], r, kh)
        dkv = jnp.concatenate([dkraw, dv_ref[...]], axis=1)
        dw_out[...] = _dot_tn(hm, dkv)
        dg_out[...] = jnp.sum(_dot_nt(dkv, w_ref[...]) * xh, axis=0, keepdims=True)

    return pl.pallas_call(
        body, name=name,
        out_shape=[jax.ShapeDtypeStruct((D_MODEL, 2 * MEM_WIDTH), F32), jax.ShapeDtypeStruct((1, D_MODEL), F32),
                   jax.ShapeDtypeStruct((1, HEAD_DIM), F32)],
        compiler_params=_cp(),
    )(mem, norm_g, w_kv, k_norm_t, fold, dkn, dv)


def _mem_scores(qn, kn, masks, h):
    s = _dot_nt(qn * masks[h], kn) * (1.0 / math.sqrt(HEAD_DIM))
    s = s - jnp.max(s, axis=-1, keepdims=True)
    e = jnp.exp(s)
    return e / jnp.sum(e, axis=-1, keepdims=True)


def mem_attn_fwd(p, colblock, kn, v, q_norm_t, *, name):
    T = p.shape[0]
    masks = _head_masks(MEM_WIDTH)

    def fn(i, nblk, q, kn, v, qg, masks):
        _, qh = _head_rms(q)
        qn = qh * qg
        out = jnp.zeros(q.shape, F32)
        for h in range(MEM_WIDTH // HEAD_DIM):
            out = out + _dot(_mem_scores(qn, kn, masks, h), v * masks[h])
        return out

    return _rowmap(fn, name=name, T=T, tb=512,
                   ins=[("row", p, MEM_WIDTH, colblock)] + [("const", c) for c in (kn, v, q_norm_t, masks)],
                   outs=[("row", MEM_WIDTH, F32)])[0]


def mem_attn_bwd(p, colblock, kn, v, q_norm_t, dycat, dcolblock, *, name):
    T = p.shape[0]
    masks, fold = _head_masks(MEM_WIDTH), _fold_ones(MEM_WIDTH)
    scale = 1.0 / math.sqrt(HEAD_DIM)

    def fn(i, nblk, q, dy, kn, v, qg, masks, fo):
        r, qh = _head_rms(q)
        qn = qh * qg
        dqn = jnp.zeros(q.shape, F32)
        dkn = jnp.zeros(kn.shape, F32)
        dv = jnp.zeros(v.shape, F32)
        for h in range(MEM_WIDTH // HEAD_DIM):
            pr = _mem_scores(qn, kn, masks, h)
            dyh = dy * masks[h]
            dpr = _dot_nt(dyh, v)
            ds = pr * (dpr - jnp.sum(dpr * pr, axis=-1, keepdims=True)) * scale
            dqn = dqn + _dot(ds, kn * masks[h])
            dkn = dkn + _dot_tn(ds, qn * masks[h])
            dv = dv + _dot_tn(pr, dyh)
        dqg = _fold_heads(jnp.sum(dqn * qh, axis=0, keepdims=True), fo)
        return _head_rms_bwd(dqn * qg, r, qh), dkn, dv, dqg

    return _rowmap(fn, name=name, T=T, tb=512,
                   ins=[("row", p, MEM_WIDTH, colblock), ("row", dycat, MEM_WIDTH, dcolblock)]
                   + [("const", c) for c in (kn, v, q_norm_t, masks, fold)],
                   outs=[("row", MEM_WIDTH, F32)],
                   accs=[((N_MEM, MEM_WIDTH), F32), ((N_MEM, MEM_WIDTH), F32), ((1, HEAD_DIM), F32)])


def _ffn_conv(i, u, halo, cw, cb):
    up1 = _shift_down(u, _row_pick(halo, 7), i == 0)
    up2 = _shift_down(up1, _row_pick(halo, 6), i == 0)
    c = cb + cw[0] * up2 + cw[1] * up1 + cw[2] * u
    return up1, up2, c[:, :D_FF], c[:, D_FF:]


def ffn_act_fwd(u, cw, cb, *, name):
    T = u.shape[0]

    def fn(i, nblk, u, halo, c0, c1, c2, cb):
        _, _, gate, val = _ffn_conv(i, u, halo, (c0, c1, c2), cb)
        return jax.nn.silu(gate) * val

    return _rowmap(fn, name=name, T=T, tb=128, ins=[("row", u), ("prev", u)] + [("const", c) for c in (*cw, cb)],
                   outs=[("row", D_FF, MMD)])[0]


def ffn_act_bwd(u, cw, cb, dz, *, name):
    T = u.shape[0]
    tb = 128

    def fn(i, nblk, u, halo, unext, c0, c1, c2, cb, dz, dznext):
        ue = jnp.concatenate([u, unext], axis=0)
        dze = jnp.concatenate([dz, jnp.where(i == nblk - 1, 0.0, 1.0) * dznext], axis=0)
        up1, up2, gate, val = _ffn_conv(i, ue, halo, (c0, c1, c2), cb)
        sg = jax.nn.sigmoid(gate)
        dce = jnp.concatenate([dze * val * sg * (1.0 + gate * (1.0 - sg)), dze * gate * sg], axis=1)
        rows = dce.shape[0]
        du = (c2 * dce + c1 * pltpu.roll(dce, rows - 1, axis=0) + c0 * pltpu.roll(dce, rows - 2, axis=0))[:tb]
        dc = dce[:tb]
        s = lambda z: jnp.sum(z, axis=0, keepdims=True)
        return du, s(dc * up2[:tb]), s(dc * up1[:tb]), s(dc * u), s(dc)

    return _rowmap(fn, name=name, T=T, tb=tb,
                   ins=[("row", u), ("prev", u), ("next", u)] + [("const", c) for c in (*cw, cb)]
                   + [("row", dz), ("next", dz)],
                   outs=[("row", 2 * D_FF, MMD)], accs=[((1, 2 * D_FF), F32)] * 4)


def _rope_swap(z):
    lane = lax.broadcasted_iota(jnp.int32, z.shape, 1) % HEAD_DIM
    w = z.shape[1]
    return jnp.where(lane < HEAD_DIM // 2, pltpu.roll(z, w - HEAD_DIM // 2, axis=1), pltpu.roll(z, HEAD_DIM // 2, axis=1))


def rope_tables(T):
    inv = (np.float32(ROPE_THETA) ** (-np.arange(0, HEAD_DIM, 2, dtype=np.float32) / np.float32(HEAD_DIM))).astype(np.float32)
    ang = (np.arange(T, dtype=np.float32)[:, None] * inv[None, :]).astype(np.float64)
    cos, sin = np.cos(ang).astype(np.float32), np.sin(ang).astype(np.float32)
    return (jnp.asarray(np.concatenate([cos, cos, cos, cos], axis=1)),
            jnp.asarray(np.concatenate([-sin, sin, -sin, sin], axis=1)))


def _rope_wide(t):
    return jnp.tile(t, (1, DIL_WIDTH // t.shape[1]))


def qk_fwd(kvp, pb, kg_t, qg_t, cos, sin):
    T = kvp.shape[0]

    def fn(i, nblk, kraw, vraw, qraw, kg, qg, c, s):
        c, s = _rope_wide(c), _rope_wide(s)
        outs = []
        for raw, g in ((qraw, qg), (kraw, kg)):
            _, xh = _head_rms(raw)
            z = xh * g
            outs.append(z * c + _rope_swap(z) * s)
        return outs[0], outs[1], vraw

    return _rowmap(fn, name="qk_fwd", T=T, tb=256,
                   ins=[("row", kvp, DIL_WIDTH, 0), ("row", kvp, DIL_WIDTH, 1), ("row", pb, DIL_WIDTH, 0)]
                   + [("const", kg_t), ("const", qg_t), ("row", cos), ("row", sin)],
                   outs=[("row", DIL_WIDTH, MMD)] * 3)


def qk_bwd(kvp, pb, kg_t, qg_t, cos, sin, dq, dk, dv, dq_mem):
    T = kvp.shape[0]
    fold = _fold_ones(DIL_WIDTH)

    def fn(i, nblk, kraw, qraw, kg, qg, c, s, fo, dq, dk, dv, dqm):
        c, s = _rope_wide(c), _rope_wide(s)
        res, dgs = [], []
        for raw, g, d in ((qraw, qg, dq), (kraw, kg, dk)):
            r, xh = _head_rms(raw)
            dz = d * c + _rope_swap(d * s)
            dgs.append(_fold_heads(jnp.sum(dz * xh, axis=0, keepdims=True), fo))
            res.append(_head_rms_bwd(dz * g, r, xh))
        return (jnp.concatenate([res[0], dqm], axis=1), jnp.concatenate([res[1], dv], axis=1), dgs[0], dgs[1])

    return _rowmap(fn, name="qk_bwd", T=T, tb=256,
                   ins=[("row", kvp, DIL_WIDTH, 0), ("row", pb, DIL_WIDTH, 0), ("const", kg_t), ("const", qg_t),
                        ("row", cos), ("row", sin), ("const", fold),
                        ("row", dq), ("row", dk), ("row", dv), ("row", dq_mem)],
                   outs=[("row", DIL_WIDTH + MEM_WIDTH, MMD), ("row", 2 * DIL_WIDTH, MMD)],
                   accs=[((1, HEAD_DIM), F32)] * 2)


def _band(kind):
    i = lax.broadcasted_iota(jnp.int32, (DIL_BLOCK, DIL_BLOCK), 0)
    j = lax.broadcasted_iota(jnp.int32, (DIL_BLOCK, DIL_BLOCK), 1)
    return (j <= i) if kind == "cur" else (j >= i)


def dil_attn_fwd(q, k, v, seq_blocks, *, name):
    T, W = q.shape
    nb = T // DIL_BLOCK
    masks = _head_masks(W)
    cur = pl.BlockSpec((DIL_BLOCK, W), lambda n: (n, 0))
    prv = pl.BlockSpec((DIL_BLOCK, W), lambda n: (jnp.maximum(n - 1, 0), 0))
    scale = 1.0 / math.sqrt(HEAD_DIM)

    def body(q_ref, kc_ref, kp_ref, vc_ref, vp_ref, m_ref, o_ref, l_ref):
        n = pl.program_id(0)
        has_prev = (n % seq_blocks) != 0
        q = q_ref[...].astype(F32)
        kc, kp = kc_ref[...].astype(F32), kp_ref[...].astype(F32)
        vc, vp = vc_ref[...].astype(F32), vp_ref[...].astype(F32)
        ok_c = _band("cur")
        ok_p = jnp.logical_and(_band("prev"), has_prev)
        o = jnp.zeros((DIL_BLOCK, W), F32)
        lse = jnp.zeros((DIL_BLOCK, W), F32)
        for h in range(W // HEAD_DIM):
            mh = m_ref[h]
            qh = q * mh
            sc = jnp.where(ok_c, _dot_nt(qh, kc) * scale, NEG_INF)
            sp = jnp.where(ok_p, _dot_nt(qh, kp) * scale, NEG_INF)
            mx = jnp.maximum(jnp.max(sc, axis=-1, keepdims=True), jnp.max(sp, axis=-1, keepdims=True))
            ec, ep = jnp.exp(sc - mx), jnp.exp(sp - mx)
            den = jnp.sum(ec, axis=-1, keepdims=True) + jnp.sum(ep, axis=-1, keepdims=True)
            o = o + (_dot(ec, vc * mh) + _dot(ep, vp * mh)) / den
            lse = lse + (mx + jnp.log(den)) * mh
        o_ref[...] = o
        l_ref[...] = lse

    return pl.pallas_call(
        body, name=name, grid=(nb,), in_specs=[cur, cur, prv, cur, prv, pl.BlockSpec(masks.shape, lambda n: (0, 0, 0))],
        out_specs=[cur, cur], out_shape=[jax.ShapeDtypeStruct((T, W), F32)] * 2,
        compiler_params=_cp(("parallel",)),
    )(q, k, k, v, v, masks)


def dil_attn_bwd(q, k, v, o, lse, do, dlse, seq_blocks, *, name):
    T, W = q.shape
    nb = T // DIL_BLOCK
    masks = _head_masks(W)
    cur = pl.BlockSpec((DIL_BLOCK, W), lambda n: (n, 0))
    prv = pl.BlockSpec((DIL_BLOCK, W), lambda n: (jnp.maximum(n - 1, 0), 0))
    nxt = pl.BlockSpec((DIL_BLOCK, W), lambda n: (jnp.minimum(n + 1, nb - 1), 0))
    scale = 1.0 / math.sqrt(HEAD_DIM)

    def body(qc_ref, qn_ref, kc_ref, kp_ref, vc_ref, vp_ref, oc_ref, on_ref, lc_ref, ln_ref, doc_ref, don_ref,
             dlc_ref, dln_ref, m_ref, dq_ref, dk_ref, dv_ref):
        n = pl.program_id(0)
        has_prev = (n % seq_blocks) != 0
        has_next = jnp.logical_and(((n + 1) % seq_blocks) != 0, n + 1 < nb)
        f = lambda ref: ref[...].astype(F32)
        qc, qn, kc, kp, vc, vp = f(qc_ref), f(qn_ref), f(kc_ref), f(kp_ref), f(vc_ref), f(vp_ref)
        doc, don = doc_ref[...], don_ref[...]
        ok_c = _band("cur")
        ok_p = jnp.logical_and(_band("prev"), has_prev)
        ok_n = jnp.logical_and(_band("prev"), has_next)
        dq = jnp.zeros((DIL_BLOCK, W), F32)
        dk = jnp.zeros((DIL_BLOCK, W), F32)
        dv = jnp.zeros((DIL_BLOCK, W), F32)

        def side(qh, kk, vv, doh, lse_h, corr, ok):
            s = _dot_nt(qh, kk) * scale
            pr = jnp.where(ok, jnp.exp(jnp.where(ok, s, NEG_INF) - lse_h), 0.0)
            ds = pr * (_dot_nt(doh, vv) + corr) * scale
            return pr, ds

        for h in range(W // HEAD_DIM):
            mh = m_ref[h]
            red = lambda z: jnp.sum(z * mh, axis=-1, keepdims=True)
            qh, doh = qc * mh, doc * mh
            lse_h = red(lc_ref[...]) * (1.0 / HEAD_DIM)
            corr = red(dlc_ref[...]) - red(doc * oc_ref[...])
            pr_c, ds_c = side(qh, kc, vc * mh, doh, lse_h, corr, ok_c)
            _, ds_p = side(qh, kp, vp * mh, doh, lse_h, corr, ok_p)
            dq = dq + _dot(ds_c, kc * mh) + _dot(ds_p, kp * mh)
            dk = dk + _dot_tn(ds_c, qh)
            dv = dv + _dot_tn(pr_c, doh)
            qh2, doh2 = qn * mh, don * mh
            lse_2 = red(ln_ref[...]) * (1.0 / HEAD_DIM)
            corr2 = red(dln_ref[...]) - red(don * on_ref[...])
            pr_n, ds_n = side(qh2, kc, vc * mh, doh2, lse_2, corr2, ok_n)
            dk = dk + _dot_tn(ds_n, qh2)
            dv = dv + _dot_tn(pr_n, doh2)
        dq_ref[...] = dq
        dk_ref[...] = dk
        dv_ref[...] = dv

    return pl.pallas_call(
        body, name=name, grid=(nb,),
        in_specs=[cur, nxt, cur, prv, cur, prv, cur, nxt, cur, nxt, cur, nxt, cur, nxt,
                  pl.BlockSpec(masks.shape, lambda n: (0, 0, 0))],
        out_specs=[cur] * 3, out_shape=[jax.ShapeDtypeStruct((T, W), F32)] * 3,
        compiler_params=_cp(("parallel",)),
    )(q, q, k, k, v, v, o, o, lse, lse, do, do, dlse, dlse, masks)


def _mix_weights(ls):
    m = jnp.maximum(jnp.maximum(ls[0], ls[1]), ls[2])
    es = [jnp.exp(l - m) for l in ls]
    den = es[0] + es[1] + es[2]
    return [e / den for e in es]


def mix_fwd(os_, ls, y_mem):
    T = y_mem.shape[0]

    def fn(i, nblk, o0, o1, o2, l0, l1, l2, ym):
        w = _mix_weights((l0, l1, l2))
        return jnp.concatenate([w[0] * o0 + w[1] * o1 + w[2] * o2, ym], axis=1)

    return _rowmap(fn, name="mix_fwd", T=T, tb=512, ins=[("row", z) for z in (*os_, *ls, y_mem)],
                   outs=[("row", 2 * MEM_WIDTH, MMD)])[0]


def mix_bwd(os_, ls, dycat):
    T = dycat.shape[0]

    def fn(i, nblk, o0, o1, o2, l0, l1, l2, dy):
        w = _mix_weights((l0, l1, l2))
        os3 = (o0, o1, o2)
        dws = [dy * o for o in os3]
        tot = w[0] * dws[0] + w[1] * dws[1] + w[2] * dws[2]
        return tuple(wg * dy for wg in w) + tuple(wg * (dw - tot) for wg, dw in zip(w, dws))

    return _rowmap(fn, name="mix_bwd", T=T, tb=512,
                   ins=[("row", z) for z in (*os_, *ls)] + [("row", dycat, MEM_WIDTH, 0)],
                   outs=[("row", MEM_WIDTH, F32)] * 6)


def loss_fwd_bwd(y, target):
    T, D = y.shape

    def fn(i, nblk, y, t):
        e = y - t
        return e * (1.0 / D), jnp.zeros((8, 128), F32) + jnp.sum(e * e) * (0.5 / D)

    return _rowmap(fn, name="loss", T=T, tb=512, ins=[("row", y), ("row", target)], outs=[("row", D, F32)],
                   accs=[((8, 128), F32)])


def _to_residues(z, dil):
    T, W = z.shape
    return z.reshape(T // dil, dil, W).transpose(1, 0, 2).reshape(T, W)


def _from_residues(z, dil):
    T, W = z.shape
    return z.reshape(dil, T // dil, W).transpose(1, 0, 2).reshape(T, W)


def _pad_rows(w, rows):
    return jnp.concatenate([w, jnp.zeros((rows - w.shape[0], w.shape[1]), w.dtype)], axis=0)


def _tile_heads(g, width):
    return jnp.tile(g.reshape(1, HEAD_DIM), (1, width // HEAD_DIM))


def _conv_rows(W, i):
    return [W["ffn_conv_w"][i][j:j + 1] for j in range(3)]


def _ffn_fwd(x, i, W):
    hn = rms_fwd(x, [W["ffn_norm"][i:i + 1]], name=f"ffn_rms{i}")[0]
    u = _mm(hn, W["ffn_w_up"][i], name=f"ffn_up{i}")
    z = ffn_act_fwd(u, _conv_rows(W, i), W["ffn_conv_b"][i:i + 1], name=f"ffn_act{i}")
    out = _mm(z, W["ffn_w_down"][i], add=x, name=f"ffn_down{i}")
    return out, (x, hn, u, z)


def _ffn_bwd(dout, i, W, saved, G):
    x, hn, u, z = saved
    dz = _mm(dout, W["ffn_w_down"][i], tb=True, name=f"ffn_down_dx{i}")
    G["ffn_w_down"][i] = _mm(z, dout, ta=True, name=f"ffn_down_dw{i}")
    du, dw0, dw1, dw2, db = ffn_act_bwd(u, _conv_rows(W, i), W["ffn_conv_b"][i:i + 1], dz, name=f"ffn_act_bwd{i}")
    G["ffn_conv_w"][i] = jnp.concatenate([dw0, dw1, dw2], axis=0)
    G["ffn_conv_b"][i] = db[0]
    dhn = _mm(du, W["ffn_w_up"][i], tb=True, name=f"ffn_up_dx{i}")
    G["ffn_w_up"][i] = _mm(hn, du, ta=True, name=f"ffn_up_dw{i}")
    dx, dg = rms_bwd(x, [W["ffn_norm"][i:i + 1]], [dhn], dout, name=f"ffn_rms_bwd{i}")
    G["ffn_norm"][i] = dg[0]
    return dx


def local_step(x, mem, target, W, late=None):
    T = x.shape[0]
    W = dict(W)
    G = {"ffn_w_down": [None, None], "ffn_w_up": [None, None], "ffn_conv_w": [None, None],
         "ffn_conv_b": [None, None], "ffn_norm": [None, None], "attn_norm": [None, None], "mem_norm": [None, None],
         "mem_w_kv": [None, None], "mem_q_norm": [None, None], "mem_k_norm": [None, None]}
    mu, w0, a0 = W["a_mu"], W["a_w0"], W["a_a0"]
    w2p, a2p, g2p = (_pad_rows(W["a_w2"][0], LORA_WIDTH),
                     jnp.concatenate([jnp.zeros((64, RWKV_WIDTH), MMD), W["a_a2"][0],
                                      jnp.zeros((128, RWKV_WIDTH), MMD)], axis=0),
                     jnp.concatenate([jnp.zeros((128, RWKV_WIDTH), MMD), W["a_g2"][0]], axis=0))
    k_k, k_a, lnx_w, lnx_b = W["a_k_k"], W["a_k_a"], W["a_lnx_w"], W["a_lnx_b"]
    r_k = W["a_r_k"].reshape(1, RWKV_WIDTH)

    h0 = rms_fwd(x, [W["attn_norm"][0:1]], name="attn_rms0")[0]
    p = _mm(h0, W["a_w_in"][0], name="a_in")
    r, w, k2, v, kk, kka, g = rwkv_pre_fwd(p, mu, w0, a0, w2p, a2p, g2p, k_k, k_a)
    if late is None:
        y, states, final_state = scan_fwd(r, w, k2, v, kk, kka)
    else:
        y, states, final_state, *gathered = scan_fwd(r, w, k2, v, kk, kka, gather=late[0])
        W.update(late[1](gathered))
    memkv = []
    for i in range(2):
        memkv.append(mem_kv_fwd(mem, W["mem_norm"][i:i + 1], W["mem_w_kv"][i], _tile_heads(W["mem_k_norm"][i], MEM_WIDTH),
                                name=f"mem_kv{i}"))
    qg0 = _tile_heads(W["mem_q_norm"][0], MEM_WIDTH)
    y_mem0 = mem_attn_fwd(p, SHIFT_WIDTH // MEM_WIDTH, memkv[0][0], memkv[0][1], qg0, name="mem_attn0")
    ycat0 = mix_gate_fwd(y, r, k2, v, g, y_mem0, lnx_w, lnx_b, r_k)
    x1 = _mm(ycat0, W["a_w_out"][0], add=x, name="a_out")
    x2, ffn0 = _ffn_fwd(x1, 0, W)

    h1, hkv = rms_fwd(x2, [W["attn_norm"][1:2], W["kv_norm"].reshape(1, -1)], name="attn_rms1")
    kvp = _mm(hkv, W["kv_w"], name="kv_in")
    pb = _mm(h1, W["b_w_in"][0], name="b_in")
    cos, sin = rope_tables(T)
    kg_t, qg_t = _tile_heads(W["kv_k_norm"], DIL_WIDTH), _tile_heads(W["b_q_norm"][0], DIL_WIDTH)
    q, ksh, vsh = qk_fwd(kvp, pb, kg_t, qg_t, cos, sin)
    os_, ls, grp = [], [], []
    for gi, (win, dil) in enumerate(DIL_GROUPS):
        sl = slice(gi * MEM_WIDTH, (gi + 1) * MEM_WIDTH)
        qg_, kg_, vg_ = (_to_residues(z[:, sl], dil) for z in (q, ksh, vsh))
        o_r, l_r = dil_attn_fwd(qg_, kg_, vg_, T // dil // DIL_BLOCK, name=f"dil_fwd{gi}")
        grp.append((qg_, kg_, vg_, o_r, l_r))
        os_.append(_from_residues(o_r, dil))
        ls.append(_from_residues(l_r, dil))
    qg1 = _tile_heads(W["mem_q_norm"][1], MEM_WIDTH)
    y_mem1 = mem_attn_fwd(pb, DIL_WIDTH // MEM_WIDTH, memkv[1][0], memkv[1][1], qg1, name="mem_attn1")
    ycat1 = mix_fwd(os_, ls, y_mem1)
    x3 = _mm(ycat1, W["b_w_out"][0], add=x2, name="b_out")
    x4, ffn1 = _ffn_fwd(x3, 1, W)

    dx4, loss = loss_fwd_bwd(x4, target)

    dx3 = _ffn_bwd(dx4, 1, W, ffn1, G)
    dycat1 = _mm(dx3, W["b_w_out"][0], tb=True, name="b_out_dx")
    G["b_w_out"] = _mm(ycat1, dx3, ta=True, name="b_out_dw")[None]
    dq_mem1, dkn1, dvm1, dqg1 = mem_attn_bwd(pb, DIL_WIDTH // MEM_WIDTH, memkv[1][0], memkv[1][1], qg1, dycat1, 1,
                                             name="mem_attn_bwd1")
    G["mem_q_norm"][1] = dqg1[0]
    d_os_ls = mix_bwd(os_, ls, dycat1)
    dqs, dks, dvs = [], [], []
    for gi, (win, dil) in enumerate(DIL_GROUPS):
        qg_, kg_, vg_, o_r, l_r = grp[gi]
        do_r, dl_r = _to_residues(d_os_ls[gi], dil), _to_residues(d_os_ls[3 + gi], dil)
        dq_r, dk_r, dv_r = dil_attn_bwd(qg_, kg_, vg_, o_r, l_r, do_r, dl_r, T // dil // DIL_BLOCK, name=f"dil_bwd{gi}")
        dqs.append(_from_residues(dq_r, dil))
        dks.append(_from_residues(dk_r, dil))
        dvs.append(_from_residues(dv_r, dil))
    dq, dk, dv = (jnp.concatenate(z, axis=1) for z in (dqs, dks, dvs))
    dpb, dkvp, dqn_g, dkn_g = qk_bwd(kvp, pb, kg_t, qg_t, cos, sin, dq, dk, dv, dq_mem1)
    G["b_q_norm"] = dqn_g
    G["kv_k_norm"] = dkn_g[0]
    dh1 = _mm(dpb, W["b_w_in"][0], tb=True, name="b_in_dx")
    G["b_w_in"] = _mm(h1, dpb, ta=True, name="b_in_dw")[None]
    dhkv = _mm(dkvp, W["kv_w"], tb=True, name="kv_in_dx")
    G["kv_w"] = _mm(hkv, dkvp, ta=True, name="kv_in_dw")
    dx2, dg1, dgkv = rms_bwd(x2, [W["attn_norm"][1:2], W["kv_norm"].reshape(1, -1)], [dh1, dhkv], dx3,
                             name="attn_rms_bwd1")
    G["attn_norm"][1] = dg1[0]
    G["kv_norm"] = dgkv[0]

    dx1 = _ffn_bwd(dx2, 0, W, ffn0, G)
    dycat0 = _mm(dx1, W["a_w_out"][0], tb=True, name="a_out_dx")
    G["a_w_out"] = _mm(ycat0, dx1, ta=True, name="a_out_dw")[None]
    dq_mem0, dkn0, dvm0, dqg0 = mem_attn_bwd(p, SHIFT_WIDTH // MEM_WIDTH, memkv[0][0], memkv[0][1], qg0, dycat0,
                                             RWKV_WIDTH // MEM_WIDTH, name="mem_attn_bwd0")
    G["mem_q_norm"][0] = dqg0[0]
    dy, dr_b, dk2_b, dv_b, dg, dlw, dlb, drk = mix_gate_bwd(y, r, k2, v, g, dycat0, lnx_w, lnx_b, r_k)
    for i, (dkn, dvm) in enumerate(((dkn0, dvm0), (dkn1, dvm1))):
        dwkv, dgm, dkg = mem_kv_bwd(mem, W["mem_norm"][i:i + 1], W["mem_w_kv"][i],
                                    _tile_heads(W["mem_k_norm"][i], MEM_WIDTH), dkn, dvm, name=f"mem_kv_bwd{i}")
        G["mem_w_kv"][i], G["mem_norm"][i], G["mem_k_norm"][i] = dwkv, dgm[0], dkg[0]
    late_out = None
    if late is None:
        dr, dw, dk2, dv, dkk, dkka = scan_bwd(r, w, k2, v, kk, kka, states, final_state, dy)
    else:
        pieces = late[2](G)
        dr, dw, dk2, dv, dkk, dkka, *received = scan_bwd(r, w, k2, v, kk, kka, states, final_state, dy, scatter=pieces)
        late_out = (received, pieces)
    dxs, dmu, dw0, da0, dw2p, da2p, dg2p, dk_k, dk_a = rwkv_pre_bwd(
        p, mu, w0, a0, w2p, a2p, g2p, k_k, k_a, (dr, dr_b), dw, (dk2, dk2_b), (dv, dv_b), dkk, dkka, dg)
    dp = shift_bwd(dxs, mu, dq_mem0)
    G.update(a_mu=dmu, a_w0=dw0, a_a0=da0, a_w2=dw2p[None, :64], a_a2=da2p[None, 64:128], a_g2=dg2p[None, 128:],
             a_k_k=dk_k, a_k_a=dk_a, a_r_k=drk.reshape(1, RWKV_HEADS, HEAD_DIM), a_lnx_w=dlw, a_lnx_b=dlb)
    dh0 = _mm(dp, W["a_w_in"][0], tb=True, name="a_in_dx")
    G["a_w_in"] = _mm(h0, dp, ta=True, name="a_in_dw")[None]
    grad_x, dg0 = rms_bwd(x, [W["attn_norm"][0:1]], [dh0], dx1, name="attn_rms_bwd0")
    G["attn_norm"][0] = dg0[0]
    for n in list(G):
        if isinstance(G[n], list):
            G[n] = jnp.stack(G[n], axis=0)
    return loss, grad_x, G, late_out


HBM_SPEC = pl.BlockSpec(memory_space=pltpu.HBM)


def _mesh_pos():
    return lax.axis_index("x"), lax.axis_index("y"), lax.axis_index("c")


def _other_chips(x, y):
    return [(1 - x, y), (x, 1 - y), (1 - x, 1 - y)]


def _remote(send_sems, recv_sems, k, src, dst, to):
    return pltpu.make_async_remote_copy(src_ref=src, dst_ref=dst, send_sem=send_sems.at[k], recv_sem=recv_sems.at[k],
                                        device_id=to, device_id_type=MESH)


def _comm_call(body, name, ins, out_shape, n_remote):
    scratch = [pltpu.SemaphoreType.DMA((n_remote,)), pltpu.SemaphoreType.DMA((n_remote,))]
    return pl.pallas_call(body, name=name, in_specs=[HBM_SPEC] * len(ins), out_specs=[HBM_SPEC] * len(out_shape),
                          out_shape=out_shape, scratch_shapes=scratch)(*ins)


def comm_gather(wbig, wsm):
    def body(wb, ws, ob, os_, send_sems, recv_sems):
        x, y, c = _mesh_pos()
        s = 2 * x + y
        me, sibling = (x, y, c), (x, y, 1 - c)
        chips = _other_chips(x, y)
        rc = functools.partial(_remote, send_sems, recv_sems)
        first = []
        for j, (cx, cy) in enumerate(chips):
            first.append(rc(j, wb.at[c], ob.at[s, c], (cx, cy, c)))
            first.append(rc(6 + j, ws, os_.at[s], (cx, cy, c)))
        for cp in first:
            cp.start()
        passed = []
        for j, (cx, cy) in enumerate(chips):
            blk = ob.at[2 * cx + cy, c]
            rc(j, blk, blk, me).wait_recv()
            passed.append(rc(3 + j, blk, blk, sibling))
            passed[-1].start()
        for j, (cx, cy) in enumerate(chips):
            blk = ob.at[2 * cx + cy, 1 - c]
            rc(3 + j, blk, blk, me).wait_recv()
            sb = os_.at[2 * cx + cy]
            rc(6 + j, sb, sb, me).wait_recv()
        for cp in first + passed:
            cp.wait_send()

    out_shape = [jax.ShapeDtypeStruct((N_CHIPS, *wbig.shape), wbig.dtype),
                 jax.ShapeDtypeStruct((N_CHIPS, *wsm.shape), wsm.dtype)]
    return _comm_call(body, "comm_gather", [wbig, wsm], out_shape, 9)


def comm_pair_exchange(gb, gs):
    def body(gb_ref, gs_ref, rb_ref, rs_ref, send_sems, recv_sems):
        x, y, c = _mesh_pos()
        sibling = (x, y, 1 - c)
        rc = functools.partial(_remote, send_sems, recv_sems)
        cps = [rc(r, gb_ref.at[r, 1 - c], rb_ref.at[r], sibling) for r in range(N_CHIPS)]
        cps.append(rc(N_CHIPS, gs_ref.at[1 - c], rs_ref, sibling))
        for cp in cps:
            cp.start()
        for cp in cps:
            cp.wait()

    out_shape = [jax.ShapeDtypeStruct((N_CHIPS, *gb.shape[2:]), gb.dtype), jax.ShapeDtypeStruct(gs.shape[1:], gs.dtype)]
    return _comm_call(body, "comm_pair_exchange", [gb, gs], out_shape, N_CHIPS + 1)


def comm_chip_exchange(hb, hs):
    def body(hb_ref, hs_ref, qb_ref, qs_ref, send_sems, recv_sems):
        x, y, c = _mesh_pos()
        s = 2 * x + y
        me = (x, y, c)
        chips = _other_chips(x, y)
        rc = functools.partial(_remote, send_sems, recv_sems)
        cps = []
        for j, (cx, cy) in enumerate(chips):
            cps.append(rc(j, hb_ref.at[2 * cx + cy], qb_ref.at[s], (cx, cy, c)))
            cps.append(rc(3 + j, hs_ref, qs_ref.at[s], (cx, cy, c)))
        for cp in cps:
            cp.start()
        for j, (cx, cy) in enumerate(chips):
            blk = qb_ref.at[2 * cx + cy]
            rc(j, blk, blk, me).wait_recv()
            sb = qs_ref.at[2 * cx + cy]
            rc(3 + j, sb, sb, me).wait_recv()
        for cp in cps:
            cp.wait_send()

    out_shape = [jax.ShapeDtypeStruct(hb.shape, hb.dtype), jax.ShapeDtypeStruct((N_CHIPS, *hs.shape), hs.dtype)]
    return _comm_call(body, "comm_chip_exchange", [hb, hs], out_shape, 6)


def comm_pair_share(halves):
    n = len(halves)

    def body(*refs):
        x, y, c = _mesh_pos()
        send_sems, recv_sems = refs[2 * n], refs[2 * n + 1]
        cps = [_remote(send_sems, recv_sems, k, refs[k], refs[n + k], (x, y, 1 - c)) for k in range(n)]
        for cp in cps:
            cp.start()
        for cp in cps:
            cp.wait()

    out_shape = [jax.ShapeDtypeStruct(h.shape, h.dtype) for h in halves]
    return _comm_call(body, "comm_pair_share", list(halves), out_shape, n)


def add_pairs(a, b, out_dtype, *, name, tb):
    T, L = a.shape
    return _rowmap(lambda i, n, p, q: p + q, name=name, T=T, tb=tb, ins=[("row", a), ("row", b)],
                   outs=[("row", L, out_dtype)])[0]


def add_chips(parts, *, name, tb):
    T, L = parts[0].shape

    def fn(i, n, *ps):
        acc = ps[0].astype(F32)
        for p in ps[1:]:
            acc = acc + p.astype(F32)
        return acc

    return _rowmap(fn, name=name, T=T, tb=tb, ins=[("row", p) for p in parts], outs=[("row", L, F32)])[0]


def adamw(g, w, m, v, *, name, tb):
    T, L = g.shape

    def fn(i, n, g, w, m, v):
        m2 = ADAM_B1 * m + (1.0 - ADAM_B1) * g
        v2 = ADAM_B2 * v + (1.0 - ADAM_B2) * (g * g)
        m_hat = m2 / (1.0 - ADAM_B1 ** ADAM_STEP)
        v_hat = v2 / (1.0 - ADAM_B2 ** ADAM_STEP)
        return -ADAM_LR * (m_hat / (jnp.sqrt(v_hat) + ADAM_EPS) + ADAM_WD * w), m2, v2

    return _rowmap(fn, name=name, T=T, tb=tb, ins=[("row", z) for z in (g, w, m, v)], outs=[("row", L, F32)] * 3)


BIG_LANES = 1024
SMALL_LANES = 128


def _flat_cat(arrs, total, dtype):
    parts = [a.reshape(-1).astype(dtype) for a in arrs]
    n = sum(p.shape[0] for p in parts)
    assert n <= total, (n, total)
    if n < total:
        parts.append(jnp.zeros((total - n,), dtype))
    return jnp.concatenate(parts)


def _split_flat(flat, shapes):
    out, off = [], 0
    for shp in shapes:
        n = math.prod(shp)
        out.append(flat[off:off + n].reshape(shp))
        off += n
    return out


def _round_up(n, m):
    return -(-n // m) * m


def _full_shape(shard_shape, axis):
    return tuple(d * N_CHIPS if i == axis else d for i, d in enumerate(shard_shape))


def kernel(x, mem, attn_norm, a_w_in, a_mu, a_w0, a_w2, a_a0, a_a2, a_g2, a_k_k, a_k_a, a_r_k, a_lnx_w, a_lnx_b, a_w_out, kv_norm, kv_w, kv_k_norm, b_w_in, b_q_norm, b_w_out, mem_norm, mem_w_kv, mem_q_norm, mem_k_norm, ffn_norm, ffn_w_up, ffn_conv_w, ffn_conv_b, ffn_w_down, loss_target, m_attn_norm, m_a_w_in, m_a_mu, m_a_w0, m_a_w2, m_a_a0, m_a_a2, m_a_g2, m_a_k_k, m_a_k_a, m_a_r_k, m_a_lnx_w, m_a_lnx_b, m_a_w_out, m_kv_norm, m_kv_w, m_kv_k_norm, m_b_w_in, m_b_q_norm, m_b_w_out, m_mem_norm, m_mem_w_kv, m_mem_q_norm, m_mem_k_norm, m_ffn_norm, m_ffn_w_up, m_ffn_conv_w, m_ffn_conv_b, m_ffn_w_down, v_attn_norm, v_a_w_in, v_a_mu, v_a_w0, v_a_w2, v_a_a0, v_a_a2, v_a_g2, v_a_k_k, v_a_k_a, v_a_r_k, v_a_lnx_w, v_a_lnx_b, v_a_w_out, v_kv_norm, v_kv_w, v_kv_k_norm, v_b_w_in, v_b_q_norm, v_b_w_out, v_mem_norm, v_mem_w_kv, v_mem_q_norm, v_mem_k_norm, v_ffn_norm, v_ffn_w_up, v_ffn_conv_w, v_ffn_conv_b, v_ffn_w_down):
    args = (attn_norm, a_w_in, a_mu, a_w0, a_w2, a_a0, a_a2, a_g2, a_k_k, a_k_a, a_r_k, a_lnx_w, a_lnx_b, a_w_out, kv_norm, kv_w, kv_k_norm, b_w_in, b_q_norm, b_w_out, mem_norm, mem_w_kv, mem_q_norm, mem_k_norm, ffn_norm, ffn_w_up, ffn_conv_w, ffn_conv_b, ffn_w_down)
    ms = (m_attn_norm, m_a_w_in, m_a_mu, m_a_w0, m_a_w2, m_a_a0, m_a_a2, m_a_g2, m_a_k_k, m_a_k_a, m_a_r_k, m_a_lnx_w, m_a_lnx_b, m_a_w_out, m_kv_norm, m_kv_w, m_kv_k_norm, m_b_w_in, m_b_q_norm, m_b_w_out, m_mem_norm, m_mem_w_kv, m_mem_q_norm, m_mem_k_norm, m_ffn_norm, m_ffn_w_up, m_ffn_conv_w, m_ffn_conv_b, m_ffn_w_down)
    vs = (v_attn_norm, v_a_w_in, v_a_mu, v_a_w0, v_a_w2, v_a_a0, v_a_a2, v_a_g2, v_a_k_k, v_a_k_a, v_a_r_k, v_a_lnx_w, v_a_lnx_b, v_a_w_out, v_kv_norm, v_kv_w, v_kv_k_norm, v_b_w_in, v_b_q_norm, v_b_w_out, v_mem_norm, v_mem_w_kv, v_mem_q_norm, v_mem_k_norm, v_ffn_norm, v_ffn_w_up, v_ffn_conv_w, v_ffn_conv_b, v_ffn_w_down)
    w_sh, m_sh, v_sh = (dict(zip(WEIGHTS, z)) for z in (args, ms, vs))
    xi, yi, ci = _mesh_pos()
    chip = 2 * xi + yi
    axes = {**dict(BIG), **dict(SMALL_SHARDED)}
    early_names = [n for n, _ in BIG if n in EARLY_BIG]
    late_names = [n for n, _ in BIG if n not in EARLY_BIG and n != NATURAL_BIG]
    ss_names, ss_axes = [n for n, _ in SMALL_SHARDED], dict(SMALL_SHARDED)
    shapes_of = lambda names: [w_sh[n].shape for n in names]
    count = lambda names: sum(math.prod(s) for s in shapes_of(names))
    n_early, n_late = count(early_names), count(late_names)
    assert n_early % (2 * 16 * BIG_LANES) == 0 and n_late % (2 * 16 * BIG_LANES) == 0
    mh, mh_late = n_early // (2 * BIG_LANES), n_late // (2 * BIG_LANES)
    n_ss = _round_up(count(ss_names), 8 * SMALL_LANES)

    def shard_pack(names, total, dtype, source):
        return _flat_cat([source[n] for n in names], total, dtype)

    def join(pieces, n):
        if pieces[0].ndim == 3:
            return [jnp.concatenate([p[l] for p in pieces], axis=axes[n] - 1) for l in range(pieces[0].shape[0])]
        return jnp.concatenate(pieces, axis=axes[n])

    def unshard(names, gathered):
        per_chip = [_split_flat(gathered[j], shapes_of(names)) for j in range(N_CHIPS)]
        return {n: join([per_chip[j][k] for j in range(N_CHIPS)], n) for k, n in enumerate(names)}

    def whole(g):
        return jnp.stack(g, axis=0) if isinstance(g, list) else g

    def by_chip(names, total, dtype, grads):
        parts = [jnp.split(whole(grads[n]), N_CHIPS, axis=axes[n]) for n in names]
        return jnp.stack([_flat_cat([p[j] for p in parts], total, dtype) for j in range(N_CHIPS)])

    wbig = shard_pack(early_names, n_early, MMD, w_sh).reshape(2, mh, BIG_LANES)
    wsm = shard_pack(ss_names, n_ss, F32, w_sh).reshape(-1, SMALL_LANES)
    wbig_all, wsm_all = comm_gather(wbig, wsm)
    wbig_all = lax.dynamic_update_index_in_dim(wbig_all, wbig, chip, 0).reshape(N_CHIPS, -1)
    wsm_all = lax.dynamic_update_index_in_dim(wsm_all, wsm, chip, 0).reshape(N_CHIPS, -1)
    W = {n: w_sh[n] for n in SMALL_REPL}
    W.update(unshard(early_names, wbig_all))
    W.update(unshard(ss_names, wsm_all))
    for n in ("a_w2", "a_a2", "a_g2"):
        W[n] = [z.astype(MMD) for z in W[n]]
    wlate = shard_pack(late_names, n_late, MMD, w_sh).reshape(2, mh_late, BIG_LANES)
    nat_axis = axes[NATURAL_BIG]
    wnat = w_sh[NATURAL_BIG].astype(MMD)
    assert wnat.shape[0] == 2 and nat_axis != 0

    def unpack_late(gathered):
        full = lax.dynamic_update_index_in_dim(gathered[0], wlate, chip, 0)
        out = unshard(late_names, full.reshape(N_CHIPS, -1))
        nat = lax.dynamic_update_index_in_dim(gathered[1], wnat, chip, 0)
        out[NATURAL_BIG] = join([nat[j] for j in range(N_CHIPS)], NATURAL_BIG)
        return out

    def pack_late(grads):
        return [by_chip(late_names, n_late, BF16, grads).reshape(N_CHIPS, 2, mh_late, BIG_LANES),
                jnp.stack(jnp.split(whole(grads[NATURAL_BIG]).astype(BF16), N_CHIPS, axis=nat_axis))]

    loss_blk, grad_x, G, (received, pieces) = local_step(x[0], mem[0], loss_target[0], W,
                                                          late=([wlate, wnat], unpack_late, pack_late))
    loss = lax.psum(loss_blk[0, 0], ("x", "y", "c"))

    own_piece = lambda p: lax.dynamic_index_in_dim(lax.dynamic_index_in_dim(p, chip, 0, keepdims=False), ci, 0,
                                                   keepdims=False)
    gh_late = add_chips([received[0][k] for k in range(len(PEER_FLIPS))] + [own_piece(pieces[0])],
                        name="add_pieces_late", tb=32)
    gh_nat = add_chips([received[1][k] for k in range(len(PEER_FLIPS))] + [own_piece(pieces[1])],
                       name="add_pieces_natural", tb=32)
    gbig = by_chip(early_names, n_early, F32, G).reshape(N_CHIPS, 2, mh, BIG_LANES)
    sm_full_names = ss_names + list(SMALL_REPL)
    sm_full_shapes = [_full_shape(w_sh[n].shape, ss_axes[n]) for n in ss_names] + [w_sh[n].shape for n in SMALL_REPL]
    n_smf = _round_up(sum(math.prod(s) for s in sm_full_shapes), 2 * 8 * SMALL_LANES)
    msh = n_smf // (2 * SMALL_LANES)
    gsm = _flat_cat([G[n] for n in sm_full_names], n_smf, F32).reshape(2, msh, SMALL_LANES)
    rb, rs = comm_pair_exchange(gbig, gsm)
    mine_b = lax.dynamic_index_in_dim(gbig, ci, axis=1, keepdims=False)
    mine_s = lax.dynamic_index_in_dim(gsm, ci, axis=0, keepdims=False)
    hb = add_pairs(mine_b.reshape(-1, BIG_LANES), rb.reshape(-1, BIG_LANES), BF16, name="add_pairs_big", tb=128)
    hs = add_pairs(mine_s, rs, F32, name="add_pairs_small", tb=msh)
    hb = hb.reshape(N_CHIPS, mh, BIG_LANES)
    qb, qs = comm_chip_exchange(hb, hs)
    qb = lax.dynamic_update_index_in_dim(qb, lax.dynamic_index_in_dim(hb, chip, 0, keepdims=False), chip, 0)
    qs = lax.dynamic_update_index_in_dim(qs, hs, chip, 0)
    gh = add_chips([qb[j] for j in range(N_CHIPS)], name="add_chips_big", tb=32)
    gsh = add_chips([qs[j] for j in range(N_CHIPS)], name="add_chips_small", tb=msh)
    rh, rh_late, rh_nat, rsh = comm_pair_share([gh, gh_late, gh_nat, gsh])
    both = lambda mine_, theirs: jnp.where(ci == 0, jnp.stack([mine_, theirs]), jnp.stack([theirs, mine_]))
    gfull, gfull_late, gfull_nat, gsfull = both(gh, rh), both(gh_late, rh_late), both(gh_nat, rh_nat), both(gsh, rsh)

    res = {tag: {} for tag in ("grad", "delta", "new_m", "new_v")}
    big_grads = (list(zip(early_names, _split_flat(gfull.reshape(-1), shapes_of(early_names))))
                 + list(zip(late_names, _split_flat(gfull_late.reshape(-1), shapes_of(late_names))))
                 + [(NATURAL_BIG, gfull_nat)])
    for n, g in big_grads:
        shp = w_sh[n].shape
        rows = lambda z: z.reshape(-1, shp[-1])
        nrow = math.prod(shp[:-1])
        tb = next(t for t in (512, 256, 128, 64) if nrow % t == 0 and t * shp[-1] <= (1 << 19))
        outs = adamw(rows(g), rows(w_sh[n]), rows(m_sh[n]), rows(v_sh[n]), name=f"adamw_{n}", tb=tb)
        res["grad"][n] = g
        for tag, o in zip(("delta", "new_m", "new_v"), outs):
            res[tag][n] = o.reshape(shp)
    sm_full = dict(zip(sm_full_names, _split_flat(gsfull.reshape(-1), sm_full_shapes)))
    g_loc = {}
    for n in ss_names:
        size = w_sh[n].shape[ss_axes[n]]
        g_loc[n] = lax.dynamic_slice_in_dim(sm_full[n], chip * size, size, axis=ss_axes[n])
    for n in SMALL_REPL:
        g_loc[n] = sm_full[n]
    n_sml = _round_up(sum(math.prod(w_sh[n].shape) for n in sm_full_names), 8 * SMALL_LANES)
    pack_sm = lambda d: _flat_cat([d[n] for n in sm_full_names], n_sml, F32).reshape(-1, SMALL_LANES)
    d_sm, m_sm, v_sm = adamw(pack_sm(g_loc), pack_sm(w_sh), pack_sm(m_sh), pack_sm(v_sh), name="adamw_small",
                             tb=n_sml // SMALL_LANES)
    sm_loc_shapes = [w_sh[n].shape for n in sm_full_names]
    res["grad"].update(g_loc)
    for tag, smv in (("delta", d_sm), ("new_m", m_sm), ("new_v", v_sm)):
        res[tag].update(dict(zip(sm_full_names, _split_flat(smv.reshape(-1), sm_loc_shapes))))
    return (loss, grad_x[None], *[res[tag][n] for tag in ("grad", "delta", "new_m", "new_v") for n in WEIGHTS])
```

```python
import functools
import math

import numpy as np
import jax
import jax.numpy as jnp
from jax import lax
from jax.experimental import pallas as pl
from jax.experimental.pallas import tpu as pltpu

F32 = jnp.float32
BF16 = jnp.bfloat16
MMD = jnp.bfloat16

D_MODEL = 1024
HEAD_DIM = 64
N_MEM = 256
MEM_WIDTH = 256
RWKV_HEADS = 12
RWKV_WIDTH = 768
SHIFT_WIDTH = 2560
LORA_WIDTH = 256
DIL_WIDTH = 768
DIL_GROUPS = ((128, 1), (512, 4), (2048, 16))
DIL_BLOCK = 128
D_FF = 2816
ROPE_THETA = 10000.0
RMS_EPS = 1e-6
LNX_EPS = 64e-5
NEG_INF = -1e30
ADAM_LR = 0.001
ADAM_B1 = 0.9
ADAM_B2 = 0.999
ADAM_EPS = 1e-08
ADAM_WD = 0.01
ADAM_STEP = 10
N_CHIPS = 4
MESH = pl.DeviceIdType.MESH
VMEM_LIMIT_MB = 56
SCAN_CHUNK = 64
SCAN_UNROLL = 32
SCAN_UNROLL_BWD = 16

BIG = (("a_w_in", 2), ("a_w_out", 1), ("kv_w", 1), ("b_w_in", 1), ("b_w_out", 2), ("mem_w_kv", 1),
       ("ffn_w_up", 2), ("ffn_w_down", 1))
EARLY_BIG = ("a_w_in",)
NATURAL_BIG = "ffn_w_up"
SMALL_SHARDED = (("a_mu", 1), ("a_w0", 1), ("a_w2", 2), ("a_a0", 1), ("a_a2", 2), ("a_g2", 2), ("a_k_k", 1),
                 ("a_k_a", 1), ("a_lnx_w", 1), ("a_lnx_b", 1), ("ffn_conv_w", 2))
SMALL_REPL = ("attn_norm", "a_r_k", "kv_norm", "kv_k_norm", "b_q_norm", "mem_norm", "mem_q_norm", "mem_k_norm",
              "ffn_norm", "ffn_conv_b")
WEIGHTS = ("attn_norm", "a_w_in", "a_mu", "a_w0", "a_w2", "a_a0", "a_a2", "a_g2", "a_k_k", "a_k_a", "a_r_k",
           "a_lnx_w", "a_lnx_b", "a_w_out", "kv_norm", "kv_w", "kv_k_norm", "b_w_in", "b_q_norm", "b_w_out",
           "mem_norm", "mem_w_kv", "mem_q_norm", "mem_k_norm", "ffn_norm", "ffn_w_up", "ffn_conv_w", "ffn_conv_b",
           "ffn_w_down")


def _cp(sem=None, **kw):
    return pltpu.CompilerParams(dimension_semantics=sem, vmem_limit_bytes=VMEM_LIMIT_MB << 20, **kw)


def _tile(n, cands=(512, 256, 128)):
    for c in cands:
        if n % c == 0:
            return c
    return n


def _mm(a, b, *, name, ta=False, tb=False, add=None, out_dtype=F32):
    K, M = a.shape if ta else a.shape[::-1]
    N = b.shape[0] if tb else b.shape[1]
    assert K == (b.shape[1] if tb else b.shape[0])
    tm, tn = _tile(M, (512, 256, 128) if ta else (1024, 512, 256, 128)), _tile(N, (512, 1408, 256, 128))
    bytes_of = lambda z: z.size * z.dtype.itemsize
    kept = 1 if bytes_of(b) + bytes_of(a) * (N // tn) < bytes_of(a) + bytes_of(b) * (M // tm) else 0
    mi, ni = ((lambda o, i: i), (lambda o, i: o)) if kept else ((lambda o, i: o), (lambda o, i: i))
    grid = (N // tn, M // tm) if kept else (M // tm, N // tn)
    a_blk, b_blk = ((K, tm) if ta else (tm, K)), ((tn, K) if tb else (K, tn))
    a_spec = pl.BlockSpec(a_blk, (lambda o, i: (0, mi(o, i))) if ta else (lambda o, i: (mi(o, i), 0)))
    b_spec = pl.BlockSpec(b_blk, (lambda o, i: (ni(o, i), 0)) if tb else (lambda o, i: (0, ni(o, i))))
    o_spec = pl.BlockSpec((tm, tn), lambda o, i: (mi(o, i), ni(o, i)))
    dn = (((0,) if ta else (1,), (1,) if tb else (0,)), ((), ()))
    has_add = add is not None
    cache = (b if kept else a).dtype != MMD

    def body(*refs):
        vals = [refs[0], refs[1]]
        o_ref = refs[2 + has_add]
        if cache:
            scr = refs[-1]

            @pl.when(pl.program_id(1) == 0)
            def _():
                scr[...] = vals[kept][...].astype(MMD)

            vals[kept] = scr
        acc = lax.dot_general(vals[0][...].astype(MMD), vals[1][...].astype(MMD), dn, preferred_element_type=F32)
        if has_add:
            acc = acc + refs[2][...]
        o_ref[...] = acc.astype(o_ref.dtype)

    ins = [a, b] + ([add] if has_add else [])
    specs = [a_spec, b_spec] + ([o_spec] if has_add else [])
    return pl.pallas_call(
        body, name=name, grid=grid, in_specs=specs, out_specs=o_spec,
        out_shape=jax.ShapeDtypeStruct((M, N), out_dtype),
        scratch_shapes=[pltpu.VMEM(b_blk if kept else a_blk, MMD)] if cache else [],
        compiler_params=_cp(("parallel", "arbitrary")),
    )(*ins)


def _rowmap(fn, *, name, T, tb, ins, outs, accs=()):
    nblk = T // tb
    assert T % tb == 0 and tb % 8 == 0
    in_specs, args = [], []
    for spec in ins:
        kind, arr = spec[0], spec[1]
        w, cb = (spec[2], spec[3]) if len(spec) > 2 else (arr.shape[-1], 0)
        if kind == "row":
            in_specs.append(pl.BlockSpec((tb, w), lambda i, cb=cb: (i, cb)))
        elif kind == "prev":
            in_specs.append(pl.BlockSpec((8, w), lambda i, cb=cb: (jnp.maximum(i * (tb // 8) - 1, 0), cb)))
        elif kind == "next":
            in_specs.append(pl.BlockSpec((8, w), lambda i, cb=cb: (jnp.minimum((i + 1) * (tb // 8), T // 8 - 1), cb)))
        elif kind == "const":
            in_specs.append(pl.BlockSpec(arr.shape, lambda i, nd=arr.ndim: (0,) * nd))
        else:
            raise ValueError(kind)
        args.append(arr)
    out_shape, out_specs = [], []
    for kind, w, dt in outs:
        out_shape.append(jax.ShapeDtypeStruct((T, w), dt))
        out_specs.append(pl.BlockSpec((tb, w), lambda i: (i, 0)))
    for shp, dt in accs:
        out_shape.append(jax.ShapeDtypeStruct(shp, dt))
        out_specs.append(pl.BlockSpec(shp, lambda i, nd=len(shp): (0,) * nd))
    n_in, n_out = len(ins), len(outs)

    def body(*refs):
        i = pl.program_id(0)
        vals = [r[...] for r in refs[:n_in]]
        res = fn(i, nblk, *vals)
        if not isinstance(res, (tuple, list)):
            res = (res,)
        assert len(res) == n_out + len(accs), (name, len(res))
        for r, v in zip(refs[n_in:n_in + n_out], res[:n_out]):
            r[...] = v.astype(r.dtype)
        acc_refs = refs[n_in + n_out:]
        if acc_refs:
            @pl.when(i == 0)
            def _():
                for r in acc_refs:
                    r[...] = jnp.zeros(r.shape, r.dtype)

            for r, v in zip(acc_refs, res[n_out:]):
                r[...] += v

    res = pl.pallas_call(
        body, name=name, grid=(nblk,), in_specs=in_specs, out_specs=out_specs, out_shape=out_shape,
        compiler_params=_cp(("arbitrary",)),
    )(*args)
    return res


def _row_pick(halo, r):
    rid = lax.broadcasted_iota(jnp.int32, halo.shape, 0)
    return jnp.sum(jnp.where(rid == r, halo, 0.0), axis=0, keepdims=True)


def _shift_down(x, row_before, is_first):
    rid = lax.broadcasted_iota(jnp.int32, x.shape, 0)
    first = jnp.where(is_first, 0.0, 1.0) * row_before
    return jnp.where(rid == 0, first, pltpu.roll(x, 1, axis=0))


def _shift_up(x, row_after, is_last):
    n = x.shape[0]
    rid = lax.broadcasted_iota(jnp.int32, x.shape, 0)
    last = jnp.where(is_last, 0.0, 1.0) * row_after
    return jnp.where(rid == n - 1, last, pltpu.roll(x, n - 1, axis=0))


def _dot(a, b, dn=(((1,), (0,)), ((), ()))):
    return lax.dot_general(a.astype(MMD), b.astype(MMD), dn, preferred_element_type=F32)


def _dot_nt(a, b):
    return _dot(a, b, (((1,), (1,)), ((), ())))


def _dot_tn(a, b):
    return _dot(a, b, (((0,), (0,)), ((), ())))


def _dot_exact01(x, g01):
    hi = x.astype(BF16)
    lo = (x - hi.astype(F32)).astype(BF16)
    dn = (((1,), (0,)), ((), ()))
    return (lax.dot_general(hi, g01, dn, preferred_element_type=F32)
            + lax.dot_general(lo, g01, dn, preferred_element_type=F32))


def _fold_heads(v, fold):
    return _row_pick(_dot_exact01(jnp.broadcast_to(v, (8, v.shape[1])), fold), 0)


def _fold_ones(width):
    idx = np.arange(width) % HEAD_DIM
    return jnp.asarray((idx[:, None] == np.arange(HEAD_DIM)[None, :]).astype(np.float32), BF16)


def _head_masks(width):
    idx = np.arange(width) // HEAD_DIM
    return jnp.asarray((idx[None, :] == np.arange(width // HEAD_DIM)[:, None]).astype(np.float32)[:, None, :], F32)


def _rms_stats(x):
    r = lax.rsqrt(jnp.mean(x * x, axis=-1, keepdims=True) + RMS_EPS)
    return r, x * r


def rms_fwd(x, gains, *, name):
    T, D = x.shape

    def fn(i, nblk, xb, *gs):
        _, xh = _rms_stats(xb)
        return tuple(xh * g for g in gs)

    return _rowmap(fn, name=name, T=T, tb=512, ins=[("row", x)] + [("const", g) for g in gains],
                   outs=[("row", D, MMD)] * len(gains))


def rms_bwd(x, gains, dhs, dres, *, name):
    T, D = x.shape
    n = len(gains)

    def fn(i, nblk, xb, dr, *rest):
        gs, ds = rest[:n], rest[n:]
        r, xh = _rms_stats(xb)
        dx = dr
        dgs = []
        for g, dh in zip(gs, ds):
            dgs.append(jnp.sum(dh * xh, axis=0, keepdims=True))
            dxh = dh * g
            dx = dx + r * (dxh - xh * jnp.mean(dxh * xh, axis=-1, keepdims=True))
        return (dx, *dgs)

    return _rowmap(fn, name=name, T=T, tb=512,
                   ins=[("row", x), ("row", dres)] + [("const", g) for g in gains] + [("row", d) for d in dhs],
                   outs=[("row", D, F32)], accs=[((1, D), F32)] * n)


def _segsum(x):
    first = lax.broadcasted_iota(jnp.int32, (x.shape[0], 128), 1) < HEAD_DIM
    outs = []
    for p in range(x.shape[1] // 128):
        xs = x[:, p * 128:(p + 1) * 128]
        lo = jnp.sum(jnp.where(first, xs, 0.0), axis=-1, keepdims=True)
        hi = jnp.sum(jnp.where(first, 0.0, xs), axis=-1, keepdims=True)
        outs.append(jnp.where(first, lo, hi))
    return jnp.concatenate(outs, axis=1)


def _pre1_common(i, ps, halo, mu, w0, a0, w2p, a2p, g2p, k_k, k_a):
    prev = _shift_down(ps, _row_pick(halo, 7), i == 0)
    xs = ps + (prev - ps) * mu
    lo = xs[:, 3 * RWKV_WIDTH:]
    tl, sl = jnp.tanh(lo), jax.nn.sigmoid(lo)
    dec = w0 + _dot(tl, w2p)
    ain = a0 + _dot(lo, a2p)
    g = _dot(sl, g2p)
    wl = -jax.nn.softplus(-dec) - 0.5
    w = jnp.exp(-jnp.exp(wl))
    a = jax.nn.sigmoid(ain)
    k = xs[:, RWKV_WIDTH:2 * RWKV_WIDTH]
    z = k * k_k
    nrm = jnp.sqrt(_segsum(z * z))
    kk = z / jnp.maximum(nrm, 1e-12)
    return prev, xs, lo, tl, sl, dec, wl, w, a, g, k, nrm, kk


def rwkv_pre_fwd(p, mu, w0, a0, w2p, a2p, g2p, k_k, k_a):
    T = p.shape[0]

    def fn(i, nblk, ps, halo, mu, w0, a0, w2p, a2p, g2p, k_k, k_a):
        _, xs, _, _, _, _, _, w, a, g, k, _, kk = _pre1_common(i, ps, halo, mu, w0, a0, w2p, a2p, g2p, k_k, k_a)
        W = RWKV_WIDTH
        return xs[:, :W], w, k * (1.0 + (a - 1.0) * k_a), xs[:, 2 * W:3 * W], kk, kk * a, g

    return _rowmap(fn, name="rwkv_pre_fwd", T=T, tb=256,
                   ins=[("row", p, SHIFT_WIDTH, 0), ("prev", p, SHIFT_WIDTH, 0)]
                   + [("const", c) for c in (mu, w0, a0, w2p, a2p, g2p, k_k, k_a)],
                   outs=[("row", RWKV_WIDTH, F32)] * 7)


def rwkv_pre_bwd(p, mu, w0, a0, w2p, a2p, g2p, k_k, k_a, drs, dw, dk2s, dvs, dkk, dkka, dg):
    T = p.shape[0]

    def fn(i, nblk, ps, halo, mu, w0, a0, w2p, a2p, g2p, k_k, k_a, dr0, dr1, dw, dk20, dk21, dv0, dv1, dkk, dkka, dg):
        prev, xs, lo, tl, sl, dec, wl, w, a, g, k, nrm, kk = _pre1_common(i, ps, halo, mu, w0, a0, w2p, a2p, g2p, k_k, k_a)
        dk2 = dk20 + dk21
        dkk_t = dkk + dkka * a
        proj = jnp.where(nrm > 1e-12, kk * _segsum(dkk_t * kk), 0.0)
        dz = (dkk_t - proj) / jnp.maximum(nrm, 1e-12)
        dk = dz * k_k + dk2 * (1.0 + (a - 1.0) * k_a)
        da = dkka * kk + dk2 * k * k_a
        ddec = dw * (-w * jnp.exp(wl)) * jax.nn.sigmoid(-dec)
        dain = da * a * (1.0 - a)
        dlo = (_dot_nt(ddec, w2p) * (1.0 - tl * tl) + _dot_nt(dain, a2p) + _dot_nt(dg, g2p) * sl * (1.0 - sl))
        dxs = jnp.concatenate([dr0 + dr1, dk, dv0 + dv1, dlo], axis=1)
        s = lambda z: jnp.sum(z, axis=0, keepdims=True)
        return (dxs, s(dxs * (prev - ps)), s(ddec), s(dain), _dot_tn(tl, ddec), _dot_tn(lo, dain), _dot_tn(sl, dg),
                s(dz * k), s(dk2 * k * (a - 1.0)))

    return _rowmap(fn, name="rwkv_pre_bwd", T=T, tb=128,
                   ins=[("row", p, SHIFT_WIDTH, 0), ("prev", p, SHIFT_WIDTH, 0)]
                   + [("const", c) for c in (mu, w0, a0, w2p, a2p, g2p, k_k, k_a)]
                   + [("row", c) for c in (*drs, dw, *dk2s, *dvs, dkk, dkka, dg)],
                   outs=[("row", SHIFT_WIDTH, F32)],
                   accs=[((1, SHIFT_WIDTH), F32), ((1, RWKV_WIDTH), F32), ((1, RWKV_WIDTH), F32)]
                   + [((LORA_WIDTH, RWKV_WIDTH), F32)] * 3 + [((1, RWKV_WIDTH), F32)] * 2)


def shift_bwd(dxs, mu, dq_mem):
    T = dxs.shape[0]

    def fn(i, nblk, d, halo, mu, dq):
        nxt = _shift_up(d, _row_pick(halo, 0), i == nblk - 1)
        return jnp.concatenate([d * (1.0 - mu) + nxt * mu, dq], axis=1)

    return _rowmap(fn, name="shift_bwd", T=T, tb=256,
                   ins=[("row", dxs), ("next", dxs), ("const", mu), ("row", dq_mem)],
                   outs=[("row", SHIFT_WIDTH + MEM_WIDTH, MMD)])[0]


N_PAIRS = RWKV_HEADS // 2


def _pair_consts():
    row = lax.broadcasted_iota(jnp.int32, (HEAD_DIM, 128), 0)
    lane = lax.broadcasted_iota(jnp.int32, (HEAD_DIM, 128), 1)
    eye2 = jnp.logical_or(lane == row, lane == row + HEAD_DIM).astype(F32)
    li = lax.broadcasted_iota(jnp.int32, (128, 128), 0) < HEAD_DIM
    lj = lax.broadcasted_iota(jnp.int32, (128, 128), 1) < HEAD_DIM
    return eye2, (li == lj).astype(BF16)


def _pair_sum(p, ones2):
    n, m, l = p.shape
    s = lax.dot_general(p.reshape(n * m, l).astype(BF16), ones2, (((1,), (0,)), ((), ())), preferred_element_type=F32)
    return s.reshape(n, m, l)


def _pair_rows(row):
    return jnp.stack([row[:, p * 128:(p + 1) * 128] for p in range(N_PAIRS)], axis=0)


def _pair_flat(rows):
    return jnp.concatenate([rows[p] for p in range(N_PAIRS)], axis=1)


def _split_bf16(v):
    hi = v.astype(BF16).astype(F32)
    return hi, v - hi


def _gather_copies(srcs, dsts, send_sems, recv_sems):
    x, y, c = _mesh_pos()
    s = 2 * x + y
    me, sibling = (x, y, c), (x, y, 1 - c)
    rc = functools.partial(_remote, send_sems, recv_sems)
    ici, land, fwd, arrived = [], [], [], []
    for b, (src, dst) in enumerate(zip(srcs, dsts)):
        for j, (cx, cy) in enumerate(_other_chips(x, y)):
            k = 6 * b + j
            ici.append(rc(k, src.at[c], dst.at[s, c], (cx, cy, c)))
            blk, blk2 = dst.at[2 * cx + cy, c], dst.at[2 * cx + cy, 1 - c]
            land.append(rc(k, blk, blk, me))
            fwd.append(rc(k + 3, blk, blk, sibling))
            arrived.append(rc(k + 3, blk2, blk2, me))
    return ici, land, fwd, arrived


def scan_fwd(r, w, k2, v, kk, kka, gather=None):
    T, W = r.shape
    tc = SCAN_CHUNK
    nchunk = T // tc
    seq = pl.BlockSpec((tc, W), lambda i: (i, 0))
    one_state = pl.BlockSpec((N_PAIRS, HEAD_DIM, 128), lambda i: (0, 0, 0))
    nb = 0 if gather is None else len(gather)

    def body(r_ref, w_ref, k2_ref, v_ref, kk_ref, kka_ref, *rest):
        if gather is None:
            y_ref, st_ref, fin_ref, s_scr, vhi_scr, vlo_scr = rest
        else:
            srcs, (y_ref, st_ref, fin_ref), dsts = rest[:nb], rest[nb:nb + 3], rest[nb + 3:2 * nb + 3]
            s_scr, vhi_scr, vlo_scr, send_sems, recv_sems = rest[2 * nb + 3:]
            ici, land, fwd, arrived = _gather_copies(srcs, dsts, send_sems, recv_sems)

            @pl.when(pl.program_id(0) == 0)
            def _():
                for cp in ici:
                    cp.start()

            @pl.when(pl.program_id(0) == nchunk // 2)
            def _():
                for a, f in zip(land, fwd):
                    a.wait_recv()
                    f.start()

        @pl.when(pl.program_id(0) == 0)
        def _():
            s_scr[...] = jnp.zeros(s_scr.shape, F32)

        vhi_scr[...], vlo_scr[...] = _split_bf16(v_ref[...])
        eye2, ones2 = _pair_consts()
        eye2b = eye2.astype(BF16)

        def step(t, carry):
            r_t, w_t, k2_t, kk_t, kka_t, vhi_t, vlo_t = (
                _pair_rows(ref[pl.ds(t, 1), :]) for ref in (r_ref, w_ref, k2_ref, kk_ref, kka_ref, vhi_scr, vlo_scr))
            S = s_scr[...]
            sa = -_pair_sum(S * kk_t, ones2)
            vb = _pair_sum(eye2b * vhi_t.astype(BF16), ones2) + _pair_sum(eye2b * vlo_t.astype(BF16), ones2)
            S2 = S * w_t + sa * kka_t + vb * k2_t
            y_ref[pl.ds(t, 1), :] = _pair_flat(jnp.sum(eye2 * _pair_sum(S2 * r_t, ones2), axis=1, keepdims=True))
            s_scr[...] = S2
            st_ref[t] = S
            return carry

        lax.fori_loop(0, tc, step, 0, unroll=SCAN_UNROLL)
        fin_ref[...] = s_scr[...]

        if gather is not None:
            @pl.when(pl.program_id(0) == nchunk - 1)
            def _():
                for a in arrived:
                    a.wait_recv()
                for cp in ici + fwd:
                    cp.wait_send()

    in_specs = [seq] * 6
    out_specs = [seq, pl.BlockSpec((tc, N_PAIRS, HEAD_DIM, 128), lambda i: (i, 0, 0, 0)), one_state]
    out_shape = [jax.ShapeDtypeStruct((T, W), F32), jax.ShapeDtypeStruct((T, N_PAIRS, HEAD_DIM, 128), F32),
                 jax.ShapeDtypeStruct((N_PAIRS, HEAD_DIM, 128), F32)]
    scratch = [pltpu.VMEM((N_PAIRS, HEAD_DIM, 128), F32), pltpu.VMEM((tc, W), F32), pltpu.VMEM((tc, W), F32)]
    args = [r, w, k2, v, kk, kka]
    if gather is not None:
        in_specs += [HBM_SPEC] * nb
        out_specs += [HBM_SPEC] * nb
        out_shape += [jax.ShapeDtypeStruct((N_CHIPS, *g.shape), g.dtype) for g in gather]
        scratch += [pltpu.SemaphoreType.DMA((6 * nb,)), pltpu.SemaphoreType.DMA((6 * nb,))]
        args += list(gather)
    return pl.pallas_call(
        body, name="rwkv_scan_fwd", grid=(nchunk,), in_specs=in_specs, out_specs=out_specs, out_shape=out_shape,
        scratch_shapes=scratch, compiler_params=_cp(("arbitrary",)),
    )(*args)


PEER_FLIPS = tuple((fx, fy, fc) for fx in (0, 1) for fy in (0, 1) for fc in (0, 1))[1:]


def scan_bwd(r, w, k2, v, kk, kka, states, final_state, dy, scatter=None):
    T, W = r.shape
    tc = SCAN_CHUNK
    nchunk = T // tc
    seq = pl.BlockSpec((tc, W), lambda i: (nchunk - 1 - i, 0))
    st_spec = pl.BlockSpec((tc, N_PAIRS, HEAD_DIM, 128), lambda i: (nchunk - 1 - i, 0, 0, 0))
    one_state = pl.BlockSpec((N_PAIRS, HEAD_DIM, 128), lambda i: (0, 0, 0))
    nb, npeer = (0 if scatter is None else len(scatter)), len(PEER_FLIPS)

    def body(r_ref, w_ref, k2_ref, v_ref, kk_ref, kka_ref, st_ref, fin_ref, dy_ref, *rest):
        if scatter is None:
            dr_ref, dw_ref, dk2_ref, dv_ref, dkk_ref, dkka_ref, ds_scr, sc_scr, vhi_scr, vlo_scr = rest
        else:
            srcs, dsts = rest[:nb], rest[nb + 6:2 * nb + 6]
            dr_ref, dw_ref, dk2_ref, dv_ref, dkk_ref, dkka_ref = rest[nb:nb + 6]
            ds_scr, sc_scr, vhi_scr, vlo_scr, send_sems, recv_sems = rest[2 * nb + 6:]
            x, y, c = _mesh_pos()
            copies = []
            for b, (src, dst) in enumerate(zip(srcs, dsts)):
                for k, (fx, fy, fc) in enumerate(PEER_FLIPS):
                    px, py, pc = (1 - x if fx else x), (1 - y if fy else y), (1 - c if fc else c)
                    copies.append(_remote(send_sems, recv_sems, npeer * b + k, src.at[2 * px + py, pc], dst.at[k],
                                          (px, py, pc)))

            @pl.when(pl.program_id(0) == 0)
            def _():
                for cp in copies:
                    cp.start()

        @pl.when(pl.program_id(0) == 0)
        def _():
            ds_scr[...] = jnp.zeros(ds_scr.shape, F32)
            sc_scr[...] = fin_ref[...]

        vhi_scr[...], vlo_scr[...] = _split_bf16(v_ref[...])
        eye2, ones2 = _pair_consts()
        eye2b = eye2.astype(BF16)
        colsum = lambda z: jnp.sum(z, axis=1, keepdims=True)

        def step(j, carry):
            t = tc - 1 - j
            r_t, w_t, k2_t, kk_t, kka_t, vhi_t, vlo_t, dy_t = (
                _pair_rows(ref[pl.ds(t, 1), :])
                for ref in (r_ref, w_ref, k2_ref, kk_ref, kka_ref, vhi_scr, vlo_scr, dy_ref))
            s_prev, s_cur = st_ref[t], sc_scr[...]
            dyb = _pair_sum(eye2b * dy_t.astype(BF16), ones2)
            vb = _pair_sum(eye2b * vhi_t.astype(BF16), ones2) + _pair_sum(eye2b * vlo_t.astype(BF16), ones2)
            sa = -_pair_sum(s_prev * kk_t, ones2)
            dS = ds_scr[...] + dyb * r_t
            dsa = _pair_sum(dS * kka_t, ones2)
            ds_scr[...] = dS * w_t - dsa * kk_t
            sc_scr[...] = s_prev
            for ref, val in zip((dr_ref, dw_ref, dk2_ref, dv_ref, dkk_ref, dkka_ref),
                                (s_cur * dyb, dS * s_prev, dS * vb, eye2 * _pair_sum(dS * k2_t, ones2),
                                 -(s_prev * dsa), dS * sa)):
                ref[pl.ds(t, 1), :] = _pair_flat(colsum(val))
            return carry

        lax.fori_loop(0, tc, step, 0, unroll=SCAN_UNROLL_BWD)

        if scatter is not None:
            @pl.when(pl.program_id(0) == nchunk - 1)
            def _():
                for cp in copies:
                    cp.wait()

    in_specs = [seq] * 6 + [st_spec, one_state, seq]
    out_specs = [seq] * 6
    out_shape = [jax.ShapeDtypeStruct((T, W), F32)] * 6
    scratch = [pltpu.VMEM((N_PAIRS, HEAD_DIM, 128), F32)] * 2 + [pltpu.VMEM((tc, W), F32)] * 2
    args = [r, w, k2, v, kk, kka, states, final_state, dy]
    if scatter is not None:
        in_specs += [HBM_SPEC] * nb
        out_specs += [HBM_SPEC] * nb
        out_shape += [jax.ShapeDtypeStruct((npeer, *s.shape[2:]), s.dtype) for s in scatter]
        scratch += [pltpu.SemaphoreType.DMA((npeer * nb,)), pltpu.SemaphoreType.DMA((npeer * nb,))]
        args += list(scatter)
    return pl.pallas_call(
        body, name="rwkv_scan_bwd", grid=(nchunk,), in_specs=in_specs, out_specs=out_specs, out_shape=out_shape,
        scratch_shapes=scratch, compiler_params=_cp(("arbitrary",)),
    )(*args)


def _mix_common(y, r, k2, v, lnx_w, lnx_b, r_k):
    yc = y - _segsum(y) * (1.0 / HEAD_DIM)
    rstd = lax.rsqrt(_segsum(yc * yc) * (1.0 / HEAD_DIM) + LNX_EPS)
    yhat = yc * rstd
    s = _segsum(r * k2 * r_k)
    return rstd, yhat, s, yhat * lnx_w + lnx_b + s * v


def mix_gate_fwd(y, r, k2, v, g, y_mem, lnx_w, lnx_b, r_k):
    T = y.shape[0]

    def fn(i, nblk, y, r, k2, v, g, ym, lw, lb, rk):
        mix = _mix_common(y, r, k2, v, lw, lb, rk)[3]
        return jnp.concatenate([mix * g, ym], axis=1)

    return _rowmap(fn, name="mix_gate_fwd", T=T, tb=256,
                   ins=[("row", z) for z in (y, r, k2, v, g, y_mem)] + [("const", c) for c in (lnx_w, lnx_b, r_k)],
                   outs=[("row", RWKV_WIDTH + MEM_WIDTH, MMD)])[0]


def mix_gate_bwd(y, r, k2, v, g, dycat, lnx_w, lnx_b, r_k):
    T = y.shape[0]

    def fn(i, nblk, y, r, k2, v, g, dyc, lw, lb, rk):
        rstd, yhat, s, mix = _mix_common(y, r, k2, v, lw, lb, rk)
        dmix = dyc * g
        dyh = dmix * lw
        inv = 1.0 / HEAD_DIM
        dy = rstd * (dyh - _segsum(dyh) * inv - yhat * (_segsum(dyh * yhat) * inv))
        ds = _segsum(dmix * v)
        cs = lambda z: jnp.sum(z, axis=0, keepdims=True)
        return (dy, ds * k2 * rk, ds * r * rk, dmix * s, dyc * mix, cs(dmix * yhat), cs(dmix), cs(ds * r * k2))

    return _rowmap(fn, name="mix_gate_bwd", T=T, tb=256,
                   ins=[("row", z) for z in (y, r, k2, v, g)] + [("row", dycat, RWKV_WIDTH, 0)]
                   + [("const", c) for c in (lnx_w, lnx_b, r_k)],
                   outs=[("row", RWKV_WIDTH, F32)] * 5, accs=[((1, RWKV_WIDTH), F32)] * 3)


def _head_rms(x):
    ms = _segsum(x * x) * (1.0 / HEAD_DIM)
    r = lax.rsqrt(ms + RMS_EPS)
    return r, x * r


def _head_rms_bwd(dxn_g, r, xh):
    return r * (dxn_g - xh * (_segsum(dxn_g * xh) * (1.0 / HEAD_DIM)))


def mem_kv_fwd(mem, norm_g, w_kv, k_norm_t, *, name):
    def body(mem_ref, g_ref, w_ref, kn_ref, k_out, v_out):
        _, xh = _rms_stats(mem_ref[...])
        kv = _dot(xh * g_ref[...], w_ref[...])
        _, kh = _head_rms(kv[:, :MEM_WIDTH])
        k_out[...] = kh * kn_ref[...]
        v_out[...] = kv[:, MEM_WIDTH:]

    return pl.pallas_call(
        body, name=name, out_shape=[jax.ShapeDtypeStruct((N_MEM, MEM_WIDTH), F32)] * 2, compiler_params=_cp(),
    )(mem, norm_g, w_kv, k_norm_t)


def mem_kv_bwd(mem, norm_g, w_kv, k_norm_t, dkn, dv, *, name):
    fold = _fold_ones(MEM_WIDTH)

    def body(mem_ref, g_ref, w_ref, kn_ref, fo_ref, dkn_ref, dv_ref, dw_out, dg_out, dkg_out):
        _, xh = _rms_stats(mem_ref[...])
        hm = xh * g_ref[...]
        kv = _dot(hm, w_ref[...])
        r, kh = _head_rms(kv[:, :MEM_WIDTH])
        dkn = dkn_ref[...]
        dkg_out[...] = _fold_heads(jnp.sum(dkn * kh, axis=0, keepdims=True), fo_ref[...])
        dkraw = _head_rms_bwd(dkn * kn_ref[...], r, kh)
        dkv = jnp.concatenate([dkraw, dv_ref[...]], axis=1)
        dw_out[...] = _dot_tn(hm, dkv)
        dg_out[...] = jnp.sum(_dot_nt(dkv, w_ref[...]) * xh, axis=0, keepdims=True)

    return pl.pallas_call(
        body, name=name,
        out_shape=[jax.ShapeDtypeStruct((D_MODEL, 2 * MEM_WIDTH), F32), jax.ShapeDtypeStruct((1, D_MODEL), F32),
                   jax.ShapeDtypeStruct((1, HEAD_DIM), F32)],
        compiler_params=_cp(),
    )(mem, norm_g, w_kv, k_norm_t, fold, dkn, dv)


def _mem_scores(qn, kn, masks, h):
    s = _dot_nt(qn * masks[h], kn) * (1.0 / math.sqrt(HEAD_DIM))
    s = s - jnp.max(s, axis=-1, keepdims=True)
    e = jnp.exp(s)
    return e / jnp.sum(e, axis=-1, keepdims=True)


def mem_attn_fwd(p, colblock, kn, v, q_norm_t, *, name):
    T = p.shape[0]
    masks = _head_masks(MEM_WIDTH)

    def fn(i, nblk, q, kn, v, qg, masks):
        _, qh = _head_rms(q)
        qn = qh * qg
        out = jnp.zeros(q.shape, F32)
        for h in range(MEM_WIDTH // HEAD_DIM):
            out = out + _dot(_mem_scores(qn, kn, masks, h), v * masks[h])
        return out

    return _rowmap(fn, name=name, T=T, tb=512,
                   ins=[("row", p, MEM_WIDTH, colblock)] + [("const", c) for c in (kn, v, q_norm_t, masks)],
                   outs=[("row", MEM_WIDTH, F32)])[0]


def mem_attn_bwd(p, colblock, kn, v, q_norm_t, dycat, dcolblock, *, name):
    T = p.shape[0]
    masks, fold = _head_masks(MEM_WIDTH), _fold_ones(MEM_WIDTH)
    scale = 1.0 / math.sqrt(HEAD_DIM)

    def fn(i, nblk, q, dy, kn, v, qg, masks, fo):
        r, qh = _head_rms(q)
        qn = qh * qg
        dqn = jnp.zeros(q.shape, F32)
        dkn = jnp.zeros(kn.shape, F32)
        dv = jnp.zeros(v.shape, F32)
        for h in range(MEM_WIDTH // HEAD_DIM):
            pr = _mem_scores(qn, kn, masks, h)
            dyh = dy * masks[h]
            dpr = _dot_nt(dyh, v)
            ds = pr * (dpr - jnp.sum(dpr * pr, axis=-1, keepdims=True)) * scale
            dqn = dqn + _dot(ds, kn * masks[h])
            dkn = dkn + _dot_tn(ds, qn * masks[h])
            dv = dv + _dot_tn(pr, dyh)
        dqg = _fold_heads(jnp.sum(dqn * qh, axis=0, keepdims=True), fo)
        return _head_rms_bwd(dqn * qg, r, qh), dkn, dv, dqg

    return _rowmap(fn, name=name, T=T, tb=512,
                   ins=[("row", p, MEM_WIDTH, colblock), ("row", dycat, MEM_WIDTH, dcolblock)]
                   + [("const", c) for c in (kn, v, q_norm_t, masks, fold)],
                   outs=[("row", MEM_WIDTH, F32)],
                   accs=[((N_MEM, MEM_WIDTH), F32), ((N_MEM, MEM_WIDTH), F32), ((1, HEAD_DIM), F32)])


def _ffn_conv(i, u, halo, cw, cb):
    up1 = _shift_down(u, _row_pick(halo, 7), i == 0)
    up2 = _shift_down(up1, _row_pick(halo, 6), i == 0)
    c = cb + cw[0] * up2 + cw[1] * up1 + cw[2] * u
    return up1, up2, c[:, :D_FF], c[:, D_FF:]


def ffn_act_fwd(u, cw, cb, *, name):
    T = u.shape[0]

    def fn(i, nblk, u, halo, c0, c1, c2, cb):
        _, _, gate, val = _ffn_conv(i, u, halo, (c0, c1, c2), cb)
        return jax.nn.silu(gate) * val

    return _rowmap(fn, name=name, T=T, tb=128, ins=[("row", u), ("prev", u)] + [("const", c) for c in (*cw, cb)],
                   outs=[("row", D_FF, MMD)])[0]


def ffn_act_bwd(u, cw, cb, dz, *, name):
    T = u.shape[0]
    tb = 128

    def fn(i, nblk, u, halo, unext, c0, c1, c2, cb, dz, dznext):
        ue = jnp.concatenate([u, unext], axis=0)
        dze = jnp.concatenate([dz, jnp.where(i == nblk - 1, 0.0, 1.0) * dznext], axis=0)
        up1, up2, gate, val = _ffn_conv(i, ue, halo, (c0, c1, c2), cb)
        sg = jax.nn.sigmoid(gate)
        dce = jnp.concatenate([dze * val * sg * (1.0 + gate * (1.0 - sg)), dze * gate * sg], axis=1)
        rows = dce.shape[0]
        du = (c2 * dce + c1 * pltpu.roll(dce, rows - 1, axis=0) + c0 * pltpu.roll(dce, rows - 2, axis=0))[:tb]
        dc = dce[:tb]
        s = lambda z: jnp.sum(z, axis=0, keepdims=True)
        return du, s(dc * up2[:tb]), s(dc * up1[:tb]), s(dc * u), s(dc)

    return _rowmap(fn, name=name, T=T, tb=tb,
                   ins=[("row", u), ("prev", u), ("next", u)] + [("const", c) for c in (*cw, cb)]
                   + [("row", dz), ("next", dz)],
                   outs=[("row", 2 * D_FF, MMD)], accs=[((1, 2 * D_FF), F32)] * 4)


def _rope_swap(z):
    lane = lax.broadcasted_iota(jnp.int32, z.shape, 1) % HEAD_DIM
    w = z.shape[1]
    return jnp.where(lane < HEAD_DIM // 2, pltpu.roll(z, w - HEAD_DIM // 2, axis=1), pltpu.roll(z, HEAD_DIM // 2, axis=1))


def rope_tables(T):
    inv = (np.float32(ROPE_THETA) ** (-np.arange(0, HEAD_DIM, 2, dtype=np.float32) / np.float32(HEAD_DIM))).astype(np.float32)
    ang = (np.arange(T, dtype=np.float32)[:, None] * inv[None, :]).astype(np.float64)
    cos, sin = np.cos(ang).astype(np.float32), np.sin(ang).astype(np.float32)
    return (jnp.asarray(np.concatenate([cos, cos, cos, cos], axis=1)),
            jnp.asarray(np.concatenate([-sin, sin, -sin, sin], axis=1)))


def _rope_wide(t):
    return jnp.tile(t, (1, DIL_WIDTH // t.shape[1]))


def qk_fwd(kvp, pb, kg_t, qg_t, cos, sin):
    T = kvp.shape[0]

    def fn(i, nblk, kraw, vraw, qraw, kg, qg, c, s):
        c, s = _rope_wide(c), _rope_wide(s)
        outs = []
        for raw, g in ((qraw, qg), (kraw, kg)):
            _, xh = _head_rms(raw)
            z = xh * g
            outs.append(z * c + _rope_swap(z) * s)
        return outs[0], outs[1], vraw

    return _rowmap(fn, name="qk_fwd", T=T, tb=256,
                   ins=[("row", kvp, DIL_WIDTH, 0), ("row", kvp, DIL_WIDTH, 1), ("row", pb, DIL_WIDTH, 0)]
                   + [("const", kg_t), ("const", qg_t), ("row", cos), ("row", sin)],
                   outs=[("row", DIL_WIDTH, MMD)] * 3)


def qk_bwd(kvp, pb, kg_t, qg_t, cos, sin, dq, dk, dv, dq_mem):
    T = kvp.shape[0]
    fold = _fold_ones(DIL_WIDTH)

    def fn(i, nblk, kraw, qraw, kg, qg, c, s, fo, dq, dk, dv, dqm):
        c, s = _rope_wide(c), _rope_wide(s)
        res, dgs = [], []
        for raw, g, d in ((qraw, qg, dq), (kraw, kg, dk)):
            r, xh = _head_rms(raw)
            dz = d * c + _rope_swap(d * s)
            dgs.append(_fold_heads(jnp.sum(dz * xh, axis=0, keepdims=True), fo))
            res.append(_head_rms_bwd(dz * g, r, xh))
        return (jnp.concatenate([res[0], dqm], axis=1), jnp.concatenate([res[1], dv], axis=1), dgs[0], dgs[1])

    return _rowmap(fn, name="qk_bwd", T=T, tb=256,
                   ins=[("row", kvp, DIL_WIDTH, 0), ("row", pb, DIL_WIDTH, 0), ("const", kg_t), ("const", qg_t),
                        ("row", cos), ("row", sin), ("const", fold),
                        ("row", dq), ("row", dk), ("row", dv), ("row", dq_mem)],
                   outs=[("row", DIL_WIDTH + MEM_WIDTH, MMD), ("row", 2 * DIL_WIDTH, MMD)],
                   accs=[((1, HEAD_DIM), F32)] * 2)


def _band(kind):
    i = lax.broadcasted_iota(jnp.int32, (DIL_BLOCK, DIL_BLOCK), 0)
    j = lax.broadcasted_iota(jnp.int32, (DIL_BLOCK, DIL_BLOCK), 1)
    return (j <= i) if kind == "cur" else (j >= i)


def dil_attn_fwd(q, k, v, seq_blocks, *, name):
    T, W = q.shape
    nb = T // DIL_BLOCK
    masks = _head_masks(W)
    cur = pl.BlockSpec((DIL_BLOCK, W), lambda n: (n, 0))
    prv = pl.BlockSpec((DIL_BLOCK, W), lambda n: (jnp.maximum(n - 1, 0), 0))
    scale = 1.0 / math.sqrt(HEAD_DIM)

    def body(q_ref, kc_ref, kp_ref, vc_ref, vp_ref, m_ref, o_ref, l_ref):
        n = pl.program_id(0)
        has_prev = (n % seq_blocks) != 0
        q = q_ref[...].astype(F32)
        kc, kp = kc_ref[...].astype(F32), kp_ref[...].astype(F32)
        vc, vp = vc_ref[...].astype(F32), vp_ref[...].astype(F32)
        ok_c = _band("cur")
        ok_p = jnp.logical_and(_band("prev"), has_prev)
        o = jnp.zeros((DIL_BLOCK, W), F32)
        lse = jnp.zeros((DIL_BLOCK, W), F32)
        for h in range(W // HEAD_DIM):
            mh = m_ref[h]
            qh = q * mh
            sc = jnp.where(ok_c, _dot_nt(qh, kc) * scale, NEG_INF)
            sp = jnp.where(ok_p, _dot_nt(qh, kp) * scale, NEG_INF)
            mx = jnp.maximum(jnp.max(sc, axis=-1, keepdims=True), jnp.max(sp, axis=-1, keepdims=True))
            ec, ep = jnp.exp(sc - mx), jnp.exp(sp - mx)
            den = jnp.sum(ec, axis=-1, keepdims=True) + jnp.sum(ep, axis=-1, keepdims=True)
            o = o + (_dot(ec, vc * mh) + _dot(ep, vp * mh)) / den
            lse = lse + (mx + jnp.log(den)) * mh
        o_ref[...] = o
        l_ref[...] = lse

    return pl.pallas_call(
        body, name=name, grid=(nb,), in_specs=[cur, cur, prv, cur, prv, pl.BlockSpec(masks.shape, lambda n: (0, 0, 0))],
        out_specs=[cur, cur], out_shape=[jax.ShapeDtypeStruct((T, W), F32)] * 2,
        compiler_params=_cp(("parallel",)),
    )(q, k, k, v, v, masks)


def dil_attn_bwd(q, k, v, o, lse, do, dlse, seq_blocks, *, name):
    T, W = q.shape
    nb = T // DIL_BLOCK
    masks = _head_masks(W)
    cur = pl.BlockSpec((DIL_BLOCK, W), lambda n: (n, 0))
    prv = pl.BlockSpec((DIL_BLOCK, W), lambda n: (jnp.maximum(n - 1, 0), 0))
    nxt = pl.BlockSpec((DIL_BLOCK, W), lambda n: (jnp.minimum(n + 1, nb - 1), 0))
    scale = 1.0 / math.sqrt(HEAD_DIM)

    def body(qc_ref, qn_ref, kc_ref, kp_ref, vc_ref, vp_ref, oc_ref, on_ref, lc_ref, ln_ref, doc_ref, don_ref,
             dlc_ref, dln_ref, m_ref, dq_ref, dk_ref, dv_ref):
        n = pl.program_id(0)
        has_prev = (n % seq_blocks) != 0
        has_next = jnp.logical_and(((n + 1) % seq_blocks) != 0, n + 1 < nb)
        f = lambda ref: ref[...].astype(F32)
        qc, qn, kc, kp, vc, vp = f(qc_ref), f(qn_ref), f(kc_ref), f(kp_ref), f(vc_ref), f(vp_ref)
        doc, don = doc_ref[...], don_ref[...]
        ok_c = _band("cur")
        ok_p = jnp.logical_and(_band("prev"), has_prev)
        ok_n = jnp.logical_and(_band("prev"), has_next)
        dq = jnp.zeros((DIL_BLOCK, W), F32)
        dk = jnp.zeros((DIL_BLOCK, W), F32)
        dv = jnp.zeros((DIL_BLOCK, W), F32)

        def side(qh, kk, vv, doh, lse_h, corr, ok):
            s = _dot_nt(qh, kk) * scale
            pr = jnp.where(ok, jnp.exp(jnp.where(ok, s, NEG_INF) - lse_h), 0.0)
            ds = pr * (_dot_nt(doh, vv) + corr) * scale
            return pr, ds

        for h in range(W // HEAD_DIM):
            mh = m_ref[h]
            red = lambda z: jnp.sum(z * mh, axis=-1, keepdims=True)
            qh, doh = qc * mh, doc * mh
            lse_h = red(lc_ref[...]) * (1.0 / HEAD_DIM)
            corr = red(dlc_ref[...]) - red(doc * oc_ref[...])
            pr_c, ds_c = side(qh, kc, vc * mh, doh, lse_h, corr, ok_c)
            _, ds_p = side(qh, kp, vp * mh, doh, lse_h, corr, ok_p)
            dq = dq + _dot(ds_c, kc * mh) + _dot(ds_p, kp * mh)
            dk = dk + _dot_tn(ds_c, qh)
            dv = dv + _dot_tn(pr_c, doh)
            qh2, doh2 = qn * mh, don * mh
            lse_2 = red(ln_ref[...]) * (1.0 / HEAD_DIM)
            corr2 = red(dln_ref[...]) - red(don * on_ref[...])
            pr_n, ds_n = side(qh2, kc, vc * mh, doh2, lse_2, corr2, ok_n)
            dk = dk + _dot_tn(ds_n, qh2)
            dv = dv + _dot_tn(pr_n, doh2)
        dq_ref[...] = dq
        dk_ref[...] = dk
        dv_ref[...] = dv

    return pl.pallas_call(
        body, name=name, grid=(nb,),
        in_specs=[cur, nxt, cur, prv, cur, prv, cur, nxt, cur, nxt, cur, nxt, cur, nxt,
                  pl.BlockSpec(masks.shape, lambda n: (0, 0, 0))],
        out_specs=[cur] * 3, out_shape=[jax.ShapeDtypeStruct((T, W), F32)] * 3,
        compiler_params=_cp(("parallel",)),
    )(q, q, k, k, v, v, o, o, lse, lse, do, do, dlse, dlse, masks)


def _mix_weights(ls):
    m = jnp.maximum(jnp.maximum(ls[0], ls[1]), ls[2])
    es = [jnp.exp(l - m) for l in ls]
    den = es[0] + es[1] + es[2]
    return [e / den for e in es]


def mix_fwd(os_, ls, y_mem):
    T = y_mem.shape[0]

    def fn(i, nblk, o0, o1, o2, l0, l1, l2, ym):
        w = _mix_weights((l0, l1, l2))
        return jnp.concatenate([w[0] * o0 + w[1] * o1 + w[2] * o2, ym], axis=1)

    return _rowmap(fn, name="mix_fwd", T=T, tb=512, ins=[("row", z) for z in (*os_, *ls, y_mem)],
                   outs=[("row", 2 * MEM_WIDTH, MMD)])[0]


def mix_bwd(os_, ls, dycat):
    T = dycat.shape[0]

    def fn(i, nblk, o0, o1, o2, l0, l1, l2, dy):
        w = _mix_weights((l0, l1, l2))
        os3 = (o0, o1, o2)
        dws = [dy * o for o in os3]
        tot = w[0] * dws[0] + w[1] * dws[1] + w[2] * dws[2]
        return tuple(wg * dy for wg in w) + tuple(wg * (dw - tot) for wg, dw in zip(w, dws))

    return _rowmap(fn, name="mix_bwd", T=T, tb=512,
                   ins=[("row", z) for z in (*os_, *ls)] + [("row", dycat, MEM_WIDTH, 0)],
                   outs=[("row", MEM_WIDTH, F32)] * 6)


def loss_fwd_bwd(y, target):
    T, D = y.shape

    def fn(i, nblk, y, t):
        e = y - t
        return e * (1.0 / D), jnp.zeros((8, 128), F32) + jnp.sum(e * e) * (0.5 / D)

    return _rowmap(fn, name="loss", T=T, tb=512, ins=[("row", y), ("row", target)], outs=[("row", D, F32)],
                   accs=[((8, 128), F32)])


def _to_residues(z, dil):
    T, W = z.shape
    return z.reshape(T // dil, dil, W).transpose(1, 0, 2).reshape(T, W)


def _from_residues(z, dil):
    T, W = z.shape
    return z.reshape(dil, T // dil, W).transpose(1, 0, 2).reshape(T, W)


def _pad_rows(w, rows):
    return jnp.concatenate([w, jnp.zeros((rows - w.shape[0], w.shape[1]), w.dtype)], axis=0)


def _tile_heads(g, width):
    return jnp.tile(g.reshape(1, HEAD_DIM), (1, width // HEAD_DIM))


def _conv_rows(W, i):
    return [W["ffn_conv_w"][i][j:j + 1] for j in range(3)]


def _ffn_fwd(x, i, W):
    hn = rms_fwd(x, [W["ffn_norm"][i:i + 1]], name=f"ffn_rms{i}")[0]
    u = _mm(hn, W["ffn_w_up"][i], name=f"ffn_up{i}")
    z = ffn_act_fwd(u, _conv_rows(W, i), W["ffn_conv_b"][i:i + 1], name=f"ffn_act{i}")
    out = _mm(z, W["ffn_w_down"][i], add=x, name=f"ffn_down{i}")
    return out, (x, hn, u, z)


def _ffn_bwd(dout, i, W, saved, G):
    x, hn, u, z = saved
    dz = _mm(dout, W["ffn_w_down"][i], tb=True, name=f"ffn_down_dx{i}")
    G["ffn_w_down"][i] = _mm(z, dout, ta=True, name=f"ffn_down_dw{i}")
    du, dw0, dw1, dw2, db = ffn_act_bwd(u, _conv_rows(W, i), W["ffn_conv_b"][i:i + 1], dz, name=f"ffn_act_bwd{i}")
    G["ffn_conv_w"][i] = jnp.concatenate([dw0, dw1, dw2], axis=0)
    G["ffn_conv_b"][i] = db[0]
    dhn = _mm(du, W["ffn_w_up"][i], tb=True, name=f"ffn_up_dx{i}")
    G["ffn_w_up"][i] = _mm(hn, du, ta=True, name=f"ffn_up_dw{i}")
    dx, dg = rms_bwd(x, [W["ffn_norm"][i:i + 1]], [dhn], dout, name=f"ffn_rms_bwd{i}")
    G["ffn_norm"][i] = dg[0]
    return dx


def local_step(x, mem, target, W, late=None):
    T = x.shape[0]
    W = dict(W)
    G = {"ffn_w_down": [None, None], "ffn_w_up": [None, None], "ffn_conv_w": [None, None],
         "ffn_conv_b": [None, None], "ffn_norm": [None, None], "attn_norm": [None, None], "mem_norm": [None, None],
         "mem_w_kv": [None, None], "mem_q_norm": [None, None], "mem_k_norm": [None, None]}
    mu, w0, a0 = W["a_mu"], W["a_w0"], W["a_a0"]
    w2p, a2p, g2p = (_pad_rows(W["a_w2"][0], LORA_WIDTH),
                     jnp.concatenate([jnp.zeros((64, RWKV_WIDTH), MMD), W["a_a2"][0],
                                      jnp.zeros((128, RWKV_WIDTH), MMD)], axis=0),
                     jnp.concatenate([jnp.zeros((128, RWKV_WIDTH), MMD), W["a_g2"][0]], axis=0))
    k_k, k_a, lnx_w, lnx_b = W["a_k_k"], W["a_k_a"], W["a_lnx_w"], W["a_lnx_b"]
    r_k = W["a_r_k"].reshape(1, RWKV_WIDTH)

    h0 = rms_fwd(x, [W["attn_norm"][0:1]], name="attn_rms0")[0]
    p = _mm(h0, W["a_w_in"][0], name="a_in")
    r, w, k2, v, kk, kka, g = rwkv_pre_fwd(p, mu, w0, a0, w2p, a2p, g2p, k_k, k_a)
    if late is None:
        y, states, final_state = scan_fwd(r, w, k2, v, kk, kka)
    else:
        y, states, final_state, *gathered = scan_fwd(r, w, k2, v, kk, kka, gather=late[0])
        W.update(late[1](gathered))
    memkv = []
    for i in range(2):
        memkv.append(mem_kv_fwd(mem, W["mem_norm"][i:i + 1], W["mem_w_kv"][i], _tile_heads(W["mem_k_norm"][i], MEM_WIDTH),
                                name=f"mem_kv{i}"))
    qg0 = _tile_heads(W["mem_q_norm"][0], MEM_WIDTH)
    y_mem0 = mem_attn_fwd(p, SHIFT_WIDTH // MEM_WIDTH, memkv[0][0], memkv[0][1], qg0, name="mem_attn0")
    ycat0 = mix_gate_fwd(y, r, k2, v, g, y_mem0, lnx_w, lnx_b, r_k)
    x1 = _mm(ycat0, W["a_w_out"][0], add=x, name="a_out")
    x2, ffn0 = _ffn_fwd(x1, 0, W)

    h1, hkv = rms_fwd(x2, [W["attn_norm"][1:2], W["kv_norm"].reshape(1, -1)], name="attn_rms1")
    kvp = _mm(hkv, W["kv_w"], name="kv_in")
    pb = _mm(h1, W["b_w_in"][0], name="b_in")
    cos, sin = rope_tables(T)
    kg_t, qg_t = _tile_heads(W["kv_k_norm"], DIL_WIDTH), _tile_heads(W["b_q_norm"][0], DIL_WIDTH)
    q, ksh, vsh = qk_fwd(kvp, pb, kg_t, qg_t, cos, sin)
    os_, ls, grp = [], [], []
    for gi, (win, dil) in enumerate(DIL_GROUPS):
        sl = slice(gi * MEM_WIDTH, (gi + 1) * MEM_WIDTH)
        qg_, kg_, vg_ = (_to_residues(z[:, sl], dil) for z in (q, ksh, vsh))
        o_r, l_r = dil_attn_fwd(qg_, kg_, vg_, T // dil // DIL_BLOCK, name=f"dil_fwd{gi}")
        grp.append((qg_, kg_, vg_, o_r, l_r))
        os_.append(_from_residues(o_r, dil))
        ls.append(_from_residues(l_r, dil))
    qg1 = _tile_heads(W["mem_q_norm"][1], MEM_WIDTH)
    y_mem1 = mem_attn_fwd(pb, DIL_WIDTH // MEM_WIDTH, memkv[1][0], memkv[1][1], qg1, name="mem_attn1")
    ycat1 = mix_fwd(os_, ls, y_mem1)
    x3 = _mm(ycat1, W["b_w_out"][0], add=x2, name="b_out")
    x4, ffn1 = _ffn_fwd(x3, 1, W)

    dx4, loss = loss_fwd_bwd(x4, target)

    dx3 = _ffn_bwd(dx4, 1, W, ffn1, G)
    dycat1 = _mm(dx3, W["b_w_out"][0], tb=True, name="b_out_dx")
    G["b_w_out"] = _mm(ycat1, dx3, ta=True, name="b_out_dw")[None]
    dq_mem1, dkn1, dvm1, dqg1 = mem_attn_bwd(pb, DIL_WIDTH // MEM_WIDTH, memkv[1][0], memkv[1][1], qg1, dycat1, 1,
                                             name="mem_attn_bwd1")
    G["mem_q_norm"][1] = dqg1[0]
    d_os_ls = mix_bwd(os_, ls, dycat1)
    dqs, dks, dvs = [], [], []
    for gi, (win, dil) in enumerate(DIL_GROUPS):
        qg_, kg_, vg_, o_r, l_r = grp[gi]
        do_r, dl_r = _to_residues(d_os_ls[gi], dil), _to_residues(d_os_ls[3 + gi], dil)
        dq_r, dk_r, dv_r = dil_attn_bwd(qg_, kg_, vg_, o_r, l_r, do_r, dl_r, T // dil // DIL_BLOCK, name=f"dil_bwd{gi}")
        dqs.append(_from_residues(dq_r, dil))
        dks.append(_from_residues(dk_r, dil))
        dvs.append(_from_residues(dv_r, dil))
    dq, dk, dv = (jnp.concatenate(z, axis=1) for z in (dqs, dks, dvs))
    dpb, dkvp, dqn_g, dkn_g = qk_bwd(kvp, pb, kg_t, qg_t, cos, sin, dq, dk, dv, dq_mem1)
    G["b_q_norm"] = dqn_g
    G["kv_k_norm"] = dkn_g[0]
    dh1 = _mm(dpb, W["b_w_in"][0], tb=True, name="b_in_dx")
    G["b_w_in"] = _mm(h1, dpb, ta=True, name="b_in_dw")[None]
    dhkv = _mm(dkvp, W["kv_w"], tb=True, name="kv_in_dx")
    G["kv_w"] = _mm(hkv, dkvp, ta=True, name="kv_in_dw")
    dx2, dg1, dgkv = rms_bwd(x2, [W["attn_norm"][1:2], W["kv_norm"].reshape(1, -1)], [dh1, dhkv], dx3,
                             name="attn_rms_bwd1")
    G["attn_norm"][1] = dg1[0]
    G["kv_norm"] = dgkv[0]

    dx1 = _ffn_bwd(dx2, 0, W, ffn0, G)
    dycat0 = _mm(dx1, W["a_w_out"][0], tb=True, name="a_out_dx")
    G["a_w_out"] = _mm(ycat0, dx1, ta=True, name="a_out_dw")[None]
    dq_mem0, dkn0, dvm0, dqg0 = mem_attn_bwd(p, SHIFT_WIDTH // MEM_WIDTH, memkv[0][0], memkv[0][1], qg0, dycat0,
                                             RWKV_WIDTH // MEM_WIDTH, name="mem_attn_bwd0")
    G["mem_q_norm"][0] = dqg0[0]
    dy, dr_b, dk2_b, dv_b, dg, dlw, dlb, drk = mix_gate_bwd(y, r, k2, v, g, dycat0, lnx_w, lnx_b, r_k)
    for i, (dkn, dvm) in enumerate(((dkn0, dvm0), (dkn1, dvm1))):
        dwkv, dgm, dkg = mem_kv_bwd(mem, W["mem_norm"][i:i + 1], W["mem_w_kv"][i],
                                    _tile_heads(W["mem_k_norm"][i], MEM_WIDTH), dkn, dvm, name=f"mem_kv_bwd{i}")
        G["mem_w_kv"][i], G["mem_norm"][i], G["mem_k_norm"][i] = dwkv, dgm[0], dkg[0]
    late_out = None
    if late is None:
        dr, dw, dk2, dv, dkk, dkka = scan_bwd(r, w, k2, v, kk, kka, states, final_state, dy)
    else:
        pieces = late[2](G)
        dr, dw, dk2, dv, dkk, dkka, *received = scan_bwd(r, w, k2, v, kk, kka, states, final_state, dy, scatter=pieces)
        late_out = (received, pieces)
    dxs, dmu, dw0, da0, dw2p, da2p, dg2p, dk_k, dk_a = rwkv_pre_bwd(
        p, mu, w0, a0, w2p, a2p, g2p, k_k, k_a, (dr, dr_b), dw, (dk2, dk2_b), (dv, dv_b), dkk, dkka, dg)
    dp = shift_bwd(dxs, mu, dq_mem0)
    G.update(a_mu=dmu, a_w0=dw0, a_a0=da0, a_w2=dw2p[None, :64], a_a2=da2p[None, 64:128], a_g2=dg2p[None, 128:],
             a_k_k=dk_k, a_k_a=dk_a, a_r_k=drk.reshape(1, RWKV_HEADS, HEAD_DIM), a_lnx_w=dlw, a_lnx_b=dlb)
    dh0 = _mm(dp, W["a_w_in"][0], tb=True, name="a_in_dx")
    G["a_w_in"] = _mm(h0, dp, ta=True, name="a_in_dw")[None]
    grad_x, dg0 = rms_bwd(x, [W["attn_norm"][0:1]], [dh0], dx1, name="attn_rms_bwd0")
    G["attn_norm"][0] = dg0[0]
    for n in list(G):
        if isinstance(G[n], list):
            G[n] = jnp.stack(G[n], axis=0)
    return loss, grad_x, G, late_out


HBM_SPEC = pl.BlockSpec(memory_space=pltpu.HBM)


def _mesh_pos():
    return lax.axis_index("x"), lax.axis_index("y"), lax.axis_index("c")


def _other_chips(x, y):
    return [(1 - x, y), (x, 1 - y), (1 - x, 1 - y)]


def _remote(send_sems, recv_sems, k, src, dst, to):
    return pltpu.make_async_remote_copy(src_ref=src, dst_ref=dst, send_sem=send_sems.at[k], recv_sem=recv_sems.at[k],
                                        device_id=to, device_id_type=MESH)


def _comm_call(body, name, ins, out_shape, n_remote):
    scratch = [pltpu.SemaphoreType.DMA((n_remote,)), pltpu.SemaphoreType.DMA((n_remote,))]
    return pl.pallas_call(body, name=name, in_specs=[HBM_SPEC] * len(ins), out_specs=[HBM_SPEC] * len(out_shape),
                          out_shape=out_shape, scratch_shapes=scratch)(*ins)


def comm_gather(wbig, wsm):
    def body(wb, ws, ob, os_, send_sems, recv_sems):
        x, y, c = _mesh_pos()
        s = 2 * x + y
        me, sibling = (x, y, c), (x, y, 1 - c)
        chips = _other_chips(x, y)
        rc = functools.partial(_remote, send_sems, recv_sems)
        first = []
        for j, (cx, cy) in enumerate(chips):
            first.append(rc(j, wb.at[c], ob.at[s, c], (cx, cy, c)))
            first.append(rc(6 + j, ws, os_.at[s], (cx, cy, c)))
        for cp in first:
            cp.start()
        passed = []
        for j, (cx, cy) in enumerate(chips):
            blk = ob.at[2 * cx + cy, c]
            rc(j, blk, blk, me).wait_recv()
            passed.append(rc(3 + j, blk, blk, sibling))
            passed[-1].start()
        for j, (cx, cy) in enumerate(chips):
            blk = ob.at[2 * cx + cy, 1 - c]
            rc(3 + j, blk, blk, me).wait_recv()
            sb = os_.at[2 * cx + cy]
            rc(6 + j, sb, sb, me).wait_recv()
        for cp in first + passed:
            cp.wait_send()

    out_shape = [jax.ShapeDtypeStruct((N_CHIPS, *wbig.shape), wbig.dtype),
                 jax.ShapeDtypeStruct((N_CHIPS, *wsm.shape), wsm.dtype)]
    return _comm_call(body, "comm_gather", [wbig, wsm], out_shape, 9)


def comm_pair_exchange(gb, gs):
    def body(gb_ref, gs_ref, rb_ref, rs_ref, send_sems, recv_sems):
        x, y, c = _mesh_pos()
        sibling = (x, y, 1 - c)
        rc = functools.partial(_remote, send_sems, recv_sems)
        cps = [rc(r, gb_ref.at[r, 1 - c], rb_ref.at[r], sibling) for r in range(N_CHIPS)]
        cps.append(rc(N_CHIPS, gs_ref.at[1 - c], rs_ref, sibling))
        for cp in cps:
            cp.start()
        for cp in cps:
            cp.wait()

    out_shape = [jax.ShapeDtypeStruct((N_CHIPS, *gb.shape[2:]), gb.dtype), jax.ShapeDtypeStruct(gs.shape[1:], gs.dtype)]
    return _comm_call(body, "comm_pair_exchange", [gb, gs], out_shape, N_CHIPS + 1)


def comm_chip_exchange(hb, hs):
    def body(hb_ref, hs_ref, qb_ref, qs_ref, send_sems, recv_sems):
        x, y, c = _mesh_pos()
        s = 2 * x + y
        me = (x, y, c)
        chips = _other_chips(x, y)
        rc = functools.partial(_remote, send_sems, recv_sems)
        cps = []
        for j, (cx, cy) in enumerate(chips):
            cps.append(rc(j, hb_ref.at[2 * cx + cy], qb_ref.at[s], (cx, cy, c)))
            cps.append(rc(3 + j, hs_ref, qs_ref.at[s], (cx, cy, c)))
        for cp in cps:
            cp.start()
        for j, (cx, cy) in enumerate(chips):
            blk = qb_ref.at[2 * cx + cy]
            rc(j, blk, blk, me).wait_recv()
            sb = qs_ref.at[2 * cx + cy]
            rc(3 + j, sb, sb, me).wait_recv()
        for cp in cps:
            cp.wait_send()

    out_shape = [jax.ShapeDtypeStruct(hb.shape, hb.dtype), jax.ShapeDtypeStruct((N_CHIPS, *hs.shape), hs.dtype)]
    return _comm_call(body, "comm_chip_exchange", [hb, hs], out_shape, 6)


def comm_pair_share(halves):
    n = len(halves)

    def body(*refs):
        x, y, c = _mesh_pos()
        send_sems, recv_sems = refs[2 * n], refs[2 * n + 1]
        cps = [_remote(send_sems, recv_sems, k, refs[k], refs[n + k], (x, y, 1 - c)) for k in range(n)]
        for cp in cps:
            cp.start()
        for cp in cps:
            cp.wait()

    out_shape = [jax.ShapeDtypeStruct(h.shape, h.dtype) for h in halves]
    return _comm_call(body, "comm_pair_share", list(halves), out_shape, n)


def add_pairs(a, b, out_dtype, *, name, tb):
    T, L = a.shape
    return _rowmap(lambda i, n, p, q: p + q, name=name, T=T, tb=tb, ins=[("row", a), ("row", b)],
                   outs=[("row", L, out_dtype)])[0]


def add_chips(parts, *, name, tb):
    T, L = parts[0].shape

    def fn(i, n, *ps):
        acc = ps[0].astype(F32)
        for p in ps[1:]:
            acc = acc + p.astype(F32)
        return acc

    return _rowmap(fn, name=name, T=T, tb=tb, ins=[("row", p) for p in parts], outs=[("row", L, F32)])[0]


def adamw(g, w, m, v, *, name, tb):
    T, L = g.shape

    def fn(i, n, g, w, m, v):
        m2 = ADAM_B1 * m + (1.0 - ADAM_B1) * g
        v2 = ADAM_B2 * v + (1.0 - ADAM_B2) * (g * g)
        m_hat = m2 / (1.0 - ADAM_B1 ** ADAM_STEP)
        v_hat = v2 / (1.0 - ADAM_B2 ** ADAM_STEP)
        return -ADAM_LR * (m_hat / (jnp.sqrt(v_hat) + ADAM_EPS) + ADAM_WD * w), m2, v2

    return _rowmap(fn, name=name, T=T, tb=tb, ins=[("row", z) for z in (g, w, m, v)], outs=[("row", L, F32)] * 3)


BIG_LANES = 1024
SMALL_LANES = 128


def _flat_cat(arrs, total, dtype):
    parts = [a.reshape(-1).astype(dtype) for a in arrs]
    n = sum(p.shape[0] for p in parts)
    assert n <= total, (n, total)
    if n < total:
        parts.append(jnp.zeros((total - n,), dtype))
    return jnp.concatenate(parts)


def _split_flat(flat, shapes):
    out, off = [], 0
    for shp in shapes:
        n = math.prod(shp)
        out.append(flat[off:off + n].reshape(shp))
        off += n
    return out


def _round_up(n, m):
    return -(-n // m) * m


def _full_shape(shard_shape, axis):
    return tuple(d * N_CHIPS if i == axis else d for i, d in enumerate(shard_shape))


def kernel(x, mem, attn_norm, a_w_in, a_mu, a_w0, a_w2, a_a0, a_a2, a_g2, a_k_k, a_k_a, a_r_k, a_lnx_w, a_lnx_b, a_w_out, kv_norm, kv_w, kv_k_norm, b_w_in, b_q_norm, b_w_out, mem_norm, mem_w_kv, mem_q_norm, mem_k_norm, ffn_norm, ffn_w_up, ffn_conv_w, ffn_conv_b, ffn_w_down, loss_target, m_attn_norm, m_a_w_in, m_a_mu, m_a_w0, m_a_w2, m_a_a0, m_a_a2, m_a_g2, m_a_k_k, m_a_k_a, m_a_r_k, m_a_lnx_w, m_a_lnx_b, m_a_w_out, m_kv_norm, m_kv_w, m_kv_k_norm, m_b_w_in, m_b_q_norm, m_b_w_out, m_mem_norm, m_mem_w_kv, m_mem_q_norm, m_mem_k_norm, m_ffn_norm, m_ffn_w_up, m_ffn_conv_w, m_ffn_conv_b, m_ffn_w_down, v_attn_norm, v_a_w_in, v_a_mu, v_a_w0, v_a_w2, v_a_a0, v_a_a2, v_a_g2, v_a_k_k, v_a_k_a, v_a_r_k, v_a_lnx_w, v_a_lnx_b, v_a_w_out, v_kv_norm, v_kv_w, v_kv_k_norm, v_b_w_in, v_b_q_norm, v_b_w_out, v_mem_norm, v_mem_w_kv, v_mem_q_norm, v_mem_k_norm, v_ffn_norm, v_ffn_w_up, v_ffn_conv_w, v_ffn_conv_b, v_ffn_w_down):
    args = (attn_norm, a_w_in, a_mu, a_w0, a_w2, a_a0, a_a2, a_g2, a_k_k, a_k_a, a_r_k, a_lnx_w, a_lnx_b, a_w_out, kv_norm, kv_w, kv_k_norm, b_w_in, b_q_norm, b_w_out, mem_norm, mem_w_kv, mem_q_norm, mem_k_norm, ffn_norm, ffn_w_up, ffn_conv_w, ffn_conv_b, ffn_w_down)
    ms = (m_attn_norm, m_a_w_in, m_a_mu, m_a_w0, m_a_w2, m_a_a0, m_a_a2, m_a_g2, m_a_k_k, m_a_k_a, m_a_r_k, m_a_lnx_w, m_a_lnx_b, m_a_w_out, m_kv_norm, m_kv_w, m_kv_k_norm, m_b_w_in, m_b_q_norm, m_b_w_out, m_mem_norm, m_mem_w_kv, m_mem_q_norm, m_mem_k_norm, m_ffn_norm, m_ffn_w_up, m_ffn_conv_w, m_ffn_conv_b, m_ffn_w_down)
    vs = (v_attn_norm, v_a_w_in, v_a_mu, v_a_w0, v_a_w2, v_a_a0, v_a_a2, v_a_g2, v_a_k_k, v_a_k_a, v_a_r_k, v_a_lnx_w, v_a_lnx_b, v_a_w_out, v_kv_norm, v_kv_w, v_kv_k_norm, v_b_w_in, v_b_q_norm, v_b_w_out, v_mem_norm, v_mem_w_kv, v_mem_q_norm, v_mem_k_norm, v_ffn_norm, v_ffn_w_up, v_ffn_conv_w, v_ffn_conv_b, v_ffn_w_down)
    w_sh, m_sh, v_sh = (dict(zip(WEIGHTS, z)) for z in (args, ms, vs))
    xi, yi, ci = _mesh_pos()
    chip = 2 * xi + yi
    axes = {**dict(BIG), **dict(SMALL_SHARDED)}
    early_names = [n for n, _ in BIG if n in EARLY_BIG]
    late_names = [n for n, _ in BIG if n not in EARLY_BIG and n != NATURAL_BIG]
    ss_names, ss_axes = [n for n, _ in SMALL_SHARDED], dict(SMALL_SHARDED)
    shapes_of = lambda names: [w_sh[n].shape for n in names]
    count = lambda names: sum(math.prod(s) for s in shapes_of(names))
    n_early, n_late = count(early_names), count(late_names)
    assert n_early % (2 * 16 * BIG_LANES) == 0 and n_late % (2 * 16 * BIG_LANES) == 0
    mh, mh_late = n_early // (2 * BIG_LANES), n_late // (2 * BIG_LANES)
    n_ss = _round_up(count(ss_names), 8 * SMALL_LANES)

    def shard_pack(names, total, dtype, source):
        return _flat_cat([source[n] for n in names], total, dtype)

    def join(pieces, n):
        if pieces[0].ndim == 3:
            return [jnp.concatenate([p[l] for p in pieces], axis=axes[n] - 1) for l in range(pieces[0].shape[0])]
        return jnp.concatenate(pieces, axis=axes[n])

    def unshard(names, gathered):
        per_chip = [_split_flat(gathered[j], shapes_of(names)) for j in range(N_CHIPS)]
        return {n: join([per_chip[j][k] for j in range(N_CHIPS)], n) for k, n in enumerate(names)}

    def whole(g):
        return jnp.stack(g, axis=0) if isinstance(g, list) else g

    def by_chip(names, total, dtype, grads):
        parts = [jnp.split(whole(grads[n]), N_CHIPS, axis=axes[n]) for n in names]
        return jnp.stack([_flat_cat([p[j] for p in parts], total, dtype) for j in range(N_CHIPS)])

    wbig = shard_pack(early_names, n_early, MMD, w_sh).reshape(2, mh, BIG_LANES)
    wsm = shard_pack(ss_names, n_ss, F32, w_sh).reshape(-1, SMALL_LANES)
    wbig_all, wsm_all = comm_gather(wbig, wsm)
    wbig_all = lax.dynamic_update_index_in_dim(wbig_all, wbig, chip, 0).reshape(N_CHIPS, -1)
    wsm_all = lax.dynamic_update_index_in_dim(wsm_all, wsm, chip, 0).reshape(N_CHIPS, -1)
    W = {n: w_sh[n] for n in SMALL_REPL}
    W.update(unshard(early_names, wbig_all))
    W.update(unshard(ss_names, wsm_all))
    for n in ("a_w2", "a_a2", "a_g2"):
        W[n] = [z.astype(MMD) for z in W[n]]
    wlate = shard_pack(late_names, n_late, MMD, w_sh).reshape(2, mh_late, BIG_LANES)
    nat_axis = axes[NATURAL_BIG]
    wnat = w_sh[NATURAL_BIG].astype(MMD)
    assert wnat.shape[0] == 2 and nat_axis != 0

    def unpack_late(gathered):
        full = lax.dynamic_update_index_in_dim(gathered[0], wlate, chip, 0)
        out = unshard(late_names, full.reshape(N_CHIPS, -1))
        nat = lax.dynamic_update_index_in_dim(gathered[1], wnat, chip, 0)
        out[NATURAL_BIG] = join([nat[j] for j in range(N_CHIPS)], NATURAL_BIG)
        return out

    def pack_late(grads):
        return [by_chip(late_names, n_late, BF16, grads).reshape(N_CHIPS, 2, mh_late, BIG_LANES),
                jnp.stack(jnp.split(whole(grads[NATURAL_BIG]).astype(BF16), N_CHIPS, axis=nat_axis))]

    loss_blk, grad_x, G, (received, pieces) = local_step(x[0], mem[0], loss_target[0], W,
                                                          late=([wlate, wnat], unpack_late, pack_late))
    loss = lax.psum(loss_blk[0, 0], ("x", "y", "c"))

    own_piece = lambda p: lax.dynamic_index_in_dim(lax.dynamic_index_in_dim(p, chip, 0, keepdims=False), ci, 0,
                                                   keepdims=False)
    gh_late = add_chips([received[0][k] for k in range(len(PEER_FLIPS))] + [own_piece(pieces[0])],
                        name="add_pieces_late", tb=32)
    gh_nat = add_chips([received[1][k] for k in range(len(PEER_FLIPS))] + [own_piece(pieces[1])],
                       name="add_pieces_natural", tb=32)
    gbig = by_chip(early_names, n_early, F32, G).reshape(N_CHIPS, 2, mh, BIG_LANES)
    sm_full_names = ss_names + list(SMALL_REPL)
    sm_full_shapes = [_full_shape(w_sh[n].shape, ss_axes[n]) for n in ss_names] + [w_sh[n].shape for n in SMALL_REPL]
    n_smf = _round_up(sum(math.prod(s) for s in sm_full_shapes), 2 * 8 * SMALL_LANES)
    msh = n_smf // (2 * SMALL_LANES)
    gsm = _flat_cat([G[n] for n in sm_full_names], n_smf, F32).reshape(2, msh, SMALL_LANES)
    rb, rs = comm_pair_exchange(gbig, gsm)
    mine_b = lax.dynamic_index_in_dim(gbig, ci, axis=1, keepdims=False)
    mine_s = lax.dynamic_index_in_dim(gsm, ci, axis=0, keepdims=False)
    hb = add_pairs(mine_b.reshape(-1, BIG_LANES), rb.reshape(-1, BIG_LANES), BF16, name="add_pairs_big", tb=128)
    hs = add_pairs(mine_s, rs, F32, name="add_pairs_small", tb=msh)
    hb = hb.reshape(N_CHIPS, mh, BIG_LANES)
    qb, qs = comm_chip_exchange(hb, hs)
    qb = lax.dynamic_update_index_in_dim(qb, lax.dynamic_index_in_dim(hb, chip, 0, keepdims=False), chip, 0)
    qs = lax.dynamic_update_index_in_dim(qs, hs, chip, 0)
    gh = add_chips([qb[j] for j in range(N_CHIPS)], name="add_chips_big", tb=32)
    gsh = add_chips([qs[j] for j in range(N_CHIPS)], name="add_chips_small", tb=msh)
    rh, rh_late, rh_nat, rsh = comm_pair_share([gh, gh_late, gh_nat, gsh])
    both = lambda mine_, theirs: jnp.where(ci == 0, jnp.stack([mine_, theirs]), jnp.stack([theirs, mine_]))
    gfull, gfull_late, gfull_nat, gsfull = both(gh, rh), both(gh_late, rh_late), both(gh_nat, rh_nat), both(gsh, rsh)

    res = {tag: {} for tag in ("grad", "delta", "new_m", "new_v")}
    big_grads = (list(zip(early_names, _split_flat(gfull.reshape(-1), shapes_of(early_names))))
                 + list(zip(late_names, _split_flat(gfull_late.reshape(-1), shapes_of(late_names))))
                 + [(NATURAL_BIG, gfull_nat)])
    for n, g in big_grads:
        shp = w_sh[n].shape
        rows = lambda z: z.reshape(-1, shp[-1])
        nrow = math.prod(shp[:-1])
        tb = next(t for t in (512, 256, 128, 64) if nrow % t == 0 and t * shp[-1] <= (1 << 19))
        outs = adamw(rows(g), rows(w_sh[n]), rows(m_sh[n]), rows(v_sh[n]), name=f"adamw_{n}", tb=tb)
        res["grad"][n] = g
        for tag, o in zip(("delta", "new_m", "new_v"), outs):
            res[tag][n] = o.reshape(shp)
    sm_full = dict(zip(sm_full_names, _split_flat(gsfull.reshape(-1), sm_full_shapes)))
    g_loc = {}
    for n in ss_names:
        size = w_sh[n].shape[ss_axes[n]]
        g_loc[n] = lax.dynamic_slice_in_dim(sm_full[n], chip * size, size, axis=ss_axes[n])
    for n in SMALL_REPL:
        g_loc[n] = sm_full[n]
    n_sml = _round_up(sum(math.prod(w_sh[n].shape) for n in sm_full_names), 8 * SMALL_LANES)
    pack_sm = lambda d: _flat_cat([d[n] for n in sm_full_names], n_sml, F32).reshape(-1, SMALL_LANES)
    d_sm, m_sm, v_sm = adamw(pack_sm(g_loc), pack_sm(w_sh), pack_sm(m_sh), pack_sm(v_sh), name="adamw_small",
                             tb=n_sml // SMALL_LANES)
    sm_loc_shapes = [w_sh[n].shape for n in sm_full_names]
    res["grad"].update(g_loc)
    for tag, smv in (("delta", d_sm), ("new_m", m_sm), ("new_v", v_sm)):
        res[tag].update(dict(zip(sm_full_names, _split_flat(smv.reshape(-1), sm_loc_shapes))))
    return (loss, grad_x[None], *[res[tag][n] for tag in ("grad", "delta", "new_m", "new_v") for n in WEIGHTS])
```

```python
import functools
import math

import numpy as np
import jax
import jax.numpy as jnp
from jax import lax
from jax.experimental import pallas as pl
from jax.experimental.pallas import tpu as pltpu

F32 = jnp.float32
BF16 = jnp.bfloat16
MMD = jnp.bfloat16

D_MODEL = 1024
HEAD_DIM = 64
N_MEM = 256
MEM_WIDTH = 256
RWKV_HEADS = 12
RWKV_WIDTH = 768
SHIFT_WIDTH = 2560
LORA_WIDTH = 256
DIL_WIDTH = 768
DIL_GROUPS = ((128, 1), (512, 4), (2048, 16))
DIL_BLOCK = 128
D_FF = 2816
ROPE_THETA = 10000.0
RMS_EPS = 1e-6
LNX_EPS = 64e-5
NEG_INF = -1e30
ADAM_LR = 0.001
ADAM_B1 = 0.9
ADAM_B2 = 0.999
ADAM_EPS = 1e-08
ADAM_WD = 0.01
ADAM_STEP = 10
N_CHIPS = 4
MESH = pl.DeviceIdType.MESH
VMEM_LIMIT_MB = 56
SCAN_CHUNK = 64
SCAN_UNROLL = 32
SCAN_UNROLL_BWD = 16

BIG = (("a_w_in", 2), ("a_w_out", 1), ("kv_w", 1), ("b_w_in", 1), ("b_w_out", 2), ("mem_w_kv", 1),
       ("ffn_w_up", 2), ("ffn_w_down", 1))
EARLY_BIG = ("a_w_in",)
NATURAL_BIG = "ffn_w_up"
SMALL_SHARDED = (("a_mu", 1), ("a_w0", 1), ("a_w2", 2), ("a_a0", 1), ("a_a2", 2), ("a_g2", 2), ("a_k_k", 1),
                 ("a_k_a", 1), ("a_lnx_w", 1), ("a_lnx_b", 1), ("ffn_conv_w", 2))
SMALL_REPL = ("attn_norm", "a_r_k", "kv_norm", "kv_k_norm", "b_q_norm", "mem_norm", "mem_q_norm", "mem_k_norm",
              "ffn_norm", "ffn_conv_b")
WEIGHTS = ("attn_norm", "a_w_in", "a_mu", "a_w0", "a_w2", "a_a0", "a_a2", "a_g2", "a_k_k", "a_k_a", "a_r_k",
           "a_lnx_w", "a_lnx_b", "a_w_out", "kv_norm", "kv_w", "kv_k_norm", "b_w_in", "b_q_norm", "b_w_out",
           "mem_norm", "mem_w_kv", "mem_q_norm", "mem_k_norm", "ffn_norm", "ffn_w_up", "ffn_conv_w", "ffn_conv_b",
           "ffn_w_down")


def _cp(sem=None, **kw):
    return pltpu.CompilerParams(dimension_semantics=sem, vmem_limit_bytes=VMEM_LIMIT_MB << 20, **kw)


def _tile(n, cands=(512, 256, 128)):
    for c in cands:
        if n % c == 0:
            return c
    return n


def _mm(a, b, *, name, ta=False, tb=False, add=None, out_dtype=F32):
    K, M = a.shape if ta else a.shape[::-1]
    N = b.shape[0] if tb else b.shape[1]
    assert K == (b.shape[1] if tb else b.shape[0])
    tm, tn = _tile(M, (512, 256, 128) if ta else (1024, 512, 256, 128)), _tile(N, (512, 1408, 256, 128))
    bytes_of = lambda z: z.size * z.dtype.itemsize
    kept = 1 if bytes_of(b) + bytes_of(a) * (N // tn) < bytes_of(a) + bytes_of(b) * (M // tm) else 0
    mi, ni = ((lambda o, i: i), (lambda o, i: o)) if kept else ((lambda o, i: o), (lambda o, i: i))
    grid = (N // tn, M // tm) if kept else (M // tm, N // tn)
    a_blk, b_blk = ((K, tm) if ta else (tm, K)), ((tn, K) if tb else (K, tn))
    a_spec = pl.BlockSpec(a_blk, (lambda o, i: (0, mi(o, i))) if ta else (lambda o, i: (mi(o, i), 0)))
    b_spec = pl.BlockSpec(b_blk, (lambda o, i: (ni(o, i), 0)) if tb else (lambda o, i: (0, ni(o, i))))
    o_spec = pl.BlockSpec((tm, tn), lambda o, i: (mi(o, i), ni(o, i)))
    dn = (((0,) if ta else (1,), (1,) if tb else (0,)), ((), ()))
    has_add = add is not None
    cache = (b if kept else a).dtype != MMD

    def body(*refs):
        vals = [refs[0], refs[1]]
        o_ref = refs[2 + has_add]
        if cache:
            scr = refs[-1]

            @pl.when(pl.program_id(1) == 0)
            def _():
                scr[...] = vals[kept][...].astype(MMD)

            vals[kept] = scr
        acc = lax.dot_general(vals[0][...].astype(MMD), vals[1][...].astype(MMD), dn, preferred_element_type=F32)
        if has_add:
            acc = acc + refs[2][...]
        o_ref[...] = acc.astype(o_ref.dtype)

    ins = [a, b] + ([add] if has_add else [])
    specs = [a_spec, b_spec] + ([o_spec] if has_add else [])
    return pl.pallas_call(
        body, name=name, grid=grid, in_specs=specs, out_specs=o_spec,
        out_shape=jax.ShapeDtypeStruct((M, N), out_dtype),
        scratch_shapes=[pltpu.VMEM(b_blk if kept else a_blk, MMD)] if cache else [],
        compiler_params=_cp(("parallel", "arbitrary")),
    )(*ins)


def _rowmap(fn, *, name, T, tb, ins, outs, accs=()):
    nblk = T // tb
    assert T % tb == 0 and tb % 8 == 0
    in_specs, args = [], []
    for spec in ins:
        kind, arr = spec[0], spec[1]
        w, cb = (spec[2], spec[3]) if len(spec) > 2 else (arr.shape[-1], 0)
        if kind == "row":
            in_specs.append(pl.BlockSpec((tb, w), lambda i, cb=cb: (i, cb)))
        elif kind == "prev":
            in_specs.append(pl.BlockSpec((8, w), lambda i, cb=cb: (jnp.maximum(i * (tb // 8) - 1, 0), cb)))
        elif kind == "next":
            in_specs.append(pl.BlockSpec((8, w), lambda i, cb=cb: (jnp.minimum((i + 1) * (tb // 8), T // 8 - 1), cb)))
        elif kind == "const":
            in_specs.append(pl.BlockSpec(arr.shape, lambda i, nd=arr.ndim: (0,) * nd))
        else:
            raise ValueError(kind)
        args.append(arr)
    out_shape, out_specs = [], []
    for kind, w, dt in outs:
        out_shape.append(jax.ShapeDtypeStruct((T, w), dt))
        out_specs.append(pl.BlockSpec((tb, w), lambda i: (i, 0)))
    for shp, dt in accs:
        out_shape.append(jax.ShapeDtypeStruct(shp, dt))
        out_specs.append(pl.BlockSpec(shp, lambda i, nd=len(shp): (0,) * nd))
    n_in, n_out = len(ins), len(outs)

    def body(*refs):
        i = pl.program_id(0)
        vals = [r[...] for r in refs[:n_in]]
        res = fn(i, nblk, *vals)
        if not isinstance(res, (tuple, list)):
            res = (res,)
        assert len(res) == n_out + len(accs), (name, len(res))
        for r, v in zip(refs[n_in:n_in + n_out], res[:n_out]):
            r[...] = v.astype(r.dtype)
        acc_refs = refs[n_in + n_out:]
        if acc_refs:
            @pl.when(i == 0)
            def _():
                for r in acc_refs:
                    r[...] = jnp.zeros(r.shape, r.dtype)

            for r, v in zip(acc_refs, res[n_out:]):
                r[...] += v

    res = pl.pallas_call(
        body, name=name, grid=(nblk,), in_specs=in_specs, out_specs=out_specs, out_shape=out_shape,
        compiler_params=_cp(("arbitrary",)),
    )(*args)
    return res


def _row_pick(halo, r):
    rid = lax.broadcasted_iota(jnp.int32, halo.shape, 0)
    return jnp.sum(jnp.where(rid == r, halo, 0.0), axis=0, keepdims=True)


def _shift_down(x, row_before, is_first):
    rid = lax.broadcasted_iota(jnp.int32, x.shape, 0)
    first = jnp.where(is_first, 0.0, 1.0) * row_before
    return jnp.where(rid == 0, first, pltpu.roll(x, 1, axis=0))


def _shift_up(x, row_after, is_last):
    n = x.shape[0]
    rid = lax.broadcasted_iota(jnp.int32, x.shape, 0)
    last = jnp.where(is_last, 0.0, 1.0) * row_after
    return jnp.where(rid == n - 1, last, pltpu.roll(x, n - 1, axis=0))


def _dot(a, b, dn=(((1,), (0,)), ((), ()))):
    return lax.dot_general(a.astype(MMD), b.astype(MMD), dn, preferred_element_type=F32)


def _dot_nt(a, b):
    return _dot(a, b, (((1,), (1,)), ((), ())))


def _dot_tn(a, b):
    return _dot(a, b, (((0,), (0,)), ((), ())))


def _dot_exact01(x, g01):
    hi = x.astype(BF16)
    lo = (x - hi.astype(F32)).astype(BF16)
    dn = (((1,), (0,)), ((), ()))
    return (lax.dot_general(hi, g01, dn, preferred_element_type=F32)
            + lax.dot_general(lo, g01, dn, preferred_element_type=F32))


def _fold_heads(v, fold):
    return _row_pick(_dot_exact01(jnp.broadcast_to(v, (8, v.shape[1])), fold), 0)


def _fold_ones(width):
    idx = np.arange(width) % HEAD_DIM
    return jnp.asarray((idx[:, None] == np.arange(HEAD_DIM)[None, :]).astype(np.float32), BF16)


def _head_masks(width):
    idx = np.arange(width) // HEAD_DIM
    return jnp.asarray((idx[None, :] == np.arange(width // HEAD_DIM)[:, None]).astype(np.float32)[:, None, :], F32)


def _rms_stats(x):
    r = lax.rsqrt(jnp.mean(x * x, axis=-1, keepdims=True) + RMS_EPS)
    return r, x * r


def rms_fwd(x, gains, *, name):
    T, D = x.shape

    def fn(i, nblk, xb, *gs):
        _, xh = _rms_stats(xb)
        return tuple(xh * g for g in gs)

    return _rowmap(fn, name=name, T=T, tb=512, ins=[("row", x)] + [("const", g) for g in gains],
                   outs=[("row", D, MMD)] * len(gains))


def rms_bwd(x, gains, dhs, dres, *, name):
    T, D = x.shape
    n = len(gains)

    def fn(i, nblk, xb, dr, *rest):
        gs, ds = rest[:n], rest[n:]
        r, xh = _rms_stats(xb)
        dx = dr
        dgs = []
        for g, dh in zip(gs, ds):
            dgs.append(jnp.sum(dh * xh, axis=0, keepdims=True))
            dxh = dh * g
            dx = dx + r * (dxh - xh * jnp.mean(dxh * xh, axis=-1, keepdims=True))
        return (dx, *dgs)

    return _rowmap(fn, name=name, T=T, tb=512,
                   ins=[("row", x), ("row", dres)] + [("const", g) for g in gains] + [("row", d) for d in dhs],
                   outs=[("row", D, F32)], accs=[((1, D), F32)] * n)


def _segsum(x):
    first = lax.broadcasted_iota(jnp.int32, (x.shape[0], 128), 1) < HEAD_DIM
    outs = []
    for p in range(x.shape[1] // 128):
        xs = x[:, p * 128:(p + 1) * 128]
        lo = jnp.sum(jnp.where(first, xs, 0.0), axis=-1, keepdims=True)
        hi = jnp.sum(jnp.where(first, 0.0, xs), axis=-1, keepdims=True)
        outs.append(jnp.where(first, lo, hi))
    return jnp.concatenate(outs, axis=1)


def _pre1_common(i, ps, halo, mu, w0, a0, w2p, a2p, g2p, k_k, k_a):
    prev = _shift_down(ps, _row_pick(halo, 7), i == 0)
    xs = ps + (prev - ps) * mu
    lo = xs[:, 3 * RWKV_WIDTH:]
    tl, sl = jnp.tanh(lo), jax.nn.sigmoid(lo)
    dec = w0 + _dot(tl, w2p)
    ain = a0 + _dot(lo, a2p)
    g = _dot(sl, g2p)
    wl = -jax.nn.softplus(-dec) - 0.5
    w = jnp.exp(-jnp.exp(wl))
    a = jax.nn.sigmoid(ain)
    k = xs[:, RWKV_WIDTH:2 * RWKV_WIDTH]
    z = k * k_k
    nrm = jnp.sqrt(_segsum(z * z))
    kk = z / jnp.maximum(nrm, 1e-12)
    return prev, xs, lo, tl, sl, dec, wl, w, a, g, k, nrm, kk


def rwkv_pre_fwd(p, mu, w0, a0, w2p, a2p, g2p, k_k, k_a):
    T = p.shape[0]

    def fn(i, nblk, ps, halo, mu, w0, a0, w2p, a2p, g2p, k_k, k_a):
        _, xs, _, _, _, _, _, w, a, g, k, _, kk = _pre1_common(i, ps, halo, mu, w0, a0, w2p, a2p, g2p, k_k, k_a)
        W = RWKV_WIDTH
        return xs[:, :W], w, k * (1.0 + (a - 1.0) * k_a), xs[:, 2 * W:3 * W], kk, kk * a, g

    return _rowmap(fn, name="rwkv_pre_fwd", T=T, tb=256,
                   ins=[("row", p, SHIFT_WIDTH, 0), ("prev", p, SHIFT_WIDTH, 0)]
                   + [("const", c) for c in (mu, w0, a0, w2p, a2p, g2p, k_k, k_a)],
                   outs=[("row", RWKV_WIDTH, F32)] * 7)


def rwkv_pre_bwd(p, mu, w0, a0, w2p, a2p, g2p, k_k, k_a, drs, dw, dk2s, dvs, dkk, dkka, dg):
    T = p.shape[0]

    def fn(i, nblk, ps, halo, mu, w0, a0, w2p, a2p, g2p, k_k, k_a, dr0, dr1, dw, dk20, dk21, dv0, dv1, dkk, dkka, dg):
        prev, xs, lo, tl, sl, dec, wl, w, a, g, k, nrm, kk = _pre1_common(i, ps, halo, mu, w0, a0, w2p, a2p, g2p, k_k, k_a)
        dk2 = dk20 + dk21
        dkk_t = dkk + dkka * a
        proj = jnp.where(nrm > 1e-12, kk * _segsum(dkk_t * kk), 0.0)
        dz = (dkk_t - proj) / jnp.maximum(nrm, 1e-12)
        dk = dz * k_k + dk2 * (1.0 + (a - 1.0) * k_a)
        da = dkka * kk + dk2 * k * k_a
        ddec = dw * (-w * jnp.exp(wl)) * jax.nn.sigmoid(-dec)
        dain = da * a * (1.0 - a)
        dlo = (_dot_nt(ddec, w2p) * (1.0 - tl * tl) + _dot_nt(dain, a2p) + _dot_nt(dg, g2p) * sl * (1.0 - sl))
        dxs = jnp.concatenate([dr0 + dr1, dk, dv0 + dv1, dlo], axis=1)
        s = lambda z: jnp.sum(z, axis=0, keepdims=True)
        return (dxs, s(dxs * (prev - ps)), s(ddec), s(dain), _dot_tn(tl, ddec), _dot_tn(lo, dain), _dot_tn(sl, dg),
                s(dz * k), s(dk2 * k * (a - 1.0)))

    return _rowmap(fn, name="rwkv_pre_bwd", T=T, tb=128,
                   ins=[("row", p, SHIFT_WIDTH, 0), ("prev", p, SHIFT_WIDTH, 0)]
                   + [("const", c) for c in (mu, w0, a0, w2p, a2p, g2p, k_k, k_a)]
                   + [("row", c) for c in (*drs, dw, *dk2s, *dvs, dkk, dkka, dg)],
                   outs=[("row", SHIFT_WIDTH, F32)],
                   accs=[((1, SHIFT_WIDTH), F32), ((1, RWKV_WIDTH), F32), ((1, RWKV_WIDTH), F32)]
                   + [((LORA_WIDTH, RWKV_WIDTH), F32)] * 3 + [((1, RWKV_WIDTH), F32)] * 2)


def shift_bwd(dxs, mu, dq_mem):
    T = dxs.shape[0]

    def fn(i, nblk, d, halo, mu, dq):
        nxt = _shift_up(d, _row_pick(halo, 0), i == nblk - 1)
        return jnp.concatenate([d * (1.0 - mu) + nxt * mu, dq], axis=1)

    return _rowmap(fn, name="shift_bwd", T=T, tb=256,
                   ins=[("row", dxs), ("next", dxs), ("const", mu), ("row", dq_mem)],
                   outs=[("row", SHIFT_WIDTH + MEM_WIDTH, MMD)])[0]


N_PAIRS = RWKV_HEADS // 2


def _pair_consts():
    row = lax.broadcasted_iota(jnp.int32, (HEAD_DIM, 128), 0)
    lane = lax.broadcasted_iota(jnp.int32, (HEAD_DIM, 128), 1)
    eye2 = jnp.logical_or(lane == row, lane == row + HEAD_DIM).astype(F32)
    li = lax.broadcasted_iota(jnp.int32, (128, 128), 0) < HEAD_DIM
    lj = lax.broadcasted_iota(jnp.int32, (128, 128), 1) < HEAD_DIM
    return eye2, (li == lj).astype(BF16)


def _pair_sum(p, ones2):
    n, m, l = p.shape
    s = lax.dot_general(p.reshape(n * m, l).astype(BF16), ones2, (((1,), (0,)), ((), ())), preferred_element_type=F32)
    return s.reshape(n, m, l)


def _pair_rows(row):
    return jnp.stack([row[:, p * 128:(p + 1) * 128] for p in range(N_PAIRS)], axis=0)


def _pair_flat(rows):
    return jnp.concatenate([rows[p] for p in range(N_PAIRS)], axis=1)


def _split_bf16(v):
    hi = v.astype(BF16).astype(F32)
    return hi, v - hi


def _gather_copies(srcs, dsts, send_sems, recv_sems):
    x, y, c = _mesh_pos()
    s = 2 * x + y
    me, sibling = (x, y, c), (x, y, 1 - c)
    rc = functools.partial(_remote, send_sems, recv_sems)
    ici, land, fwd, arrived = [], [], [], []
    for b, (src, dst) in enumerate(zip(srcs, dsts)):
        for j, (cx, cy) in enumerate(_other_chips(x, y)):
            k = 6 * b + j
            ici.append(rc(k, src.at[c], dst.at[s, c], (cx, cy, c)))
            blk, blk2 = dst.at[2 * cx + cy, c], dst.at[2 * cx + cy, 1 - c]
            land.append(rc(k, blk, blk, me))
            fwd.append(rc(k + 3, blk, blk, sibling))
            arrived.append(rc(k + 3, blk2, blk2, me))
    return ici, land, fwd, arrived


def scan_fwd(r, w, k2, v, kk, kka, gather=None):
    T, W = r.shape
    tc = SCAN_CHUNK
    nchunk = T // tc
    seq = pl.BlockSpec((tc, W), lambda i: (i, 0))
    one_state = pl.BlockSpec((N_PAIRS, HEAD_DIM, 128), lambda i: (0, 0, 0))
    nb = 0 if gather is None else len(gather)

    def body(r_ref, w_ref, k2_ref, v_ref, kk_ref, kka_ref, *rest):
        if gather is None:
            y_ref, st_ref, fin_ref, s_scr, vhi_scr, vlo_scr = rest
        else:
            srcs, (y_ref, st_ref, fin_ref), dsts = rest[:nb], rest[nb:nb + 3], rest[nb + 3:2 * nb + 3]
            s_scr, vhi_scr, vlo_scr, send_sems, recv_sems = rest[2 * nb + 3:]
            ici, land, fwd, arrived = _gather_copies(srcs, dsts, send_sems, recv_sems)

            @pl.when(pl.program_id(0) == 0)
            def _():
                for cp in ici:
                    cp.start()

            @pl.when(pl.program_id(0) == nchunk // 2)
            def _():
                for a, f in zip(land, fwd):
                    a.wait_recv()
                    f.start()

        @pl.when(pl.program_id(0) == 0)
        def _():
            s_scr[...] = jnp.zeros(s_scr.shape, F32)

        vhi_scr[...], vlo_scr[...] = _split_bf16(v_ref[...])
        eye2, ones2 = _pair_consts()
        eye2b = eye2.astype(BF16)

        def step(t, carry):
            r_t, w_t, k2_t, kk_t, kka_t, vhi_t, vlo_t = (
                _pair_rows(ref[pl.ds(t, 1), :]) for ref in (r_ref, w_ref, k2_ref, kk_ref, kka_ref, vhi_scr, vlo_scr))
            S = s_scr[...]
            sa = -_pair_sum(S * kk_t, ones2)
            vb = _pair_sum(eye2b * vhi_t.astype(BF16), ones2) + _pair_sum(eye2b * vlo_t.astype(BF16), ones2)
            S2 = S * w_t + sa * kka_t + vb * k2_t
            y_ref[pl.ds(t, 1), :] = _pair_flat(jnp.sum(eye2 * _pair_sum(S2 * r_t, ones2), axis=1, keepdims=True))
            s_scr[...] = S2
            st_ref[t] = S
            return carry

        lax.fori_loop(0, tc, step, 0, unroll=SCAN_UNROLL)
        fin_ref[...] = s_scr[...]

        if gather is not None:
            @pl.when(pl.program_id(0) == nchunk - 1)
            def _():
                for a in arrived:
                    a.wait_recv()
                for cp in ici + fwd:
                    cp.wait_send()

    in_specs = [seq] * 6
    out_specs = [seq, pl.BlockSpec((tc, N_PAIRS, HEAD_DIM, 128), lambda i: (i, 0, 0, 0)), one_state]
    out_shape = [jax.ShapeDtypeStruct((T, W), F32), jax.ShapeDtypeStruct((T, N_PAIRS, HEAD_DIM, 128), F32),
                 jax.ShapeDtypeStruct((N_PAIRS, HEAD_DIM, 128), F32)]
    scratch = [pltpu.VMEM((N_PAIRS, HEAD_DIM, 128), F32), pltpu.VMEM((tc, W), F32), pltpu.VMEM((tc, W), F32)]
    args = [r, w, k2, v, kk, kka]
    if gather is not None:
        in_specs += [HBM_SPEC] * nb
        out_specs += [HBM_SPEC] * nb
        out_shape += [jax.ShapeDtypeStruct((N_CHIPS, *g.shape), g.dtype) for g in gather]
        scratch += [pltpu.SemaphoreType.DMA((6 * nb,)), pltpu.SemaphoreType.DMA((6 * nb,))]
        args += list(gather)
    return pl.pallas_call(
        body, name="rwkv_scan_fwd", grid=(nchunk,), in_specs=in_specs, out_specs=out_specs, out_shape=out_shape,
        scratch_shapes=scratch, compiler_params=_cp(("arbitrary",)),
    )(*args)


PEER_FLIPS = tuple((fx, fy, fc) for fx in (0, 1) for fy in (0, 1) for fc in (0, 1))[1:]


def scan_bwd(r, w, k2, v, kk, kka, states, final_state, dy, scatter=None):
    T, W = r.shape
    tc = SCAN_CHUNK
    nchunk = T // tc
    seq = pl.BlockSpec((tc, W), lambda i: (nchunk - 1 - i, 0))
    st_spec = pl.BlockSpec((tc, N_PAIRS, HEAD_DIM, 128), lambda i: (nchunk - 1 - i, 0, 0, 0))
    one_state = pl.BlockSpec((N_PAIRS, HEAD_DIM, 128), lambda i: (0, 0, 0))
    nb, npeer = (0 if scatter is None else len(scatter)), len(PEER_FLIPS)

    def body(r_ref, w_ref, k2_ref, v_ref, kk_ref, kka_ref, st_ref, fin_ref, dy_ref, *rest):
        if scatter is None:
            dr_ref, dw_ref, dk2_ref, dv_ref, dkk_ref, dkka_ref, ds_scr, sc_scr, vhi_scr, vlo_scr = rest
        else:
            srcs, dsts = rest[:nb], rest[nb + 6:2 * nb + 6]
            dr_ref, dw_ref, dk2_ref, dv_ref, dkk_ref, dkka_ref = rest[nb:nb + 6]
            ds_scr, sc_scr, vhi_scr, vlo_scr, send_sems, recv_sems = rest[2 * nb + 6:]
            x, y, c = _mesh_pos()
            copies = []
            for b, (src, dst) in enumerate(zip(srcs, dsts)):
                for k, (fx, fy, fc) in enumerate(PEER_FLIPS):
                    px, py, pc = (1 - x if fx else x), (1 - y if fy else y), (1 - c if fc else c)
                    copies.append(_remote(send_sems, recv_sems, npeer * b + k, src.at[2 * px + py, pc], dst.at[k],
                                          (px, py, pc)))

            @pl.when(pl.program_id(0) == 0)
            def _():
                for cp in copies:
                    cp.start()

        @pl.when(pl.program_id(0) == 0)
        def _():
            ds_scr[...] = jnp.zeros(ds_scr.shape, F32)
            sc_scr[...] = fin_ref[...]

        vhi_scr[...], vlo_scr[...] = _split_bf16(v_ref[...])
        eye2, ones2 = _pair_consts()
        eye2b = eye2.astype(BF16)
        colsum = lambda z: jnp.sum(z, axis=1, keepdims=True)

        def step(j, carry):
            t = tc - 1 - j
            r_t, w_t, k2_t, kk_t, kka_t, vhi_t, vlo_t, dy_t = (
                _pair_rows(ref[pl.ds(t, 1), :])
                for ref in (r_ref, w_ref, k2_ref, kk_ref, kka_ref, vhi_scr, vlo_scr, dy_ref))
            s_prev, s_cur = st_ref[t], sc_scr[...]
            dyb = _pair_sum(eye2b * dy_t.astype(BF16), ones2)
            vb = _pair_sum(eye2b * vhi_t.astype(BF16), ones2) + _pair_sum(eye2b * vlo_t.astype(BF16), ones2)
            sa = -_pair_sum(s_prev * kk_t, ones2)
            dS = ds_scr[...] + dyb * r_t
            dS_b = dS.astype(BF16)
            dsa = _pair_sum(dS_b * kka_t.astype(BF16), ones2)
            ds_scr[...] = dS * w_t - dsa * kk_t
            sc_scr[...] = s_prev
            for ref, val in zip((dr_ref, dw_ref, dk2_ref, dv_ref, dkk_ref, dkka_ref),
                                (s_cur * dyb, dS * s_prev, dS * vb, eye2 * _pair_sum(dS_b * k2_t.astype(BF16), ones2),
                                 -(s_prev * dsa), dS * sa)):
                ref[pl.ds(t, 1), :] = _pair_flat(colsum(val))
            return carry

        lax.fori_loop(0, tc, step, 0, unroll=SCAN_UNROLL_BWD)

        if scatter is not None:
            @pl.when(pl.program_id(0) == nchunk - 1)
            def _():
                for cp in copies:
                    cp.wait()

    in_specs = [seq] * 6 + [st_spec, one_state, seq]
    out_specs = [seq] * 6
    out_shape = [jax.ShapeDtypeStruct((T, W), F32)] * 6
    scratch = [pltpu.VMEM((N_PAIRS, HEAD_DIM, 128), F32)] * 2 + [pltpu.VMEM((tc, W), F32)] * 2
    args = [r, w, k2, v, kk, kka, states, final_state, dy]
    if scatter is not None:
        in_specs += [HBM_SPEC] * nb
        out_specs += [HBM_SPEC] * nb
        out_shape += [jax.ShapeDtypeStruct((npeer, *s.shape[2:]), s.dtype) for s in scatter]
        scratch += [pltpu.SemaphoreType.DMA((npeer * nb,)), pltpu.SemaphoreType.DMA((npeer * nb,))]
        args += list(scatter)
    return pl.pallas_call(
        body, name="rwkv_scan_bwd", grid=(nchunk,), in_specs=in_specs, out_specs=out_specs, out_shape=out_shape,
        scratch_shapes=scratch, compiler_params=_cp(("arbitrary",)),
    )(*args)


def _mix_common(y, r, k2, v, lnx_w, lnx_b, r_k):
    yc = y - _segsum(y) * (1.0 / HEAD_DIM)
    rstd = lax.rsqrt(_segsum(yc * yc) * (1.0 / HEAD_DIM) + LNX_EPS)
    yhat = yc * rstd
    s = _segsum(r * k2 * r_k)
    return rstd, yhat, s, yhat * lnx_w + lnx_b + s * v


def mix_gate_fwd(y, r, k2, v, g, y_mem, lnx_w, lnx_b, r_k):
    T = y.shape[0]

    def fn(i, nblk, y, r, k2, v, g, ym, lw, lb, rk):
        mix = _mix_common(y, r, k2, v, lw, lb, rk)[3]
        return jnp.concatenate([mix * g, ym], axis=1)

    return _rowmap(fn, name="mix_gate_fwd", T=T, tb=256,
                   ins=[("row", z) for z in (y, r, k2, v, g, y_mem)] + [("const", c) for c in (lnx_w, lnx_b, r_k)],
                   outs=[("row", RWKV_WIDTH + MEM_WIDTH, MMD)])[0]


def mix_gate_bwd(y, r, k2, v, g, dycat, lnx_w, lnx_b, r_k):
    T = y.shape[0]

    def fn(i, nblk, y, r, k2, v, g, dyc, lw, lb, rk):
        rstd, yhat, s, mix = _mix_common(y, r, k2, v, lw, lb, rk)
        dmix = dyc * g
        dyh = dmix * lw
        inv = 1.0 / HEAD_DIM
        dy = rstd * (dyh - _segsum(dyh) * inv - yhat * (_segsum(dyh * yhat) * inv))
        ds = _segsum(dmix * v)
        cs = lambda z: jnp.sum(z, axis=0, keepdims=True)
        return (dy, ds * k2 * rk, ds * r * rk, dmix * s, dyc * mix, cs(dmix * yhat), cs(dmix), cs(ds * r * k2))

    return _rowmap(fn, name="mix_gate_bwd", T=T, tb=256,
                   ins=[("row", z) for z in (y, r, k2, v, g)] + [("row", dycat, RWKV_WIDTH, 0)]
                   + [("const", c) for c in (lnx_w, lnx_b, r_k)],
                   outs=[("row", RWKV_WIDTH, F32)] * 5, accs=[((1, RWKV_WIDTH), F32)] * 3)


def _head_rms(x):
    ms = _segsum(x * x) * (1.0 / HEAD_DIM)
    r = lax.rsqrt(ms + RMS_EPS)
    return r, x * r


def _head_rms_bwd(dxn_g, r, xh):
    return r * (dxn_g - xh * (_segsum(dxn_g * xh) * (1.0 / HEAD_DIM)))


def mem_kv_fwd(mem, norm_g, w_kv, k_norm_t, *, name):
    def body(mem_ref, g_ref, w_ref, kn_ref, k_out, v_out):
        _, xh = _rms_stats(mem_ref[...])
        kv = _dot(xh * g_ref[...], w_ref[...])
        _, kh = _head_rms(kv[:, :MEM_WIDTH])
        k_out[...] = kh * kn_ref[...]
        v_out[...] = kv[:, MEM_WIDTH:]

    return pl.pallas_call(
        body, name=name, out_shape=[jax.ShapeDtypeStruct((N_MEM, MEM_WIDTH), F32)] * 2, compiler_params=_cp(),
    )(mem, norm_g, w_kv, k_norm_t)


def mem_kv_bwd(mem, norm_g, w_kv, k_norm_t, dkn, dv, *, name):
    fold = _fold_ones(MEM_WIDTH)

    def body(mem_ref, g_ref, w_ref, kn_ref, fo_ref, dkn_ref, dv_ref, dw_out, dg_out, dkg_out):
        _, xh = _rms_stats(mem_ref[...])
        hm = xh * g_ref[...]
        kv = _dot(hm, w_ref[...])
        r, kh = _head_rms(kv[:, :MEM_WIDTH])
        dkn = dkn_ref[...]
        dkg_out[...] = _fold_heads(jnp.sum(dkn * kh, axis=0, keepdims=True), fo_ref[...])
        dkraw = _head_rms_bwd(dkn * kn_ref[...], r, kh)
        dkv = jnp.concatenate([dkraw, dv_ref[...]], axis=1)
        dw_out[...] = _dot_tn(hm, dkv)
        dg_out[...] = jnp.sum(_dot_nt(dkv, w_ref[...]) * xh, axis=0, keepdims=True)

    return pl.pallas_call(
        body, name=name,
        out_shape=[jax.ShapeDtypeStruct((D_MODEL, 2 * MEM_WIDTH), F32), jax.ShapeDtypeStruct((1, D_MODEL), F32),
                   jax.ShapeDtypeStruct((1, HEAD_DIM), F32)],
        compiler_params=_cp(),
    )(mem, norm_g, w_kv, k_norm_t, fold, dkn, dv)


def _mem_scores(qn, kn, masks, h):
    s = _dot_nt(qn * masks[h], kn) * (1.0 / math.sqrt(HEAD_DIM))
    s = s - jnp.max(s, axis=-1, keepdims=True)
    e = jnp.exp(s)
    return e / jnp.sum(e, axis=-1, keepdims=True)


def mem_attn_fwd(p, colblock, kn, v, q_norm_t, *, name):
    T = p.shape[0]
    masks = _head_masks(MEM_WIDTH)

    def fn(i, nblk, q, kn, v, qg, masks):
        _, qh = _head_rms(q)
        qn = qh * qg
        out = jnp.zeros(q.shape, F32)
        for h in range(MEM_WIDTH // HEAD_DIM):
            out = out + _dot(_mem_scores(qn, kn, masks, h), v * masks[h])
        return out

    return _rowmap(fn, name=name, T=T, tb=512,
                   ins=[("row", p, MEM_WIDTH, colblock)] + [("const", c) for c in (kn, v, q_norm_t, masks)],
                   outs=[("row", MEM_WIDTH, F32)])[0]


def mem_attn_bwd(p, colblock, kn, v, q_norm_t, dycat, dcolblock, *, name):
    T = p.shape[0]
    masks, fold = _head_masks(MEM_WIDTH), _fold_ones(MEM_WIDTH)
    scale = 1.0 / math.sqrt(HEAD_DIM)

    def fn(i, nblk, q, dy, kn, v, qg, masks, fo):
        r, qh = _head_rms(q)
        qn = qh * qg
        dqn = jnp.zeros(q.shape, F32)
        dkn = jnp.zeros(kn.shape, F32)
        dv = jnp.zeros(v.shape, F32)
        for h in range(MEM_WIDTH // HEAD_DIM):
            pr = _mem_scores(qn, kn, masks, h)
            dyh = dy * masks[h]
            dpr = _dot_nt(dyh, v)
            ds = pr * (dpr - jnp.sum(dpr * pr, axis=-1, keepdims=True)) * scale
            dqn = dqn + _dot(ds, kn * masks[h])
            dkn = dkn + _dot_tn(ds, qn * masks[h])
            dv = dv + _dot_tn(pr, dyh)
        dqg = _fold_heads(jnp.sum(dqn * qh, axis=0, keepdims=True), fo)
        return _head_rms_bwd(dqn * qg, r, qh), dkn, dv, dqg

    return _rowmap(fn, name=name, T=T, tb=512,
                   ins=[("row", p, MEM_WIDTH, colblock), ("row", dycat, MEM_WIDTH, dcolblock)]
                   + [("const", c) for c in (kn, v, q_norm_t, masks, fold)],
                   outs=[("row", MEM_WIDTH, F32)],
                   accs=[((N_MEM, MEM_WIDTH), F32), ((N_MEM, MEM_WIDTH), F32), ((1, HEAD_DIM), F32)])


def _ffn_conv(i, u, halo, cw, cb):
    up1 = _shift_down(u, _row_pick(halo, 7), i == 0)
    up2 = _shift_down(up1, _row_pick(halo, 6), i == 0)
    c = cb + cw[0] * up2 + cw[1] * up1 + cw[2] * u
    return up1, up2, c[:, :D_FF], c[:, D_FF:]


def ffn_act_fwd(u, cw, cb, *, name):
    T = u.shape[0]

    def fn(i, nblk, u, halo, c0, c1, c2, cb):
        _, _, gate, val = _ffn_conv(i, u, halo, (c0, c1, c2), cb)
        return jax.nn.silu(gate) * val

    return _rowmap(fn, name=name, T=T, tb=128, ins=[("row", u), ("prev", u)] + [("const", c) for c in (*cw, cb)],
                   outs=[("row", D_FF, MMD)])[0]


def ffn_act_bwd(u, cw, cb, dz, *, name):
    T = u.shape[0]
    tb = 128

    def fn(i, nblk, u, halo, unext, c0, c1, c2, cb, dz, dznext):
        ue = jnp.concatenate([u, unext], axis=0)
        dze = jnp.concatenate([dz, jnp.where(i == nblk - 1, 0.0, 1.0) * dznext], axis=0)
        up1, up2, gate, val = _ffn_conv(i, ue, halo, (c0, c1, c2), cb)
        sg = jax.nn.sigmoid(gate)
        dce = jnp.concatenate([dze * val * sg * (1.0 + gate * (1.0 - sg)), dze * gate * sg], axis=1)
        rows = dce.shape[0]
        du = (c2 * dce + c1 * pltpu.roll(dce, rows - 1, axis=0) + c0 * pltpu.roll(dce, rows - 2, axis=0))[:tb]
        dc = dce[:tb]
        s = lambda z: jnp.sum(z, axis=0, keepdims=True)
        return du, s(dc * up2[:tb]), s(dc * up1[:tb]), s(dc * u), s(dc)

    return _rowmap(fn, name=name, T=T, tb=tb,
                   ins=[("row", u), ("prev", u), ("next", u)] + [("const", c) for c in (*cw, cb)]
                   + [("row", dz), ("next", dz)],
                   outs=[("row", 2 * D_FF, MMD)], accs=[((1, 2 * D_FF), F32)] * 4)


def _rope_swap(z):
    lane = lax.broadcasted_iota(jnp.int32, z.shape, 1) % HEAD_DIM
    w = z.shape[1]
    return jnp.where(lane < HEAD_DIM // 2, pltpu.roll(z, w - HEAD_DIM // 2, axis=1), pltpu.roll(z, HEAD_DIM // 2, axis=1))


def rope_tables(T):
    inv = (np.float32(ROPE_THETA) ** (-np.arange(0, HEAD_DIM, 2, dtype=np.float32) / np.float32(HEAD_DIM))).astype(np.float32)
    ang = (np.arange(T, dtype=np.float32)[:, None] * inv[None, :]).astype(np.float64)
    cos, sin = np.cos(ang).astype(np.float32), np.sin(ang).astype(np.float32)
    return (jnp.asarray(np.concatenate([cos, cos, cos, cos], axis=1)),
            jnp.asarray(np.concatenate([-sin, sin, -sin, sin], axis=1)))


def _rope_wide(t):
    return jnp.tile(t, (1, DIL_WIDTH // t.shape[1]))


def qk_fwd(kvp, pb, kg_t, qg_t, cos, sin):
    T = kvp.shape[0]

    def fn(i, nblk, kraw, vraw, qraw, kg, qg, c, s):
        c, s = _rope_wide(c), _rope_wide(s)
        outs = []
        for raw, g in ((qraw, qg), (kraw, kg)):
            _, xh = _head_rms(raw)
            z = xh * g
            outs.append(z * c + _rope_swap(z) * s)
        return outs[0], outs[1], vraw

    return _rowmap(fn, name="qk_fwd", T=T, tb=256,
                   ins=[("row", kvp, DIL_WIDTH, 0), ("row", kvp, DIL_WIDTH, 1), ("row", pb, DIL_WIDTH, 0)]
                   + [("const", kg_t), ("const", qg_t), ("row", cos), ("row", sin)],
                   outs=[("row", DIL_WIDTH, MMD)] * 3)


def qk_bwd(kvp, pb, kg_t, qg_t, cos, sin, dq, dk, dv, dq_mem):
    T = kvp.shape[0]
    fold = _fold_ones(DIL_WIDTH)

    def fn(i, nblk, kraw, qraw, kg, qg, c, s, fo, dq, dk, dv, dqm):
        c, s = _rope_wide(c), _rope_wide(s)
        res, dgs = [], []
        for raw, g, d in ((qraw, qg, dq), (kraw, kg, dk)):
            r, xh = _head_rms(raw)
            dz = d * c + _rope_swap(d * s)
            dgs.append(_fold_heads(jnp.sum(dz * xh, axis=0, keepdims=True), fo))
            res.append(_head_rms_bwd(dz * g, r, xh))
        return (jnp.concatenate([res[0], dqm], axis=1), jnp.concatenate([res[1], dv], axis=1), dgs[0], dgs[1])

    return _rowmap(fn, name="qk_bwd", T=T, tb=256,
                   ins=[("row", kvp, DIL_WIDTH, 0), ("row", pb, DIL_WIDTH, 0), ("const", kg_t), ("const", qg_t),
                        ("row", cos), ("row", sin), ("const", fold),
                        ("row", dq), ("row", dk), ("row", dv), ("row", dq_mem)],
                   outs=[("row", DIL_WIDTH + MEM_WIDTH, MMD), ("row", 2 * DIL_WIDTH, MMD)],
                   accs=[((1, HEAD_DIM), F32)] * 2)


def _band(kind):
    i = lax.broadcasted_iota(jnp.int32, (DIL_BLOCK, DIL_BLOCK), 0)
    j = lax.broadcasted_iota(jnp.int32, (DIL_BLOCK, DIL_BLOCK), 1)
    return (j <= i) if kind == "cur" else (j >= i)


def dil_attn_fwd(q, k, v, seq_blocks, *, name):
    T, W = q.shape
    nb = T // DIL_BLOCK
    masks = _head_masks(W)
    cur = pl.BlockSpec((DIL_BLOCK, W), lambda n: (n, 0))
    prv = pl.BlockSpec((DIL_BLOCK, W), lambda n: (jnp.maximum(n - 1, 0), 0))
    scale = 1.0 / math.sqrt(HEAD_DIM)

    def body(q_ref, kc_ref, kp_ref, vc_ref, vp_ref, m_ref, o_ref, l_ref):
        n = pl.program_id(0)
        has_prev = (n % seq_blocks) != 0
        q = q_ref[...].astype(F32)
        kc, kp = kc_ref[...].astype(F32), kp_ref[...].astype(F32)
        vc, vp = vc_ref[...].astype(F32), vp_ref[...].astype(F32)
        ok_c = _band("cur")
        ok_p = jnp.logical_and(_band("prev"), has_prev)
        o = jnp.zeros((DIL_BLOCK, W), F32)
        lse = jnp.zeros((DIL_BLOCK, W), F32)
        for h in range(W // HEAD_DIM):
            mh = m_ref[h]
            qh = q * mh
            sc = jnp.where(ok_c, _dot_nt(qh, kc) * scale, NEG_INF)
            sp = jnp.where(ok_p, _dot_nt(qh, kp) * scale, NEG_INF)
            mx = jnp.maximum(jnp.max(sc, axis=-1, keepdims=True), jnp.max(sp, axis=-1, keepdims=True))
            ec, ep = jnp.exp(sc - mx), jnp.exp(sp - mx)
            den = jnp.sum(ec, axis=-1, keepdims=True) + jnp.sum(ep, axis=-1, keepdims=True)
            o = o + (_dot(ec, vc * mh) + _dot(ep, vp * mh)) / den
            lse = lse + (mx + jnp.log(den)) * mh
        o_ref[...] = o
        l_ref[...] = lse

    return pl.pallas_call(
        body, name=name, grid=(nb,), in_specs=[cur, cur, prv, cur, prv, pl.BlockSpec(masks.shape, lambda n: (0, 0, 0))],
        out_specs=[cur, cur], out_shape=[jax.ShapeDtypeStruct((T, W), F32)] * 2,
        compiler_params=_cp(("parallel",)),
    )(q, k, k, v, v, masks)


def dil_attn_bwd(q, k, v, o, lse, do, dlse, seq_blocks, *, name):
    T, W = q.shape
    nb = T // DIL_BLOCK
    masks = _head_masks(W)
    cur = pl.BlockSpec((DIL_BLOCK, W), lambda n: (n, 0))
    prv = pl.BlockSpec((DIL_BLOCK, W), lambda n: (jnp.maximum(n - 1, 0), 0))
    nxt = pl.BlockSpec((DIL_BLOCK, W), lambda n: (jnp.minimum(n + 1, nb - 1), 0))
    scale = 1.0 / math.sqrt(HEAD_DIM)

    def body(qc_ref, qn_ref, kc_ref, kp_ref, vc_ref, vp_ref, oc_ref, on_ref, lc_ref, ln_ref, doc_ref, don_ref,
             dlc_ref, dln_ref, m_ref, dq_ref, dk_ref, dv_ref):
        n = pl.program_id(0)
        has_prev = (n % seq_blocks) != 0
        has_next = jnp.logical_and(((n + 1) % seq_blocks) != 0, n + 1 < nb)
        f = lambda ref: ref[...].astype(F32)
        qc, qn, kc, kp, vc, vp = f(qc_ref), f(qn_ref), f(kc_ref), f(kp_ref), f(vc_ref), f(vp_ref)
        doc, don = doc_ref[...], don_ref[...]
        ok_c = _band("cur")
        ok_p = jnp.logical_and(_band("prev"), has_prev)
        ok_n = jnp.logical_and(_band("prev"), has_next)
        dq = jnp.zeros((DIL_BLOCK, W), F32)
        dk = jnp.zeros((DIL_BLOCK, W), F32)
        dv = jnp.zeros((DIL_BLOCK, W), F32)

        def side(qh, kk, vv, doh, lse_h, corr, ok):
            s = _dot_nt(qh, kk) * scale
            pr = jnp.where(ok, jnp.exp(jnp.where(ok, s, NEG_INF) - lse_h), 0.0)
            ds = pr * (_dot_nt(doh, vv) + corr) * scale
            return pr, ds

        for h in range(W // HEAD_DIM):
            mh = m_ref[h]
            red = lambda z: jnp.sum(z * mh, axis=-1, keepdims=True)
            qh, doh = qc * mh, doc * mh
            lse_h = red(lc_ref[...]) * (1.0 / HEAD_DIM)
            corr = red(dlc_ref[...]) - red(doc * oc_ref[...])
            pr_c, ds_c = side(qh, kc, vc * mh, doh, lse_h, corr, ok_c)
            _, ds_p = side(qh, kp, vp * mh, doh, lse_h, corr, ok_p)
            dq = dq + _dot(ds_c, kc * mh) + _dot(ds_p, kp * mh)
            dk = dk + _dot_tn(ds_c, qh)
            dv = dv + _dot_tn(pr_c, doh)
            qh2, doh2 = qn * mh, don * mh
            lse_2 = red(ln_ref[...]) * (1.0 / HEAD_DIM)
            corr2 = red(dln_ref[...]) - red(don * on_ref[...])
            pr_n, ds_n = side(qh2, kc, vc * mh, doh2, lse_2, corr2, ok_n)
            dk = dk + _dot_tn(ds_n, qh2)
            dv = dv + _dot_tn(pr_n, doh2)
        dq_ref[...] = dq
        dk_ref[...] = dk
        dv_ref[...] = dv

    return pl.pallas_call(
        body, name=name, grid=(nb,),
        in_specs=[cur, nxt, cur, prv, cur, prv, cur, nxt, cur, nxt, cur, nxt, cur, nxt,
                  pl.BlockSpec(masks.shape, lambda n: (0, 0, 0))],
        out_specs=[cur] * 3, out_shape=[jax.ShapeDtypeStruct((T, W), F32)] * 3,
        compiler_params=_cp(("parallel",)),
    )(q, q, k, k, v, v, o, o, lse, lse, do, do, dlse, dlse, masks)


def _mix_weights(ls):
    m = jnp.maximum(jnp.maximum(ls[0], ls[1]), ls[2])
    es = [jnp.exp(l - m) for l in ls]
    den = es[0] + es[1] + es[2]
    return [e / den for e in es]


def mix_fwd(os_, ls, y_mem):
    T = y_mem.shape[0]

    def fn(i, nblk, o0, o1, o2, l0, l1, l2, ym):
        w = _mix_weights((l0, l1, l2))
        return jnp.concatenate([w[0] * o0 + w[1] * o1 + w[2] * o2, ym], axis=1)

    return _rowmap(fn, name="mix_fwd", T=T, tb=512, ins=[("row", z) for z in (*os_, *ls, y_mem)],
                   outs=[("row", 2 * MEM_WIDTH, MMD)])[0]


def mix_bwd(os_, ls, dycat):
    T = dycat.shape[0]

    def fn(i, nblk, o0, o1, o2, l0, l1, l2, dy):
        w = _mix_weights((l0, l1, l2))
        os3 = (o0, o1, o2)
        dws = [dy * o for o in os3]
        tot = w[0] * dws[0] + w[1] * dws[1] + w[2] * dws[2]
        return tuple(wg * dy for wg in w) + tuple(wg * (dw - tot) for wg, dw in zip(w, dws))

    return _rowmap(fn, name="mix_bwd", T=T, tb=512,
                   ins=[("row", z) for z in (*os_, *ls)] + [("row", dycat, MEM_WIDTH, 0)],
                   outs=[("row", MEM_WIDTH, F32)] * 6)


def loss_fwd_bwd(y, target):
    T, D = y.shape

    def fn(i, nblk, y, t):
        e = y - t
        return e * (1.0 / D), jnp.zeros((8, 128), F32) + jnp.sum(e * e) * (0.5 / D)

    return _rowmap(fn, name="loss", T=T, tb=512, ins=[("row", y), ("row", target)], outs=[("row", D, F32)],
                   accs=[((8, 128), F32)])


def _to_residues(z, dil):
    T, W = z.shape
    return z.reshape(T // dil, dil, W).transpose(1, 0, 2).reshape(T, W)


def _from_residues(z, dil):
    T, W = z.shape
    return z.reshape(dil, T // dil, W).transpose(1, 0, 2).reshape(T, W)


def _pad_rows(w, rows):
    return jnp.concatenate([w, jnp.zeros((rows - w.shape[0], w.shape[1]), w.dtype)], axis=0)


def _tile_heads(g, width):
    return jnp.tile(g.reshape(1, HEAD_DIM), (1, width // HEAD_DIM))


def _conv_rows(W, i):
    return [W["ffn_conv_w"][i][j:j + 1] for j in range(3)]


def _ffn_fwd(x, i, W):
    hn = rms_fwd(x, [W["ffn_norm"][i:i + 1]], name=f"ffn_rms{i}")[0]
    u = _mm(hn, W["ffn_w_up"][i], name=f"ffn_up{i}")
    z = ffn_act_fwd(u, _conv_rows(W, i), W["ffn_conv_b"][i:i + 1], name=f"ffn_act{i}")
    out = _mm(z, W["ffn_w_down"][i], add=x, name=f"ffn_down{i}")
    return out, (x, hn, u, z)


def _ffn_bwd(dout, i, W, saved, G):
    x, hn, u, z = saved
    dz = _mm(dout, W["ffn_w_down"][i], tb=True, name=f"ffn_down_dx{i}")
    G["ffn_w_down"][i] = _mm(z, dout, ta=True, name=f"ffn_down_dw{i}")
    du, dw0, dw1, dw2, db = ffn_act_bwd(u, _conv_rows(W, i), W["ffn_conv_b"][i:i + 1], dz, name=f"ffn_act_bwd{i}")
    G["ffn_conv_w"][i] = jnp.concatenate([dw0, dw1, dw2], axis=0)
    G["ffn_conv_b"][i] = db[0]
    dhn = _mm(du, W["ffn_w_up"][i], tb=True, name=f"ffn_up_dx{i}")
    G["ffn_w_up"][i] = _mm(hn, du, ta=True, name=f"ffn_up_dw{i}")
    dx, dg = rms_bwd(x, [W["ffn_norm"][i:i + 1]], [dhn], dout, name=f"ffn_rms_bwd{i}")
    G["ffn_norm"][i] = dg[0]
    return dx


def local_step(x, mem, target, W, late=None):
    T = x.shape[0]
    W = dict(W)
    G = {"ffn_w_down": [None, None], "ffn_w_up": [None, None], "ffn_conv_w": [None, None],
         "ffn_conv_b": [None, None], "ffn_norm": [None, None], "attn_norm": [None, None], "mem_norm": [None, None],
         "mem_w_kv": [None, None], "mem_q_norm": [None, None], "mem_k_norm": [None, None]}
    mu, w0, a0 = W["a_mu"], W["a_w0"], W["a_a0"]
    w2p, a2p, g2p = (_pad_rows(W["a_w2"][0], LORA_WIDTH),
                     jnp.concatenate([jnp.zeros((64, RWKV_WIDTH), MMD), W["a_a2"][0],
                                      jnp.zeros((128, RWKV_WIDTH), MMD)], axis=0),
                     jnp.concatenate([jnp.zeros((128, RWKV_WIDTH), MMD), W["a_g2"][0]], axis=0))
    k_k, k_a, lnx_w, lnx_b = W["a_k_k"], W["a_k_a"], W["a_lnx_w"], W["a_lnx_b"]
    r_k = W["a_r_k"].reshape(1, RWKV_WIDTH)

    h0 = rms_fwd(x, [W["attn_norm"][0:1]], name="attn_rms0")[0]
    p = _mm(h0, W["a_w_in"][0], name="a_in")
    r, w, k2, v, kk, kka, g = rwkv_pre_fwd(p, mu, w0, a0, w2p, a2p, g2p, k_k, k_a)
    if late is None:
        y, states, final_state = scan_fwd(r, w, k2, v, kk, kka)
    else:
        y, states, final_state, *gathered = scan_fwd(r, w, k2, v, kk, kka, gather=late[0])
        W.update(late[1](gathered))
    memkv = []
    for i in range(2):
        memkv.append(mem_kv_fwd(mem, W["mem_norm"][i:i + 1], W["mem_w_kv"][i], _tile_heads(W["mem_k_norm"][i], MEM_WIDTH),
                                name=f"mem_kv{i}"))
    qg0 = _tile_heads(W["mem_q_norm"][0], MEM_WIDTH)
    y_mem0 = mem_attn_fwd(p, SHIFT_WIDTH // MEM_WIDTH, memkv[0][0], memkv[0][1], qg0, name="mem_attn0")
    ycat0 = mix_gate_fwd(y, r, k2, v, g, y_mem0, lnx_w, lnx_b, r_k)
    x1 = _mm(ycat0, W["a_w_out"][0], add=x, name="a_out")
    x2, ffn0 = _ffn_fwd(x1, 0, W)

    h1, hkv = rms_fwd(x2, [W["attn_norm"][1:2], W["kv_norm"].reshape(1, -1)], name="attn_rms1")
    kvp = _mm(hkv, W["kv_w"], name="kv_in")
    pb = _mm(h1, W["b_w_in"][0], name="b_in")
    cos, sin = rope_tables(T)
    kg_t, qg_t = _tile_heads(W["kv_k_norm"], DIL_WIDTH), _tile_heads(W["b_q_norm"][0], DIL_WIDTH)
    q, ksh, vsh = qk_fwd(kvp, pb, kg_t, qg_t, cos, sin)
    os_, ls, grp = [], [], []
    for gi, (win, dil) in enumerate(DIL_GROUPS):
        sl = slice(gi * MEM_WIDTH, (gi + 1) * MEM_WIDTH)
        qg_, kg_, vg_ = (_to_residues(z[:, sl], dil) for z in (q, ksh, vsh))
        o_r, l_r = dil_attn_fwd(qg_, kg_, vg_, T // dil // DIL_BLOCK, name=f"dil_fwd{gi}")
        grp.append((qg_, kg_, vg_, o_r, l_r))
        os_.append(_from_residues(o_r, dil))
        ls.append(_from_residues(l_r, dil))
    qg1 = _tile_heads(W["mem_q_norm"][1], MEM_WIDTH)
    y_mem1 = mem_attn_fwd(pb, DIL_WIDTH // MEM_WIDTH, memkv[1][0], memkv[1][1], qg1, name="mem_attn1")
    ycat1 = mix_fwd(os_, ls, y_mem1)
    x3 = _mm(ycat1, W["b_w_out"][0], add=x2, name="b_out")
    x4, ffn1 = _ffn_fwd(x3, 1, W)

    dx4, loss = loss_fwd_bwd(x4, target)

    dx3 = _ffn_bwd(dx4, 1, W, ffn1, G)
    dycat1 = _mm(dx3, W["b_w_out"][0], tb=True, name="b_out_dx")
    G["b_w_out"] = _mm(ycat1, dx3, ta=True, name="b_out_dw")[None]
    dq_mem1, dkn1, dvm1, dqg1 = mem_attn_bwd(pb, DIL_WIDTH // MEM_WIDTH, memkv[1][0], memkv[1][1], qg1, dycat1, 1,
                                             name="mem_attn_bwd1")
    G["mem_q_norm"][1] = dqg1[0]
    d_os_ls = mix_bwd(os_, ls, dycat1)
    dqs, dks, dvs = [], [], []
    for gi, (win, dil) in enumerate(DIL_GROUPS):
        qg_, kg_, vg_, o_r, l_r = grp[gi]
        do_r, dl_r = _to_residues(d_os_ls[gi], dil), _to_residues(d_os_ls[3 + gi], dil)
        dq_r, dk_r, dv_r = dil_attn_bwd(qg_, kg_, vg_, o_r, l_r, do_r, dl_r, T // dil // DIL_BLOCK, name=f"dil_bwd{gi}")
        dqs.append(_from_residues(dq_r, dil))
        dks.append(_from_residues(dk_r, dil))
        dvs.append(_from_residues(dv_r, dil))
    dq, dk, dv = (jnp.concatenate(z, axis=1) for z in (dqs, dks, dvs))
    dpb, dkvp, dqn_g, dkn_g = qk_bwd(kvp, pb, kg_t, qg_t, cos, sin, dq, dk, dv, dq_mem1)
    G["b_q_norm"] = dqn_g
    G["kv_k_norm"] = dkn_g[0]
    dh1 = _mm(dpb, W["b_w_in"][0], tb=True, name="b_in_dx")
    G["b_w_in"] = _mm(h1, dpb, ta=True, name="b_in_dw")[None]
    dhkv = _mm(dkvp, W["kv_w"], tb=True, name="kv_in_dx")
    G["kv_w"] = _mm(hkv, dkvp, ta=True, name="kv_in_dw")
    dx2, dg1, dgkv = rms_bwd(x2, [W["attn_norm"][1:2], W["kv_norm"].reshape(1, -1)], [dh1, dhkv], dx3,
                             name="attn_rms_bwd1")
    G["attn_norm"][1] = dg1[0]
    G["kv_norm"] = dgkv[0]

    dx1 = _ffn_bwd(dx2, 0, W, ffn0, G)
    dycat0 = _mm(dx1, W["a_w_out"][0], tb=True, name="a_out_dx")
    G["a_w_out"] = _mm(ycat0, dx1, ta=True, name="a_out_dw")[None]
    dq_mem0, dkn0, dvm0, dqg0 = mem_attn_bwd(p, SHIFT_WIDTH // MEM_WIDTH, memkv[0][0], memkv[0][1], qg0, dycat0,
                                             RWKV_WIDTH // MEM_WIDTH, name="mem_attn_bwd0")
    G["mem_q_norm"][0] = dqg0[0]
    dy, dr_b, dk2_b, dv_b, dg, dlw, dlb, drk = mix_gate_bwd(y, r, k2, v, g, dycat0, lnx_w, lnx_b, r_k)
    for i, (dkn, dvm) in enumerate(((dkn0, dvm0), (dkn1, dvm1))):
        dwkv, dgm, dkg = mem_kv_bwd(mem, W["mem_norm"][i:i + 1], W["mem_w_kv"][i],
                                    _tile_heads(W["mem_k_norm"][i], MEM_WIDTH), dkn, dvm, name=f"mem_kv_bwd{i}")
        G["mem_w_kv"][i], G["mem_norm"][i], G["mem_k_norm"][i] = dwkv, dgm[0], dkg[0]
    late_out = None
    if late is None:
        dr, dw, dk2, dv, dkk, dkka = scan_bwd(r, w, k2, v, kk, kka, states, final_state, dy)
    else:
        pieces = late[2](G)
        dr, dw, dk2, dv, dkk, dkka, *received = scan_bwd(r, w, k2, v, kk, kka, states, final_state, dy, scatter=pieces)
        late_out = (received, pieces)
    dxs, dmu, dw0, da0, dw2p, da2p, dg2p, dk_k, dk_a = rwkv_pre_bwd(
        p, mu, w0, a0, w2p, a2p, g2p, k_k, k_a, (dr, dr_b), dw, (dk2, dk2_b), (dv, dv_b), dkk, dkka, dg)
    dp = shift_bwd(dxs, mu, dq_mem0)
    G.update(a_mu=dmu, a_w0=dw0, a_a0=da0, a_w2=dw2p[None, :64], a_a2=da2p[None, 64:128], a_g2=dg2p[None, 128:],
             a_k_k=dk_k, a_k_a=dk_a, a_r_k=drk.reshape(1, RWKV_HEADS, HEAD_DIM), a_lnx_w=dlw, a_lnx_b=dlb)
    dh0 = _mm(dp, W["a_w_in"][0], tb=True, name="a_in_dx")
    G["a_w_in"] = _mm(h0, dp, ta=True, name="a_in_dw")[None]
    grad_x, dg0 = rms_bwd(x, [W["attn_norm"][0:1]], [dh0], dx1, name="attn_rms_bwd0")
    G["attn_norm"][0] = dg0[0]
    for n in list(G):
        if isinstance(G[n], list):
            G[n] = jnp.stack(G[n], axis=0)
    return loss, grad_x, G, late_out


HBM_SPEC = pl.BlockSpec(memory_space=pltpu.HBM)


def _mesh_pos():
    return lax.axis_index("x"), lax.axis_index("y"), lax.axis_index("c")


def _other_chips(x, y):
    return [(1 - x, y), (x, 1 - y), (1 - x, 1 - y)]


def _remote(send_sems, recv_sems, k, src, dst, to):
    return pltpu.make_async_remote_copy(src_ref=src, dst_ref=dst, send_sem=send_sems.at[k], recv_sem=recv_sems.at[k],
                                        device_id=to, device_id_type=MESH)


def _comm_call(body, name, ins, out_shape, n_remote):
    scratch = [pltpu.SemaphoreType.DMA((n_remote,)), pltpu.SemaphoreType.DMA((n_remote,))]
    return pl.pallas_call(body, name=name, in_specs=[HBM_SPEC] * len(ins), out_specs=[HBM_SPEC] * len(out_shape),
                          out_shape=out_shape, scratch_shapes=scratch)(*ins)


def comm_gather(wbig, wsm):
    def body(wb, ws, ob, os_, send_sems, recv_sems):
        x, y, c = _mesh_pos()
        s = 2 * x + y
        me, sibling = (x, y, c), (x, y, 1 - c)
        chips = _other_chips(x, y)
        rc = functools.partial(_remote, send_sems, recv_sems)
        first = []
        for j, (cx, cy) in enumerate(chips):
            first.append(rc(j, wb.at[c], ob.at[s, c], (cx, cy, c)))
            first.append(rc(6 + j, ws, os_.at[s], (cx, cy, c)))
        for cp in first:
            cp.start()
        passed = []
        for j, (cx, cy) in enumerate(chips):
            blk = ob.at[2 * cx + cy, c]
            rc(j, blk, blk, me).wait_recv()
            passed.append(rc(3 + j, blk, blk, sibling))
            passed[-1].start()
        for j, (cx, cy) in enumerate(chips):
            blk = ob.at[2 * cx + cy, 1 - c]
            rc(3 + j, blk, blk, me).wait_recv()
            sb = os_.at[2 * cx + cy]
            rc(6 + j, sb, sb, me).wait_recv()
        for cp in first + passed:
            cp.wait_send()

    out_shape = [jax.ShapeDtypeStruct((N_CHIPS, *wbig.shape), wbig.dtype),
                 jax.ShapeDtypeStruct((N_CHIPS, *wsm.shape), wsm.dtype)]
    return _comm_call(body, "comm_gather", [wbig, wsm], out_shape, 9)


def comm_pair_exchange(gb, gs):
    def body(gb_ref, gs_ref, rb_ref, rs_ref, send_sems, recv_sems):
        x, y, c = _mesh_pos()
        sibling = (x, y, 1 - c)
        rc = functools.partial(_remote, send_sems, recv_sems)
        cps = [rc(r, gb_ref.at[r, 1 - c], rb_ref.at[r], sibling) for r in range(N_CHIPS)]
        cps.append(rc(N_CHIPS, gs_ref.at[1 - c], rs_ref, sibling))
        for cp in cps:
            cp.start()
        for cp in cps:
            cp.wait()

    out_shape = [jax.ShapeDtypeStruct((N_CHIPS, *gb.shape[2:]), gb.dtype), jax.ShapeDtypeStruct(gs.shape[1:], gs.dtype)]
    return _comm_call(body, "comm_pair_exchange", [gb, gs], out_shape, N_CHIPS + 1)


def comm_chip_exchange(hb, hs):
    def body(hb_ref, hs_ref, qb_ref, qs_ref, send_sems, recv_sems):
        x, y, c = _mesh_pos()
        s = 2 * x + y
        me = (x, y, c)
        chips = _other_chips(x, y)
        rc = functools.partial(_remote, send_sems, recv_sems)
        cps = []
        for j, (cx, cy) in enumerate(chips):
            cps.append(rc(j, hb_ref.at[2 * cx + cy], qb_ref.at[s], (cx, cy, c)))
            cps.append(rc(3 + j, hs_ref, qs_ref.at[s], (cx, cy, c)))
        for cp in cps:
            cp.start()
        for j, (cx, cy) in enumerate(chips):
            blk = qb_ref.at[2 * cx + cy]
            rc(j, blk, blk, me).wait_recv()
            sb = qs_ref.at[2 * cx + cy]
            rc(3 + j, sb, sb, me).wait_recv()
        for cp in cps:
            cp.wait_send()

    out_shape = [jax.ShapeDtypeStruct(hb.shape, hb.dtype), jax.ShapeDtypeStruct((N_CHIPS, *hs.shape), hs.dtype)]
    return _comm_call(body, "comm_chip_exchange", [hb, hs], out_shape, 6)


def comm_pair_share(halves):
    n = len(halves)

    def body(*refs):
        x, y, c = _mesh_pos()
        send_sems, recv_sems = refs[2 * n], refs[2 * n + 1]
        cps = [_remote(send_sems, recv_sems, k, refs[k], refs[n + k], (x, y, 1 - c)) for k in range(n)]
        for cp in cps:
            cp.start()
        for cp in cps:
            cp.wait()

    out_shape = [jax.ShapeDtypeStruct(h.shape, h.dtype) for h in halves]
    return _comm_call(body, "comm_pair_share", list(halves), out_shape, n)


def add_pairs(a, b, out_dtype, *, name, tb):
    T, L = a.shape
    return _rowmap(lambda i, n, p, q: p + q, name=name, T=T, tb=tb, ins=[("row", a), ("row", b)],
                   outs=[("row", L, out_dtype)])[0]


def add_chips(parts, *, name, tb):
    T, L = parts[0].shape

    def fn(i, n, *ps):
        acc = ps[0].astype(F32)
        for p in ps[1:]:
            acc = acc + p.astype(F32)
        return acc

    return _rowmap(fn, name=name, T=T, tb=tb, ins=[("row", p) for p in parts], outs=[("row", L, F32)])[0]


def adamw(g, w, m, v, *, name, tb):
    T, L = g.shape

    def fn(i, n, g, w, m, v):
        m2 = ADAM_B1 * m + (1.0 - ADAM_B1) * g
        v2 = ADAM_B2 * v + (1.0 - ADAM_B2) * (g * g)
        m_hat = m2 / (1.0 - ADAM_B1 ** ADAM_STEP)
        v_hat = v2 / (1.0 - ADAM_B2 ** ADAM_STEP)
        return -ADAM_LR * (m_hat / (jnp.sqrt(v_hat) + ADAM_EPS) + ADAM_WD * w), m2, v2

    return _rowmap(fn, name=name, T=T, tb=tb, ins=[("row", z) for z in (g, w, m, v)], outs=[("row", L, F32)] * 3)


BIG_LANES = 1024
SMALL_LANES = 128


def _flat_cat(arrs, total, dtype):
    parts = [a.reshape(-1).astype(dtype) for a in arrs]
    n = sum(p.shape[0] for p in parts)
    assert n <= total, (n, total)
    if n < total:
        parts.append(jnp.zeros((total - n,), dtype))
    return jnp.concatenate(parts)


def _split_flat(flat, shapes):
    out, off = [], 0
    for shp in shapes:
        n = math.prod(shp)
        out.append(flat[off:off + n].reshape(shp))
        off += n
    return out


def _round_up(n, m):
    return -(-n // m) * m


def _full_shape(shard_shape, axis):
    return tuple(d * N_CHIPS if i == axis else d for i, d in enumerate(shard_shape))


def kernel(x, mem, attn_norm, a_w_in, a_mu, a_w0, a_w2, a_a0, a_a2, a_g2, a_k_k, a_k_a, a_r_k, a_lnx_w, a_lnx_b, a_w_out, kv_norm, kv_w, kv_k_norm, b_w_in, b_q_norm, b_w_out, mem_norm, mem_w_kv, mem_q_norm, mem_k_norm, ffn_norm, ffn_w_up, ffn_conv_w, ffn_conv_b, ffn_w_down, loss_target, m_attn_norm, m_a_w_in, m_a_mu, m_a_w0, m_a_w2, m_a_a0, m_a_a2, m_a_g2, m_a_k_k, m_a_k_a, m_a_r_k, m_a_lnx_w, m_a_lnx_b, m_a_w_out, m_kv_norm, m_kv_w, m_kv_k_norm, m_b_w_in, m_b_q_norm, m_b_w_out, m_mem_norm, m_mem_w_kv, m_mem_q_norm, m_mem_k_norm, m_ffn_norm, m_ffn_w_up, m_ffn_conv_w, m_ffn_conv_b, m_ffn_w_down, v_attn_norm, v_a_w_in, v_a_mu, v_a_w0, v_a_w2, v_a_a0, v_a_a2, v_a_g2, v_a_k_k, v_a_k_a, v_a_r_k, v_a_lnx_w, v_a_lnx_b, v_a_w_out, v_kv_norm, v_kv_w, v_kv_k_norm, v_b_w_in, v_b_q_norm, v_b_w_out, v_mem_norm, v_mem_w_kv, v_mem_q_norm, v_mem_k_norm, v_ffn_norm, v_ffn_w_up, v_ffn_conv_w, v_ffn_conv_b, v_ffn_w_down):
    args = (attn_norm, a_w_in, a_mu, a_w0, a_w2, a_a0, a_a2, a_g2, a_k_k, a_k_a, a_r_k, a_lnx_w, a_lnx_b, a_w_out, kv_norm, kv_w, kv_k_norm, b_w_in, b_q_norm, b_w_out, mem_norm, mem_w_kv, mem_q_norm, mem_k_norm, ffn_norm, ffn_w_up, ffn_conv_w, ffn_conv_b, ffn_w_down)
    ms = (m_attn_norm, m_a_w_in, m_a_mu, m_a_w0, m_a_w2, m_a_a0, m_a_a2, m_a_g2, m_a_k_k, m_a_k_a, m_a_r_k, m_a_lnx_w, m_a_lnx_b, m_a_w_out, m_kv_norm, m_kv_w, m_kv_k_norm, m_b_w_in, m_b_q_norm, m_b_w_out, m_mem_norm, m_mem_w_kv, m_mem_q_norm, m_mem_k_norm, m_ffn_norm, m_ffn_w_up, m_ffn_conv_w, m_ffn_conv_b, m_ffn_w_down)
    vs = (v_attn_norm, v_a_w_in, v_a_mu, v_a_w0, v_a_w2, v_a_a0, v_a_a2, v_a_g2, v_a_k_k, v_a_k_a, v_a_r_k, v_a_lnx_w, v_a_lnx_b, v_a_w_out, v_kv_norm, v_kv_w, v_kv_k_norm, v_b_w_in, v_b_q_norm, v_b_w_out, v_mem_norm, v_mem_w_kv, v_mem_q_norm, v_mem_k_norm, v_ffn_norm, v_ffn_w_up, v_ffn_conv_w, v_ffn_conv_b, v_ffn_w_down)
    w_sh, m_sh, v_sh = (dict(zip(WEIGHTS, z)) for z in (args, ms, vs))
    xi, yi, ci = _mesh_pos()
    chip = 2 * xi + yi
    axes = {**dict(BIG), **dict(SMALL_SHARDED)}
    early_names = [n for n, _ in BIG if n in EARLY_BIG]
    late_names = [n for n, _ in BIG if n not in EARLY_BIG and n != NATURAL_BIG]
    ss_names, ss_axes = [n for n, _ in SMALL_SHARDED], dict(SMALL_SHARDED)
    shapes_of = lambda names: [w_sh[n].shape for n in names]
    count = lambda names: sum(math.prod(s) for s in shapes_of(names))
    n_early, n_late = count(early_names), count(late_names)
    assert n_early % (2 * 16 * BIG_LANES) == 0 and n_late % (2 * 16 * BIG_LANES) == 0
    mh, mh_late = n_early // (2 * BIG_LANES), n_late // (2 * BIG_LANES)
    n_ss = _round_up(count(ss_names), 8 * SMALL_LANES)

    def shard_pack(names, total, dtype, source):
        return _flat_cat([source[n] for n in names], total, dtype)

    def join(pieces, n):
        if pieces[0].ndim == 3:
            return [jnp.concatenate([p[l] for p in pieces], axis=axes[n] - 1) for l in range(pieces[0].shape[0])]
        return jnp.concatenate(pieces, axis=axes[n])

    def unshard(names, gathered):
        per_chip = [_split_flat(gathered[j], shapes_of(names)) for j in range(N_CHIPS)]
        return {n: join([per_chip[j][k] for j in range(N_CHIPS)], n) for k, n in enumerate(names)}

    def whole(g):
        return jnp.stack(g, axis=0) if isinstance(g, list) else g

    def by_chip(names, total, dtype, grads):
        parts = [jnp.split(whole(grads[n]), N_CHIPS, axis=axes[n]) for n in names]
        return jnp.stack([_flat_cat([p[j] for p in parts], total, dtype) for j in range(N_CHIPS)])

    wbig = shard_pack(early_names, n_early, MMD, w_sh).reshape(2, mh, BIG_LANES)
    wsm = shard_pack(ss_names, n_ss, F32, w_sh).reshape(-1, SMALL_LANES)
    wbig_all, wsm_all = comm_gather(wbig, wsm)
    wbig_all = lax.dynamic_update_index_in_dim(wbig_all, wbig, chip, 0).reshape(N_CHIPS, -1)
    wsm_all = lax.dynamic_update_index_in_dim(wsm_all, wsm, chip, 0).reshape(N_CHIPS, -1)
    W = {n: w_sh[n] for n in SMALL_REPL}
    W.update(unshard(early_names, wbig_all))
    W.update(unshard(ss_names, wsm_all))
    for n in ("a_w2", "a_a2", "a_g2"):
        W[n] = [z.astype(MMD) for z in W[n]]
    wlate = shard_pack(late_names, n_late, MMD, w_sh).reshape(2, mh_late, BIG_LANES)
    nat_axis = axes[NATURAL_BIG]
    wnat = w_sh[NATURAL_BIG].astype(MMD)
    assert wnat.shape[0] == 2 and nat_axis != 0

    def unpack_late(gathered):
        full = lax.dynamic_update_index_in_dim(gathered[0], wlate, chip, 0)
        out = unshard(late_names, full.reshape(N_CHIPS, -1))
        nat = lax.dynamic_update_index_in_dim(gathered[1], wnat, chip, 0)
        out[NATURAL_BIG] = join([nat[j] for j in range(N_CHIPS)], NATURAL_BIG)
        return out

    def pack_late(grads):
        return [by_chip(late_names, n_late, BF16, grads).reshape(N_CHIPS, 2, mh_late, BIG_LANES),
                jnp.stack(jnp.split(whole(grads[NATURAL_BIG]).astype(BF16), N_CHIPS, axis=nat_axis))]

    loss_blk, grad_x, G, (received, pieces) = local_step(x[0], mem[0], loss_target[0], W,
                                                          late=([wlate, wnat], unpack_late, pack_late))
    loss = lax.psum(loss_blk[0, 0], ("x", "y", "c"))

    own_piece = lambda p: lax.dynamic_index_in_dim(lax.dynamic_index_in_dim(p, chip, 0, keepdims=False), ci, 0,
                                                   keepdims=False)
    gh_late = add_chips([received[0][k] for k in range(len(PEER_FLIPS))] + [own_piece(pieces[0])],
                        name="add_pieces_late", tb=32)
    gh_nat = add_chips([received[1][k] for k in range(len(PEER_FLIPS))] + [own_piece(pieces[1])],
                       name="add_pieces_natural", tb=32)
    gbig = by_chip(early_names, n_early, F32, G).reshape(N_CHIPS, 2, mh, BIG_LANES)
    sm_full_names = ss_names + list(SMALL_REPL)
    sm_full_shapes = [_full_shape(w_sh[n].shape, ss_axes[n]) for n in ss_names] + [w_sh[n].shape for n in SMALL_REPL]
    n_smf = _round_up(sum(math.prod(s) for s in sm_full_shapes), 2 * 8 * SMALL_LANES)
    msh = n_smf // (2 * SMALL_LANES)
    gsm = _flat_cat([G[n] for n in sm_full_names], n_smf, F32).reshape(2, msh, SMALL_LANES)
    rb, rs = comm_pair_exchange(gbig, gsm)
    mine_b = lax.dynamic_index_in_dim(gbig, ci, axis=1, keepdims=False)
    mine_s = lax.dynamic_index_in_dim(gsm, ci, axis=0, keepdims=False)
    hb = add_pairs(mine_b.reshape(-1, BIG_LANES), rb.reshape(-1, BIG_LANES), BF16, name="add_pairs_big", tb=128)
    hs = add_pairs(mine_s, rs, F32, name="add_pairs_small", tb=msh)
    hb = hb.reshape(N_CHIPS, mh, BIG_LANES)
    qb, qs = comm_chip_exchange(hb, hs)
    qb = lax.dynamic_update_index_in_dim(qb, lax.dynamic_index_in_dim(hb, chip, 0, keepdims=False), chip, 0)
    qs = lax.dynamic_update_index_in_dim(qs, hs, chip, 0)
    gh = add_chips([qb[j] for j in range(N_CHIPS)], name="add_chips_big", tb=32)
    gsh = add_chips([qs[j] for j in range(N_CHIPS)], name="add_chips_small", tb=msh)
    rh, rh_late, rh_nat, rsh = comm_pair_share([gh, gh_late, gh_nat, gsh])
    both = lambda mine_, theirs: jnp.where(ci == 0, jnp.stack([mine_, theirs]), jnp.stack([theirs, mine_]))
    gfull, gfull_late, gfull_nat, gsfull = both(gh, rh), both(gh_late, rh_late), both(gh_nat, rh_nat), both(gsh, rsh)

    res = {tag: {} for tag in ("grad", "delta", "new_m", "new_v")}
    big_grads = (list(zip(early_names, _split_flat(gfull.reshape(-1), shapes_of(early_names))))
                 + list(zip(late_names, _split_flat(gfull_late.reshape(-1), shapes_of(late_names))))
                 + [(NATURAL_BIG, gfull_nat)])
    for n, g in big_grads:
        shp = w_sh[n].shape
        rows = lambda z: z.reshape(-1, shp[-1])
        nrow = math.prod(shp[:-1])
        tb = next(t for t in (512, 256, 128, 64) if nrow % t == 0 and t * shp[-1] <= (1 << 19))
        outs = adamw(rows(g), rows(w_sh[n]), rows(m_sh[n]), rows(v_sh[n]), name=f"adamw_{n}", tb=tb)
        res["grad"][n] = g
        for tag, o in zip(("delta", "new_m", "new_v"), outs):
            res[tag][n] = o.reshape(shp)
    sm_full = dict(zip(sm_full_names, _split_flat(gsfull.reshape(-1), sm_full_shapes)))
    g_loc = {}
    for n in ss_names:
        size = w_sh[n].shape[ss_axes[n]]
        g_loc[n] = lax.dynamic_slice_in_dim(sm_full[n], chip * size, size, axis=ss_axes[n])
    for n in SMALL_REPL:
        g_loc[n] = sm_full[n]
    n_sml = _round_up(sum(math.prod(w_sh[n].shape) for n in sm_full_names), 8 * SMALL_LANES)
    pack_sm = lambda d: _flat_cat([d[n] for n in sm_full_names], n_sml, F32).reshape(-1, SMALL_LANES)
    d_sm, m_sm, v_sm = adamw(pack_sm(g_loc), pack_sm(w_sh), pack_sm(m_sh), pack_sm(v_sh), name="adamw_small",
                             tb=n_sml // SMALL_LANES)
    sm_loc_shapes = [w_sh[n].shape for n in sm_full_names]
    res["grad"].update(g_loc)
    for tag, smv in (("delta", d_sm), ("new_m", m_sm), ("new_v", v_sm)):
        res[tag].update(dict(zip(sm_full_names, _split_flat(smv.reshape(-1), sm_loc_shapes))))
    return (loss, grad_x[None], *[res[tag][n] for tag in ("grad", "delta", "new_m", "new_v") for n in WEIGHTS])
```

```python
import functools
import math

import numpy as np
import jax
import jax.numpy as jnp
from jax import lax
from jax.experimental import pallas as pl
from jax.experimental.pallas import tpu as pltpu

F32 = jnp.float32
BF16 = jnp.bfloat16
MMD = jnp.bfloat16

D_MODEL = 1024
HEAD_DIM = 64
N_MEM = 256
MEM_WIDTH = 256
RWKV_HEADS = 12
RWKV_WIDTH = 768
SHIFT_WIDTH = 2560
LORA_WIDTH = 256
DIL_WIDTH = 768
DIL_GROUPS = ((128, 1), (512, 4), (2048, 16))
DIL_BLOCK = 128
D_FF = 2816
ROPE_THETA = 10000.0
RMS_EPS = 1e-6
LNX_EPS = 64e-5
NEG_INF = -1e30
ADAM_LR = 0.001
ADAM_B1 = 0.9
ADAM_B2 = 0.999
ADAM_EPS = 1e-08
ADAM_WD = 0.01
ADAM_STEP = 10
N_CHIPS = 4
MESH = pl.DeviceIdType.MESH
VMEM_LIMIT_MB = 56
SCAN_CHUNK = 64
SCAN_UNROLL = 32
SCAN_UNROLL_BWD = 16

BIG = (("a_w_in", 2), ("a_w_out", 1), ("kv_w", 1), ("b_w_in", 1), ("b_w_out", 2), ("mem_w_kv", 1),
       ("ffn_w_up", 2), ("ffn_w_down", 1))
EARLY_BIG = ("a_w_in",)
NATURAL_BIG = "ffn_w_up"
SMALL_SHARDED = (("a_mu", 1), ("a_w0", 1), ("a_w2", 2), ("a_a0", 1), ("a_a2", 2), ("a_g2", 2), ("a_k_k", 1),
                 ("a_k_a", 1), ("a_lnx_w", 1), ("a_lnx_b", 1), ("ffn_conv_w", 2))
SMALL_REPL = ("attn_norm", "a_r_k", "kv_norm", "kv_k_norm", "b_q_norm", "mem_norm", "mem_q_norm", "mem_k_norm",
              "ffn_norm", "ffn_conv_b")
WEIGHTS = ("attn_norm", "a_w_in", "a_mu", "a_w0", "a_w2", "a_a0", "a_a2", "a_g2", "a_k_k", "a_k_a", "a_r_k",
           "a_lnx_w", "a_lnx_b", "a_w_out", "kv_norm", "kv_w", "kv_k_norm", "b_w_in", "b_q_norm", "b_w_out",
           "mem_norm", "mem_w_kv", "mem_q_norm", "mem_k_norm", "ffn_norm", "ffn_w_up", "ffn_conv_w", "ffn_conv_b",
           "ffn_w_down")


def _cp(sem=None, **kw):
    return pltpu.CompilerParams(dimension_semantics=sem, vmem_limit_bytes=VMEM_LIMIT_MB << 20, **kw)


def _tile(n, cands=(512, 256, 128)):
    for c in cands:
        if n % c == 0:
            return c
    return n


def _mm(a, b, *, name, ta=False, tb=False, add=None, out_dtype=F32):
    K, M = a.shape if ta else a.shape[::-1]
    N = b.shape[0] if tb else b.shape[1]
    assert K == (b.shape[1] if tb else b.shape[0])
    tm, tn = _tile(M, (512, 256, 128) if ta else (1024, 512, 256, 128)), _tile(N, (512, 1408, 256, 128))
    bytes_of = lambda z: z.size * z.dtype.itemsize
    kept = 1 if bytes_of(b) + bytes_of(a) * (N // tn) < bytes_of(a) + bytes_of(b) * (M // tm) else 0
    mi, ni = ((lambda o, i: i), (lambda o, i: o)) if kept else ((lambda o, i: o), (lambda o, i: i))
    grid = (N // tn, M // tm) if kept else (M // tm, N // tn)
    a_blk, b_blk = ((K, tm) if ta else (tm, K)), ((tn, K) if tb else (K, tn))
    a_spec = pl.BlockSpec(a_blk, (lambda o, i: (0, mi(o, i))) if ta else (lambda o, i: (mi(o, i), 0)))
    b_spec = pl.BlockSpec(b_blk, (lambda o, i: (ni(o, i), 0)) if tb else (lambda o, i: (0, ni(o, i))))
    o_spec = pl.BlockSpec((tm, tn), lambda o, i: (mi(o, i), ni(o, i)))
    dn = (((0,) if ta else (1,), (1,) if tb else (0,)), ((), ()))
    has_add = add is not None
    cache = (b if kept else a).dtype != MMD

    def body(*refs):
        vals = [refs[0], refs[1]]
        o_ref = refs[2 + has_add]
        if cache:
            scr = refs[-1]

            @pl.when(pl.program_id(1) == 0)
            def _():
                scr[...] = vals[kept][...].astype(MMD)

            vals[kept] = scr
        acc = lax.dot_general(vals[0][...].astype(MMD), vals[1][...].astype(MMD), dn, preferred_element_type=F32)
        if has_add:
            acc = acc + refs[2][...]
        o_ref[...] = acc.astype(o_ref.dtype)

    ins = [a, b] + ([add] if has_add else [])
    specs = [a_spec, b_spec] + ([o_spec] if has_add else [])
    return pl.pallas_call(
        body, name=name, grid=grid, in_specs=specs, out_specs=o_spec,
        out_shape=jax.ShapeDtypeStruct((M, N), out_dtype),
        scratch_shapes=[pltpu.VMEM(b_blk if kept else a_blk, MMD)] if cache else [],
        compiler_params=_cp(("parallel", "arbitrary")),
    )(*ins)


def _rowmap(fn, *, name, T, tb, ins, outs, accs=()):
    nblk = T // tb
    assert T % tb == 0 and tb % 8 == 0
    in_specs, args = [], []
    for spec in ins:
        kind, arr = spec[0], spec[1]
        w, cb = (spec[2], spec[3]) if len(spec) > 2 else (arr.shape[-1], 0)
        if kind == "row":
            in_specs.append(pl.BlockSpec((tb, w), lambda i, cb=cb: (i, cb)))
        elif kind == "prev":
            in_specs.append(pl.BlockSpec((8, w), lambda i, cb=cb: (jnp.maximum(i * (tb // 8) - 1, 0), cb)))
        elif kind == "next":
            in_specs.append(pl.BlockSpec((8, w), lambda i, cb=cb: (jnp.minimum((i + 1) * (tb // 8), T // 8 - 1), cb)))
        elif kind == "const":
            in_specs.append(pl.BlockSpec(arr.shape, lambda i, nd=arr.ndim: (0,) * nd))
        else:
            raise ValueError(kind)
        args.append(arr)
    out_shape, out_specs = [], []
    for kind, w, dt in outs:
        out_shape.append(jax.ShapeDtypeStruct((T, w), dt))
        out_specs.append(pl.BlockSpec((tb, w), lambda i: (i, 0)))
    for shp, dt in accs:
        out_shape.append(jax.ShapeDtypeStruct(shp, dt))
        out_specs.append(pl.BlockSpec(shp, lambda i, nd=len(shp): (0,) * nd))
    n_in, n_out = len(ins), len(outs)

    def body(*refs):
        i = pl.program_id(0)
        vals = [r[...] for r in refs[:n_in]]
        res = fn(i, nblk, *vals)
        if not isinstance(res, (tuple, list)):
            res = (res,)
        assert len(res) == n_out + len(accs), (name, len(res))
        for r, v in zip(refs[n_in:n_in + n_out], res[:n_out]):
            r[...] = v.astype(r.dtype)
        acc_refs = refs[n_in + n_out:]
        if acc_refs:
            @pl.when(i == 0)
            def _():
                for r in acc_refs:
                    r[...] = jnp.zeros(r.shape, r.dtype)

            for r, v in zip(acc_refs, res[n_out:]):
                r[...] += v

    res = pl.pallas_call(
        body, name=name, grid=(nblk,), in_specs=in_specs, out_specs=out_specs, out_shape=out_shape,
        compiler_params=_cp(("arbitrary",)),
    )(*args)
    return res


def _row_pick(halo, r):
    rid = lax.broadcasted_iota(jnp.int32, halo.shape, 0)
    return jnp.sum(jnp.where(rid == r, halo, 0.0), axis=0, keepdims=True)


def _shift_down(x, row_before, is_first):
    rid = lax.broadcasted_iota(jnp.int32, x.shape, 0)
    first = jnp.where(is_first, 0.0, 1.0) * row_before
    return jnp.where(rid == 0, first, pltpu.roll(x, 1, axis=0))


def _shift_up(x, row_after, is_last):
    n = x.shape[0]
    rid = lax.broadcasted_iota(jnp.int32, x.shape, 0)
    last = jnp.where(is_last, 0.0, 1.0) * row_after
    return jnp.where(rid == n - 1, last, pltpu.roll(x, n - 1, axis=0))


def _dot(a, b, dn=(((1,), (0,)), ((), ()))):
    return lax.dot_general(a.astype(MMD), b.astype(MMD), dn, preferred_element_type=F32)


def _dot_nt(a, b):
    return _dot(a, b, (((1,), (1,)), ((), ())))


def _dot_tn(a, b):
    return _dot(a, b, (((0,), (0,)), ((), ())))


def _dot_exact01(x, g01):
    hi = x.astype(BF16)
    lo = (x - hi.astype(F32)).astype(BF16)
    dn = (((1,), (0,)), ((), ()))
    return (lax.dot_general(hi, g01, dn, preferred_element_type=F32)
            + lax.dot_general(lo, g01, dn, preferred_element_type=F32))


def _fold_heads(v, fold):
    return _row_pick(_dot_exact01(jnp.broadcast_to(v, (8, v.shape[1])), fold), 0)


def _fold_ones(width):
    idx = np.arange(width) % HEAD_DIM
    return jnp.asarray((idx[:, None] == np.arange(HEAD_DIM)[None, :]).astype(np.float32), BF16)


def _head_masks(width):
    idx = np.arange(width) // HEAD_DIM
    return jnp.asarray((idx[None, :] == np.arange(width // HEAD_DIM)[:, None]).astype(np.float32)[:, None, :], F32)


def _rms_stats(x):
    r = lax.rsqrt(jnp.mean(x * x, axis=-1, keepdims=True) + RMS_EPS)
    return r, x * r


def rms_fwd(x, gains, *, name):
    T, D = x.shape

    def fn(i, nblk, xb, *gs):
        _, xh = _rms_stats(xb)
        return tuple(xh * g for g in gs)

    return _rowmap(fn, name=name, T=T, tb=512, ins=[("row", x)] + [("const", g) for g in gains],
                   outs=[("row", D, MMD)] * len(gains))


def rms_bwd(x, gains, dhs, dres, *, name):
    T, D = x.shape
    n = len(gains)

    def fn(i, nblk, xb, dr, *rest):
        gs, ds = rest[:n], rest[n:]
        r, xh = _rms_stats(xb)
        dx = dr
        dgs = []
        for g, dh in zip(gs, ds):
            dgs.append(jnp.sum(dh * xh, axis=0, keepdims=True))
            dxh = dh * g
            dx = dx + r * (dxh - xh * jnp.mean(dxh * xh, axis=-1, keepdims=True))
        return (dx, *dgs)

    return _rowmap(fn, name=name, T=T, tb=512,
                   ins=[("row", x), ("row", dres)] + [("const", g) for g in gains] + [("row", d) for d in dhs],
                   outs=[("row", D, F32)], accs=[((1, D), F32)] * n)


def _segsum(x):
    first = lax.broadcasted_iota(jnp.int32, (x.shape[0], 128), 1) < HEAD_DIM
    outs = []
    for p in range(x.shape[1] // 128):
        xs = x[:, p * 128:(p + 1) * 128]
        lo = jnp.sum(jnp.where(first, xs, 0.0), axis=-1, keepdims=True)
        hi = jnp.sum(jnp.where(first, 0.0, xs), axis=-1, keepdims=True)
        outs.append(jnp.where(first, lo, hi))
    return jnp.concatenate(outs, axis=1)


def _pre1_common(i, ps, halo, mu, w0, a0, w2p, a2p, g2p, k_k, k_a):
    prev = _shift_down(ps, _row_pick(halo, 7), i == 0)
    xs = ps + (prev - ps) * mu
    lo = xs[:, 3 * RWKV_WIDTH:]
    tl, sl = jnp.tanh(lo), jax.nn.sigmoid(lo)
    dec = w0 + _dot(tl, w2p)
    ain = a0 + _dot(lo, a2p)
    g = _dot(sl, g2p)
    wl = -jax.nn.softplus(-dec) - 0.5
    w = jnp.exp(-jnp.exp(wl))
    a = jax.nn.sigmoid(ain)
    k = xs[:, RWKV_WIDTH:2 * RWKV_WIDTH]
    z = k * k_k
    nrm = jnp.sqrt(_segsum(z * z))
    kk = z / jnp.maximum(nrm, 1e-12)
    return prev, xs, lo, tl, sl, dec, wl, w, a, g, k, nrm, kk


def rwkv_pre_fwd(p, mu, w0, a0, w2p, a2p, g2p, k_k, k_a):
    T = p.shape[0]

    def fn(i, nblk, ps, halo, mu, w0, a0, w2p, a2p, g2p, k_k, k_a):
        _, xs, _, _, _, _, _, w, a, g, k, _, kk = _pre1_common(i, ps, halo, mu, w0, a0, w2p, a2p, g2p, k_k, k_a)
        W = RWKV_WIDTH
        return xs[:, :W], w, k * (1.0 + (a - 1.0) * k_a), xs[:, 2 * W:3 * W], kk, kk * a, g

    return _rowmap(fn, name="rwkv_pre_fwd", T=T, tb=256,
                   ins=[("row", p, SHIFT_WIDTH, 0), ("prev", p, SHIFT_WIDTH, 0)]
                   + [("const", c) for c in (mu, w0, a0, w2p, a2p, g2p, k_k, k_a)],
                   outs=[("row", RWKV_WIDTH, F32)] * 7)


def rwkv_pre_bwd(p, mu, w0, a0, w2p, a2p, g2p, k_k, k_a, drs, dw, dk2s, dvs, dkk, dkka, dg):
    T = p.shape[0]

    def fn(i, nblk, ps, halo, mu, w0, a0, w2p, a2p, g2p, k_k, k_a, dr0, dr1, dw, dk20, dk21, dv0, dv1, dkk, dkka, dg):
        prev, xs, lo, tl, sl, dec, wl, w, a, g, k, nrm, kk = _pre1_common(i, ps, halo, mu, w0, a0, w2p, a2p, g2p, k_k, k_a)
        dk2 = dk20 + dk21
        dkk_t = dkk + dkka * a
        proj = jnp.where(nrm > 1e-12, kk * _segsum(dkk_t * kk), 0.0)
        dz = (dkk_t - proj) / jnp.maximum(nrm, 1e-12)
        dk = dz * k_k + dk2 * (1.0 + (a - 1.0) * k_a)
        da = dkka * kk + dk2 * k * k_a
        ddec = dw * (-w * jnp.exp(wl)) * jax.nn.sigmoid(-dec)
        dain = da * a * (1.0 - a)
        dlo = (_dot_nt(ddec, w2p) * (1.0 - tl * tl) + _dot_nt(dain, a2p) + _dot_nt(dg, g2p) * sl * (1.0 - sl))
        dxs = jnp.concatenate([dr0 + dr1, dk, dv0 + dv1, dlo], axis=1)
        s = lambda z: jnp.sum(z, axis=0, keepdims=True)
        return (dxs, s(dxs * (prev - ps)), s(ddec), s(dain), _dot_tn(tl, ddec), _dot_tn(lo, dain), _dot_tn(sl, dg),
                s(dz * k), s(dk2 * k * (a - 1.0)))

    return _rowmap(fn, name="rwkv_pre_bwd", T=T, tb=128,
                   ins=[("row", p, SHIFT_WIDTH, 0), ("prev", p, SHIFT_WIDTH, 0)]
                   + [("const", c) for c in (mu, w0, a0, w2p, a2p, g2p, k_k, k_a)]
                   + [("row", c) for c in (*drs, dw, *dk2s, *dvs, dkk, dkka, dg)],
                   outs=[("row", SHIFT_WIDTH, F32)],
                   accs=[((1, SHIFT_WIDTH), F32), ((1, RWKV_WIDTH), F32), ((1, RWKV_WIDTH), F32)]
                   + [((LORA_WIDTH, RWKV_WIDTH), F32)] * 3 + [((1, RWKV_WIDTH), F32)] * 2)


def shift_bwd(dxs, mu, dq_mem):
    T = dxs.shape[0]

    def fn(i, nblk, d, halo, mu, dq):
        nxt = _shift_up(d, _row_pick(halo, 0), i == nblk - 1)
        return jnp.concatenate([d * (1.0 - mu) + nxt * mu, dq], axis=1)

    return _rowmap(fn, name="shift_bwd", T=T, tb=256,
                   ins=[("row", dxs), ("next", dxs), ("const", mu), ("row", dq_mem)],
                   outs=[("row", SHIFT_WIDTH + MEM_WIDTH, MMD)])[0]


N_PAIRS = RWKV_HEADS // 2


def _pair_consts():
    row = lax.broadcasted_iota(jnp.int32, (HEAD_DIM, 128), 0)
    lane = lax.broadcasted_iota(jnp.int32, (HEAD_DIM, 128), 1)
    eye2 = jnp.logical_or(lane == row, lane == row + HEAD_DIM).astype(F32)
    li = lax.broadcasted_iota(jnp.int32, (128, 128), 0) < HEAD_DIM
    lj = lax.broadcasted_iota(jnp.int32, (128, 128), 1) < HEAD_DIM
    return eye2, (li == lj).astype(BF16)


def _pair_sum(p, ones2):
    n, m, l = p.shape
    s = lax.dot_general(p.reshape(n * m, l).astype(BF16), ones2, (((1,), (0,)), ((), ())), preferred_element_type=F32)
    return s.reshape(n, m, l)


def _pair_rows(row):
    return jnp.stack([row[:, p * 128:(p + 1) * 128] for p in range(N_PAIRS)], axis=0)


def _pair_flat(rows):
    return jnp.concatenate([rows[p] for p in range(N_PAIRS)], axis=1)


def _split_bf16(v):
    hi = v.astype(BF16).astype(F32)
    return hi, v - hi


def _gather_copies(srcs, dsts, send_sems, recv_sems):
    x, y, c = _mesh_pos()
    s = 2 * x + y
    me, sibling = (x, y, c), (x, y, 1 - c)
    rc = functools.partial(_remote, send_sems, recv_sems)
    ici, land, fwd, arrived = [], [], [], []
    for b, (src, dst) in enumerate(zip(srcs, dsts)):
        for j, (cx, cy) in enumerate(_other_chips(x, y)):
            k = 6 * b + j
            ici.append(rc(k, src.at[c], dst.at[s, c], (cx, cy, c)))
            blk, blk2 = dst.at[2 * cx + cy, c], dst.at[2 * cx + cy, 1 - c]
            land.append(rc(k, blk, blk, me))
            fwd.append(rc(k + 3, blk, blk, sibling))
            arrived.append(rc(k + 3, blk2, blk2, me))
    return ici, land, fwd, arrived


def scan_fwd(r, w, k2, v, kk, kka, gather=None):
    T, W = r.shape
    tc = SCAN_CHUNK
    nchunk = T // tc
    seq = pl.BlockSpec((tc, W), lambda i: (i, 0))
    one_state = pl.BlockSpec((N_PAIRS, HEAD_DIM, 128), lambda i: (0, 0, 0))
    nb = 0 if gather is None else len(gather)

    def body(r_ref, w_ref, k2_ref, v_ref, kk_ref, kka_ref, *rest):
        if gather is None:
            y_ref, st_ref, fin_ref, s_scr, vhi_scr, vlo_scr = rest
        else:
            srcs, (y_ref, st_ref, fin_ref), dsts = rest[:nb], rest[nb:nb + 3], rest[nb + 3:2 * nb + 3]
            s_scr, vhi_scr, vlo_scr, send_sems, recv_sems = rest[2 * nb + 3:]
            ici, land, fwd, arrived = _gather_copies(srcs, dsts, send_sems, recv_sems)

            @pl.when(pl.program_id(0) == 0)
            def _():
                for cp in ici:
                    cp.start()

            @pl.when(pl.program_id(0) == nchunk // 2)
            def _():
                for a, f in zip(land, fwd):
                    a.wait_recv()
                    f.start()

        @pl.when(pl.program_id(0) == 0)
        def _():
            s_scr[...] = jnp.zeros(s_scr.shape, F32)

        vhi_scr[...], vlo_scr[...] = _split_bf16(v_ref[...])
        eye2, ones2 = _pair_consts()
        eye2b = eye2.astype(BF16)

        def step(t, carry):
            r_t, w_t, k2_t, kk_t, kka_t, vhi_t, vlo_t = (
                _pair_rows(ref[pl.ds(t, 1), :]) for ref in (r_ref, w_ref, k2_ref, kk_ref, kka_ref, vhi_scr, vlo_scr))
            S = s_scr[...]
            skk = _pair_sum(S * kk_t, ones2)
            vb = _pair_sum(eye2b * vhi_t.astype(BF16), ones2) + _pair_sum(eye2b * vlo_t.astype(BF16), ones2)
            S2 = S * w_t - skk * kka_t + vb * k2_t
            y_ref[pl.ds(t, 1), :] = _pair_flat(jnp.sum(eye2 * _pair_sum(S2 * r_t, ones2), axis=1, keepdims=True))
            s_scr[...] = S2
            st_ref[t] = S
            return carry

        lax.fori_loop(0, tc, step, 0, unroll=SCAN_UNROLL)
        fin_ref[...] = s_scr[...]

        if gather is not None:
            @pl.when(pl.program_id(0) == nchunk - 1)
            def _():
                for a in arrived:
                    a.wait_recv()
                for cp in ici + fwd:
                    cp.wait_send()

    in_specs = [seq] * 6
    out_specs = [seq, pl.BlockSpec((tc, N_PAIRS, HEAD_DIM, 128), lambda i: (i, 0, 0, 0)), one_state]
    out_shape = [jax.ShapeDtypeStruct((T, W), F32), jax.ShapeDtypeStruct((T, N_PAIRS, HEAD_DIM, 128), F32),
                 jax.ShapeDtypeStruct((N_PAIRS, HEAD_DIM, 128), F32)]
    scratch = [pltpu.VMEM((N_PAIRS, HEAD_DIM, 128), F32), pltpu.VMEM((tc, W), F32), pltpu.VMEM((tc, W), F32)]
    args = [r, w, k2, v, kk, kka]
    if gather is not None:
        in_specs += [HBM_SPEC] * nb
        out_specs += [HBM_SPEC] * nb
        out_shape += [jax.ShapeDtypeStruct((N_CHIPS, *g.shape), g.dtype) for g in gather]
        scratch += [pltpu.SemaphoreType.DMA((6 * nb,)), pltpu.SemaphoreType.DMA((6 * nb,))]
        args += list(gather)
    return pl.pallas_call(
        body, name="rwkv_scan_fwd", grid=(nchunk,), in_specs=in_specs, out_specs=out_specs, out_shape=out_shape,
        scratch_shapes=scratch, compiler_params=_cp(("arbitrary",)),
    )(*args)


PEER_FLIPS = tuple((fx, fy, fc) for fx in (0, 1) for fy in (0, 1) for fc in (0, 1))[1:]


def scan_bwd(r, w, k2, v, kk, kka, states, final_state, dy, scatter=None):
    T, W = r.shape
    tc = SCAN_CHUNK
    nchunk = T // tc
    seq = pl.BlockSpec((tc, W), lambda i: (nchunk - 1 - i, 0))
    st_spec = pl.BlockSpec((tc, N_PAIRS, HEAD_DIM, 128), lambda i: (nchunk - 1 - i, 0, 0, 0))
    one_state = pl.BlockSpec((N_PAIRS, HEAD_DIM, 128), lambda i: (0, 0, 0))
    nb, npeer = (0 if scatter is None else len(scatter)), len(PEER_FLIPS)

    def body(r_ref, w_ref, k2_ref, v_ref, kk_ref, kka_ref, st_ref, fin_ref, dy_ref, *rest):
        if scatter is None:
            dr_ref, dw_ref, dk2_ref, dv_ref, dkk_ref, dkka_ref, ds_scr, sc_scr, vhi_scr, vlo_scr = rest
        else:
            srcs, dsts = rest[:nb], rest[nb + 6:2 * nb + 6]
            dr_ref, dw_ref, dk2_ref, dv_ref, dkk_ref, dkka_ref = rest[nb:nb + 6]
            ds_scr, sc_scr, vhi_scr, vlo_scr, send_sems, recv_sems = rest[2 * nb + 6:]
            x, y, c = _mesh_pos()
            copies = []
            for b, (src, dst) in enumerate(zip(srcs, dsts)):
                for k, (fx, fy, fc) in enumerate(PEER_FLIPS):
                    px, py, pc = (1 - x if fx else x), (1 - y if fy else y), (1 - c if fc else c)
                    copies.append(_remote(send_sems, recv_sems, npeer * b + k, src.at[2 * px + py, pc], dst.at[k],
                                          (px, py, pc)))

            @pl.when(pl.program_id(0) == 0)
            def _():
                for cp in copies:
                    cp.start()

        @pl.when(pl.program_id(0) == 0)
        def _():
            ds_scr[...] = jnp.zeros(ds_scr.shape, F32)
            sc_scr[...] = fin_ref[...]

        vhi_scr[...], vlo_scr[...] = _split_bf16(v_ref[...])
        eye2, ones2 = _pair_consts()
        eye2b = eye2.astype(BF16)
        colsum = lambda z: jnp.sum(z, axis=1, keepdims=True)

        def step(j, carry):
            t = tc - 1 - j
            r_t, w_t, k2_t, kk_t, kka_t, vhi_t, vlo_t, dy_t = (
                _pair_rows(ref[pl.ds(t, 1), :])
                for ref in (r_ref, w_ref, k2_ref, kk_ref, kka_ref, vhi_scr, vlo_scr, dy_ref))
            s_prev, s_cur = st_ref[t], sc_scr[...]
            dyb = _pair_sum(eye2b * dy_t.astype(BF16), ones2)
            vb = _pair_sum(eye2b * vhi_t.astype(BF16), ones2) + _pair_sum(eye2b * vlo_t.astype(BF16), ones2)
            skk = _pair_sum(s_prev * kk_t, ones2)
            dS = ds_scr[...] + dyb * r_t
            dS_b = dS.astype(BF16)
            dsa = _pair_sum(dS_b * kka_t.astype(BF16), ones2)
            ds_scr[...] = dS * w_t - dsa * kk_t
            sc_scr[...] = s_prev
            for ref, sign, val in zip((dr_ref, dw_ref, dk2_ref, dv_ref, dkk_ref, dkka_ref), (1, 1, 1, 1, -1, -1),
                                      (s_cur * dyb, dS * s_prev, dS * vb,
                                       eye2 * _pair_sum(dS_b * k2_t.astype(BF16), ones2), s_prev * dsa, dS * skk)):
                ref[pl.ds(t, 1), :] = _pair_flat(colsum(val) if sign > 0 else -colsum(val))
            return carry

        lax.fori_loop(0, tc, step, 0, unroll=SCAN_UNROLL_BWD)

        if scatter is not None:
            @pl.when(pl.program_id(0) == nchunk - 1)
            def _():
                for cp in copies:
                    cp.wait()

    in_specs = [seq] * 6 + [st_spec, one_state, seq]
    out_specs = [seq] * 6
    out_shape = [jax.ShapeDtypeStruct((T, W), F32)] * 6
    scratch = [pltpu.VMEM((N_PAIRS, HEAD_DIM, 128), F32)] * 2 + [pltpu.VMEM((tc, W), F32)] * 2
    args = [r, w, k2, v, kk, kka, states, final_state, dy]
    if scatter is not None:
        in_specs += [HBM_SPEC] * nb
        out_specs += [HBM_SPEC] * nb
        out_shape += [jax.ShapeDtypeStruct((npeer, *s.shape[2:]), s.dtype) for s in scatter]
        scratch += [pltpu.SemaphoreType.DMA((npeer * nb,)), pltpu.SemaphoreType.DMA((npeer * nb,))]
        args += list(scatter)
    return pl.pallas_call(
        body, name="rwkv_scan_bwd", grid=(nchunk,), in_specs=in_specs, out_specs=out_specs, out_shape=out_shape,
        scratch_shapes=scratch, compiler_params=_cp(("arbitrary",)),
    )(*args)


def _mix_common(y, r, k2, v, lnx_w, lnx_b, r_k):
    yc = y - _segsum(y) * (1.0 / HEAD_DIM)
    rstd = lax.rsqrt(_segsum(yc * yc) * (1.0 / HEAD_DIM) + LNX_EPS)
    yhat = yc * rstd
    s = _segsum(r * k2 * r_k)
    return rstd, yhat, s, yhat * lnx_w + lnx_b + s * v


def mix_gate_fwd(y, r, k2, v, g, y_mem, lnx_w, lnx_b, r_k):
    T = y.shape[0]

    def fn(i, nblk, y, r, k2, v, g, ym, lw, lb, rk):
        mix = _mix_common(y, r, k2, v, lw, lb, rk)[3]
        return jnp.concatenate([mix * g, ym], axis=1)

    return _rowmap(fn, name="mix_gate_fwd", T=T, tb=256,
                   ins=[("row", z) for z in (y, r, k2, v, g, y_mem)] + [("const", c) for c in (lnx_w, lnx_b, r_k)],
                   outs=[("row", RWKV_WIDTH + MEM_WIDTH, MMD)])[0]


def mix_gate_bwd(y, r, k2, v, g, dycat, lnx_w, lnx_b, r_k):
    T = y.shape[0]

    def fn(i, nblk, y, r, k2, v, g, dyc, lw, lb, rk):
        rstd, yhat, s, mix = _mix_common(y, r, k2, v, lw, lb, rk)
        dmix = dyc * g
        dyh = dmix * lw
        inv = 1.0 / HEAD_DIM
        dy = rstd * (dyh - _segsum(dyh) * inv - yhat * (_segsum(dyh * yhat) * inv))
        ds = _segsum(dmix * v)
        cs = lambda z: jnp.sum(z, axis=0, keepdims=True)
        return (dy, ds * k2 * rk, ds * r * rk, dmix * s, dyc * mix, cs(dmix * yhat), cs(dmix), cs(ds * r * k2))

    return _rowmap(fn, name="mix_gate_bwd", T=T, tb=256,
                   ins=[("row", z) for z in (y, r, k2, v, g)] + [("row", dycat, RWKV_WIDTH, 0)]
                   + [("const", c) for c in (lnx_w, lnx_b, r_k)],
                   outs=[("row", RWKV_WIDTH, F32)] * 5, accs=[((1, RWKV_WIDTH), F32)] * 3)


def _head_rms(x):
    ms = _segsum(x * x) * (1.0 / HEAD_DIM)
    r = lax.rsqrt(ms + RMS_EPS)
    return r, x * r


def _head_rms_bwd(dxn_g, r, xh):
    return r * (dxn_g - xh * (_segsum(dxn_g * xh) * (1.0 / HEAD_DIM)))


def mem_kv_fwd(mem, norm_g, w_kv, k_norm_t, *, name):
    def body(mem_ref, g_ref, w_ref, kn_ref, k_out, v_out):
        _, xh = _rms_stats(mem_ref[...])
        kv = _dot(xh * g_ref[...], w_ref[...])
        _, kh = _head_rms(kv[:, :MEM_WIDTH])
        k_out[...] = kh * kn_ref[...]
        v_out[...] = kv[:, MEM_WIDTH:]

    return pl.pallas_call(
        body, name=name, out_shape=[jax.ShapeDtypeStruct((N_MEM, MEM_WIDTH), F32)] * 2, compiler_params=_cp(),
    )(mem, norm_g, w_kv, k_norm_t)


def mem_kv_bwd(mem, norm_g, w_kv, k_norm_t, dkn, dv, *, name):
    fold = _fold_ones(MEM_WIDTH)

    def body(mem_ref, g_ref, w_ref, kn_ref, fo_ref, dkn_ref, dv_ref, dw_out, dg_out, dkg_out):
        _, xh = _rms_stats(mem_ref[...])
        hm = xh * g_ref[...]
        kv = _dot(hm, w_ref[...])
        r, kh = _head_rms(kv[:, :MEM_WIDTH])
        dkn = dkn_ref[...]
        dkg_out[...] = _fold_heads(jnp.sum(dkn * kh, axis=0, keepdims=True), fo_ref[...])
        dkraw = _head_rms_bwd(dkn * kn_ref[...], r, kh)
        dkv = jnp.concatenate([dkraw, dv_ref[...]], axis=1)
        dw_out[...] = _dot_tn(hm, dkv)
        dg_out[...] = jnp.sum(_dot_nt(dkv, w_ref[...]) * xh, axis=0, keepdims=True)

    return pl.pallas_call(
        body, name=name,
        out_shape=[jax.ShapeDtypeStruct((D_MODEL, 2 * MEM_WIDTH), F32), jax.ShapeDtypeStruct((1, D_MODEL), F32),
                   jax.ShapeDtypeStruct((1, HEAD_DIM), F32)],
        compiler_params=_cp(),
    )(mem, norm_g, w_kv, k_norm_t, fold, dkn, dv)


def _mem_scores(qn, kn, masks, h):
    s = _dot_nt(qn * masks[h], kn) * (1.0 / math.sqrt(HEAD_DIM))
    s = s - jnp.max(s, axis=-1, keepdims=True)
    e = jnp.exp(s)
    return e / jnp.sum(e, axis=-1, keepdims=True)


def mem_attn_fwd(p, colblock, kn, v, q_norm_t, *, name):
    T = p.shape[0]
    masks = _head_masks(MEM_WIDTH)

    def fn(i, nblk, q, kn, v, qg, masks):
        _, qh = _head_rms(q)
        qn = qh * qg
        out = jnp.zeros(q.shape, F32)
        for h in range(MEM_WIDTH // HEAD_DIM):
            out = out + _dot(_mem_scores(qn, kn, masks, h), v * masks[h])
        return out

    return _rowmap(fn, name=name, T=T, tb=512,
                   ins=[("row", p, MEM_WIDTH, colblock)] + [("const", c) for c in (kn, v, q_norm_t, masks)],
                   outs=[("row", MEM_WIDTH, F32)])[0]


def mem_attn_bwd(p, colblock, kn, v, q_norm_t, dycat, dcolblock, *, name):
    T = p.shape[0]
    masks, fold = _head_masks(MEM_WIDTH), _fold_ones(MEM_WIDTH)
    scale = 1.0 / math.sqrt(HEAD_DIM)

    def fn(i, nblk, q, dy, kn, v, qg, masks, fo):
        r, qh = _head_rms(q)
        qn = qh * qg
        dqn = jnp.zeros(q.shape, F32)
        dkn = jnp.zeros(kn.shape, F32)
        dv = jnp.zeros(v.shape, F32)
        for h in range(MEM_WIDTH // HEAD_DIM):
            pr = _mem_scores(qn, kn, masks, h)
            dyh = dy * masks[h]
            dpr = _dot_nt(dyh, v)
            ds = pr * (dpr - jnp.sum(dpr * pr, axis=-1, keepdims=True)) * scale
            dqn = dqn + _dot(ds, kn * masks[h])
            dkn = dkn + _dot_tn(ds, qn * masks[h])
            dv = dv + _dot_tn(pr, dyh)
        dqg = _fold_heads(jnp.sum(dqn * qh, axis=0, keepdims=True), fo)
        return _head_rms_bwd(dqn * qg, r, qh), dkn, dv, dqg

    return _rowmap(fn, name=name, T=T, tb=512,
                   ins=[("row", p, MEM_WIDTH, colblock), ("row", dycat, MEM_WIDTH, dcolblock)]
                   + [("const", c) for c in (kn, v, q_norm_t, masks, fold)],
                   outs=[("row", MEM_WIDTH, F32)],
                   accs=[((N_MEM, MEM_WIDTH), F32), ((N_MEM, MEM_WIDTH), F32), ((1, HEAD_DIM), F32)])


def _ffn_conv(i, u, halo, cw, cb):
    up1 = _shift_down(u, _row_pick(halo, 7), i == 0)
    up2 = _shift_down(up1, _row_pick(halo, 6), i == 0)
    c = cb + cw[0] * up2 + cw[1] * up1 + cw[2] * u
    return up1, up2, c[:, :D_FF], c[:, D_FF:]


def ffn_act_fwd(u, cw, cb, *, name):
    T = u.shape[0]

    def fn(i, nblk, u, halo, c0, c1, c2, cb):
        _, _, gate, val = _ffn_conv(i, u, halo, (c0, c1, c2), cb)
        return jax.nn.silu(gate) * val

    return _rowmap(fn, name=name, T=T, tb=128, ins=[("row", u), ("prev", u)] + [("const", c) for c in (*cw, cb)],
                   outs=[("row", D_FF, MMD)])[0]


def ffn_act_bwd(u, cw, cb, dz, *, name):
    T = u.shape[0]
    tb = 128

    def fn(i, nblk, u, halo, unext, c0, c1, c2, cb, dz, dznext):
        ue = jnp.concatenate([u, unext], axis=0)
        dze = jnp.concatenate([dz, jnp.where(i == nblk - 1, 0.0, 1.0) * dznext], axis=0)
        up1, up2, gate, val = _ffn_conv(i, ue, halo, (c0, c1, c2), cb)
        sg = jax.nn.sigmoid(gate)
        dce = jnp.concatenate([dze * val * sg * (1.0 + gate * (1.0 - sg)), dze * gate * sg], axis=1)
        rows = dce.shape[0]
        du = (c2 * dce + c1 * pltpu.roll(dce, rows - 1, axis=0) + c0 * pltpu.roll(dce, rows - 2, axis=0))[:tb]
        dc = dce[:tb]
        s = lambda z: jnp.sum(z, axis=0, keepdims=True)
        return du, s(dc * up2[:tb]), s(dc * up1[:tb]), s(dc * u), s(dc)

    return _rowmap(fn, name=name, T=T, tb=tb,
                   ins=[("row", u), ("prev", u), ("next", u)] + [("const", c) for c in (*cw, cb)]
                   + [("row", dz), ("next", dz)],
                   outs=[("row", 2 * D_FF, MMD)], accs=[((1, 2 * D_FF), F32)] * 4)


def _rope_swap(z):
    lane = lax.broadcasted_iota(jnp.int32, z.shape, 1) % HEAD_DIM
    w = z.shape[1]
    return jnp.where(lane < HEAD_DIM // 2, pltpu.roll(z, w - HEAD_DIM // 2, axis=1), pltpu.roll(z, HEAD_DIM // 2, axis=1))


def rope_tables(T):
    inv = (np.float32(ROPE_THETA) ** (-np.arange(0, HEAD_DIM, 2, dtype=np.float32) / np.float32(HEAD_DIM))).astype(np.float32)
    ang = (np.arange(T, dtype=np.float32)[:, None] * inv[None, :]).astype(np.float64)
    cos, sin = np.cos(ang).astype(np.float32), np.sin(ang).astype(np.float32)
    return (jnp.asarray(np.concatenate([cos, cos, cos, cos], axis=1)),
            jnp.asarray(np.concatenate([-sin, sin, -sin, sin], axis=1)))


def _rope_wide(t):
    return jnp.tile(t, (1, DIL_WIDTH // t.shape[1]))


def qk_fwd(kvp, pb, kg_t, qg_t, cos, sin):
    T = kvp.shape[0]

    def fn(i, nblk, kraw, vraw, qraw, kg, qg, c, s):
        c, s = _rope_wide(c), _rope_wide(s)
        outs = []
        for raw, g in ((qraw, qg), (kraw, kg)):
            _, xh = _head_rms(raw)
            z = xh * g
            outs.append(z * c + _rope_swap(z) * s)
        return outs[0], outs[1], vraw

    return _rowmap(fn, name="qk_fwd", T=T, tb=256,
                   ins=[("row", kvp, DIL_WIDTH, 0), ("row", kvp, DIL_WIDTH, 1), ("row", pb, DIL_WIDTH, 0)]
                   + [("const", kg_t), ("const", qg_t), ("row", cos), ("row", sin)],
                   outs=[("row", DIL_WIDTH, MMD)] * 3)


def qk_bwd(kvp, pb, kg_t, qg_t, cos, sin, dq, dk, dv, dq_mem):
    T = kvp.shape[0]
    fold = _fold_ones(DIL_WIDTH)

    def fn(i, nblk, kraw, qraw, kg, qg, c, s, fo, dq, dk, dv, dqm):
        c, s = _rope_wide(c), _rope_wide(s)
        res, dgs = [], []
        for raw, g, d in ((qraw, qg, dq), (kraw, kg, dk)):
            r, xh = _head_rms(raw)
            dz = d * c + _rope_swap(d * s)
            dgs.append(_fold_heads(jnp.sum(dz * xh, axis=0, keepdims=True), fo))
            res.append(_head_rms_bwd(dz * g, r, xh))
        return (jnp.concatenate([res[0], dqm], axis=1), jnp.concatenate([res[1], dv], axis=1), dgs[0], dgs[1])

    return _rowmap(fn, name="qk_bwd", T=T, tb=256,
                   ins=[("row", kvp, DIL_WIDTH, 0), ("row", pb, DIL_WIDTH, 0), ("const", kg_t), ("const", qg_t),
                        ("row", cos), ("row", sin), ("const", fold),
                        ("row", dq), ("row", dk), ("row", dv), ("row", dq_mem)],
                   outs=[("row", DIL_WIDTH + MEM_WIDTH, MMD), ("row", 2 * DIL_WIDTH, MMD)],
                   accs=[((1, HEAD_DIM), F32)] * 2)


def _band(kind):
    i = lax.broadcasted_iota(jnp.int32, (DIL_BLOCK, DIL_BLOCK), 0)
    j = lax.broadcasted_iota(jnp.int32, (DIL_BLOCK, DIL_BLOCK), 1)
    return (j <= i) if kind == "cur" else (j >= i)


def dil_attn_fwd(q, k, v, seq_blocks, *, name):
    T, W = q.shape
    nb = T // DIL_BLOCK
    masks = _head_masks(W)
    cur = pl.BlockSpec((DIL_BLOCK, W), lambda n: (n, 0))
    prv = pl.BlockSpec((DIL_BLOCK, W), lambda n: (jnp.maximum(n - 1, 0), 0))
    scale = 1.0 / math.sqrt(HEAD_DIM)

    def body(q_ref, kc_ref, kp_ref, vc_ref, vp_ref, m_ref, o_ref, l_ref):
        n = pl.program_id(0)
        has_prev = (n % seq_blocks) != 0
        q = q_ref[...].astype(F32)
        kc, kp = kc_ref[...].astype(F32), kp_ref[...].astype(F32)
        vc, vp = vc_ref[...].astype(F32), vp_ref[...].astype(F32)
        ok_c = _band("cur")
        ok_p = jnp.logical_and(_band("prev"), has_prev)
        o = jnp.zeros((DIL_BLOCK, W), F32)
        lse = jnp.zeros((DIL_BLOCK, W), F32)
        for h in range(W // HEAD_DIM):
            mh = m_ref[h]
            qh = q * mh
            sc = jnp.where(ok_c, _dot_nt(qh, kc) * scale, NEG_INF)
            sp = jnp.where(ok_p, _dot_nt(qh, kp) * scale, NEG_INF)
            mx = jnp.maximum(jnp.max(sc, axis=-1, keepdims=True), jnp.max(sp, axis=-1, keepdims=True))
            ec, ep = jnp.exp(sc - mx), jnp.exp(sp - mx)
            den = jnp.sum(ec, axis=-1, keepdims=True) + jnp.sum(ep, axis=-1, keepdims=True)
            o = o + (_dot(ec, vc * mh) + _dot(ep, vp * mh)) / den
            lse = lse + (mx + jnp.log(den)) * mh
        o_ref[...] = o
        l_ref[...] = lse

    return pl.pallas_call(
        body, name=name, grid=(nb,), in_specs=[cur, cur, prv, cur, prv, pl.BlockSpec(masks.shape, lambda n: (0, 0, 0))],
        out_specs=[cur, cur], out_shape=[jax.ShapeDtypeStruct((T, W), F32)] * 2,
        compiler_params=_cp(("parallel",)),
    )(q, k, k, v, v, masks)


def dil_attn_bwd(q, k, v, o, lse, do, dlse, seq_blocks, *, name):
    T, W = q.shape
    nb = T // DIL_BLOCK
    masks = _head_masks(W)
    cur = pl.BlockSpec((DIL_BLOCK, W), lambda n: (n, 0))
    prv = pl.BlockSpec((DIL_BLOCK, W), lambda n: (jnp.maximum(n - 1, 0), 0))
    nxt = pl.BlockSpec((DIL_BLOCK, W), lambda n: (jnp.minimum(n + 1, nb - 1), 0))
    scale = 1.0 / math.sqrt(HEAD_DIM)

    def body(qc_ref, qn_ref, kc_ref, kp_ref, vc_ref, vp_ref, oc_ref, on_ref, lc_ref, ln_ref, doc_ref, don_ref,
             dlc_ref, dln_ref, m_ref, dq_ref, dk_ref, dv_ref):
        n = pl.program_id(0)
        has_prev = (n % seq_blocks) != 0
        has_next = jnp.logical_and(((n + 1) % seq_blocks) != 0, n + 1 < nb)
        f = lambda ref: ref[...].astype(F32)
        qc, qn, kc, kp, vc, vp = f(qc_ref), f(qn_ref), f(kc_ref), f(kp_ref), f(vc_ref), f(vp_ref)
        doc, don = doc_ref[...], don_ref[...]
        ok_c = _band("cur")
        ok_p = jnp.logical_and(_band("prev"), has_prev)
        ok_n = jnp.logical_and(_band("prev"), has_next)
        dq = jnp.zeros((DIL_BLOCK, W), F32)
        dk = jnp.zeros((DIL_BLOCK, W), F32)
        dv = jnp.zeros((DIL_BLOCK, W), F32)

        def side(qh, kk, vv, doh, lse_h, corr, ok):
            s = _dot_nt(qh, kk) * scale
            pr = jnp.where(ok, jnp.exp(jnp.where(ok, s, NEG_INF) - lse_h), 0.0)
            ds = pr * (_dot_nt(doh, vv) + corr) * scale
            return pr, ds

        for h in range(W // HEAD_DIM):
            mh = m_ref[h]
            red = lambda z: jnp.sum(z * mh, axis=-1, keepdims=True)
            qh, doh = qc * mh, doc * mh
            lse_h = red(lc_ref[...]) * (1.0 / HEAD_DIM)
            corr = red(dlc_ref[...]) - red(doc * oc_ref[...])
            pr_c, ds_c = side(qh, kc, vc * mh, doh, lse_h, corr, ok_c)
            _, ds_p = side(qh, kp, vp * mh, doh, lse_h, corr, ok_p)
            dq = dq + _dot(ds_c, kc * mh) + _dot(ds_p, kp * mh)
            dk = dk + _dot_tn(ds_c, qh)
            dv = dv + _dot_tn(pr_c, doh)
            qh2, doh2 = qn * mh, don * mh
            lse_2 = red(ln_ref[...]) * (1.0 / HEAD_DIM)
            corr2 = red(dln_ref[...]) - red(don * on_ref[...])
            pr_n, ds_n = side(qh2, kc, vc * mh, doh2, lse_2, corr2, ok_n)
            dk = dk + _dot_tn(ds_n, qh2)
            dv = dv + _dot_tn(pr_n, doh2)
        dq_ref[...] = dq
        dk_ref[...] = dk
        dv_ref[...] = dv

    return pl.pallas_call(
        body, name=name, grid=(nb,),
        in_specs=[cur, nxt, cur, prv, cur, prv, cur, nxt, cur, nxt, cur, nxt, cur, nxt,
                  pl.BlockSpec(masks.shape, lambda n: (0, 0, 0))],
        out_specs=[cur] * 3, out_shape=[jax.ShapeDtypeStruct((T, W), F32)] * 3,
        compiler_params=_cp(("parallel",)),
    )(q, q, k, k, v, v, o, o, lse, lse, do, do, dlse, dlse, masks)


def _mix_weights(ls):
    m = jnp.maximum(jnp.maximum(ls[0], ls[1]), ls[2])
    es = [jnp.exp(l - m) for l in ls]
    den = es[0] + es[1] + es[2]
    return [e / den for e in es]


def mix_fwd(os_, ls, y_mem):
    T = y_mem.shape[0]

    def fn(i, nblk, o0, o1, o2, l0, l1, l2, ym):
        w = _mix_weights((l0, l1, l2))
        return jnp.concatenate([w[0] * o0 + w[1] * o1 + w[2] * o2, ym], axis=1)

    return _rowmap(fn, name="mix_fwd", T=T, tb=512, ins=[("row", z) for z in (*os_, *ls, y_mem)],
                   outs=[("row", 2 * MEM_WIDTH, MMD)])[0]


def mix_bwd(os_, ls, dycat):
    T = dycat.shape[0]

    def fn(i, nblk, o0, o1, o2, l0, l1, l2, dy):
        w = _mix_weights((l0, l1, l2))
        os3 = (o0, o1, o2)
        dws = [dy * o for o in os3]
        tot = w[0] * dws[0] + w[1] * dws[1] + w[2] * dws[2]
        return tuple(wg * dy for wg in w) + tuple(wg * (dw - tot) for wg, dw in zip(w, dws))

    return _rowmap(fn, name="mix_bwd", T=T, tb=512,
                   ins=[("row", z) for z in (*os_, *ls)] + [("row", dycat, MEM_WIDTH, 0)],
                   outs=[("row", MEM_WIDTH, F32)] * 6)


def loss_fwd_bwd(y, target):
    T, D = y.shape

    def fn(i, nblk, y, t):
        e = y - t
        return e * (1.0 / D), jnp.zeros((8, 128), F32) + jnp.sum(e * e) * (0.5 / D)

    return _rowmap(fn, name="loss", T=T, tb=512, ins=[("row", y), ("row", target)], outs=[("row", D, F32)],
                   accs=[((8, 128), F32)])


def _to_residues(z, dil):
    T, W = z.shape
    return z.reshape(T // dil, dil, W).transpose(1, 0, 2).reshape(T, W)


def _from_residues(z, dil):
    T, W = z.shape
    return z.reshape(dil, T // dil, W).transpose(1, 0, 2).reshape(T, W)


def _pad_rows(w, rows):
    return jnp.concatenate([w, jnp.zeros((rows - w.shape[0], w.shape[1]), w.dtype)], axis=0)


def _tile_heads(g, width):
    return jnp.tile(g.reshape(1, HEAD_DIM), (1, width // HEAD_DIM))


def _conv_rows(W, i):
    return [W["ffn_conv_w"][i][j:j + 1] for j in range(3)]


def _ffn_fwd(x, i, W):
    hn = rms_fwd(x, [W["ffn_norm"][i:i + 1]], name=f"ffn_rms{i}")[0]
    u = _mm(hn, W["ffn_w_up"][i], name=f"ffn_up{i}")
    z = ffn_act_fwd(u, _conv_rows(W, i), W["ffn_conv_b"][i:i + 1], name=f"ffn_act{i}")
    out = _mm(z, W["ffn_w_down"][i], add=x, name=f"ffn_down{i}")
    return out, (x, hn, u, z)


def _ffn_bwd(dout, i, W, saved, G):
    x, hn, u, z = saved
    dz = _mm(dout, W["ffn_w_down"][i], tb=True, name=f"ffn_down_dx{i}")
    G["ffn_w_down"][i] = _mm(z, dout, ta=True, name=f"ffn_down_dw{i}")
    du, dw0, dw1, dw2, db = ffn_act_bwd(u, _conv_rows(W, i), W["ffn_conv_b"][i:i + 1], dz, name=f"ffn_act_bwd{i}")
    G["ffn_conv_w"][i] = jnp.concatenate([dw0, dw1, dw2], axis=0)
    G["ffn_conv_b"][i] = db[0]
    dhn = _mm(du, W["ffn_w_up"][i], tb=True, name=f"ffn_up_dx{i}")
    G["ffn_w_up"][i] = _mm(hn, du, ta=True, name=f"ffn_up_dw{i}")
    dx, dg = rms_bwd(x, [W["ffn_norm"][i:i + 1]], [dhn], dout, name=f"ffn_rms_bwd{i}")
    G["ffn_norm"][i] = dg[0]
    return dx


def local_step(x, mem, target, W, late=None):
    T = x.shape[0]
    W = dict(W)
    G = {"ffn_w_down": [None, None], "ffn_w_up": [None, None], "ffn_conv_w": [None, None],
         "ffn_conv_b": [None, None], "ffn_norm": [None, None], "attn_norm": [None, None], "mem_norm": [None, None],
         "mem_w_kv": [None, None], "mem_q_norm": [None, None], "mem_k_norm": [None, None]}
    mu, w0, a0 = W["a_mu"], W["a_w0"], W["a_a0"]
    w2p, a2p, g2p = (_pad_rows(W["a_w2"][0], LORA_WIDTH),
                     jnp.concatenate([jnp.zeros((64, RWKV_WIDTH), MMD), W["a_a2"][0],
                                      jnp.zeros((128, RWKV_WIDTH), MMD)], axis=0),
                     jnp.concatenate([jnp.zeros((128, RWKV_WIDTH), MMD), W["a_g2"][0]], axis=0))
    k_k, k_a, lnx_w, lnx_b = W["a_k_k"], W["a_k_a"], W["a_lnx_w"], W["a_lnx_b"]
    r_k = W["a_r_k"].reshape(1, RWKV_WIDTH)

    h0 = rms_fwd(x, [W["attn_norm"][0:1]], name="attn_rms0")[0]
    p = _mm(h0, W["a_w_in"][0], name="a_in")
    r, w, k2, v, kk, kka, g = rwkv_pre_fwd(p, mu, w0, a0, w2p, a2p, g2p, k_k, k_a)
    if late is None:
        y, states, final_state = scan_fwd(r, w, k2, v, kk, kka)
    else:
        y, states, final_state, *gathered = scan_fwd(r, w, k2, v, kk, kka, gather=late[0])
        W.update(late[1](gathered))
    memkv = []
    for i in range(2):
        memkv.append(mem_kv_fwd(mem, W["mem_norm"][i:i + 1], W["mem_w_kv"][i], _tile_heads(W["mem_k_norm"][i], MEM_WIDTH),
                                name=f"mem_kv{i}"))
    qg0 = _tile_heads(W["mem_q_norm"][0], MEM_WIDTH)
    y_mem0 = mem_attn_fwd(p, SHIFT_WIDTH // MEM_WIDTH, memkv[0][0], memkv[0][1], qg0, name="mem_attn0")
    ycat0 = mix_gate_fwd(y, r, k2, v, g, y_mem0, lnx_w, lnx_b, r_k)
    x1 = _mm(ycat0, W["a_w_out"][0], add=x, name="a_out")
    x2, ffn0 = _ffn_fwd(x1, 0, W)

    h1, hkv = rms_fwd(x2, [W["attn_norm"][1:2], W["kv_norm"].reshape(1, -1)], name="attn_rms1")
    kvp = _mm(hkv, W["kv_w"], name="kv_in")
    pb = _mm(h1, W["b_w_in"][0], name="b_in")
    cos, sin = rope_tables(T)
    kg_t, qg_t = _tile_heads(W["kv_k_norm"], DIL_WIDTH), _tile_heads(W["b_q_norm"][0], DIL_WIDTH)
    q, ksh, vsh = qk_fwd(kvp, pb, kg_t, qg_t, cos, sin)
    os_, ls, grp = [], [], []
    for gi, (win, dil) in enumerate(DIL_GROUPS):
        sl = slice(gi * MEM_WIDTH, (gi + 1) * MEM_WIDTH)
        qg_, kg_, vg_ = (_to_residues(z[:, sl], dil) for z in (q, ksh, vsh))
        o_r, l_r = dil_attn_fwd(qg_, kg_, vg_, T // dil // DIL_BLOCK, name=f"dil_fwd{gi}")
        grp.append((qg_, kg_, vg_, o_r, l_r))
        os_.append(_from_residues(o_r, dil))
        ls.append(_from_residues(l_r, dil))
    qg1 = _tile_heads(W["mem_q_norm"][1], MEM_WIDTH)
    y_mem1 = mem_attn_fwd(pb, DIL_WIDTH // MEM_WIDTH, memkv[1][0], memkv[1][1], qg1, name="mem_attn1")
    ycat1 = mix_fwd(os_, ls, y_mem1)
    x3 = _mm(ycat1, W["b_w_out"][0], add=x2, name="b_out")
    x4, ffn1 = _ffn_fwd(x3, 1, W)

    dx4, loss = loss_fwd_bwd(x4, target)

    dx3 = _ffn_bwd(dx4, 1, W, ffn1, G)
    dycat1 = _mm(dx3, W["b_w_out"][0], tb=True, name="b_out_dx")
    G["b_w_out"] = _mm(ycat1, dx3, ta=True, name="b_out_dw")[None]
    dq_mem1, dkn1, dvm1, dqg1 = mem_attn_bwd(pb, DIL_WIDTH // MEM_WIDTH, memkv[1][0], memkv[1][1], qg1, dycat1, 1,
                                             name="mem_attn_bwd1")
    G["mem_q_norm"][1] = dqg1[0]
    d_os_ls = mix_bwd(os_, ls, dycat1)
    dqs, dks, dvs = [], [], []
    for gi, (win, dil) in enumerate(DIL_GROUPS):
        qg_, kg_, vg_, o_r, l_r = grp[gi]
        do_r, dl_r = _to_residues(d_os_ls[gi], dil), _to_residues(d_os_ls[3 + gi], dil)
        dq_r, dk_r, dv_r = dil_attn_bwd(qg_, kg_, vg_, o_r, l_r, do_r, dl_r, T // dil // DIL_BLOCK, name=f"dil_bwd{gi}")
        dqs.append(_from_residues(dq_r, dil))
        dks.append(_from_residues(dk_r, dil))
        dvs.append(_from_residues(dv_r, dil))
    dq, dk, dv = (jnp.concatenate(z, axis=1) for z in (dqs, dks, dvs))
    dpb, dkvp, dqn_g, dkn_g = qk_bwd(kvp, pb, kg_t, qg_t, cos, sin, dq, dk, dv, dq_mem1)
    G["b_q_norm"] = dqn_g
    G["kv_k_norm"] = dkn_g[0]
    dh1 = _mm(dpb, W["b_w_in"][0], tb=True, name="b_in_dx")
    G["b_w_in"] = _mm(h1, dpb, ta=True, name="b_in_dw")[None]
    dhkv = _mm(dkvp, W["kv_w"], tb=True, name="kv_in_dx")
    G["kv_w"] = _mm(hkv, dkvp, ta=True, name="kv_in_dw")
    dx2, dg1, dgkv = rms_bwd(x2, [W["attn_norm"][1:2], W["kv_norm"].reshape(1, -1)], [dh1, dhkv], dx3,
                             name="attn_rms_bwd1")
    G["attn_norm"][1] = dg1[0]
    G["kv_norm"] = dgkv[0]

    dx1 = _ffn_bwd(dx2, 0, W, ffn0, G)
    dycat0 = _mm(dx1, W["a_w_out"][0], tb=True, name="a_out_dx")
    G["a_w_out"] = _mm(ycat0, dx1, ta=True, name="a_out_dw")[None]
    dq_mem0, dkn0, dvm0, dqg0 = mem_attn_bwd(p, SHIFT_WIDTH // MEM_WIDTH, memkv[0][0], memkv[0][1], qg0, dycat0,
                                             RWKV_WIDTH // MEM_WIDTH, name="mem_attn_bwd0")
    G["mem_q_norm"][0] = dqg0[0]
    dy, dr_b, dk2_b, dv_b, dg, dlw, dlb, drk = mix_gate_bwd(y, r, k2, v, g, dycat0, lnx_w, lnx_b, r_k)
    for i, (dkn, dvm) in enumerate(((dkn0, dvm0), (dkn1, dvm1))):
        dwkv, dgm, dkg = mem_kv_bwd(mem, W["mem_norm"][i:i + 1], W["mem_w_kv"][i],
                                    _tile_heads(W["mem_k_norm"][i], MEM_WIDTH), dkn, dvm, name=f"mem_kv_bwd{i}")
        G["mem_w_kv"][i], G["mem_norm"][i], G["mem_k_norm"][i] = dwkv, dgm[0], dkg[0]
    late_out = None
    if late is None:
        dr, dw, dk2, dv, dkk, dkka = scan_bwd(r, w, k2, v, kk, kka, states, final_state, dy)
    else:
        pieces = late[2](G)
        dr, dw, dk2, dv, dkk, dkka, *received = scan_bwd(r, w, k2, v, kk, kka, states, final_state, dy, scatter=pieces)
        late_out = (received, pieces)
    dxs, dmu, dw0, da0, dw2p, da2p, dg2p, dk_k, dk_a = rwkv_pre_bwd(
        p, mu, w0, a0, w2p, a2p, g2p, k_k, k_a, (dr, dr_b), dw, (dk2, dk2_b), (dv, dv_b), dkk, dkka, dg)
    dp = shift_bwd(dxs, mu, dq_mem0)
    G.update(a_mu=dmu, a_w0=dw0, a_a0=da0, a_w2=dw2p[None, :64], a_a2=da2p[None, 64:128], a_g2=dg2p[None, 128:],
             a_k_k=dk_k, a_k_a=dk_a, a_r_k=drk.reshape(1, RWKV_HEADS, HEAD_DIM), a_lnx_w=dlw, a_lnx_b=dlb)
    dh0 = _mm(dp, W["a_w_in"][0], tb=True, name="a_in_dx")
    G["a_w_in"] = _mm(h0, dp, ta=True, name="a_in_dw")[None]
    grad_x, dg0 = rms_bwd(x, [W["attn_norm"][0:1]], [dh0], dx1, name="attn_rms_bwd0")
    G["attn_norm"][0] = dg0[0]
    for n in list(G):
        if isinstance(G[n], list):
            G[n] = jnp.stack(G[n], axis=0)
    return loss, grad_x, G, late_out


HBM_SPEC = pl.BlockSpec(memory_space=pltpu.HBM)


def _mesh_pos():
    return lax.axis_index("x"), lax.axis_index("y"), lax.axis_index("c")


def _other_chips(x, y):
    return [(1 - x, y), (x, 1 - y), (1 - x, 1 - y)]


def _remote(send_sems, recv_sems, k, src, dst, to):
    return pltpu.make_async_remote_copy(src_ref=src, dst_ref=dst, send_sem=send_sems.at[k], recv_sem=recv_sems.at[k],
                                        device_id=to, device_id_type=MESH)


def _comm_call(body, name, ins, out_shape, n_remote):
    scratch = [pltpu.SemaphoreType.DMA((n_remote,)), pltpu.SemaphoreType.DMA((n_remote,))]
    return pl.pallas_call(body, name=name, in_specs=[HBM_SPEC] * len(ins), out_specs=[HBM_SPEC] * len(out_shape),
                          out_shape=out_shape, scratch_shapes=scratch)(*ins)


def comm_gather(wbig, wsm):
    def body(wb, ws, ob, os_, send_sems, recv_sems):
        x, y, c = _mesh_pos()
        s = 2 * x + y
        me, sibling = (x, y, c), (x, y, 1 - c)
        chips = _other_chips(x, y)
        rc = functools.partial(_remote, send_sems, recv_sems)
        first = []
        for j, (cx, cy) in enumerate(chips):
            first.append(rc(j, wb.at[c], ob.at[s, c], (cx, cy, c)))
            first.append(rc(6 + j, ws, os_.at[s], (cx, cy, c)))
        for cp in first:
            cp.start()
        passed = []
        for j, (cx, cy) in enumerate(chips):
            blk = ob.at[2 * cx + cy, c]
            rc(j, blk, blk, me).wait_recv()
            passed.append(rc(3 + j, blk, blk, sibling))
            passed[-1].start()
        for j, (cx, cy) in enumerate(chips):
            blk = ob.at[2 * cx + cy, 1 - c]
            rc(3 + j, blk, blk, me).wait_recv()
            sb = os_.at[2 * cx + cy]
            rc(6 + j, sb, sb, me).wait_recv()
        for cp in first + passed:
            cp.wait_send()

    out_shape = [jax.ShapeDtypeStruct((N_CHIPS, *wbig.shape), wbig.dtype),
                 jax.ShapeDtypeStruct((N_CHIPS, *wsm.shape), wsm.dtype)]
    return _comm_call(body, "comm_gather", [wbig, wsm], out_shape, 9)


def comm_pair_exchange(gb, gs):
    def body(gb_ref, gs_ref, rb_ref, rs_ref, send_sems, recv_sems):
        x, y, c = _mesh_pos()
        sibling = (x, y, 1 - c)
        rc = functools.partial(_remote, send_sems, recv_sems)
        cps = [rc(r, gb_ref.at[r, 1 - c], rb_ref.at[r], sibling) for r in range(N_CHIPS)]
        cps.append(rc(N_CHIPS, gs_ref.at[1 - c], rs_ref, sibling))
        for cp in cps:
            cp.start()
        for cp in cps:
            cp.wait()

    out_shape = [jax.ShapeDtypeStruct((N_CHIPS, *gb.shape[2:]), gb.dtype), jax.ShapeDtypeStruct(gs.shape[1:], gs.dtype)]
    return _comm_call(body, "comm_pair_exchange", [gb, gs], out_shape, N_CHIPS + 1)


def comm_chip_exchange(hb, hs):
    def body(hb_ref, hs_ref, qb_ref, qs_ref, send_sems, recv_sems):
        x, y, c = _mesh_pos()
        s = 2 * x + y
        me = (x, y, c)
        chips = _other_chips(x, y)
        rc = functools.partial(_remote, send_sems, recv_sems)
        cps = []
        for j, (cx, cy) in enumerate(chips):
            cps.append(rc(j, hb_ref.at[2 * cx + cy], qb_ref.at[s], (cx, cy, c)))
            cps.append(rc(3 + j, hs_ref, qs_ref.at[s], (cx, cy, c)))
        for cp in cps:
            cp.start()
        for j, (cx, cy) in enumerate(chips):
            blk = qb_ref.at[2 * cx + cy]
            rc(j, blk, blk, me).wait_recv()
            sb = qs_ref.at[2 * cx + cy]
            rc(3 + j, sb, sb, me).wait_recv()
        for cp in cps:
            cp.wait_send()

    out_shape = [jax.ShapeDtypeStruct(hb.shape, hb.dtype), jax.ShapeDtypeStruct((N_CHIPS, *hs.shape), hs.dtype)]
    return _comm_call(body, "comm_chip_exchange", [hb, hs], out_shape, 6)


def comm_pair_share(halves):
    n = len(halves)

    def body(*refs):
        x, y, c = _mesh_pos()
        send_sems, recv_sems = refs[2 * n], refs[2 * n + 1]
        cps = [_remote(send_sems, recv_sems, k, refs[k], refs[n + k], (x, y, 1 - c)) for k in range(n)]
        for cp in cps:
            cp.start()
        for cp in cps:
            cp.wait()

    out_shape = [jax.ShapeDtypeStruct(h.shape, h.dtype) for h in halves]
    return _comm_call(body, "comm_pair_share", list(halves), out_shape, n)


def add_pairs(a, b, out_dtype, *, name, tb):
    T, L = a.shape
    return _rowmap(lambda i, n, p, q: p + q, name=name, T=T, tb=tb, ins=[("row", a), ("row", b)],
                   outs=[("row", L, out_dtype)])[0]


def add_chips(parts, *, name, tb):
    T, L = parts[0].shape

    def fn(i, n, *ps):
        acc = ps[0].astype(F32)
        for p in ps[1:]:
            acc = acc + p.astype(F32)
        return acc

    return _rowmap(fn, name=name, T=T, tb=tb, ins=[("row", p) for p in parts], outs=[("row", L, F32)])[0]


def adamw(g, w, m, v, *, name, tb):
    T, L = g.shape

    def fn(i, n, g, w, m, v):
        m2 = ADAM_B1 * m + (1.0 - ADAM_B1) * g
        v2 = ADAM_B2 * v + (1.0 - ADAM_B2) * (g * g)
        m_hat = m2 / (1.0 - ADAM_B1 ** ADAM_STEP)
        v_hat = v2 / (1.0 - ADAM_B2 ** ADAM_STEP)
        return -ADAM_LR * (m_hat / (jnp.sqrt(v_hat) + ADAM_EPS) + ADAM_WD * w), m2, v2

    return _rowmap(fn, name=name, T=T, tb=tb, ins=[("row", z) for z in (g, w, m, v)], outs=[("row", L, F32)] * 3)


BIG_LANES = 1024
SMALL_LANES = 128


def _flat_cat(arrs, total, dtype):
    parts = [a.reshape(-1).astype(dtype) for a in arrs]
    n = sum(p.shape[0] for p in parts)
    assert n <= total, (n, total)
    if n < total:
        parts.append(jnp.zeros((total - n,), dtype))
    return jnp.concatenate(parts)


def _split_flat(flat, shapes):
    out, off = [], 0
    for shp in shapes:
        n = math.prod(shp)
        out.append(flat[off:off + n].reshape(shp))
        off += n
    return out


def _round_up(n, m):
    return -(-n // m) * m


def _full_shape(shard_shape, axis):
    return tuple(d * N_CHIPS if i == axis else d for i, d in enumerate(shard_shape))


def kernel(x, mem, attn_norm, a_w_in, a_mu, a_w0, a_w2, a_a0, a_a2, a_g2, a_k_k, a_k_a, a_r_k, a_lnx_w, a_lnx_b, a_w_out, kv_norm, kv_w, kv_k_norm, b_w_in, b_q_norm, b_w_out, mem_norm, mem_w_kv, mem_q_norm, mem_k_norm, ffn_norm, ffn_w_up, ffn_conv_w, ffn_conv_b, ffn_w_down, loss_target, m_attn_norm, m_a_w_in, m_a_mu, m_a_w0, m_a_w2, m_a_a0, m_a_a2, m_a_g2, m_a_k_k, m_a_k_a, m_a_r_k, m_a_lnx_w, m_a_lnx_b, m_a_w_out, m_kv_norm, m_kv_w, m_kv_k_norm, m_b_w_in, m_b_q_norm, m_b_w_out, m_mem_norm, m_mem_w_kv, m_mem_q_norm, m_mem_k_norm, m_ffn_norm, m_ffn_w_up, m_ffn_conv_w, m_ffn_conv_b, m_ffn_w_down, v_attn_norm, v_a_w_in, v_a_mu, v_a_w0, v_a_w2, v_a_a0, v_a_a2, v_a_g2, v_a_k_k, v_a_k_a, v_a_r_k, v_a_lnx_w, v_a_lnx_b, v_a_w_out, v_kv_norm, v_kv_w, v_kv_k_norm, v_b_w_in, v_b_q_norm, v_b_w_out, v_mem_norm, v_mem_w_kv, v_mem_q_norm, v_mem_k_norm, v_ffn_norm, v_ffn_w_up, v_ffn_conv_w, v_ffn_conv_b, v_ffn_w_down):
    args = (attn_norm, a_w_in, a_mu, a_w0, a_w2, a_a0, a_a2, a_g2, a_k_k, a_k_a, a_r_k, a_lnx_w, a_lnx_b, a_w_out, kv_norm, kv_w, kv_k_norm, b_w_in, b_q_norm, b_w_out, mem_norm, mem_w_kv, mem_q_norm, mem_k_norm, ffn_norm, ffn_w_up, ffn_conv_w, ffn_conv_b, ffn_w_down)
    ms = (m_attn_norm, m_a_w_in, m_a_mu, m_a_w0, m_a_w2, m_a_a0, m_a_a2, m_a_g2, m_a_k_k, m_a_k_a, m_a_r_k, m_a_lnx_w, m_a_lnx_b, m_a_w_out, m_kv_norm, m_kv_w, m_kv_k_norm, m_b_w_in, m_b_q_norm, m_b_w_out, m_mem_norm, m_mem_w_kv, m_mem_q_norm, m_mem_k_norm, m_ffn_norm, m_ffn_w_up, m_ffn_conv_w, m_ffn_conv_b, m_ffn_w_down)
    vs = (v_attn_norm, v_a_w_in, v_a_mu, v_a_w0, v_a_w2, v_a_a0, v_a_a2, v_a_g2, v_a_k_k, v_a_k_a, v_a_r_k, v_a_lnx_w, v_a_lnx_b, v_a_w_out, v_kv_norm, v_kv_w, v_kv_k_norm, v_b_w_in, v_b_q_norm, v_b_w_out, v_mem_norm, v_mem_w_kv, v_mem_q_norm, v_mem_k_norm, v_ffn_norm, v_ffn_w_up, v_ffn_conv_w, v_ffn_conv_b, v_ffn_w_down)
    w_sh, m_sh, v_sh = (dict(zip(WEIGHTS, z)) for z in (args, ms, vs))
    xi, yi, ci = _mesh_pos()
    chip = 2 * xi + yi
    axes = {**dict(BIG), **dict(SMALL_SHARDED)}
    early_names = [n for n, _ in BIG if n in EARLY_BIG]
    late_names = [n for n, _ in BIG if n not in EARLY_BIG and n != NATURAL_BIG]
    ss_names, ss_axes = [n for n, _ in SMALL_SHARDED], dict(SMALL_SHARDED)
    shapes_of = lambda names: [w_sh[n].shape for n in names]
    count = lambda names: sum(math.prod(s) for s in shapes_of(names))
    n_early, n_late = count(early_names), count(late_names)
    assert n_early % (2 * 16 * BIG_LANES) == 0 and n_late % (2 * 16 * BIG_LANES) == 0
    mh, mh_late = n_early // (2 * BIG_LANES), n_late // (2 * BIG_LANES)
    n_ss = _round_up(count(ss_names), 8 * SMALL_LANES)

    def shard_pack(names, total, dtype, source):
        return _flat_cat([source[n] for n in names], total, dtype)

    def join(pieces, n):
        if pieces[0].ndim == 3:
            return [jnp.concatenate([p[l] for p in pieces], axis=axes[n] - 1) for l in range(pieces[0].shape[0])]
        return jnp.concatenate(pieces, axis=axes[n])

    def unshard(names, gathered):
        per_chip = [_split_flat(gathered[j], shapes_of(names)) for j in range(N_CHIPS)]
        return {n: join([per_chip[j][k] for j in range(N_CHIPS)], n) for k, n in enumerate(names)}

    def whole(g):
        return jnp.stack(g, axis=0) if isinstance(g, list) else g

    def by_chip(names, total, dtype, grads):
        parts = [jnp.split(whole(grads[n]), N_CHIPS, axis=axes[n]) for n in names]
        return jnp.stack([_flat_cat([p[j] for p in parts], total, dtype) for j in range(N_CHIPS)])

    wbig = shard_pack(early_names, n_early, MMD, w_sh).reshape(2, mh, BIG_LANES)
    wsm = shard_pack(ss_names, n_ss, F32, w_sh).reshape(-1, SMALL_LANES)
    wbig_all, wsm_all = comm_gather(wbig, wsm)
    wbig_all = lax.dynamic_update_index_in_dim(wbig_all, wbig, chip, 0).reshape(N_CHIPS, -1)
    wsm_all = lax.dynamic_update_index_in_dim(wsm_all, wsm, chip, 0).reshape(N_CHIPS, -1)
    W = {n: w_sh[n] for n in SMALL_REPL}
    W.update(unshard(early_names, wbig_all))
    W.update(unshard(ss_names, wsm_all))
    for n in ("a_w2", "a_a2", "a_g2"):
        W[n] = [z.astype(MMD) for z in W[n]]
    wlate = shard_pack(late_names, n_late, MMD, w_sh).reshape(2, mh_late, BIG_LANES)
    nat_axis = axes[NATURAL_BIG]
    wnat = w_sh[NATURAL_BIG].astype(MMD)
    assert wnat.shape[0] == 2 and nat_axis != 0

    def unpack_late(gathered):
        full = lax.dynamic_update_index_in_dim(gathered[0], wlate, chip, 0)
        out = unshard(late_names, full.reshape(N_CHIPS, -1))
        nat = lax.dynamic_update_index_in_dim(gathered[1], wnat, chip, 0)
        out[NATURAL_BIG] = join([nat[j] for j in range(N_CHIPS)], NATURAL_BIG)
        return out

    def pack_late(grads):
        return [by_chip(late_names, n_late, BF16, grads).reshape(N_CHIPS, 2, mh_late, BIG_LANES),
                jnp.stack(jnp.split(whole(grads[NATURAL_BIG]).astype(BF16), N_CHIPS, axis=nat_axis))]

    loss_blk, grad_x, G, (received, pieces) = local_step(x[0], mem[0], loss_target[0], W,
                                                          late=([wlate, wnat], unpack_late, pack_late))
    loss = lax.psum(loss_blk[0, 0], ("x", "y", "c"))

    own_piece = lambda p: lax.dynamic_index_in_dim(lax.dynamic_index_in_dim(p, chip, 0, keepdims=False), ci, 0,
                                                   keepdims=False)
    gh_late = add_chips([received[0][k] for k in range(len(PEER_FLIPS))] + [own_piece(pieces[0])],
                        name="add_pieces_late", tb=32)
    gh_nat = add_chips([received[1][k] for k in range(len(PEER_FLIPS))] + [own_piece(pieces[1])],
                       name="add_pieces_natural", tb=32)
    gbig = by_chip(early_names, n_early, F32, G).reshape(N_CHIPS, 2, mh, BIG_LANES)
    sm_full_names = ss_names + list(SMALL_REPL)
    sm_full_shapes = [_full_shape(w_sh[n].shape, ss_axes[n]) for n in ss_names] + [w_sh[n].shape for n in SMALL_REPL]
    n_smf = _round_up(sum(math.prod(s) for s in sm_full_shapes), 2 * 8 * SMALL_LANES)
    msh = n_smf // (2 * SMALL_LANES)
    gsm = _flat_cat([G[n] for n in sm_full_names], n_smf, F32).reshape(2, msh, SMALL_LANES)
    rb, rs = comm_pair_exchange(gbig, gsm)
    mine_b = lax.dynamic_index_in_dim(gbig, ci, axis=1, keepdims=False)
    mine_s = lax.dynamic_index_in_dim(gsm, ci, axis=0, keepdims=False)
    hb = add_pairs(mine_b.reshape(-1, BIG_LANES), rb.reshape(-1, BIG_LANES), BF16, name="add_pairs_big", tb=128)
    hs = add_pairs(mine_s, rs, F32, name="add_pairs_small", tb=msh)
    hb = hb.reshape(N_CHIPS, mh, BIG_LANES)
    qb, qs = comm_chip_exchange(hb, hs)
    qb = lax.dynamic_update_index_in_dim(qb, lax.dynamic_index_in_dim(hb, chip, 0, keepdims=False), chip, 0)
    qs = lax.dynamic_update_index_in_dim(qs, hs, chip, 0)
    gh = add_chips([qb[j] for j in range(N_CHIPS)], name="add_chips_big", tb=32)
    gsh = add_chips([qs[j] for j in range(N_CHIPS)], name="add_chips_small", tb=msh)
    rh, rh_late, rh_nat, rsh = comm_pair_share([gh, gh_late, gh_nat, gsh])
    both = lambda mine_, theirs: jnp.where(ci == 0, jnp.stack([mine_, theirs]), jnp.stack([theirs, mine_]))
    gfull, gfull_late, gfull_nat, gsfull = both(gh, rh), both(gh_late, rh_late), both(gh_nat, rh_nat), both(gsh, rsh)

    res = {tag: {} for tag in ("grad", "delta", "new_m", "new_v")}
    big_grads = (list(zip(early_names, _split_flat(gfull.reshape(-1), shapes_of(early_names))))
                 + list(zip(late_names, _split_flat(gfull_late.reshape(-1), shapes_of(late_names))))
                 + [(NATURAL_BIG, gfull_nat)])
    for n, g in big_grads:
        shp = w_sh[n].shape
        rows = lambda z: z.reshape(-1, shp[-1])
        nrow = math.prod(shp[:-1])
        tb = next(t for t in (512, 256, 128, 64) if nrow % t == 0 and t * shp[-1] <= (1 << 19))
        outs = adamw(rows(g), rows(w_sh[n]), rows(m_sh[n]), rows(v_sh[n]), name=f"adamw_{n}", tb=tb)
        res["grad"][n] = g
        for tag, o in zip(("delta", "new_m", "new_v"), outs):
            res[tag][n] = o.reshape(shp)
    sm_full = dict(zip(sm_full_names, _split_flat(gsfull.reshape(-1), sm_full_shapes)))
    g_loc = {}
    for n in ss_names:
        size = w_sh[n].shape[ss_axes[n]]
        g_loc[n] = lax.dynamic_slice_in_dim(sm_full[n], chip * size, size, axis=ss_axes[n])
    for n in SMALL_REPL:
        g_loc[n] = sm_full[n]
    n_sml = _round_up(sum(math.prod(w_sh[n].shape) for n in sm_full_names), 8 * SMALL_LANES)
    pack_sm = lambda d: _flat_cat([d[n] for n in sm_full_names], n_sml, F32).reshape(-1, SMALL_LANES)
    d_sm, m_sm, v_sm = adamw(pack_sm(g_loc), pack_sm(w_sh), pack_sm(m_sh), pack_sm(v_sh), name="adamw_small",
                             tb=n_sml // SMALL_LANES)
    sm_loc_shapes = [w_sh[n].shape for n in sm_full_names]
    res["grad"].update(g_loc)
    for tag, smv in (("delta", d_sm), ("new_m", m_sm), ("new_v", v_sm)):
        res[tag].update(dict(zip(sm_full_names, _split_flat(smv.reshape(-1), sm_loc_shapes))))
    return (loss, grad_x[None], *[res[tag][n] for tag in ("grad", "delta", "new_m", "new_v") for n in WEIGHTS])
```

```python
import functools
import math

import numpy as np
import jax
import jax.numpy as jnp
from jax import lax
from jax.experimental import pallas as pl
from jax.experimental.pallas import tpu as pltpu

F32 = jnp.float32
BF16 = jnp.bfloat16
MMD = jnp.bfloat16

D_MODEL = 1024
HEAD_DIM = 64
N_MEM = 256
MEM_WIDTH = 256
RWKV_HEADS = 12
RWKV_WIDTH = 768
SHIFT_WIDTH = 2560
LORA_WIDTH = 256
DIL_WIDTH = 768
DIL_GROUPS = ((128, 1), (512, 4), (2048, 16))
DIL_BLOCK = 128
D_FF = 2816
ROPE_THETA = 10000.0
RMS_EPS = 1e-6
LNX_EPS = 64e-5
NEG_INF = -1e30
ADAM_LR = 0.001
ADAM_B1 = 0.9
ADAM_B2 = 0.999
ADAM_EPS = 1e-08
ADAM_WD = 0.01
ADAM_STEP = 10
N_CHIPS = 4
MESH = pl.DeviceIdType.MESH
VMEM_LIMIT_MB = 56
SCAN_CHUNK = 64
SCAN_UNROLL = 32
SCAN_UNROLL_BWD = 16

BIG = (("a_w_in", 2), ("a_w_out", 1), ("kv_w", 1), ("b_w_in", 1), ("b_w_out", 2), ("mem_w_kv", 1),
       ("ffn_w_up", 2), ("ffn_w_down", 1))
EARLY_BIG = ("a_w_in",)
NATURAL_BIG = "ffn_w_up"
SMALL_SHARDED = (("a_mu", 1), ("a_w0", 1), ("a_w2", 2), ("a_a0", 1), ("a_a2", 2), ("a_g2", 2), ("a_k_k", 1),
                 ("a_k_a", 1), ("a_lnx_w", 1), ("a_lnx_b", 1), ("ffn_conv_w", 2))
SMALL_REPL = ("attn_norm", "a_r_k", "kv_norm", "kv_k_norm", "b_q_norm", "mem_norm", "mem_q_norm", "mem_k_norm",
              "ffn_norm", "ffn_conv_b")
WEIGHTS = ("attn_norm", "a_w_in", "a_mu", "a_w0", "a_w2", "a_a0", "a_a2", "a_g2", "a_k_k", "a_k_a", "a_r_k",
           "a_lnx_w", "a_lnx_b", "a_w_out", "kv_norm", "kv_w", "kv_k_norm", "b_w_in", "b_q_norm", "b_w_out",
           "mem_norm", "mem_w_kv", "mem_q_norm", "mem_k_norm", "ffn_norm", "ffn_w_up", "ffn_conv_w", "ffn_conv_b",
           "ffn_w_down")


def _cp(sem=None, **kw):
    return pltpu.CompilerParams(dimension_semantics=sem, vmem_limit_bytes=VMEM_LIMIT_MB << 20, **kw)


def _tile(n, cands=(512, 256, 128)):
    for c in cands:
        if n % c == 0:
            return c
    return n


def _mm(a, b, *, name, ta=False, tb=False, add=None, out_dtype=F32):
    K, M = a.shape if ta else a.shape[::-1]
    N = b.shape[0] if tb else b.shape[1]
    assert K == (b.shape[1] if tb else b.shape[0])
    tm, tn = _tile(M, (512, 256, 128) if ta else (1024, 512, 256, 128)), _tile(N, (512, 1408, 256, 128))
    bytes_of = lambda z: z.size * z.dtype.itemsize
    kept = 1 if bytes_of(b) + bytes_of(a) * (N // tn) < bytes_of(a) + bytes_of(b) * (M // tm) else 0
    mi, ni = ((lambda o, i: i), (lambda o, i: o)) if kept else ((lambda o, i: o), (lambda o, i: i))
    grid = (N // tn, M // tm) if kept else (M // tm, N // tn)
    a_blk, b_blk = ((K, tm) if ta else (tm, K)), ((tn, K) if tb else (K, tn))
    a_spec = pl.BlockSpec(a_blk, (lambda o, i: (0, mi(o, i))) if ta else (lambda o, i: (mi(o, i), 0)))
    b_spec = pl.BlockSpec(b_blk, (lambda o, i: (ni(o, i), 0)) if tb else (lambda o, i: (0, ni(o, i))))
    o_spec = pl.BlockSpec((tm, tn), lambda o, i: (mi(o, i), ni(o, i)))
    dn = (((0,) if ta else (1,), (1,) if tb else (0,)), ((), ()))
    has_add = add is not None
    cache = (b if kept else a).dtype != MMD

    def body(*refs):
        vals = [refs[0], refs[1]]
        o_ref = refs[2 + has_add]
        if cache:
            scr = refs[-1]

            @pl.when(pl.program_id(1) == 0)
            def _():
                scr[...] = vals[kept][...].astype(MMD)

            vals[kept] = scr
        acc = lax.dot_general(vals[0][...].astype(MMD), vals[1][...].astype(MMD), dn, preferred_element_type=F32)
        if has_add:
            acc = acc + refs[2][...]
        o_ref[...] = acc.astype(o_ref.dtype)

    ins = [a, b] + ([add] if has_add else [])
    specs = [a_spec, b_spec] + ([o_spec] if has_add else [])
    return pl.pallas_call(
        body, name=name, grid=grid, in_specs=specs, out_specs=o_spec,
        out_shape=jax.ShapeDtypeStruct((M, N), out_dtype),
        scratch_shapes=[pltpu.VMEM(b_blk if kept else a_blk, MMD)] if cache else [],
        compiler_params=_cp(("parallel", "arbitrary")),
    )(*ins)


def _rowmap(fn, *, name, T, tb, ins, outs, accs=()):
    nblk = T // tb
    assert T % tb == 0 and tb % 8 == 0
    in_specs, args = [], []
    for spec in ins:
        kind, arr = spec[0], spec[1]
        w, cb = (spec[2], spec[3]) if len(spec) > 2 else (arr.shape[-1], 0)
        if kind == "row":
            in_specs.append(pl.BlockSpec((tb, w), lambda i, cb=cb: (i, cb)))
        elif kind == "prev":
            in_specs.append(pl.BlockSpec((8, w), lambda i, cb=cb: (jnp.maximum(i * (tb // 8) - 1, 0), cb)))
        elif kind == "next":
            in_specs.append(pl.BlockSpec((8, w), lambda i, cb=cb: (jnp.minimum((i + 1) * (tb // 8), T // 8 - 1), cb)))
        elif kind == "const":
            in_specs.append(pl.BlockSpec(arr.shape, lambda i, nd=arr.ndim: (0,) * nd))
        else:
            raise ValueError(kind)
        args.append(arr)
    out_shape, out_specs = [], []
    for kind, w, dt in outs:
        out_shape.append(jax.ShapeDtypeStruct((T, w), dt))
        out_specs.append(pl.BlockSpec((tb, w), lambda i: (i, 0)))
    for shp, dt in accs:
        out_shape.append(jax.ShapeDtypeStruct(shp, dt))
        out_specs.append(pl.BlockSpec(shp, lambda i, nd=len(shp): (0,) * nd))
    n_in, n_out = len(ins), len(outs)

    def body(*refs):
        i = pl.program_id(0)
        vals = [r[...] for r in refs[:n_in]]
        res = fn(i, nblk, *vals)
        if not isinstance(res, (tuple, list)):
            res = (res,)
        assert len(res) == n_out + len(accs), (name, len(res))
        for r, v in zip(refs[n_in:n_in + n_out], res[:n_out]):
            r[...] = v.astype(r.dtype)
        acc_refs = refs[n_in + n_out:]
        if acc_refs:
            @pl.when(i == 0)
            def _():
                for r in acc_refs:
                    r[...] = jnp.zeros(r.shape, r.dtype)

            for r, v in zip(acc_refs, res[n_out:]):
                r[...] += v

    res = pl.pallas_call(
        body, name=name, grid=(nblk,), in_specs=in_specs, out_specs=out_specs, out_shape=out_shape,
        compiler_params=_cp(("arbitrary",)),
    )(*args)
    return res


def _row_pick(halo, r):
    rid = lax.broadcasted_iota(jnp.int32, halo.shape, 0)
    return jnp.sum(jnp.where(rid == r, halo, 0.0), axis=0, keepdims=True)


def _shift_down(x, row_before, is_first):
    rid = lax.broadcasted_iota(jnp.int32, x.shape, 0)
    first = jnp.where(is_first, 0.0, 1.0) * row_before
    return jnp.where(rid == 0, first, pltpu.roll(x, 1, axis=0))


def _shift_up(x, row_after, is_last):
    n = x.shape[0]
    rid = lax.broadcasted_iota(jnp.int32, x.shape, 0)
    last = jnp.where(is_last, 0.0, 1.0) * row_after
    return jnp.where(rid == n - 1, last, pltpu.roll(x, n - 1, axis=0))


def _dot(a, b, dn=(((1,), (0,)), ((), ()))):
    return lax.dot_general(a.astype(MMD), b.astype(MMD), dn, preferred_element_type=F32)


def _dot_nt(a, b):
    return _dot(a, b, (((1,), (1,)), ((), ())))


def _dot_tn(a, b):
    return _dot(a, b, (((0,), (0,)), ((), ())))


def _dot_exact01(x, g01):
    hi = x.astype(BF16)
    lo = (x - hi.astype(F32)).astype(BF16)
    dn = (((1,), (0,)), ((), ()))
    return (lax.dot_general(hi, g01, dn, preferred_element_type=F32)
            + lax.dot_general(lo, g01, dn, preferred_element_type=F32))


def _fold_heads(v, fold):
    return _row_pick(_dot_exact01(jnp.broadcast_to(v, (8, v.shape[1])), fold), 0)


def _fold_ones(width):
    idx = np.arange(width) % HEAD_DIM
    return jnp.asarray((idx[:, None] == np.arange(HEAD_DIM)[None, :]).astype(np.float32), BF16)


def _head_masks(width):
    idx = np.arange(width) // HEAD_DIM
    return jnp.asarray((idx[None, :] == np.arange(width // HEAD_DIM)[:, None]).astype(np.float32)[:, None, :], F32)


def _rms_stats(x):
    r = lax.rsqrt(jnp.mean(x * x, axis=-1, keepdims=True) + RMS_EPS)
    return r, x * r


def rms_fwd(x, gains, *, name):
    T, D = x.shape

    def fn(i, nblk, xb, *gs):
        _, xh = _rms_stats(xb)
        return tuple(xh * g for g in gs)

    return _rowmap(fn, name=name, T=T, tb=512, ins=[("row", x)] + [("const", g) for g in gains],
                   outs=[("row", D, MMD)] * len(gains))


def rms_bwd(x, gains, dhs, dres, *, name):
    T, D = x.shape
    n = len(gains)

    def fn(i, nblk, xb, dr, *rest):
        gs, ds = rest[:n], rest[n:]
        r, xh = _rms_stats(xb)
        dx = dr
        dgs = []
        for g, dh in zip(gs, ds):
            dgs.append(jnp.sum(dh * xh, axis=0, keepdims=True))
            dxh = dh * g
            dx = dx + r * (dxh - xh * jnp.mean(dxh * xh, axis=-1, keepdims=True))
        return (dx, *dgs)

    return _rowmap(fn, name=name, T=T, tb=512,
                   ins=[("row", x), ("row", dres)] + [("const", g) for g in gains] + [("row", d) for d in dhs],
                   outs=[("row", D, F32)], accs=[((1, D), F32)] * n)


def _segsum(x):
    first = lax.broadcasted_iota(jnp.int32, (x.shape[0], 128), 1) < HEAD_DIM
    outs = []
    for p in range(x.shape[1] // 128):
        xs = x[:, p * 128:(p + 1) * 128]
        lo = jnp.sum(jnp.where(first, xs, 0.0), axis=-1, keepdims=True)
        hi = jnp.sum(jnp.where(first, 0.0, xs), axis=-1, keepdims=True)
        outs.append(jnp.where(first, lo, hi))
    return jnp.concatenate(outs, axis=1)


def _pre1_common(i, ps, halo, mu, w0, a0, w2p, a2p, g2p, k_k, k_a):
    prev = _shift_down(ps, _row_pick(halo, 7), i == 0)
    xs = ps + (prev - ps) * mu
    lo = xs[:, 3 * RWKV_WIDTH:]
    tl, sl = jnp.tanh(lo), jax.nn.sigmoid(lo)
    dec = w0 + _dot(tl, w2p)
    ain = a0 + _dot(lo, a2p)
    g = _dot(sl, g2p)
    wl = -jax.nn.softplus(-dec) - 0.5
    w = jnp.exp(-jnp.exp(wl))
    a = jax.nn.sigmoid(ain)
    k = xs[:, RWKV_WIDTH:2 * RWKV_WIDTH]
    z = k * k_k
    nrm = jnp.sqrt(_segsum(z * z))
    kk = z / jnp.maximum(nrm, 1e-12)
    return prev, xs, lo, tl, sl, dec, wl, w, a, g, k, nrm, kk


def rwkv_pre_fwd(p, mu, w0, a0, w2p, a2p, g2p, k_k, k_a):
    T = p.shape[0]

    def fn(i, nblk, ps, halo, mu, w0, a0, w2p, a2p, g2p, k_k, k_a):
        _, xs, _, _, _, _, _, w, a, g, k, _, kk = _pre1_common(i, ps, halo, mu, w0, a0, w2p, a2p, g2p, k_k, k_a)
        W = RWKV_WIDTH
        return xs[:, :W], w, k * (1.0 + (a - 1.0) * k_a), xs[:, 2 * W:3 * W], kk, kk * a, g

    return _rowmap(fn, name="rwkv_pre_fwd", T=T, tb=256,
                   ins=[("row", p, SHIFT_WIDTH, 0), ("prev", p, SHIFT_WIDTH, 0)]
                   + [("const", c) for c in (mu, w0, a0, w2p, a2p, g2p, k_k, k_a)],
                   outs=[("row", RWKV_WIDTH, F32)] * 7)


def rwkv_pre_bwd(p, mu, w0, a0, w2p, a2p, g2p, k_k, k_a, drs, dw, dk2s, dvs, dkk, dkka, dg):
    T = p.shape[0]

    def fn(i, nblk, ps, halo, mu, w0, a0, w2p, a2p, g2p, k_k, k_a, dr0, dr1, dw, dk20, dk21, dv0, dv1, dkk, dkka, dg):
        prev, xs, lo, tl, sl, dec, wl, w, a, g, k, nrm, kk = _pre1_common(i, ps, halo, mu, w0, a0, w2p, a2p, g2p, k_k, k_a)
        dk2 = dk20 + dk21
        dkk_t = dkk + dkka * a
        proj = jnp.where(nrm > 1e-12, kk * _segsum(dkk_t * kk), 0.0)
        dz = (dkk_t - proj) / jnp.maximum(nrm, 1e-12)
        dk = dz * k_k + dk2 * (1.0 + (a - 1.0) * k_a)
        da = dkka * kk + dk2 * k * k_a
        ddec = dw * (-w * jnp.exp(wl)) * jax.nn.sigmoid(-dec)
        dain = da * a * (1.0 - a)
        dlo = (_dot_nt(ddec, w2p) * (1.0 - tl * tl) + _dot_nt(dain, a2p) + _dot_nt(dg, g2p) * sl * (1.0 - sl))
        dxs = jnp.concatenate([dr0 + dr1, dk, dv0 + dv1, dlo], axis=1)
        s = lambda z: jnp.sum(z, axis=0, keepdims=True)
        return (dxs, s(dxs * (prev - ps)), s(ddec), s(dain), _dot_tn(tl, ddec), _dot_tn(lo, dain), _dot_tn(sl, dg),
                s(dz * k), s(dk2 * k * (a - 1.0)))

    return _rowmap(fn, name="rwkv_pre_bwd", T=T, tb=128,
                   ins=[("row", p, SHIFT_WIDTH, 0), ("prev", p, SHIFT_WIDTH, 0)]
                   + [("const", c) for c in (mu, w0, a0, w2p, a2p, g2p, k_k, k_a)]
                   + [("row", c) for c in (*drs, dw, *dk2s, *dvs, dkk, dkka, dg)],
                   outs=[("row", SHIFT_WIDTH, F32)],
                   accs=[((1, SHIFT_WIDTH), F32), ((1, RWKV_WIDTH), F32), ((1, RWKV_WIDTH), F32)]
                   + [((LORA_WIDTH, RWKV_WIDTH), F32)] * 3 + [((1, RWKV_WIDTH), F32)] * 2)


def shift_bwd(dxs, mu, dq_mem):
    T = dxs.shape[0]

    def fn(i, nblk, d, halo, mu, dq):
        nxt = _shift_up(d, _row_pick(halo, 0), i == nblk - 1)
        return jnp.concatenate([d * (1.0 - mu) + nxt * mu, dq], axis=1)

    return _rowmap(fn, name="shift_bwd", T=T, tb=256,
                   ins=[("row", dxs), ("next", dxs), ("const", mu), ("row", dq_mem)],
                   outs=[("row", SHIFT_WIDTH + MEM_WIDTH, MMD)])[0]


N_PAIRS = RWKV_HEADS // 2


def _pair_consts():
    row = lax.broadcasted_iota(jnp.int32, (HEAD_DIM, 128), 0)
    lane = lax.broadcasted_iota(jnp.int32, (HEAD_DIM, 128), 1)
    eye2 = jnp.logical_or(lane == row, lane == row + HEAD_DIM).astype(F32)
    li = lax.broadcasted_iota(jnp.int32, (128, 128), 0) < HEAD_DIM
    lj = lax.broadcasted_iota(jnp.int32, (128, 128), 1) < HEAD_DIM
    return eye2, (li == lj).astype(BF16)


def _pair_sum(p, ones2):
    n, m, l = p.shape
    s = lax.dot_general(p.reshape(n * m, l).astype(BF16), ones2, (((1,), (0,)), ((), ())), preferred_element_type=F32)
    return s.reshape(n, m, l)


def _pair_rows(row):
    return jnp.stack([row[:, p * 128:(p + 1) * 128] for p in range(N_PAIRS)], axis=0)


def _pair_flat(rows):
    return jnp.concatenate([rows[p] for p in range(N_PAIRS)], axis=1)


def _split_bf16(v):
    hi = v.astype(BF16).astype(F32)
    return hi, v - hi


def _gather_copies(srcs, dsts, send_sems, recv_sems):
    x, y, c = _mesh_pos()
    s = 2 * x + y
    me, sibling = (x, y, c), (x, y, 1 - c)
    rc = functools.partial(_remote, send_sems, recv_sems)
    ici, land, fwd, arrived = [], [], [], []
    for b, (src, dst) in enumerate(zip(srcs, dsts)):
        for j, (cx, cy) in enumerate(_other_chips(x, y)):
            k = 6 * b + j
            ici.append(rc(k, src.at[c], dst.at[s, c], (cx, cy, c)))
            blk, blk2 = dst.at[2 * cx + cy, c], dst.at[2 * cx + cy, 1 - c]
            land.append(rc(k, blk, blk, me))
            fwd.append(rc(k + 3, blk, blk, sibling))
            arrived.append(rc(k + 3, blk2, blk2, me))
    return ici, land, fwd, arrived


def scan_fwd(r, w, k2, v, kk, kka, gather=None):
    T, W = r.shape
    tc = SCAN_CHUNK
    nchunk = T // tc
    seq = pl.BlockSpec((tc, W), lambda i: (i, 0))
    one_state = pl.BlockSpec((N_PAIRS, HEAD_DIM, 128), lambda i: (0, 0, 0))
    nb = 0 if gather is None else len(gather)

    def body(r_ref, w_ref, k2_ref, v_ref, kk_ref, kka_ref, *rest):
        if gather is None:
            y_ref, st_ref, fin_ref, s_scr, vhi_scr, vlo_scr = rest
        else:
            srcs, (y_ref, st_ref, fin_ref), dsts = rest[:nb], rest[nb:nb + 3], rest[nb + 3:2 * nb + 3]
            s_scr, vhi_scr, vlo_scr, send_sems, recv_sems = rest[2 * nb + 3:]
            ici, land, fwd, arrived = _gather_copies(srcs, dsts, send_sems, recv_sems)

            @pl.when(pl.program_id(0) == 0)
            def _():
                for cp in ici:
                    cp.start()

            @pl.when(pl.program_id(0) == nchunk // 2)
            def _():
                for a, f in zip(land, fwd):
                    a.wait_recv()
                    f.start()

        @pl.when(pl.program_id(0) == 0)
        def _():
            s_scr[...] = jnp.zeros(s_scr.shape, F32)

        vhi_scr[...], vlo_scr[...] = _split_bf16(v_ref[...])
        eye2, ones2 = _pair_consts()
        eye2b = eye2.astype(BF16)

        def step(t, carry):
            r_t, w_t, k2_t, kk_t, kka_t, vhi_t, vlo_t = (
                _pair_rows(ref[pl.ds(t, 1), :]) for ref in (r_ref, w_ref, k2_ref, kk_ref, kka_ref, vhi_scr, vlo_scr))
            S = s_scr[...]
            skk = _pair_sum(S * kk_t, ones2)
            vb = _pair_sum(eye2b * vhi_t.astype(BF16), ones2) + _pair_sum(eye2b * vlo_t.astype(BF16), ones2)
            S2 = S * w_t - skk * kka_t + vb * k2_t
            y_ref[pl.ds(t, 1), :] = _pair_flat(jnp.sum(eye2 * _pair_sum(S2 * r_t, ones2), axis=1, keepdims=True))
            s_scr[...] = S2
            st_ref[t] = S
            return carry

        lax.fori_loop(0, tc, step, 0, unroll=SCAN_UNROLL)
        fin_ref[...] = s_scr[...]

        if gather is not None:
            @pl.when(pl.program_id(0) == nchunk - 1)
            def _():
                for a in arrived:
                    a.wait_recv()
                for cp in ici + fwd:
                    cp.wait_send()

    in_specs = [seq] * 6
    out_specs = [seq, pl.BlockSpec((tc, N_PAIRS, HEAD_DIM, 128), lambda i: (i, 0, 0, 0)), one_state]
    out_shape = [jax.ShapeDtypeStruct((T, W), F32), jax.ShapeDtypeStruct((T, N_PAIRS, HEAD_DIM, 128), F32),
                 jax.ShapeDtypeStruct((N_PAIRS, HEAD_DIM, 128), F32)]
    scratch = [pltpu.VMEM((N_PAIRS, HEAD_DIM, 128), F32), pltpu.VMEM((tc, W), F32), pltpu.VMEM((tc, W), F32)]
    args = [r, w, k2, v, kk, kka]
    if gather is not None:
        in_specs += [HBM_SPEC] * nb
        out_specs += [HBM_SPEC] * nb
        out_shape += [jax.ShapeDtypeStruct((N_CHIPS, *g.shape), g.dtype) for g in gather]
        scratch += [pltpu.SemaphoreType.DMA((6 * nb,)), pltpu.SemaphoreType.DMA((6 * nb,))]
        args += list(gather)
    return pl.pallas_call(
        body, name="rwkv_scan_fwd", grid=(nchunk,), in_specs=in_specs, out_specs=out_specs, out_shape=out_shape,
        scratch_shapes=scratch, compiler_params=_cp(("arbitrary",)),
    )(*args)


PEER_FLIPS = tuple((fx, fy, fc) for fx in (0, 1) for fy in (0, 1) for fc in (0, 1))[1:]


def scan_bwd(r, w, k2, v, kk, kka, states, final_state, dy, scatter=None):
    T, W = r.shape
    tc = SCAN_CHUNK
    nchunk = T // tc
    seq = pl.BlockSpec((tc, W), lambda i: (nchunk - 1 - i, 0))
    st_spec = pl.BlockSpec((tc, N_PAIRS, HEAD_DIM, 128), lambda i: (nchunk - 1 - i, 0, 0, 0))
    one_state = pl.BlockSpec((N_PAIRS, HEAD_DIM, 128), lambda i: (0, 0, 0))
    nb, npeer = (0 if scatter is None else len(scatter)), len(PEER_FLIPS)

    def body(r_ref, w_ref, k2_ref, v_ref, kk_ref, kka_ref, st_ref, fin_ref, dy_ref, *rest):
        if scatter is None:
            dr_ref, dw_ref, dk2_ref, dv_ref, dkk_ref, dkka_ref, ds_scr, sc_scr, vhi_scr, vlo_scr = rest
        else:
            srcs, dsts = rest[:nb], rest[nb + 6:2 * nb + 6]
            dr_ref, dw_ref, dk2_ref, dv_ref, dkk_ref, dkka_ref = rest[nb:nb + 6]
            ds_scr, sc_scr, vhi_scr, vlo_scr, send_sems, recv_sems = rest[2 * nb + 6:]
            x, y, c = _mesh_pos()
            copies = []
            for b, (src, dst) in enumerate(zip(srcs, dsts)):
                for k, (fx, fy, fc) in enumerate(PEER_FLIPS):
                    px, py, pc = (1 - x if fx else x), (1 - y if fy else y), (1 - c if fc else c)
                    copies.append(_remote(send_sems, recv_sems, npeer * b + k, src.at[2 * px + py, pc], dst.at[k],
                                          (px, py, pc)))

            @pl.when(pl.program_id(0) == 0)
            def _():
                for cp in copies:
                    cp.start()

        @pl.when(pl.program_id(0) == 0)
        def _():
            ds_scr[...] = jnp.zeros(ds_scr.shape, F32)
            sc_scr[...] = fin_ref[...]

        vhi_scr[...], vlo_scr[...] = _split_bf16(v_ref[...])
        eye2, ones2 = _pair_consts()
        eye2b = eye2.astype(BF16)
        colsum = lambda z: jnp.sum(z, axis=1, keepdims=True)

        def step(j, carry):
            t = tc - 1 - j
            r_t, w_t, k2_t, kk_t, kka_t, vhi_t, vlo_t, dy_t = (
                _pair_rows(ref[pl.ds(t, 1), :])
                for ref in (r_ref, w_ref, k2_ref, kk_ref, kka_ref, vhi_scr, vlo_scr, dy_ref))
            s_prev, s_cur = st_ref[t], sc_scr[...]
            dyb = _pair_sum(eye2b * dy_t.astype(BF16), ones2)
            vb = _pair_sum(eye2b * vhi_t.astype(BF16), ones2) + _pair_sum(eye2b * vlo_t.astype(BF16), ones2)
            skk = _pair_sum(s_prev * kk_t, ones2)
            dS = ds_scr[...] + dyb * r_t
            dS_b = dS.astype(BF16)
            dsa = _pair_sum(dS_b * kka_t.astype(BF16), ones2)
            ds_scr[...] = dS * w_t - dsa * kk_t
            sc_scr[...] = s_prev
            for ref, sign, val in zip((dr_ref, dw_ref, dk2_ref, dv_ref, dkk_ref, dkka_ref), (1, 1, 1, 1, -1, -1),
                                      (s_cur * dyb, dS * s_prev, dS * vb,
                                       eye2 * _pair_sum(dS_b * k2_t.astype(BF16), ones2), s_prev * dsa, dS * skk)):
                ref[pl.ds(t, 1), :] = _pair_flat(colsum(val) if sign > 0 else -colsum(val))
            return carry

        lax.fori_loop(0, tc, step, 0, unroll=SCAN_UNROLL_BWD)

        if scatter is not None:
            @pl.when(pl.program_id(0) == nchunk - 1)
            def _():
                for cp in copies:
                    cp.wait()

    in_specs = [seq] * 6 + [st_spec, one_state, seq]
    out_specs = [seq] * 6
    out_shape = [jax.ShapeDtypeStruct((T, W), F32)] * 6
    scratch = [pltpu.VMEM((N_PAIRS, HEAD_DIM, 128), F32)] * 2 + [pltpu.VMEM((tc, W), F32)] * 2
    args = [r, w, k2, v, kk, kka, states, final_state, dy]
    if scatter is not None:
        in_specs += [HBM_SPEC] * nb
        out_specs += [HBM_SPEC] * nb
        out_shape += [jax.ShapeDtypeStruct((npeer, *s.shape[2:]), s.dtype) for s in scatter]
        scratch += [pltpu.SemaphoreType.DMA((npeer * nb,)), pltpu.SemaphoreType.DMA((npeer * nb,))]
        args += list(scatter)
    return pl.pallas_call(
        body, name="rwkv_scan_bwd", grid=(nchunk,), in_specs=in_specs, out_specs=out_specs, out_shape=out_shape,
        scratch_shapes=scratch, compiler_params=_cp(("arbitrary",)),
    )(*args)


def _mix_common(y, r, k2, v, lnx_w, lnx_b, r_k):
    yc = y - _segsum(y) * (1.0 / HEAD_DIM)
    rstd = lax.rsqrt(_segsum(yc * yc) * (1.0 / HEAD_DIM) + LNX_EPS)
    yhat = yc * rstd
    s = _segsum(r * k2 * r_k)
    return rstd, yhat, s, yhat * lnx_w + lnx_b + s * v


def mix_gate_fwd(y, r, k2, v, g, y_mem, lnx_w, lnx_b, r_k):
    T = y.shape[0]

    def fn(i, nblk, y, r, k2, v, g, ym, lw, lb, rk):
        mix = _mix_common(y, r, k2, v, lw, lb, rk)[3]
        return jnp.concatenate([mix * g, ym], axis=1)

    return _rowmap(fn, name="mix_gate_fwd", T=T, tb=256,
                   ins=[("row", z) for z in (y, r, k2, v, g, y_mem)] + [("const", c) for c in (lnx_w, lnx_b, r_k)],
                   outs=[("row", RWKV_WIDTH + MEM_WIDTH, MMD)])[0]


def mix_gate_bwd(y, r, k2, v, g, dycat, lnx_w, lnx_b, r_k):
    T = y.shape[0]

    def fn(i, nblk, y, r, k2, v, g, dyc, lw, lb, rk):
        rstd, yhat, s, mix = _mix_common(y, r, k2, v, lw, lb, rk)
        dmix = dyc * g
        dyh = dmix * lw
        inv = 1.0 / HEAD_DIM
        dy = rstd * (dyh - _segsum(dyh) * inv - yhat * (_segsum(dyh * yhat) * inv))
        ds = _segsum(dmix * v)
        cs = lambda z: jnp.sum(z, axis=0, keepdims=True)
        return (dy, ds * k2 * rk, ds * r * rk, dmix * s, dyc * mix, cs(dmix * yhat), cs(dmix), cs(ds * r * k2))

    return _rowmap(fn, name="mix_gate_bwd", T=T, tb=256,
                   ins=[("row", z) for z in (y, r, k2, v, g)] + [("row", dycat, RWKV_WIDTH, 0)]
                   + [("const", c) for c in (lnx_w, lnx_b, r_k)],
                   outs=[("row", RWKV_WIDTH, F32)] * 5, accs=[((1, RWKV_WIDTH), F32)] * 3)


def _head_rms(x):
    ms = _segsum(x * x) * (1.0 / HEAD_DIM)
    r = lax.rsqrt(ms + RMS_EPS)
    return r, x * r


def _head_rms_bwd(dxn_g, r, xh):
    return r * (dxn_g - xh * (_segsum(dxn_g * xh) * (1.0 / HEAD_DIM)))


def mem_kv_fwd(mem, norm_g, w_kv, k_norm_t, *, name):
    def body(mem_ref, g_ref, w_ref, kn_ref, k_out, v_out):
        _, xh = _rms_stats(mem_ref[...])
        kv = _dot(xh * g_ref[...], w_ref[...])
        _, kh = _head_rms(kv[:, :MEM_WIDTH])
        k_out[...] = kh * kn_ref[...]
        v_out[...] = kv[:, MEM_WIDTH:]

    return pl.pallas_call(
        body, name=name, out_shape=[jax.ShapeDtypeStruct((N_MEM, MEM_WIDTH), F32)] * 2, compiler_params=_cp(),
    )(mem, norm_g, w_kv, k_norm_t)


def mem_kv_bwd(mem, norm_g, w_kv, k_norm_t, dkn, dv, *, name):
    fold = _fold_ones(MEM_WIDTH)

    def body(mem_ref, g_ref, w_ref, kn_ref, fo_ref, dkn_ref, dv_ref, dw_out, dg_out, dkg_out):
        _, xh = _rms_stats(mem_ref[...])
        hm = xh * g_ref[...]
        kv = _dot(hm, w_ref[...])
        r, kh = _head_rms(kv[:, :MEM_WIDTH])
        dkn = dkn_ref[...]
        dkg_out[...] = _fold_heads(jnp.sum(dkn * kh, axis=0, keepdims=True), fo_ref[...])
        dkraw = _head_rms_bwd(dkn * kn_ref[...], r, kh)
        dkv = jnp.concatenate([dkraw, dv_ref[...]], axis=1)
        dw_out[...] = _dot_tn(hm, dkv)
        dg_out[...] = jnp.sum(_dot_nt(dkv, w_ref[...]) * xh, axis=0, keepdims=True)

    return pl.pallas_call(
        body, name=name,
        out_shape=[jax.ShapeDtypeStruct((D_MODEL, 2 * MEM_WIDTH), F32), jax.ShapeDtypeStruct((1, D_MODEL), F32),
                   jax.ShapeDtypeStruct((1, HEAD_DIM), F32)],
        compiler_params=_cp(),
    )(mem, norm_g, w_kv, k_norm_t, fold, dkn, dv)


def _mem_scores(qn, kn, masks, h):
    s = _dot_nt(qn * masks[h], kn) * (1.0 / math.sqrt(HEAD_DIM))
    s = s - jnp.max(s, axis=-1, keepdims=True)
    e = jnp.exp(s)
    return e / jnp.sum(e, axis=-1, keepdims=True)


def mem_attn_fwd(p, colblock, kn, v, q_norm_t, *, name):
    T = p.shape[0]
    masks = _head_masks(MEM_WIDTH)

    def fn(i, nblk, q, kn, v, qg, masks):
        _, qh = _head_rms(q)
        qn = qh * qg
        out = jnp.zeros(q.shape, F32)
        for h in range(MEM_WIDTH // HEAD_DIM):
            out = out + _dot(_mem_scores(qn, kn, masks, h), v * masks[h])
        return out

    return _rowmap(fn, name=name, T=T, tb=512,
                   ins=[("row", p, MEM_WIDTH, colblock)] + [("const", c) for c in (kn, v, q_norm_t, masks)],
                   outs=[("row", MEM_WIDTH, F32)])[0]


def mem_attn_bwd(p, colblock, kn, v, q_norm_t, dycat, dcolblock, *, name):
    T = p.shape[0]
    masks, fold = _head_masks(MEM_WIDTH), _fold_ones(MEM_WIDTH)
    scale = 1.0 / math.sqrt(HEAD_DIM)

    def fn(i, nblk, q, dy, kn, v, qg, masks, fo):
        r, qh = _head_rms(q)
        qn = qh * qg
        dqn = jnp.zeros(q.shape, F32)
        dkn = jnp.zeros(kn.shape, F32)
        dv = jnp.zeros(v.shape, F32)
        for h in range(MEM_WIDTH // HEAD_DIM):
            pr = _mem_scores(qn, kn, masks, h)
            dyh = dy * masks[h]
            dpr = _dot_nt(dyh, v)
            ds = pr * (dpr - jnp.sum(dpr * pr, axis=-1, keepdims=True)) * scale
            dqn = dqn + _dot(ds, kn * masks[h])
            dkn = dkn + _dot_tn(ds, qn * masks[h])
            dv = dv + _dot_tn(pr, dyh)
        dqg = _fold_heads(jnp.sum(dqn * qh, axis=0, keepdims=True), fo)
        return _head_rms_bwd(dqn * qg, r, qh), dkn, dv, dqg

    return _rowmap(fn, name=name, T=T, tb=512,
                   ins=[("row", p, MEM_WIDTH, colblock), ("row", dycat, MEM_WIDTH, dcolblock)]
                   + [("const", c) for c in (kn, v, q_norm_t, masks, fold)],
                   outs=[("row", MEM_WIDTH, F32)],
                   accs=[((N_MEM, MEM_WIDTH), F32), ((N_MEM, MEM_WIDTH), F32), ((1, HEAD_DIM), F32)])


def _ffn_conv(i, u, halo, cw, cb):
    ext = jnp.concatenate([jnp.where(i == 0, 0.0, 1.0) * halo, u], axis=0)
    up1 = pltpu.roll(ext, 1, axis=0)[8:]
    up2 = pltpu.roll(ext, 2, axis=0)[8:]
    c = cb + cw[0] * up2 + cw[1] * up1 + cw[2] * u
    return up1, up2, c[:, :D_FF], c[:, D_FF:]


def ffn_act_fwd(u, cw, cb, *, name):
    T = u.shape[0]

    def fn(i, nblk, u, halo, c0, c1, c2, cb):
        _, _, gate, val = _ffn_conv(i, u, halo, (c0, c1, c2), cb)
        return jax.nn.silu(gate) * val

    return _rowmap(fn, name=name, T=T, tb=128, ins=[("row", u), ("prev", u)] + [("const", c) for c in (*cw, cb)],
                   outs=[("row", D_FF, MMD)])[0]


def ffn_act_bwd(u, cw, cb, dz, *, name):
    T = u.shape[0]
    tb = 128

    def fn(i, nblk, u, halo, unext, c0, c1, c2, cb, dz, dznext):
        ue = jnp.concatenate([u, unext], axis=0)
        dze = jnp.concatenate([dz, jnp.where(i == nblk - 1, 0.0, 1.0) * dznext], axis=0)
        up1, up2, gate, val = _ffn_conv(i, ue, halo, (c0, c1, c2), cb)
        sg = jax.nn.sigmoid(gate)
        dce = jnp.concatenate([dze * val * sg * (1.0 + gate * (1.0 - sg)), dze * gate * sg], axis=1)
        rows = dce.shape[0]
        du = (c2 * dce + c1 * pltpu.roll(dce, rows - 1, axis=0) + c0 * pltpu.roll(dce, rows - 2, axis=0))[:tb]
        dc = dce[:tb]
        s = lambda z: jnp.sum(z, axis=0, keepdims=True)
        return du, s(dc * up2[:tb]), s(dc * up1[:tb]), s(dc * u), s(dc)

    return _rowmap(fn, name=name, T=T, tb=tb,
                   ins=[("row", u), ("prev", u), ("next", u)] + [("const", c) for c in (*cw, cb)]
                   + [("row", dz), ("next", dz)],
                   outs=[("row", 2 * D_FF, MMD)], accs=[((1, 2 * D_FF), F32)] * 4)


def _rope_swap(z):
    lane = lax.broadcasted_iota(jnp.int32, z.shape, 1) % HEAD_DIM
    w = z.shape[1]
    return jnp.where(lane < HEAD_DIM // 2, pltpu.roll(z, w - HEAD_DIM // 2, axis=1), pltpu.roll(z, HEAD_DIM // 2, axis=1))


def rope_tables(T):
    inv = (np.float32(ROPE_THETA) ** (-np.arange(0, HEAD_DIM, 2, dtype=np.float32) / np.float32(HEAD_DIM))).astype(np.float32)
    ang = (np.arange(T, dtype=np.float32)[:, None] * inv[None, :]).astype(np.float64)
    cos, sin = np.cos(ang).astype(np.float32), np.sin(ang).astype(np.float32)
    return (jnp.asarray(np.concatenate([cos, cos, cos, cos], axis=1)),
            jnp.asarray(np.concatenate([-sin, sin, -sin, sin], axis=1)))


def _rope_wide(t):
    return jnp.tile(t, (1, DIL_WIDTH // t.shape[1]))


def qk_fwd(kvp, pb, kg_t, qg_t, cos, sin):
    T = kvp.shape[0]

    def fn(i, nblk, kraw, vraw, qraw, kg, qg, c, s):
        c, s = _rope_wide(c), _rope_wide(s)
        outs = []
        for raw, g in ((qraw, qg), (kraw, kg)):
            _, xh = _head_rms(raw)
            z = xh * g
            outs.append(z * c + _rope_swap(z) * s)
        return outs[0], outs[1], vraw

    return _rowmap(fn, name="qk_fwd", T=T, tb=256,
                   ins=[("row", kvp, DIL_WIDTH, 0), ("row", kvp, DIL_WIDTH, 1), ("row", pb, DIL_WIDTH, 0)]
                   + [("const", kg_t), ("const", qg_t), ("row", cos), ("row", sin)],
                   outs=[("row", DIL_WIDTH, MMD)] * 3)


def qk_bwd(kvp, pb, kg_t, qg_t, cos, sin, dq, dk, dv, dq_mem):
    T = kvp.shape[0]
    fold = _fold_ones(DIL_WIDTH)

    def fn(i, nblk, kraw, qraw, kg, qg, c, s, fo, dq, dk, dv, dqm):
        c, s = _rope_wide(c), _rope_wide(s)
        res, dgs = [], []
        for raw, g, d in ((qraw, qg, dq), (kraw, kg, dk)):
            r, xh = _head_rms(raw)
            dz = d * c + _rope_swap(d * s)
            dgs.append(_fold_heads(jnp.sum(dz * xh, axis=0, keepdims=True), fo))
            res.append(_head_rms_bwd(dz * g, r, xh))
        return (jnp.concatenate([res[0], dqm], axis=1), jnp.concatenate([res[1], dv], axis=1), dgs[0], dgs[1])

    return _rowmap(fn, name="qk_bwd", T=T, tb=256,
                   ins=[("row", kvp, DIL_WIDTH, 0), ("row", pb, DIL_WIDTH, 0), ("const", kg_t), ("const", qg_t),
                        ("row", cos), ("row", sin), ("const", fold),
                        ("row", dq), ("row", dk), ("row", dv), ("row", dq_mem)],
                   outs=[("row", DIL_WIDTH + MEM_WIDTH, MMD), ("row", 2 * DIL_WIDTH, MMD)],
                   accs=[((1, HEAD_DIM), F32)] * 2)


def _band(kind):
    i = lax.broadcasted_iota(jnp.int32, (DIL_BLOCK, DIL_BLOCK), 0)
    j = lax.broadcasted_iota(jnp.int32, (DIL_BLOCK, DIL_BLOCK), 1)
    return (j <= i) if kind == "cur" else (j >= i)


def dil_attn_fwd(q, k, v, seq_blocks, *, name):
    T, W = q.shape
    nb = T // DIL_BLOCK
    masks = _head_masks(W)
    cur = pl.BlockSpec((DIL_BLOCK, W), lambda n: (n, 0))
    prv = pl.BlockSpec((DIL_BLOCK, W), lambda n: (jnp.maximum(n - 1, 0), 0))
    scale = 1.0 / math.sqrt(HEAD_DIM)

    def body(q_ref, kc_ref, kp_ref, vc_ref, vp_ref, m_ref, o_ref, l_ref):
        n = pl.program_id(0)
        has_prev = (n % seq_blocks) != 0
        q = q_ref[...].astype(F32)
        kc, kp = kc_ref[...].astype(F32), kp_ref[...].astype(F32)
        vc, vp = vc_ref[...].astype(F32), vp_ref[...].astype(F32)
        ok_c = _band("cur")
        ok_p = jnp.logical_and(_band("prev"), has_prev)
        o = jnp.zeros((DIL_BLOCK, W), F32)
        lse = jnp.zeros((DIL_BLOCK, W), F32)
        for h in range(W // HEAD_DIM):
            mh = m_ref[h]
            qh = q * mh
            sc = jnp.where(ok_c, _dot_nt(qh, kc) * scale, NEG_INF)
            sp = jnp.where(ok_p, _dot_nt(qh, kp) * scale, NEG_INF)
            mx = jnp.maximum(jnp.max(sc, axis=-1, keepdims=True), jnp.max(sp, axis=-1, keepdims=True))
            ec, ep = jnp.exp(sc - mx), jnp.exp(sp - mx)
            den = jnp.sum(ec, axis=-1, keepdims=True) + jnp.sum(ep, axis=-1, keepdims=True)
            o = o + (_dot(ec, vc * mh) + _dot(ep, vp * mh)) / den
            lse = lse + (mx + jnp.log(den)) * mh
        o_ref[...] = o
        l_ref[...] = lse

    return pl.pallas_call(
        body, name=name, grid=(nb,), in_specs=[cur, cur, prv, cur, prv, pl.BlockSpec(masks.shape, lambda n: (0, 0, 0))],
        out_specs=[cur, cur], out_shape=[jax.ShapeDtypeStruct((T, W), F32)] * 2,
        compiler_params=_cp(("parallel",)),
    )(q, k, k, v, v, masks)


def dil_attn_bwd(q, k, v, o, lse, do, dlse, seq_blocks, *, name):
    T, W = q.shape
    nb = T // DIL_BLOCK
    masks = _head_masks(W)
    cur = pl.BlockSpec((DIL_BLOCK, W), lambda n: (n, 0))
    prv = pl.BlockSpec((DIL_BLOCK, W), lambda n: (jnp.maximum(n - 1, 0), 0))
    nxt = pl.BlockSpec((DIL_BLOCK, W), lambda n: (jnp.minimum(n + 1, nb - 1), 0))
    scale = 1.0 / math.sqrt(HEAD_DIM)

    def body(qc_ref, qn_ref, kc_ref, kp_ref, vc_ref, vp_ref, oc_ref, on_ref, lc_ref, ln_ref, doc_ref, don_ref,
             dlc_ref, dln_ref, m_ref, dq_ref, dk_ref, dv_ref):
        n = pl.program_id(0)
        has_prev = (n % seq_blocks) != 0
        has_next = jnp.logical_and(((n + 1) % seq_blocks) != 0, n + 1 < nb)
        f = lambda ref: ref[...].astype(F32)
        qc, qn, kc, kp, vc, vp = f(qc_ref), f(qn_ref), f(kc_ref), f(kp_ref), f(vc_ref), f(vp_ref)
        doc, don = doc_ref[...], don_ref[...]
        ok_c = _band("cur")
        ok_p = jnp.logical_and(_band("prev"), has_prev)
        ok_n = jnp.logical_and(_band("prev"), has_next)
        dq = jnp.zeros((DIL_BLOCK, W), F32)
        dk = jnp.zeros((DIL_BLOCK, W), F32)
        dv = jnp.zeros((DIL_BLOCK, W), F32)

        def side(qh, kk, vv, doh, lse_h, corr, ok):
            s = _dot_nt(qh, kk) * scale
            pr = jnp.where(ok, jnp.exp(jnp.where(ok, s, NEG_INF) - lse_h), 0.0)
            ds = pr * (_dot_nt(doh, vv) + corr) * scale
            return pr, ds

        for h in range(W // HEAD_DIM):
            mh = m_ref[h]
            red = lambda z: jnp.sum(z * mh, axis=-1, keepdims=True)
            qh, doh = qc * mh, doc * mh
            lse_h = red(lc_ref[...]) * (1.0 / HEAD_DIM)
            corr = red(dlc_ref[...]) - red(doc * oc_ref[...])
            pr_c, ds_c = side(qh, kc, vc * mh, doh, lse_h, corr, ok_c)
            _, ds_p = side(qh, kp, vp * mh, doh, lse_h, corr, ok_p)
            dq = dq + _dot(ds_c, kc * mh) + _dot(ds_p, kp * mh)
            dk = dk + _dot_tn(ds_c, qh)
            dv = dv + _dot_tn(pr_c, doh)
            qh2, doh2 = qn * mh, don * mh
            lse_2 = red(ln_ref[...]) * (1.0 / HEAD_DIM)
            corr2 = red(dln_ref[...]) - red(don * on_ref[...])
            pr_n, ds_n = side(qh2, kc, vc * mh, doh2, lse_2, corr2, ok_n)
            dk = dk + _dot_tn(ds_n, qh2)
            dv = dv + _dot_tn(pr_n, doh2)
        dq_ref[...] = dq
        dk_ref[...] = dk
        dv_ref[...] = dv

    return pl.pallas_call(
        body, name=name, grid=(nb,),
        in_specs=[cur, nxt, cur, prv, cur, prv, cur, nxt, cur, nxt, cur, nxt, cur, nxt,
                  pl.BlockSpec(masks.shape, lambda n: (0, 0, 0))],
        out_specs=[cur] * 3, out_shape=[jax.ShapeDtypeStruct((T, W), F32)] * 3,
        compiler_params=_cp(("parallel",)),
    )(q, q, k, k, v, v, o, o, lse, lse, do, do, dlse, dlse, masks)


def _mix_weights(ls):
    m = jnp.maximum(jnp.maximum(ls[0], ls[1]), ls[2])
    es = [jnp.exp(l - m) for l in ls]
    den = es[0] + es[1] + es[2]
    return [e / den for e in es]


def mix_fwd(os_, ls, y_mem):
    T = y_mem.shape[0]

    def fn(i, nblk, o0, o1, o2, l0, l1, l2, ym):
        w = _mix_weights((l0, l1, l2))
        return jnp.concatenate([w[0] * o0 + w[1] * o1 + w[2] * o2, ym], axis=1)

    return _rowmap(fn, name="mix_fwd", T=T, tb=512, ins=[("row", z) for z in (*os_, *ls, y_mem)],
                   outs=[("row", 2 * MEM_WIDTH, MMD)])[0]


def mix_bwd(os_, ls, dycat):
    T = dycat.shape[0]

    def fn(i, nblk, o0, o1, o2, l0, l1, l2, dy):
        w = _mix_weights((l0, l1, l2))
        os3 = (o0, o1, o2)
        dws = [dy * o for o in os3]
        tot = w[0] * dws[0] + w[1] * dws[1] + w[2] * dws[2]
        return tuple(wg * dy for wg in w) + tuple(wg * (dw - tot) for wg, dw in zip(w, dws))

    return _rowmap(fn, name="mix_bwd", T=T, tb=512,
                   ins=[("row", z) for z in (*os_, *ls)] + [("row", dycat, MEM_WIDTH, 0)],
                   outs=[("row", MEM_WIDTH, F32)] * 6)


def loss_fwd_bwd(y, target):
    T, D = y.shape

    def fn(i, nblk, y, t):
        e = y - t
        return e * (1.0 / D), jnp.zeros((8, 128), F32) + jnp.sum(e * e) * (0.5 / D)

    return _rowmap(fn, name="loss", T=T, tb=512, ins=[("row", y), ("row", target)], outs=[("row", D, F32)],
                   accs=[((8, 128), F32)])


def _to_residues(z, dil):
    T, W = z.shape
    return z.reshape(T // dil, dil, W).transpose(1, 0, 2).reshape(T, W)


def _from_residues(z, dil):
    T, W = z.shape
    return z.reshape(dil, T // dil, W).transpose(1, 0, 2).reshape(T, W)


def _pad_rows(w, rows):
    return jnp.concatenate([w, jnp.zeros((rows - w.shape[0], w.shape[1]), w.dtype)], axis=0)


def _tile_heads(g, width):
    return jnp.tile(g.reshape(1, HEAD_DIM), (1, width // HEAD_DIM))


def _conv_rows(W, i):
    return [W["ffn_conv_w"][i][j:j + 1] for j in range(3)]


def _ffn_fwd(x, i, W):
    hn = rms_fwd(x, [W["ffn_norm"][i:i + 1]], name=f"ffn_rms{i}")[0]
    u = _mm(hn, W["ffn_w_up"][i], name=f"ffn_up{i}")
    z = ffn_act_fwd(u, _conv_rows(W, i), W["ffn_conv_b"][i:i + 1], name=f"ffn_act{i}")
    out = _mm(z, W["ffn_w_down"][i], add=x, name=f"ffn_down{i}")
    return out, (x, hn, u, z)


def _ffn_bwd(dout, i, W, saved, G):
    x, hn, u, z = saved
    dz = _mm(dout, W["ffn_w_down"][i], tb=True, name=f"ffn_down_dx{i}")
    G["ffn_w_down"][i] = _mm(z, dout, ta=True, name=f"ffn_down_dw{i}")
    du, dw0, dw1, dw2, db = ffn_act_bwd(u, _conv_rows(W, i), W["ffn_conv_b"][i:i + 1], dz, name=f"ffn_act_bwd{i}")
    G["ffn_conv_w"][i] = jnp.concatenate([dw0, dw1, dw2], axis=0)
    G["ffn_conv_b"][i] = db[0]
    dhn = _mm(du, W["ffn_w_up"][i], tb=True, name=f"ffn_up_dx{i}")
    G["ffn_w_up"][i] = _mm(hn, du, ta=True, name=f"ffn_up_dw{i}")
    dx, dg = rms_bwd(x, [W["ffn_norm"][i:i + 1]], [dhn], dout, name=f"ffn_rms_bwd{i}")
    G["ffn_norm"][i] = dg[0]
    return dx


def local_step(x, mem, target, W, late=None):
    T = x.shape[0]
    W = dict(W)
    G = {"ffn_w_down": [None, None], "ffn_w_up": [None, None], "ffn_conv_w": [None, None],
         "ffn_conv_b": [None, None], "ffn_norm": [None, None], "attn_norm": [None, None], "mem_norm": [None, None],
         "mem_w_kv": [None, None], "mem_q_norm": [None, None], "mem_k_norm": [None, None]}
    mu, w0, a0 = W["a_mu"], W["a_w0"], W["a_a0"]
    w2p, a2p, g2p = (_pad_rows(W["a_w2"][0], LORA_WIDTH),
                     jnp.concatenate([jnp.zeros((64, RWKV_WIDTH), MMD), W["a_a2"][0],
                                      jnp.zeros((128, RWKV_WIDTH), MMD)], axis=0),
                     jnp.concatenate([jnp.zeros((128, RWKV_WIDTH), MMD), W["a_g2"][0]], axis=0))
    k_k, k_a, lnx_w, lnx_b = W["a_k_k"], W["a_k_a"], W["a_lnx_w"], W["a_lnx_b"]
    r_k = W["a_r_k"].reshape(1, RWKV_WIDTH)

    h0 = rms_fwd(x, [W["attn_norm"][0:1]], name="attn_rms0")[0]
    p = _mm(h0, W["a_w_in"][0], name="a_in")
    r, w, k2, v, kk, kka, g = rwkv_pre_fwd(p, mu, w0, a0, w2p, a2p, g2p, k_k, k_a)
    if late is None:
        y, states, final_state = scan_fwd(r, w, k2, v, kk, kka)
    else:
        y, states, final_state, *gathered = scan_fwd(r, w, k2, v, kk, kka, gather=late[0])
        W.update(late[1](gathered))
    memkv = []
    for i in range(2):
        memkv.append(mem_kv_fwd(mem, W["mem_norm"][i:i + 1], W["mem_w_kv"][i], _tile_heads(W["mem_k_norm"][i], MEM_WIDTH),
                                name=f"mem_kv{i}"))
    qg0 = _tile_heads(W["mem_q_norm"][0], MEM_WIDTH)
    y_mem0 = mem_attn_fwd(p, SHIFT_WIDTH // MEM_WIDTH, memkv[0][0], memkv[0][1], qg0, name="mem_attn0")
    ycat0 = mix_gate_fwd(y, r, k2, v, g, y_mem0, lnx_w, lnx_b, r_k)
    x1 = _mm(ycat0, W["a_w_out"][0], add=x, name="a_out")
    x2, ffn0 = _ffn_fwd(x1, 0, W)

    h1, hkv = rms_fwd(x2, [W["attn_norm"][1:2], W["kv_norm"].reshape(1, -1)], name="attn_rms1")
    kvp = _mm(hkv, W["kv_w"], name="kv_in")
    pb = _mm(h1, W["b_w_in"][0], name="b_in")
    cos, sin = rope_tables(T)
    kg_t, qg_t = _tile_heads(W["kv_k_norm"], DIL_WIDTH), _tile_heads(W["b_q_norm"][0], DIL_WIDTH)
    q, ksh, vsh = qk_fwd(kvp, pb, kg_t, qg_t, cos, sin)
    os_, ls, grp = [], [], []
    for gi, (win, dil) in enumerate(DIL_GROUPS):
        sl = slice(gi * MEM_WIDTH, (gi + 1) * MEM_WIDTH)
        qg_, kg_, vg_ = (_to_residues(z[:, sl], dil) for z in (q, ksh, vsh))
        o_r, l_r = dil_attn_fwd(qg_, kg_, vg_, T // dil // DIL_BLOCK, name=f"dil_fwd{gi}")
        grp.append((qg_, kg_, vg_, o_r, l_r))
        os_.append(_from_residues(o_r, dil))
        ls.append(_from_residues(l_r, dil))
    qg1 = _tile_heads(W["mem_q_norm"][1], MEM_WIDTH)
    y_mem1 = mem_attn_fwd(pb, DIL_WIDTH // MEM_WIDTH, memkv[1][0], memkv[1][1], qg1, name="mem_attn1")
    ycat1 = mix_fwd(os_, ls, y_mem1)
    x3 = _mm(ycat1, W["b_w_out"][0], add=x2, name="b_out")
    x4, ffn1 = _ffn_fwd(x3, 1, W)

    dx4, loss = loss_fwd_bwd(x4, target)

    dx3 = _ffn_bwd(dx4, 1, W, ffn1, G)
    dycat1 = _mm(dx3, W["b_w_out"][0], tb=True, name="b_out_dx")
    G["b_w_out"] = _mm(ycat1, dx3, ta=True, name="b_out_dw")[None]
    dq_mem1, dkn1, dvm1, dqg1 = mem_attn_bwd(pb, DIL_WIDTH // MEM_WIDTH, memkv[1][0], memkv[1][1], qg1, dycat1, 1,
                                             name="mem_attn_bwd1")
    G["mem_q_norm"][1] = dqg1[0]
    d_os_ls = mix_bwd(os_, ls, dycat1)
    dqs, dks, dvs = [], [], []
    for gi, (win, dil) in enumerate(DIL_GROUPS):
        qg_, kg_, vg_, o_r, l_r = grp[gi]
        do_r, dl_r = _to_residues(d_os_ls[gi], dil), _to_residues(d_os_ls[3 + gi], dil)
        dq_r, dk_r, dv_r = dil_attn_bwd(qg_, kg_, vg_, o_r, l_r, do_r, dl_r, T // dil // DIL_BLOCK, name=f"dil_bwd{gi}")
        dqs.append(_from_residues(dq_r, dil))
        dks.append(_from_residues(dk_r, dil))
        dvs.append(_from_residues(dv_r, dil))
    dq, dk, dv = (jnp.concatenate(z, axis=1) for z in (dqs, dks, dvs))
    dpb, dkvp, dqn_g, dkn_g = qk_bwd(kvp, pb, kg_t, qg_t, cos, sin, dq, dk, dv, dq_mem1)
    G["b_q_norm"] = dqn_g
    G["kv_k_norm"] = dkn_g[0]
    dh1 = _mm(dpb, W["b_w_in"][0], tb=True, name="b_in_dx")
    G["b_w_in"] = _mm(h1, dpb, ta=True, name="b_in_dw")[None]
    dhkv = _mm(dkvp, W["kv_w"], tb=True, name="kv_in_dx")
    G["kv_w"] = _mm(hkv, dkvp, ta=True, name="kv_in_dw")
    dx2, dg1, dgkv = rms_bwd(x2, [W["attn_norm"][1:2], W["kv_norm"].reshape(1, -1)], [dh1, dhkv], dx3,
                             name="attn_rms_bwd1")
    G["attn_norm"][1] = dg1[0]
    G["kv_norm"] = dgkv[0]

    dx1 = _ffn_bwd(dx2, 0, W, ffn0, G)
    dycat0 = _mm(dx1, W["a_w_out"][0], tb=True, name="a_out_dx")
    G["a_w_out"] = _mm(ycat0, dx1, ta=True, name="a_out_dw")[None]
    dq_mem0, dkn0, dvm0, dqg0 = mem_attn_bwd(p, SHIFT_WIDTH // MEM_WIDTH, memkv[0][0], memkv[0][1], qg0, dycat0,
                                             RWKV_WIDTH // MEM_WIDTH, name="mem_attn_bwd0")
    G["mem_q_norm"][0] = dqg0[0]
    dy, dr_b, dk2_b, dv_b, dg, dlw, dlb, drk = mix_gate_bwd(y, r, k2, v, g, dycat0, lnx_w, lnx_b, r_k)
    for i, (dkn, dvm) in enumerate(((dkn0, dvm0), (dkn1, dvm1))):
        dwkv, dgm, dkg = mem_kv_bwd(mem, W["mem_norm"][i:i + 1], W["mem_w_kv"][i],
                                    _tile_heads(W["mem_k_norm"][i], MEM_WIDTH), dkn, dvm, name=f"mem_kv_bwd{i}")
        G["mem_w_kv"][i], G["mem_norm"][i], G["mem_k_norm"][i] = dwkv, dgm[0], dkg[0]
    late_out = None
    if late is None:
        dr, dw, dk2, dv, dkk, dkka = scan_bwd(r, w, k2, v, kk, kka, states, final_state, dy)
    else:
        pieces = late[2](G)
        dr, dw, dk2, dv, dkk, dkka, *received = scan_bwd(r, w, k2, v, kk, kka, states, final_state, dy, scatter=pieces)
        late_out = (received, pieces)
    dxs, dmu, dw0, da0, dw2p, da2p, dg2p, dk_k, dk_a = rwkv_pre_bwd(
        p, mu, w0, a0, w2p, a2p, g2p, k_k, k_a, (dr, dr_b), dw, (dk2, dk2_b), (dv, dv_b), dkk, dkka, dg)
    dp = shift_bwd(dxs, mu, dq_mem0)
    G.update(a_mu=dmu, a_w0=dw0, a_a0=da0, a_w2=dw2p[None, :64], a_a2=da2p[None, 64:128], a_g2=dg2p[None, 128:],
             a_k_k=dk_k, a_k_a=dk_a, a_r_k=drk.reshape(1, RWKV_HEADS, HEAD_DIM), a_lnx_w=dlw, a_lnx_b=dlb)
    dh0 = _mm(dp, W["a_w_in"][0], tb=True, name="a_in_dx")
    G["a_w_in"] = _mm(h0, dp, ta=True, name="a_in_dw")[None]
    grad_x, dg0 = rms_bwd(x, [W["attn_norm"][0:1]], [dh0], dx1, name="attn_rms_bwd0")
    G["attn_norm"][0] = dg0[0]
    for n in list(G):
        if isinstance(G[n], list):
            G[n] = jnp.stack(G[n], axis=0)
    return loss, grad_x, G, late_out


HBM_SPEC = pl.BlockSpec(memory_space=pltpu.HBM)


def _mesh_pos():
    return lax.axis_index("x"), lax.axis_index("y"), lax.axis_index("c")


def _other_chips(x, y):
    return [(1 - x, y), (x, 1 - y), (1 - x, 1 - y)]


def _remote(send_sems, recv_sems, k, src, dst, to):
    return pltpu.make_async_remote_copy(src_ref=src, dst_ref=dst, send_sem=send_sems.at[k], recv_sem=recv_sems.at[k],
                                        device_id=to, device_id_type=MESH)


def _comm_call(body, name, ins, out_shape, n_remote):
    scratch = [pltpu.SemaphoreType.DMA((n_remote,)), pltpu.SemaphoreType.DMA((n_remote,))]
    return pl.pallas_call(body, name=name, in_specs=[HBM_SPEC] * len(ins), out_specs=[HBM_SPEC] * len(out_shape),
                          out_shape=out_shape, scratch_shapes=scratch)(*ins)


def comm_gather(wbig, wsm):
    def body(wb, ws, ob, os_, send_sems, recv_sems):
        x, y, c = _mesh_pos()
        s = 2 * x + y
        me, sibling = (x, y, c), (x, y, 1 - c)
        chips = _other_chips(x, y)
        rc = functools.partial(_remote, send_sems, recv_sems)
        first = []
        for j, (cx, cy) in enumerate(chips):
            first.append(rc(j, wb.at[c], ob.at[s, c], (cx, cy, c)))
            first.append(rc(6 + j, ws, os_.at[s], (cx, cy, c)))
        for cp in first:
            cp.start()
        passed = []
        for j, (cx, cy) in enumerate(chips):
            blk = ob.at[2 * cx + cy, c]
            rc(j, blk, blk, me).wait_recv()
            passed.append(rc(3 + j, blk, blk, sibling))
            passed[-1].start()
        for j, (cx, cy) in enumerate(chips):
            blk = ob.at[2 * cx + cy, 1 - c]
            rc(3 + j, blk, blk, me).wait_recv()
            sb = os_.at[2 * cx + cy]
            rc(6 + j, sb, sb, me).wait_recv()
        for cp in first + passed:
            cp.wait_send()

    out_shape = [jax.ShapeDtypeStruct((N_CHIPS, *wbig.shape), wbig.dtype),
                 jax.ShapeDtypeStruct((N_CHIPS, *wsm.shape), wsm.dtype)]
    return _comm_call(body, "comm_gather", [wbig, wsm], out_shape, 9)


def comm_pair_exchange(gb, gs):
    def body(gb_ref, gs_ref, rb_ref, rs_ref, send_sems, recv_sems):
        x, y, c = _mesh_pos()
        sibling = (x, y, 1 - c)
        rc = functools.partial(_remote, send_sems, recv_sems)
        cps = [rc(r, gb_ref.at[r, 1 - c], rb_ref.at[r], sibling) for r in range(N_CHIPS)]
        cps.append(rc(N_CHIPS, gs_ref.at[1 - c], rs_ref, sibling))
        for cp in cps:
            cp.start()
        for cp in cps:
            cp.wait()

    out_shape = [jax.ShapeDtypeStruct((N_CHIPS, *gb.shape[2:]), gb.dtype), jax.ShapeDtypeStruct(gs.shape[1:], gs.dtype)]
    return _comm_call(body, "comm_pair_exchange", [gb, gs], out_shape, N_CHIPS + 1)


def comm_chip_exchange(hb, hs):
    def body(hb_ref, hs_ref, qb_ref, qs_ref, send_sems, recv_sems):
        x, y, c = _mesh_pos()
        s = 2 * x + y
        me = (x, y, c)
        chips = _other_chips(x, y)
        rc = functools.partial(_remote, send_sems, recv_sems)
        cps = []
        for j, (cx, cy) in enumerate(chips):
            cps.append(rc(j, hb_ref.at[2 * cx + cy], qb_ref.at[s], (cx, cy, c)))
            cps.append(rc(3 + j, hs_ref, qs_ref.at[s], (cx, cy, c)))
        for cp in cps:
            cp.start()
        for j, (cx, cy) in enumerate(chips):
            blk = qb_ref.at[2 * cx + cy]
            rc(j, blk, blk, me).wait_recv()
            sb = qs_ref.at[2 * cx + cy]
            rc(3 + j, sb, sb, me).wait_recv()
        for cp in cps:
            cp.wait_send()

    out_shape = [jax.ShapeDtypeStruct(hb.shape, hb.dtype), jax.ShapeDtypeStruct((N_CHIPS, *hs.shape), hs.dtype)]
    return _comm_call(body, "comm_chip_exchange", [hb, hs], out_shape, 6)


def comm_pair_share(halves):
    n = len(halves)

    def body(*refs):
        x, y, c = _mesh_pos()
        send_sems, recv_sems = refs[2 * n], refs[2 * n + 1]
        cps = [_remote(send_sems, recv_sems, k, refs[k], refs[n + k], (x, y, 1 - c)) for k in range(n)]
        for cp in cps:
            cp.start()
        for cp in cps:
            cp.wait()

    out_shape = [jax.ShapeDtypeStruct(h.shape, h.dtype) for h in halves]
    return _comm_call(body, "comm_pair_share", list(halves), out_shape, n)


def add_pairs(a, b, out_dtype, *, name, tb):
    T, L = a.shape
    return _rowmap(lambda i, n, p, q: p + q, name=name, T=T, tb=tb, ins=[("row", a), ("row", b)],
                   outs=[("row", L, out_dtype)])[0]


def add_chips(parts, *, name, tb):
    T, L = parts[0].shape

    def fn(i, n, *ps):
        acc = ps[0].astype(F32)
        for p in ps[1:]:
            acc = acc + p.astype(F32)
        return acc

    return _rowmap(fn, name=name, T=T, tb=tb, ins=[("row", p) for p in parts], outs=[("row", L, F32)])[0]


def adamw(g, w, m, v, *, name, tb):
    T, L = g.shape

    def fn(i, n, g, w, m, v):
        m2 = ADAM_B1 * m + (1.0 - ADAM_B1) * g
        v2 = ADAM_B2 * v + (1.0 - ADAM_B2) * (g * g)
        m_hat = m2 / (1.0 - ADAM_B1 ** ADAM_STEP)
        v_hat = v2 / (1.0 - ADAM_B2 ** ADAM_STEP)
        return -ADAM_LR * (m_hat / (jnp.sqrt(v_hat) + ADAM_EPS) + ADAM_WD * w), m2, v2

    return _rowmap(fn, name=name, T=T, tb=tb, ins=[("row", z) for z in (g, w, m, v)], outs=[("row", L, F32)] * 3)


BIG_LANES = 1024
SMALL_LANES = 128


def _flat_cat(arrs, total, dtype):
    parts = [a.reshape(-1).astype(dtype) for a in arrs]
    n = sum(p.shape[0] for p in parts)
    assert n <= total, (n, total)
    if n < total:
        parts.append(jnp.zeros((total - n,), dtype))
    return jnp.concatenate(parts)


def _split_flat(flat, shapes):
    out, off = [], 0
    for shp in shapes:
        n = math.prod(shp)
        out.append(flat[off:off + n].reshape(shp))
        off += n
    return out


def _round_up(n, m):
    return -(-n // m) * m


def _full_shape(shard_shape, axis):
    return tuple(d * N_CHIPS if i == axis else d for i, d in enumerate(shard_shape))


def kernel(x, mem, attn_norm, a_w_in, a_mu, a_w0, a_w2, a_a0, a_a2, a_g2, a_k_k, a_k_a, a_r_k, a_lnx_w, a_lnx_b, a_w_out, kv_norm, kv_w, kv_k_norm, b_w_in, b_q_norm, b_w_out, mem_norm, mem_w_kv, mem_q_norm, mem_k_norm, ffn_norm, ffn_w_up, ffn_conv_w, ffn_conv_b, ffn_w_down, loss_target, m_attn_norm, m_a_w_in, m_a_mu, m_a_w0, m_a_w2, m_a_a0, m_a_a2, m_a_g2, m_a_k_k, m_a_k_a, m_a_r_k, m_a_lnx_w, m_a_lnx_b, m_a_w_out, m_kv_norm, m_kv_w, m_kv_k_norm, m_b_w_in, m_b_q_norm, m_b_w_out, m_mem_norm, m_mem_w_kv, m_mem_q_norm, m_mem_k_norm, m_ffn_norm, m_ffn_w_up, m_ffn_conv_w, m_ffn_conv_b, m_ffn_w_down, v_attn_norm, v_a_w_in, v_a_mu, v_a_w0, v_a_w2, v_a_a0, v_a_a2, v_a_g2, v_a_k_k, v_a_k_a, v_a_r_k, v_a_lnx_w, v_a_lnx_b, v_a_w_out, v_kv_norm, v_kv_w, v_kv_k_norm, v_b_w_in, v_b_q_norm, v_b_w_out, v_mem_norm, v_mem_w_kv, v_mem_q_norm, v_mem_k_norm, v_ffn_norm, v_ffn_w_up, v_ffn_conv_w, v_ffn_conv_b, v_ffn_w_down):
    args = (attn_norm, a_w_in, a_mu, a_w0, a_w2, a_a0, a_a2, a_g2, a_k_k, a_k_a, a_r_k, a_lnx_w, a_lnx_b, a_w_out, kv_norm, kv_w, kv_k_norm, b_w_in, b_q_norm, b_w_out, mem_norm, mem_w_kv, mem_q_norm, mem_k_norm, ffn_norm, ffn_w_up, ffn_conv_w, ffn_conv_b, ffn_w_down)
    ms = (m_attn_norm, m_a_w_in, m_a_mu, m_a_w0, m_a_w2, m_a_a0, m_a_a2, m_a_g2, m_a_k_k, m_a_k_a, m_a_r_k, m_a_lnx_w, m_a_lnx_b, m_a_w_out, m_kv_norm, m_kv_w, m_kv_k_norm, m_b_w_in, m_b_q_norm, m_b_w_out, m_mem_norm, m_mem_w_kv, m_mem_q_norm, m_mem_k_norm, m_ffn_norm, m_ffn_w_up, m_ffn_conv_w, m_ffn_conv_b, m_ffn_w_down)
    vs = (v_attn_norm, v_a_w_in, v_a_mu, v_a_w0, v_a_w2, v_a_a0, v_a_a2, v_a_g2, v_a_k_k, v_a_k_a, v_a_r_k, v_a_lnx_w, v_a_lnx_b, v_a_w_out, v_kv_norm, v_kv_w, v_kv_k_norm, v_b_w_in, v_b_q_norm, v_b_w_out, v_mem_norm, v_mem_w_kv, v_mem_q_norm, v_mem_k_norm, v_ffn_norm, v_ffn_w_up, v_ffn_conv_w, v_ffn_conv_b, v_ffn_w_down)
    w_sh, m_sh, v_sh = (dict(zip(WEIGHTS, z)) for z in (args, ms, vs))
    xi, yi, ci = _mesh_pos()
    chip = 2 * xi + yi
    axes = {**dict(BIG), **dict(SMALL_SHARDED)}
    early_names = [n for n, _ in BIG if n in EARLY_BIG]
    late_names = [n for n, _ in BIG if n not in EARLY_BIG and n != NATURAL_BIG]
    ss_names, ss_axes = [n for n, _ in SMALL_SHARDED], dict(SMALL_SHARDED)
    shapes_of = lambda names: [w_sh[n].shape for n in names]
    count = lambda names: sum(math.prod(s) for s in shapes_of(names))
    n_early, n_late = count(early_names), count(late_names)
    assert n_early % (2 * 16 * BIG_LANES) == 0 and n_late % (2 * 16 * BIG_LANES) == 0
    mh, mh_late = n_early // (2 * BIG_LANES), n_late // (2 * BIG_LANES)
    n_ss = _round_up(count(ss_names), 8 * SMALL_LANES)

    def shard_pack(names, total, dtype, source):
        return _flat_cat([source[n] for n in names], total, dtype)

    def join(pieces, n):
        if pieces[0].ndim == 3:
            return [jnp.concatenate([p[l] for p in pieces], axis=axes[n] - 1) for l in range(pieces[0].shape[0])]
        return jnp.concatenate(pieces, axis=axes[n])

    def unshard(names, gathered):
        per_chip = [_split_flat(gathered[j], shapes_of(names)) for j in range(N_CHIPS)]
        return {n: join([per_chip[j][k] for j in range(N_CHIPS)], n) for k, n in enumerate(names)}

    def whole(g):
        return jnp.stack(g, axis=0) if isinstance(g, list) else g

    def by_chip(names, total, dtype, grads):
        parts = [jnp.split(whole(grads[n]), N_CHIPS, axis=axes[n]) for n in names]
        return jnp.stack([_flat_cat([p[j] for p in parts], total, dtype) for j in range(N_CHIPS)])

    wbig = shard_pack(early_names, n_early, MMD, w_sh).reshape(2, mh, BIG_LANES)
    wsm = shard_pack(ss_names, n_ss, F32, w_sh).reshape(-1, SMALL_LANES)
    wbig_all, wsm_all = comm_gather(wbig, wsm)
    wbig_all = lax.dynamic_update_index_in_dim(wbig_all, wbig, chip, 0).reshape(N_CHIPS, -1)
    wsm_all = lax.dynamic_update_index_in_dim(wsm_all, wsm, chip, 0).reshape(N_CHIPS, -1)
    W = {n: w_sh[n] for n in SMALL_REPL}
    W.update(unshard(early_names, wbig_all))
    W.update(unshard(ss_names, wsm_all))
    for n in ("a_w2", "a_a2", "a_g2"):
        W[n] = [z.astype(MMD) for z in W[n]]
    wlate = shard_pack(late_names, n_late, MMD, w_sh).reshape(2, mh_late, BIG_LANES)
    nat_axis = axes[NATURAL_BIG]
    wnat = w_sh[NATURAL_BIG].astype(MMD)
    assert wnat.shape[0] == 2 and nat_axis != 0

    def unpack_late(gathered):
        full = lax.dynamic_update_index_in_dim(gathered[0], wlate, chip, 0)
        out = unshard(late_names, full.reshape(N_CHIPS, -1))
        nat = lax.dynamic_update_index_in_dim(gathered[1], wnat, chip, 0)
        out[NATURAL_BIG] = join([nat[j] for j in range(N_CHIPS)], NATURAL_BIG)
        return out

    def pack_late(grads):
        return [by_chip(late_names, n_late, BF16, grads).reshape(N_CHIPS, 2, mh_late, BIG_LANES),
                jnp.stack(jnp.split(whole(grads[NATURAL_BIG]).astype(BF16), N_CHIPS, axis=nat_axis))]

    loss_blk, grad_x, G, (received, pieces) = local_step(x[0], mem[0], loss_target[0], W,
                                                          late=([wlate, wnat], unpack_late, pack_late))
    loss = lax.psum(loss_blk[0, 0], ("x", "y", "c"))

    own_piece = lambda p: lax.dynamic_index_in_dim(lax.dynamic_index_in_dim(p, chip, 0, keepdims=False), ci, 0,
                                                   keepdims=False)
    gh_late = add_chips([received[0][k] for k in range(len(PEER_FLIPS))] + [own_piece(pieces[0])],
                        name="add_pieces_late", tb=32)
    gh_nat = add_chips([received[1][k] for k in range(len(PEER_FLIPS))] + [own_piece(pieces[1])],
                       name="add_pieces_natural", tb=32)
    gbig = by_chip(early_names, n_early, F32, G).reshape(N_CHIPS, 2, mh, BIG_LANES)
    sm_full_names = ss_names + list(SMALL_REPL)
    sm_full_shapes = [_full_shape(w_sh[n].shape, ss_axes[n]) for n in ss_names] + [w_sh[n].shape for n in SMALL_REPL]
    n_smf = _round_up(sum(math.prod(s) for s in sm_full_shapes), 2 * 8 * SMALL_LANES)
    msh = n_smf // (2 * SMALL_LANES)
    gsm = _flat_cat([G[n] for n in sm_full_names], n_smf, F32).reshape(2, msh, SMALL_LANES)
    rb, rs = comm_pair_exchange(gbig, gsm)
    mine_b = lax.dynamic_index_in_dim(gbig, ci, axis=1, keepdims=False)
    mine_s = lax.dynamic_index_in_dim(gsm, ci, axis=0, keepdims=False)
    hb = add_pairs(mine_b.reshape(-1, BIG_LANES), rb.reshape(-1, BIG_LANES), BF16, name="add_pairs_big", tb=128)
    hs = add_pairs(mine_s, rs, F32, name="add_pairs_small", tb=msh)
    hb = hb.reshape(N_CHIPS, mh, BIG_LANES)
    qb, qs = comm_chip_exchange(hb, hs)
    qb = lax.dynamic_update_index_in_dim(qb, lax.dynamic_index_in_dim(hb, chip, 0, keepdims=False), chip, 0)
    qs = lax.dynamic_update_index_in_dim(qs, hs, chip, 0)
    gh = add_chips([qb[j] for j in range(N_CHIPS)], name="add_chips_big", tb=32)
    gsh = add_chips([qs[j] for j in range(N_CHIPS)], name="add_chips_small", tb=msh)
    rh, rh_late, rh_nat, rsh = comm_pair_share([gh, gh_late, gh_nat, gsh])
    both = lambda mine_, theirs: jnp.where(ci == 0, jnp.stack([mine_, theirs]), jnp.stack([theirs, mine_]))
    gfull, gfull_late, gfull_nat, gsfull = both(gh, rh), both(gh_late, rh_late), both(gh_nat, rh_nat), both(gsh, rsh)

    res = {tag: {} for tag in ("grad", "delta", "new_m", "new_v")}
    big_grads = (list(zip(early_names, _split_flat(gfull.reshape(-1), shapes_of(early_names))))
                 + list(zip(late_names, _split_flat(gfull_late.reshape(-1), shapes_of(late_names))))
                 + [(NATURAL_BIG, gfull_nat)])
    for n, g in big_grads:
        shp = w_sh[n].shape
        rows = lambda z: z.reshape(-1, shp[-1])
        nrow = math.prod(shp[:-1])
        tb = next(t for t in (512, 256, 128, 64) if nrow % t == 0 and t * shp[-1] <= (1 << 19))
        outs = adamw(rows(g), rows(w_sh[n]), rows(m_sh[n]), rows(v_sh[n]), name=f"adamw_{n}", tb=tb)
        res["grad"][n] = g
        for tag, o in zip(("delta", "new_m", "new_v"), outs):
            res[tag][n] = o.reshape(shp)
    sm_full = dict(zip(sm_full_names, _split_flat(gsfull.reshape(-1), sm_full_shapes)))
    g_loc = {}
    for n in ss_names:
        size = w_sh[n].shape[ss_axes[n]]
        g_loc[n] = lax.dynamic_slice_in_dim(sm_full[n], chip * size, size, axis=ss_axes[n])
    for n in SMALL_REPL:
        g_loc[n] = sm_full[n]
    n_sml = _round_up(sum(math.prod(w_sh[n].shape) for n in sm_full_names), 8 * SMALL_LANES)
    pack_sm = lambda d: _flat_cat([d[n] for n in sm_full_names], n_sml, F32).reshape(-1, SMALL_LANES)
    d_sm, m_sm, v_sm = adamw(pack_sm(g_loc), pack_sm(w_sh), pack_sm(m_sh), pack_sm(v_sh), name="adamw_small",
                             tb=n_sml // SMALL_LANES)
    sm_loc_shapes = [w_sh[n].shape for n in sm_full_names]
    res["grad"].update(g_loc)
    for tag, smv in (("delta", d_sm), ("new_m", m_sm), ("new_v", v_sm)):
        res[tag].update(dict(zip(sm_full_names, _split_flat(smv.reshape(-1), sm_loc_shapes))))
    return (loss, grad_x[None], *[res[tag][n] for tag in ("grad", "delta", "new_m", "new_v") for n in WEIGHTS])
```

```python
import functools
import math

import numpy as np
import jax
import jax.numpy as jnp
from jax import lax
from jax.experimental import pallas as pl
from jax.experimental.pallas import tpu as pltpu

F32 = jnp.float32
BF16 = jnp.bfloat16
MMD = jnp.bfloat16

D_MODEL = 1024
HEAD_DIM = 64
N_MEM = 256
MEM_WIDTH = 256
RWKV_HEADS = 12
RWKV_WIDTH = 768
SHIFT_WIDTH = 2560
LORA_WIDTH = 256
DIL_WIDTH = 768
DIL_GROUPS = ((128, 1), (512, 4), (2048, 16))
DIL_BLOCK = 128
D_FF = 2816
ROPE_THETA = 10000.0
RMS_EPS = 1e-6
LNX_EPS = 64e-5
NEG_INF = -1e30
ADAM_LR = 0.001
ADAM_B1 = 0.9
ADAM_B2 = 0.999
ADAM_EPS = 1e-08
ADAM_WD = 0.01
ADAM_STEP = 10
N_CHIPS = 4
MESH = pl.DeviceIdType.MESH
VMEM_LIMIT_MB = 56
SCAN_CHUNK = 64
SCAN_UNROLL = 32
SCAN_UNROLL_BWD = 16

BIG = (("a_w_in", 2), ("a_w_out", 1), ("kv_w", 1), ("b_w_in", 1), ("b_w_out", 2), ("mem_w_kv", 1),
       ("ffn_w_up", 2), ("ffn_w_down", 1))
EARLY_BIG = ("a_w_in",)
NATURAL_BIG = "ffn_w_up"
SMALL_SHARDED = (("a_mu", 1), ("a_w0", 1), ("a_w2", 2), ("a_a0", 1), ("a_a2", 2), ("a_g2", 2), ("a_k_k", 1),
                 ("a_k_a", 1), ("a_lnx_w", 1), ("a_lnx_b", 1), ("ffn_conv_w", 2))
SMALL_REPL = ("attn_norm", "a_r_k", "kv_norm", "kv_k_norm", "b_q_norm", "mem_norm", "mem_q_norm", "mem_k_norm",
              "ffn_norm", "ffn_conv_b")
WEIGHTS = ("attn_norm", "a_w_in", "a_mu", "a_w0", "a_w2", "a_a0", "a_a2", "a_g2", "a_k_k", "a_k_a", "a_r_k",
           "a_lnx_w", "a_lnx_b", "a_w_out", "kv_norm", "kv_w", "kv_k_norm", "b_w_in", "b_q_norm", "b_w_out",
           "mem_norm", "mem_w_kv", "mem_q_norm", "mem_k_norm", "ffn_norm", "ffn_w_up", "ffn_conv_w", "ffn_conv_b",
           "ffn_w_down")


def _cp(sem=None, **kw):
    return pltpu.CompilerParams(dimension_semantics=sem, vmem_limit_bytes=VMEM_LIMIT_MB << 20, **kw)


def _tile(n, cands=(512, 256, 128)):
    for c in cands:
        if n % c == 0:
            return c
    return n


def _mm(a, b, *, name, ta=False, tb=False, add=None, out_dtype=F32):
    K, M = a.shape if ta else a.shape[::-1]
    N = b.shape[0] if tb else b.shape[1]
    assert K == (b.shape[1] if tb else b.shape[0])
    tm, tn = _tile(M, (512, 256, 128) if ta else (1024, 512, 256, 128)), _tile(N, (512, 1408, 256, 128))
    bytes_of = lambda z: z.size * z.dtype.itemsize
    kept = 1 if bytes_of(b) + bytes_of(a) * (N // tn) < bytes_of(a) + bytes_of(b) * (M // tm) else 0
    mi, ni = ((lambda o, i: i), (lambda o, i: o)) if kept else ((lambda o, i: o), (lambda o, i: i))
    grid = (N // tn, M // tm) if kept else (M // tm, N // tn)
    a_blk, b_blk = ((K, tm) if ta else (tm, K)), ((tn, K) if tb else (K, tn))
    a_spec = pl.BlockSpec(a_blk, (lambda o, i: (0, mi(o, i))) if ta else (lambda o, i: (mi(o, i), 0)))
    b_spec = pl.BlockSpec(b_blk, (lambda o, i: (ni(o, i), 0)) if tb else (lambda o, i: (0, ni(o, i))))
    o_spec = pl.BlockSpec((tm, tn), lambda o, i: (mi(o, i), ni(o, i)))
    dn = (((0,) if ta else (1,), (1,) if tb else (0,)), ((), ()))
    has_add = add is not None
    cache = (b if kept else a).dtype != MMD

    def body(*refs):
        vals = [refs[0], refs[1]]
        o_ref = refs[2 + has_add]
        if cache:
            scr = refs[-1]

            @pl.when(pl.program_id(1) == 0)
            def _():
                scr[...] = vals[kept][...].astype(MMD)

            vals[kept] = scr
        acc = lax.dot_general(vals[0][...].astype(MMD), vals[1][...].astype(MMD), dn, preferred_element_type=F32)
        if has_add:
            acc = acc + refs[2][...]
        o_ref[...] = acc.astype(o_ref.dtype)

    ins = [a, b] + ([add] if has_add else [])
    specs = [a_spec, b_spec] + ([o_spec] if has_add else [])
    return pl.pallas_call(
        body, name=name, grid=grid, in_specs=specs, out_specs=o_spec,
        out_shape=jax.ShapeDtypeStruct((M, N), out_dtype),
        scratch_shapes=[pltpu.VMEM(b_blk if kept else a_blk, MMD)] if cache else [],
        compiler_params=_cp(("parallel", "arbitrary")),
    )(*ins)


def _rowmap(fn, *, name, T, tb, ins, outs, accs=()):
    nblk = T // tb
    assert T % tb == 0 and tb % 8 == 0
    in_specs, args = [], []
    for spec in ins:
        kind, arr = spec[0], spec[1]
        w, cb = (spec[2], spec[3]) if len(spec) > 2 else (arr.shape[-1], 0)
        if kind == "row":
            in_specs.append(pl.BlockSpec((tb, w), lambda i, cb=cb: (i, cb)))
        elif kind == "prev":
            in_specs.append(pl.BlockSpec((8, w), lambda i, cb=cb: (jnp.maximum(i * (tb // 8) - 1, 0), cb)))
        elif kind == "next":
            in_specs.append(pl.BlockSpec((8, w), lambda i, cb=cb: (jnp.minimum((i + 1) * (tb // 8), T // 8 - 1), cb)))
        elif kind == "const":
            in_specs.append(pl.BlockSpec(arr.shape, lambda i, nd=arr.ndim: (0,) * nd))
        else:
            raise ValueError(kind)
        args.append(arr)
    out_shape, out_specs = [], []
    for kind, w, dt in outs:
        out_shape.append(jax.ShapeDtypeStruct((T, w), dt))
        out_specs.append(pl.BlockSpec((tb, w), lambda i: (i, 0)))
    for shp, dt in accs:
        out_shape.append(jax.ShapeDtypeStruct(shp, dt))
        out_specs.append(pl.BlockSpec(shp, lambda i, nd=len(shp): (0,) * nd))
    n_in, n_out = len(ins), len(outs)

    def body(*refs):
        i = pl.program_id(0)
        vals = [r[...] for r in refs[:n_in]]
        res = fn(i, nblk, *vals)
        if not isinstance(res, (tuple, list)):
            res = (res,)
        assert len(res) == n_out + len(accs), (name, len(res))
        for r, v in zip(refs[n_in:n_in + n_out], res[:n_out]):
            r[...] = v.astype(r.dtype)
        acc_refs = refs[n_in + n_out:]
        if acc_refs:
            @pl.when(i == 0)
            def _():
                for r in acc_refs:
                    r[...] = jnp.zeros(r.shape, r.dtype)

            for r, v in zip(acc_refs, res[n_out:]):
                r[...] += v

    res = pl.pallas_call(
        body, name=name, grid=(nblk,), in_specs=in_specs, out_specs=out_specs, out_shape=out_shape,
        compiler_params=_cp(("arbitrary",)),
    )(*args)
    return res


def _row_pick(halo, r):
    rid = lax.broadcasted_iota(jnp.int32, halo.shape, 0)
    return jnp.sum(jnp.where(rid == r, halo, 0.0), axis=0, keepdims=True)


def _shift_down(x, halo, is_first):
    ext = jnp.concatenate([jnp.where(is_first, 0.0, 1.0) * halo, x], axis=0)
    return pltpu.roll(ext, 1, axis=0)[8:]


def _shift_up(x, halo, is_last):
    n = x.shape[0]
    ext = jnp.concatenate([x, jnp.where(is_last, 0.0, 1.0) * halo], axis=0)
    return pltpu.roll(ext, n + 7, axis=0)[:n]


def _dot(a, b, dn=(((1,), (0,)), ((), ()))):
    return lax.dot_general(a.astype(MMD), b.astype(MMD), dn, preferred_element_type=F32)


def _dot_nt(a, b):
    return _dot(a, b, (((1,), (1,)), ((), ())))


def _dot_tn(a, b):
    return _dot(a, b, (((0,), (0,)), ((), ())))


def _dot_exact01(x, g01):
    hi = x.astype(BF16)
    lo = (x - hi.astype(F32)).astype(BF16)
    dn = (((1,), (0,)), ((), ()))
    return (lax.dot_general(hi, g01, dn, preferred_element_type=F32)
            + lax.dot_general(lo, g01, dn, preferred_element_type=F32))


def _fold_heads(v, fold):
    return _row_pick(_dot_exact01(jnp.broadcast_to(v, (8, v.shape[1])), fold), 0)


def _fold_ones(width):
    idx = np.arange(width) % HEAD_DIM
    return jnp.asarray((idx[:, None] == np.arange(HEAD_DIM)[None, :]).astype(np.float32), BF16)


def _head_masks(width):
    idx = np.arange(width) // HEAD_DIM
    return jnp.asarray((idx[None, :] == np.arange(width // HEAD_DIM)[:, None]).astype(np.float32)[:, None, :], F32)


def _rms_stats(x):
    r = lax.rsqrt(jnp.mean(x * x, axis=-1, keepdims=True) + RMS_EPS)
    return r, x * r


def rms_fwd(x, gains, *, name):
    T, D = x.shape

    def fn(i, nblk, xb, *gs):
        _, xh = _rms_stats(xb)
        return tuple(xh * g for g in gs)

    return _rowmap(fn, name=name, T=T, tb=512, ins=[("row", x)] + [("const", g) for g in gains],
                   outs=[("row", D, MMD)] * len(gains))


def rms_bwd(x, gains, dhs, dres, *, name):
    T, D = x.shape
    n = len(gains)

    def fn(i, nblk, xb, dr, *rest):
        gs, ds = rest[:n], rest[n:]
        r, xh = _rms_stats(xb)
        dx = dr
        dgs = []
        for g, dh in zip(gs, ds):
            dgs.append(jnp.sum(dh * xh, axis=0, keepdims=True))
            dxh = dh * g
            dx = dx + r * (dxh - xh * jnp.mean(dxh * xh, axis=-1, keepdims=True))
        return (dx, *dgs)

    return _rowmap(fn, name=name, T=T, tb=512,
                   ins=[("row", x), ("row", dres)] + [("const", g) for g in gains] + [("row", d) for d in dhs],
                   outs=[("row", D, F32)], accs=[((1, D), F32)] * n)


def _segsum(x):
    first = lax.broadcasted_iota(jnp.int32, (x.shape[0], 128), 1) < HEAD_DIM
    outs = []
    for p in range(x.shape[1] // 128):
        xs = x[:, p * 128:(p + 1) * 128]
        lo = jnp.sum(jnp.where(first, xs, 0.0), axis=-1, keepdims=True)
        hi = jnp.sum(jnp.where(first, 0.0, xs), axis=-1, keepdims=True)
        outs.append(jnp.where(first, lo, hi))
    return jnp.concatenate(outs, axis=1)


def _pre1_common(i, ps, halo, mu, w0, a0, w2p, a2p, g2p, k_k, k_a):
    prev = _shift_down(ps, halo, i == 0)
    xs = ps + (prev - ps) * mu
    lo = xs[:, 3 * RWKV_WIDTH:]
    tl, sl = jnp.tanh(lo), jax.nn.sigmoid(lo)
    dec = w0 + _dot(tl, w2p)
    ain = a0 + _dot(lo, a2p)
    g = _dot(sl, g2p)
    wl = -jax.nn.softplus(-dec) - 0.5
    w = jnp.exp(-jnp.exp(wl))
    a = jax.nn.sigmoid(ain)
    k = xs[:, RWKV_WIDTH:2 * RWKV_WIDTH]
    z = k * k_k
    nrm = jnp.sqrt(_segsum(z * z))
    kk = z / jnp.maximum(nrm, 1e-12)
    return prev, xs, lo, tl, sl, dec, wl, w, a, g, k, nrm, kk


def rwkv_pre_fwd(p, mu, w0, a0, w2p, a2p, g2p, k_k, k_a):
    T = p.shape[0]

    def fn(i, nblk, ps, halo, mu, w0, a0, w2p, a2p, g2p, k_k, k_a):
        _, xs, _, _, _, _, _, w, a, g, k, _, kk = _pre1_common(i, ps, halo, mu, w0, a0, w2p, a2p, g2p, k_k, k_a)
        W = RWKV_WIDTH
        return xs[:, :W], w, k * (1.0 + (a - 1.0) * k_a), xs[:, 2 * W:3 * W], kk, kk * a, g

    return _rowmap(fn, name="rwkv_pre_fwd", T=T, tb=256,
                   ins=[("row", p, SHIFT_WIDTH, 0), ("prev", p, SHIFT_WIDTH, 0)]
                   + [("const", c) for c in (mu, w0, a0, w2p, a2p, g2p, k_k, k_a)],
                   outs=[("row", RWKV_WIDTH, F32)] * 7)


def rwkv_pre_bwd(p, mu, w0, a0, w2p, a2p, g2p, k_k, k_a, drs, dw, dk2s, dvs, dkk, dkka, dg):
    T = p.shape[0]

    def fn(i, nblk, ps, halo, mu, w0, a0, w2p, a2p, g2p, k_k, k_a, dr0, dr1, dw, dk20, dk21, dv0, dv1, dkk, dkka, dg):
        prev, xs, lo, tl, sl, dec, wl, w, a, g, k, nrm, kk = _pre1_common(i, ps, halo, mu, w0, a0, w2p, a2p, g2p, k_k, k_a)
        dk2 = dk20 + dk21
        dkk_t = dkk + dkka * a
        proj = jnp.where(nrm > 1e-12, kk * _segsum(dkk_t * kk), 0.0)
        dz = (dkk_t - proj) / jnp.maximum(nrm, 1e-12)
        dk = dz * k_k + dk2 * (1.0 + (a - 1.0) * k_a)
        da = dkka * kk + dk2 * k * k_a
        ddec = dw * (-w * jnp.exp(wl)) * jax.nn.sigmoid(-dec)
        dain = da * a * (1.0 - a)
        dlo = (_dot_nt(ddec, w2p) * (1.0 - tl * tl) + _dot_nt(dain, a2p) + _dot_nt(dg, g2p) * sl * (1.0 - sl))
        dxs = jnp.concatenate([dr0 + dr1, dk, dv0 + dv1, dlo], axis=1)
        s = lambda z: jnp.sum(z, axis=0, keepdims=True)
        return (dxs, s(dxs * (prev - ps)), s(ddec), s(dain), _dot_tn(tl, ddec), _dot_tn(lo, dain), _dot_tn(sl, dg),
                s(dz * k), s(dk2 * k * (a - 1.0)))

    return _rowmap(fn, name="rwkv_pre_bwd", T=T, tb=128,
                   ins=[("row", p, SHIFT_WIDTH, 0), ("prev", p, SHIFT_WIDTH, 0)]
                   + [("const", c) for c in (mu, w0, a0, w2p, a2p, g2p, k_k, k_a)]
                   + [("row", c) for c in (*drs, dw, *dk2s, *dvs, dkk, dkka, dg)],
                   outs=[("row", SHIFT_WIDTH, F32)],
                   accs=[((1, SHIFT_WIDTH), F32), ((1, RWKV_WIDTH), F32), ((1, RWKV_WIDTH), F32)]
                   + [((LORA_WIDTH, RWKV_WIDTH), F32)] * 3 + [((1, RWKV_WIDTH), F32)] * 2)


def shift_bwd(dxs, mu, dq_mem):
    T = dxs.shape[0]

    def fn(i, nblk, d, halo, mu, dq):
        nxt = _shift_up(d, halo, i == nblk - 1)
        return jnp.concatenate([d * (1.0 - mu) + nxt * mu, dq], axis=1)

    return _rowmap(fn, name="shift_bwd", T=T, tb=256,
                   ins=[("row", dxs), ("next", dxs), ("const", mu), ("row", dq_mem)],
                   outs=[("row", SHIFT_WIDTH + MEM_WIDTH, MMD)])[0]


N_PAIRS = RWKV_HEADS // 2


def _pair_consts():
    row = lax.broadcasted_iota(jnp.int32, (HEAD_DIM, 128), 0)
    lane = lax.broadcasted_iota(jnp.int32, (HEAD_DIM, 128), 1)
    eye2 = jnp.logical_or(lane == row, lane == row + HEAD_DIM).astype(F32)
    li = lax.broadcasted_iota(jnp.int32, (128, 128), 0) < HEAD_DIM
    lj = lax.broadcasted_iota(jnp.int32, (128, 128), 1) < HEAD_DIM
    return eye2, (li == lj).astype(BF16)


def _pair_sum(p, ones2):
    n, m, l = p.shape
    s = lax.dot_general(p.reshape(n * m, l).astype(BF16), ones2, (((1,), (0,)), ((), ())), preferred_element_type=F32)
    return s.reshape(n, m, l)


def _pair_rows(row):
    return jnp.stack([row[:, p * 128:(p + 1) * 128] for p in range(N_PAIRS)], axis=0)


def _pair_flat(rows):
    return jnp.concatenate([rows[p] for p in range(N_PAIRS)], axis=1)


def _split_bf16(v):
    hi = v.astype(BF16).astype(F32)
    return hi, v - hi


def _gather_copies(srcs, dsts, send_sems, recv_sems):
    x, y, c = _mesh_pos()
    s = 2 * x + y
    me, sibling = (x, y, c), (x, y, 1 - c)
    rc = functools.partial(_remote, send_sems, recv_sems)
    ici, land, fwd, arrived = [], [], [], []
    for b, (src, dst) in enumerate(zip(srcs, dsts)):
        for j, (cx, cy) in enumerate(_other_chips(x, y)):
            k = 6 * b + j
            ici.append(rc(k, src.at[c], dst.at[s, c], (cx, cy, c)))
            blk, blk2 = dst.at[2 * cx + cy, c], dst.at[2 * cx + cy, 1 - c]
            land.append(rc(k, blk, blk, me))
            fwd.append(rc(k + 3, blk, blk, sibling))
            arrived.append(rc(k + 3, blk2, blk2, me))
    return ici, land, fwd, arrived


def scan_fwd(r, w, k2, v, kk, kka, gather=None):
    T, W = r.shape
    tc = SCAN_CHUNK
    nchunk = T // tc
    seq = pl.BlockSpec((tc, W), lambda i: (i, 0))
    one_state = pl.BlockSpec((N_PAIRS, HEAD_DIM, 128), lambda i: (0, 0, 0))
    nb = 0 if gather is None else len(gather)

    def body(r_ref, w_ref, k2_ref, v_ref, kk_ref, kka_ref, *rest):
        if gather is None:
            y_ref, st_ref, fin_ref, s_scr, vhi_scr, vlo_scr = rest
        else:
            srcs, (y_ref, st_ref, fin_ref), dsts = rest[:nb], rest[nb:nb + 3], rest[nb + 3:2 * nb + 3]
            s_scr, vhi_scr, vlo_scr, send_sems, recv_sems = rest[2 * nb + 3:]
            ici, land, fwd, arrived = _gather_copies(srcs, dsts, send_sems, recv_sems)

            @pl.when(pl.program_id(0) == 0)
            def _():
                for cp in ici:
                    cp.start()

            @pl.when(pl.program_id(0) == nchunk // 2)
            def _():
                for a, f in zip(land, fwd):
                    a.wait_recv()
                    f.start()

        @pl.when(pl.program_id(0) == 0)
        def _():
            s_scr[...] = jnp.zeros(s_scr.shape, F32)

        vhi_scr[...], vlo_scr[...] = _split_bf16(v_ref[...])
        eye2, ones2 = _pair_consts()
        eye2b = eye2.astype(BF16)

        def step(t, carry):
            r_t, w_t, k2_t, kk_t, kka_t, vhi_t, vlo_t = (
                _pair_rows(ref[pl.ds(t, 1), :]) for ref in (r_ref, w_ref, k2_ref, kk_ref, kka_ref, vhi_scr, vlo_scr))
            S = s_scr[...]
            skk = _pair_sum(S * kk_t, ones2)
            vb = _pair_sum(eye2b * vhi_t.astype(BF16), ones2) + _pair_sum(eye2b * vlo_t.astype(BF16), ones2)
            S2 = S * w_t - skk * kka_t + vb * k2_t
            y_ref[pl.ds(t, 1), :] = _pair_flat(jnp.sum(eye2 * _pair_sum(S2 * r_t, ones2), axis=1, keepdims=True))
            s_scr[...] = S2
            st_ref[t] = S
            return carry

        lax.fori_loop(0, tc, step, 0, unroll=SCAN_UNROLL)
        fin_ref[...] = s_scr[...]

        if gather is not None:
            @pl.when(pl.program_id(0) == nchunk - 1)
            def _():
                for a in arrived:
                    a.wait_recv()
                for cp in ici + fwd:
                    cp.wait_send()

    in_specs = [seq] * 6
    out_specs = [seq, pl.BlockSpec((tc, N_PAIRS, HEAD_DIM, 128), lambda i: (i, 0, 0, 0)), one_state]
    out_shape = [jax.ShapeDtypeStruct((T, W), F32), jax.ShapeDtypeStruct((T, N_PAIRS, HEAD_DIM, 128), F32),
                 jax.ShapeDtypeStruct((N_PAIRS, HEAD_DIM, 128), F32)]
    scratch = [pltpu.VMEM((N_PAIRS, HEAD_DIM, 128), F32), pltpu.VMEM((tc, W), F32), pltpu.VMEM((tc, W), F32)]
    args = [r, w, k2, v, kk, kka]
    if gather is not None:
        in_specs += [HBM_SPEC] * nb
        out_specs += [HBM_SPEC] * nb
        out_shape += [jax.ShapeDtypeStruct((N_CHIPS, *g.shape), g.dtype) for g in gather]
        scratch += [pltpu.SemaphoreType.DMA((6 * nb,)), pltpu.SemaphoreType.DMA((6 * nb,))]
        args += list(gather)
    return pl.pallas_call(
        body, name="rwkv_scan_fwd", grid=(nchunk,), in_specs=in_specs, out_specs=out_specs, out_shape=out_shape,
        scratch_shapes=scratch, compiler_params=_cp(("arbitrary",)),
    )(*args)


PEER_FLIPS = tuple((fx, fy, fc) for fx in (0, 1) for fy in (0, 1) for fc in (0, 1))[1:]


def scan_bwd(r, w, k2, v, kk, kka, states, final_state, dy, scatter=None):
    T, W = r.shape
    tc = SCAN_CHUNK
    nchunk = T // tc
    seq = pl.BlockSpec((tc, W), lambda i: (nchunk - 1 - i, 0))
    st_spec = pl.BlockSpec((tc, N_PAIRS, HEAD_DIM, 128), lambda i: (nchunk - 1 - i, 0, 0, 0))
    one_state = pl.BlockSpec((N_PAIRS, HEAD_DIM, 128), lambda i: (0, 0, 0))
    nb, npeer = (0 if scatter is None else len(scatter)), len(PEER_FLIPS)

    def body(r_ref, w_ref, k2_ref, v_ref, kk_ref, kka_ref, st_ref, fin_ref, dy_ref, *rest):
        if scatter is None:
            dr_ref, dw_ref, dk2_ref, dv_ref, dkk_ref, dkka_ref, ds_scr, sc_scr, vhi_scr, vlo_scr = rest
        else:
            srcs, dsts = rest[:nb], rest[nb + 6:2 * nb + 6]
            dr_ref, dw_ref, dk2_ref, dv_ref, dkk_ref, dkka_ref = rest[nb:nb + 6]
            ds_scr, sc_scr, vhi_scr, vlo_scr, send_sems, recv_sems = rest[2 * nb + 6:]
            x, y, c = _mesh_pos()
            copies = []
            for b, (src, dst) in enumerate(zip(srcs, dsts)):
                for k, (fx, fy, fc) in enumerate(PEER_FLIPS):
                    px, py, pc = (1 - x if fx else x), (1 - y if fy else y), (1 - c if fc else c)
                    copies.append(_remote(send_sems, recv_sems, npeer * b + k, src.at[2 * px + py, pc], dst.at[k],
                                          (px, py, pc)))

            @pl.when(pl.program_id(0) == 0)
            def _():
                for cp in copies:
                    cp.start()

        @pl.when(pl.program_id(0) == 0)
        def _():
            ds_scr[...] = jnp.zeros(ds_scr.shape, F32)
            sc_scr[...] = fin_ref[...]

        vhi_scr[...], vlo_scr[...] = _split_bf16(v_ref[...])
        eye2, ones2 = _pair_consts()
        eye2b = eye2.astype(BF16)
        colsum = lambda z: jnp.sum(z, axis=1, keepdims=True)

        def step(j, carry):
            t = tc - 1 - j
            r_t, w_t, k2_t, kk_t, kka_t, vhi_t, vlo_t, dy_t = (
                _pair_rows(ref[pl.ds(t, 1), :])
                for ref in (r_ref, w_ref, k2_ref, kk_ref, kka_ref, vhi_scr, vlo_scr, dy_ref))
            s_prev, s_cur = st_ref[t], sc_scr[...]
            dyb = _pair_sum(eye2b * dy_t.astype(BF16), ones2)
            vb = _pair_sum(eye2b * vhi_t.astype(BF16), ones2) + _pair_sum(eye2b * vlo_t.astype(BF16), ones2)
            skk = _pair_sum(s_prev * kk_t, ones2)
            dS = ds_scr[...] + dyb * r_t
            dS_b = dS.astype(BF16)
            dsa = _pair_sum(dS_b * kka_t.astype(BF16), ones2)
            ds_scr[...] = dS * w_t - dsa * kk_t
            sc_scr[...] = s_prev
            for ref, sign, val in zip((dr_ref, dw_ref, dk2_ref, dv_ref, dkk_ref, dkka_ref), (1, 1, 1, 1, -1, -1),
                                      (s_cur * dyb, dS * s_prev, dS * vb,
                                       eye2 * _pair_sum(dS_b * k2_t.astype(BF16), ones2), s_prev * dsa, dS * skk)):
                ref[pl.ds(t, 1), :] = _pair_flat(colsum(val) if sign > 0 else -colsum(val))
            return carry

        lax.fori_loop(0, tc, step, 0, unroll=SCAN_UNROLL_BWD)

        if scatter is not None:
            @pl.when(pl.program_id(0) == nchunk - 1)
            def _():
                for cp in copies:
                    cp.wait()

    in_specs = [seq] * 6 + [st_spec, one_state, seq]
    out_specs = [seq] * 6
    out_shape = [jax.ShapeDtypeStruct((T, W), F32)] * 6
    scratch = [pltpu.VMEM((N_PAIRS, HEAD_DIM, 128), F32)] * 2 + [pltpu.VMEM((tc, W), F32)] * 2
    args = [r, w, k2, v, kk, kka, states, final_state, dy]
    if scatter is not None:
        in_specs += [HBM_SPEC] * nb
        out_specs += [HBM_SPEC] * nb
        out_shape += [jax.ShapeDtypeStruct((npeer, *s.shape[2:]), s.dtype) for s in scatter]
        scratch += [pltpu.SemaphoreType.DMA((npeer * nb,)), pltpu.SemaphoreType.DMA((npeer * nb,))]
        args += list(scatter)
    return pl.pallas_call(
        body, name="rwkv_scan_bwd", grid=(nchunk,), in_specs=in_specs, out_specs=out_specs, out_shape=out_shape,
        scratch_shapes=scratch, compiler_params=_cp(("arbitrary",)),
    )(*args)


def _mix_common(y, r, k2, v, lnx_w, lnx_b, r_k):
    yc = y - _segsum(y) * (1.0 / HEAD_DIM)
    rstd = lax.rsqrt(_segsum(yc * yc) * (1.0 / HEAD_DIM) + LNX_EPS)
    yhat = yc * rstd
    s = _segsum(r * k2 * r_k)
    return rstd, yhat, s, yhat * lnx_w + lnx_b + s * v


def mix_gate_fwd(y, r, k2, v, g, y_mem, lnx_w, lnx_b, r_k):
    T = y.shape[0]

    def fn(i, nblk, y, r, k2, v, g, ym, lw, lb, rk):
        mix = _mix_common(y, r, k2, v, lw, lb, rk)[3]
        return jnp.concatenate([mix * g, ym], axis=1)

    return _rowmap(fn, name="mix_gate_fwd", T=T, tb=256,
                   ins=[("row", z) for z in (y, r, k2, v, g, y_mem)] + [("const", c) for c in (lnx_w, lnx_b, r_k)],
                   outs=[("row", RWKV_WIDTH + MEM_WIDTH, MMD)])[0]


def mix_gate_bwd(y, r, k2, v, g, dycat, lnx_w, lnx_b, r_k):
    T = y.shape[0]

    def fn(i, nblk, y, r, k2, v, g, dyc, lw, lb, rk):
        rstd, yhat, s, mix = _mix_common(y, r, k2, v, lw, lb, rk)
        dmix = dyc * g
        dyh = dmix * lw
        inv = 1.0 / HEAD_DIM
        dy = rstd * (dyh - _segsum(dyh) * inv - yhat * (_segsum(dyh * yhat) * inv))
        ds = _segsum(dmix * v)
        cs = lambda z: jnp.sum(z, axis=0, keepdims=True)
        return (dy, ds * k2 * rk, ds * r * rk, dmix * s, dyc * mix, cs(dmix * yhat), cs(dmix), cs(ds * r * k2))

    return _rowmap(fn, name="mix_gate_bwd", T=T, tb=256,
                   ins=[("row", z) for z in (y, r, k2, v, g)] + [("row", dycat, RWKV_WIDTH, 0)]
                   + [("const", c) for c in (lnx_w, lnx_b, r_k)],
                   outs=[("row", RWKV_WIDTH, F32)] * 5, accs=[((1, RWKV_WIDTH), F32)] * 3)


def _head_rms(x):
    ms = _segsum(x * x) * (1.0 / HEAD_DIM)
    r = lax.rsqrt(ms + RMS_EPS)
    return r, x * r


def _head_rms_bwd(dxn_g, r, xh):
    return r * (dxn_g - xh * (_segsum(dxn_g * xh) * (1.0 / HEAD_DIM)))


def mem_kv_fwd(mem, norm_g, w_kv, k_norm_t, *, name):
    def body(mem_ref, g_ref, w_ref, kn_ref, k_out, v_out):
        _, xh = _rms_stats(mem_ref[...])
        kv = _dot(xh * g_ref[...], w_ref[...])
        _, kh = _head_rms(kv[:, :MEM_WIDTH])
        k_out[...] = kh * kn_ref[...]
        v_out[...] = kv[:, MEM_WIDTH:]

    return pl.pallas_call(
        body, name=name, out_shape=[jax.ShapeDtypeStruct((N_MEM, MEM_WIDTH), F32)] * 2, compiler_params=_cp(),
    )(mem, norm_g, w_kv, k_norm_t)


def mem_kv_bwd(mem, norm_g, w_kv, k_norm_t, dkn, dv, *, name):
    fold = _fold_ones(MEM_WIDTH)

    def body(mem_ref, g_ref, w_ref, kn_ref, fo_ref, dkn_ref, dv_ref, dw_out, dg_out, dkg_out):
        _, xh = _rms_stats(mem_ref[...])
        hm = xh * g_ref[...]
        kv = _dot(hm, w_ref[...])
        r, kh = _head_rms(kv[:, :MEM_WIDTH])
        dkn = dkn_ref[...]
        dkg_out[...] = _fold_heads(jnp.sum(dkn * kh, axis=0, keepdims=True), fo_ref[...])
        dkraw = _head_rms_bwd(dkn * kn_ref[...], r, kh)
        dkv = jnp.concatenate([dkraw, dv_ref[...]], axis=1)
        dw_out[...] = _dot_tn(hm, dkv)
        dg_out[...] = jnp.sum(_dot_nt(dkv, w_ref[...]) * xh, axis=0, keepdims=True)

    return pl.pallas_call(
        body, name=name,
        out_shape=[jax.ShapeDtypeStruct((D_MODEL, 2 * MEM_WIDTH), F32), jax.ShapeDtypeStruct((1, D_MODEL), F32),
                   jax.ShapeDtypeStruct((1, HEAD_DIM), F32)],
        compiler_params=_cp(),
    )(mem, norm_g, w_kv, k_norm_t, fold, dkn, dv)


def _mem_scores(qn, kn, masks, h):
    s = _dot_nt(qn * masks[h], kn) * (1.0 / math.sqrt(HEAD_DIM))
    s = s - jnp.max(s, axis=-1, keepdims=True)
    e = jnp.exp(s)
    return e / jnp.sum(e, axis=-1, keepdims=True)


def mem_attn_fwd(p, colblock, kn, v, q_norm_t, *, name):
    T = p.shape[0]
    masks = _head_masks(MEM_WIDTH)

    def fn(i, nblk, q, kn, v, qg, masks):
        _, qh = _head_rms(q)
        qn = qh * qg
        out = jnp.zeros(q.shape, F32)
        for h in range(MEM_WIDTH // HEAD_DIM):
            out = out + _dot(_mem_scores(qn, kn, masks, h), v * masks[h])
        return out

    return _rowmap(fn, name=name, T=T, tb=512,
                   ins=[("row", p, MEM_WIDTH, colblock)] + [("const", c) for c in (kn, v, q_norm_t, masks)],
                   outs=[("row", MEM_WIDTH, F32)])[0]


def mem_attn_bwd(p, colblock, kn, v, q_norm_t, dycat, dcolblock, *, name):
    T = p.shape[0]
    masks, fold = _head_masks(MEM_WIDTH), _fold_ones(MEM_WIDTH)
    scale = 1.0 / math.sqrt(HEAD_DIM)

    def fn(i, nblk, q, dy, kn, v, qg, masks, fo):
        r, qh = _head_rms(q)
        qn = qh * qg
        dqn = jnp.zeros(q.shape, F32)
        dkn = jnp.zeros(kn.shape, F32)
        dv = jnp.zeros(v.shape, F32)
        for h in range(MEM_WIDTH // HEAD_DIM):
            pr = _mem_scores(qn, kn, masks, h)
            dyh = dy * masks[h]
            dpr = _dot_nt(dyh, v)
            ds = pr * (dpr - jnp.sum(dpr * pr, axis=-1, keepdims=True)) * scale
            dqn = dqn + _dot(ds, kn * masks[h])
            dkn = dkn + _dot_tn(ds, qn * masks[h])
            dv = dv + _dot_tn(pr, dyh)
        dqg = _fold_heads(jnp.sum(dqn * qh, axis=0, keepdims=True), fo)
        return _head_rms_bwd(dqn * qg, r, qh), dkn, dv, dqg

    return _rowmap(fn, name=name, T=T, tb=512,
                   ins=[("row", p, MEM_WIDTH, colblock), ("row", dycat, MEM_WIDTH, dcolblock)]
                   + [("const", c) for c in (kn, v, q_norm_t, masks, fold)],
                   outs=[("row", MEM_WIDTH, F32)],
                   accs=[((N_MEM, MEM_WIDTH), F32), ((N_MEM, MEM_WIDTH), F32), ((1, HEAD_DIM), F32)])


def _ffn_conv(i, u, halo, cw, cb):
    ext = jnp.concatenate([jnp.where(i == 0, 0.0, 1.0) * halo, u], axis=0)
    up1 = pltpu.roll(ext, 1, axis=0)[8:]
    up2 = pltpu.roll(ext, 2, axis=0)[8:]
    c = cb + cw[0] * up2 + cw[1] * up1 + cw[2] * u
    return up1, up2, c[:, :D_FF], c[:, D_FF:]


def ffn_act_fwd(u, cw, cb, *, name):
    T = u.shape[0]

    def fn(i, nblk, u, halo, c0, c1, c2, cb):
        _, _, gate, val = _ffn_conv(i, u, halo, (c0, c1, c2), cb)
        return jax.nn.silu(gate) * val

    return _rowmap(fn, name=name, T=T, tb=128, ins=[("row", u), ("prev", u)] + [("const", c) for c in (*cw, cb)],
                   outs=[("row", D_FF, MMD)])[0]


def ffn_act_bwd(u, cw, cb, dz, *, name):
    T = u.shape[0]
    tb = 128

    def fn(i, nblk, u, halo, unext, c0, c1, c2, cb, dz, dznext):
        ue = jnp.concatenate([u, unext], axis=0)
        dze = jnp.concatenate([dz, jnp.where(i == nblk - 1, 0.0, 1.0) * dznext], axis=0)
        up1, up2, gate, val = _ffn_conv(i, ue, halo, (c0, c1, c2), cb)
        sg = jax.nn.sigmoid(gate)
        dce = jnp.concatenate([dze * val * sg * (1.0 + gate * (1.0 - sg)), dze * gate * sg], axis=1)
        rows = dce.shape[0]
        du = (c2 * dce + c1 * pltpu.roll(dce, rows - 1, axis=0) + c0 * pltpu.roll(dce, rows - 2, axis=0))[:tb]
        dc = dce[:tb]
        s = lambda z: jnp.sum(z, axis=0, keepdims=True)
        return du, s(dc * up2[:tb]), s(dc * up1[:tb]), s(dc * u), s(dc)

    return _rowmap(fn, name=name, T=T, tb=tb,
                   ins=[("row", u), ("prev", u), ("next", u)] + [("const", c) for c in (*cw, cb)]
                   + [("row", dz), ("next", dz)],
                   outs=[("row", 2 * D_FF, MMD)], accs=[((1, 2 * D_FF), F32)] * 4)


def _rope_swap(z):
    lane = lax.broadcasted_iota(jnp.int32, z.shape, 1) % HEAD_DIM
    w = z.shape[1]
    return jnp.where(lane < HEAD_DIM // 2, pltpu.roll(z, w - HEAD_DIM // 2, axis=1), pltpu.roll(z, HEAD_DIM // 2, axis=1))


def rope_tables(T):
    inv = (np.float32(ROPE_THETA) ** (-np.arange(0, HEAD_DIM, 2, dtype=np.float32) / np.float32(HEAD_DIM))).astype(np.float32)
    ang = (np.arange(T, dtype=np.float32)[:, None] * inv[None, :]).astype(np.float64)
    cos, sin = np.cos(ang).astype(np.float32), np.sin(ang).astype(np.float32)
    return (jnp.asarray(np.concatenate([cos, cos, cos, cos], axis=1)),
            jnp.asarray(np.concatenate([-sin, sin, -sin, sin], axis=1)))


def _rope_wide(t):
    return jnp.tile(t, (1, DIL_WIDTH // t.shape[1]))


def qk_fwd(kvp, pb, kg_t, qg_t, cos, sin):
    T = kvp.shape[0]

    def fn(i, nblk, kraw, vraw, qraw, kg, qg, c, s):
        c, s = _rope_wide(c), _rope_wide(s)
        outs = []
        for raw, g in ((qraw, qg), (kraw, kg)):
            _, xh = _head_rms(raw)
            z = xh * g
            outs.append(z * c + _rope_swap(z) * s)
        return outs[0], outs[1], vraw

    return _rowmap(fn, name="qk_fwd", T=T, tb=256,
                   ins=[("row", kvp, DIL_WIDTH, 0), ("row", kvp, DIL_WIDTH, 1), ("row", pb, DIL_WIDTH, 0)]
                   + [("const", kg_t), ("const", qg_t), ("row", cos), ("row", sin)],
                   outs=[("row", DIL_WIDTH, MMD)] * 3)


def qk_bwd(kvp, pb, kg_t, qg_t, cos, sin, dq, dk, dv, dq_mem):
    T = kvp.shape[0]
    fold = _fold_ones(DIL_WIDTH)

    def fn(i, nblk, kraw, qraw, kg, qg, c, s, fo, dq, dk, dv, dqm):
        c, s = _rope_wide(c), _rope_wide(s)
        res, dgs = [], []
        for raw, g, d in ((qraw, qg, dq), (kraw, kg, dk)):
            r, xh = _head_rms(raw)
            dz = d * c + _rope_swap(d * s)
            dgs.append(_fold_heads(jnp.sum(dz * xh, axis=0, keepdims=True), fo))
            res.append(_head_rms_bwd(dz * g, r, xh))
        return (jnp.concatenate([res[0], dqm], axis=1), jnp.concatenate([res[1], dv], axis=1), dgs[0], dgs[1])

    return _rowmap(fn, name="qk_bwd", T=T, tb=256,
                   ins=[("row", kvp, DIL_WIDTH, 0), ("row", pb, DIL_WIDTH, 0), ("const", kg_t), ("const", qg_t),
                        ("row", cos), ("row", sin), ("const", fold),
                        ("row", dq), ("row", dk), ("row", dv), ("row", dq_mem)],
                   outs=[("row", DIL_WIDTH + MEM_WIDTH, MMD), ("row", 2 * DIL_WIDTH, MMD)],
                   accs=[((1, HEAD_DIM), F32)] * 2)


def _band(kind):
    i = lax.broadcasted_iota(jnp.int32, (DIL_BLOCK, DIL_BLOCK), 0)
    j = lax.broadcasted_iota(jnp.int32, (DIL_BLOCK, DIL_BLOCK), 1)
    return (j <= i) if kind == "cur" else (j >= i)


def dil_attn_fwd(q, k, v, seq_blocks, *, name):
    T, W = q.shape
    nb = T // DIL_BLOCK
    masks = _head_masks(W)
    cur = pl.BlockSpec((DIL_BLOCK, W), lambda n: (n, 0))
    prv = pl.BlockSpec((DIL_BLOCK, W), lambda n: (jnp.maximum(n - 1, 0), 0))
    scale = 1.0 / math.sqrt(HEAD_DIM)

    def body(q_ref, kc_ref, kp_ref, vc_ref, vp_ref, m_ref, o_ref, l_ref):
        n = pl.program_id(0)
        has_prev = (n % seq_blocks) != 0
        q = q_ref[...].astype(F32)
        kc, kp = kc_ref[...].astype(F32), kp_ref[...].astype(F32)
        vc, vp = vc_ref[...].astype(F32), vp_ref[...].astype(F32)
        ok_c = _band("cur")
        ok_p = jnp.logical_and(_band("prev"), has_prev)
        o = jnp.zeros((DIL_BLOCK, W), F32)
        lse = jnp.zeros((DIL_BLOCK, W), F32)
        for h in range(W // HEAD_DIM):
            mh = m_ref[h]
            qh = q * mh
            sc = jnp.where(ok_c, _dot_nt(qh, kc) * scale, NEG_INF)
            sp = jnp.where(ok_p, _dot_nt(qh, kp) * scale, NEG_INF)
            mx = jnp.maximum(jnp.max(sc, axis=-1, keepdims=True), jnp.max(sp, axis=-1, keepdims=True))
            ec, ep = jnp.exp(sc - mx), jnp.exp(sp - mx)
            den = jnp.sum(ec, axis=-1, keepdims=True) + jnp.sum(ep, axis=-1, keepdims=True)
            o = o + (_dot(ec, vc * mh) + _dot(ep, vp * mh)) / den
            lse = lse + (mx + jnp.log(den)) * mh
        o_ref[...] = o
        l_ref[...] = lse

    return pl.pallas_call(
        body, name=name, grid=(nb,), in_specs=[cur, cur, prv, cur, prv, pl.BlockSpec(masks.shape, lambda n: (0, 0, 0))],
        out_specs=[cur, cur], out_shape=[jax.ShapeDtypeStruct((T, W), F32)] * 2,
        compiler_params=_cp(("parallel",)),
    )(q, k, k, v, v, masks)


def dil_attn_bwd(q, k, v, o, lse, do, dlse, seq_blocks, *, name):
    T, W = q.shape
    nb = T // DIL_BLOCK
    masks = _head_masks(W)
    cur = pl.BlockSpec((DIL_BLOCK, W), lambda n: (n, 0))
    prv = pl.BlockSpec((DIL_BLOCK, W), lambda n: (jnp.maximum(n - 1, 0), 0))
    nxt = pl.BlockSpec((DIL_BLOCK, W), lambda n: (jnp.minimum(n + 1, nb - 1), 0))
    scale = 1.0 / math.sqrt(HEAD_DIM)

    def body(qc_ref, qn_ref, kc_ref, kp_ref, vc_ref, vp_ref, oc_ref, on_ref, lc_ref, ln_ref, doc_ref, don_ref,
             dlc_ref, dln_ref, m_ref, dq_ref, dk_ref, dv_ref):
        n = pl.program_id(0)
        has_prev = (n % seq_blocks) != 0
        has_next = jnp.logical_and(((n + 1) % seq_blocks) != 0, n + 1 < nb)
        f = lambda ref: ref[...].astype(F32)
        qc, qn, kc, kp, vc, vp = f(qc_ref), f(qn_ref), f(kc_ref), f(kp_ref), f(vc_ref), f(vp_ref)
        doc, don = doc_ref[...], don_ref[...]
        ok_c = _band("cur")
        ok_p = jnp.logical_and(_band("prev"), has_prev)
        ok_n = jnp.logical_and(_band("prev"), has_next)
        dq = jnp.zeros((DIL_BLOCK, W), F32)
        dk = jnp.zeros((DIL_BLOCK, W), F32)
        dv = jnp.zeros((DIL_BLOCK, W), F32)

        def side(qh, kk, vv, doh, lse_h, corr, ok):
            s = _dot_nt(qh, kk) * scale
            pr = jnp.where(ok, jnp.exp(jnp.where(ok, s, NEG_INF) - lse_h), 0.0)
            ds = pr * (_dot_nt(doh, vv) + corr) * scale
            return pr, ds

        for h in range(W // HEAD_DIM):
            mh = m_ref[h]
            red = lambda z: jnp.sum(z * mh, axis=-1, keepdims=True)
            qh, doh = qc * mh, doc * mh
            lse_h = red(lc_ref[...]) * (1.0 / HEAD_DIM)
            corr = red(dlc_ref[...]) - red(doc * oc_ref[...])
            pr_c, ds_c = side(qh, kc, vc * mh, doh, lse_h, corr, ok_c)
            _, ds_p = side(qh, kp, vp * mh, doh, lse_h, corr, ok_p)
            dq = dq + _dot(ds_c, kc * mh) + _dot(ds_p, kp * mh)
            dk = dk + _dot_tn(ds_c, qh)
            dv = dv + _dot_tn(pr_c, doh)
            qh2, doh2 = qn * mh, don * mh
            lse_2 = red(ln_ref[...]) * (1.0 / HEAD_DIM)
            corr2 = red(dln_ref[...]) - red(don * on_ref[...])
            pr_n, ds_n = side(qh2, kc, vc * mh, doh2, lse_2, corr2, ok_n)
            dk = dk + _dot_tn(ds_n, qh2)
            dv = dv + _dot_tn(pr_n, doh2)
        dq_ref[...] = dq
        dk_ref[...] = dk
        dv_ref[...] = dv

    return pl.pallas_call(
        body, name=name, grid=(nb,),
        in_specs=[cur, nxt, cur, prv, cur, prv, cur, nxt, cur, nxt, cur, nxt, cur, nxt,
                  pl.BlockSpec(masks.shape, lambda n: (0, 0, 0))],
        out_specs=[cur] * 3, out_shape=[jax.ShapeDtypeStruct((T, W), F32)] * 3,
        compiler_params=_cp(("parallel",)),
    )(q, q, k, k, v, v, o, o, lse, lse, do, do, dlse, dlse, masks)


def _mix_weights(ls):
    m = jnp.maximum(jnp.maximum(ls[0], ls[1]), ls[2])
    es = [jnp.exp(l - m) for l in ls]
    den = es[0] + es[1] + es[2]
    return [e / den for e in es]


def mix_fwd(os_, ls, y_mem):
    T = y_mem.shape[0]

    def fn(i, nblk, o0, o1, o2, l0, l1, l2, ym):
        w = _mix_weights((l0, l1, l2))
        return jnp.concatenate([w[0] * o0 + w[1] * o1 + w[2] * o2, ym], axis=1)

    return _rowmap(fn, name="mix_fwd", T=T, tb=512, ins=[("row", z) for z in (*os_, *ls, y_mem)],
                   outs=[("row", 2 * MEM_WIDTH, MMD)])[0]


def mix_bwd(os_, ls, dycat):
    T = dycat.shape[0]

    def fn(i, nblk, o0, o1, o2, l0, l1, l2, dy):
        w = _mix_weights((l0, l1, l2))
        os3 = (o0, o1, o2)
        dws = [dy * o for o in os3]
        tot = w[0] * dws[0] + w[1] * dws[1] + w[2] * dws[2]
        return tuple(wg * dy for wg in w) + tuple(wg * (dw - tot) for wg, dw in zip(w, dws))

    return _rowmap(fn, name="mix_bwd", T=T, tb=512,
                   ins=[("row", z) for z in (*os_, *ls)] + [("row", dycat, MEM_WIDTH, 0)],
                   outs=[("row", MEM_WIDTH, F32)] * 6)


def loss_fwd_bwd(y, target):
    T, D = y.shape

    def fn(i, nblk, y, t):
        e = y - t
        return e * (1.0 / D), jnp.zeros((8, 128), F32) + jnp.sum(e * e) * (0.5 / D)

    return _rowmap(fn, name="loss", T=T, tb=512, ins=[("row", y), ("row", target)], outs=[("row", D, F32)],
                   accs=[((8, 128), F32)])


def _to_residues(z, dil):
    T, W = z.shape
    return z.reshape(T // dil, dil, W).transpose(1, 0, 2).reshape(T, W)


def _from_residues(z, dil):
    T, W = z.shape
    return z.reshape(dil, T // dil, W).transpose(1, 0, 2).reshape(T, W)


def _pad_rows(w, rows):
    return jnp.concatenate([w, jnp.zeros((rows - w.shape[0], w.shape[1]), w.dtype)], axis=0)


def _tile_heads(g, width):
    return jnp.tile(g.reshape(1, HEAD_DIM), (1, width // HEAD_DIM))


def _conv_rows(W, i):
    return [W["ffn_conv_w"][i][j:j + 1] for j in range(3)]


def _ffn_fwd(x, i, W):
    hn = rms_fwd(x, [W["ffn_norm"][i:i + 1]], name=f"ffn_rms{i}")[0]
    u = _mm(hn, W["ffn_w_up"][i], name=f"ffn_up{i}")
    z = ffn_act_fwd(u, _conv_rows(W, i), W["ffn_conv_b"][i:i + 1], name=f"ffn_act{i}")
    out = _mm(z, W["ffn_w_down"][i], add=x, name=f"ffn_down{i}")
    return out, (x, hn, u, z)


def _ffn_bwd(dout, i, W, saved, G):
    x, hn, u, z = saved
    dz = _mm(dout, W["ffn_w_down"][i], tb=True, name=f"ffn_down_dx{i}")
    G["ffn_w_down"][i] = _mm(z, dout, ta=True, name=f"ffn_down_dw{i}")
    du, dw0, dw1, dw2, db = ffn_act_bwd(u, _conv_rows(W, i), W["ffn_conv_b"][i:i + 1], dz, name=f"ffn_act_bwd{i}")
    G["ffn_conv_w"][i] = jnp.concatenate([dw0, dw1, dw2], axis=0)
    G["ffn_conv_b"][i] = db[0]
    dhn = _mm(du, W["ffn_w_up"][i], tb=True, name=f"ffn_up_dx{i}")
    G["ffn_w_up"][i] = _mm(hn, du, ta=True, name=f"ffn_up_dw{i}")
    dx, dg = rms_bwd(x, [W["ffn_norm"][i:i + 1]], [dhn], dout, name=f"ffn_rms_bwd{i}")
    G["ffn_norm"][i] = dg[0]
    return dx


def local_step(x, mem, target, W, late=None):
    T = x.shape[0]
    W = dict(W)
    G = {"ffn_w_down": [None, None], "ffn_w_up": [None, None], "ffn_conv_w": [None, None],
         "ffn_conv_b": [None, None], "ffn_norm": [None, None], "attn_norm": [None, None], "mem_norm": [None, None],
         "mem_w_kv": [None, None], "mem_q_norm": [None, None], "mem_k_norm": [None, None]}
    mu, w0, a0 = W["a_mu"], W["a_w0"], W["a_a0"]
    w2p, a2p, g2p = (_pad_rows(W["a_w2"][0], LORA_WIDTH),
                     jnp.concatenate([jnp.zeros((64, RWKV_WIDTH), MMD), W["a_a2"][0],
                                      jnp.zeros((128, RWKV_WIDTH), MMD)], axis=0),
                     jnp.concatenate([jnp.zeros((128, RWKV_WIDTH), MMD), W["a_g2"][0]], axis=0))
    k_k, k_a, lnx_w, lnx_b = W["a_k_k"], W["a_k_a"], W["a_lnx_w"], W["a_lnx_b"]
    r_k = W["a_r_k"].reshape(1, RWKV_WIDTH)

    h0 = rms_fwd(x, [W["attn_norm"][0:1]], name="attn_rms0")[0]
    p = _mm(h0, W["a_w_in"][0], name="a_in")
    r, w, k2, v, kk, kka, g = rwkv_pre_fwd(p, mu, w0, a0, w2p, a2p, g2p, k_k, k_a)
    if late is None:
        y, states, final_state = scan_fwd(r, w, k2, v, kk, kka)
    else:
        y, states, final_state, *gathered = scan_fwd(r, w, k2, v, kk, kka, gather=late[0])
        W.update(late[1](gathered))
    memkv = []
    for i in range(2):
        memkv.append(mem_kv_fwd(mem, W["mem_norm"][i:i + 1], W["mem_w_kv"][i], _tile_heads(W["mem_k_norm"][i], MEM_WIDTH),
                                name=f"mem_kv{i}"))
    qg0 = _tile_heads(W["mem_q_norm"][0], MEM_WIDTH)
    y_mem0 = mem_attn_fwd(p, SHIFT_WIDTH // MEM_WIDTH, memkv[0][0], memkv[0][1], qg0, name="mem_attn0")
    ycat0 = mix_gate_fwd(y, r, k2, v, g, y_mem0, lnx_w, lnx_b, r_k)
    x1 = _mm(ycat0, W["a_w_out"][0], add=x, name="a_out")
    x2, ffn0 = _ffn_fwd(x1, 0, W)

    h1, hkv = rms_fwd(x2, [W["attn_norm"][1:2], W["kv_norm"].reshape(1, -1)], name="attn_rms1")
    kvp = _mm(hkv, W["kv_w"], name="kv_in")
    pb = _mm(h1, W["b_w_in"][0], name="b_in")
    cos, sin = rope_tables(T)
    kg_t, qg_t = _tile_heads(W["kv_k_norm"], DIL_WIDTH), _tile_heads(W["b_q_norm"][0], DIL_WIDTH)
    q, ksh, vsh = qk_fwd(kvp, pb, kg_t, qg_t, cos, sin)
    os_, ls, grp = [], [], []
    for gi, (win, dil) in enumerate(DIL_GROUPS):
        sl = slice(gi * MEM_WIDTH, (gi + 1) * MEM_WIDTH)
        qg_, kg_, vg_ = (_to_residues(z[:, sl], dil) for z in (q, ksh, vsh))
        o_r, l_r = dil_attn_fwd(qg_, kg_, vg_, T // dil // DIL_BLOCK, name=f"dil_fwd{gi}")
        grp.append((qg_, kg_, vg_, o_r, l_r))
        os_.append(_from_residues(o_r, dil))
        ls.append(_from_residues(l_r, dil))
    qg1 = _tile_heads(W["mem_q_norm"][1], MEM_WIDTH)
    y_mem1 = mem_attn_fwd(pb, DIL_WIDTH // MEM_WIDTH, memkv[1][0], memkv[1][1], qg1, name="mem_attn1")
    ycat1 = mix_fwd(os_, ls, y_mem1)
    x3 = _mm(ycat1, W["b_w_out"][0], add=x2, name="b_out")
    x4, ffn1 = _ffn_fwd(x3, 1, W)

    dx4, loss = loss_fwd_bwd(x4, target)

    dx3 = _ffn_bwd(dx4, 1, W, ffn1, G)
    dycat1 = _mm(dx3, W["b_w_out"][0], tb=True, name="b_out_dx")
    G["b_w_out"] = _mm(ycat1, dx3, ta=True, name="b_out_dw")[None]
    dq_mem1, dkn1, dvm1, dqg1 = mem_attn_bwd(pb, DIL_WIDTH // MEM_WIDTH, memkv[1][0], memkv[1][1], qg1, dycat1, 1,
                                             name="mem_attn_bwd1")
    G["mem_q_norm"][1] = dqg1[0]
    d_os_ls = mix_bwd(os_, ls, dycat1)
    dqs, dks, dvs = [], [], []
    for gi, (win, dil) in enumerate(DIL_GROUPS):
        qg_, kg_, vg_, o_r, l_r = grp[gi]
        do_r, dl_r = _to_residues(d_os_ls[gi], dil), _to_residues(d_os_ls[3 + gi], dil)
        dq_r, dk_r, dv_r = dil_attn_bwd(qg_, kg_, vg_, o_r, l_r, do_r, dl_r, T // dil // DIL_BLOCK, name=f"dil_bwd{gi}")
        dqs.append(_from_residues(dq_r, dil))
        dks.append(_from_residues(dk_r, dil))
        dvs.append(_from_residues(dv_r, dil))
    dq, dk, dv = (jnp.concatenate(z, axis=1) for z in (dqs, dks, dvs))
    dpb, dkvp, dqn_g, dkn_g = qk_bwd(kvp, pb, kg_t, qg_t, cos, sin, dq, dk, dv, dq_mem1)
    G["b_q_norm"] = dqn_g
    G["kv_k_norm"] = dkn_g[0]
    dh1 = _mm(dpb, W["b_w_in"][0], tb=True, name="b_in_dx")
    G["b_w_in"] = _mm(h1, dpb, ta=True, name="b_in_dw")[None]
    dhkv = _mm(dkvp, W["kv_w"], tb=True, name="kv_in_dx")
    G["kv_w"] = _mm(hkv, dkvp, ta=True, name="kv_in_dw")
    dx2, dg1, dgkv = rms_bwd(x2, [W["attn_norm"][1:2], W["kv_norm"].reshape(1, -1)], [dh1, dhkv], dx3,
                             name="attn_rms_bwd1")
    G["attn_norm"][1] = dg1[0]
    G["kv_norm"] = dgkv[0]

    dx1 = _ffn_bwd(dx2, 0, W, ffn0, G)
    dycat0 = _mm(dx1, W["a_w_out"][0], tb=True, name="a_out_dx")
    G["a_w_out"] = _mm(ycat0, dx1, ta=True, name="a_out_dw")[None]
    dq_mem0, dkn0, dvm0, dqg0 = mem_attn_bwd(p, SHIFT_WIDTH // MEM_WIDTH, memkv[0][0], memkv[0][1], qg0, dycat0,
                                             RWKV_WIDTH // MEM_WIDTH, name="mem_attn_bwd0")
    G["mem_q_norm"][0] = dqg0[0]
    dy, dr_b, dk2_b, dv_b, dg, dlw, dlb, drk = mix_gate_bwd(y, r, k2, v, g, dycat0, lnx_w, lnx_b, r_k)
    for i, (dkn, dvm) in enumerate(((dkn0, dvm0), (dkn1, dvm1))):
        dwkv, dgm, dkg = mem_kv_bwd(mem, W["mem_norm"][i:i + 1], W["mem_w_kv"][i],
                                    _tile_heads(W["mem_k_norm"][i], MEM_WIDTH), dkn, dvm, name=f"mem_kv_bwd{i}")
        G["mem_w_kv"][i], G["mem_norm"][i], G["mem_k_norm"][i] = dwkv, dgm[0], dkg[0]
    late_out = None
    if late is None:
        dr, dw, dk2, dv, dkk, dkka = scan_bwd(r, w, k2, v, kk, kka, states, final_state, dy)
    else:
        pieces = late[2](G)
        dr, dw, dk2, dv, dkk, dkka, *received = scan_bwd(r, w, k2, v, kk, kka, states, final_state, dy, scatter=pieces)
        late_out = (received, pieces)
    dxs, dmu, dw0, da0, dw2p, da2p, dg2p, dk_k, dk_a = rwkv_pre_bwd(
        p, mu, w0, a0, w2p, a2p, g2p, k_k, k_a, (dr, dr_b), dw, (dk2, dk2_b), (dv, dv_b), dkk, dkka, dg)
    dp = shift_bwd(dxs, mu, dq_mem0)
    G.update(a_mu=dmu, a_w0=dw0, a_a0=da0, a_w2=dw2p[None, :64], a_a2=da2p[None, 64:128], a_g2=dg2p[None, 128:],
             a_k_k=dk_k, a_k_a=dk_a, a_r_k=drk.reshape(1, RWKV_HEADS, HEAD_DIM), a_lnx_w=dlw, a_lnx_b=dlb)
    dh0 = _mm(dp, W["a_w_in"][0], tb=True, name="a_in_dx")
    G["a_w_in"] = _mm(h0, dp, ta=True, name="a_in_dw")[None]
    grad_x, dg0 = rms_bwd(x, [W["attn_norm"][0:1]], [dh0], dx1, name="attn_rms_bwd0")
    G["attn_norm"][0] = dg0[0]
    for n in list(G):
        if isinstance(G[n], list):
            G[n] = jnp.stack(G[n], axis=0)
    return loss, grad_x, G, late_out


HBM_SPEC = pl.BlockSpec(memory_space=pltpu.HBM)


def _mesh_pos():
    return lax.axis_index("x"), lax.axis_index("y"), lax.axis_index("c")


def _other_chips(x, y):
    return [(1 - x, y), (x, 1 - y), (1 - x, 1 - y)]


def _remote(send_sems, recv_sems, k, src, dst, to):
    return pltpu.make_async_remote_copy(src_ref=src, dst_ref=dst, send_sem=send_sems.at[k], recv_sem=recv_sems.at[k],
                                        device_id=to, device_id_type=MESH)


def _comm_call(body, name, ins, out_shape, n_remote):
    scratch = [pltpu.SemaphoreType.DMA((n_remote,)), pltpu.SemaphoreType.DMA((n_remote,))]
    return pl.pallas_call(body, name=name, in_specs=[HBM_SPEC] * len(ins), out_specs=[HBM_SPEC] * len(out_shape),
                          out_shape=out_shape, scratch_shapes=scratch)(*ins)


def comm_gather(wbig, wsm):
    def body(wb, ws, ob, os_, send_sems, recv_sems):
        x, y, c = _mesh_pos()
        s = 2 * x + y
        me, sibling = (x, y, c), (x, y, 1 - c)
        chips = _other_chips(x, y)
        rc = functools.partial(_remote, send_sems, recv_sems)
        first = []
        for j, (cx, cy) in enumerate(chips):
            first.append(rc(j, wb.at[c], ob.at[s, c], (cx, cy, c)))
            first.append(rc(6 + j, ws, os_.at[s], (cx, cy, c)))
        for cp in first:
            cp.start()
        passed = []
        for j, (cx, cy) in enumerate(chips):
            blk = ob.at[2 * cx + cy, c]
            rc(j, blk, blk, me).wait_recv()
            passed.append(rc(3 + j, blk, blk, sibling))
            passed[-1].start()
        for j, (cx, cy) in enumerate(chips):
            blk = ob.at[2 * cx + cy, 1 - c]
            rc(3 + j, blk, blk, me).wait_recv()
            sb = os_.at[2 * cx + cy]
            rc(6 + j, sb, sb, me).wait_recv()
        for cp in first + passed:
            cp.wait_send()

    out_shape = [jax.ShapeDtypeStruct((N_CHIPS, *wbig.shape), wbig.dtype),
                 jax.ShapeDtypeStruct((N_CHIPS, *wsm.shape), wsm.dtype)]
    return _comm_call(body, "comm_gather", [wbig, wsm], out_shape, 9)


def comm_pair_exchange(gb, gs):
    def body(gb_ref, gs_ref, rb_ref, rs_ref, send_sems, recv_sems):
        x, y, c = _mesh_pos()
        sibling = (x, y, 1 - c)
        rc = functools.partial(_remote, send_sems, recv_sems)
        cps = [rc(r, gb_ref.at[r, 1 - c], rb_ref.at[r], sibling) for r in range(N_CHIPS)]
        cps.append(rc(N_CHIPS, gs_ref.at[1 - c], rs_ref, sibling))
        for cp in cps:
            cp.start()
        for cp in cps:
            cp.wait()

    out_shape = [jax.ShapeDtypeStruct((N_CHIPS, *gb.shape[2:]), gb.dtype), jax.ShapeDtypeStruct(gs.shape[1:], gs.dtype)]
    return _comm_call(body, "comm_pair_exchange", [gb, gs], out_shape, N_CHIPS + 1)


def comm_chip_exchange(hb, hs):
    def body(hb_ref, hs_ref, qb_ref, qs_ref, send_sems, recv_sems):
        x, y, c = _mesh_pos()
        s = 2 * x + y
        me = (x, y, c)
        chips = _other_chips(x, y)
        rc = functools.partial(_remote, send_sems, recv_sems)
        cps = []
        for j, (cx, cy) in enumerate(chips):
            cps.append(rc(j, hb_ref.at[2 * cx + cy], qb_ref.at[s], (cx, cy, c)))
            cps.append(rc(3 + j, hs_ref, qs_ref.at[s], (cx, cy, c)))
        for cp in cps:
            cp.start()
        for j, (cx, cy) in enumerate(chips):
            blk = qb_ref.at[2 * cx + cy]
            rc(j, blk, blk, me).wait_recv()
            sb = qs_ref.at[2 * cx + cy]
            rc(3 + j, sb, sb, me).wait_recv()
        for cp in cps:
            cp.wait_send()

    out_shape = [jax.ShapeDtypeStruct(hb.shape, hb.dtype), jax.ShapeDtypeStruct((N_CHIPS, *hs.shape), hs.dtype)]
    return _comm_call(body, "comm_chip_exchange", [hb, hs], out_shape, 6)


def comm_pair_share(halves):
    n = len(halves)

    def body(*refs):
        x, y, c = _mesh_pos()
        send_sems, recv_sems = refs[2 * n], refs[2 * n + 1]
        cps = [_remote(send_sems, recv_sems, k, refs[k], refs[n + k], (x, y, 1 - c)) for k in range(n)]
        for cp in cps:
            cp.start()
        for cp in cps:
            cp.wait()

    out_shape = [jax.ShapeDtypeStruct(h.shape, h.dtype) for h in halves]
    return _comm_call(body, "comm_pair_share", list(halves), out_shape, n)


def add_pairs(a, b, out_dtype, *, name, tb):
    T, L = a.shape
    return _rowmap(lambda i, n, p, q: p + q, name=name, T=T, tb=tb, ins=[("row", a), ("row", b)],
                   outs=[("row", L, out_dtype)])[0]


def add_chips(parts, *, name, tb):
    T, L = parts[0].shape

    def fn(i, n, *ps):
        acc = ps[0].astype(F32)
        for p in ps[1:]:
            acc = acc + p.astype(F32)
        return acc

    return _rowmap(fn, name=name, T=T, tb=tb, ins=[("row", p) for p in parts], outs=[("row", L, F32)])[0]


def adamw(g, w, m, v, *, name, tb):
    T, L = g.shape

    def fn(i, n, g, w, m, v):
        m2 = ADAM_B1 * m + (1.0 - ADAM_B1) * g
        v2 = ADAM_B2 * v + (1.0 - ADAM_B2) * (g * g)
        m_hat = m2 / (1.0 - ADAM_B1 ** ADAM_STEP)
        v_hat = v2 / (1.0 - ADAM_B2 ** ADAM_STEP)
        return -ADAM_LR * (m_hat / (jnp.sqrt(v_hat) + ADAM_EPS) + ADAM_WD * w), m2, v2

    return _rowmap(fn, name=name, T=T, tb=tb, ins=[("row", z) for z in (g, w, m, v)], outs=[("row", L, F32)] * 3)


BIG_LANES = 1024
SMALL_LANES = 128


def _flat_cat(arrs, total, dtype):
    parts = [a.reshape(-1).astype(dtype) for a in arrs]
    n = sum(p.shape[0] for p in parts)
    assert n <= total, (n, total)
    if n < total:
        parts.append(jnp.zeros((total - n,), dtype))
    return jnp.concatenate(parts)


def _split_flat(flat, shapes):
    out, off = [], 0
    for shp in shapes:
        n = math.prod(shp)
        out.append(flat[off:off + n].reshape(shp))
        off += n
    return out


def _round_up(n, m):
    return -(-n // m) * m


def _full_shape(shard_shape, axis):
    return tuple(d * N_CHIPS if i == axis else d for i, d in enumerate(shard_shape))


def kernel(x, mem, attn_norm, a_w_in, a_mu, a_w0, a_w2, a_a0, a_a2, a_g2, a_k_k, a_k_a, a_r_k, a_lnx_w, a_lnx_b, a_w_out, kv_norm, kv_w, kv_k_norm, b_w_in, b_q_norm, b_w_out, mem_norm, mem_w_kv, mem_q_norm, mem_k_norm, ffn_norm, ffn_w_up, ffn_conv_w, ffn_conv_b, ffn_w_down, loss_target, m_attn_norm, m_a_w_in, m_a_mu, m_a_w0, m_a_w2, m_a_a0, m_a_a2, m_a_g2, m_a_k_k, m_a_k_a, m_a_r_k, m_a_lnx_w, m_a_lnx_b, m_a_w_out, m_kv_norm, m_kv_w, m_kv_k_norm, m_b_w_in, m_b_q_norm, m_b_w_out, m_mem_norm, m_mem_w_kv, m_mem_q_norm, m_mem_k_norm, m_ffn_norm, m_ffn_w_up, m_ffn_conv_w, m_ffn_conv_b, m_ffn_w_down, v_attn_norm, v_a_w_in, v_a_mu, v_a_w0, v_a_w2, v_a_a0, v_a_a2, v_a_g2, v_a_k_k, v_a_k_a, v_a_r_k, v_a_lnx_w, v_a_lnx_b, v_a_w_out, v_kv_norm, v_kv_w, v_kv_k_norm, v_b_w_in, v_b_q_norm, v_b_w_out, v_mem_norm, v_mem_w_kv, v_mem_q_norm, v_mem_k_norm, v_ffn_norm, v_ffn_w_up, v_ffn_conv_w, v_ffn_conv_b, v_ffn_w_down):
    args = (attn_norm, a_w_in, a_mu, a_w0, a_w2, a_a0, a_a2, a_g2, a_k_k, a_k_a, a_r_k, a_lnx_w, a_lnx_b, a_w_out, kv_norm, kv_w, kv_k_norm, b_w_in, b_q_norm, b_w_out, mem_norm, mem_w_kv, mem_q_norm, mem_k_norm, ffn_norm, ffn_w_up, ffn_conv_w, ffn_conv_b, ffn_w_down)
    ms = (m_attn_norm, m_a_w_in, m_a_mu, m_a_w0, m_a_w2, m_a_a0, m_a_a2, m_a_g2, m_a_k_k, m_a_k_a, m_a_r_k, m_a_lnx_w, m_a_lnx_b, m_a_w_out, m_kv_norm, m_kv_w, m_kv_k_norm, m_b_w_in, m_b_q_norm, m_b_w_out, m_mem_norm, m_mem_w_kv, m_mem_q_norm, m_mem_k_norm, m_ffn_norm, m_ffn_w_up, m_ffn_conv_w, m_ffn_conv_b, m_ffn_w_down)
    vs = (v_attn_norm, v_a_w_in, v_a_mu, v_a_w0, v_a_w2, v_a_a0, v_a_a2, v_a_g2, v_a_k_k, v_a_k_a, v_a_r_k, v_a_lnx_w, v_a_lnx_b, v_a_w_out, v_kv_norm, v_kv_w, v_kv_k_norm, v_b_w_in, v_b_q_norm, v_b_w_out, v_mem_norm, v_mem_w_kv, v_mem_q_norm, v_mem_k_norm, v_ffn_norm, v_ffn_w_up, v_ffn_conv_w, v_ffn_conv_b, v_ffn_w_down)
    w_sh, m_sh, v_sh = (dict(zip(WEIGHTS, z)) for z in (args, ms, vs))
    xi, yi, ci = _mesh_pos()
    chip = 2 * xi + yi
    axes = {**dict(BIG), **dict(SMALL_SHARDED)}
    early_names = [n for n, _ in BIG if n in EARLY_BIG]
    late_names = [n for n, _ in BIG if n not in EARLY_BIG and n != NATURAL_BIG]
    ss_names, ss_axes = [n for n, _ in SMALL_SHARDED], dict(SMALL_SHARDED)
    shapes_of = lambda names: [w_sh[n].shape for n in names]
    count = lambda names: sum(math.prod(s) for s in shapes_of(names))
    n_early, n_late = count(early_names), count(late_names)
    assert n_early % (2 * 16 * BIG_LANES) == 0 and n_late % (2 * 16 * BIG_LANES) == 0
    mh, mh_late = n_early // (2 * BIG_LANES), n_late // (2 * BIG_LANES)
    n_ss = _round_up(count(ss_names), 8 * SMALL_LANES)

    def shard_pack(names, total, dtype, source):
        return _flat_cat([source[n] for n in names], total, dtype)

    def join(pieces, n):
        if pieces[0].ndim == 3:
            return [jnp.concatenate([p[l] for p in pieces], axis=axes[n] - 1) for l in range(pieces[0].shape[0])]
        return jnp.concatenate(pieces, axis=axes[n])

    def unshard(names, gathered):
        per_chip = [_split_flat(gathered[j], shapes_of(names)) for j in range(N_CHIPS)]
        return {n: join([per_chip[j][k] for j in range(N_CHIPS)], n) for k, n in enumerate(names)}

    def whole(g):
        return jnp.stack(g, axis=0) if isinstance(g, list) else g

    def by_chip(names, total, dtype, grads):
        parts = [jnp.split(whole(grads[n]), N_CHIPS, axis=axes[n]) for n in names]
        return jnp.stack([_flat_cat([p[j] for p in parts], total, dtype) for j in range(N_CHIPS)])

    wbig = shard_pack(early_names, n_early, MMD, w_sh).reshape(2, mh, BIG_LANES)
    wsm = shard_pack(ss_names, n_ss, F32, w_sh).reshape(-1, SMALL_LANES)
    wbig_all, wsm_all = comm_gather(wbig, wsm)
    wbig_all = lax.dynamic_update_index_in_dim(wbig_all, wbig, chip, 0).reshape(N_CHIPS, -1)
    wsm_all = lax.dynamic_update_index_in_dim(wsm_all, wsm, chip, 0).reshape(N_CHIPS, -1)
    W = {n: w_sh[n] for n in SMALL_REPL}
    W.update(unshard(early_names, wbig_all))
    W.update(unshard(ss_names, wsm_all))
    for n in ("a_w2", "a_a2", "a_g2"):
        W[n] = [z.astype(MMD) for z in W[n]]
    wlate = shard_pack(late_names, n_late, MMD, w_sh).reshape(2, mh_late, BIG_LANES)
    nat_axis = axes[NATURAL_BIG]
    wnat = w_sh[NATURAL_BIG].astype(MMD)
    assert wnat.shape[0] == 2 and nat_axis != 0

    def unpack_late(gathered):
        full = lax.dynamic_update_index_in_dim(gathered[0], wlate, chip, 0)
        out = unshard(late_names, full.reshape(N_CHIPS, -1))
        nat = lax.dynamic_update_index_in_dim(gathered[1], wnat, chip, 0)
        out[NATURAL_BIG] = join([nat[j] for j in range(N_CHIPS)], NATURAL_BIG)
        return out

    def pack_late(grads):
        return [by_chip(late_names, n_late, BF16, grads).reshape(N_CHIPS, 2, mh_late, BIG_LANES),
                jnp.stack(jnp.split(whole(grads[NATURAL_BIG]).astype(BF16), N_CHIPS, axis=nat_axis))]

    loss_blk, grad_x, G, (received, pieces) = local_step(x[0], mem[0], loss_target[0], W,
                                                          late=([wlate, wnat], unpack_late, pack_late))
    loss = lax.psum(loss_blk[0, 0], ("x", "y", "c"))

    own_piece = lambda p: lax.dynamic_index_in_dim(lax.dynamic_index_in_dim(p, chip, 0, keepdims=False), ci, 0,
                                                   keepdims=False)
    gh_late = add_chips([received[0][k] for k in range(len(PEER_FLIPS))] + [own_piece(pieces[0])],
                        name="add_pieces_late", tb=32)
    gh_nat = add_chips([received[1][k] for k in range(len(PEER_FLIPS))] + [own_piece(pieces[1])],
                       name="add_pieces_natural", tb=32)
    gbig = by_chip(early_names, n_early, F32, G).reshape(N_CHIPS, 2, mh, BIG_LANES)
    sm_full_names = ss_names + list(SMALL_REPL)
    sm_full_shapes = [_full_shape(w_sh[n].shape, ss_axes[n]) for n in ss_names] + [w_sh[n].shape for n in SMALL_REPL]
    n_smf = _round_up(sum(math.prod(s) for s in sm_full_shapes), 2 * 8 * SMALL_LANES)
    msh = n_smf // (2 * SMALL_LANES)
    gsm = _flat_cat([G[n] for n in sm_full_names], n_smf, F32).reshape(2, msh, SMALL_LANES)
    rb, rs = comm_pair_exchange(gbig, gsm)
    mine_b = lax.dynamic_index_in_dim(gbig, ci, axis=1, keepdims=False)
    mine_s = lax.dynamic_index_in_dim(gsm, ci, axis=0, keepdims=False)
    hb = add_pairs(mine_b.reshape(-1, BIG_LANES), rb.reshape(-1, BIG_LANES), BF16, name="add_pairs_big", tb=128)
    hs = add_pairs(mine_s, rs, F32, name="add_pairs_small", tb=msh)
    hb = hb.reshape(N_CHIPS, mh, BIG_LANES)
    qb, qs = comm_chip_exchange(hb, hs)
    qb = lax.dynamic_update_index_in_dim(qb, lax.dynamic_index_in_dim(hb, chip, 0, keepdims=False), chip, 0)
    qs = lax.dynamic_update_index_in_dim(qs, hs, chip, 0)
    gh = add_chips([qb[j] for j in range(N_CHIPS)], name="add_chips_big", tb=32)
    gsh = add_chips([qs[j] for j in range(N_CHIPS)], name="add_chips_small", tb=msh)
    rh, rh_late, rh_nat, rsh = comm_pair_share([gh, gh_late, gh_nat, gsh])
    both = lambda mine_, theirs: jnp.where(ci == 0, jnp.stack([mine_, theirs]), jnp.stack([theirs, mine_]))
    gfull, gfull_late, gfull_nat, gsfull = both(gh, rh), both(gh_late, rh_late), both(gh_nat, rh_nat), both(gsh, rsh)

    res = {tag: {} for tag in ("grad", "delta", "new_m", "new_v")}
    big_grads = (list(zip(early_names, _split_flat(gfull.reshape(-1), shapes_of(early_names))))
                 + list(zip(late_names, _split_flat(gfull_late.reshape(-1), shapes_of(late_names))))
                 + [(NATURAL_BIG, gfull_nat)])
    for n, g in big_grads:
        shp = w_sh[n].shape
        rows = lambda z: z.reshape(-1, shp[-1])
        nrow = math.prod(shp[:-1])
        tb = next(t for t in (512, 256, 128, 64) if nrow % t == 0 and t * shp[-1] <= (1 << 19))
        outs = adamw(rows(g), rows(w_sh[n]), rows(m_sh[n]), rows(v_sh[n]), name=f"adamw_{n}", tb=tb)
        res["grad"][n] = g
        for tag, o in zip(("delta", "new_m", "new_v"), outs):
            res[tag][n] = o.reshape(shp)
    sm_full = dict(zip(sm_full_names, _split_flat(gsfull.reshape(-1), sm_full_shapes)))
    g_loc = {}
    for n in ss_names:
        size = w_sh[n].shape[ss_axes[n]]
        g_loc[n] = lax.dynamic_slice_in_dim(sm_full[n], chip * size, size, axis=ss_axes[n])
    for n in SMALL_REPL:
        g_loc[n] = sm_full[n]
    n_sml = _round_up(sum(math.prod(w_sh[n].shape) for n in sm_full_names), 8 * SMALL_LANES)
    pack_sm = lambda d: _flat_cat([d[n] for n in sm_full_names], n_sml, F32).reshape(-1, SMALL_LANES)
    d_sm, m_sm, v_sm = adamw(pack_sm(g_loc), pack_sm(w_sh), pack_sm(m_sh), pack_sm(v_sh), name="adamw_small",
                             tb=n_sml // SMALL_LANES)
    sm_loc_shapes = [w_sh[n].shape for n in sm_full_names]
    res["grad"].update(g_loc)
    for tag, smv in (("delta", d_sm), ("new_m", m_sm), ("new_v", v_sm)):
        res[tag].update(dict(zip(sm_full_names, _split_flat(smv.reshape(-1), sm_loc_shapes))))
    return (loss, grad_x[None], *[res[tag][n] for tag in ("grad", "delta", "new_m", "new_v") for n in WEIGHTS])
```
